```python
import jax, jax.numpy as jnp
from jax import lax
import numpy as np

D_MODEL = 1024
BATCH = 8
SEQ = 2048
DEPTH = 1

CHUNK = 64
RET_HEADS = 8
RET_DQK = 64
RET_DV = 128
SB_HEADS = 8
SB_DH = 64
SB_BLOCK = 128
D_FF = 4 * D_MODEL
ROPE_BASE = 10000.0
EPS = 1e-6

RET_QK = RET_HEADS * RET_DQK
RET_V = RET_HEADS * RET_DV
SB_W = SB_HEADS * SB_DH
IN_SPLITS = (RET_QK, RET_QK, RET_V, RET_V, SB_W, SB_W, SB_W, D_MODEL, D_MODEL)
D_IN = 2 * RET_QK + 2 * RET_V + 3 * SB_W + 2 * D_MODEL

kernel_name = "hybrid_retention_stickbreaking_block"


def rmsnorm(x, g):
    xf = x.astype(jnp.float32)
    y = xf * lax.rsqrt(jnp.mean(xf * xf, axis=-1, keepdims=True) + EPS)
    return y * g


def modulate(h, shift, scale):
    return h * (1.0 + scale[:, None, :]) + shift[:, None, :]


def rotary(x, pos):
    d = x.shape[-1]
    inv_freq = ROPE_BASE ** (-jnp.arange(0, d, 2, dtype=jnp.float32) / d)
    ang = pos.astype(jnp.float32)[..., None] * inv_freq
    cos = jnp.cos(ang)[:, :, None, :]
    sin = jnp.sin(ang)[:, :, None, :]
    x1, x2 = x[..., : d // 2], x[..., d // 2:]
    return jnp.concatenate([x1 * cos - x2 * sin, x1 * sin + x2 * cos], axis=-1)


def retention(q, k, v, pos):
    B, S = q.shape[0], q.shape[1]
    nc = S // CHUNK
    q = rotary(q.astype(jnp.float32), pos)
    k = rotary(k.astype(jnp.float32), pos) * (RET_DQK ** -0.5)
    v = v.astype(jnp.float32)
    log_gamma = jnp.log1p(-(2.0 ** (-5.0 - jnp.arange(RET_HEADS, dtype=jnp.float32))))
    qc = q.reshape(B, nc, CHUNK, RET_HEADS, RET_DQK)
    kc = k.reshape(B, nc, CHUNK, RET_HEADS, RET_DQK)
    vc = v.reshape(B, nc, CHUNK, RET_HEADS, RET_DV)
    idx = jnp.arange(CHUNK, dtype=jnp.float32)
    intra_decay = jnp.exp(jnp.abs(idx[:, None] - idx[None, :])[None] * log_gamma[:, None, None])
    scores = jnp.einsum('bnihd,bnjhd->bnhij', qc, kc) * intra_decay
    intra = jnp.einsum('bnhij,bnjhe->bnihe', scores, vc)
    k_to_end = jnp.exp((CHUNK - 1.0 - idx)[:, None] * log_gamma[None, :])
    kv = jnp.einsum('bnjhd,jh,bnjhe->nbhde', kc, k_to_end, vc)
    chunk_decay = jnp.exp(CHUNK * log_gamma)[None, :, None, None]

    def step(state, kv_n):
        return state * chunk_decay + kv_n, state

    _, s_prev = lax.scan(step, jnp.zeros(kv.shape[1:], jnp.float32), kv)
    q_from_start = jnp.exp((idx + 1.0)[:, None] * log_gamma[None, :])
    cross = jnp.einsum('bnihd,ih,nbhde->bnihe', qc, q_from_start, s_prev)
    return (intra + cross).reshape(B, S, RET_HEADS, RET_DV)


def stick_breaking(q, k, v):
    S = q.shape[1]
    scale = SB_DH ** -0.5
    q = q.astype(jnp.float32)
    k = k.astype(jnp.float32)
    v = v.astype(jnp.float32)
    outs = []
    for blk in range(S // SB_BLOCK):
        t0 = blk * SB_BLOCK
        t1 = t0 + SB_BLOCK
        z = jnp.einsum('bthd,bshd->bhts', q[:, t0:t1], k[:, :t1]) * scale
        t_idx = t0 + jnp.arange(SB_BLOCK)
        s_idx = jnp.arange(t1)
        valid = s_idx[None, :] < t_idx[:, None]
        log_1m = jnp.where(valid, -jax.nn.softplus(z), 0.0)
        log_stick = lax.cumsum(log_1m, axis=3, reverse=True) - log_1m
        a = jnp.where(valid, jnp.exp(jax.nn.log_sigmoid(z) + log_stick), 0.0)
        outs.append(jnp.einsum('bhts,bshd->bthd', a, v[:, :t1]))
    return jnp.concatenate(outs, axis=1)


def _fwd_setup_inputs(seed: int = 0) -> dict:
    key = jax.random.key(seed)
    ks = jax.random.split(key, 18)
    nrm = jax.random.normal
    f32 = jnp.float32
    x = nrm(ks[0], (BATCH, SEQ, D_MODEL), f32)
    c = nrm(ks[1], (BATCH, D_MODEL), f32)
    offset = jax.random.randint(ks[2], (BATCH, 1), 0, 256, dtype=jnp.int32) * CHUNK
    positions = (offset + jnp.arange(SEQ, dtype=jnp.int32)[None, :]).astype(jnp.int32)
    ada_w = nrm(ks[3], (DEPTH, D_MODEL, 6 * D_MODEL), f32) * D_MODEL ** -0.5
    ada_b = 0.01 * nrm(ks[4], (DEPTH, 6 * D_MODEL), f32)
    pre_mix_g = 1.0 + 0.02 * nrm(ks[5], (DEPTH, D_MODEL), f32)
    post_mix_g = 1.0 + 0.02 * nrm(ks[6], (DEPTH, D_MODEL), f32)
    pre_ffn_g = 1.0 + 0.02 * nrm(ks[7], (DEPTH, D_MODEL), f32)
    post_ffn_g = 1.0 + 0.02 * nrm(ks[8], (DEPTH, D_MODEL), f32)
    w_in = nrm(ks[9], (DEPTH, D_MODEL, D_IN), f32) * D_MODEL ** -0.5
    ret_gn_g = 1.0 + 0.02 * nrm(ks[10], (DEPTH, RET_V), f32)
    w_ret_branch = nrm(ks[11], (DEPTH, RET_V, D_MODEL), f32) * RET_V ** -0.5
    w_sb_branch = nrm(ks[12], (DEPTH, SB_W, D_MODEL), f32) * SB_W ** -0.5
    w_out = nrm(ks[13], (DEPTH, D_MODEL, D_MODEL), f32) * D_MODEL ** -0.5
    w_ff1 = nrm(ks[14], (DEPTH, D_MODEL, D_FF), f32) * D_MODEL ** -0.5
    w_ff2 = nrm(ks[15], (DEPTH, D_FF, D_MODEL), f32) * D_FF ** -0.5
    return {"x": x, "c": c, "positions": positions, "ada_w": ada_w, "ada_b": ada_b,
            "pre_mix_g": pre_mix_g, "post_mix_g": post_mix_g, "pre_ffn_g": pre_ffn_g,
            "post_ffn_g": post_ffn_g, "w_in": w_in, "ret_gn_g": ret_gn_g,
            "w_ret_branch": w_ret_branch, "w_sb_branch": w_sb_branch, "w_out": w_out,
            "w_ff1": w_ff1, "w_ff2": w_ff2}


def _fwd_reference(x, c, positions, ada_w, ada_b, pre_mix_g, post_mix_g, pre_ffn_g, post_ffn_g,
              w_in, ret_gn_g, w_ret_branch, w_sb_branch, w_out, w_ff1, w_ff2):
    out_dtype = x.dtype
    B, S, _ = x.shape
    split_pts = [int(p) for p in np.cumsum(IN_SPLITS)[:-1]]
    h_res = x.astype(jnp.float32)
    for l in range(DEPTH):
        mod = jax.nn.silu(c.astype(jnp.float32)) @ ada_w[l] + ada_b[l]
        sh1, sc1, gt1, sh2, sc2, gt2 = jnp.split(mod, 6, axis=-1)

        h = modulate(rmsnorm(h_res, pre_mix_g[l]), sh1, sc1)
        proj = h @ w_in[l]
        q_r, k_r, v_r, g_r, q_s, k_s, v_s, a_r, a_s = jnp.split(proj, split_pts, axis=-1)

        ret = retention(q_r.reshape(B, S, RET_HEADS, RET_DQK),
                        k_r.reshape(B, S, RET_HEADS, RET_DQK),
                        v_r.reshape(B, S, RET_HEADS, RET_DV), positions)
        mu = jnp.mean(ret, axis=-1, keepdims=True)
        var = jnp.mean(jnp.square(ret - mu), axis=-1, keepdims=True)
        ret = (ret - mu) * lax.rsqrt(var + EPS) * ret_gn_g[l].reshape(RET_HEADS, RET_DV)
        ret = jax.nn.silu(g_r) * ret.reshape(B, S, RET_V)

        sb = stick_breaking(q_s.reshape(B, S, SB_HEADS, SB_DH),
                            k_s.reshape(B, S, SB_HEADS, SB_DH),
                            v_s.reshape(B, S, SB_HEADS, SB_DH)).reshape(B, S, SB_W)

        mixed = (jax.nn.sigmoid(a_r) * (ret @ w_ret_branch[l])
                 + jax.nn.sigmoid(a_s) * (sb @ w_sb_branch[l]))
        y = mixed @ w_out[l]
        h_res = h_res + gt1[:, None, :] * rmsnorm(y, post_mix_g[l])

        h2 = modulate(rmsnorm(h_res, pre_ffn_g[l]), sh2, sc2)
        f = jnp.square(jax.nn.relu(h2 @ w_ff1[l])) @ w_ff2[l]
        h_res = h_res + gt2[:, None, :] * rmsnorm(f, post_ffn_g[l])
    return h_res.astype(out_dtype)


import jax as _jax
import jax.numpy as _jnp

TWIN_FORMAT = 'train_step'
FWD_PARAMS = ['x', 'c', 'positions', 'ada_w', 'ada_b', 'pre_mix_g', 'post_mix_g', 'pre_ffn_g', 'post_ffn_g', 'w_in', 'ret_gn_g', 'w_ret_branch', 'w_sb_branch', 'w_out', 'w_ff1', 'w_ff2']
TWIN_WEIGHTS = ['ada_w', 'ada_b', 'pre_mix_g', 'post_mix_g', 'pre_ffn_g', 'post_ffn_g', 'w_in', 'ret_gn_g', 'w_ret_branch', 'w_sb_branch', 'w_out', 'w_ff1', 'w_ff2']
TWIN_DIFF_INPUT = 'x'
TWIN_INPUTS = ['x', 'c', 'positions', 'ada_w', 'ada_b', 'pre_mix_g', 'post_mix_g', 'pre_ffn_g', 'post_ffn_g', 'w_in', 'ret_gn_g', 'w_ret_branch', 'w_sb_branch', 'w_out', 'w_ff1', 'w_ff2', 'loss_target', 'm_ada_w', 'm_ada_b', 'm_pre_mix_g', 'm_post_mix_g', 'm_pre_ffn_g', 'm_post_ffn_g', 'm_w_in', 'm_ret_gn_g', 'm_w_ret_branch', 'm_w_sb_branch', 'm_w_out', 'm_w_ff1', 'm_w_ff2', 'v_ada_w', 'v_ada_b', 'v_pre_mix_g', 'v_post_mix_g', 'v_pre_ffn_g', 'v_post_ffn_g', 'v_w_in', 'v_ret_gn_g', 'v_w_ret_branch', 'v_w_sb_branch', 'v_w_out', 'v_w_ff1', 'v_w_ff2']
TWIN_OUTPUTS = ['loss', 'grad_x', 'grad_ada_w', 'grad_ada_b', 'grad_pre_mix_g', 'grad_post_mix_g', 'grad_pre_ffn_g', 'grad_post_ffn_g', 'grad_w_in', 'grad_ret_gn_g', 'grad_w_ret_branch', 'grad_w_sb_branch', 'grad_w_out', 'grad_w_ff1', 'grad_w_ff2', 'delta_ada_w', 'delta_ada_b', 'delta_pre_mix_g', 'delta_post_mix_g', 'delta_pre_ffn_g', 'delta_post_ffn_g', 'delta_w_in', 'delta_ret_gn_g', 'delta_w_ret_branch', 'delta_w_sb_branch', 'delta_w_out', 'delta_w_ff1', 'delta_w_ff2', 'new_m_ada_w', 'new_m_ada_b', 'new_m_pre_mix_g', 'new_m_post_mix_g', 'new_m_pre_ffn_g', 'new_m_post_ffn_g', 'new_m_w_in', 'new_m_ret_gn_g', 'new_m_w_ret_branch', 'new_m_w_sb_branch', 'new_m_w_out', 'new_m_w_ff1', 'new_m_w_ff2', 'new_v_ada_w', 'new_v_ada_b', 'new_v_pre_mix_g', 'new_v_post_mix_g', 'new_v_pre_ffn_g', 'new_v_post_ffn_g', 'new_v_w_in', 'new_v_ret_gn_g', 'new_v_w_ret_branch', 'new_v_w_sb_branch', 'new_v_w_out', 'new_v_w_ff1', 'new_v_w_ff2']
TWIN_LEAF_KINDS = {'loss': 'loss', 'grad_x': 'grad_x', 'grad_ada_w': 'grad_w', 'grad_ada_b': 'grad_w', 'grad_pre_mix_g': 'grad_w', 'grad_post_mix_g': 'grad_w', 'grad_pre_ffn_g': 'grad_w', 'grad_post_ffn_g': 'grad_w', 'grad_w_in': 'grad_w', 'grad_ret_gn_g': 'grad_w', 'grad_w_ret_branch': 'grad_w', 'grad_w_sb_branch': 'grad_w', 'grad_w_out': 'grad_w', 'grad_w_ff1': 'grad_w', 'grad_w_ff2': 'grad_w', 'delta_ada_w': 'delta_w', 'delta_ada_b': 'delta_w', 'delta_pre_mix_g': 'delta_w', 'delta_post_mix_g': 'delta_w', 'delta_pre_ffn_g': 'delta_w', 'delta_post_ffn_g': 'delta_w', 'delta_w_in': 'delta_w', 'delta_ret_gn_g': 'delta_w', 'delta_w_ret_branch': 'delta_w', 'delta_w_sb_branch': 'delta_w', 'delta_w_out': 'delta_w', 'delta_w_ff1': 'delta_w', 'delta_w_ff2': 'delta_w', 'new_m_ada_w': 'new_m', 'new_m_ada_b': 'new_m', 'new_m_pre_mix_g': 'new_m', 'new_m_post_mix_g': 'new_m', 'new_m_pre_ffn_g': 'new_m', 'new_m_post_ffn_g': 'new_m', 'new_m_w_in': 'new_m', 'new_m_ret_gn_g': 'new_m', 'new_m_w_ret_branch': 'new_m', 'new_m_w_sb_branch': 'new_m', 'new_m_w_out': 'new_m', 'new_m_w_ff1': 'new_m', 'new_m_w_ff2': 'new_m', 'new_v_ada_w': 'new_v', 'new_v_ada_b': 'new_v', 'new_v_pre_mix_g': 'new_v', 'new_v_post_mix_g': 'new_v', 'new_v_pre_ffn_g': 'new_v', 'new_v_post_ffn_g': 'new_v', 'new_v_w_in': 'new_v', 'new_v_ret_gn_g': 'new_v', 'new_v_w_ret_branch': 'new_v', 'new_v_w_sb_branch': 'new_v', 'new_v_w_out': 'new_v', 'new_v_w_ff1': 'new_v', 'new_v_w_ff2': 'new_v'}


def _forward(args):
    return _fwd_reference(*[args[k] for k in FWD_PARAMS])


def _output_shape():
    out = _jax.eval_shape(lambda: _forward(_fwd_setup_inputs(0)))
    return out.shape, out.dtype

N_MICROBATCH = 1
ADAM_LR = 0.001
ADAM_B1 = 0.9
ADAM_B2 = 0.999
ADAM_EPS = 1e-08
ADAM_WD = 0.01
ADAM_STEP = 10
PER_EXAMPLE_BATCH_AXIS = {'x': 0, 'c': 0, 'positions': 0, 'loss_target': 0}
SHARED_INPUTS = []
_WEIGHT_DTYPES = {'ada_w': _jnp.float32, 'ada_b': _jnp.float32, 'pre_mix_g': _jnp.float32, 'post_mix_g': _jnp.float32, 'pre_ffn_g': _jnp.float32, 'post_ffn_g': _jnp.float32, 'w_in': _jnp.float32, 'ret_gn_g': _jnp.float32, 'w_ret_branch': _jnp.float32, 'w_sb_branch': _jnp.float32, 'w_out': _jnp.float32, 'w_ff1': _jnp.float32, 'w_ff2': _jnp.float32}
MOMENT_SCALE = {'ada_w': 2.103186e+00, 'ada_b': 3.826716e+00, 'pre_mix_g': 2.851627e-01, 'post_mix_g': 7.923355e+00, 'pre_ffn_g': 4.429431e-01, 'post_ffn_g': 8.190813e+00, 'w_in': 4.843591e-01, 'ret_gn_g': 3.159073e-01, 'w_ret_branch': 2.763161e-01, 'w_sb_branch': 1.122424e+00, 'w_out': 1.167774e+00, 'w_ff1': 5.452032e-01, 'w_ff2': 1.565118e+00}


def _to_microbatches(a, axis):
    t = _jnp.moveaxis(a, axis, 0)
    t = t.reshape((N_MICROBATCH, t.shape[0] // N_MICROBATCH) + t.shape[1:])
    return _jnp.moveaxis(t, 1, axis + 1)


def setup_inputs(seed: int = 0) -> dict:
    inp = _fwd_setup_inputs(seed)
    key = _jax.random.fold_in(_jax.random.key(seed), 7919)
    shape, _ = _output_shape()
    out = dict(inp)
    out["loss_target"] = _jax.random.normal(_jax.random.fold_in(key, 0), shape, _jnp.float32)
    for i, name in enumerate(TWIN_WEIGHTS):
        w = inp[name].astype(_jnp.float32)
        if MOMENT_SCALE is None:
            s = _jnp.sqrt(_jnp.mean(_jnp.square(w)) + 1e-30)
        else:
            s = MOMENT_SCALE[name]
        km, kv = _jax.random.split(_jax.random.fold_in(key, i + 1))
        out[name] = w
        out["m_" + name] = s * _jax.random.normal(km, w.shape, _jnp.float32)
        out["v_" + name] = (s * s) * _jax.random.uniform(kv, w.shape, _jnp.float32, 0.5, 1.5)
    if N_MICROBATCH > 1:
        for name, axis in PER_EXAMPLE_BATCH_AXIS.items():
            out[name] = _to_microbatches(out[name], axis)
    return {'x': out['x'], 'c': out['c'], 'positions': out['positions'], 'ada_w': out['ada_w'], 'ada_b': out['ada_b'], 'pre_mix_g': out['pre_mix_g'], 'post_mix_g': out['post_mix_g'], 'pre_ffn_g': out['pre_ffn_g'], 'post_ffn_g': out['post_ffn_g'], 'w_in': out['w_in'], 'ret_gn_g': out['ret_gn_g'], 'w_ret_branch': out['w_ret_branch'], 'w_sb_branch': out['w_sb_branch'], 'w_out': out['w_out'], 'w_ff1': out['w_ff1'], 'w_ff2': out['w_ff2'], 'loss_target': out['loss_target'], 'm_ada_w': out['m_ada_w'], 'm_ada_b': out['m_ada_b'], 'm_pre_mix_g': out['m_pre_mix_g'], 'm_post_mix_g': out['m_post_mix_g'], 'm_pre_ffn_g': out['m_pre_ffn_g'], 'm_post_ffn_g': out['m_post_ffn_g'], 'm_w_in': out['m_w_in'], 'm_ret_gn_g': out['m_ret_gn_g'], 'm_w_ret_branch': out['m_w_ret_branch'], 'm_w_sb_branch': out['m_w_sb_branch'], 'm_w_out': out['m_w_out'], 'm_w_ff1': out['m_w_ff1'], 'm_w_ff2': out['m_w_ff2'], 'v_ada_w': out['v_ada_w'], 'v_ada_b': out['v_ada_b'], 'v_pre_mix_g': out['v_pre_mix_g'], 'v_post_mix_g': out['v_post_mix_g'], 'v_pre_ffn_g': out['v_pre_ffn_g'], 'v_post_ffn_g': out['v_post_ffn_g'], 'v_w_in': out['v_w_in'], 'v_ret_gn_g': out['v_ret_gn_g'], 'v_w_ret_branch': out['v_w_ret_branch'], 'v_w_sb_branch': out['v_w_sb_branch'], 'v_w_out': out['v_w_out'], 'v_w_ff1': out['v_w_ff1'], 'v_w_ff2': out['v_w_ff2']}


def _loss(weights, diff, rest, loss_target):
    with _jax.named_scope("forward"):
        args = {**rest, TWIN_DIFF_INPUT: diff, **{k: w.astype(_WEIGHT_DTYPES[k]) for k, w in weights.items()}}
        y = _forward(args)
    with _jax.named_scope("loss_head"):
        err = _jnp.square(y.astype(_jnp.float32) - loss_target)
        return 0.5 * _jnp.sum(_jnp.mean(err, axis=-1)) if err.ndim else 0.5 * err


def _adamw(w, g, m, v):
    m = ADAM_B1 * m + (1.0 - ADAM_B1) * g
    v = ADAM_B2 * v + (1.0 - ADAM_B2) * _jnp.square(g)
    m_hat = m / (1.0 - ADAM_B1 ** ADAM_STEP)
    v_hat = v / (1.0 - ADAM_B2 ** ADAM_STEP)
    delta = -ADAM_LR * (m_hat / (_jnp.sqrt(v_hat) + ADAM_EPS) + ADAM_WD * w)
    return delta, m, v


def reference(x, c, positions, ada_w, ada_b, pre_mix_g, post_mix_g, pre_ffn_g, post_ffn_g, w_in, ret_gn_g, w_ret_branch, w_sb_branch, w_out, w_ff1, w_ff2, loss_target, m_ada_w, m_ada_b, m_pre_mix_g, m_post_mix_g, m_pre_ffn_g, m_post_ffn_g, m_w_in, m_ret_gn_g, m_w_ret_branch, m_w_sb_branch, m_w_out, m_w_ff1, m_w_ff2, v_ada_w, v_ada_b, v_pre_mix_g, v_post_mix_g, v_pre_ffn_g, v_post_ffn_g, v_w_in, v_ret_gn_g, v_w_ret_branch, v_w_sb_branch, v_w_out, v_w_ff1, v_w_ff2):
    given = dict(x=x, c=c, positions=positions, ada_w=ada_w, ada_b=ada_b, pre_mix_g=pre_mix_g, post_mix_g=post_mix_g, pre_ffn_g=pre_ffn_g, post_ffn_g=post_ffn_g, w_in=w_in, ret_gn_g=ret_gn_g, w_ret_branch=w_ret_branch, w_sb_branch=w_sb_branch, w_out=w_out, w_ff1=w_ff1, w_ff2=w_ff2, loss_target=loss_target, m_ada_w=m_ada_w, m_ada_b=m_ada_b, m_pre_mix_g=m_pre_mix_g, m_post_mix_g=m_post_mix_g, m_pre_ffn_g=m_pre_ffn_g, m_post_ffn_g=m_post_ffn_g, m_w_in=m_w_in, m_ret_gn_g=m_ret_gn_g, m_w_ret_branch=m_w_ret_branch, m_w_sb_branch=m_w_sb_branch, m_w_out=m_w_out, m_w_ff1=m_w_ff1, m_w_ff2=m_w_ff2, v_ada_w=v_ada_w, v_ada_b=v_ada_b, v_pre_mix_g=v_pre_mix_g, v_post_mix_g=v_post_mix_g, v_pre_ffn_g=v_pre_ffn_g, v_post_ffn_g=v_post_ffn_g, v_w_in=v_w_in, v_ret_gn_g=v_ret_gn_g, v_w_ret_branch=v_w_ret_branch, v_w_sb_branch=v_w_sb_branch, v_w_out=v_w_out, v_w_ff1=v_w_ff1, v_w_ff2=v_w_ff2)
    weights = {n: given[n] for n in TWIN_WEIGHTS}
    shared = {n: given[n] for n in SHARED_INPUTS}
    per_example = {n: given[n] for n in ['x', 'c', 'positions']}
    grad_fn = _jax.value_and_grad(_loss, argnums=(0, 1))

    def one_microbatch(ex, loss_target):
        ex = dict(ex)
        diff = ex.pop(TWIN_DIFF_INPUT)
        return grad_fn(weights, diff, {**shared, **ex}, loss_target)

    if N_MICROBATCH == 1:
        loss, (grad_w, grad_x) = one_microbatch(per_example, given["loss_target"])
    else:
        def body(carry, xs):
            loss_sum, grad_sum = carry
            l_k, (gw_k, gx_k) = one_microbatch(xs[0], xs[1])
            with _jax.named_scope("update"):
                return (loss_sum + l_k, _jax.tree.map(_jnp.add, grad_sum, gw_k)), gx_k

        init = (_jnp.zeros((), _jnp.float32), _jax.tree.map(_jnp.zeros_like, weights))
        (loss, grad_w), grad_x = _jax.lax.scan(body, init, (per_example, given["loss_target"]))
    with _jax.named_scope("update"):
        delta_w, new_m, new_v = {}, {}, {}
        for n in TWIN_WEIGHTS:
            delta_w[n], new_m[n], new_v[n] = _adamw(weights[n], grad_w[n], given["m_" + n], given["v_" + n])
    return (loss, grad_x, *[grad_w[n] for n in TWIN_WEIGHTS], *[delta_w[n] for n in TWIN_WEIGHTS],
            *[new_m[n] for n in TWIN_WEIGHTS], *[new_v[n] for n in TWIN_WEIGHTS])
```

```python
import functools

import numpy as np
import jax
import jax.numpy as jnp
from jax import lax
from jax.experimental import pallas as pl
from jax.experimental.pallas import tpu as pltpu

SEQ = 2048
D_MODEL = 1024
D_IN = 6656
D_FF = 4096
N_CHIPS = 4
EPS = 1e-6
ROPE_BASE = 10000.0
RET_BLOCK = 256
RET_CHUNK_SHIFT = 6
SB_BLOCK = 128
QK_SCALE = 0.125
N_PAIRS = 4

ADAM_LR = 0.001
ADAM_B1 = 0.9
ADAM_B2 = 0.999
ADAM_EPS = 1e-08
ADAM_WD = 0.01
ADAM_STEP = 10

BF = jnp.bfloat16
F32 = jnp.float32
MESH = pl.DeviceIdType.MESH
VMEM_LIMIT = 56 * 1024 * 1024
ANY = pl.BlockSpec(memory_space=pl.ANY)

C_QR, C_KR, C_VR, C_GR, C_QS, C_KS, C_VS, C_AR, C_AS = 0, 512, 1024, 2048, 3072, 3584, 4096, 4608, 5632

V_SH1, V_SC1, V_GT1, V_SH2, V_SC2, V_GT2, V_G1, V_G2, V_G3, V_G4 = range(10)
P_DSH1, P_DSC1, P_DGT1, P_DSH2, P_DSC2, P_DGT2, P_DG1, P_DG2, P_DG3, P_DG4, P_DGN, P_LOSS = range(12)
N_PAY = 12


def _cp(sem=None, **kw):
    if sem is not None:
        kw["dimension_semantics"] = sem
    return pltpu.CompilerParams(vmem_limit_bytes=VMEM_LIMIT, **kw)


def _dot(a, b):
    return jnp.dot(a, b, preferred_element_type=F32)


def _dot_nt(a, b):
    return lax.dot_general(a, b, (((1,), (1,)), ((), ())), preferred_element_type=F32)


def _dot_tn(a, b):
    return lax.dot_general(a, b, (((0,), (0,)), ((), ())), preferred_element_type=F32)


def _row(ref, i):
    return ref[i:i + 1, :]


def _rms(v):
    return lax.rsqrt(jnp.mean(v * v, axis=1, keepdims=True) + EPS)


def _colsum(v):
    return jnp.sum(v, axis=0, keepdims=True)


def _rowmean(v):
    return jnp.mean(v, axis=1, keepdims=True)


def _sigmoid(v):
    return 1.0 / (1.0 + jnp.exp(-v))


def _cast_bf16(w, name):
    rows, cols = w.shape
    tr = min(rows, 256)

    def body(w_ref, o_ref):
        o_ref[...] = w_ref[...].astype(BF)

    return pl.pallas_call(
        body, name=name, grid=(rows // tr,),
        in_specs=[pl.BlockSpec((tr, cols), lambda i: (i, 0))],
        out_specs=pl.BlockSpec((tr, cols), lambda i: (i, 0)),
        out_shape=jax.ShapeDtypeStruct((rows, cols), BF),
        compiler_params=_cp(("parallel",)),
    )(w)


def _adamw_math(w, g, m, v):
    m = ADAM_B1 * m + (1.0 - ADAM_B1) * g
    v = ADAM_B2 * v + (1.0 - ADAM_B2) * (g * g)
    m_hat = m / (1.0 - ADAM_B1 ** ADAM_STEP)
    v_hat = v / (1.0 - ADAM_B2 ** ADAM_STEP)
    delta = -ADAM_LR * (m_hat / (jnp.sqrt(v_hat) + ADAM_EPS) + ADAM_WD * w)
    return delta, m, v


def _adamw(w, m, v, g, name):
    rows, cols = w.shape
    tr = min(rows, 128)

    def body(w_ref, m_ref, v_ref, g_ref, go_ref, d_ref, mo_ref, vo_ref):
        gg = g_ref[...]
        d, mm, vv = _adamw_math(w_ref[...], gg, m_ref[...], v_ref[...])
        go_ref[...] = gg
        d_ref[...] = d
        mo_ref[...] = mm
        vo_ref[...] = vv

    spec = pl.BlockSpec((tr, cols), lambda i: (i, 0))
    shp = jax.ShapeDtypeStruct((rows, cols), F32)
    return pl.pallas_call(
        body, name=name, grid=(rows // tr,),
        in_specs=[spec] * 4, out_specs=[spec] * 4, out_shape=[shp] * 4,
        compiler_params=_cp(("parallel",)),
    )(w, m, v, g)


def _place():
    x, y, c = lax.axis_index("x"), lax.axis_index("y"), lax.axis_index("c")
    return x, y, c


def _gather_weights(shards):
    n = len(shards)

    def body(*refs):
        ins, outs = refs[:n], refs[n:2 * n]
        send_sems, recv_sems, loc_sems = refs[2 * n:]
        x, y, c = _place()
        k = 2 * x + y
        sib = (x, y, 1 - c)
        chips = [(1 - x, y), (x, 1 - y), (1 - x, 1 - y)]

        def half(w, cc):
            rh = shards[w].shape[0] // 2
            return pl.ds(cc * rh, rh)

        def copy(w, slot, src, dst, to):
            return pltpu.make_async_remote_copy(
                src_ref=src, dst_ref=dst, send_sem=send_sems.at[w * 6 + slot],
                recv_sem=recv_sems.at[w * 6 + slot], device_id=to, device_id_type=MESH)

        local = [pltpu.make_async_copy(ins[w], outs[w].at[k], loc_sems.at[w]) for w in range(n)]
        for cp in local:
            cp.start()
        first = []
        for w in range(n):
            for j, (cx, cy) in enumerate(chips):
                cp = copy(w, j, ins[w].at[half(w, c)], outs[w].at[k, half(w, c)], (cx, cy, c))
                cp.start()
                first.append(cp)
        passed = []
        for w in range(n):
            for j, (cx, cy) in enumerate(chips):
                kj = 2 * cx + cy
                land = outs[w].at[kj, half(w, c)]
                copy(w, j, land, land, (cx, cy, c)).wait_recv()
                cp = copy(w, 3 + j, land, land, sib)
                cp.start()
                passed.append(cp)
        for w in range(n):
            for j, (cx, cy) in enumerate(chips):
                kj = 2 * cx + cy
                land = outs[w].at[kj, half(w, 1 - c)]
                copy(w, 3 + j, land, land, sib).wait_recv()
        for cp in first + passed:
            cp.wait_send()
        for cp in local:
            cp.wait()

    return pl.pallas_call(
        body, name="gather_weights",
        in_specs=[ANY] * n, out_specs=[ANY] * n,
        out_shape=[jax.ShapeDtypeStruct((N_CHIPS,) + s.shape, BF) for s in shards],
        scratch_shapes=[pltpu.SemaphoreType.DMA((6 * n,)), pltpu.SemaphoreType.DMA((6 * n,)),
                        pltpu.SemaphoreType.DMA((n,))],
    )(*shards)


def _pair_send_halves(grads):
    n = len(grads)

    def body(*refs):
        ins, outs = refs[:n], refs[n:2 * n]
        send_sems, recv_sems = refs[2 * n:]
        x, y, c = _place()
        sib = (x, y, 1 - c)
        cps = []
        for w in range(n):
            cp = pltpu.make_async_remote_copy(
                src_ref=ins[w].at[:, 1 - c], dst_ref=outs[w], send_sem=send_sems.at[w],
                recv_sem=recv_sems.at[w], device_id=sib, device_id_type=MESH)
            cp.start()
            cps.append(cp)
        for cp in cps:
            cp.wait()

    return pl.pallas_call(
        body, name="pair_send_halves",
        in_specs=[ANY] * n, out_specs=[ANY] * n,
        out_shape=[jax.ShapeDtypeStruct((N_CHIPS,) + g.shape[2:], F32) for g in grads],
        scratch_shapes=[pltpu.SemaphoreType.DMA((n,)), pltpu.SemaphoreType.DMA((n,))],
    )(*grads)


def _pair_add(g, recv, cidx, name):
    _, _, rh, cols = g.shape
    tr = min(rh, 256)

    def body(c_ref, g_ref, r_ref, o_ref):
        o_ref[...] = (g_ref[...] + r_ref[...]).astype(BF)

    return pl.pallas_call(
        body, name=name,
        grid_spec=pltpu.PrefetchScalarGridSpec(
            num_scalar_prefetch=1, grid=(N_CHIPS, rh // tr),
            in_specs=[pl.BlockSpec((None, None, tr, cols), lambda s, i, c_ref: (s, c_ref[0], i, 0)),
                      pl.BlockSpec((None, tr, cols), lambda s, i, c_ref: (s, i, 0))],
            out_specs=pl.BlockSpec((None, tr, cols), lambda s, i, c_ref: (s, i, 0))),
        out_shape=jax.ShapeDtypeStruct((N_CHIPS, rh, cols), BF),
        compiler_params=_cp(("parallel", "parallel")),
    )(cidx, g, recv)


def _chip_exchange(sums):
    n = len(sums)

    def body(*refs):
        ins, outs = refs[:n], refs[n:2 * n]
        send_sems, recv_sems, loc_sems = refs[2 * n:]
        x, y, c = _place()
        k = 2 * x + y
        chips = [(1 - x, y), (x, 1 - y), (1 - x, 1 - y)]
        local = [pltpu.make_async_copy(ins[w].at[k], outs[w].at[k], loc_sems.at[w]) for w in range(n)]
        for cp in local:
            cp.start()
        sent = []
        for w in range(n):
            for j, (cx, cy) in enumerate(chips):
                cp = pltpu.make_async_remote_copy(
                    src_ref=ins[w].at[2 * cx + cy], dst_ref=outs[w].at[k],
                    send_sem=send_sems.at[w * 3 + j], recv_sem=recv_sems.at[w * 3 + j],
                    device_id=(cx, cy, c), device_id_type=MESH)
                cp.start()
                sent.append(cp)
        for w in range(n):
            for j, (cx, cy) in enumerate(chips):
                land = outs[w].at[2 * cx + cy]
                pltpu.make_async_remote_copy(
                    src_ref=land, dst_ref=land, send_sem=send_sems.at[w * 3 + j],
                    recv_sem=recv_sems.at[w * 3 + j], device_id=(cx, cy, c),
                    device_id_type=MESH).wait_recv()
        for cp in sent:
            cp.wait_send()
        for cp in local:
            cp.wait()

    return pl.pallas_call(
        body, name="chip_exchange",
        in_specs=[ANY] * n, out_specs=[ANY] * n,
        out_shape=[jax.ShapeDtypeStruct(s.shape, BF) for s in sums],
        scratch_shapes=[pltpu.SemaphoreType.DMA((3 * n,)), pltpu.SemaphoreType.DMA((3 * n,)),
                        pltpu.SemaphoreType.DMA((n,))],
    )(*sums)


def _chip_add(parts, name):
    _, rh, cols = parts.shape
    tr = min(rh, 256)

    def body(p_ref, o_ref):
        acc = p_ref[0].astype(F32)
        for s in range(1, N_CHIPS):
            acc = acc + p_ref[s].astype(F32)
        o_ref[...] = acc

    return pl.pallas_call(
        body, name=name, grid=(rh // tr,),
        in_specs=[pl.BlockSpec((N_CHIPS, tr, cols), lambda i: (0, i, 0))],
        out_specs=pl.BlockSpec((tr, cols), lambda i: (i, 0)),
        out_shape=jax.ShapeDtypeStruct((rh, cols), F32),
        compiler_params=_cp(("parallel",)),
    )(parts)


def _pair_exchange(halves):
    n = len(halves)

    def body(*refs):
        ins, outs = refs[:n], refs[n:2 * n]
        send_sems, recv_sems, loc_sems = refs[2 * n:]
        x, y, c = _place()
        sib = (x, y, 1 - c)
        local = [pltpu.make_async_copy(ins[w], outs[w].at[c], loc_sems.at[w]) for w in range(n)]
        for cp in local:
            cp.start()
        cps = []
        for w in range(n):
            cp = pltpu.make_async_remote_copy(
                src_ref=ins[w], dst_ref=outs[w].at[c], send_sem=send_sems.at[w],
                recv_sem=recv_sems.at[w], device_id=sib, device_id_type=MESH)
            cp.start()
            cps.append(cp)
        for w in range(n):
            land = outs[w].at[1 - c]
            pltpu.make_async_remote_copy(
                src_ref=land, dst_ref=land, send_sem=send_sems.at[w], recv_sem=recv_sems.at[w],
                device_id=sib, device_id_type=MESH).wait_recv()
        for cp in cps:
            cp.wait_send()
        for cp in local:
            cp.wait()

    return pl.pallas_call(
        body, name="pair_exchange",
        in_specs=[ANY] * n, out_specs=[ANY] * n,
        out_shape=[jax.ShapeDtypeStruct((2,) + h.shape, F32) for h in halves],
        scratch_shapes=[pltpu.SemaphoreType.DMA((n,)), pltpu.SemaphoreType.DMA((n,)),
                        pltpu.SemaphoreType.DMA((n,))],
    )(*halves)


def _peers(x, y, c):
    out = []
    for code in range(1, 8):
        fx, fy, fc = (code >> 2) & 1, (code >> 1) & 1, code & 1
        px = 1 - x if fx else x
        py = 1 - y if fy else y
        pc = 1 - c if fc else c
        out.append((code, (px, py, pc)))
    return out


def _mod_exchange(c_row, ada_w, ada_b4):
    ncol = ada_w.shape[1]

    def body(c_ref, w_ref, b_ref, call_ref, mod_ref, part_ref, send_sems, recv_sems):
        x, y, c = _place()
        k = 2 * x + y
        me = 4 * x + 2 * y + c
        call_ref[pl.ds(me, 1), :] = c_ref[...]
        sends = []
        for code, peer in _peers(x, y, c):
            cp = pltpu.make_async_remote_copy(
                src_ref=c_ref, dst_ref=call_ref.at[pl.ds(me, 1), :],
                send_sem=send_sems.at[code], recv_sem=recv_sems.at[code],
                device_id=peer, device_id_type=MESH)
            cp.start()
            sends.append(cp)
        for code, (px, py, pc) in _peers(x, y, c):
            land = call_ref.at[pl.ds(4 * px + 2 * py + pc, 1), :]
            pltpu.make_async_remote_copy(
                src_ref=land, dst_ref=land, send_sem=send_sems.at[code], recv_sem=recv_sems.at[code],
                device_id=(px, py, pc), device_id_type=MESH).wait_recv()
        call = call_ref[...]
        act = call * _sigmoid(call)
        part = jnp.dot(act, w_ref[...], preferred_element_type=F32,
                       precision=lax.Precision.HIGHEST) + b_ref[pl.ds(k, 1), :]
        part_ref[...] = part
        mod_ref[pl.ds(k, 1), :] = part_ref[pl.ds(me, 1), :]
        chips = [(8 + j, peer) for j, (code, peer) in enumerate(_peers(x, y, c)) if code in (2, 4, 6)]
        for slot, (px, py, pc) in chips:
            cp = pltpu.make_async_remote_copy(
                src_ref=part_ref.at[pl.ds(4 * px + 2 * py + pc, 1), :], dst_ref=mod_ref.at[pl.ds(k, 1), :],
                send_sem=send_sems.at[slot], recv_sem=recv_sems.at[slot],
                device_id=(px, py, pc), device_id_type=MESH)
            cp.start()
            sends.append(cp)
        for slot, (px, py, pc) in chips:
            land = mod_ref.at[pl.ds(2 * px + py, 1), :]
            pltpu.make_async_remote_copy(
                src_ref=land, dst_ref=land, send_sem=send_sems.at[slot], recv_sem=recv_sems.at[slot],
                device_id=(px, py, pc), device_id_type=MESH).wait_recv()
        for cp in sends:
            cp.wait_send()

    vm = pl.BlockSpec(memory_space=pltpu.VMEM)
    return pl.pallas_call(
        body, name="mod_exchange",
        in_specs=[vm, vm, vm], out_specs=[vm, vm],
        out_shape=[jax.ShapeDtypeStruct((8, D_MODEL), F32), jax.ShapeDtypeStruct((N_CHIPS, ncol), F32)],
        scratch_shapes=[pltpu.VMEM((8, ncol), F32), pltpu.SemaphoreType.DMA((16,)),
                        pltpu.SemaphoreType.DMA((16,))],
        compiler_params=_cp(),
    )(c_row, ada_w, ada_b4)


def _small_exchange(payload, c_all, wsm, msm, vsm):
    ncol = 6 * D_MODEL // N_CHIPS

    def body(p_ref, call_ref, w_ref, m_ref, v_ref, gw_ref, g_ref, d_ref, mo_ref, vo_ref, loss_ref,
             all_ref, dm_ref, send_sems, recv_sems):
        x, y, c = _place()
        k = 2 * x + y
        me = 4 * x + 2 * y + c
        all_ref[:, pl.ds(me, 1), :] = p_ref[...]
        sends = []
        for code, peer in _peers(x, y, c):
            cp = pltpu.make_async_remote_copy(
                src_ref=p_ref, dst_ref=all_ref.at[:, pl.ds(me, 1), :],
                send_sem=send_sems.at[code], recv_sem=recv_sems.at[code],
                device_id=peer, device_id_type=MESH)
            cp.start()
            sends.append(cp)
        for code, (px, py, pc) in _peers(x, y, c):
            land = all_ref.at[:, pl.ds(4 * px + 2 * py + pc, 1), :]
            pltpu.make_async_remote_copy(
                src_ref=land, dst_ref=land, send_sem=send_sems.at[code], recv_sem=recv_sems.at[code],
                device_id=(px, py, pc), device_id_type=MESH).wait_recv()
        for cp in sends:
            cp.wait_send()
        tot = [_colsum(all_ref[r]) for r in range(N_PAY)]
        loss_ref[...] = jnp.sum(tot[P_LOSS], axis=1, keepdims=True)
        g_ref[...] = jnp.zeros_like(g_ref)
        for r in range(P_LOSS):
            g_ref[r:r + 1, :] = tot[r]
        g = g_ref[...]
        d, mm, vv = _adamw_math(w_ref[...], g, m_ref[...], v_ref[...])
        d_ref[...] = d
        mo_ref[...] = mm
        vo_ref[...] = vv
        half = D_MODEL // 2
        for kk in range(N_CHIPS):
            @pl.when(k == kk)
            def _():
                r0 = 3 * (kk // 2)
                if kk % 2 == 0:
                    dm_ref[:, :D_MODEL] = all_ref[r0]
                    dm_ref[:, D_MODEL:] = all_ref[r0 + 1][:, :half]
                else:
                    dm_ref[:, :half] = all_ref[r0 + 1][:, half:]
                    dm_ref[:, half:] = all_ref[r0 + 2]
        call = call_ref[...]
        act = call * _sigmoid(call)
        gw_ref[...] = lax.dot_general(act, dm_ref[...], (((0,), (0,)), ((), ())),
                                      preferred_element_type=F32, precision=lax.Precision.HIGHEST)

    vm = pl.BlockSpec(memory_space=pltpu.VMEM)
    small = jax.ShapeDtypeStruct((16, D_MODEL), F32)
    return pl.pallas_call(
        body, name="small_exchange",
        in_specs=[vm] * 5, out_specs=[vm] * 6,
        out_shape=[jax.ShapeDtypeStruct((D_MODEL, ncol), F32), small, small, small, small,
                   jax.ShapeDtypeStruct((1, 1), F32)],
        scratch_shapes=[pltpu.VMEM((N_PAY, 8, D_MODEL), F32), pltpu.VMEM((8, ncol), F32),
                        pltpu.SemaphoreType.DMA((8,)), pltpu.SemaphoreType.DMA((8,))],
        compiler_params=_cp(),
    )(payload, c_all, wsm, msm, vsm)


def _rope_tables(pos_col, inv_freq, sign):
    def body(p_ref, f_ref, s_ref, cos_ref, sin_ref):
        ang = p_ref[...].astype(F32) * f_ref[...]
        cos_ref[...] = jnp.cos(ang)
        sin_ref[...] = jnp.sin(ang) * s_ref[...]

    tr = 512
    shp = jax.ShapeDtypeStruct((SEQ, 128), F32)
    return pl.pallas_call(
        body, name="rope_tables", grid=(SEQ // tr,),
        in_specs=[pl.BlockSpec((tr, 1), lambda i: (i, 0)), pl.BlockSpec((1, 128), lambda i: (0, 0)),
                  pl.BlockSpec((1, 128), lambda i: (0, 0))],
        out_specs=[pl.BlockSpec((tr, 128), lambda i: (i, 0))] * 2, out_shape=[shp, shp],
        compiler_params=_cp(("parallel",)),
    )(pos_col, inv_freq, sign)


def _ln_proj(x, vecs, w_in4):
    tm = 512
    wc = w_in4.shape[2]

    def body(x_ref, vec_ref, w_ref, h_ref, proj_ref):
        @pl.when(pl.program_id(1) == 0)
        def _():
            xx = x_ref[...]
            g = _row(vec_ref, V_G1) * (1.0 + _row(vec_ref, V_SC1))
            h_ref[...] = (xx * _rms(xx) * g + _row(vec_ref, V_SH1)).astype(BF)
        proj_ref[...] = _dot(h_ref[...], w_ref[...]).astype(BF)

    return pl.pallas_call(
        body, name="ln_proj", grid=(SEQ // tm, N_CHIPS),
        in_specs=[pl.BlockSpec((tm, D_MODEL), lambda i, j: (i, 0)),
                  pl.BlockSpec((16, D_MODEL), lambda i, j: (0, 0)),
                  pl.BlockSpec((None, D_MODEL, wc), lambda i, j: (j, 0, 0))],
        out_specs=[pl.BlockSpec((tm, D_MODEL), lambda i, j: (i, 0)),
                   pl.BlockSpec((tm, wc), lambda i, j: (i, j))],
        out_shape=[jax.ShapeDtypeStruct((SEQ, D_MODEL), BF), jax.ShapeDtypeStruct((SEQ, D_IN), BF)],
        compiler_params=_cp(("parallel", "arbitrary")),
    )(x, vecs, w_in4)


def _lane_first(shape):
    lane = lax.broadcasted_iota(jnp.int32, shape, 1)
    return (lane & 32) == 0


def _rot(v, cos, sin_s):
    partner = jnp.where(_lane_first(v.shape), pltpu.roll(v, 96, 1), pltpu.roll(v, 32, 1))
    return v * cos + partner * sin_s


def _rot_t(dv, cos, sin_s):
    t = dv * sin_s
    partner = jnp.where(_lane_first(dv.shape), pltpu.roll(t, 96, 1), pltpu.roll(t, 32, 1))
    return dv * cos + partner


def _ret_masks(lg):
    t = RET_BLOCK
    ii = lax.broadcasted_iota(jnp.int32, (t, t), 0)
    jj = lax.broadcasted_iota(jnp.int32, (t, t), 1)
    dist = jnp.abs(ii - jj).astype(F32)
    future = (jj >> RET_CHUNK_SHIFT) > (ii >> RET_CHUNK_SHIFT)
    mask = jnp.where(future, 0.0, jnp.exp(lg * dist))
    ti = lax.broadcasted_iota(jnp.int32, (t, 1), 0).astype(F32)
    from_start = jnp.exp(lg * (ti + 1.0))
    to_end = jnp.exp(lg * (t - 1.0 - ti))
    whole = jnp.exp(jnp.full((1, 128), lg * t, F32))
    return mask, from_start, to_end, whole


def _head_lanes(shape, hh):
    lane = lax.broadcasted_iota(jnp.int32, shape, 1)
    return (lane >> 6) == hh


def _ret_specs():
    t = RET_BLOCK
    return dict(
        q=lambda f: pl.BlockSpec((t, 128), lambda p, n: (f(n), C_QR // 128 + p)),
        k=lambda f: pl.BlockSpec((t, 128), lambda p, n: (f(n), C_KR // 128 + p)),
        v=lambda f: pl.BlockSpec((t, 256), lambda p, n: (f(n), C_VR // 256 + p)),
        g=lambda f: pl.BlockSpec((t, 256), lambda p, n: (f(n), C_GR // 256 + p)),
        tab=lambda f: pl.BlockSpec((t, 128), lambda p, n: (f(n), 0)),
        wide=lambda f: pl.BlockSpec((t, 256), lambda p, n: (f(n), p)),
        narrow=lambda f: pl.BlockSpec((t, 128), lambda p, n: (f(n), p)),
        state=lambda f: pl.BlockSpec((None, None, 2, 128, 128), lambda p, n: (p, f(n), 0, 0, 0)),
    )


def _ret_fwd(proj, cos, sin_s, gn_g, log_gamma):
    t = RET_BLOCK
    nb = SEQ // t

    def body(lg_ref, q_ref, k_ref, v_ref, g_ref, cos_ref, sin_ref, gn_ref, o_ref, retg_ref, st_ref, state):
        p = pl.program_id(0)

        @pl.when(pl.program_id(1) == 0)
        def _():
            state[...] = jnp.zeros_like(state)

        cos, sn = cos_ref[...], sin_ref[...]
        q = _rot(q_ref[...].astype(F32), cos, sn)
        k = _rot(k_ref[...].astype(F32), cos, sn) * QK_SCALE
        for hh in range(2):
            lg = lg_ref[2 * p + hh]
            mask, from_start, to_end, whole = _ret_masks(lg)
            lanes = _head_lanes(q.shape, hh)
            qm = jnp.where(lanes, q, 0.0)
            km = jnp.where(lanes, k, 0.0)
            vh = v_ref[:, 128 * hh:128 * (hh + 1)]
            sc = _dot_nt(qm.astype(BF), km.astype(BF)) * mask
            st = state[hh]
            st_ref[hh] = st
            o = _dot(sc.astype(BF), vh) + _dot((qm * from_start).astype(BF), st.astype(BF))
            state[hh] = whole * st + _dot_tn((km * to_end).astype(BF), vh)
            d = o - _rowmean(o)
            nh = d * lax.rsqrt(_rowmean(d * d) + EPS)
            gr = g_ref[:, 128 * hh:128 * (hh + 1)].astype(F32)
            o_ref[:, 128 * hh:128 * (hh + 1)] = o
            retg_ref[:, 128 * hh:128 * (hh + 1)] = (
                gr * _sigmoid(gr) * nh * gn_ref[:, 128 * hh:128 * (hh + 1)]).astype(BF)

    sp = _ret_specs()
    ident = lambda n: n
    return pl.pallas_call(
        body, name="ret_fwd", grid=(N_PAIRS, nb),
        in_specs=[pl.BlockSpec(memory_space=pltpu.SMEM), sp["q"](ident), sp["k"](ident), sp["v"](ident),
                  sp["g"](ident), sp["tab"](ident), sp["tab"](ident),
                  pl.BlockSpec((1, 256), lambda p, n: (0, p))],
        out_specs=[sp["wide"](ident), sp["wide"](ident), sp["state"](ident)],
        out_shape=[jax.ShapeDtypeStruct((SEQ, D_MODEL), F32), jax.ShapeDtypeStruct((SEQ, D_MODEL), BF),
                   jax.ShapeDtypeStruct((N_PAIRS, nb, 2, 128, 128), F32)],
        scratch_shapes=[pltpu.VMEM((2, 128, 128), F32)],
        compiler_params=_cp(("parallel", "arbitrary")),
    )(log_gamma, proj, proj, proj, proj, cos, sin_s, gn_g)


def _ret_bwd(proj, cos, sin_s, dret, states, log_gamma):
    t = RET_BLOCK
    nb = SEQ // t

    def body(lg_ref, q_ref, k_ref, v_ref, cos_ref, sin_ref, do_ref, st_ref, dq_ref, dk_ref, dv_ref, dstate):
        p = pl.program_id(0)

        @pl.when(pl.program_id(1) == 0)
        def _():
            dstate[...] = jnp.zeros_like(dstate)

        cos, sn = cos_ref[...], sin_ref[...]
        q = _rot(q_ref[...].astype(F32), cos, sn)
        k = _rot(k_ref[...].astype(F32), cos, sn) * QK_SCALE
        dq_rot = jnp.zeros(q.shape, F32)
        dk_rot = jnp.zeros(q.shape, F32)
        for hh in range(2):
            lg = lg_ref[2 * p + hh]
            mask, from_start, to_end, whole = _ret_masks(lg)
            lanes = _head_lanes(q.shape, hh)
            qm = jnp.where(lanes, q, 0.0)
            km = jnp.where(lanes, k, 0.0)
            qb, kb = qm.astype(BF), km.astype(BF)
            vh = v_ref[:, 128 * hh:128 * (hh + 1)]
            do = do_ref[:, 128 * hh:128 * (hh + 1)]
            sc = (_dot_nt(qb, kb) * mask).astype(BF)
            st = st_ref[hh].astype(BF)
            dst = dstate[hh]
            dstb = dst.astype(BF)
            k_end = (km * to_end).astype(BF)
            q_start = (qm * from_start).astype(BF)
            dv_ref[:, 128 * hh:128 * (hh + 1)] = (_dot_tn(sc, do) + _dot(k_end, dstb)).astype(BF)
            dsc = (_dot_nt(do, vh) * mask).astype(BF)
            dq_h = _dot(dsc, kb) + _dot_nt(do, st) * from_start
            dq_rot = dq_rot + jnp.where(lanes, dq_h, 0.0)
            dk_rot = dk_rot + _dot_tn(dsc, qb) + _dot_nt(vh, dstb) * to_end
            dstate[hh] = whole * dst + _dot_tn(q_start, do)
        dq_ref[...] = _rot_t(dq_rot, cos, sn).astype(BF)
        dk_ref[...] = _rot_t(dk_rot * QK_SCALE, cos, sn).astype(BF)

    sp = _ret_specs()
    rev = lambda n: nb - 1 - n
    return pl.pallas_call(
        body, name="ret_bwd", grid=(N_PAIRS, nb),
        in_specs=[pl.BlockSpec(memory_space=pltpu.SMEM), sp["q"](rev), sp["k"](rev), sp["v"](rev),
                  sp["tab"](rev), sp["tab"](rev), sp["wide"](rev), sp["state"](rev)],
        out_specs=[sp["narrow"](rev), sp["narrow"](rev), sp["wide"](rev)],
        out_shape=[jax.ShapeDtypeStruct((SEQ, 512), BF), jax.ShapeDtypeStruct((SEQ, 512), BF),
                   jax.ShapeDtypeStruct((SEQ, D_MODEL), BF)],
        scratch_shapes=[pltpu.VMEM((2, 128, 128), F32)],
        compiler_params=_cp(("parallel", "arbitrary")),
    )(log_gamma, proj, proj, proj, cos, sin_s, dret, states)


def _split_dot(v, tri):
    hi = v.astype(BF)
    lo = (v - hi.astype(F32)).astype(BF)
    return _dot(hi, tri) + _dot(lo, tri)


def _log_one_minus_beta(z):
    return -(jnp.maximum(z, 0.0) + jnp.log(1.0 + jnp.exp(-jnp.abs(z))))


def _sb_fwd(proj):
    tq = SB_BLOCK
    nq = SEQ // tq

    def body(q_ref, k_ref, v_ref, o_ref, tot_ref):
        i = pl.program_id(1)
        q = q_ref[...]
        rr = lax.broadcasted_iota(jnp.int32, (tq, tq), 0)
        cc = lax.broadcasted_iota(jnp.int32, (tq, tq), 1)
        later = (rr > cc).astype(BF)
        valid = cc < rr
        accs, tots = [], []
        for hh in range(2):
            qm = jnp.where(_head_lanes(q.shape, hh), q, jnp.zeros_like(q))

            def tile(j):
                rows = pl.ds(pl.multiple_of(j * tq, tq), tq)
                z = _dot_nt(qm, k_ref[rows, :]) * QK_SCALE
                return z, _log_one_minus_beta(z), v_ref[rows, :]

            z, lm, vb = tile(i)
            lm = jnp.where(valid, lm, 0.0)
            a = jnp.where(valid, jnp.exp(z + lm + _split_dot(lm, later)), 0.0)
            acc = _dot(a.astype(BF), vb)
            run = jnp.sum(lm, axis=1, keepdims=True)

            def step(s, carry):
                acc, run = carry
                z, lm, vb = tile(i - 1 - s)
                a = jnp.exp(z + lm + _split_dot(lm, later) + run)
                return acc + _dot(a.astype(BF), vb), run + jnp.sum(lm, axis=1, keepdims=True)

            acc, run = lax.fori_loop(0, i, step, (acc, run))
            accs.append(acc)
            tots.append(run)
        first = _head_lanes(q.shape, 0)
        o_ref[...] = jnp.where(first, accs[0], accs[1]).astype(BF)
        tot_ref[...] = jnp.where(first, tots[0], tots[1])

    return pl.pallas_call(
        body, name="sb_fwd", grid=(N_PAIRS, nq),
        in_specs=[pl.BlockSpec((tq, 128), lambda p, i: (i, C_QS // 128 + p)),
                  pl.BlockSpec((SEQ, 128), lambda p, i: (0, C_KS // 128 + p)),
                  pl.BlockSpec((SEQ, 128), lambda p, i: (0, C_VS // 128 + p))],
        out_specs=[pl.BlockSpec((tq, 128), lambda p, i: (i, p))] * 2,
        out_shape=[jax.ShapeDtypeStruct((SEQ, 512), BF), jax.ShapeDtypeStruct((SEQ, 512), F32)],
        compiler_params=_cp(("parallel", "arbitrary")),
    )(proj, proj, proj)


def _sb_bwd(proj, dsb, tot):
    tq = SB_BLOCK
    nq = SEQ // tq

    def body(q_ref, k_ref, v_ref, do_ref, tot_ref, dq_ref, dk_ref, dv_ref, dk_acc, dv_acc):
        i = pl.program_id(1)

        @pl.when(i == 0)
        def _():
            dk_acc[...] = jnp.zeros_like(dk_acc)
            dv_acc[...] = jnp.zeros_like(dv_acc)

        q = q_ref[...]
        do = do_ref[...]
        tots = tot_ref[...]
        rr = lax.broadcasted_iota(jnp.int32, (tq, tq), 0)
        cc = lax.broadcasted_iota(jnp.int32, (tq, tq), 1)
        upto = (rr <= cc).astype(BF)
        before = (rr < cc).astype(BF)
        valid = cc < rr
        dqs = []
        for hh in range(2):
            lanes = _head_lanes(q.shape, hh)
            qm = jnp.where(lanes, q, jnp.zeros_like(q))
            dom = jnp.where(lanes, do, jnp.zeros_like(do))
            total = tots[:, 64 * hh:64 * hh + 1]

            def tile(j, carry, diagonal):
                dq, run_l, run_g = carry
                rows = pl.ds(pl.multiple_of(j * tq, tq), tq)
                kb, vb = k_ref[rows, :], v_ref[rows, :]
                z = _dot_nt(qm, kb) * QK_SCALE
                lm = _log_one_minus_beta(z)
                if diagonal:
                    lm = jnp.where(valid, lm, 0.0)
                between = total - (_split_dot(lm, upto) + run_l)
                a = jnp.exp(z + lm + between)
                if diagonal:
                    a = jnp.where(valid, a, 0.0)
                g = a * _dot_nt(dom, vb)
                prefix = _split_dot(g, before) + run_g
                dz = (g * jnp.exp(lm) - prefix * jnp.exp(z + lm)) * QK_SCALE
                if diagonal:
                    dz = jnp.where(valid, dz, 0.0)
                dzb = dz.astype(BF)
                dk_acc[rows, :] += _dot_tn(dzb, qm)
                dv_acc[rows, :] += _dot_tn(a.astype(BF), dom)
                return (dq + _dot(dzb, kb), run_l + jnp.sum(lm, axis=1, keepdims=True),
                        run_g + jnp.sum(g, axis=1, keepdims=True))

            zero = jnp.zeros((tq, 1), F32)
            carry = lax.fori_loop(0, i, lambda j, cr: tile(j, cr, False),
                                  (jnp.zeros((tq, 128), F32), zero, zero))
            dqs.append(tile(i, carry, True)[0])
        dq_ref[...] = jnp.where(_head_lanes(q.shape, 0), dqs[0], dqs[1]).astype(BF)

        @pl.when(i == nq - 1)
        def _():
            dk_ref[...] = dk_acc[...].astype(BF)
            dv_ref[...] = dv_acc[...].astype(BF)

    tile_spec = pl.BlockSpec((tq, 128), lambda p, i: (i, p))
    col_spec = pl.BlockSpec((SEQ, 128), lambda p, i: (0, p))
    shp = jax.ShapeDtypeStruct((SEQ, 512), BF)
    return pl.pallas_call(
        body, name="sb_bwd", grid=(N_PAIRS, nq),
        in_specs=[pl.BlockSpec((tq, 128), lambda p, i: (i, C_QS // 128 + p)),
                  pl.BlockSpec((SEQ, 128), lambda p, i: (0, C_KS // 128 + p)),
                  pl.BlockSpec((SEQ, 128), lambda p, i: (0, C_VS // 128 + p)),
                  tile_spec, tile_spec],
        out_specs=[tile_spec, col_spec, col_spec],
        out_shape=[shp, shp, shp],
        scratch_shapes=[pltpu.VMEM((SEQ, 128), F32), pltpu.VMEM((SEQ, 128), F32)],
        compiler_params=_cp(("parallel", "arbitrary")),
    )(proj, proj, proj, dsb, tot)


def _mix(retg, sb, proj, w_ret, w_sb4):
    tm, tn = 512, 256

    def body(r_ref, s_ref, ar_ref, as_ref, wr_ref, ws_ref, mix_ref, rb_ref, sbp_ref):
        rb = _dot(r_ref[...], wr_ref[...])
        sbp = _dot(s_ref[...], ws_ref[...])
        mix = _sigmoid(ar_ref[...].astype(F32)) * rb + _sigmoid(as_ref[...].astype(F32)) * sbp
        mix_ref[...] = mix.astype(BF)
        rb_ref[...] = rb.astype(BF)
        sbp_ref[...] = sbp.astype(BF)

    out = pl.BlockSpec((tm, tn), lambda j, i: (i, j))
    shp = jax.ShapeDtypeStruct((SEQ, D_MODEL), BF)
    return pl.pallas_call(
        body, name="mix", grid=(D_MODEL // tn, SEQ // tm),
        in_specs=[pl.BlockSpec((tm, D_MODEL), lambda j, i: (i, 0)),
                  pl.BlockSpec((tm, 512), lambda j, i: (i, 0)),
                  pl.BlockSpec((tm, tn), lambda j, i: (i, C_AR // tn + j)),
                  pl.BlockSpec((tm, tn), lambda j, i: (i, C_AS // tn + j)),
                  pl.BlockSpec((D_MODEL, tn), lambda j, i: (0, j)),
                  pl.BlockSpec((None, 512, tn), lambda j, i: (j, 0, 0))],
        out_specs=[out, out, out], out_shape=[shp, shp, shp],
        compiler_params=_cp(("parallel", "parallel")),
    )(retg, sb, proj, proj, w_ret, w_sb4)


def _out_proj(mixed, x, vecs, w_out):
    tm = 256

    def body(m_ref, x_ref, vec_ref, w_ref, y_ref, h1_ref, h2_ref):
        y = _dot(m_ref[...], w_ref[...])
        h1 = x_ref[...] + _row(vec_ref, V_GT1) * (y * _rms(y)) * _row(vec_ref, V_G2)
        g = _row(vec_ref, V_G3) * (1.0 + _row(vec_ref, V_SC2))
        y_ref[...] = y
        h1_ref[...] = h1
        h2_ref[...] = (h1 * _rms(h1) * g + _row(vec_ref, V_SH2)).astype(BF)

    row = pl.BlockSpec((tm, D_MODEL), lambda i: (i, 0))
    f32 = jax.ShapeDtypeStruct((SEQ, D_MODEL), F32)
    return pl.pallas_call(
        body, name="out_proj", grid=(SEQ // tm,),
        in_specs=[row, row, pl.BlockSpec((16, D_MODEL), lambda i: (0, 0)),
                  pl.BlockSpec((D_MODEL, D_MODEL), lambda i: (0, 0))],
        out_specs=[row, row, row],
        out_shape=[f32, f32, jax.ShapeDtypeStruct((SEQ, D_MODEL), BF)],
        compiler_params=_cp(("parallel",)),
    )(mixed, x, vecs, w_out)


def _ffn_up(h2, w_ff14):
    tm = 512

    def body(h_ref, w_ref, u_ref, a_ref):
        u = _dot(h_ref[...], w_ref[...])
        r = jnp.maximum(u, 0.0)
        u_ref[...] = u.astype(BF)
        a_ref[...] = (r * r).astype(BF)

    out = pl.BlockSpec((tm, D_MODEL), lambda j, i: (i, j))
    shp = jax.ShapeDtypeStruct((SEQ, D_FF), BF)
    return pl.pallas_call(
        body, name="ffn_up", grid=(N_CHIPS, SEQ // tm),
        in_specs=[pl.BlockSpec((tm, D_MODEL), lambda j, i: (i, 0)),
                  pl.BlockSpec((None, D_MODEL, D_MODEL), lambda j, i: (j, 0, 0))],
        out_specs=[out, out], out_shape=[shp, shp],
        compiler_params=_cp(("parallel", "parallel")),
    )(h2, w_ff14)


def _ffn_down_loss(act, h1, target, vecs, w_ff2):
    tm = 256

    def body(a_ref, h1_ref, t_ref, vec_ref, w_ref, dout_ref, df_ref, st_ref):
        @pl.when(pl.program_id(0) == 0)
        def _():
            st_ref[...] = jnp.zeros_like(st_ref)

        f = _dot(a_ref[...], w_ref[...])
        r4 = _rms(f)
        fn = f * r4
        gt2, g4 = _row(vec_ref, V_GT2), _row(vec_ref, V_G4)
        diff = h1_ref[...] + gt2 * fn * g4 - t_ref[...]
        dout = diff * (1.0 / D_MODEL)
        dfn = dout * gt2 * g4
        dout_ref[...] = dout
        df_ref[...] = (r4 * (dfn - fn * _rowmean(dfn * fn))).astype(BF)
        st_ref[0:1, :] += _colsum(dout * fn * g4)
        st_ref[1:2, :] += _colsum(dout * gt2 * fn)
        st_ref[2:3, :] += _colsum(diff * diff) * (0.5 / D_MODEL)

    row = pl.BlockSpec((tm, D_MODEL), lambda i: (i, 0))
    return pl.pallas_call(
        body, name="ffn_down_loss", grid=(SEQ // tm,),
        in_specs=[pl.BlockSpec((tm, D_FF), lambda i: (i, 0)), row, row,
                  pl.BlockSpec((16, D_MODEL), lambda i: (0, 0)),
                  pl.BlockSpec((D_FF, D_MODEL), lambda i: (0, 0))],
        out_specs=[row, row, pl.BlockSpec((8, D_MODEL), lambda i: (0, 0))],
        out_shape=[jax.ShapeDtypeStruct((SEQ, D_MODEL), F32), jax.ShapeDtypeStruct((SEQ, D_MODEL), BF),
                   jax.ShapeDtypeStruct((8, D_MODEL), F32)],
        compiler_params=_cp(("arbitrary",)),
    )(act, h1, target, vecs, w_ff2)


def _ffn_down_bwd(df, u, w_ff2):
    tm = 256

    def body(df_ref, u_ref, w_ref, du_ref):
        da = _dot_nt(df_ref[...], w_ref[...])
        du_ref[...] = (da * (2.0 * jnp.maximum(u_ref[...].astype(F32), 0.0))).astype(BF)

    return pl.pallas_call(
        body, name="ffn_down_bwd", grid=(D_FF // 1024, SEQ // tm),
        in_specs=[pl.BlockSpec((tm, D_MODEL), lambda j, i: (i, 0)),
                  pl.BlockSpec((tm, 1024), lambda j, i: (i, j)),
                  pl.BlockSpec((1024, D_MODEL), lambda j, i: (j, 0))],
        out_specs=pl.BlockSpec((tm, 1024), lambda j, i: (i, j)),
        out_shape=jax.ShapeDtypeStruct((SEQ, D_FF), BF),
        compiler_params=_cp(("parallel", "parallel")),
    )(df, u, w_ff2)


def _ffn_up_bwd(du, h1, y, dout, vecs, w_ff14):
    tm = 256

    def body(du_ref, h1_ref, y_ref, dout_ref, vec_ref, w_ref, dh1_ref, dy_ref, st_ref, acc):
        kk = pl.program_id(1)

        @pl.when((pl.program_id(0) == 0) & (kk == 0))
        def _():
            st_ref[...] = jnp.zeros_like(st_ref)

        part = _dot_nt(du_ref[...], w_ref[...])

        @pl.when(kk == 0)
        def _():
            acc[...] = part

        @pl.when(kk > 0)
        def _():
            acc[...] += part

        @pl.when(kk == N_CHIPS - 1)
        def _():
            dh2 = acc[...]
            h1 = h1_ref[...]
            r3 = _rms(h1)
            hn3 = h1 * r3
            g3, sc2 = _row(vec_ref, V_G3), _row(vec_ref, V_SC2)
            dhn3 = dh2 * g3 * (1.0 + sc2)
            dh1 = dout_ref[...] + r3 * (dhn3 - hn3 * _rowmean(dhn3 * hn3))
            y = y_ref[...]
            r2 = _rms(y)
            yn = y * r2
            gt1, g2 = _row(vec_ref, V_GT1), _row(vec_ref, V_G2)
            dyn = dh1 * gt1 * g2
            dh1_ref[...] = dh1
            dy_ref[...] = (r2 * (dyn - yn * _rowmean(dyn * yn))).astype(BF)
            st_ref[0:1, :] += _colsum(dh2)
            st_ref[1:2, :] += _colsum(dh2 * hn3 * g3)
            st_ref[2:3, :] += _colsum(dh2 * hn3 * (1.0 + sc2))
            st_ref[3:4, :] += _colsum(dh1 * yn * g2)
            st_ref[4:5, :] += _colsum(dh1 * gt1 * yn)

    row = pl.BlockSpec((tm, D_MODEL), lambda i, k: (i, 0))
    return pl.pallas_call(
        body, name="ffn_up_bwd", grid=(SEQ // tm, N_CHIPS),
        in_specs=[pl.BlockSpec((tm, D_MODEL), lambda i, k: (i, k)), row, row, row,
                  pl.BlockSpec((16, D_MODEL), lambda i, k: (0, 0)),
                  pl.BlockSpec((None, D_MODEL, D_MODEL), lambda i, k: (k, 0, 0))],
        out_specs=[row, row, pl.BlockSpec((8, D_MODEL), lambda i, k: (0, 0))],
        out_shape=[jax.ShapeDtypeStruct((SEQ, D_MODEL), F32), jax.ShapeDtypeStruct((SEQ, D_MODEL), BF),
                   jax.ShapeDtypeStruct((8, D_MODEL), F32)],
        scratch_shapes=[pltpu.VMEM((tm, D_MODEL), F32)],
        compiler_params=_cp(("arbitrary", "arbitrary")),
    )(du, h1, y, dout, vecs, w_ff14)


def _mix_bwd(dy, proj, rb, sbp, w_out, w_sb4):
    tm, tn = 512, 256

    def body(dy_ref, ar_ref, as_ref, rb_ref, sbp_ref, wo_ref, ws_ref,
             drb_ref, dsbp_ref, dar_ref, das_ref, dsb_ref, acc):
        j = pl.program_id(1)
        dm = _dot_nt(dy_ref[...], wo_ref[...])
        sr = _sigmoid(ar_ref[...].astype(F32))
        ss = _sigmoid(as_ref[...].astype(F32))
        dsbp = (dm * ss).astype(BF)
        drb_ref[...] = (dm * sr).astype(BF)
        dsbp_ref[...] = dsbp
        dar_ref[...] = (dm * rb_ref[...].astype(F32) * sr * (1.0 - sr)).astype(BF)
        das_ref[...] = (dm * sbp_ref[...].astype(F32) * ss * (1.0 - ss)).astype(BF)
        part = _dot_nt(dsbp, ws_ref[...])

        @pl.when(j == 0)
        def _():
            acc[...] = part

        @pl.when(j > 0)
        def _():
            acc[...] += part

        @pl.when(j == D_MODEL // tn - 1)
        def _():
            dsb_ref[...] = acc[...].astype(BF)

    tile = pl.BlockSpec((tm, tn), lambda i, j: (i, j))
    shp = jax.ShapeDtypeStruct((SEQ, D_MODEL), BF)
    return pl.pallas_call(
        body, name="mix_bwd", grid=(SEQ // tm, D_MODEL // tn),
        in_specs=[pl.BlockSpec((tm, D_MODEL), lambda i, j: (i, 0)),
                  pl.BlockSpec((tm, tn), lambda i, j: (i, C_AR // tn + j)),
                  pl.BlockSpec((tm, tn), lambda i, j: (i, C_AS // tn + j)),
                  tile, tile,
                  pl.BlockSpec((tn, D_MODEL), lambda i, j: (j, 0)),
                  pl.BlockSpec((None, 512, tn), lambda i, j: (j, 0, 0))],
        out_specs=[tile, tile, tile, tile, pl.BlockSpec((tm, 512), lambda i, j: (i, 0))],
        out_shape=[shp, shp, shp, shp, jax.ShapeDtypeStruct((SEQ, 512), BF)],
        scratch_shapes=[pltpu.VMEM((tm, 512), F32)],
        compiler_params=_cp(("parallel", "arbitrary")),
    )(dy, proj, proj, rb, sbp, w_out, w_sb4)


def _ret_branch_bwd(drb, proj, o_raw, gn_g, w_ret):
    tm, tn = 512, 256

    def body(d_ref, g_ref, o_ref, gn_ref, w_ref, dret_ref, dgr_ref, st_ref):
        @pl.when(pl.program_id(1) == 0)
        def _():
            st_ref[...] = jnp.zeros_like(st_ref)

        dretg = _dot_nt(d_ref[...], w_ref[...])
        for gi in range(tn // 128):
            cols = slice(128 * gi, 128 * (gi + 1))
            o = o_ref[:, cols]
            d = o - _rowmean(o)
            rstd = lax.rsqrt(_rowmean(d * d) + EPS)
            nh = d * rstd
            gain = gn_ref[:, cols]
            gr = g_ref[:, cols].astype(F32)
            sg = _sigmoid(gr)
            dg = dretg[:, cols]
            dgn = dg * gr * sg
            dnh = dgn * gain
            dgr_ref[:, cols] = (dg * nh * gain * sg * (1.0 + gr * (1.0 - sg))).astype(BF)
            dret_ref[:, cols] = (rstd * (dnh - _rowmean(dnh) - nh * _rowmean(dnh * nh))).astype(BF)
            st_ref[0:1, cols] += _colsum(dgn * nh)

    tile = pl.BlockSpec((tm, tn), lambda j, i: (i, j))
    shp = jax.ShapeDtypeStruct((SEQ, D_MODEL), BF)
    return pl.pallas_call(
        body, name="ret_branch_bwd", grid=(D_MODEL // tn, SEQ // tm),
        in_specs=[pl.BlockSpec((tm, D_MODEL), lambda j, i: (i, 0)),
                  pl.BlockSpec((tm, tn), lambda j, i: (i, C_GR // tn + j)),
                  tile, pl.BlockSpec((1, tn), lambda j, i: (0, j)),
                  pl.BlockSpec((tn, D_MODEL), lambda j, i: (j, 0))],
        out_specs=[tile, tile, pl.BlockSpec((8, tn), lambda j, i: (0, j))],
        out_shape=[shp, shp, jax.ShapeDtypeStruct((8, D_MODEL), F32)],
        compiler_params=_cp(("parallel", "arbitrary")),
    )(drb, proj, o_raw, gn_g, w_ret)


def _in_proj_bwd(dproj, x, dh1, vecs, w_in4):
    tm = 512
    wc = w_in4.shape[2]

    def body(dp_ref, x_ref, dh1_ref, vec_ref, w_ref, dx_ref, st_ref, acc):
        kk = pl.program_id(1)

        @pl.when((pl.program_id(0) == 0) & (kk == 0))
        def _():
            st_ref[...] = jnp.zeros_like(st_ref)

        part = _dot_nt(dp_ref[...], w_ref[...])

        @pl.when(kk == 0)
        def _():
            acc[...] = part

        @pl.when(kk > 0)
        def _():
            acc[...] += part

        @pl.when(kk == N_CHIPS - 1)
        def _():
            dh = acc[...]
            xx = x_ref[...]
            r1 = _rms(xx)
            xn = xx * r1
            g1, sc1 = _row(vec_ref, V_G1), _row(vec_ref, V_SC1)
            dxn = dh * g1 * (1.0 + sc1)
            dx_ref[...] = dh1_ref[...] + r1 * (dxn - xn * _rowmean(dxn * xn))
            st_ref[0:1, :] += _colsum(dh)
            st_ref[1:2, :] += _colsum(dh * xn * g1)
            st_ref[2:3, :] += _colsum(dh * xn * (1.0 + sc1))

    row = pl.BlockSpec((tm, D_MODEL), lambda i, k: (i, 0))
    return pl.pallas_call(
        body, name="in_proj_bwd", grid=(SEQ // tm, N_CHIPS),
        in_specs=[pl.BlockSpec((tm, wc), lambda i, k: (i, k)), row, row,
                  pl.BlockSpec((16, D_MODEL), lambda i, k: (0, 0)),
                  pl.BlockSpec((None, D_MODEL, wc), lambda i, k: (k, 0, 0))],
        out_specs=[row, pl.BlockSpec((8, D_MODEL), lambda i, k: (0, 0))],
        out_shape=[jax.ShapeDtypeStruct((SEQ, D_MODEL), F32), jax.ShapeDtypeStruct((8, D_MODEL), F32)],
        scratch_shapes=[pltpu.VMEM((tm, D_MODEL), F32)],
        compiler_params=_cp(("arbitrary", "arbitrary")),
    )(dproj, x, dh1, vecs, w_in4)


def _weight_grad(a, b, ta, tb, col_sharded, name):
    ka, nb_ = a.shape[1], b.shape[1]

    def body(a_ref, b_ref, o_ref):
        o_ref[...] = _dot_tn(a_ref[...], b_ref[...])

    if col_sharded:
        per = nb_ // N_CHIPS // tb
        out_shape = jax.ShapeDtypeStruct((N_CHIPS, ka, nb_ // N_CHIPS), F32)
        out_spec = pl.BlockSpec((None, ta, tb), lambda i, j: (j // per, i, j % per))
    else:
        per = ka // N_CHIPS // ta
        out_shape = jax.ShapeDtypeStruct((N_CHIPS, ka // N_CHIPS, nb_), F32)
        out_spec = pl.BlockSpec((None, ta, tb), lambda i, j: (i // per, i % per, j))
    return pl.pallas_call(
        body, name=name, grid=(ka // ta, nb_ // tb),
        in_specs=[pl.BlockSpec((SEQ, ta), lambda i, j: (0, i)), pl.BlockSpec((SEQ, tb), lambda i, j: (0, j))],
        out_specs=out_spec, out_shape=out_shape,
        compiler_params=_cp(("parallel", "parallel")),
    )(a, b)


def _rope_constants():
    freq = np.float32(ROPE_BASE) ** (-np.arange(0, 64, 2, dtype=np.float32) / np.float32(64))
    inv = np.tile(freq.astype(np.float32), 4).reshape(1, 128)
    sign = np.tile(np.concatenate([-np.ones(32, np.float32), np.ones(32, np.float32)]), 2).reshape(1, 128)
    return jnp.asarray(inv), jnp.asarray(sign)


def _log_gamma():
    return jnp.asarray(np.log1p(-(2.0 ** (-5.0 - np.arange(8, dtype=np.float64)))).astype(np.float32))


def _local_step(x, target, pos_col, vecs, gn_g, w_in4, w_ret, w_sb4, w_out, w_ff14, w_ff2):
    inv_freq, sign = _rope_constants()
    lg = _log_gamma()
    cos, sin_s = _rope_tables(pos_col, inv_freq, sign)
    h, proj = _ln_proj(x, vecs, w_in4)
    o_raw, retg, states = _ret_fwd(proj, cos, sin_s, gn_g, lg)
    sb, tot = _sb_fwd(proj)
    mixed, rb, sbp = _mix(retg, sb, proj, w_ret, w_sb4)
    y, h1, h2 = _out_proj(mixed, x, vecs, w_out)
    u, act = _ffn_up(h2, w_ff14)
    dout, df, st_a = _ffn_down_loss(act, h1, target, vecs, w_ff2)

    du = _ffn_down_bwd(df, u, w_ff2)
    g_ff2 = _weight_grad(act, df, 512, 1024, False, "grad_w_ff2")
    dh1, dy, st_b = _ffn_up_bwd(du, h1, y, dout, vecs, w_ff14)
    g_ff1 = _weight_grad(h2, du, 512, 1024, True, "grad_w_ff1")
    drb, dsbp, da_r, da_s, dsb = _mix_bwd(dy, proj, rb, sbp, w_out, w_sb4)
    g_out = _weight_grad(mixed, dy, 256, 1024, False, "grad_w_out")
    dret, dg_r, st_c = _ret_branch_bwd(drb, proj, o_raw, gn_g, w_ret)
    g_ret = _weight_grad(retg, drb, 256, 1024, False, "grad_w_ret")
    g_sb = _weight_grad(sb, dsbp, 512, 256, True, "grad_w_sb")
    dq_r, dk_r, dv_r = _ret_bwd(proj, cos, sin_s, dret, states, lg)
    dq_s, dk_s, dv_s = _sb_bwd(proj, dsb, tot)
    dproj = jnp.concatenate([dq_r, dk_r, dv_r, dg_r, dq_s, dk_s, dv_s, da_r, da_s], axis=1)
    dx, st_d = _in_proj_bwd(dproj, x, dh1, vecs, w_in4)
    g_in = _weight_grad(h, dproj, 512, 1664, True, "grad_w_in")

    payload = jnp.concatenate([
        st_d[0:2], st_b[3:4], st_b[0:2], st_a[0:1],
        st_d[2:3], st_b[4:5], st_b[2:3], st_a[1:2],
        st_c[0:1], st_a[2:3]], axis=0)
    return dx, (g_in, g_ret, g_sb, g_out, g_ff1, g_ff2), payload


def kernel(x, c, positions, ada_w, ada_b, pre_mix_g, post_mix_g, pre_ffn_g, post_ffn_g, w_in, ret_gn_g, w_ret_branch, w_sb_branch, w_out, w_ff1, w_ff2, loss_target, m_ada_w, m_ada_b, m_pre_mix_g, m_post_mix_g, m_pre_ffn_g, m_post_ffn_g, m_w_in, m_ret_gn_g, m_w_ret_branch, m_w_sb_branch, m_w_out, m_w_ff1, m_w_ff2, v_ada_w, v_ada_b, v_pre_mix_g, v_post_mix_g, v_pre_ffn_g, v_post_ffn_g, v_w_in, v_ret_gn_g, v_w_ret_branch, v_w_sb_branch, v_w_out, v_w_ff1, v_w_ff2):
    big = [w_in, w_ret_branch, w_sb_branch, w_out, w_ff1, w_ff2]
    big_m = [m_w_in, m_w_ret_branch, m_w_sb_branch, m_w_out, m_w_ff1, m_w_ff2]
    big_v = [v_w_in, v_w_ret_branch, v_w_sb_branch, v_w_out, v_w_ff1, v_w_ff2]
    names = ["w_in", "w_ret", "w_sb", "w_out", "w_ff1", "w_ff2"]

    c_all, mod4 = _mod_exchange(c, ada_w[0], ada_b.reshape(N_CHIPS, -1))
    mod = mod4.reshape(6, D_MODEL)
    vecs = jnp.concatenate([mod, pre_mix_g, post_mix_g, pre_ffn_g, post_ffn_g,
                            jnp.zeros((6, D_MODEL), F32)], axis=0)
    shards = [_cast_bf16(w[0], "cast_" + nm) for w, nm in zip(big, names)]
    w_in4, w_ret4, w_sb4, w_out4, w_ff14, w_ff24 = _gather_weights(shards)

    dx, grads, payload = _local_step(
        x[0], loss_target[0], positions.reshape(SEQ, 1), vecs, ret_gn_g,
        w_in4, w_ret4.reshape(D_MODEL, D_MODEL), w_sb4, w_out4.reshape(D_MODEL, D_MODEL),
        w_ff14, w_ff24.reshape(D_FF, D_MODEL))

    cidx = lax.axis_index("c").astype(jnp.int32).reshape(1)
    split = [g.reshape(N_CHIPS, 2, g.shape[1] // 2, g.shape[2]) for g in grads]
    from_sib = _pair_send_halves(split)
    pair_sums = [_pair_add(g, r, cidx, "pair_add_" + nm) for g, r, nm in zip(split, from_sib, names)]
    parts = _chip_exchange(pair_sums)
    halves = [_chip_add(p, "chip_add_" + nm) for p, nm in zip(parts, names)]
    full = _pair_exchange(halves)
    big_out = [_adamw(w[0], m[0], v[0], g.reshape(w.shape[1:]), "adamw_" + nm)
               for w, m, v, g, nm in zip(big, big_m, big_v, full, names)]

    def table(b6, g5):
        return jnp.concatenate([b6.reshape(6, D_MODEL)] + g5 + [jnp.zeros((5, D_MODEL), F32)], axis=0)

    wsm = table(ada_b, [pre_mix_g, post_mix_g, pre_ffn_g, post_ffn_g, ret_gn_g])
    msm = table(m_ada_b, [m_pre_mix_g, m_post_mix_g, m_pre_ffn_g, m_post_ffn_g, m_ret_gn_g])
    vsm = table(v_ada_b, [v_pre_mix_g, v_post_mix_g, v_pre_ffn_g, v_post_ffn_g, v_ret_gn_g])
    g_ada, gsm, dsm, mosm, vosm, loss = _small_exchange(
        payload.reshape(N_PAY, 1, D_MODEL), c_all, wsm, msm, vsm)
    ada_out = _adamw(ada_w[0], m_ada_w[0], v_ada_w[0], g_ada, "adamw_ada_w")

    def unpack(tab):
        return [tab[0:6].reshape(1, 6 * D_MODEL)] + [tab[6 + r:7 + r] for r in range(5)]

    def ordered(which):
        sm = unpack([gsm, dsm, mosm, vosm][which])
        bg = [o[which][None] for o in big_out]
        return [ada_out[which][None], sm[0], sm[1], sm[2], sm[3], sm[4], bg[0], sm[5]] + bg[1:]

    return (loss.reshape(()), dx[None], *ordered(0), *ordered(1), *ordered(2), *ordered(3))
```

```python
import functools

import numpy as np
import jax
import jax.numpy as jnp
from jax import lax
from jax.experimental import pallas as pl
from jax.experimental.pallas import tpu as pltpu

SEQ = 2048
D_MODEL = 1024
D_IN = 6656
D_FF = 4096
N_CHIPS = 4
EPS = 1e-6
ROPE_BASE = 10000.0
RET_BLOCK = 256
RET_CHUNK_SHIFT = 6
SB_BLOCK = 256
QK_SCALE = 0.125
N_PAIRS = 4

ADAM_LR = 0.001
ADAM_B1 = 0.9
ADAM_B2 = 0.999
ADAM_EPS = 1e-08
ADAM_WD = 0.01
ADAM_STEP = 10

BF = jnp.bfloat16
F32 = jnp.float32
MESH = pl.DeviceIdType.MESH
VMEM_LIMIT = 56 * 1024 * 1024
ANY = pl.BlockSpec(memory_space=pl.ANY)

C_QR, C_KR, C_VR, C_GR, C_QS, C_KS, C_VS, C_AR, C_AS = 0, 512, 1024, 2048, 3072, 3584, 4096, 4608, 5632

V_SH1, V_SC1, V_GT1, V_SH2, V_SC2, V_GT2, V_G1, V_G2, V_G3, V_G4 = range(10)
P_DSH1, P_DSC1, P_DGT1, P_DSH2, P_DSC2, P_DGT2, P_DG1, P_DG2, P_DG3, P_DG4, P_DGN, P_LOSS = range(12)
N_PAY = 12


def _cp(sem=None, **kw):
    if sem is not None:
        kw["dimension_semantics"] = sem
    return pltpu.CompilerParams(vmem_limit_bytes=VMEM_LIMIT, **kw)


def _dot(a, b):
    return jnp.dot(a, b, preferred_element_type=F32)


def _dot_nt(a, b):
    return lax.dot_general(a, b, (((1,), (1,)), ((), ())), preferred_element_type=F32)


def _dot_tn(a, b):
    return lax.dot_general(a, b, (((0,), (0,)), ((), ())), preferred_element_type=F32)


def _row(ref, i):
    return ref[i:i + 1, :]


def _rms(v):
    return lax.rsqrt(jnp.mean(v * v, axis=1, keepdims=True) + EPS)


def _colsum(v):
    return jnp.sum(v, axis=0, keepdims=True)


def _rowmean(v):
    return jnp.mean(v, axis=1, keepdims=True)


def _sigmoid(v):
    return 1.0 / (1.0 + jnp.exp(-v))


def _cast_bf16(w, kidx, name):
    rows, cols = w.shape
    tr = min(rows, 256)

    def body(k_ref, w_ref, o_ref):
        o_ref[...] = w_ref[...].astype(BF)

    return pl.pallas_call(
        body, name=name,
        grid_spec=pltpu.PrefetchScalarGridSpec(
            num_scalar_prefetch=1, grid=(rows // tr,),
            in_specs=[pl.BlockSpec((tr, cols), lambda i, k_ref: (i, 0))],
            out_specs=pl.BlockSpec((None, tr, cols), lambda i, k_ref: (k_ref[0], i, 0))),
        out_shape=jax.ShapeDtypeStruct((N_CHIPS, rows, cols), BF),
        compiler_params=_cp(("parallel",)),
    )(kidx, w)


def _adamw_math(w, g, m, v):
    m = ADAM_B1 * m + (1.0 - ADAM_B1) * g
    v = ADAM_B2 * v + (1.0 - ADAM_B2) * (g * g)
    m_hat = m / (1.0 - ADAM_B1 ** ADAM_STEP)
    v_hat = v / (1.0 - ADAM_B2 ** ADAM_STEP)
    delta = -ADAM_LR * (m_hat / (jnp.sqrt(v_hat) + ADAM_EPS) + ADAM_WD * w)
    return delta, m, v


def _adamw(w, m, v, g, name):
    rows, cols = w.shape
    tr = min(rows, 128)

    def body(w_ref, m_ref, v_ref, g_ref, go_ref, d_ref, mo_ref, vo_ref):
        gg = g_ref[...]
        d, mm, vv = _adamw_math(w_ref[...], gg, m_ref[...], v_ref[...])
        go_ref[...] = gg
        d_ref[...] = d
        mo_ref[...] = mm
        vo_ref[...] = vv

    spec = pl.BlockSpec((tr, cols), lambda i: (i, 0))
    shp = jax.ShapeDtypeStruct((rows, cols), F32)
    return pl.pallas_call(
        body, name=name, grid=(rows // tr,),
        in_specs=[spec] * 4, out_specs=[spec] * 4, out_shape=[shp] * 4,
        compiler_params=_cp(("parallel",)),
    )(w, m, v, g)


def _place():
    x, y, c = lax.axis_index("x"), lax.axis_index("y"), lax.axis_index("c")
    return x, y, c


def _gather_weights(bufs):
    n = len(bufs)

    def body(*refs):
        outs = refs[n:2 * n]
        send_sems, recv_sems = refs[2 * n:]
        x, y, c = _place()
        k = 2 * x + y
        sib = (x, y, 1 - c)
        chips = [(1 - x, y), (x, 1 - y), (1 - x, 1 - y)]

        def half(w, cc):
            rh = bufs[w].shape[1] // 2
            return pl.ds(cc * rh, rh)

        def copy(w, slot, part, to):
            return pltpu.make_async_remote_copy(
                src_ref=part, dst_ref=part, send_sem=send_sems.at[w * 6 + slot],
                recv_sem=recv_sems.at[w * 6 + slot], device_id=to, device_id_type=MESH)

        first = []
        for w in range(n):
            for j, (cx, cy) in enumerate(chips):
                cp = copy(w, j, outs[w].at[k, half(w, c)], (cx, cy, c))
                cp.start()
                first.append(cp)
        passed = []
        for w in range(n):
            for j, (cx, cy) in enumerate(chips):
                land = outs[w].at[2 * cx + cy, half(w, c)]
                copy(w, j, land, (cx, cy, c)).wait_recv()
                cp = copy(w, 3 + j, land, sib)
                cp.start()
                passed.append(cp)
        for w in range(n):
            for j, (cx, cy) in enumerate(chips):
                copy(w, 3 + j, outs[w].at[2 * cx + cy, half(w, 1 - c)], sib).wait_recv()
        for cp in first + passed:
            cp.wait_send()

    return pl.pallas_call(
        body, name="gather_weights",
        in_specs=[ANY] * n, out_specs=[ANY] * n,
        out_shape=[jax.ShapeDtypeStruct(b.shape, BF) for b in bufs],
        input_output_aliases={w: w for w in range(n)},
        scratch_shapes=[pltpu.SemaphoreType.DMA((6 * n,)), pltpu.SemaphoreType.DMA((6 * n,))],
    )(*bufs)


def _pair_send_halves(grads):
    n = len(grads)

    def body(*refs):
        ins, outs = refs[:n], refs[n:2 * n]
        send_sems, recv_sems = refs[2 * n:]
        x, y, c = _place()
        sib = (x, y, 1 - c)
        cps = []
        for w in range(n):
            cp = pltpu.make_async_remote_copy(
                src_ref=ins[w].at[:, 1 - c], dst_ref=outs[w], send_sem=send_sems.at[w],
                recv_sem=recv_sems.at[w], device_id=sib, device_id_type=MESH)
            cp.start()
            cps.append(cp)
        for cp in cps:
            cp.wait()

    return pl.pallas_call(
        body, name="pair_send_halves",
        in_specs=[ANY] * n, out_specs=[ANY] * n,
        out_shape=[jax.ShapeDtypeStruct((N_CHIPS,) + g.shape[2:], F32) for g in grads],
        scratch_shapes=[pltpu.SemaphoreType.DMA((n,)), pltpu.SemaphoreType.DMA((n,))],
    )(*grads)


def _pair_add(g, recv, cidx, name):
    _, _, rh, cols = g.shape
    tr = min(rh, 256)

    def body(c_ref, g_ref, r_ref, o_ref):
        o_ref[...] = (g_ref[...] + r_ref[...]).astype(BF)

    return pl.pallas_call(
        body, name=name,
        grid_spec=pltpu.PrefetchScalarGridSpec(
            num_scalar_prefetch=1, grid=(N_CHIPS, rh // tr),
            in_specs=[pl.BlockSpec((None, None, tr, cols), lambda s, i, c_ref: (s, c_ref[0], i, 0)),
                      pl.BlockSpec((None, tr, cols), lambda s, i, c_ref: (s, i, 0))],
            out_specs=pl.BlockSpec((None, tr, cols), lambda s, i, c_ref: (s, i, 0))),
        out_shape=jax.ShapeDtypeStruct((N_CHIPS, rh, cols), BF),
        compiler_params=_cp(("parallel", "parallel")),
    )(cidx, g, recv)


def _chip_exchange(sums):
    n = len(sums)

    def body(*refs):
        ins, outs = refs[:n], refs[n:2 * n]
        send_sems, recv_sems = refs[2 * n:]
        x, y, c = _place()
        chips = [(1 - x, y), (x, 1 - y), (1 - x, 1 - y)]
        sent = []
        for w in range(n):
            for j, (cx, cy) in enumerate(chips):
                cp = pltpu.make_async_remote_copy(
                    src_ref=ins[w].at[2 * cx + cy], dst_ref=outs[w].at[j],
                    send_sem=send_sems.at[w * 3 + j], recv_sem=recv_sems.at[w * 3 + j],
                    device_id=(cx, cy, c), device_id_type=MESH)
                cp.start()
                sent.append(cp)
        for w in range(n):
            for j, (cx, cy) in enumerate(chips):
                land = outs[w].at[j]
                pltpu.make_async_remote_copy(
                    src_ref=land, dst_ref=land, send_sem=send_sems.at[w * 3 + j],
                    recv_sem=recv_sems.at[w * 3 + j], device_id=(cx, cy, c),
                    device_id_type=MESH).wait_recv()
        for cp in sent:
            cp.wait_send()

    return pl.pallas_call(
        body, name="chip_exchange",
        in_specs=[ANY] * n, out_specs=[ANY] * n,
        out_shape=[jax.ShapeDtypeStruct((3,) + s.shape[1:], BF) for s in sums],
        scratch_shapes=[pltpu.SemaphoreType.DMA((3 * n,)), pltpu.SemaphoreType.DMA((3 * n,))],
    )(*sums)


def _chip_add(own, parts, kidx, cidx, name):
    _, rh, cols = parts.shape
    tr = min(rh, 256)

    def body(k_ref, c_ref, own_ref, p_ref, o_ref):
        acc = own_ref[...].astype(F32)
        for s in range(3):
            acc = acc + p_ref[s].astype(F32)
        o_ref[...] = acc

    return pl.pallas_call(
        body, name=name,
        grid_spec=pltpu.PrefetchScalarGridSpec(
            num_scalar_prefetch=2, grid=(rh // tr,),
            in_specs=[pl.BlockSpec((None, tr, cols), lambda i, k_ref, c_ref: (k_ref[0], i, 0)),
                      pl.BlockSpec((3, tr, cols), lambda i, k_ref, c_ref: (0, i, 0))],
            out_specs=pl.BlockSpec((None, tr, cols), lambda i, k_ref, c_ref: (c_ref[0], i, 0))),
        out_shape=jax.ShapeDtypeStruct((2, rh, cols), F32),
        compiler_params=_cp(("parallel",)),
    )(kidx, cidx, own, parts)


def _pair_exchange(bufs):
    n = len(bufs)

    def body(*refs):
        outs = refs[n:2 * n]
        send_sems, recv_sems = refs[2 * n:]
        x, y, c = _place()
        sib = (x, y, 1 - c)
        cps = []
        for w in range(n):
            cp = pltpu.make_async_remote_copy(
                src_ref=outs[w].at[c], dst_ref=outs[w].at[c], send_sem=send_sems.at[w],
                recv_sem=recv_sems.at[w], device_id=sib, device_id_type=MESH)
            cp.start()
            cps.append(cp)
        for w in range(n):
            land = outs[w].at[1 - c]
            pltpu.make_async_remote_copy(
                src_ref=land, dst_ref=land, send_sem=send_sems.at[w], recv_sem=recv_sems.at[w],
                device_id=sib, device_id_type=MESH).wait_recv()
        for cp in cps:
            cp.wait_send()

    return pl.pallas_call(
        body, name="pair_exchange",
        in_specs=[ANY] * n, out_specs=[ANY] * n,
        out_shape=[jax.ShapeDtypeStruct(b.shape, F32) for b in bufs],
        input_output_aliases={w: w for w in range(n)},
        scratch_shapes=[pltpu.SemaphoreType.DMA((n,)), pltpu.SemaphoreType.DMA((n,))],
    )(*bufs)


def _peers(x, y, c):
    out = []
    for code in range(1, 8):
        fx, fy, fc = (code >> 2) & 1, (code >> 1) & 1, code & 1
        px = 1 - x if fx else x
        py = 1 - y if fy else y
        pc = 1 - c if fc else c
        out.append((code, (px, py, pc)))
    return out


def _mod_exchange(c_row, ada_w, ada_b4):
    ncol = ada_w.shape[1]

    def body(c_ref, w_ref, b_ref, call_ref, mod_ref, part_ref, send_sems, recv_sems):
        x, y, c = _place()
        k = 2 * x + y
        me = 4 * x + 2 * y + c
        call_ref[pl.ds(me, 1), :] = c_ref[...]
        sends = []
        for code, peer in _peers(x, y, c):
            cp = pltpu.make_async_remote_copy(
                src_ref=c_ref, dst_ref=call_ref.at[pl.ds(me, 1), :],
                send_sem=send_sems.at[code], recv_sem=recv_sems.at[code],
                device_id=peer, device_id_type=MESH)
            cp.start()
            sends.append(cp)
        for code, (px, py, pc) in _peers(x, y, c):
            land = call_ref.at[pl.ds(4 * px + 2 * py + pc, 1), :]
            pltpu.make_async_remote_copy(
                src_ref=land, dst_ref=land, send_sem=send_sems.at[code], recv_sem=recv_sems.at[code],
                device_id=(px, py, pc), device_id_type=MESH).wait_recv()
        call = call_ref[...]
        act = call * _sigmoid(call)
        part = jnp.dot(act, w_ref[...], preferred_element_type=F32,
                       precision=lax.Precision.HIGHEST) + b_ref[pl.ds(k, 1), :]
        part_ref[...] = part
        mod_ref[pl.ds(k, 1), :] = part_ref[pl.ds(me, 1), :]
        chips = [(8 + j, peer) for j, (code, peer) in enumerate(_peers(x, y, c)) if code in (2, 4, 6)]
        for slot, (px, py, pc) in chips:
            cp = pltpu.make_async_remote_copy(
                src_ref=part_ref.at[pl.ds(4 * px + 2 * py + pc, 1), :], dst_ref=mod_ref.at[pl.ds(k, 1), :],
                send_sem=send_sems.at[slot], recv_sem=recv_sems.at[slot],
                device_id=(px, py, pc), device_id_type=MESH)
            cp.start()
            sends.append(cp)
        for slot, (px, py, pc) in chips:
            land = mod_ref.at[pl.ds(2 * px + py, 1), :]
            pltpu.make_async_remote_copy(
                src_ref=land, dst_ref=land, send_sem=send_sems.at[slot], recv_sem=recv_sems.at[slot],
                device_id=(px, py, pc), device_id_type=MESH).wait_recv()
        for cp in sends:
            cp.wait_send()

    vm = pl.BlockSpec(memory_space=pltpu.VMEM)
    return pl.pallas_call(
        body, name="mod_exchange",
        in_specs=[vm, vm, vm], out_specs=[vm, vm],
        out_shape=[jax.ShapeDtypeStruct((8, D_MODEL), F32), jax.ShapeDtypeStruct((N_CHIPS, ncol), F32)],
        scratch_shapes=[pltpu.VMEM((8, ncol), F32), pltpu.SemaphoreType.DMA((16,)),
                        pltpu.SemaphoreType.DMA((16,))],
        compiler_params=_cp(),
    )(c_row, ada_w, ada_b4)


def _small_exchange(payload, c_all, wsm, msm, vsm):
    ncol = 6 * D_MODEL // N_CHIPS

    def body(p_ref, call_ref, w_ref, m_ref, v_ref, gw_ref, g_ref, d_ref, mo_ref, vo_ref, loss_ref,
             all_ref, dm_ref, send_sems, recv_sems):
        x, y, c = _place()
        k = 2 * x + y
        me = 4 * x + 2 * y + c
        all_ref[:, pl.ds(me, 1), :] = p_ref[...]
        sends = []
        for code, peer in _peers(x, y, c):
            cp = pltpu.make_async_remote_copy(
                src_ref=p_ref, dst_ref=all_ref.at[:, pl.ds(me, 1), :],
                send_sem=send_sems.at[code], recv_sem=recv_sems.at[code],
                device_id=peer, device_id_type=MESH)
            cp.start()
            sends.append(cp)
        for code, (px, py, pc) in _peers(x, y, c):
            land = all_ref.at[:, pl.ds(4 * px + 2 * py + pc, 1), :]
            pltpu.make_async_remote_copy(
                src_ref=land, dst_ref=land, send_sem=send_sems.at[code], recv_sem=recv_sems.at[code],
                device_id=(px, py, pc), device_id_type=MESH).wait_recv()
        for cp in sends:
            cp.wait_send()
        tot = [_colsum(all_ref[r]) for r in range(N_PAY)]
        loss_ref[...] = jnp.sum(tot[P_LOSS], axis=1, keepdims=True)
        g_ref[...] = jnp.zeros_like(g_ref)
        for r in range(P_LOSS):
            g_ref[r:r + 1, :] = tot[r]
        g = g_ref[...]
        d, mm, vv = _adamw_math(w_ref[...], g, m_ref[...], v_ref[...])
        d_ref[...] = d
        mo_ref[...] = mm
        vo_ref[...] = vv
        half = D_MODEL // 2
        for kk in range(N_CHIPS):
            @pl.when(k == kk)
            def _():
                r0 = 3 * (kk // 2)
                if kk % 2 == 0:
                    dm_ref[:, :D_MODEL] = all_ref[r0]
                    dm_ref[:, D_MODEL:] = all_ref[r0 + 1][:, :half]
                else:
                    dm_ref[:, :half] = all_ref[r0 + 1][:, half:]
                    dm_ref[:, half:] = all_ref[r0 + 2]
        call = call_ref[...]
        act = call * _sigmoid(call)
        gw_ref[...] = lax.dot_general(act, dm_ref[...], (((0,), (0,)), ((), ())),
                                      preferred_element_type=F32, precision=lax.Precision.HIGHEST)

    vm = pl.BlockSpec(memory_space=pltpu.VMEM)
    small = jax.ShapeDtypeStruct((16, D_MODEL), F32)
    return pl.pallas_call(
        body, name="small_exchange",
        in_specs=[vm] * 5, out_specs=[vm] * 6,
        out_shape=[jax.ShapeDtypeStruct((D_MODEL, ncol), F32), small, small, small, small,
                   jax.ShapeDtypeStruct((1, 1), F32)],
        scratch_shapes=[pltpu.VMEM((N_PAY, 8, D_MODEL), F32), pltpu.VMEM((8, ncol), F32),
                        pltpu.SemaphoreType.DMA((8,)), pltpu.SemaphoreType.DMA((8,))],
        compiler_params=_cp(),
    )(payload, c_all, wsm, msm, vsm)


def _rope_tables(pos_col, inv_freq, sign):
    def body(p_ref, f_ref, s_ref, cos_ref, sin_ref):
        ang = p_ref[...].astype(F32) * f_ref[...]
        cos_ref[...] = jnp.cos(ang)
        sin_ref[...] = jnp.sin(ang) * s_ref[...]

    tr = 512
    shp = jax.ShapeDtypeStruct((SEQ, 128), F32)
    return pl.pallas_call(
        body, name="rope_tables", grid=(SEQ // tr,),
        in_specs=[pl.BlockSpec((tr, 1), lambda i: (i, 0)), pl.BlockSpec((1, 128), lambda i: (0, 0)),
                  pl.BlockSpec((1, 128), lambda i: (0, 0))],
        out_specs=[pl.BlockSpec((tr, 128), lambda i: (i, 0))] * 2, out_shape=[shp, shp],
        compiler_params=_cp(("parallel",)),
    )(pos_col, inv_freq, sign)


def _ln_proj(x, vecs, w_in4):
    tm = 512
    wc = w_in4.shape[2]

    def body(x_ref, vec_ref, w_ref, h_ref, proj_ref):
        @pl.when(pl.program_id(1) == 0)
        def _():
            xx = x_ref[...]
            g = _row(vec_ref, V_G1) * (1.0 + _row(vec_ref, V_SC1))
            h_ref[...] = (xx * _rms(xx) * g + _row(vec_ref, V_SH1)).astype(BF)
        proj_ref[...] = _dot(h_ref[...], w_ref[...]).astype(BF)

    return pl.pallas_call(
        body, name="ln_proj", grid=(SEQ // tm, N_CHIPS),
        in_specs=[pl.BlockSpec((tm, D_MODEL), lambda i, j: (i, 0)),
                  pl.BlockSpec((16, D_MODEL), lambda i, j: (0, 0)),
                  pl.BlockSpec((None, D_MODEL, wc), lambda i, j: (j, 0, 0))],
        out_specs=[pl.BlockSpec((tm, D_MODEL), lambda i, j: (i, 0)),
                   pl.BlockSpec((tm, wc), lambda i, j: (i, j))],
        out_shape=[jax.ShapeDtypeStruct((SEQ, D_MODEL), BF), jax.ShapeDtypeStruct((SEQ, D_IN), BF)],
        compiler_params=_cp(("parallel", "arbitrary")),
    )(x, vecs, w_in4)


def _lane_first(shape):
    lane = lax.broadcasted_iota(jnp.int32, shape, 1)
    return (lane & 32) == 0


def _rot(v, cos, sin_s):
    partner = jnp.where(_lane_first(v.shape), pltpu.roll(v, 96, 1), pltpu.roll(v, 32, 1))
    return v * cos + partner * sin_s


def _rot_t(dv, cos, sin_s):
    t = dv * sin_s
    partner = jnp.where(_lane_first(dv.shape), pltpu.roll(t, 96, 1), pltpu.roll(t, 32, 1))
    return dv * cos + partner


def _ret_masks(lg):
    t = RET_BLOCK
    ii = lax.broadcasted_iota(jnp.int32, (t, t), 0)
    jj = lax.broadcasted_iota(jnp.int32, (t, t), 1)
    dist = jnp.abs(ii - jj).astype(F32)
    future = (jj >> RET_CHUNK_SHIFT) > (ii >> RET_CHUNK_SHIFT)
    mask = jnp.where(future, 0.0, jnp.exp(lg * dist))
    ti = lax.broadcasted_iota(jnp.int32, (t, 1), 0).astype(F32)
    from_start = jnp.exp(lg * (ti + 1.0))
    to_end = jnp.exp(lg * (t - 1.0 - ti))
    whole = jnp.exp(jnp.full((1, 128), lg * t, F32))
    return mask, from_start, to_end, whole


def _head_lanes(shape, hh):
    lane = lax.broadcasted_iota(jnp.int32, shape, 1)
    return (lane >> 6) == hh


def _ret_specs():
    t = RET_BLOCK
    return dict(
        q=lambda f: pl.BlockSpec((t, 128), lambda p, n: (f(n), C_QR // 128 + p)),
        k=lambda f: pl.BlockSpec((t, 128), lambda p, n: (f(n), C_KR // 128 + p)),
        v=lambda f: pl.BlockSpec((t, 256), lambda p, n: (f(n), C_VR // 256 + p)),
        g=lambda f: pl.BlockSpec((t, 256), lambda p, n: (f(n), C_GR // 256 + p)),
        tab=lambda f: pl.BlockSpec((t, 128), lambda p, n: (f(n), 0)),
        wide=lambda f: pl.BlockSpec((t, 256), lambda p, n: (f(n), p)),
        narrow=lambda f: pl.BlockSpec((t, 128), lambda p, n: (f(n), p)),
        state=lambda f: pl.BlockSpec((None, None, 2, 128, 128), lambda p, n: (p, f(n), 0, 0, 0)),
    )


def _ret_fwd(proj, cos, sin_s, gn_g, log_gamma):
    t = RET_BLOCK
    nb = SEQ // t

    def body(lg_ref, q_ref, k_ref, v_ref, g_ref, cos_ref, sin_ref, gn_ref, o_ref, retg_ref, st_ref, state):
        p = pl.program_id(0)

        @pl.when(pl.program_id(1) == 0)
        def _():
            state[...] = jnp.zeros_like(state)

        cos, sn = cos_ref[...], sin_ref[...]
        q = _rot(q_ref[...].astype(F32), cos, sn)
        k = _rot(k_ref[...].astype(F32), cos, sn) * QK_SCALE
        for hh in range(2):
            lg = lg_ref[2 * p + hh]
            mask, from_start, to_end, whole = _ret_masks(lg)
            lanes = _head_lanes(q.shape, hh)
            qm = jnp.where(lanes, q, 0.0)
            km = jnp.where(lanes, k, 0.0)
            vh = v_ref[:, 128 * hh:128 * (hh + 1)]
            sc = _dot_nt(qm.astype(BF), km.astype(BF)) * mask
            st = state[hh]
            st_ref[hh] = st
            o = _dot(sc.astype(BF), vh) + _dot((qm * from_start).astype(BF), st.astype(BF))
            state[hh] = whole * st + _dot_tn((km * to_end).astype(BF), vh)
            d = o - _rowmean(o)
            nh = d * lax.rsqrt(_rowmean(d * d) + EPS)
            gr = g_ref[:, 128 * hh:128 * (hh + 1)].astype(F32)
            o_ref[:, 128 * hh:128 * (hh + 1)] = o
            retg_ref[:, 128 * hh:128 * (hh + 1)] = (
                gr * _sigmoid(gr) * nh * gn_ref[:, 128 * hh:128 * (hh + 1)]).astype(BF)

    sp = _ret_specs()
    ident = lambda n: n
    return pl.pallas_call(
        body, name="ret_fwd", grid=(N_PAIRS, nb),
        in_specs=[pl.BlockSpec(memory_space=pltpu.SMEM), sp["q"](ident), sp["k"](ident), sp["v"](ident),
                  sp["g"](ident), sp["tab"](ident), sp["tab"](ident),
                  pl.BlockSpec((1, 256), lambda p, n: (0, p))],
        out_specs=[sp["wide"](ident), sp["wide"](ident), sp["state"](ident)],
        out_shape=[jax.ShapeDtypeStruct((SEQ, D_MODEL), F32), jax.ShapeDtypeStruct((SEQ, D_MODEL), BF),
                   jax.ShapeDtypeStruct((N_PAIRS, nb, 2, 128, 128), F32)],
        scratch_shapes=[pltpu.VMEM((2, 128, 128), F32)],
        compiler_params=_cp(("parallel", "arbitrary")),
    )(log_gamma, proj, proj, proj, proj, cos, sin_s, gn_g)


def _ret_bwd(proj, cos, sin_s, dret, states, log_gamma):
    t = RET_BLOCK
    nb = SEQ // t

    def body(lg_ref, q_ref, k_ref, v_ref, cos_ref, sin_ref, do_ref, st_ref, dq_ref, dk_ref, dv_ref, dstate):
        p = pl.program_id(0)

        @pl.when(pl.program_id(1) == 0)
        def _():
            dstate[...] = jnp.zeros_like(dstate)

        cos, sn = cos_ref[...], sin_ref[...]
        q = _rot(q_ref[...].astype(F32), cos, sn)
        k = _rot(k_ref[...].astype(F32), cos, sn) * QK_SCALE
        dq_rot = jnp.zeros(q.shape, F32)
        dk_rot = jnp.zeros(q.shape, F32)
        for hh in range(2):
            lg = lg_ref[2 * p + hh]
            mask, from_start, to_end, whole = _ret_masks(lg)
            lanes = _head_lanes(q.shape, hh)
            qm = jnp.where(lanes, q, 0.0)
            km = jnp.where(lanes, k, 0.0)
            qb, kb = qm.astype(BF), km.astype(BF)
            vh = v_ref[:, 128 * hh:128 * (hh + 1)]
            do = do_ref[:, 128 * hh:128 * (hh + 1)]
            sc = (_dot_nt(qb, kb) * mask).astype(BF)
            st = st_ref[hh].astype(BF)
            dst = dstate[hh]
            dstb = dst.astype(BF)
            k_end = (km * to_end).astype(BF)
            q_start = (qm * from_start).astype(BF)
            dv_ref[:, 128 * hh:128 * (hh + 1)] = (_dot_tn(sc, do) + _dot(k_end, dstb)).astype(BF)
            dsc = (_dot_nt(do, vh) * mask).astype(BF)
            dq_h = _dot(dsc, kb) + _dot_nt(do, st) * from_start
            dq_rot = dq_rot + jnp.where(lanes, dq_h, 0.0)
            dk_rot = dk_rot + _dot_tn(dsc, qb) + _dot_nt(vh, dstb) * to_end
            dstate[hh] = whole * dst + _dot_tn(q_start, do)
        dq_ref[...] = _rot_t(dq_rot, cos, sn).astype(BF)
        dk_ref[...] = _rot_t(dk_rot * QK_SCALE, cos, sn).astype(BF)

    sp = _ret_specs()
    rev = lambda n: nb - 1 - n
    return pl.pallas_call(
        body, name="ret_bwd", grid=(N_PAIRS, nb),
        in_specs=[pl.BlockSpec(memory_space=pltpu.SMEM), sp["q"](rev), sp["k"](rev), sp["v"](rev),
                  sp["tab"](rev), sp["tab"](rev), sp["wide"](rev), sp["state"](rev)],
        out_specs=[sp["narrow"](rev), sp["narrow"](rev), sp["wide"](rev)],
        out_shape=[jax.ShapeDtypeStruct((SEQ, 512), BF), jax.ShapeDtypeStruct((SEQ, 512), BF),
                   jax.ShapeDtypeStruct((SEQ, D_MODEL), BF)],
        scratch_shapes=[pltpu.VMEM((2, 128, 128), F32)],
        compiler_params=_cp(("parallel", "arbitrary")),
    )(log_gamma, proj, proj, proj, cos, sin_s, dret, states)


def _stack_heads(v):
    return jnp.concatenate([jnp.where(_head_lanes(v.shape, hh), v, jnp.zeros_like(v)) for hh in range(2)], axis=0)


def _unstack_heads(v):
    t = v.shape[0] // 2
    return jnp.where(_head_lanes((t, v.shape[1]), 0), v[:t], v[t:])


def _sb_masks(t):
    rr = lax.broadcasted_iota(jnp.int32, (t, t), 0)
    cc = lax.broadcasted_iota(jnp.int32, (t, t), 1)
    r2 = lax.broadcasted_iota(jnp.int32, (2 * t, t), 0) & (t - 1)
    c2 = lax.broadcasted_iota(jnp.int32, (2 * t, t), 1)
    return rr, cc, c2 < r2


def _split_dot2(v, tri):
    hi = v.astype(BF)
    lo = (v - hi.astype(F32)).astype(BF)
    both = _dot(jnp.concatenate([hi, lo], axis=0), tri)
    return both[:v.shape[0]] + both[v.shape[0]:]


def _log_one_minus_beta(z):
    return -(jnp.maximum(z, 0.0) + jnp.log(1.0 + jnp.exp(-jnp.abs(z))))


def _sb_fwd(proj):
    t = SB_BLOCK
    nq = SEQ // t

    def body(q_ref, k_ref, v_ref, o_ref, tot_ref, kt_ref):
        i = pl.program_id(1)

        @pl.when(i == 0)
        def _():
            for jj in range(nq):
                kt_ref[jj] = k_ref[jj * t:(jj + 1) * t, :].T

        q2 = _stack_heads((q_ref[...].astype(F32) * QK_SCALE).astype(BF))
        rr, cc, valid = _sb_masks(t)
        later = (rr > cc).astype(BF)

        def tile(j, carry, diagonal):
            acc, run = carry
            z = _dot(q2, kt_ref[j])
            lm = _log_one_minus_beta(z)
            if diagonal:
                lm = jnp.where(valid, lm, 0.0)
            after = _split_dot2(lm, later)
            a = jnp.exp(z + lm + after + run)
            if diagonal:
                a = jnp.where(valid, a, 0.0)
            vb = v_ref[pl.ds(pl.multiple_of(j * t, t), t), :]
            return acc + _dot(a.astype(BF), vb), run + after[:, 0:1] + lm[:, 0:1]

        carry = tile(i, (jnp.zeros((2 * t, 128), F32), jnp.zeros((2 * t, 1), F32)), True)
        acc, run = lax.fori_loop(0, i, lambda s, cr: tile(i - 1 - s, cr, False), carry)
        o_ref[...] = _unstack_heads(acc).astype(BF)
        tot_ref[...] = _unstack_heads(jnp.broadcast_to(run, (2 * t, 128)))

    return pl.pallas_call(
        body, name="sb_fwd", grid=(N_PAIRS, nq),
        in_specs=[pl.BlockSpec((t, 128), lambda p, i: (i, C_QS // 128 + p)),
                  pl.BlockSpec((SEQ, 128), lambda p, i: (0, C_KS // 128 + p)),
                  pl.BlockSpec((SEQ, 128), lambda p, i: (0, C_VS // 128 + p))],
        out_specs=[pl.BlockSpec((t, 128), lambda p, i: (i, p))] * 2,
        out_shape=[jax.ShapeDtypeStruct((SEQ, 512), BF), jax.ShapeDtypeStruct((SEQ, 512), F32)],
        scratch_shapes=[pltpu.VMEM((nq, 128, t), BF)],
        compiler_params=_cp(("parallel", "arbitrary")),
    )(proj, proj, proj)


def _sb_bwd(proj, dsb, tot):
    t = SB_BLOCK
    nq = SEQ // t

    def body(q_ref, k_ref, v_ref, do_ref, tot_ref, dq_ref, dk_ref, dv_ref, kt_ref, vt_ref, dkt_acc, dvt_acc):
        i = pl.program_id(1)

        @pl.when(i == 0)
        def _():
            dkt_acc[...] = jnp.zeros_like(dkt_acc)
            dvt_acc[...] = jnp.zeros_like(dvt_acc)
            for jj in range(nq):
                kt_ref[jj] = k_ref[jj * t:(jj + 1) * t, :].T
                vt_ref[jj] = v_ref[jj * t:(jj + 1) * t, :].T

        q2 = _stack_heads((q_ref[...].astype(F32) * QK_SCALE).astype(BF))
        do2 = _stack_heads(do_ref[...])
        q2t, do2t = q2.T, do2.T
        tots = tot_ref[...]
        total = jnp.concatenate([tots[:, 0:1], tots[:, 64:65]], axis=0)
        rr, cc, valid = _sb_masks(t)
        upto = (rr <= cc).astype(BF)
        before = (rr < cc).astype(BF)

        def tile(j, carry, diagonal):
            dq, run_l, run_g = carry
            z = _dot(q2, kt_ref[j])
            lm = _log_one_minus_beta(z)
            if diagonal:
                lm = jnp.where(valid, lm, 0.0)
            incl = _split_dot2(lm, upto)
            a = jnp.exp(z + lm + (total - (incl + run_l)))
            if diagonal:
                a = jnp.where(valid, a, 0.0)
            g = a * _dot(do2, vt_ref[j])
            excl = _split_dot2(g, before)
            dz = g * jnp.exp(lm) - (excl + run_g) * jnp.exp(z + lm)
            if diagonal:
                dz = jnp.where(valid, dz, 0.0)
            dzb = dz.astype(BF)
            dkt_acc[j] += _dot(q2t, dzb)
            dvt_acc[j] += _dot(do2t, a.astype(BF))
            kb = k_ref[pl.ds(pl.multiple_of(j * t, t), t), :]
            return (dq + _dot(dzb, kb), run_l + incl[:, t - 1:t], run_g + excl[:, t - 1:t] + g[:, t - 1:t])

        zero = jnp.zeros((2 * t, 1), F32)
        carry = lax.fori_loop(0, i, lambda j, cr: tile(j, cr, False), (jnp.zeros((2 * t, 128), F32), zero, zero))
        dq = tile(i, carry, True)[0]
        dq_ref[...] = (_unstack_heads(dq) * QK_SCALE).astype(BF)

        @pl.when(i == nq - 1)
        def _():
            for jj in range(nq):
                dk_ref[jj * t:(jj + 1) * t, :] = dkt_acc[jj].T.astype(BF)
                dv_ref[jj * t:(jj + 1) * t, :] = dvt_acc[jj].T.astype(BF)

    tile_spec = pl.BlockSpec((t, 128), lambda p, i: (i, p))
    col_spec = pl.BlockSpec((SEQ, 128), lambda p, i: (0, p))
    shp = jax.ShapeDtypeStruct((SEQ, 512), BF)
    return pl.pallas_call(
        body, name="sb_bwd", grid=(N_PAIRS, nq),
        in_specs=[pl.BlockSpec((t, 128), lambda p, i: (i, C_QS // 128 + p)),
                  pl.BlockSpec((SEQ, 128), lambda p, i: (0, C_KS // 128 + p)),
                  pl.BlockSpec((SEQ, 128), lambda p, i: (0, C_VS // 128 + p)),
                  tile_spec, tile_spec],
        out_specs=[tile_spec, col_spec, col_spec],
        out_shape=[shp, shp, shp],
        scratch_shapes=[pltpu.VMEM((nq, 128, t), BF), pltpu.VMEM((nq, 128, t), BF),
                        pltpu.VMEM((nq, 128, t), F32), pltpu.VMEM((nq, 128, t), F32)],
        compiler_params=_cp(("parallel", "arbitrary")),
    )(proj, proj, proj, dsb, tot)


def _mix(retg, sb, proj, w_ret, w_sb4):
    tm, tn = 512, 256

    def body(r_ref, s_ref, ar_ref, as_ref, wr_ref, ws_ref, mix_ref, rb_ref, sbp_ref):
        rb = _dot(r_ref[...], wr_ref[...])
        sbp = _dot(s_ref[...], ws_ref[...])
        mix = _sigmoid(ar_ref[...].astype(F32)) * rb + _sigmoid(as_ref[...].astype(F32)) * sbp
        mix_ref[...] = mix.astype(BF)
        rb_ref[...] = rb.astype(BF)
        sbp_ref[...] = sbp.astype(BF)

    out = pl.BlockSpec((tm, tn), lambda j, i: (i, j))
    shp = jax.ShapeDtypeStruct((SEQ, D_MODEL), BF)
    return pl.pallas_call(
        body, name="mix", grid=(D_MODEL // tn, SEQ // tm),
        in_specs=[pl.BlockSpec((tm, D_MODEL), lambda j, i: (i, 0)),
                  pl.BlockSpec((tm, 512), lambda j, i: (i, 0)),
                  pl.BlockSpec((tm, tn), lambda j, i: (i, C_AR // tn + j)),
                  pl.BlockSpec((tm, tn), lambda j, i: (i, C_AS // tn + j)),
                  pl.BlockSpec((D_MODEL, tn), lambda j, i: (0, j)),
                  pl.BlockSpec((None, 512, tn), lambda j, i: (j, 0, 0))],
        out_specs=[out, out, out], out_shape=[shp, shp, shp],
        compiler_params=_cp(("parallel", "parallel")),
    )(retg, sb, proj, proj, w_ret, w_sb4)


def _out_proj(mixed, x, vecs, w_out):
    tm = 256

    def body(m_ref, x_ref, vec_ref, w_ref, y_ref, h1_ref, h2_ref):
        y = _dot(m_ref[...], w_ref[...])
        h1 = x_ref[...] + _row(vec_ref, V_GT1) * (y * _rms(y)) * _row(vec_ref, V_G2)
        g = _row(vec_ref, V_G3) * (1.0 + _row(vec_ref, V_SC2))
        y_ref[...] = y
        h1_ref[...] = h1
        h2_ref[...] = (h1 * _rms(h1) * g + _row(vec_ref, V_SH2)).astype(BF)

    row = pl.BlockSpec((tm, D_MODEL), lambda i: (i, 0))
    f32 = jax.ShapeDtypeStruct((SEQ, D_MODEL), F32)
    return pl.pallas_call(
        body, name="out_proj", grid=(SEQ // tm,),
        in_specs=[row, row, pl.BlockSpec((16, D_MODEL), lambda i: (0, 0)),
                  pl.BlockSpec((D_MODEL, D_MODEL), lambda i: (0, 0))],
        out_specs=[row, row, row],
        out_shape=[f32, f32, jax.ShapeDtypeStruct((SEQ, D_MODEL), BF)],
        compiler_params=_cp(("parallel",)),
    )(mixed, x, vecs, w_out)


def _ffn_up(h2, w_ff14):
    tm = 512

    def body(h_ref, w_ref, u_ref, a_ref):
        u = _dot(h_ref[...], w_ref[...])
        r = jnp.maximum(u, 0.0)
        u_ref[...] = u.astype(BF)
        a_ref[...] = (r * r).astype(BF)

    out = pl.BlockSpec((tm, D_MODEL), lambda j, i: (i, j))
    shp = jax.ShapeDtypeStruct((SEQ, D_FF), BF)
    return pl.pallas_call(
        body, name="ffn_up", grid=(N_CHIPS, SEQ // tm),
        in_specs=[pl.BlockSpec((tm, D_MODEL), lambda j, i: (i, 0)),
                  pl.BlockSpec((None, D_MODEL, D_MODEL), lambda j, i: (j, 0, 0))],
        out_specs=[out, out], out_shape=[shp, shp],
        compiler_params=_cp(("parallel", "parallel")),
    )(h2, w_ff14)


def _ffn_down_loss(act, h1, target, vecs, w_ff2):
    tm = 256

    def body(a_ref, h1_ref, t_ref, vec_ref, w_ref, dout_ref, df_ref, st_ref):
        @pl.when(pl.program_id(0) == 0)
        def _():
            st_ref[...] = jnp.zeros_like(st_ref)

        f = _dot(a_ref[...], w_ref[...])
        r4 = _rms(f)
        fn = f * r4
        gt2, g4 = _row(vec_ref, V_GT2), _row(vec_ref, V_G4)
        diff = h1_ref[...] + gt2 * fn * g4 - t_ref[...]
        dout = diff * (1.0 / D_MODEL)
        dfn = dout * gt2 * g4
        dout_ref[...] = dout
        df_ref[...] = (r4 * (dfn - fn * _rowmean(dfn * fn))).astype(BF)
        st_ref[0:1, :] += _colsum(dout * fn * g4)
        st_ref[1:2, :] += _colsum(dout * gt2 * fn)
        st_ref[2:3, :] += _colsum(diff * diff) * (0.5 / D_MODEL)

    row = pl.BlockSpec((tm, D_MODEL), lambda i: (i, 0))
    return pl.pallas_call(
        body, name="ffn_down_loss", grid=(SEQ // tm,),
        in_specs=[pl.BlockSpec((tm, D_FF), lambda i: (i, 0)), row, row,
                  pl.BlockSpec((16, D_MODEL), lambda i: (0, 0)),
                  pl.BlockSpec((D_FF, D_MODEL), lambda i: (0, 0))],
        out_specs=[row, row, pl.BlockSpec((8, D_MODEL), lambda i: (0, 0))],
        out_shape=[jax.ShapeDtypeStruct((SEQ, D_MODEL), F32), jax.ShapeDtypeStruct((SEQ, D_MODEL), BF),
                   jax.ShapeDtypeStruct((8, D_MODEL), F32)],
        compiler_params=_cp(("arbitrary",)),
    )(act, h1, target, vecs, w_ff2)


def _ffn_down_bwd(df, u, w_ff2):
    tm = 256

    def body(df_ref, u_ref, w_ref, du_ref):
        da = _dot_nt(df_ref[...], w_ref[...])
        du_ref[...] = (da * (2.0 * jnp.maximum(u_ref[...].astype(F32), 0.0))).astype(BF)

    return pl.pallas_call(
        body, name="ffn_down_bwd", grid=(D_FF // 1024, SEQ // tm),
        in_specs=[pl.BlockSpec((tm, D_MODEL), lambda j, i: (i, 0)),
                  pl.BlockSpec((tm, 1024), lambda j, i: (i, j)),
                  pl.BlockSpec((1024, D_MODEL), lambda j, i: (j, 0))],
        out_specs=pl.BlockSpec((tm, 1024), lambda j, i: (i, j)),
        out_shape=jax.ShapeDtypeStruct((SEQ, D_FF), BF),
        compiler_params=_cp(("parallel", "parallel")),
    )(df, u, w_ff2)


def _ffn_up_bwd(du, h1, y, dout, vecs, w_ff14):
    tm = 256

    def body(du_ref, h1_ref, y_ref, dout_ref, vec_ref, w_ref, dh1_ref, dy_ref, st_ref, acc):
        kk = pl.program_id(1)

        @pl.when((pl.program_id(0) == 0) & (kk == 0))
        def _():
            st_ref[...] = jnp.zeros_like(st_ref)

        part = _dot_nt(du_ref[...], w_ref[...])

        @pl.when(kk == 0)
        def _():
            acc[...] = part

        @pl.when(kk > 0)
        def _():
            acc[...] += part

        @pl.when(kk == N_CHIPS - 1)
        def _():
            dh2 = acc[...]
            h1 = h1_ref[...]
            r3 = _rms(h1)
            hn3 = h1 * r3
            g3, sc2 = _row(vec_ref, V_G3), _row(vec_ref, V_SC2)
            dhn3 = dh2 * g3 * (1.0 + sc2)
            dh1 = dout_ref[...] + r3 * (dhn3 - hn3 * _rowmean(dhn3 * hn3))
            y = y_ref[...]
            r2 = _rms(y)
            yn = y * r2
            gt1, g2 = _row(vec_ref, V_GT1), _row(vec_ref, V_G2)
            dyn = dh1 * gt1 * g2
            dh1_ref[...] = dh1
            dy_ref[...] = (r2 * (dyn - yn * _rowmean(dyn * yn))).astype(BF)
            st_ref[0:1, :] += _colsum(dh2)
            st_ref[1:2, :] += _colsum(dh2 * hn3 * g3)
            st_ref[2:3, :] += _colsum(dh2 * hn3 * (1.0 + sc2))
            st_ref[3:4, :] += _colsum(dh1 * yn * g2)
            st_ref[4:5, :] += _colsum(dh1 * gt1 * yn)

    row = pl.BlockSpec((tm, D_MODEL), lambda i, k: (i, 0))
    return pl.pallas_call(
        body, name="ffn_up_bwd", grid=(SEQ // tm, N_CHIPS),
        in_specs=[pl.BlockSpec((tm, D_MODEL), lambda i, k: (i, k)), row, row, row,
                  pl.BlockSpec((16, D_MODEL), lambda i, k: (0, 0)),
                  pl.BlockSpec((None, D_MODEL, D_MODEL), lambda i, k: (k, 0, 0))],
        out_specs=[row, row, pl.BlockSpec((8, D_MODEL), lambda i, k: (0, 0))],
        out_shape=[jax.ShapeDtypeStruct((SEQ, D_MODEL), F32), jax.ShapeDtypeStruct((SEQ, D_MODEL), BF),
                   jax.ShapeDtypeStruct((8, D_MODEL), F32)],
        scratch_shapes=[pltpu.VMEM((tm, D_MODEL), F32)],
        compiler_params=_cp(("arbitrary", "arbitrary")),
    )(du, h1, y, dout, vecs, w_ff14)


def _mix_bwd(dy, proj, rb, sbp, w_out, w_sb4):
    tm, tn = 512, 256

    def body(dy_ref, ar_ref, as_ref, rb_ref, sbp_ref, wo_ref, ws_ref,
             drb_ref, dsbp_ref, dar_ref, das_ref, dsb_ref, acc):
        j = pl.program_id(1)
        dm = _dot_nt(dy_ref[...], wo_ref[...])
        sr = _sigmoid(ar_ref[...].astype(F32))
        ss = _sigmoid(as_ref[...].astype(F32))
        dsbp = (dm * ss).astype(BF)
        drb_ref[...] = (dm * sr).astype(BF)
        dsbp_ref[...] = dsbp
        dar_ref[...] = (dm * rb_ref[...].astype(F32) * sr * (1.0 - sr)).astype(BF)
        das_ref[...] = (dm * sbp_ref[...].astype(F32) * ss * (1.0 - ss)).astype(BF)
        part = _dot_nt(dsbp, ws_ref[...])

        @pl.when(j == 0)
        def _():
            acc[...] = part

        @pl.when(j > 0)
        def _():
            acc[...] += part

        @pl.when(j == D_MODEL // tn - 1)
        def _():
            dsb_ref[...] = acc[...].astype(BF)

    tile = pl.BlockSpec((tm, tn), lambda i, j: (i, j))
    shp = jax.ShapeDtypeStruct((SEQ, D_MODEL), BF)
    return pl.pallas_call(
        body, name="mix_bwd", grid=(SEQ // tm, D_MODEL // tn),
        in_specs=[pl.BlockSpec((tm, D_MODEL), lambda i, j: (i, 0)),
                  pl.BlockSpec((tm, tn), lambda i, j: (i, C_AR // tn + j)),
                  pl.BlockSpec((tm, tn), lambda i, j: (i, C_AS // tn + j)),
                  tile, tile,
                  pl.BlockSpec((tn, D_MODEL), lambda i, j: (j, 0)),
                  pl.BlockSpec((None, 512, tn), lambda i, j: (j, 0, 0))],
        out_specs=[tile, tile, tile, tile, pl.BlockSpec((tm, 512), lambda i, j: (i, 0))],
        out_shape=[shp, shp, shp, shp, jax.ShapeDtypeStruct((SEQ, 512), BF)],
        scratch_shapes=[pltpu.VMEM((tm, 512), F32)],
        compiler_params=_cp(("parallel", "arbitrary")),
    )(dy, proj, proj, rb, sbp, w_out, w_sb4)


def _ret_branch_bwd(drb, proj, o_raw, gn_g, w_ret):
    tm, tn = 512, 256

    def body(d_ref, g_ref, o_ref, gn_ref, w_ref, dret_ref, dgr_ref, st_ref):
        @pl.when(pl.program_id(1) == 0)
        def _():
            st_ref[...] = jnp.zeros_like(st_ref)

        dretg = _dot_nt(d_ref[...], w_ref[...])
        for gi in range(tn // 128):
            cols = slice(128 * gi, 128 * (gi + 1))
            o = o_ref[:, cols]
            d = o - _rowmean(o)
            rstd = lax.rsqrt(_rowmean(d * d) + EPS)
            nh = d * rstd
            gain = gn_ref[:, cols]
            gr = g_ref[:, cols].astype(F32)
            sg = _sigmoid(gr)
            dg = dretg[:, cols]
            dgn = dg * gr * sg
            dnh = dgn * gain
            dgr_ref[:, cols] = (dg * nh * gain * sg * (1.0 + gr * (1.0 - sg))).astype(BF)
            dret_ref[:, cols] = (rstd * (dnh - _rowmean(dnh) - nh * _rowmean(dnh * nh))).astype(BF)
            st_ref[0:1, cols] += _colsum(dgn * nh)

    tile = pl.BlockSpec((tm, tn), lambda j, i: (i, j))
    shp = jax.ShapeDtypeStruct((SEQ, D_MODEL), BF)
    return pl.pallas_call(
        body, name="ret_branch_bwd", grid=(D_MODEL // tn, SEQ // tm),
        in_specs=[pl.BlockSpec((tm, D_MODEL), lambda j, i: (i, 0)),
                  pl.BlockSpec((tm, tn), lambda j, i: (i, C_GR // tn + j)),
                  tile, pl.BlockSpec((1, tn), lambda j, i: (0, j)),
                  pl.BlockSpec((tn, D_MODEL), lambda j, i: (j, 0))],
        out_specs=[tile, tile, pl.BlockSpec((8, tn), lambda j, i: (0, j))],
        out_shape=[shp, shp, jax.ShapeDtypeStruct((8, D_MODEL), F32)],
        compiler_params=_cp(("parallel", "arbitrary")),
    )(drb, proj, o_raw, gn_g, w_ret)


def _in_proj_bwd(dproj, x, dh1, vecs, w_in4):
    tm = 512
    wc = w_in4.shape[2]

    def body(dp_ref, x_ref, dh1_ref, vec_ref, w_ref, dx_ref, st_ref, acc):
        kk = pl.program_id(1)

        @pl.when((pl.program_id(0) == 0) & (kk == 0))
        def _():
            st_ref[...] = jnp.zeros_like(st_ref)

        part = _dot_nt(dp_ref[...], w_ref[...])

        @pl.when(kk == 0)
        def _():
            acc[...] = part

        @pl.when(kk > 0)
        def _():
            acc[...] += part

        @pl.when(kk == N_CHIPS - 1)
        def _():
            dh = acc[...]
            xx = x_ref[...]
            r1 = _rms(xx)
            xn = xx * r1
            g1, sc1 = _row(vec_ref, V_G1), _row(vec_ref, V_SC1)
            dxn = dh * g1 * (1.0 + sc1)
            dx_ref[...] = dh1_ref[...] + r1 * (dxn - xn * _rowmean(dxn * xn))
            st_ref[0:1, :] += _colsum(dh)
            st_ref[1:2, :] += _colsum(dh * xn * g1)
            st_ref[2:3, :] += _colsum(dh * xn * (1.0 + sc1))

    row = pl.BlockSpec((tm, D_MODEL), lambda i, k: (i, 0))
    return pl.pallas_call(
        body, name="in_proj_bwd", grid=(SEQ // tm, N_CHIPS),
        in_specs=[pl.BlockSpec((tm, wc), lambda i, k: (i, k)), row, row,
                  pl.BlockSpec((16, D_MODEL), lambda i, k: (0, 0)),
                  pl.BlockSpec((None, D_MODEL, wc), lambda i, k: (k, 0, 0))],
        out_specs=[row, pl.BlockSpec((8, D_MODEL), lambda i, k: (0, 0))],
        out_shape=[jax.ShapeDtypeStruct((SEQ, D_MODEL), F32), jax.ShapeDtypeStruct((8, D_MODEL), F32)],
        scratch_shapes=[pltpu.VMEM((tm, D_MODEL), F32)],
        compiler_params=_cp(("arbitrary", "arbitrary")),
    )(dproj, x, dh1, vecs, w_in4)


def _weight_grad(a, b, ta, tb, col_sharded, name):
    ka, nb_ = a.shape[1], b.shape[1]

    def body(a_ref, b_ref, o_ref):
        o_ref[...] = _dot_tn(a_ref[...], b_ref[...])

    if col_sharded:
        per = nb_ // N_CHIPS // tb
        out_shape = jax.ShapeDtypeStruct((N_CHIPS, ka, nb_ // N_CHIPS), F32)
        out_spec = pl.BlockSpec((None, ta, tb), lambda i, j: (j // per, i, j % per))
    else:
        per = ka // N_CHIPS // ta
        out_shape = jax.ShapeDtypeStruct((N_CHIPS, ka // N_CHIPS, nb_), F32)
        out_spec = pl.BlockSpec((None, ta, tb), lambda i, j: (i // per, i % per, j))
    return pl.pallas_call(
        body, name=name, grid=(ka // ta, nb_ // tb),
        in_specs=[pl.BlockSpec((SEQ, ta), lambda i, j: (0, i)), pl.BlockSpec((SEQ, tb), lambda i, j: (0, j))],
        out_specs=out_spec, out_shape=out_shape,
        compiler_params=_cp(("parallel", "parallel")),
    )(a, b)


def _rope_constants():
    freq = np.float32(ROPE_BASE) ** (-np.arange(0, 64, 2, dtype=np.float32) / np.float32(64))
    inv = np.tile(freq.astype(np.float32), 4).reshape(1, 128)
    sign = np.tile(np.concatenate([-np.ones(32, np.float32), np.ones(32, np.float32)]), 2).reshape(1, 128)
    return jnp.asarray(inv), jnp.asarray(sign)


def _log_gamma():
    return jnp.asarray(np.log1p(-(2.0 ** (-5.0 - np.arange(8, dtype=np.float64)))).astype(np.float32))


def _local_step(x, target, pos_col, vecs, gn_g, w_in4, w_ret, w_sb4, w_out, w_ff14, w_ff2):
    inv_freq, sign = _rope_constants()
    lg = _log_gamma()
    cos, sin_s = _rope_tables(pos_col, inv_freq, sign)
    h, proj = _ln_proj(x, vecs, w_in4)
    o_raw, retg, states = _ret_fwd(proj, cos, sin_s, gn_g, lg)
    sb, tot = _sb_fwd(proj)
    mixed, rb, sbp = _mix(retg, sb, proj, w_ret, w_sb4)
    y, h1, h2 = _out_proj(mixed, x, vecs, w_out)
    u, act = _ffn_up(h2, w_ff14)
    dout, df, st_a = _ffn_down_loss(act, h1, target, vecs, w_ff2)

    du = _ffn_down_bwd(df, u, w_ff2)
    g_ff2 = _weight_grad(act, df, 512, 1024, False, "grad_w_ff2")
    dh1, dy, st_b = _ffn_up_bwd(du, h1, y, dout, vecs, w_ff14)
    g_ff1 = _weight_grad(h2, du, 512, 1024, True, "grad_w_ff1")
    drb, dsbp, da_r, da_s, dsb = _mix_bwd(dy, proj, rb, sbp, w_out, w_sb4)
    g_out = _weight_grad(mixed, dy, 256, 1024, False, "grad_w_out")
    dret, dg_r, st_c = _ret_branch_bwd(drb, proj, o_raw, gn_g, w_ret)
    g_ret = _weight_grad(retg, drb, 256, 1024, False, "grad_w_ret")
    g_sb = _weight_grad(sb, dsbp, 512, 256, True, "grad_w_sb")
    dq_r, dk_r, dv_r = _ret_bwd(proj, cos, sin_s, dret, states, lg)
    dq_s, dk_s, dv_s = _sb_bwd(proj, dsb, tot)
    dproj = jnp.concatenate([dq_r, dk_r, dv_r, dg_r, dq_s, dk_s, dv_s, da_r, da_s], axis=1)
    dx, st_d = _in_proj_bwd(dproj, x, dh1, vecs, w_in4)
    g_in = _weight_grad(h, dproj, 512, 1664, True, "grad_w_in")

    payload = jnp.concatenate([
        st_d[0:2], st_b[3:4], st_b[0:2], st_a[0:1],
        st_d[2:3], st_b[4:5], st_b[2:3], st_a[1:2],
        st_c[0:1], st_a[2:3]], axis=0)
    return dx, (g_in, g_ret, g_sb, g_out, g_ff1, g_ff2), payload


def kernel(x, c, positions, ada_w, ada_b, pre_mix_g, post_mix_g, pre_ffn_g, post_ffn_g, w_in, ret_gn_g, w_ret_branch, w_sb_branch, w_out, w_ff1, w_ff2, loss_target, m_ada_w, m_ada_b, m_pre_mix_g, m_post_mix_g, m_pre_ffn_g, m_post_ffn_g, m_w_in, m_ret_gn_g, m_w_ret_branch, m_w_sb_branch, m_w_out, m_w_ff1, m_w_ff2, v_ada_w, v_ada_b, v_pre_mix_g, v_post_mix_g, v_pre_ffn_g, v_post_ffn_g, v_w_in, v_ret_gn_g, v_w_ret_branch, v_w_sb_branch, v_w_out, v_w_ff1, v_w_ff2):
    big = [w_in, w_ret_branch, w_sb_branch, w_out, w_ff1, w_ff2]
    big_m = [m_w_in, m_w_ret_branch, m_w_sb_branch, m_w_out, m_w_ff1, m_w_ff2]
    big_v = [v_w_in, v_w_ret_branch, v_w_sb_branch, v_w_out, v_w_ff1, v_w_ff2]
    names = ["w_in", "w_ret", "w_sb", "w_out", "w_ff1", "w_ff2"]

    c_all, mod4 = _mod_exchange(c, ada_w[0], ada_b.reshape(N_CHIPS, -1))
    mod = mod4.reshape(6, D_MODEL)
    vecs = jnp.concatenate([mod, pre_mix_g, post_mix_g, pre_ffn_g, post_ffn_g,
                            jnp.zeros((6, D_MODEL), F32)], axis=0)
    cidx = lax.axis_index("c").astype(jnp.int32).reshape(1)
    kidx = (2 * lax.axis_index("x") + lax.axis_index("y")).astype(jnp.int32).reshape(1)
    w_in4, w_ret4, w_sb4, w_out4, w_ff14, w_ff24 = _gather_weights(
        [_cast_bf16(w[0], kidx, "cast_" + nm) for w, nm in zip(big, names)])

    dx, grads, payload = _local_step(
        x[0], loss_target[0], positions.reshape(SEQ, 1), vecs, ret_gn_g,
        w_in4, w_ret4.reshape(D_MODEL, D_MODEL), w_sb4, w_out4.reshape(D_MODEL, D_MODEL),
        w_ff14, w_ff24.reshape(D_FF, D_MODEL))

    split = [g.reshape(N_CHIPS, 2, g.shape[1] // 2, g.shape[2]) for g in grads]
    from_sib = _pair_send_halves(split)
    pair_sums = [_pair_add(g, r, cidx, "pair_add_" + nm) for g, r, nm in zip(split, from_sib, names)]
    parts = _chip_exchange(pair_sums)
    full = _pair_exchange([_chip_add(o, p, kidx, cidx, "chip_add_" + nm)
                           for o, p, nm in zip(pair_sums, parts, names)])
    big_out = [_adamw(w[0], m[0], v[0], g.reshape(w.shape[1:]), "adamw_" + nm)
               for w, m, v, g, nm in zip(big, big_m, big_v, full, names)]

    def table(b6, g5):
        return jnp.concatenate([b6.reshape(6, D_MODEL)] + g5 + [jnp.zeros((5, D_MODEL), F32)], axis=0)

    wsm = table(ada_b, [pre_mix_g, post_mix_g, pre_ffn_g, post_ffn_g, ret_gn_g])
    msm = table(m_ada_b, [m_pre_mix_g, m_post_mix_g, m_pre_ffn_g, m_post_ffn_g, m_ret_gn_g])
    vsm = table(v_ada_b, [v_pre_mix_g, v_post_mix_g, v_pre_ffn_g, v_post_ffn_g, v_ret_gn_g])
    g_ada, gsm, dsm, mosm, vosm, loss = _small_exchange(
        payload.reshape(N_PAY, 1, D_MODEL), c_all, wsm, msm, vsm)
    ada_out = _adamw(ada_w[0], m_ada_w[0], v_ada_w[0], g_ada, "adamw_ada_w")

    def unpack(tab):
        return [tab[0:6].reshape(1, 6 * D_MODEL)] + [tab[6 + r:7 + r] for r in range(5)]

    def ordered(which):
        sm = unpack([gsm, dsm, mosm, vosm][which])
        bg = [o[which][None] for o in big_out]
        return [ada_out[which][None], sm[0], sm[1], sm[2], sm[3], sm[4], bg[0], sm[5]] + bg[1:]

    return (loss.reshape(()), dx[None], *ordered(0), *ordered(1), *ordered(2), *ordered(3))
```

```python
import functools

import numpy as np
import jax
import jax.numpy as jnp
from jax import lax
from jax.experimental import pallas as pl
from jax.experimental.pallas import tpu as pltpu

SEQ = 2048
D_MODEL = 1024
D_IN = 6656
D_FF = 4096
N_CHIPS = 4
EPS = 1e-6
ROPE_BASE = 10000.0
RET_BLOCK = 256
RET_CHUNK_SHIFT = 6
SB_BLOCK = 256
QK_SCALE = 0.125
N_PAIRS = 4

ADAM_LR = 0.001
ADAM_B1 = 0.9
ADAM_B2 = 0.999
ADAM_EPS = 1e-08
ADAM_WD = 0.01
ADAM_STEP = 10

BF = jnp.bfloat16
F32 = jnp.float32
MESH = pl.DeviceIdType.MESH
VMEM_LIMIT = 56 * 1024 * 1024
ANY = pl.BlockSpec(memory_space=pl.ANY)

C_QR, C_KR, C_VR, C_GR, C_QS, C_KS, C_VS, C_AR, C_AS = 0, 512, 1024, 2048, 3072, 3584, 4096, 4608, 5632

V_SH1, V_SC1, V_GT1, V_SH2, V_SC2, V_GT2, V_G1, V_G2, V_G3, V_G4 = range(10)
P_DSH1, P_DSC1, P_DGT1, P_DSH2, P_DSC2, P_DGT2, P_DG1, P_DG2, P_DG3, P_DG4, P_DGN, P_LOSS = range(12)
N_PAY = 12


def _cp(sem=None, **kw):
    if sem is not None:
        kw["dimension_semantics"] = sem
    return pltpu.CompilerParams(vmem_limit_bytes=VMEM_LIMIT, **kw)


def _dot(a, b):
    return jnp.dot(a, b, preferred_element_type=F32)


def _dot_nt(a, b):
    return lax.dot_general(a, b, (((1,), (1,)), ((), ())), preferred_element_type=F32)


def _dot_tn(a, b):
    return lax.dot_general(a, b, (((0,), (0,)), ((), ())), preferred_element_type=F32)


def _row(ref, i):
    return ref[i:i + 1, :]


def _rms(v):
    return lax.rsqrt(jnp.mean(v * v, axis=1, keepdims=True) + EPS)


def _colsum(v):
    return jnp.sum(v, axis=0, keepdims=True)


def _rowmean(v):
    return jnp.mean(v, axis=1, keepdims=True)


def _sigmoid(v):
    return 1.0 / (1.0 + jnp.exp(-v))


def _cast_bf16(w, kidx, name):
    rows, cols = w.shape
    tr = min(rows, 256)

    def body(k_ref, w_ref, o_ref):
        o_ref[...] = w_ref[...].astype(BF)

    return pl.pallas_call(
        body, name=name,
        grid_spec=pltpu.PrefetchScalarGridSpec(
            num_scalar_prefetch=1, grid=(rows // tr,),
            in_specs=[pl.BlockSpec((tr, cols), lambda i, k_ref: (i, 0))],
            out_specs=pl.BlockSpec((None, tr, cols), lambda i, k_ref: (k_ref[0], i, 0))),
        out_shape=jax.ShapeDtypeStruct((N_CHIPS, rows, cols), BF),
        compiler_params=_cp(("parallel",)),
    )(kidx, w)


def _adamw_math(w, g, m, v):
    m = ADAM_B1 * m + (1.0 - ADAM_B1) * g
    v = ADAM_B2 * v + (1.0 - ADAM_B2) * (g * g)
    m_hat = m / (1.0 - ADAM_B1 ** ADAM_STEP)
    v_hat = v / (1.0 - ADAM_B2 ** ADAM_STEP)
    delta = -ADAM_LR * (m_hat / (jnp.sqrt(v_hat) + ADAM_EPS) + ADAM_WD * w)
    return delta, m, v


def _adamw(w, m, v, g, name):
    rows, cols = w.shape
    tr = min(rows, 128)

    def body(w_ref, m_ref, v_ref, g_ref, go_ref, d_ref, mo_ref, vo_ref):
        gg = g_ref[...]
        d, mm, vv = _adamw_math(w_ref[...], gg, m_ref[...], v_ref[...])
        go_ref[...] = gg
        d_ref[...] = d
        mo_ref[...] = mm
        vo_ref[...] = vv

    spec = pl.BlockSpec((tr, cols), lambda i: (i, 0))
    shp = jax.ShapeDtypeStruct((rows, cols), F32)
    return pl.pallas_call(
        body, name=name, grid=(rows // tr,),
        in_specs=[spec] * 4, out_specs=[spec] * 4, out_shape=[shp] * 4,
        compiler_params=_cp(("parallel",)),
    )(w, m, v, g)


def _place():
    x, y, c = lax.axis_index("x"), lax.axis_index("y"), lax.axis_index("c")
    return x, y, c


HBM = pl.BlockSpec(memory_space=pltpu.HBM)
SEM = pl.BlockSpec(memory_space=pltpu.SEMAPHORE)
EFFECT = pltpu.SideEffectType.DATAFLOW_SIDE_EFFECTING


def _tie(token, value):
    return lax.optimization_barrier((token, value))[1]


def _split_call(name, bufs, run, old=None, after=None, new=0):
    nb = len(bufs)
    n_in = nb + (3 if old is not None else 0)

    def body(*refs):
        old_sems = (refs[nb], refs[nb + 1]) if old is not None else None
        new_sems = (refs[n_in], refs[n_in + 1]) if new else None
        run(refs[:nb], old_sems, new_sems)
        if new:
            refs[-1][...] = jnp.zeros_like(refs[-1])

    in_specs = [HBM] * nb + ([SEM, SEM, ANY] if old is not None else [])
    out_shape = [pltpu.SemaphoreType.DMA((new,))] * 2 if new else []
    out_specs = [SEM, SEM] if new else []
    out_shape += [pltpu.HBM(b.shape, b.dtype) for b in bufs]
    out_specs += [HBM] * nb
    if new:
        out_shape.append(jax.ShapeDtypeStruct((8, 128), F32))
        out_specs.append(pl.BlockSpec(memory_space=pltpu.VMEM))
    first = 2 if new else 0
    args = [pltpu.with_memory_space_constraint(b, pltpu.HBM) for b in bufs]
    if old is not None:
        args += [old[0], old[1], after]
    outs = pl.pallas_call(
        body, name=name, in_specs=tuple(in_specs), out_specs=tuple(out_specs), out_shape=tuple(out_shape),
        input_output_aliases={i: i + first for i in range(nb)},
        compiler_params=pltpu.CompilerParams(has_side_effects=EFFECT),
    )(*args)
    thru = list(outs[first:first + nb])
    if new:
        return thru, (outs[0], outs[1]), outs[-1]
    return thru, None, None


def _remote(part_src, part_dst, sems, i, to):
    return pltpu.make_async_remote_copy(src_ref=part_src, dst_ref=part_dst, send_sem=sems[0].at[i],
                                        recv_sem=sems[1].at[i], device_id=to, device_id_type=MESH)


def _other_chips(x, y):
    return [(1 - x, y), (x, 1 - y), (1 - x, 1 - y)]


def _gather_start(name, bufs):
    def run(refs, old, new):
        x, y, c = _place()
        k = 2 * x + y
        for w, ref in enumerate(refs):
            rh = bufs[w].shape[1] // 2
            part = ref.at[k, pl.ds(c * rh, rh)]
            for j, (cx, cy) in enumerate(_other_chips(x, y)):
                _remote(part, part, new, 3 * w + j, (cx, cy, c)).start()

    return _split_call(name, bufs, run, new=3 * len(bufs))


def _gather_pass(name, bufs, sems, after):
    def run(refs, old, new):
        x, y, c = _place()
        k = 2 * x + y
        sib = (x, y, 1 - c)
        for w, ref in enumerate(refs):
            rh = bufs[w].shape[1] // 2
            for j, (cx, cy) in enumerate(_other_chips(x, y)):
                land = ref.at[2 * cx + cy, pl.ds(c * rh, rh)]
                _remote(land, land, old, 3 * w + j, (cx, cy, c)).wait_recv()
                _remote(land, land, new, 3 * w + j, sib).start()
        for w, ref in enumerate(refs):
            rh = bufs[w].shape[1] // 2
            part = ref.at[k, pl.ds(c * rh, rh)]
            for j, (cx, cy) in enumerate(_other_chips(x, y)):
                _remote(part, part, old, 3 * w + j, (cx, cy, c)).wait_send()

    return _split_call(name, bufs, run, old=sems, after=after, new=3 * len(bufs))


def _gather_finish(name, bufs, sems, after):
    def run(refs, old, new):
        x, y, c = _place()
        sib = (x, y, 1 - c)
        for w, ref in enumerate(refs):
            rh = bufs[w].shape[1] // 2
            for j, (cx, cy) in enumerate(_other_chips(x, y)):
                sent = ref.at[2 * cx + cy, pl.ds(c * rh, rh)]
                _remote(sent, sent, old, 3 * w + j, sib).wait_send()
                land = ref.at[2 * cx + cy, pl.ds((1 - c) * rh, rh)]
                _remote(land, land, old, 3 * w + j, sib).wait_recv()

    return _split_call(name, bufs, run, old=sems, after=after)[0]


def _pair_send_start(name, grads):
    n = len(grads)
    lands = [lax.empty((N_CHIPS,) + g.shape[2:], F32) for g in grads]

    def run(refs, old, new):
        x, y, c = _place()
        for w in range(n):
            _remote(refs[w].at[:, 1 - c], refs[n + w], new, w, (x, y, 1 - c)).start()

    return _split_call(name, list(grads) + lands, run, new=n)


def _pair_send_wait(name, bufs, sems, after):
    n = len(bufs) // 2

    def run(refs, old, new):
        x, y, c = _place()
        for w in range(n):
            cp = _remote(refs[w].at[:, 1 - c], refs[n + w], old, w, (x, y, 1 - c))
            cp.wait_send()
            cp.wait_recv()

    thru = _split_call(name, bufs, run, old=sems, after=after)[0]
    return thru[:n], thru[n:]


def _pair_add(g, recv, cidx, name):
    _, _, rh, cols = g.shape
    tr = min(rh, 256)

    def body(c_ref, g_ref, r_ref, o_ref):
        o_ref[...] = (g_ref[...] + r_ref[...]).astype(BF)

    return pl.pallas_call(
        body, name=name,
        grid_spec=pltpu.PrefetchScalarGridSpec(
            num_scalar_prefetch=1, grid=(N_CHIPS, rh // tr),
            in_specs=[pl.BlockSpec((None, None, tr, cols), lambda s, i, c_ref: (s, c_ref[0], i, 0)),
                      pl.BlockSpec((None, tr, cols), lambda s, i, c_ref: (s, i, 0))],
            out_specs=pl.BlockSpec((None, tr, cols), lambda s, i, c_ref: (s, i, 0))),
        out_shape=jax.ShapeDtypeStruct((N_CHIPS, rh, cols), BF),
        compiler_params=_cp(("parallel", "parallel")),
    )(cidx, g, recv)


def _chip_send_start(name, sums):
    n = len(sums)
    lands = [lax.empty((3,) + s.shape[1:], BF) for s in sums]

    def run(refs, old, new):
        x, y, c = _place()
        for w in range(n):
            for j, (cx, cy) in enumerate(_other_chips(x, y)):
                _remote(refs[w].at[2 * cx + cy], refs[n + w].at[j], new, 3 * w + j, (cx, cy, c)).start()

    return _split_call(name, list(sums) + lands, run, new=3 * n)


def _chip_send_wait(name, bufs, sems, after):
    n = len(bufs) // 2

    def run(refs, old, new):
        x, y, c = _place()
        for w in range(n):
            for j, (cx, cy) in enumerate(_other_chips(x, y)):
                cp = _remote(refs[w].at[2 * cx + cy], refs[n + w].at[j], old, 3 * w + j, (cx, cy, c))
                cp.wait_send()
                cp.wait_recv()

    thru = _split_call(name, bufs, run, old=sems, after=after)[0]
    return thru[:n], thru[n:]


def _chip_add(own, parts, kidx, cidx, name):
    _, rh, cols = parts.shape
    tr = min(rh, 256)

    def body(k_ref, c_ref, own_ref, p_ref, o_ref):
        acc = own_ref[...].astype(F32)
        for s in range(3):
            acc = acc + p_ref[s].astype(F32)
        o_ref[...] = acc

    return pl.pallas_call(
        body, name=name,
        grid_spec=pltpu.PrefetchScalarGridSpec(
            num_scalar_prefetch=2, grid=(rh // tr,),
            in_specs=[pl.BlockSpec((None, tr, cols), lambda i, k_ref, c_ref: (k_ref[0], i, 0)),
                      pl.BlockSpec((3, tr, cols), lambda i, k_ref, c_ref: (0, i, 0))],
            out_specs=pl.BlockSpec((None, tr, cols), lambda i, k_ref, c_ref: (c_ref[0], i, 0))),
        out_shape=jax.ShapeDtypeStruct((2, rh, cols), F32),
        compiler_params=_cp(("parallel",)),
    )(kidx, cidx, own, parts)


def _pair_swap_start(name, bufs):
    def run(refs, old, new):
        x, y, c = _place()
        for w, ref in enumerate(refs):
            _remote(ref.at[c], ref.at[c], new, w, (x, y, 1 - c)).start()

    return _split_call(name, bufs, run, new=len(bufs))


def _pair_swap_wait(name, bufs, sems, after):
    def run(refs, old, new):
        x, y, c = _place()
        for w, ref in enumerate(refs):
            _remote(ref.at[c], ref.at[c], old, w, (x, y, 1 - c)).wait_send()
            _remote(ref.at[1 - c], ref.at[1 - c], old, w, (x, y, 1 - c)).wait_recv()

    return _split_call(name, bufs, run, old=sems, after=after)[0]


def _peers(x, y, c):
    out = []
    for code in range(1, 8):
        fx, fy, fc = (code >> 2) & 1, (code >> 1) & 1, code & 1
        px = 1 - x if fx else x
        py = 1 - y if fy else y
        pc = 1 - c if fc else c
        out.append((code, (px, py, pc)))
    return out


def _mod_exchange(c_row, ada_w, ada_b4):
    ncol = ada_w.shape[1]

    def body(c_ref, w_ref, b_ref, call_ref, mod_ref, part_ref, send_sems, recv_sems):
        x, y, c = _place()
        k = 2 * x + y
        me = 4 * x + 2 * y + c
        call_ref[pl.ds(me, 1), :] = c_ref[...]
        sends = []
        for code, peer in _peers(x, y, c):
            cp = pltpu.make_async_remote_copy(
                src_ref=c_ref, dst_ref=call_ref.at[pl.ds(me, 1), :],
                send_sem=send_sems.at[code], recv_sem=recv_sems.at[code],
                device_id=peer, device_id_type=MESH)
            cp.start()
            sends.append(cp)
        for code, (px, py, pc) in _peers(x, y, c):
            land = call_ref.at[pl.ds(4 * px + 2 * py + pc, 1), :]
            pltpu.make_async_remote_copy(
                src_ref=land, dst_ref=land, send_sem=send_sems.at[code], recv_sem=recv_sems.at[code],
                device_id=(px, py, pc), device_id_type=MESH).wait_recv()
        call = call_ref[...]
        act = call * _sigmoid(call)
        part = jnp.dot(act, w_ref[...], preferred_element_type=F32,
                       precision=lax.Precision.HIGHEST) + b_ref[pl.ds(k, 1), :]
        part_ref[...] = part
        mod_ref[pl.ds(k, 1), :] = part_ref[pl.ds(me, 1), :]
        chips = [(8 + j, peer) for j, (code, peer) in enumerate(_peers(x, y, c)) if code in (2, 4, 6)]
        for slot, (px, py, pc) in chips:
            cp = pltpu.make_async_remote_copy(
                src_ref=part_ref.at[pl.ds(4 * px + 2 * py + pc, 1), :], dst_ref=mod_ref.at[pl.ds(k, 1), :],
                send_sem=send_sems.at[slot], recv_sem=recv_sems.at[slot],
                device_id=(px, py, pc), device_id_type=MESH)
            cp.start()
            sends.append(cp)
        for slot, (px, py, pc) in chips:
            land = mod_ref.at[pl.ds(2 * px + py, 1), :]
            pltpu.make_async_remote_copy(
                src_ref=land, dst_ref=land, send_sem=send_sems.at[slot], recv_sem=recv_sems.at[slot],
                device_id=(px, py, pc), device_id_type=MESH).wait_recv()
        for cp in sends:
            cp.wait_send()

    vm = pl.BlockSpec(memory_space=pltpu.VMEM)
    return pl.pallas_call(
        body, name="mod_exchange",
        in_specs=[vm, vm, vm], out_specs=[vm, vm],
        out_shape=[jax.ShapeDtypeStruct((8, D_MODEL), F32), jax.ShapeDtypeStruct((N_CHIPS, ncol), F32)],
        scratch_shapes=[pltpu.VMEM((8, ncol), F32), pltpu.SemaphoreType.DMA((16,)),
                        pltpu.SemaphoreType.DMA((16,))],
        compiler_params=_cp(),
    )(c_row, ada_w, ada_b4)


def _small_exchange(payload, c_all, wsm, msm, vsm):
    ncol = 6 * D_MODEL // N_CHIPS

    def body(p_ref, call_ref, w_ref, m_ref, v_ref, gw_ref, g_ref, d_ref, mo_ref, vo_ref, loss_ref,
             all_ref, dm_ref, send_sems, recv_sems):
        x, y, c = _place()
        k = 2 * x + y
        me = 4 * x + 2 * y + c
        all_ref[:, pl.ds(me, 1), :] = p_ref[...]
        sends = []
        for code, peer in _peers(x, y, c):
            cp = pltpu.make_async_remote_copy(
                src_ref=p_ref, dst_ref=all_ref.at[:, pl.ds(me, 1), :],
                send_sem=send_sems.at[code], recv_sem=recv_sems.at[code],
                device_id=peer, device_id_type=MESH)
            cp.start()
            sends.append(cp)
        for code, (px, py, pc) in _peers(x, y, c):
            land = all_ref.at[:, pl.ds(4 * px + 2 * py + pc, 1), :]
            pltpu.make_async_remote_copy(
                src_ref=land, dst_ref=land, send_sem=send_sems.at[code], recv_sem=recv_sems.at[code],
                device_id=(px, py, pc), device_id_type=MESH).wait_recv()
        for cp in sends:
            cp.wait_send()
        tot = [_colsum(all_ref[r]) for r in range(N_PAY)]
        loss_ref[...] = jnp.sum(tot[P_LOSS], axis=1, keepdims=True)
        g_ref[...] = jnp.zeros_like(g_ref)
        for r in range(P_LOSS):
            g_ref[r:r + 1, :] = tot[r]
        g = g_ref[...]
        d, mm, vv = _adamw_math(w_ref[...], g, m_ref[...], v_ref[...])
        d_ref[...] = d
        mo_ref[...] = mm
        vo_ref[...] = vv
        half = D_MODEL // 2
        for kk in range(N_CHIPS):
            @pl.when(k == kk)
            def _():
                r0 = 3 * (kk // 2)
                if kk % 2 == 0:
                    dm_ref[:, :D_MODEL] = all_ref[r0]
                    dm_ref[:, D_MODEL:] = all_ref[r0 + 1][:, :half]
                else:
                    dm_ref[:, :half] = all_ref[r0 + 1][:, half:]
                    dm_ref[:, half:] = all_ref[r0 + 2]
        call = call_ref[...]
        act = call * _sigmoid(call)
        gw_ref[...] = lax.dot_general(act, dm_ref[...], (((0,), (0,)), ((), ())),
                                      preferred_element_type=F32, precision=lax.Precision.HIGHEST)

    vm = pl.BlockSpec(memory_space=pltpu.VMEM)
    small = jax.ShapeDtypeStruct((16, D_MODEL), F32)
    return pl.pallas_call(
        body, name="small_exchange",
        in_specs=[vm] * 5, out_specs=[vm] * 6,
        out_shape=[jax.ShapeDtypeStruct((D_MODEL, ncol), F32), small, small, small, small,
                   jax.ShapeDtypeStruct((1, 1), F32)],
        scratch_shapes=[pltpu.VMEM((N_PAY, 8, D_MODEL), F32), pltpu.VMEM((8, ncol), F32),
                        pltpu.SemaphoreType.DMA((8,)), pltpu.SemaphoreType.DMA((8,))],
        compiler_params=_cp(),
    )(payload, c_all, wsm, msm, vsm)


def _rope_tables(pos_col, inv_freq, sign):
    def body(p_ref, f_ref, s_ref, cos_ref, sin_ref):
        ang = p_ref[...].astype(F32) * f_ref[...]
        cos_ref[...] = jnp.cos(ang)
        sin_ref[...] = jnp.sin(ang) * s_ref[...]

    tr = 512
    shp = jax.ShapeDtypeStruct((SEQ, 128), F32)
    return pl.pallas_call(
        body, name="rope_tables", grid=(SEQ // tr,),
        in_specs=[pl.BlockSpec((tr, 1), lambda i: (i, 0)), pl.BlockSpec((1, 128), lambda i: (0, 0)),
                  pl.BlockSpec((1, 128), lambda i: (0, 0))],
        out_specs=[pl.BlockSpec((tr, 128), lambda i: (i, 0))] * 2, out_shape=[shp, shp],
        compiler_params=_cp(("parallel",)),
    )(pos_col, inv_freq, sign)


def _ln_proj(x, vecs, w_in4):
    tm = 512
    wc = w_in4.shape[2]

    def body(x_ref, vec_ref, w_ref, h_ref, proj_ref):
        @pl.when(pl.program_id(1) == 0)
        def _():
            xx = x_ref[...]
            g = _row(vec_ref, V_G1) * (1.0 + _row(vec_ref, V_SC1))
            h_ref[...] = (xx * _rms(xx) * g + _row(vec_ref, V_SH1)).astype(BF)
        proj_ref[...] = _dot(h_ref[...], w_ref[...]).astype(BF)

    return pl.pallas_call(
        body, name="ln_proj", grid=(SEQ // tm, N_CHIPS),
        in_specs=[pl.BlockSpec((tm, D_MODEL), lambda i, j: (i, 0)),
                  pl.BlockSpec((16, D_MODEL), lambda i, j: (0, 0)),
                  pl.BlockSpec((None, D_MODEL, wc), lambda i, j: (j, 0, 0))],
        out_specs=[pl.BlockSpec((tm, D_MODEL), lambda i, j: (i, 0)),
                   pl.BlockSpec((tm, wc), lambda i, j: (i, j))],
        out_shape=[jax.ShapeDtypeStruct((SEQ, D_MODEL), BF), jax.ShapeDtypeStruct((SEQ, D_IN), BF)],
        compiler_params=_cp(("parallel", "arbitrary")),
    )(x, vecs, w_in4)


def _lane_first(shape):
    lane = lax.broadcasted_iota(jnp.int32, shape, 1)
    return (lane & 32) == 0


def _rot(v, cos, sin_s):
    partner = jnp.where(_lane_first(v.shape), pltpu.roll(v, 96, 1), pltpu.roll(v, 32, 1))
    return v * cos + partner * sin_s


def _rot_t(dv, cos, sin_s):
    t = dv * sin_s
    partner = jnp.where(_lane_first(dv.shape), pltpu.roll(t, 96, 1), pltpu.roll(t, 32, 1))
    return dv * cos + partner


def _ret_masks(lg):
    t = RET_BLOCK
    ii = lax.broadcasted_iota(jnp.int32, (t, t), 0)
    jj = lax.broadcasted_iota(jnp.int32, (t, t), 1)
    dist = jnp.abs(ii - jj).astype(F32)
    future = (jj >> RET_CHUNK_SHIFT) > (ii >> RET_CHUNK_SHIFT)
    mask = jnp.where(future, 0.0, jnp.exp(lg * dist))
    ti = lax.broadcasted_iota(jnp.int32, (t, 1), 0).astype(F32)
    from_start = jnp.exp(lg * (ti + 1.0))
    to_end = jnp.exp(lg * (t - 1.0 - ti))
    whole = jnp.exp(jnp.full((1, 128), lg * t, F32))
    return mask, from_start, to_end, whole


def _head_lanes(shape, hh):
    lane = lax.broadcasted_iota(jnp.int32, shape, 1)
    return (lane >> 6) == hh


def _ret_specs():
    t = RET_BLOCK
    return dict(
        q=lambda f: pl.BlockSpec((t, 128), lambda p, n: (f(n), C_QR // 128 + p)),
        k=lambda f: pl.BlockSpec((t, 128), lambda p, n: (f(n), C_KR // 128 + p)),
        v=lambda f: pl.BlockSpec((t, 256), lambda p, n: (f(n), C_VR // 256 + p)),
        g=lambda f: pl.BlockSpec((t, 256), lambda p, n: (f(n), C_GR // 256 + p)),
        tab=lambda f: pl.BlockSpec((t, 128), lambda p, n: (f(n), 0)),
        wide=lambda f: pl.BlockSpec((t, 256), lambda p, n: (f(n), p)),
        narrow=lambda f: pl.BlockSpec((t, 128), lambda p, n: (f(n), p)),
        state=lambda f: pl.BlockSpec((None, None, 2, 128, 128), lambda p, n: (p, f(n), 0, 0, 0)),
    )


def _ret_fwd(proj, cos, sin_s, gn_g, log_gamma):
    t = RET_BLOCK
    nb = SEQ // t

    def body(lg_ref, q_ref, k_ref, v_ref, g_ref, cos_ref, sin_ref, gn_ref, o_ref, retg_ref, st_ref, state):
        p = pl.program_id(0)

        @pl.when(pl.program_id(1) == 0)
        def _():
            state[...] = jnp.zeros_like(state)

        cos, sn = cos_ref[...], sin_ref[...]
        q = _rot(q_ref[...].astype(F32), cos, sn)
        k = _rot(k_ref[...].astype(F32), cos, sn) * QK_SCALE
        for hh in range(2):
            lg = lg_ref[2 * p + hh]
            mask, from_start, to_end, whole = _ret_masks(lg)
            lanes = _head_lanes(q.shape, hh)
            qm = jnp.where(lanes, q, 0.0)
            km = jnp.where(lanes, k, 0.0)
            vh = v_ref[:, 128 * hh:128 * (hh + 1)]
            sc = _dot_nt(qm.astype(BF), km.astype(BF)) * mask
            st = state[hh]
            st_ref[hh] = st
            o = _dot(sc.astype(BF), vh) + _dot((qm * from_start).astype(BF), st.astype(BF))
            state[hh] = whole * st + _dot_tn((km * to_end).astype(BF), vh)
            d = o - _rowmean(o)
            nh = d * lax.rsqrt(_rowmean(d * d) + EPS)
            gr = g_ref[:, 128 * hh:128 * (hh + 1)].astype(F32)
            o_ref[:, 128 * hh:128 * (hh + 1)] = o
            retg_ref[:, 128 * hh:128 * (hh + 1)] = (
                gr * _sigmoid(gr) * nh * gn_ref[:, 128 * hh:128 * (hh + 1)]).astype(BF)

    sp = _ret_specs()
    ident = lambda n: n
    return pl.pallas_call(
        body, name="ret_fwd", grid=(N_PAIRS, nb),
        in_specs=[pl.BlockSpec(memory_space=pltpu.SMEM), sp["q"](ident), sp["k"](ident), sp["v"](ident),
                  sp["g"](ident), sp["tab"](ident), sp["tab"](ident),
                  pl.BlockSpec((1, 256), lambda p, n: (0, p))],
        out_specs=[sp["wide"](ident), sp["wide"](ident), sp["state"](ident)],
        out_shape=[jax.ShapeDtypeStruct((SEQ, D_MODEL), F32), jax.ShapeDtypeStruct((SEQ, D_MODEL), BF),
                   jax.ShapeDtypeStruct((N_PAIRS, nb, 2, 128, 128), F32)],
        scratch_shapes=[pltpu.VMEM((2, 128, 128), F32)],
        compiler_params=_cp(("parallel", "arbitrary")),
    )(log_gamma, proj, proj, proj, proj, cos, sin_s, gn_g)


def _ret_bwd(proj, cos, sin_s, dret, states, log_gamma):
    t = RET_BLOCK
    nb = SEQ // t

    def body(lg_ref, q_ref, k_ref, v_ref, cos_ref, sin_ref, do_ref, st_ref, dq_ref, dk_ref, dv_ref, dstate):
        p = pl.program_id(0)

        @pl.when(pl.program_id(1) == 0)
        def _():
            dstate[...] = jnp.zeros_like(dstate)

        cos, sn = cos_ref[...], sin_ref[...]
        q = _rot(q_ref[...].astype(F32), cos, sn)
        k = _rot(k_ref[...].astype(F32), cos, sn) * QK_SCALE
        dq_rot = jnp.zeros(q.shape, F32)
        dk_rot = jnp.zeros(q.shape, F32)
        for hh in range(2):
            lg = lg_ref[2 * p + hh]
            mask, from_start, to_end, whole = _ret_masks(lg)
            lanes = _head_lanes(q.shape, hh)
            qm = jnp.where(lanes, q, 0.0)
            km = jnp.where(lanes, k, 0.0)
            qb, kb = qm.astype(BF), km.astype(BF)
            vh = v_ref[:, 128 * hh:128 * (hh + 1)]
            do = do_ref[:, 128 * hh:128 * (hh + 1)]
            sc = (_dot_nt(qb, kb) * mask).astype(BF)
            st = st_ref[hh].astype(BF)
            dst = dstate[hh]
            dstb = dst.astype(BF)
            k_end = (km * to_end).astype(BF)
            q_start = (qm * from_start).astype(BF)
            dv_ref[:, 128 * hh:128 * (hh + 1)] = (_dot_tn(sc, do) + _dot(k_end, dstb)).astype(BF)
            dsc = (_dot_nt(do, vh) * mask).astype(BF)
            dq_h = _dot(dsc, kb) + _dot_nt(do, st) * from_start
            dq_rot = dq_rot + jnp.where(lanes, dq_h, 0.0)
            dk_rot = dk_rot + _dot_tn(dsc, qb) + _dot_nt(vh, dstb) * to_end
            dstate[hh] = whole * dst + _dot_tn(q_start, do)
        dq_ref[...] = _rot_t(dq_rot, cos, sn).astype(BF)
        dk_ref[...] = _rot_t(dk_rot * QK_SCALE, cos, sn).astype(BF)

    sp = _ret_specs()
    rev = lambda n: nb - 1 - n
    return pl.pallas_call(
        body, name="ret_bwd", grid=(N_PAIRS, nb),
        in_specs=[pl.BlockSpec(memory_space=pltpu.SMEM), sp["q"](rev), sp["k"](rev), sp["v"](rev),
                  sp["tab"](rev), sp["tab"](rev), sp["wide"](rev), sp["state"](rev)],
        out_specs=[sp["narrow"](rev), sp["narrow"](rev), sp["wide"](rev)],
        out_shape=[jax.ShapeDtypeStruct((SEQ, 512), BF), jax.ShapeDtypeStruct((SEQ, 512), BF),
                   jax.ShapeDtypeStruct((SEQ, D_MODEL), BF)],
        scratch_shapes=[pltpu.VMEM((2, 128, 128), F32)],
        compiler_params=_cp(("parallel", "arbitrary")),
    )(log_gamma, proj, proj, proj, cos, sin_s, dret, states)


def _stack_heads(v):
    return jnp.concatenate([jnp.where(_head_lanes(v.shape, hh), v, jnp.zeros_like(v)) for hh in range(2)], axis=0)


def _unstack_heads(v):
    t = v.shape[0] // 2
    return jnp.where(_head_lanes((t, v.shape[1]), 0), v[:t], v[t:])


def _sb_masks(t):
    rr = lax.broadcasted_iota(jnp.int32, (t, t), 0)
    cc = lax.broadcasted_iota(jnp.int32, (t, t), 1)
    r2 = lax.broadcasted_iota(jnp.int32, (2 * t, t), 0) & (t - 1)
    c2 = lax.broadcasted_iota(jnp.int32, (2 * t, t), 1)
    return rr, cc, c2 < r2


def _split_dot2(v, tri):
    hi = v.astype(BF)
    lo = (v - hi.astype(F32)).astype(BF)
    both = _dot(jnp.concatenate([hi, lo], axis=0), tri)
    return both[:v.shape[0]] + both[v.shape[0]:]


def _log_one_minus_beta(z):
    return -(jnp.maximum(z, 0.0) + jnp.log(1.0 + jnp.exp(-jnp.abs(z))))


def _sb_fwd(proj):
    t = SB_BLOCK
    nq = SEQ // t

    def body(q_ref, k_ref, v_ref, o_ref, tot_ref, kt_ref):
        i = pl.program_id(1)

        @pl.when(i == 0)
        def _():
            for jj in range(nq):
                kt_ref[jj] = k_ref[jj * t:(jj + 1) * t, :].T

        q2 = _stack_heads((q_ref[...].astype(F32) * QK_SCALE).astype(BF))
        rr, cc, valid = _sb_masks(t)
        later = (rr > cc).astype(BF)

        def tile(j, carry, diagonal):
            acc, run = carry
            z = _dot(q2, kt_ref[j])
            lm = _log_one_minus_beta(z)
            if diagonal:
                lm = jnp.where(valid, lm, 0.0)
            after = _split_dot2(lm, later)
            a = jnp.exp(z + lm + after + run)
            if diagonal:
                a = jnp.where(valid, a, 0.0)
            vb = v_ref[pl.ds(pl.multiple_of(j * t, t), t), :]
            return acc + _dot(a.astype(BF), vb), run + after[:, 0:1] + lm[:, 0:1]

        carry = tile(i, (jnp.zeros((2 * t, 128), F32), jnp.zeros((2 * t, 1), F32)), True)
        acc, run = lax.fori_loop(0, i, lambda s, cr: tile(i - 1 - s, cr, False), carry)
        o_ref[...] = _unstack_heads(acc).astype(BF)
        tot_ref[...] = _unstack_heads(jnp.broadcast_to(run, (2 * t, 128)))

    return pl.pallas_call(
        body, name="sb_fwd", grid=(N_PAIRS, nq),
        in_specs=[pl.BlockSpec((t, 128), lambda p, i: (i, C_QS // 128 + p)),
                  pl.BlockSpec((SEQ, 128), lambda p, i: (0, C_KS // 128 + p)),
                  pl.BlockSpec((SEQ, 128), lambda p, i: (0, C_VS // 128 + p))],
        out_specs=[pl.BlockSpec((t, 128), lambda p, i: (i, p))] * 2,
        out_shape=[jax.ShapeDtypeStruct((SEQ, 512), BF), jax.ShapeDtypeStruct((SEQ, 512), F32)],
        scratch_shapes=[pltpu.VMEM((nq, 128, t), BF)],
        compiler_params=_cp(("parallel", "arbitrary")),
    )(proj, proj, proj)


def _sb_bwd(proj, dsb, tot):
    t = SB_BLOCK
    nq = SEQ // t

    def body(q_ref, k_ref, v_ref, do_ref, tot_ref, dq_ref, dk_ref, dv_ref, kt_ref, vt_ref, dkt_acc, dvt_acc):
        i = pl.program_id(1)

        @pl.when(i == 0)
        def _():
            dkt_acc[...] = jnp.zeros_like(dkt_acc)
            dvt_acc[...] = jnp.zeros_like(dvt_acc)
            for jj in range(nq):
                kt_ref[jj] = k_ref[jj * t:(jj + 1) * t, :].T
                vt_ref[jj] = v_ref[jj * t:(jj + 1) * t, :].T

        q2 = _stack_heads((q_ref[...].astype(F32) * QK_SCALE).astype(BF))
        do2 = _stack_heads(do_ref[...])
        q2t, do2t = q2.T, do2.T
        tots = tot_ref[...]
        total = jnp.concatenate([tots[:, 0:1], tots[:, 64:65]], axis=0)
        rr, cc, valid = _sb_masks(t)
        upto = (rr <= cc).astype(BF)
        before = (rr < cc).astype(BF)

        def tile(j, carry, diagonal):
            dq, run_l, run_g = carry
            z = _dot(q2, kt_ref[j])
            lm = _log_one_minus_beta(z)
            if diagonal:
                lm = jnp.where(valid, lm, 0.0)
            incl = _split_dot2(lm, upto)
            a = jnp.exp(z + lm + (total - (incl + run_l)))
            if diagonal:
                a = jnp.where(valid, a, 0.0)
            g = a * _dot(do2, vt_ref[j])
            excl = _split_dot2(g, before)
            dz = g * jnp.exp(lm) - (excl + run_g) * jnp.exp(z + lm)
            if diagonal:
                dz = jnp.where(valid, dz, 0.0)
            dzb = dz.astype(BF)
            dkt_acc[j] += _dot(q2t, dzb)
            dvt_acc[j] += _dot(do2t, a.astype(BF))
            kb = k_ref[pl.ds(pl.multiple_of(j * t, t), t), :]
            return (dq + _dot(dzb, kb), run_l + incl[:, t - 1:t], run_g + excl[:, t - 1:t] + g[:, t - 1:t])

        zero = jnp.zeros((2 * t, 1), F32)
        carry = lax.fori_loop(0, i, lambda j, cr: tile(j, cr, False), (jnp.zeros((2 * t, 128), F32), zero, zero))
        dq = tile(i, carry, True)[0]
        dq_ref[...] = (_unstack_heads(dq) * QK_SCALE).astype(BF)

        @pl.when(i == nq - 1)
        def _():
            for jj in range(nq):
                dk_ref[jj * t:(jj + 1) * t, :] = dkt_acc[jj].T.astype(BF)
                dv_ref[jj * t:(jj + 1) * t, :] = dvt_acc[jj].T.astype(BF)

    tile_spec = pl.BlockSpec((t, 128), lambda p, i: (i, p))
    col_spec = pl.BlockSpec((SEQ, 128), lambda p, i: (0, p))
    shp = jax.ShapeDtypeStruct((SEQ, 512), BF)
    return pl.pallas_call(
        body, name="sb_bwd", grid=(N_PAIRS, nq),
        in_specs=[pl.BlockSpec((t, 128), lambda p, i: (i, C_QS // 128 + p)),
                  pl.BlockSpec((SEQ, 128), lambda p, i: (0, C_KS // 128 + p)),
                  pl.BlockSpec((SEQ, 128), lambda p, i: (0, C_VS // 128 + p)),
                  tile_spec, tile_spec],
        out_specs=[tile_spec, col_spec, col_spec],
        out_shape=[shp, shp, shp],
        scratch_shapes=[pltpu.VMEM((nq, 128, t), BF), pltpu.VMEM((nq, 128, t), BF),
                        pltpu.VMEM((nq, 128, t), F32), pltpu.VMEM((nq, 128, t), F32)],
        compiler_params=_cp(("parallel", "arbitrary")),
    )(proj, proj, proj, dsb, tot)


def _mix(retg, sb, proj, w_ret, w_sb4):
    tm, tn = 512, 256

    def body(r_ref, s_ref, ar_ref, as_ref, wr_ref, ws_ref, mix_ref, rb_ref, sbp_ref):
        rb = _dot(r_ref[...], wr_ref[...])
        sbp = _dot(s_ref[...], ws_ref[...])
        mix = _sigmoid(ar_ref[...].astype(F32)) * rb + _sigmoid(as_ref[...].astype(F32)) * sbp
        mix_ref[...] = mix.astype(BF)
        rb_ref[...] = rb.astype(BF)
        sbp_ref[...] = sbp.astype(BF)

    out = pl.BlockSpec((tm, tn), lambda j, i: (i, j))
    shp = jax.ShapeDtypeStruct((SEQ, D_MODEL), BF)
    return pl.pallas_call(
        body, name="mix", grid=(D_MODEL // tn, SEQ // tm),
        in_specs=[pl.BlockSpec((tm, D_MODEL), lambda j, i: (i, 0)),
                  pl.BlockSpec((tm, 512), lambda j, i: (i, 0)),
                  pl.BlockSpec((tm, tn), lambda j, i: (i, C_AR // tn + j)),
                  pl.BlockSpec((tm, tn), lambda j, i: (i, C_AS // tn + j)),
                  pl.BlockSpec((D_MODEL, tn), lambda j, i: (0, j)),
                  pl.BlockSpec((None, 512, tn), lambda j, i: (j, 0, 0))],
        out_specs=[out, out, out], out_shape=[shp, shp, shp],
        compiler_params=_cp(("parallel", "parallel")),
    )(retg, sb, proj, proj, w_ret, w_sb4)


def _out_proj(mixed, x, vecs, w_out):
    tm = 256

    def body(m_ref, x_ref, vec_ref, w_ref, y_ref, h1_ref, h2_ref):
        y = _dot(m_ref[...], w_ref[...])
        h1 = x_ref[...] + _row(vec_ref, V_GT1) * (y * _rms(y)) * _row(vec_ref, V_G2)
        g = _row(vec_ref, V_G3) * (1.0 + _row(vec_ref, V_SC2))
        y_ref[...] = y
        h1_ref[...] = h1
        h2_ref[...] = (h1 * _rms(h1) * g + _row(vec_ref, V_SH2)).astype(BF)

    row = pl.BlockSpec((tm, D_MODEL), lambda i: (i, 0))
    f32 = jax.ShapeDtypeStruct((SEQ, D_MODEL), F32)
    return pl.pallas_call(
        body, name="out_proj", grid=(SEQ // tm,),
        in_specs=[row, row, pl.BlockSpec((16, D_MODEL), lambda i: (0, 0)),
                  pl.BlockSpec((D_MODEL, D_MODEL), lambda i: (0, 0))],
        out_specs=[row, row, row],
        out_shape=[f32, f32, jax.ShapeDtypeStruct((SEQ, D_MODEL), BF)],
        compiler_params=_cp(("parallel",)),
    )(mixed, x, vecs, w_out)


def _ffn_up(h2, w_ff14):
    tm = 512

    def body(h_ref, w_ref, u_ref, a_ref):
        u = _dot(h_ref[...], w_ref[...])
        r = jnp.maximum(u, 0.0)
        u_ref[...] = u.astype(BF)
        a_ref[...] = (r * r).astype(BF)

    out = pl.BlockSpec((tm, D_MODEL), lambda j, i: (i, j))
    shp = jax.ShapeDtypeStruct((SEQ, D_FF), BF)
    return pl.pallas_call(
        body, name="ffn_up", grid=(N_CHIPS, SEQ // tm),
        in_specs=[pl.BlockSpec((tm, D_MODEL), lambda j, i: (i, 0)),
                  pl.BlockSpec((None, D_MODEL, D_MODEL), lambda j, i: (j, 0, 0))],
        out_specs=[out, out], out_shape=[shp, shp],
        compiler_params=_cp(("parallel", "parallel")),
    )(h2, w_ff14)


def _ffn_down_loss(act, h1, target, vecs, w_ff2):
    tm = 256

    def body(a_ref, h1_ref, t_ref, vec_ref, w_ref, dout_ref, df_ref, st_ref):
        @pl.when(pl.program_id(0) == 0)
        def _():
            st_ref[...] = jnp.zeros_like(st_ref)

        f = _dot(a_ref[...], w_ref[...])
        r4 = _rms(f)
        fn = f * r4
        gt2, g4 = _row(vec_ref, V_GT2), _row(vec_ref, V_G4)
        diff = h1_ref[...] + gt2 * fn * g4 - t_ref[...]
        dout = diff * (1.0 / D_MODEL)
        dfn = dout * gt2 * g4
        dout_ref[...] = dout
        df_ref[...] = (r4 * (dfn - fn * _rowmean(dfn * fn))).astype(BF)
        st_ref[0:1, :] += _colsum(dout * fn * g4)
        st_ref[1:2, :] += _colsum(dout * gt2 * fn)
        st_ref[2:3, :] += _colsum(diff * diff) * (0.5 / D_MODEL)

    row = pl.BlockSpec((tm, D_MODEL), lambda i: (i, 0))
    return pl.pallas_call(
        body, name="ffn_down_loss", grid=(SEQ // tm,),
        in_specs=[pl.BlockSpec((tm, D_FF), lambda i: (i, 0)), row, row,
                  pl.BlockSpec((16, D_MODEL), lambda i: (0, 0)),
                  pl.BlockSpec((D_FF, D_MODEL), lambda i: (0, 0))],
        out_specs=[row, row, pl.BlockSpec((8, D_MODEL), lambda i: (0, 0))],
        out_shape=[jax.ShapeDtypeStruct((SEQ, D_MODEL), F32), jax.ShapeDtypeStruct((SEQ, D_MODEL), BF),
                   jax.ShapeDtypeStruct((8, D_MODEL), F32)],
        compiler_params=_cp(("arbitrary",)),
    )(act, h1, target, vecs, w_ff2)


def _ffn_down_bwd(df, u, w_ff2):
    tm = 256

    def body(df_ref, u_ref, w_ref, du_ref):
        da = _dot_nt(df_ref[...], w_ref[...])
        du_ref[...] = (da * (2.0 * jnp.maximum(u_ref[...].astype(F32), 0.0))).astype(BF)

    return pl.pallas_call(
        body, name="ffn_down_bwd", grid=(D_FF // 1024, SEQ // tm),
        in_specs=[pl.BlockSpec((tm, D_MODEL), lambda j, i: (i, 0)),
                  pl.BlockSpec((tm, 1024), lambda j, i: (i, j)),
                  pl.BlockSpec((1024, D_MODEL), lambda j, i: (j, 0))],
        out_specs=pl.BlockSpec((tm, 1024), lambda j, i: (i, j)),
        out_shape=jax.ShapeDtypeStruct((SEQ, D_FF), BF),
        compiler_params=_cp(("parallel", "parallel")),
    )(df, u, w_ff2)


def _ffn_up_bwd(du, h1, y, dout, vecs, w_ff14):
    tm = 256

    def body(du_ref, h1_ref, y_ref, dout_ref, vec_ref, w_ref, dh1_ref, dy_ref, st_ref, acc):
        kk = pl.program_id(1)

        @pl.when((pl.program_id(0) == 0) & (kk == 0))
        def _():
            st_ref[...] = jnp.zeros_like(st_ref)

        part = _dot_nt(du_ref[...], w_ref[...])

        @pl.when(kk == 0)
        def _():
            acc[...] = part

        @pl.when(kk > 0)
        def _():
            acc[...] += part

        @pl.when(kk == N_CHIPS - 1)
        def _():
            dh2 = acc[...]
            h1 = h1_ref[...]
            r3 = _rms(h1)
            hn3 = h1 * r3
            g3, sc2 = _row(vec_ref, V_G3), _row(vec_ref, V_SC2)
            dhn3 = dh2 * g3 * (1.0 + sc2)
            dh1 = dout_ref[...] + r3 * (dhn3 - hn3 * _rowmean(dhn3 * hn3))
            y = y_ref[...]
            r2 = _rms(y)
            yn = y * r2
            gt1, g2 = _row(vec_ref, V_GT1), _row(vec_ref, V_G2)
            dyn = dh1 * gt1 * g2
            dh1_ref[...] = dh1
            dy_ref[...] = (r2 * (dyn - yn * _rowmean(dyn * yn))).astype(BF)
            st_ref[0:1, :] += _colsum(dh2)
            st_ref[1:2, :] += _colsum(dh2 * hn3 * g3)
            st_ref[2:3, :] += _colsum(dh2 * hn3 * (1.0 + sc2))
            st_ref[3:4, :] += _colsum(dh1 * yn * g2)
            st_ref[4:5, :] += _colsum(dh1 * gt1 * yn)

    row = pl.BlockSpec((tm, D_MODEL), lambda i, k: (i, 0))
    return pl.pallas_call(
        body, name="ffn_up_bwd", grid=(SEQ // tm, N_CHIPS),
        in_specs=[pl.BlockSpec((tm, D_MODEL), lambda i, k: (i, k)), row, row, row,
                  pl.BlockSpec((16, D_MODEL), lambda i, k: (0, 0)),
                  pl.BlockSpec((None, D_MODEL, D_MODEL), lambda i, k: (k, 0, 0))],
        out_specs=[row, row, pl.BlockSpec((8, D_MODEL), lambda i, k: (0, 0))],
        out_shape=[jax.ShapeDtypeStruct((SEQ, D_MODEL), F32), jax.ShapeDtypeStruct((SEQ, D_MODEL), BF),
                   jax.ShapeDtypeStruct((8, D_MODEL), F32)],
        scratch_shapes=[pltpu.VMEM((tm, D_MODEL), F32)],
        compiler_params=_cp(("arbitrary", "arbitrary")),
    )(du, h1, y, dout, vecs, w_ff14)


def _mix_bwd(dy, proj, rb, sbp, w_out, w_sb4):
    tm, tn = 512, 256

    def body(dy_ref, ar_ref, as_ref, rb_ref, sbp_ref, wo_ref, ws_ref,
             drb_ref, dsbp_ref, dar_ref, das_ref, dsb_ref, acc):
        j = pl.program_id(1)
        dm = _dot_nt(dy_ref[...], wo_ref[...])
        sr = _sigmoid(ar_ref[...].astype(F32))
        ss = _sigmoid(as_ref[...].astype(F32))
        dsbp = (dm * ss).astype(BF)
        drb_ref[...] = (dm * sr).astype(BF)
        dsbp_ref[...] = dsbp
        dar_ref[...] = (dm * rb_ref[...].astype(F32) * sr * (1.0 - sr)).astype(BF)
        das_ref[...] = (dm * sbp_ref[...].astype(F32) * ss * (1.0 - ss)).astype(BF)
        part = _dot_nt(dsbp, ws_ref[...])

        @pl.when(j == 0)
        def _():
            acc[...] = part

        @pl.when(j > 0)
        def _():
            acc[...] += part

        @pl.when(j == D_MODEL // tn - 1)
        def _():
            dsb_ref[...] = acc[...].astype(BF)

    tile = pl.BlockSpec((tm, tn), lambda i, j: (i, j))
    shp = jax.ShapeDtypeStruct((SEQ, D_MODEL), BF)
    return pl.pallas_call(
        body, name="mix_bwd", grid=(SEQ // tm, D_MODEL // tn),
        in_specs=[pl.BlockSpec((tm, D_MODEL), lambda i, j: (i, 0)),
                  pl.BlockSpec((tm, tn), lambda i, j: (i, C_AR // tn + j)),
                  pl.BlockSpec((tm, tn), lambda i, j: (i, C_AS // tn + j)),
                  tile, tile,
                  pl.BlockSpec((tn, D_MODEL), lambda i, j: (j, 0)),
                  pl.BlockSpec((None, 512, tn), lambda i, j: (j, 0, 0))],
        out_specs=[tile, tile, tile, tile, pl.BlockSpec((tm, 512), lambda i, j: (i, 0))],
        out_shape=[shp, shp, shp, shp, jax.ShapeDtypeStruct((SEQ, 512), BF)],
        scratch_shapes=[pltpu.VMEM((tm, 512), F32)],
        compiler_params=_cp(("parallel", "arbitrary")),
    )(dy, proj, proj, rb, sbp, w_out, w_sb4)


def _ret_branch_bwd(drb, proj, o_raw, gn_g, w_ret):
    tm, tn = 512, 256

    def body(d_ref, g_ref, o_ref, gn_ref, w_ref, dret_ref, dgr_ref, st_ref):
        @pl.when(pl.program_id(1) == 0)
        def _():
            st_ref[...] = jnp.zeros_like(st_ref)

        dretg = _dot_nt(d_ref[...], w_ref[...])
        for gi in range(tn // 128):
            cols = slice(128 * gi, 128 * (gi + 1))
            o = o_ref[:, cols]
            d = o - _rowmean(o)
            rstd = lax.rsqrt(_rowmean(d * d) + EPS)
            nh = d * rstd
            gain = gn_ref[:, cols]
            gr = g_ref[:, cols].astype(F32)
            sg = _sigmoid(gr)
            dg = dretg[:, cols]
            dgn = dg * gr * sg
            dnh = dgn * gain
            dgr_ref[:, cols] = (dg * nh * gain * sg * (1.0 + gr * (1.0 - sg))).astype(BF)
            dret_ref[:, cols] = (rstd * (dnh - _rowmean(dnh) - nh * _rowmean(dnh * nh))).astype(BF)
            st_ref[0:1, cols] += _colsum(dgn * nh)

    tile = pl.BlockSpec((tm, tn), lambda j, i: (i, j))
    shp = jax.ShapeDtypeStruct((SEQ, D_MODEL), BF)
    return pl.pallas_call(
        body, name="ret_branch_bwd", grid=(D_MODEL // tn, SEQ // tm),
        in_specs=[pl.BlockSpec((tm, D_MODEL), lambda j, i: (i, 0)),
                  pl.BlockSpec((tm, tn), lambda j, i: (i, C_GR // tn + j)),
                  tile, pl.BlockSpec((1, tn), lambda j, i: (0, j)),
                  pl.BlockSpec((tn, D_MODEL), lambda j, i: (j, 0))],
        out_specs=[tile, tile, pl.BlockSpec((8, tn), lambda j, i: (0, j))],
        out_shape=[shp, shp, jax.ShapeDtypeStruct((8, D_MODEL), F32)],
        compiler_params=_cp(("parallel", "arbitrary")),
    )(drb, proj, o_raw, gn_g, w_ret)


def _in_proj_bwd(dproj, x, dh1, vecs, w_in4):
    tm = 512
    wc = w_in4.shape[2]

    def body(dp_ref, x_ref, dh1_ref, vec_ref, w_ref, dx_ref, st_ref, acc):
        kk = pl.program_id(1)

        @pl.when((pl.program_id(0) == 0) & (kk == 0))
        def _():
            st_ref[...] = jnp.zeros_like(st_ref)

        part = _dot_nt(dp_ref[...], w_ref[...])

        @pl.when(kk == 0)
        def _():
            acc[...] = part

        @pl.when(kk > 0)
        def _():
            acc[...] += part

        @pl.when(kk == N_CHIPS - 1)
        def _():
            dh = acc[...]
            xx = x_ref[...]
            r1 = _rms(xx)
            xn = xx * r1
            g1, sc1 = _row(vec_ref, V_G1), _row(vec_ref, V_SC1)
            dxn = dh * g1 * (1.0 + sc1)
            dx_ref[...] = dh1_ref[...] + r1 * (dxn - xn * _rowmean(dxn * xn))
            st_ref[0:1, :] += _colsum(dh)
            st_ref[1:2, :] += _colsum(dh * xn * g1)
            st_ref[2:3, :] += _colsum(dh * xn * (1.0 + sc1))

    row = pl.BlockSpec((tm, D_MODEL), lambda i, k: (i, 0))
    return pl.pallas_call(
        body, name="in_proj_bwd", grid=(SEQ // tm, N_CHIPS),
        in_specs=[pl.BlockSpec((tm, wc), lambda i, k: (i, k)), row, row,
                  pl.BlockSpec((16, D_MODEL), lambda i, k: (0, 0)),
                  pl.BlockSpec((None, D_MODEL, wc), lambda i, k: (k, 0, 0))],
        out_specs=[row, pl.BlockSpec((8, D_MODEL), lambda i, k: (0, 0))],
        out_shape=[jax.ShapeDtypeStruct((SEQ, D_MODEL), F32), jax.ShapeDtypeStruct((8, D_MODEL), F32)],
        scratch_shapes=[pltpu.VMEM((tm, D_MODEL), F32)],
        compiler_params=_cp(("arbitrary", "arbitrary")),
    )(dproj, x, dh1, vecs, w_in4)


def _weight_grad(a, b, ta, tb, col_sharded, name):
    ka, nb_ = a.shape[1], b.shape[1]

    def body(a_ref, b_ref, o_ref):
        o_ref[...] = _dot_tn(a_ref[...], b_ref[...])

    if col_sharded:
        per = nb_ // N_CHIPS // tb
        out_shape = jax.ShapeDtypeStruct((N_CHIPS, ka, nb_ // N_CHIPS), F32)
        out_spec = pl.BlockSpec((None, ta, tb), lambda i, j: (j // per, i, j % per))
    else:
        per = ka // N_CHIPS // ta
        out_shape = jax.ShapeDtypeStruct((N_CHIPS, ka // N_CHIPS, nb_), F32)
        out_spec = pl.BlockSpec((None, ta, tb), lambda i, j: (i // per, i % per, j))
    return pl.pallas_call(
        body, name=name, grid=(ka // ta, nb_ // tb),
        in_specs=[pl.BlockSpec((SEQ, ta), lambda i, j: (0, i)), pl.BlockSpec((SEQ, tb), lambda i, j: (0, j))],
        out_specs=out_spec, out_shape=out_shape,
        compiler_params=_cp(("parallel", "parallel")),
    )(a, b)


def _rope_constants():
    freq = np.float32(ROPE_BASE) ** (-np.arange(0, 64, 2, dtype=np.float32) / np.float32(64))
    inv = np.tile(freq.astype(np.float32), 4).reshape(1, 128)
    sign = np.tile(np.concatenate([-np.ones(32, np.float32), np.ones(32, np.float32)]), 2).reshape(1, 128)
    return jnp.asarray(inv), jnp.asarray(sign)


def _log_gamma():
    return jnp.asarray(np.log1p(-(2.0 ** (-5.0 - np.arange(8, dtype=np.float64)))).astype(np.float32))


def _halves(g):
    return g.reshape(N_CHIPS, 2, g.shape[1] // 2, g.shape[2])


def kernel(x, c, positions, ada_w, ada_b, pre_mix_g, post_mix_g, pre_ffn_g, post_ffn_g, w_in, ret_gn_g, w_ret_branch, w_sb_branch, w_out, w_ff1, w_ff2, loss_target, m_ada_w, m_ada_b, m_pre_mix_g, m_post_mix_g, m_pre_ffn_g, m_post_ffn_g, m_w_in, m_ret_gn_g, m_w_ret_branch, m_w_sb_branch, m_w_out, m_w_ff1, m_w_ff2, v_ada_w, v_ada_b, v_pre_mix_g, v_post_mix_g, v_pre_ffn_g, v_post_ffn_g, v_w_in, v_ret_gn_g, v_w_ret_branch, v_w_sb_branch, v_w_out, v_w_ff1, v_w_ff2):
    names = ["w_in", "w_ret", "w_sb", "w_out", "w_ff1", "w_ff2"]
    big = dict(zip(names, [w_in, w_ret_branch, w_sb_branch, w_out, w_ff1, w_ff2]))
    big_m = dict(zip(names, [m_w_in, m_w_ret_branch, m_w_sb_branch, m_w_out, m_w_ff1, m_w_ff2]))
    big_v = dict(zip(names, [v_w_in, v_w_ret_branch, v_w_sb_branch, v_w_out, v_w_ff1, v_w_ff2]))
    rest = names[1:]
    cidx = lax.axis_index("c").astype(jnp.int32).reshape(1)
    kidx = (2 * lax.axis_index("x") + lax.axis_index("y")).astype(jnp.int32).reshape(1)
    x0, target = x[0], loss_target[0]

    cast = {nm: _cast_bf16(big[nm][0], kidx, "cast_" + nm) for nm in names}
    buf_in, sem_in, tok_in = _gather_start("gather_in_start", [cast["w_in"]])
    buf_rest, sem_rest, tok_rest = _gather_start("gather_rest_start", [cast[nm] for nm in rest])

    c_all, mod4 = _mod_exchange(_tie(tok_rest, _tie(tok_in, c)), ada_w[0], ada_b.reshape(N_CHIPS, -1))
    vecs = jnp.concatenate([mod4.reshape(6, D_MODEL), pre_mix_g, post_mix_g, pre_ffn_g, post_ffn_g,
                            jnp.zeros((6, D_MODEL), F32)], axis=0)
    inv_freq, sign = _rope_constants()
    lg = _log_gamma()
    cos, sin_s = _rope_tables(positions.reshape(SEQ, 1), inv_freq, sign)
    buf_in, sem_in, tok_in = _gather_pass("gather_in_pass", buf_in, sem_in, cos)
    (w_in4,) = _gather_finish("gather_in_finish", buf_in, sem_in, tok_in)

    h, proj = _ln_proj(x0, vecs, w_in4)
    sb, tot = _sb_fwd(proj)
    buf_rest, sem_rest, tok_rest = _gather_pass("gather_rest_pass", buf_rest, sem_rest, sb)
    o_raw, retg, states = _ret_fwd(_tie(tok_rest, proj), cos, sin_s, ret_gn_g, lg)
    w_ret4, w_sb4, w_out4, w_ff14, w_ff24 = _gather_finish("gather_rest_finish", buf_rest, sem_rest, retg)
    w_ret = w_ret4.reshape(D_MODEL, D_MODEL)
    w_out2 = w_out4.reshape(D_MODEL, D_MODEL)
    w_ff2_2 = w_ff24.reshape(D_FF, D_MODEL)
    mixed, rb, sbp = _mix(retg, sb, proj, w_ret, w_sb4)
    y, h1, h2 = _out_proj(mixed, x0, vecs, w_out2)
    u, act = _ffn_up(h2, w_ff14)
    dout, df, st_a = _ffn_down_loss(act, h1, target, vecs, w_ff2_2)

    du = _ffn_down_bwd(df, u, w_ff2_2)
    grads = {"w_ff2": _weight_grad(act, df, 512, 1024, False, "grad_w_ff2")}
    dh1, dy, st_b = _ffn_up_bwd(du, h1, y, dout, vecs, w_ff14)
    grads["w_ff1"] = _weight_grad(h2, du, 512, 1024, True, "grad_w_ff1")
    drb, dsbp, da_r, da_s, dsb = _mix_bwd(dy, proj, rb, sbp, w_out2, w_sb4)
    grads["w_out"] = _weight_grad(mixed, dy, 256, 1024, False, "grad_w_out")
    dret, dg_r, st_c = _ret_branch_bwd(drb, proj, o_raw, ret_gn_g, w_ret)
    grads["w_ret"] = _weight_grad(retg, drb, 256, 1024, False, "grad_w_ret")
    grads["w_sb"] = _weight_grad(sb, dsbp, 512, 256, True, "grad_w_sb")

    bufs, sems, tok = _pair_send_start("rs_rest_pair_send", [_halves(grads[nm]) for nm in rest])
    dq_r, dk_r, dv_r = _ret_bwd(proj, cos, sin_s, _tie(tok, dret), states, lg)
    mine, theirs = _pair_send_wait("rs_rest_pair_recv", bufs, sems, dq_r)
    pair_sums = [_pair_add(g, r, cidx, "pair_add_" + nm) for g, r, nm in zip(mine, theirs, rest)]
    bufs, sems, tok = _chip_send_start("rs_rest_chip_send", pair_sums)
    dq_s, dk_s, dv_s = _sb_bwd(proj, _tie(tok, dsb), tot)
    own, parts = _chip_send_wait("rs_rest_chip_recv", bufs, sems, dq_s)
    sums = [_chip_add(o, p, kidx, cidx, "chip_add_" + nm) for o, p, nm in zip(own, parts, rest)]
    bufs, sems, tok = _pair_swap_start("rs_rest_pair_swap", sums)
    dproj = jnp.concatenate([_tie(tok, dq_r), dk_r, dv_r, dg_r, dq_s, dk_s, dv_s, da_r, da_s], axis=1)
    g_in = _weight_grad(h, dproj, 512, 1664, True, "grad_w_in")
    full_rest = _pair_swap_wait("rs_rest_pair_swapped", bufs, sems, g_in)

    bufs, sems, tok = _pair_send_start("rs_in_pair_send", [_halves(g_in)])
    dx, st_d = _in_proj_bwd(_tie(tok, dproj), x0, dh1, vecs, w_in4)
    mine, theirs = _pair_send_wait("rs_in_pair_recv", bufs, sems, dx)
    bufs, sems, tok = _chip_send_start("rs_in_chip_send", [_pair_add(mine[0], theirs[0], cidx, "pair_add_w_in")])
    out = {}
    for nm, g in zip(rest, full_rest):
        w = big[nm][0]
        out[nm] = _adamw(w, big_m[nm][0], big_v[nm][0], _tie(tok, g).reshape(w.shape), "adamw_" + nm)

    payload = jnp.concatenate([
        st_d[0:2], st_b[3:4], st_b[0:2], st_a[0:1],
        st_d[2:3], st_b[4:5], st_b[2:3], st_a[1:2],
        st_c[0:1], st_a[2:3]], axis=0)

    def table(b6, g5):
        return jnp.concatenate([b6.reshape(6, D_MODEL)] + g5 + [jnp.zeros((5, D_MODEL), F32)], axis=0)

    wsm = table(ada_b, [pre_mix_g, post_mix_g, pre_ffn_g, post_ffn_g, ret_gn_g])
    msm = table(m_ada_b, [m_pre_mix_g, m_post_mix_g, m_pre_ffn_g, m_post_ffn_g, m_ret_gn_g])
    vsm = table(v_ada_b, [v_pre_mix_g, v_post_mix_g, v_pre_ffn_g, v_post_ffn_g, v_ret_gn_g])
    g_ada, gsm, dsm, mosm, vosm, loss = _small_exchange(
        payload.reshape(N_PAY, 1, D_MODEL), c_all, wsm, msm, vsm)
    ada_out = _adamw(ada_w[0], m_ada_w[0], v_ada_w[0], g_ada, "adamw_ada_w")

    own, parts = _chip_send_wait("rs_in_chip_recv", bufs, sems, ada_out[1])
    bufs, sems, tok = _pair_swap_start(
        "rs_in_pair_swap", [_chip_add(own[0], parts[0], kidx, cidx, "chip_add_w_in")])
    (full_in,) = _pair_swap_wait("rs_in_pair_swapped", bufs, sems, tok)
    out["w_in"] = _adamw(w_in[0], m_w_in[0], v_w_in[0], full_in.reshape(w_in.shape[1:]), "adamw_w_in")

    def unpack(tab):
        return [tab[0:6].reshape(1, 6 * D_MODEL)] + [tab[6 + r:7 + r] for r in range(5)]

    def ordered(which):
        sm = unpack([gsm, dsm, mosm, vosm][which])
        bg = [out[nm][which][None] for nm in names]
        return [ada_out[which][None], sm[0], sm[1], sm[2], sm[3], sm[4], bg[0], sm[5]] + bg[1:]

    return (loss.reshape(()), dx[None], *ordered(0), *ordered(1), *ordered(2), *ordered(3))
```

```python
import functools

import numpy as np
import jax
import jax.numpy as jnp
from jax import lax
from jax.experimental import pallas as pl
from jax.experimental.pallas import tpu as pltpu

SEQ = 2048
D_MODEL = 1024
D_IN = 6656
D_FF = 4096
N_CHIPS = 4
EPS = 1e-6
ROPE_BASE = 10000.0
RET_BLOCK = 256
RET_CHUNK_SHIFT = 6
SB_BLOCK = 256
QK_SCALE = 0.125
N_PAIRS = 4

ADAM_LR = 0.001
ADAM_B1 = 0.9
ADAM_B2 = 0.999
ADAM_EPS = 1e-08
ADAM_WD = 0.01
ADAM_STEP = 10

BF = jnp.bfloat16
F32 = jnp.float32
MESH = pl.DeviceIdType.MESH
VMEM_LIMIT = 56 * 1024 * 1024
ANY = pl.BlockSpec(memory_space=pl.ANY)

C_QR, C_KR, C_VR, C_GR, C_QS, C_KS, C_VS, C_AR, C_AS = 0, 512, 1024, 2048, 3072, 3584, 4096, 4608, 5632

V_SH1, V_SC1, V_GT1, V_SH2, V_SC2, V_GT2, V_G1, V_G2, V_G3, V_G4 = range(10)
P_DSH1, P_DSC1, P_DGT1, P_DSH2, P_DSC2, P_DGT2, P_DG1, P_DG2, P_DG3, P_DG4, P_DGN, P_LOSS = range(12)
N_PAY = 12


def _cp(sem=None, **kw):
    if sem is not None:
        kw["dimension_semantics"] = sem
    return pltpu.CompilerParams(vmem_limit_bytes=VMEM_LIMIT, **kw)


def _dot(a, b):
    return jnp.dot(a, b, preferred_element_type=F32)


def _dot_nt(a, b):
    return lax.dot_general(a, b, (((1,), (1,)), ((), ())), preferred_element_type=F32)


def _dot_tn(a, b):
    return lax.dot_general(a, b, (((0,), (0,)), ((), ())), preferred_element_type=F32)


def _row(ref, i):
    return ref[i:i + 1, :]


def _rms(v):
    return lax.rsqrt(jnp.mean(v * v, axis=1, keepdims=True) + EPS)


def _colsum(v):
    return jnp.sum(v, axis=0, keepdims=True)


def _rowmean(v):
    return jnp.mean(v, axis=1, keepdims=True)


def _sigmoid(v):
    return 1.0 / (1.0 + jnp.exp(-v))


def _cast_bf16(w, kidx, name):
    rows, cols = w.shape
    tr = min(rows, 256)

    def body(k_ref, w_ref, o_ref):
        o_ref[...] = w_ref[...].astype(BF)

    return pl.pallas_call(
        body, name=name,
        grid_spec=pltpu.PrefetchScalarGridSpec(
            num_scalar_prefetch=1, grid=(rows // tr,),
            in_specs=[pl.BlockSpec((tr, cols), lambda i, k_ref: (i, 0))],
            out_specs=pl.BlockSpec((None, tr, cols), lambda i, k_ref: (k_ref[0], i, 0))),
        out_shape=jax.ShapeDtypeStruct((N_CHIPS, rows, cols), BF),
        compiler_params=_cp(("parallel",)),
    )(kidx, w)


def _adamw_math(w, g, m, v):
    m = ADAM_B1 * m + (1.0 - ADAM_B1) * g
    v = ADAM_B2 * v + (1.0 - ADAM_B2) * (g * g)
    m_hat = m / (1.0 - ADAM_B1 ** ADAM_STEP)
    v_hat = v / (1.0 - ADAM_B2 ** ADAM_STEP)
    delta = -ADAM_LR * (m_hat / (jnp.sqrt(v_hat) + ADAM_EPS) + ADAM_WD * w)
    return delta, m, v


def _adamw(w, m, v, g, name, dep=None):
    rows, cols = w.shape
    tr = min(rows, 128)

    def body(w_ref, m_ref, v_ref, g_ref, go_ref, d_ref, mo_ref, vo_ref):
        gg = g_ref[...]
        d, mm, vv = _adamw_math(w_ref[...], gg, m_ref[...], v_ref[...])
        go_ref[...] = gg
        d_ref[...] = d
        mo_ref[...] = mm
        vo_ref[...] = vv

    spec = pl.BlockSpec((tr, cols), lambda i: (i, 0))
    shp = jax.ShapeDtypeStruct((rows, cols), F32)
    body, in_specs, args = _add_dep(body, [spec] * 4, [w, m, v, g], dep)
    return pl.pallas_call(
        body, name=name, grid=(rows // tr,),
        in_specs=in_specs, out_specs=[spec] * 4, out_shape=[shp] * 4,
        compiler_params=_cp(("parallel",)),
    )(*args)


def _place():
    x, y, c = lax.axis_index("x"), lax.axis_index("y"), lax.axis_index("c")
    return x, y, c


HBM = pl.BlockSpec(memory_space=pltpu.HBM)
SEM = pl.BlockSpec(memory_space=pltpu.SEMAPHORE)
EFFECT = pltpu.SideEffectType.DATAFLOW_SIDE_EFFECTING


def _tie(token, small):
    return small + token[0, 0]


def _add_dep(body, in_specs, args, dep):
    if dep is None:
        return body, list(in_specs), list(args)
    n = len(args)

    def wrapped(*refs):
        body(*refs[:n], *refs[n + 1:])

    return wrapped, list(in_specs) + [ANY], list(args) + [dep]


def _split_call(name, bufs, run, old=None, after=None, new=0):
    nb = len(bufs)
    n_in = nb + (3 if old is not None else 0)

    def body(*refs):
        old_sems = (refs[nb], refs[nb + 1]) if old is not None else None
        new_sems = (refs[n_in], refs[n_in + 1]) if new else None
        run(refs[:nb], old_sems, new_sems)
        if new:
            refs[-1][...] = jnp.zeros_like(refs[-1])

    in_specs = [HBM] * nb + ([SEM, SEM, ANY] if old is not None else [])
    out_shape = [pltpu.SemaphoreType.DMA((new,))] * 2 if new else []
    out_specs = [SEM, SEM] if new else []
    out_shape += [pltpu.HBM(b.shape, b.dtype) for b in bufs]
    out_specs += [HBM] * nb
    if new:
        out_shape.append(jax.ShapeDtypeStruct((8, 128), F32))
        out_specs.append(pl.BlockSpec(memory_space=pltpu.VMEM))
    first = 2 if new else 0
    args = [pltpu.with_memory_space_constraint(b, pltpu.HBM) for b in bufs]
    if old is not None:
        args += [old[0], old[1], after]
    outs = pl.pallas_call(
        body, name=name, in_specs=tuple(in_specs), out_specs=tuple(out_specs), out_shape=tuple(out_shape),
        input_output_aliases={i: i + first for i in range(nb)},
        compiler_params=pltpu.CompilerParams(has_side_effects=EFFECT),
    )(*args)
    thru = list(outs[first:first + nb])
    if new:
        return thru, (outs[0], outs[1]), outs[-1]
    return thru, None, None


def _remote(part_src, part_dst, sems, i, to):
    return pltpu.make_async_remote_copy(src_ref=part_src, dst_ref=part_dst, send_sem=sems[0].at[i],
                                        recv_sem=sems[1].at[i], device_id=to, device_id_type=MESH)


def _other_chips(x, y):
    return [(1 - x, y), (x, 1 - y), (1 - x, 1 - y)]


def _gather_start(name, bufs):
    def run(refs, old, new):
        x, y, c = _place()
        k = 2 * x + y
        for w, ref in enumerate(refs):
            rh = bufs[w].shape[1] // 2
            part = ref.at[k, pl.ds(c * rh, rh)]
            for j, (cx, cy) in enumerate(_other_chips(x, y)):
                _remote(part, part, new, 3 * w + j, (cx, cy, c)).start()

    return _split_call(name, bufs, run, new=3 * len(bufs))


def _gather_pass(name, bufs, sems, after):
    def run(refs, old, new):
        x, y, c = _place()
        k = 2 * x + y
        sib = (x, y, 1 - c)
        for w, ref in enumerate(refs):
            rh = bufs[w].shape[1] // 2
            for j, (cx, cy) in enumerate(_other_chips(x, y)):
                land = ref.at[2 * cx + cy, pl.ds(c * rh, rh)]
                _remote(land, land, old, 3 * w + j, (cx, cy, c)).wait_recv()
                _remote(land, land, new, 3 * w + j, sib).start()
        for w, ref in enumerate(refs):
            rh = bufs[w].shape[1] // 2
            part = ref.at[k, pl.ds(c * rh, rh)]
            for j, (cx, cy) in enumerate(_other_chips(x, y)):
                _remote(part, part, old, 3 * w + j, (cx, cy, c)).wait_send()

    return _split_call(name, bufs, run, old=sems, after=after, new=3 * len(bufs))


def _gather_finish(name, bufs, sems, after):
    def run(refs, old, new):
        x, y, c = _place()
        sib = (x, y, 1 - c)
        for w, ref in enumerate(refs):
            rh = bufs[w].shape[1] // 2
            for j, (cx, cy) in enumerate(_other_chips(x, y)):
                sent = ref.at[2 * cx + cy, pl.ds(c * rh, rh)]
                _remote(sent, sent, old, 3 * w + j, sib).wait_send()
                land = ref.at[2 * cx + cy, pl.ds((1 - c) * rh, rh)]
                _remote(land, land, old, 3 * w + j, sib).wait_recv()

    return _split_call(name, bufs, run, old=sems, after=after)[0]


def _pair_send_start(name, grads):
    n = len(grads)
    lands = [lax.empty((N_CHIPS,) + g.shape[2:], F32) for g in grads]

    def run(refs, old, new):
        x, y, c = _place()
        for w in range(n):
            _remote(refs[w].at[:, 1 - c], refs[n + w], new, w, (x, y, 1 - c)).start()

    return _split_call(name, list(grads) + lands, run, new=n)


def _pair_send_wait(name, bufs, sems, after):
    n = len(bufs) // 2

    def run(refs, old, new):
        x, y, c = _place()
        for w in range(n):
            cp = _remote(refs[w].at[:, 1 - c], refs[n + w], old, w, (x, y, 1 - c))
            cp.wait_send()
            cp.wait_recv()

    thru = _split_call(name, bufs, run, old=sems, after=after)[0]
    return thru[:n], thru[n:]


def _pair_add(g, recv, cidx, name):
    _, _, rh, cols = g.shape
    tr = min(rh, 256)

    def body(c_ref, g_ref, r_ref, o_ref):
        o_ref[...] = (g_ref[...] + r_ref[...]).astype(BF)

    return pl.pallas_call(
        body, name=name,
        grid_spec=pltpu.PrefetchScalarGridSpec(
            num_scalar_prefetch=1, grid=(N_CHIPS, rh // tr),
            in_specs=[pl.BlockSpec((None, None, tr, cols), lambda s, i, c_ref: (s, c_ref[0], i, 0)),
                      pl.BlockSpec((None, tr, cols), lambda s, i, c_ref: (s, i, 0))],
            out_specs=pl.BlockSpec((None, tr, cols), lambda s, i, c_ref: (s, i, 0))),
        out_shape=jax.ShapeDtypeStruct((N_CHIPS, rh, cols), BF),
        compiler_params=_cp(("parallel", "parallel")),
    )(cidx, g, recv)


def _chip_send_start(name, sums):
    n = len(sums)
    lands = [lax.empty((3,) + s.shape[1:], BF) for s in sums]

    def run(refs, old, new):
        x, y, c = _place()
        for w in range(n):
            for j, (cx, cy) in enumerate(_other_chips(x, y)):
                _remote(refs[w].at[2 * cx + cy], refs[n + w].at[j], new, 3 * w + j, (cx, cy, c)).start()

    return _split_call(name, list(sums) + lands, run, new=3 * n)


def _chip_send_wait(name, bufs, sems, after):
    n = len(bufs) // 2

    def run(refs, old, new):
        x, y, c = _place()
        for w in range(n):
            for j, (cx, cy) in enumerate(_other_chips(x, y)):
                cp = _remote(refs[w].at[2 * cx + cy], refs[n + w].at[j], old, 3 * w + j, (cx, cy, c))
                cp.wait_send()
                cp.wait_recv()

    thru = _split_call(name, bufs, run, old=sems, after=after)[0]
    return thru[:n], thru[n:]


def _chip_add(own, parts, kidx, cidx, name):
    _, rh, cols = parts.shape
    tr = min(rh, 256)

    def body(k_ref, c_ref, own_ref, p_ref, o_ref):
        acc = own_ref[...].astype(F32)
        for s in range(3):
            acc = acc + p_ref[s].astype(F32)
        o_ref[...] = acc

    return pl.pallas_call(
        body, name=name,
        grid_spec=pltpu.PrefetchScalarGridSpec(
            num_scalar_prefetch=2, grid=(rh // tr,),
            in_specs=[pl.BlockSpec((None, tr, cols), lambda i, k_ref, c_ref: (k_ref[0], i, 0)),
                      pl.BlockSpec((3, tr, cols), lambda i, k_ref, c_ref: (0, i, 0))],
            out_specs=pl.BlockSpec((None, tr, cols), lambda i, k_ref, c_ref: (c_ref[0], i, 0))),
        out_shape=jax.ShapeDtypeStruct((2, rh, cols), F32),
        compiler_params=_cp(("parallel",)),
    )(kidx, cidx, own, parts)


def _pair_swap_start(name, bufs):
    def run(refs, old, new):
        x, y, c = _place()
        for w, ref in enumerate(refs):
            _remote(ref.at[c], ref.at[c], new, w, (x, y, 1 - c)).start()

    return _split_call(name, bufs, run, new=len(bufs))


def _pair_swap_wait(name, bufs, sems, after):
    def run(refs, old, new):
        x, y, c = _place()
        for w, ref in enumerate(refs):
            _remote(ref.at[c], ref.at[c], old, w, (x, y, 1 - c)).wait_send()
            _remote(ref.at[1 - c], ref.at[1 - c], old, w, (x, y, 1 - c)).wait_recv()

    return _split_call(name, bufs, run, old=sems, after=after)[0]


def _peers(x, y, c):
    out = []
    for code in range(1, 8):
        fx, fy, fc = (code >> 2) & 1, (code >> 1) & 1, code & 1
        px = 1 - x if fx else x
        py = 1 - y if fy else y
        pc = 1 - c if fc else c
        out.append((code, (px, py, pc)))
    return out


def _mod_exchange(c_row, ada_w, ada_b4):
    ncol = ada_w.shape[1]

    def body(c_ref, w_ref, b_ref, call_ref, mod_ref, part_ref, send_sems, recv_sems):
        x, y, c = _place()
        k = 2 * x + y
        me = 4 * x + 2 * y + c
        call_ref[pl.ds(me, 1), :] = c_ref[...]
        sends = []
        for code, peer in _peers(x, y, c):
            cp = pltpu.make_async_remote_copy(
                src_ref=c_ref, dst_ref=call_ref.at[pl.ds(me, 1), :],
                send_sem=send_sems.at[code], recv_sem=recv_sems.at[code],
                device_id=peer, device_id_type=MESH)
            cp.start()
            sends.append(cp)
        for code, (px, py, pc) in _peers(x, y, c):
            land = call_ref.at[pl.ds(4 * px + 2 * py + pc, 1), :]
            pltpu.make_async_remote_copy(
                src_ref=land, dst_ref=land, send_sem=send_sems.at[code], recv_sem=recv_sems.at[code],
                device_id=(px, py, pc), device_id_type=MESH).wait_recv()
        call = call_ref[...]
        act = call * _sigmoid(call)
        part = jnp.dot(act, w_ref[...], preferred_element_type=F32,
                       precision=lax.Precision.HIGHEST) + b_ref[pl.ds(k, 1), :]
        part_ref[...] = part
        mod_ref[pl.ds(k, 1), :] = part_ref[pl.ds(me, 1), :]
        chips = [(8 + j, peer) for j, (code, peer) in enumerate(_peers(x, y, c)) if code in (2, 4, 6)]
        for slot, (px, py, pc) in chips:
            cp = pltpu.make_async_remote_copy(
                src_ref=part_ref.at[pl.ds(4 * px + 2 * py + pc, 1), :], dst_ref=mod_ref.at[pl.ds(k, 1), :],
                send_sem=send_sems.at[slot], recv_sem=recv_sems.at[slot],
                device_id=(px, py, pc), device_id_type=MESH)
            cp.start()
            sends.append(cp)
        for slot, (px, py, pc) in chips:
            land = mod_ref.at[pl.ds(2 * px + py, 1), :]
            pltpu.make_async_remote_copy(
                src_ref=land, dst_ref=land, send_sem=send_sems.at[slot], recv_sem=recv_sems.at[slot],
                device_id=(px, py, pc), device_id_type=MESH).wait_recv()
        for cp in sends:
            cp.wait_send()

    vm = pl.BlockSpec(memory_space=pltpu.VMEM)
    return pl.pallas_call(
        body, name="mod_exchange",
        in_specs=[vm, vm, vm], out_specs=[vm, vm],
        out_shape=[jax.ShapeDtypeStruct((8, D_MODEL), F32), jax.ShapeDtypeStruct((N_CHIPS, ncol), F32)],
        scratch_shapes=[pltpu.VMEM((8, ncol), F32), pltpu.SemaphoreType.DMA((16,)),
                        pltpu.SemaphoreType.DMA((16,))],
        compiler_params=_cp(),
    )(c_row, ada_w, ada_b4)


def _small_exchange(payload, c_all, wsm, msm, vsm):
    ncol = 6 * D_MODEL // N_CHIPS

    def body(p_ref, call_ref, w_ref, m_ref, v_ref, gw_ref, g_ref, d_ref, mo_ref, vo_ref, loss_ref,
             all_ref, dm_ref, send_sems, recv_sems):
        x, y, c = _place()
        k = 2 * x + y
        me = 4 * x + 2 * y + c
        all_ref[:, pl.ds(me, 1), :] = p_ref[...]
        sends = []
        for code, peer in _peers(x, y, c):
            cp = pltpu.make_async_remote_copy(
                src_ref=p_ref, dst_ref=all_ref.at[:, pl.ds(me, 1), :],
                send_sem=send_sems.at[code], recv_sem=recv_sems.at[code],
                device_id=peer, device_id_type=MESH)
            cp.start()
            sends.append(cp)
        for code, (px, py, pc) in _peers(x, y, c):
            land = all_ref.at[:, pl.ds(4 * px + 2 * py + pc, 1), :]
            pltpu.make_async_remote_copy(
                src_ref=land, dst_ref=land, send_sem=send_sems.at[code], recv_sem=recv_sems.at[code],
                device_id=(px, py, pc), device_id_type=MESH).wait_recv()
        for cp in sends:
            cp.wait_send()
        tot = [_colsum(all_ref[r]) for r in range(N_PAY)]
        loss_ref[...] = jnp.sum(tot[P_LOSS], axis=1, keepdims=True)
        g_ref[...] = jnp.zeros_like(g_ref)
        for r in range(P_LOSS):
            g_ref[r:r + 1, :] = tot[r]
        g = g_ref[...]
        d, mm, vv = _adamw_math(w_ref[...], g, m_ref[...], v_ref[...])
        d_ref[...] = d
        mo_ref[...] = mm
        vo_ref[...] = vv
        half = D_MODEL // 2
        for kk in range(N_CHIPS):
            @pl.when(k == kk)
            def _():
                r0 = 3 * (kk // 2)
                if kk % 2 == 0:
                    dm_ref[:, :D_MODEL] = all_ref[r0]
                    dm_ref[:, D_MODEL:] = all_ref[r0 + 1][:, :half]
                else:
                    dm_ref[:, :half] = all_ref[r0 + 1][:, half:]
                    dm_ref[:, half:] = all_ref[r0 + 2]
        call = call_ref[...]
        act = call * _sigmoid(call)
        gw_ref[...] = lax.dot_general(act, dm_ref[...], (((0,), (0,)), ((), ())),
                                      preferred_element_type=F32, precision=lax.Precision.HIGHEST)

    vm = pl.BlockSpec(memory_space=pltpu.VMEM)
    small = jax.ShapeDtypeStruct((16, D_MODEL), F32)
    return pl.pallas_call(
        body, name="small_exchange",
        in_specs=[vm] * 5, out_specs=[vm] * 6,
        out_shape=[jax.ShapeDtypeStruct((D_MODEL, ncol), F32), small, small, small, small,
                   jax.ShapeDtypeStruct((1, 1), F32)],
        scratch_shapes=[pltpu.VMEM((N_PAY, 8, D_MODEL), F32), pltpu.VMEM((8, ncol), F32),
                        pltpu.SemaphoreType.DMA((8,)), pltpu.SemaphoreType.DMA((8,))],
        compiler_params=_cp(),
    )(payload, c_all, wsm, msm, vsm)


def _rope_tables(pos_col, inv_freq, sign):
    def body(p_ref, f_ref, s_ref, cos_ref, sin_ref):
        ang = p_ref[...].astype(F32) * f_ref[...]
        cos_ref[...] = jnp.cos(ang)
        sin_ref[...] = jnp.sin(ang) * s_ref[...]

    tr = 512
    shp = jax.ShapeDtypeStruct((SEQ, 128), F32)
    return pl.pallas_call(
        body, name="rope_tables", grid=(SEQ // tr,),
        in_specs=[pl.BlockSpec((tr, 1), lambda i: (i, 0)), pl.BlockSpec((1, 128), lambda i: (0, 0)),
                  pl.BlockSpec((1, 128), lambda i: (0, 0))],
        out_specs=[pl.BlockSpec((tr, 128), lambda i: (i, 0))] * 2, out_shape=[shp, shp],
        compiler_params=_cp(("parallel",)),
    )(pos_col, inv_freq, sign)


def _ln_proj(x, vecs, w_in4):
    tm = 512
    wc = w_in4.shape[2]

    def body(x_ref, vec_ref, w_ref, h_ref, proj_ref):
        @pl.when(pl.program_id(1) == 0)
        def _():
            xx = x_ref[...]
            g = _row(vec_ref, V_G1) * (1.0 + _row(vec_ref, V_SC1))
            h_ref[...] = (xx * _rms(xx) * g + _row(vec_ref, V_SH1)).astype(BF)
        proj_ref[...] = _dot(h_ref[...], w_ref[...]).astype(BF)

    return pl.pallas_call(
        body, name="ln_proj", grid=(SEQ // tm, N_CHIPS),
        in_specs=[pl.BlockSpec((tm, D_MODEL), lambda i, j: (i, 0)),
                  pl.BlockSpec((16, D_MODEL), lambda i, j: (0, 0)),
                  pl.BlockSpec((None, D_MODEL, wc), lambda i, j: (j, 0, 0))],
        out_specs=[pl.BlockSpec((tm, D_MODEL), lambda i, j: (i, 0)),
                   pl.BlockSpec((tm, wc), lambda i, j: (i, j))],
        out_shape=[jax.ShapeDtypeStruct((SEQ, D_MODEL), BF), jax.ShapeDtypeStruct((SEQ, D_IN), BF)],
        compiler_params=_cp(("parallel", "arbitrary")),
    )(x, vecs, w_in4)


def _lane_first(shape):
    lane = lax.broadcasted_iota(jnp.int32, shape, 1)
    return (lane & 32) == 0


def _rot(v, cos, sin_s):
    partner = jnp.where(_lane_first(v.shape), pltpu.roll(v, 96, 1), pltpu.roll(v, 32, 1))
    return v * cos + partner * sin_s


def _rot_t(dv, cos, sin_s):
    t = dv * sin_s
    partner = jnp.where(_lane_first(dv.shape), pltpu.roll(t, 96, 1), pltpu.roll(t, 32, 1))
    return dv * cos + partner


def _ret_masks(lg):
    t = RET_BLOCK
    ii = lax.broadcasted_iota(jnp.int32, (t, t), 0)
    jj = lax.broadcasted_iota(jnp.int32, (t, t), 1)
    dist = jnp.abs(ii - jj).astype(F32)
    future = (jj >> RET_CHUNK_SHIFT) > (ii >> RET_CHUNK_SHIFT)
    mask = jnp.where(future, 0.0, jnp.exp(lg * dist))
    ti = lax.broadcasted_iota(jnp.int32, (t, 1), 0).astype(F32)
    from_start = jnp.exp(lg * (ti + 1.0))
    to_end = jnp.exp(lg * (t - 1.0 - ti))
    whole = jnp.exp(jnp.full((1, 128), lg * t, F32))
    return mask, from_start, to_end, whole


def _head_lanes(shape, hh):
    lane = lax.broadcasted_iota(jnp.int32, shape, 1)
    return (lane >> 6) == hh


def _ret_specs():
    t = RET_BLOCK
    return dict(
        q=lambda f: pl.BlockSpec((t, 128), lambda p, n: (f(n), C_QR // 128 + p)),
        k=lambda f: pl.BlockSpec((t, 128), lambda p, n: (f(n), C_KR // 128 + p)),
        v=lambda f: pl.BlockSpec((t, 256), lambda p, n: (f(n), C_VR // 256 + p)),
        g=lambda f: pl.BlockSpec((t, 256), lambda p, n: (f(n), C_GR // 256 + p)),
        tab=lambda f: pl.BlockSpec((t, 128), lambda p, n: (f(n), 0)),
        wide=lambda f: pl.BlockSpec((t, 256), lambda p, n: (f(n), p)),
        narrow=lambda f: pl.BlockSpec((t, 128), lambda p, n: (f(n), p)),
        state=lambda f: pl.BlockSpec((None, None, 2, 128, 128), lambda p, n: (p, f(n), 0, 0, 0)),
    )


def _ret_fwd(proj, cos, sin_s, gn_g, log_gamma):
    t = RET_BLOCK
    nb = SEQ // t

    def body(lg_ref, q_ref, k_ref, v_ref, g_ref, cos_ref, sin_ref, gn_ref, o_ref, retg_ref, st_ref, state):
        p = pl.program_id(0)

        @pl.when(pl.program_id(1) == 0)
        def _():
            state[...] = jnp.zeros_like(state)

        cos, sn = cos_ref[...], sin_ref[...]
        q = _rot(q_ref[...].astype(F32), cos, sn)
        k = _rot(k_ref[...].astype(F32), cos, sn) * QK_SCALE
        for hh in range(2):
            lg = lg_ref[2 * p + hh]
            mask, from_start, to_end, whole = _ret_masks(lg)
            lanes = _head_lanes(q.shape, hh)
            qm = jnp.where(lanes, q, 0.0)
            km = jnp.where(lanes, k, 0.0)
            vh = v_ref[:, 128 * hh:128 * (hh + 1)]
            sc = _dot_nt(qm.astype(BF), km.astype(BF)) * mask
            st = state[hh]
            st_ref[hh] = st
            o = _dot(sc.astype(BF), vh) + _dot((qm * from_start).astype(BF), st.astype(BF))
            state[hh] = whole * st + _dot_tn((km * to_end).astype(BF), vh)
            d = o - _rowmean(o)
            nh = d * lax.rsqrt(_rowmean(d * d) + EPS)
            gr = g_ref[:, 128 * hh:128 * (hh + 1)].astype(F32)
            o_ref[:, 128 * hh:128 * (hh + 1)] = o
            retg_ref[:, 128 * hh:128 * (hh + 1)] = (
                gr * _sigmoid(gr) * nh * gn_ref[:, 128 * hh:128 * (hh + 1)]).astype(BF)

    sp = _ret_specs()
    ident = lambda n: n
    return pl.pallas_call(
        body, name="ret_fwd", grid=(N_PAIRS, nb),
        in_specs=[pl.BlockSpec(memory_space=pltpu.SMEM), sp["q"](ident), sp["k"](ident), sp["v"](ident),
                  sp["g"](ident), sp["tab"](ident), sp["tab"](ident),
                  pl.BlockSpec((1, 256), lambda p, n: (0, p))],
        out_specs=[sp["wide"](ident), sp["wide"](ident), sp["state"](ident)],
        out_shape=[jax.ShapeDtypeStruct((SEQ, D_MODEL), F32), jax.ShapeDtypeStruct((SEQ, D_MODEL), BF),
                   jax.ShapeDtypeStruct((N_PAIRS, nb, 2, 128, 128), F32)],
        scratch_shapes=[pltpu.VMEM((2, 128, 128), F32)],
        compiler_params=_cp(("parallel", "arbitrary")),
    )(log_gamma, proj, proj, proj, proj, cos, sin_s, gn_g)


def _ret_bwd(proj, cos, sin_s, dret, states, log_gamma):
    t = RET_BLOCK
    nb = SEQ // t

    def body(lg_ref, q_ref, k_ref, v_ref, cos_ref, sin_ref, do_ref, st_ref, dq_ref, dk_ref, dv_ref, dstate):
        p = pl.program_id(0)

        @pl.when(pl.program_id(1) == 0)
        def _():
            dstate[...] = jnp.zeros_like(dstate)

        cos, sn = cos_ref[...], sin_ref[...]
        q = _rot(q_ref[...].astype(F32), cos, sn)
        k = _rot(k_ref[...].astype(F32), cos, sn) * QK_SCALE
        dq_rot = jnp.zeros(q.shape, F32)
        dk_rot = jnp.zeros(q.shape, F32)
        for hh in range(2):
            lg = lg_ref[2 * p + hh]
            mask, from_start, to_end, whole = _ret_masks(lg)
            lanes = _head_lanes(q.shape, hh)
            qm = jnp.where(lanes, q, 0.0)
            km = jnp.where(lanes, k, 0.0)
            qb, kb = qm.astype(BF), km.astype(BF)
            vh = v_ref[:, 128 * hh:128 * (hh + 1)]
            do = do_ref[:, 128 * hh:128 * (hh + 1)]
            sc = (_dot_nt(qb, kb) * mask).astype(BF)
            st = st_ref[hh].astype(BF)
            dst = dstate[hh]
            dstb = dst.astype(BF)
            k_end = (km * to_end).astype(BF)
            q_start = (qm * from_start).astype(BF)
            dv_ref[:, 128 * hh:128 * (hh + 1)] = (_dot_tn(sc, do) + _dot(k_end, dstb)).astype(BF)
            dsc = (_dot_nt(do, vh) * mask).astype(BF)
            dq_h = _dot(dsc, kb) + _dot_nt(do, st) * from_start
            dq_rot = dq_rot + jnp.where(lanes, dq_h, 0.0)
            dk_rot = dk_rot + _dot_tn(dsc, qb) + _dot_nt(vh, dstb) * to_end
            dstate[hh] = whole * dst + _dot_tn(q_start, do)
        dq_ref[...] = _rot_t(dq_rot, cos, sn).astype(BF)
        dk_ref[...] = _rot_t(dk_rot * QK_SCALE, cos, sn).astype(BF)

    sp = _ret_specs()
    rev = lambda n: nb - 1 - n
    return pl.pallas_call(
        body, name="ret_bwd", grid=(N_PAIRS, nb),
        in_specs=[pl.BlockSpec(memory_space=pltpu.SMEM), sp["q"](rev), sp["k"](rev), sp["v"](rev),
                  sp["tab"](rev), sp["tab"](rev), sp["wide"](rev), sp["state"](rev)],
        out_specs=[sp["narrow"](rev), sp["narrow"](rev), sp["wide"](rev)],
        out_shape=[jax.ShapeDtypeStruct((SEQ, 512), BF), jax.ShapeDtypeStruct((SEQ, 512), BF),
                   jax.ShapeDtypeStruct((SEQ, D_MODEL), BF)],
        scratch_shapes=[pltpu.VMEM((2, 128, 128), F32)],
        compiler_params=_cp(("parallel", "arbitrary")),
    )(log_gamma, proj, proj, proj, cos, sin_s, dret, states)


def _stack_heads(v):
    return jnp.concatenate([jnp.where(_head_lanes(v.shape, hh), v, jnp.zeros_like(v)) for hh in range(2)], axis=0)


def _unstack_heads(v):
    t = v.shape[0] // 2
    return jnp.where(_head_lanes((t, v.shape[1]), 0), v[:t], v[t:])


def _sb_masks(t):
    rr = lax.broadcasted_iota(jnp.int32, (t, t), 0)
    cc = lax.broadcasted_iota(jnp.int32, (t, t), 1)
    r2 = lax.broadcasted_iota(jnp.int32, (2 * t, t), 0) & (t - 1)
    c2 = lax.broadcasted_iota(jnp.int32, (2 * t, t), 1)
    return rr, cc, c2 < r2


def _split_dot2(v, tri):
    hi = v.astype(BF)
    lo = (v - hi.astype(F32)).astype(BF)
    both = _dot(jnp.concatenate([hi, lo], axis=0), tri)
    return both[:v.shape[0]] + both[v.shape[0]:]


def _log_one_minus_beta(z):
    return -(jnp.maximum(z, 0.0) + jnp.log(1.0 + jnp.exp(-jnp.abs(z))))


def _sb_fwd(proj):
    t = SB_BLOCK
    nq = SEQ // t

    def body(q_ref, k_ref, v_ref, o_ref, tot_ref, kt_ref):
        i = pl.program_id(1)

        @pl.when(i == 0)
        def _():
            for jj in range(nq):
                kt_ref[jj] = k_ref[jj * t:(jj + 1) * t, :].T

        q2 = _stack_heads((q_ref[...].astype(F32) * QK_SCALE).astype(BF))
        rr, cc, valid = _sb_masks(t)
        later = (rr > cc).astype(BF)

        def tile(j, carry, diagonal):
            acc, run = carry
            z = _dot(q2, kt_ref[j])
            lm = _log_one_minus_beta(z)
            if diagonal:
                lm = jnp.where(valid, lm, 0.0)
            after = _split_dot2(lm, later)
            a = jnp.exp(z + lm + after + run)
            if diagonal:
                a = jnp.where(valid, a, 0.0)
            vb = v_ref[pl.ds(pl.multiple_of(j * t, t), t), :]
            return acc + _dot(a.astype(BF), vb), run + after[:, 0:1] + lm[:, 0:1]

        carry = tile(i, (jnp.zeros((2 * t, 128), F32), jnp.zeros((2 * t, 1), F32)), True)
        acc, run = lax.fori_loop(0, i, lambda s, cr: tile(i - 1 - s, cr, False), carry)
        o_ref[...] = _unstack_heads(acc).astype(BF)
        tot_ref[...] = _unstack_heads(jnp.broadcast_to(run, (2 * t, 128)))

    return pl.pallas_call(
        body, name="sb_fwd", grid=(N_PAIRS, nq),
        in_specs=[pl.BlockSpec((t, 128), lambda p, i: (i, C_QS // 128 + p)),
                  pl.BlockSpec((SEQ, 128), lambda p, i: (0, C_KS // 128 + p)),
                  pl.BlockSpec((SEQ, 128), lambda p, i: (0, C_VS // 128 + p))],
        out_specs=[pl.BlockSpec((t, 128), lambda p, i: (i, p))] * 2,
        out_shape=[jax.ShapeDtypeStruct((SEQ, 512), BF), jax.ShapeDtypeStruct((SEQ, 512), F32)],
        scratch_shapes=[pltpu.VMEM((nq, 128, t), BF)],
        compiler_params=_cp(("parallel", "arbitrary")),
    )(proj, proj, proj)


def _sb_bwd(proj, dsb, tot, dep=None):
    t = SB_BLOCK
    nq = SEQ // t

    def body(q_ref, k_ref, v_ref, do_ref, tot_ref, dq_ref, dk_ref, dv_ref, kt_ref, vt_ref, dkt_acc, dvt_acc):
        i = pl.program_id(1)

        @pl.when(i == 0)
        def _():
            dkt_acc[...] = jnp.zeros_like(dkt_acc)
            dvt_acc[...] = jnp.zeros_like(dvt_acc)
            for jj in range(nq):
                kt_ref[jj] = k_ref[jj * t:(jj + 1) * t, :].T
                vt_ref[jj] = v_ref[jj * t:(jj + 1) * t, :].T

        q2 = _stack_heads((q_ref[...].astype(F32) * QK_SCALE).astype(BF))
        do2 = _stack_heads(do_ref[...])
        q2t, do2t = q2.T, do2.T
        tots = tot_ref[...]
        total = jnp.concatenate([tots[:, 0:1], tots[:, 64:65]], axis=0)
        rr, cc, valid = _sb_masks(t)
        upto = (rr <= cc).astype(BF)
        before = (rr < cc).astype(BF)

        def tile(j, carry, diagonal):
            dq, run_l, run_g = carry
            z = _dot(q2, kt_ref[j])
            lm = _log_one_minus_beta(z)
            if diagonal:
                lm = jnp.where(valid, lm, 0.0)
            incl = _split_dot2(lm, upto)
            a = jnp.exp(z + lm + (total - (incl + run_l)))
            if diagonal:
                a = jnp.where(valid, a, 0.0)
            g = a * _dot(do2, vt_ref[j])
            excl = _split_dot2(g, before)
            dz = g * jnp.exp(lm) - (excl + run_g) * jnp.exp(z + lm)
            if diagonal:
                dz = jnp.where(valid, dz, 0.0)
            dzb = dz.astype(BF)
            dkt_acc[j] += _dot(q2t, dzb)
            dvt_acc[j] += _dot(do2t, a.astype(BF))
            kb = k_ref[pl.ds(pl.multiple_of(j * t, t), t), :]
            return (dq + _dot(dzb, kb), run_l + incl[:, t - 1:t], run_g + excl[:, t - 1:t] + g[:, t - 1:t])

        zero = jnp.zeros((2 * t, 1), F32)
        carry = lax.fori_loop(0, i, lambda j, cr: tile(j, cr, False), (jnp.zeros((2 * t, 128), F32), zero, zero))
        dq = tile(i, carry, True)[0]
        dq_ref[...] = (_unstack_heads(dq) * QK_SCALE).astype(BF)

        @pl.when(i == nq - 1)
        def _():
            for jj in range(nq):
                dk_ref[jj * t:(jj + 1) * t, :] = dkt_acc[jj].T.astype(BF)
                dv_ref[jj * t:(jj + 1) * t, :] = dvt_acc[jj].T.astype(BF)

    tile_spec = pl.BlockSpec((t, 128), lambda p, i: (i, p))
    col_spec = pl.BlockSpec((SEQ, 128), lambda p, i: (0, p))
    shp = jax.ShapeDtypeStruct((SEQ, 512), BF)
    body, in_specs, args = _add_dep(
        body, [pl.BlockSpec((t, 128), lambda p, i: (i, C_QS // 128 + p)),
               pl.BlockSpec((SEQ, 128), lambda p, i: (0, C_KS // 128 + p)),
               pl.BlockSpec((SEQ, 128), lambda p, i: (0, C_VS // 128 + p)),
               tile_spec, tile_spec],
        [proj, proj, proj, dsb, tot], dep)
    return pl.pallas_call(
        body, name="sb_bwd", grid=(N_PAIRS, nq),
        in_specs=in_specs,
        out_specs=[tile_spec, col_spec, col_spec],
        out_shape=[shp, shp, shp],
        scratch_shapes=[pltpu.VMEM((nq, 128, t), BF), pltpu.VMEM((nq, 128, t), BF),
                        pltpu.VMEM((nq, 128, t), F32), pltpu.VMEM((nq, 128, t), F32)],
        compiler_params=_cp(("parallel", "arbitrary")),
    )(*args)


def _mix(retg, sb, proj, w_ret, w_sb4):
    tm, tn = 512, 256

    def body(r_ref, s_ref, ar_ref, as_ref, wr_ref, ws_ref, mix_ref, rb_ref, sbp_ref):
        rb = _dot(r_ref[...], wr_ref[...])
        sbp = _dot(s_ref[...], ws_ref[...])
        mix = _sigmoid(ar_ref[...].astype(F32)) * rb + _sigmoid(as_ref[...].astype(F32)) * sbp
        mix_ref[...] = mix.astype(BF)
        rb_ref[...] = rb.astype(BF)
        sbp_ref[...] = sbp.astype(BF)

    out = pl.BlockSpec((tm, tn), lambda j, i: (i, j))
    shp = jax.ShapeDtypeStruct((SEQ, D_MODEL), BF)
    return pl.pallas_call(
        body, name="mix", grid=(D_MODEL // tn, SEQ // tm),
        in_specs=[pl.BlockSpec((tm, D_MODEL), lambda j, i: (i, 0)),
                  pl.BlockSpec((tm, 512), lambda j, i: (i, 0)),
                  pl.BlockSpec((tm, tn), lambda j, i: (i, C_AR // tn + j)),
                  pl.BlockSpec((tm, tn), lambda j, i: (i, C_AS // tn + j)),
                  pl.BlockSpec((D_MODEL, tn), lambda j, i: (0, j)),
                  pl.BlockSpec((None, 512, tn), lambda j, i: (j, 0, 0))],
        out_specs=[out, out, out], out_shape=[shp, shp, shp],
        compiler_params=_cp(("parallel", "parallel")),
    )(retg, sb, proj, proj, w_ret, w_sb4)


def _out_proj(mixed, x, vecs, w_out):
    tm = 256

    def body(m_ref, x_ref, vec_ref, w_ref, y_ref, h1_ref, h2_ref):
        y = _dot(m_ref[...], w_ref[...])
        h1 = x_ref[...] + _row(vec_ref, V_GT1) * (y * _rms(y)) * _row(vec_ref, V_G2)
        g = _row(vec_ref, V_G3) * (1.0 + _row(vec_ref, V_SC2))
        y_ref[...] = y
        h1_ref[...] = h1
        h2_ref[...] = (h1 * _rms(h1) * g + _row(vec_ref, V_SH2)).astype(BF)

    row = pl.BlockSpec((tm, D_MODEL), lambda i: (i, 0))
    f32 = jax.ShapeDtypeStruct((SEQ, D_MODEL), F32)
    return pl.pallas_call(
        body, name="out_proj", grid=(SEQ // tm,),
        in_specs=[row, row, pl.BlockSpec((16, D_MODEL), lambda i: (0, 0)),
                  pl.BlockSpec((D_MODEL, D_MODEL), lambda i: (0, 0))],
        out_specs=[row, row, row],
        out_shape=[f32, f32, jax.ShapeDtypeStruct((SEQ, D_MODEL), BF)],
        compiler_params=_cp(("parallel",)),
    )(mixed, x, vecs, w_out)


def _ffn_up(h2, w_ff14):
    tm = 512

    def body(h_ref, w_ref, u_ref, a_ref):
        u = _dot(h_ref[...], w_ref[...])
        r = jnp.maximum(u, 0.0)
        u_ref[...] = u.astype(BF)
        a_ref[...] = (r * r).astype(BF)

    out = pl.BlockSpec((tm, D_MODEL), lambda j, i: (i, j))
    shp = jax.ShapeDtypeStruct((SEQ, D_FF), BF)
    return pl.pallas_call(
        body, name="ffn_up", grid=(N_CHIPS, SEQ // tm),
        in_specs=[pl.BlockSpec((tm, D_MODEL), lambda j, i: (i, 0)),
                  pl.BlockSpec((None, D_MODEL, D_MODEL), lambda j, i: (j, 0, 0))],
        out_specs=[out, out], out_shape=[shp, shp],
        compiler_params=_cp(("parallel", "parallel")),
    )(h2, w_ff14)


def _ffn_down_loss(act, h1, target, vecs, w_ff2):
    tm = 256

    def body(a_ref, h1_ref, t_ref, vec_ref, w_ref, dout_ref, df_ref, st_ref):
        @pl.when(pl.program_id(0) == 0)
        def _():
            st_ref[...] = jnp.zeros_like(st_ref)

        f = _dot(a_ref[...], w_ref[...])
        r4 = _rms(f)
        fn = f * r4
        gt2, g4 = _row(vec_ref, V_GT2), _row(vec_ref, V_G4)
        diff = h1_ref[...] + gt2 * fn * g4 - t_ref[...]
        dout = diff * (1.0 / D_MODEL)
        dfn = dout * gt2 * g4
        dout_ref[...] = dout
        df_ref[...] = (r4 * (dfn - fn * _rowmean(dfn * fn))).astype(BF)
        st_ref[0:1, :] += _colsum(dout * fn * g4)
        st_ref[1:2, :] += _colsum(dout * gt2 * fn)
        st_ref[2:3, :] += _colsum(diff * diff) * (0.5 / D_MODEL)

    row = pl.BlockSpec((tm, D_MODEL), lambda i: (i, 0))
    return pl.pallas_call(
        body, name="ffn_down_loss", grid=(SEQ // tm,),
        in_specs=[pl.BlockSpec((tm, D_FF), lambda i: (i, 0)), row, row,
                  pl.BlockSpec((16, D_MODEL), lambda i: (0, 0)),
                  pl.BlockSpec((D_FF, D_MODEL), lambda i: (0, 0))],
        out_specs=[row, row, pl.BlockSpec((8, D_MODEL), lambda i: (0, 0))],
        out_shape=[jax.ShapeDtypeStruct((SEQ, D_MODEL), F32), jax.ShapeDtypeStruct((SEQ, D_MODEL), BF),
                   jax.ShapeDtypeStruct((8, D_MODEL), F32)],
        compiler_params=_cp(("arbitrary",)),
    )(act, h1, target, vecs, w_ff2)


def _ffn_down_bwd(df, u, w_ff2):
    tm = 256

    def body(df_ref, u_ref, w_ref, du_ref):
        da = _dot_nt(df_ref[...], w_ref[...])
        du_ref[...] = (da * (2.0 * jnp.maximum(u_ref[...].astype(F32), 0.0))).astype(BF)

    return pl.pallas_call(
        body, name="ffn_down_bwd", grid=(D_FF // 1024, SEQ // tm),
        in_specs=[pl.BlockSpec((tm, D_MODEL), lambda j, i: (i, 0)),
                  pl.BlockSpec((tm, 1024), lambda j, i: (i, j)),
                  pl.BlockSpec((1024, D_MODEL), lambda j, i: (j, 0))],
        out_specs=pl.BlockSpec((tm, 1024), lambda j, i: (i, j)),
        out_shape=jax.ShapeDtypeStruct((SEQ, D_FF), BF),
        compiler_params=_cp(("parallel", "parallel")),
    )(df, u, w_ff2)


def _ffn_up_bwd(du, h1, y, dout, vecs, w_ff14):
    tm = 256

    def body(du_ref, h1_ref, y_ref, dout_ref, vec_ref, w_ref, dh1_ref, dy_ref, st_ref, acc):
        kk = pl.program_id(1)

        @pl.when((pl.program_id(0) == 0) & (kk == 0))
        def _():
            st_ref[...] = jnp.zeros_like(st_ref)

        part = _dot_nt(du_ref[...], w_ref[...])

        @pl.when(kk == 0)
        def _():
            acc[...] = part

        @pl.when(kk > 0)
        def _():
            acc[...] += part

        @pl.when(kk == N_CHIPS - 1)
        def _():
            dh2 = acc[...]
            h1 = h1_ref[...]
            r3 = _rms(h1)
            hn3 = h1 * r3
            g3, sc2 = _row(vec_ref, V_G3), _row(vec_ref, V_SC2)
            dhn3 = dh2 * g3 * (1.0 + sc2)
            dh1 = dout_ref[...] + r3 * (dhn3 - hn3 * _rowmean(dhn3 * hn3))
            y = y_ref[...]
            r2 = _rms(y)
            yn = y * r2
            gt1, g2 = _row(vec_ref, V_GT1), _row(vec_ref, V_G2)
            dyn = dh1 * gt1 * g2
            dh1_ref[...] = dh1
            dy_ref[...] = (r2 * (dyn - yn * _rowmean(dyn * yn))).astype(BF)
            st_ref[0:1, :] += _colsum(dh2)
            st_ref[1:2, :] += _colsum(dh2 * hn3 * g3)
            st_ref[2:3, :] += _colsum(dh2 * hn3 * (1.0 + sc2))
            st_ref[3:4, :] += _colsum(dh1 * yn * g2)
            st_ref[4:5, :] += _colsum(dh1 * gt1 * yn)

    row = pl.BlockSpec((tm, D_MODEL), lambda i, k: (i, 0))
    return pl.pallas_call(
        body, name="ffn_up_bwd", grid=(SEQ // tm, N_CHIPS),
        in_specs=[pl.BlockSpec((tm, D_MODEL), lambda i, k: (i, k)), row, row, row,
                  pl.BlockSpec((16, D_MODEL), lambda i, k: (0, 0)),
                  pl.BlockSpec((None, D_MODEL, D_MODEL), lambda i, k: (k, 0, 0))],
        out_specs=[row, row, pl.BlockSpec((8, D_MODEL), lambda i, k: (0, 0))],
        out_shape=[jax.ShapeDtypeStruct((SEQ, D_MODEL), F32), jax.ShapeDtypeStruct((SEQ, D_MODEL), BF),
                   jax.ShapeDtypeStruct((8, D_MODEL), F32)],
        scratch_shapes=[pltpu.VMEM((tm, D_MODEL), F32)],
        compiler_params=_cp(("arbitrary", "arbitrary")),
    )(du, h1, y, dout, vecs, w_ff14)


def _mix_bwd(dy, proj, rb, sbp, w_out, w_sb4):
    tm, tn = 512, 256

    def body(dy_ref, ar_ref, as_ref, rb_ref, sbp_ref, wo_ref, ws_ref,
             drb_ref, dsbp_ref, dar_ref, das_ref, dsb_ref, acc):
        j = pl.program_id(1)
        dm = _dot_nt(dy_ref[...], wo_ref[...])
        sr = _sigmoid(ar_ref[...].astype(F32))
        ss = _sigmoid(as_ref[...].astype(F32))
        dsbp = (dm * ss).astype(BF)
        drb_ref[...] = (dm * sr).astype(BF)
        dsbp_ref[...] = dsbp
        dar_ref[...] = (dm * rb_ref[...].astype(F32) * sr * (1.0 - sr)).astype(BF)
        das_ref[...] = (dm * sbp_ref[...].astype(F32) * ss * (1.0 - ss)).astype(BF)
        part = _dot_nt(dsbp, ws_ref[...])

        @pl.when(j == 0)
        def _():
            acc[...] = part

        @pl.when(j > 0)
        def _():
            acc[...] += part

        @pl.when(j == D_MODEL // tn - 1)
        def _():
            dsb_ref[...] = acc[...].astype(BF)

    tile = pl.BlockSpec((tm, tn), lambda i, j: (i, j))
    shp = jax.ShapeDtypeStruct((SEQ, D_MODEL), BF)
    return pl.pallas_call(
        body, name="mix_bwd", grid=(SEQ // tm, D_MODEL // tn),
        in_specs=[pl.BlockSpec((tm, D_MODEL), lambda i, j: (i, 0)),
                  pl.BlockSpec((tm, tn), lambda i, j: (i, C_AR // tn + j)),
                  pl.BlockSpec((tm, tn), lambda i, j: (i, C_AS // tn + j)),
                  tile, tile,
                  pl.BlockSpec((tn, D_MODEL), lambda i, j: (j, 0)),
                  pl.BlockSpec((None, 512, tn), lambda i, j: (j, 0, 0))],
        out_specs=[tile, tile, tile, tile, pl.BlockSpec((tm, 512), lambda i, j: (i, 0))],
        out_shape=[shp, shp, shp, shp, jax.ShapeDtypeStruct((SEQ, 512), BF)],
        scratch_shapes=[pltpu.VMEM((tm, 512), F32)],
        compiler_params=_cp(("parallel", "arbitrary")),
    )(dy, proj, proj, rb, sbp, w_out, w_sb4)


def _ret_branch_bwd(drb, proj, o_raw, gn_g, w_ret):
    tm, tn = 512, 256

    def body(d_ref, g_ref, o_ref, gn_ref, w_ref, dret_ref, dgr_ref, st_ref):
        @pl.when(pl.program_id(1) == 0)
        def _():
            st_ref[...] = jnp.zeros_like(st_ref)

        dretg = _dot_nt(d_ref[...], w_ref[...])
        for gi in range(tn // 128):
            cols = slice(128 * gi, 128 * (gi + 1))
            o = o_ref[:, cols]
            d = o - _rowmean(o)
            rstd = lax.rsqrt(_rowmean(d * d) + EPS)
            nh = d * rstd
            gain = gn_ref[:, cols]
            gr = g_ref[:, cols].astype(F32)
            sg = _sigmoid(gr)
            dg = dretg[:, cols]
            dgn = dg * gr * sg
            dnh = dgn * gain
            dgr_ref[:, cols] = (dg * nh * gain * sg * (1.0 + gr * (1.0 - sg))).astype(BF)
            dret_ref[:, cols] = (rstd * (dnh - _rowmean(dnh) - nh * _rowmean(dnh * nh))).astype(BF)
            st_ref[0:1, cols] += _colsum(dgn * nh)

    tile = pl.BlockSpec((tm, tn), lambda j, i: (i, j))
    shp = jax.ShapeDtypeStruct((SEQ, D_MODEL), BF)
    return pl.pallas_call(
        body, name="ret_branch_bwd", grid=(D_MODEL // tn, SEQ // tm),
        in_specs=[pl.BlockSpec((tm, D_MODEL), lambda j, i: (i, 0)),
                  pl.BlockSpec((tm, tn), lambda j, i: (i, C_GR // tn + j)),
                  tile, pl.BlockSpec((1, tn), lambda j, i: (0, j)),
                  pl.BlockSpec((tn, D_MODEL), lambda j, i: (j, 0))],
        out_specs=[tile, tile, pl.BlockSpec((8, tn), lambda j, i: (0, j))],
        out_shape=[shp, shp, jax.ShapeDtypeStruct((8, D_MODEL), F32)],
        compiler_params=_cp(("parallel", "arbitrary")),
    )(drb, proj, o_raw, gn_g, w_ret)


def _in_proj_bwd(dproj, x, dh1, vecs, w_in4):
    tm = 512
    wc = w_in4.shape[2]

    def body(dp_ref, x_ref, dh1_ref, vec_ref, w_ref, dx_ref, st_ref, acc):
        kk = pl.program_id(1)

        @pl.when((pl.program_id(0) == 0) & (kk == 0))
        def _():
            st_ref[...] = jnp.zeros_like(st_ref)

        part = _dot_nt(dp_ref[...], w_ref[...])

        @pl.when(kk == 0)
        def _():
            acc[...] = part

        @pl.when(kk > 0)
        def _():
            acc[...] += part

        @pl.when(kk == N_CHIPS - 1)
        def _():
            dh = acc[...]
            xx = x_ref[...]
            r1 = _rms(xx)
            xn = xx * r1
            g1, sc1 = _row(vec_ref, V_G1), _row(vec_ref, V_SC1)
            dxn = dh * g1 * (1.0 + sc1)
            dx_ref[...] = dh1_ref[...] + r1 * (dxn - xn * _rowmean(dxn * xn))
            st_ref[0:1, :] += _colsum(dh)
            st_ref[1:2, :] += _colsum(dh * xn * g1)
            st_ref[2:3, :] += _colsum(dh * xn * (1.0 + sc1))

    row = pl.BlockSpec((tm, D_MODEL), lambda i, k: (i, 0))
    return pl.pallas_call(
        body, name="in_proj_bwd", grid=(SEQ // tm, N_CHIPS),
        in_specs=[pl.BlockSpec((tm, wc), lambda i, k: (i, k)), row, row,
                  pl.BlockSpec((16, D_MODEL), lambda i, k: (0, 0)),
                  pl.BlockSpec((None, D_MODEL, wc), lambda i, k: (k, 0, 0))],
        out_specs=[row, pl.BlockSpec((8, D_MODEL), lambda i, k: (0, 0))],
        out_shape=[jax.ShapeDtypeStruct((SEQ, D_MODEL), F32), jax.ShapeDtypeStruct((8, D_MODEL), F32)],
        scratch_shapes=[pltpu.VMEM((tm, D_MODEL), F32)],
        compiler_params=_cp(("arbitrary", "arbitrary")),
    )(dproj, x, dh1, vecs, w_in4)


def _weight_grad(a, b, ta, tb, col_sharded, name, dep=None):
    ka, nb_ = a.shape[1], b.shape[1]

    def body(a_ref, b_ref, o_ref):
        o_ref[...] = _dot_tn(a_ref[...], b_ref[...])

    body, in_specs, args = _add_dep(
        body, [pl.BlockSpec((SEQ, ta), lambda i, j: (0, i)), pl.BlockSpec((SEQ, tb), lambda i, j: (0, j))],
        [a, b], dep)

    if col_sharded:
        per = nb_ // N_CHIPS // tb
        out_shape = jax.ShapeDtypeStruct((N_CHIPS, ka, nb_ // N_CHIPS), F32)
        out_spec = pl.BlockSpec((None, ta, tb), lambda i, j: (j // per, i, j % per))
    else:
        per = ka // N_CHIPS // ta
        out_shape = jax.ShapeDtypeStruct((N_CHIPS, ka // N_CHIPS, nb_), F32)
        out_spec = pl.BlockSpec((None, ta, tb), lambda i, j: (i // per, i % per, j))
    return pl.pallas_call(
        body, name=name, grid=(ka // ta, nb_ // tb),
        in_specs=in_specs, out_specs=out_spec, out_shape=out_shape,
        compiler_params=_cp(("parallel", "parallel")),
    )(*args)


def _rope_constants():
    freq = np.float32(ROPE_BASE) ** (-np.arange(0, 64, 2, dtype=np.float32) / np.float32(64))
    inv = np.tile(freq.astype(np.float32), 4).reshape(1, 128)
    sign = np.tile(np.concatenate([-np.ones(32, np.float32), np.ones(32, np.float32)]), 2).reshape(1, 128)
    return jnp.asarray(inv), jnp.asarray(sign)


def _log_gamma():
    return jnp.asarray(np.log1p(-(2.0 ** (-5.0 - np.arange(8, dtype=np.float64)))).astype(np.float32))


def _halves(g):
    return g.reshape(N_CHIPS, 2, g.shape[1] // 2, g.shape[2])


def kernel(x, c, positions, ada_w, ada_b, pre_mix_g, post_mix_g, pre_ffn_g, post_ffn_g, w_in, ret_gn_g, w_ret_branch, w_sb_branch, w_out, w_ff1, w_ff2, loss_target, m_ada_w, m_ada_b, m_pre_mix_g, m_post_mix_g, m_pre_ffn_g, m_post_ffn_g, m_w_in, m_ret_gn_g, m_w_ret_branch, m_w_sb_branch, m_w_out, m_w_ff1, m_w_ff2, v_ada_w, v_ada_b, v_pre_mix_g, v_post_mix_g, v_pre_ffn_g, v_post_ffn_g, v_w_in, v_ret_gn_g, v_w_ret_branch, v_w_sb_branch, v_w_out, v_w_ff1, v_w_ff2):
    names = ["w_in", "w_ret", "w_sb", "w_out", "w_ff1", "w_ff2"]
    big = dict(zip(names, [w_in, w_ret_branch, w_sb_branch, w_out, w_ff1, w_ff2]))
    big_m = dict(zip(names, [m_w_in, m_w_ret_branch, m_w_sb_branch, m_w_out, m_w_ff1, m_w_ff2]))
    big_v = dict(zip(names, [v_w_in, v_w_ret_branch, v_w_sb_branch, v_w_out, v_w_ff1, v_w_ff2]))
    rest = names[1:]
    cidx = lax.axis_index("c").astype(jnp.int32).reshape(1)
    kidx = (2 * lax.axis_index("x") + lax.axis_index("y")).astype(jnp.int32).reshape(1)
    x0, target = x[0], loss_target[0]

    cast = {nm: _cast_bf16(big[nm][0], kidx, "cast_" + nm) for nm in names}
    buf_in, sem_in, tok_in = _gather_start("gather_in_start", [cast["w_in"]])
    buf_rest, sem_rest, tok_rest = _gather_start("gather_rest_start", [cast[nm] for nm in rest])

    c_all, mod4 = _mod_exchange(_tie(tok_rest, _tie(tok_in, c)), ada_w[0], ada_b.reshape(N_CHIPS, -1))
    vecs = jnp.concatenate([mod4.reshape(6, D_MODEL), pre_mix_g, post_mix_g, pre_ffn_g, post_ffn_g,
                            jnp.zeros((6, D_MODEL), F32)], axis=0)
    inv_freq, sign = _rope_constants()
    lg = _log_gamma()
    cos, sin_s = _rope_tables(positions.reshape(SEQ, 1), inv_freq, sign)
    buf_in, sem_in, tok_in = _gather_pass("gather_in_pass", buf_in, sem_in, vecs)
    (w_in4,) = _gather_finish("gather_in_finish", buf_in, sem_in, tok_in)

    h, proj = _ln_proj(x0, vecs, w_in4)
    sb, tot = _sb_fwd(proj)
    buf_rest, sem_rest, tok_rest = _gather_pass("gather_rest_pass", buf_rest, sem_rest, sb)
    o_raw, retg, states = _ret_fwd(proj, cos, sin_s, _tie(tok_rest, ret_gn_g), lg)
    w_ret4, w_sb4, w_out4, w_ff14, w_ff24 = _gather_finish("gather_rest_finish", buf_rest, sem_rest, retg)
    w_ret = w_ret4.reshape(D_MODEL, D_MODEL)
    w_out2 = w_out4.reshape(D_MODEL, D_MODEL)
    w_ff2_2 = w_ff24.reshape(D_FF, D_MODEL)
    mixed, rb, sbp = _mix(retg, sb, proj, w_ret, w_sb4)
    y, h1, h2 = _out_proj(mixed, x0, vecs, w_out2)
    u, act = _ffn_up(h2, w_ff14)
    dout, df, st_a = _ffn_down_loss(act, h1, target, vecs, w_ff2_2)

    du = _ffn_down_bwd(df, u, w_ff2_2)
    grads = {"w_ff2": _weight_grad(act, df, 512, 1024, False, "grad_w_ff2")}
    dh1, dy, st_b = _ffn_up_bwd(du, h1, y, dout, vecs, w_ff14)
    grads["w_ff1"] = _weight_grad(h2, du, 512, 1024, True, "grad_w_ff1")
    drb, dsbp, da_r, da_s, dsb = _mix_bwd(dy, proj, rb, sbp, w_out2, w_sb4)
    grads["w_out"] = _weight_grad(mixed, dy, 256, 1024, False, "grad_w_out")
    dret, dg_r, st_c = _ret_branch_bwd(drb, proj, o_raw, ret_gn_g, w_ret)
    grads["w_ret"] = _weight_grad(retg, drb, 256, 1024, False, "grad_w_ret")
    grads["w_sb"] = _weight_grad(sb, dsbp, 512, 256, True, "grad_w_sb")

    bufs, sems, tok = _pair_send_start("rs_rest_pair_send", [_halves(grads[nm]) for nm in rest])
    dq_r, dk_r, dv_r = _ret_bwd(proj, cos, sin_s, dret, states, _tie(tok, lg))
    mine, theirs = _pair_send_wait("rs_rest_pair_recv", bufs, sems, dq_r)
    pair_sums = [_pair_add(g, r, cidx, "pair_add_" + nm) for g, r, nm in zip(mine, theirs, rest)]
    bufs, sems, tok = _chip_send_start("rs_rest_chip_send", pair_sums)
    dq_s, dk_s, dv_s = _sb_bwd(proj, dsb, tot, dep=tok)
    own, parts = _chip_send_wait("rs_rest_chip_recv", bufs, sems, dq_s)
    sums = [_chip_add(o, p, kidx, cidx, "chip_add_" + nm) for o, p, nm in zip(own, parts, rest)]
    bufs, sems, tok = _pair_swap_start("rs_rest_pair_swap", sums)
    dproj = jnp.concatenate([dq_r, dk_r, dv_r, dg_r, dq_s, dk_s, dv_s, da_r, da_s], axis=1)
    g_in = _weight_grad(h, dproj, 512, 1664, True, "grad_w_in", dep=tok)
    full_rest = _pair_swap_wait("rs_rest_pair_swapped", bufs, sems, g_in)

    bufs, sems, tok = _pair_send_start("rs_in_pair_send", [_halves(g_in)])
    dx, st_d = _in_proj_bwd(dproj, x0, dh1, _tie(tok, vecs), w_in4)
    mine, theirs = _pair_send_wait("rs_in_pair_recv", bufs, sems, dx)
    bufs, sems, tok = _chip_send_start("rs_in_chip_send", [_pair_add(mine[0], theirs[0], cidx, "pair_add_w_in")])
    out = {}
    for nm, g in zip(rest, full_rest):
        w = big[nm][0]
        out[nm] = _adamw(w, big_m[nm][0], big_v[nm][0], g.reshape(w.shape), "adamw_" + nm, dep=tok)

    payload = jnp.concatenate([
        st_d[0:2], st_b[3:4], st_b[0:2], st_a[0:1],
        st_d[2:3], st_b[4:5], st_b[2:3], st_a[1:2],
        st_c[0:1], st_a[2:3]], axis=0)

    def table(b6, g5):
        return jnp.concatenate([b6.reshape(6, D_MODEL)] + g5 + [jnp.zeros((5, D_MODEL), F32)], axis=0)

    wsm = table(ada_b, [pre_mix_g, post_mix_g, pre_ffn_g, post_ffn_g, ret_gn_g])
    msm = table(m_ada_b, [m_pre_mix_g, m_post_mix_g, m_pre_ffn_g, m_post_ffn_g, m_ret_gn_g])
    vsm = table(v_ada_b, [v_pre_mix_g, v_post_mix_g, v_pre_ffn_g, v_post_ffn_g, v_ret_gn_g])
    g_ada, gsm, dsm, mosm, vosm, loss = _small_exchange(
        payload.reshape(N_PAY, 1, D_MODEL), c_all, wsm, msm, vsm)
    ada_out = _adamw(ada_w[0], m_ada_w[0], v_ada_w[0], g_ada, "adamw_ada_w")

    own, parts = _chip_send_wait("rs_in_chip_recv", bufs, sems, ada_out[1])
    bufs, sems, tok = _pair_swap_start(
        "rs_in_pair_swap", [_chip_add(own[0], parts[0], kidx, cidx, "chip_add_w_in")])
    (full_in,) = _pair_swap_wait("rs_in_pair_swapped", bufs, sems, tok)
    out["w_in"] = _adamw(w_in[0], m_w_in[0], v_w_in[0], full_in.reshape(w_in.shape[1:]), "adamw_w_in")

    def unpack(tab):
        return [tab[0:6].reshape(1, 6 * D_MODEL)] + [tab[6 + r:7 + r] for r in range(5)]

    def ordered(which):
        sm = unpack([gsm, dsm, mosm, vosm][which])
        bg = [out[nm][which][None] for nm in names]
        return [ada_out[which][None], sm[0], sm[1], sm[2], sm[3], sm[4], bg[0], sm[5]] + bg[1:]

    return (loss.reshape(()), dx[None], *ordered(0), *ordered(1), *ordered(2), *ordered(3))
```

```python
import functools

import numpy as np
import jax
import jax.numpy as jnp
from jax import lax
from jax.experimental import pallas as pl
from jax.experimental.pallas import tpu as pltpu

SEQ = 2048
D_MODEL = 1024
D_IN = 6656
D_FF = 4096
N_CHIPS = 4
EPS = 1e-6
ROPE_BASE = 10000.0
RET_BLOCK = 256
RET_CHUNK_SHIFT = 6
SB_BLOCK = 256
QK_SCALE = 0.125
N_PAIRS = 4
SB_GROUP = 2

ADAM_LR = 0.001
ADAM_B1 = 0.9
ADAM_B2 = 0.999
ADAM_EPS = 1e-08
ADAM_WD = 0.01
ADAM_STEP = 10

BF = jnp.bfloat16
F32 = jnp.float32
MESH = pl.DeviceIdType.MESH
VMEM_LIMIT = 56 * 1024 * 1024
ANY = pl.BlockSpec(memory_space=pl.ANY)

C_QR, C_KR, C_VR, C_GR, C_QS, C_KS, C_VS, C_AR, C_AS = 0, 512, 1024, 2048, 3072, 3584, 4096, 4608, 5632

V_SH1, V_SC1, V_GT1, V_SH2, V_SC2, V_GT2, V_G1, V_G2, V_G3, V_G4 = range(10)
P_DSH1, P_DSC1, P_DGT1, P_DSH2, P_DSC2, P_DGT2, P_DG1, P_DG2, P_DG3, P_DG4, P_DGN, P_LOSS = range(12)
N_PAY = 12


def _cp(sem=None, **kw):
    if sem is not None:
        kw["dimension_semantics"] = sem
    return pltpu.CompilerParams(vmem_limit_bytes=VMEM_LIMIT, **kw)


def _dot(a, b):
    return jnp.dot(a, b, preferred_element_type=F32)


def _dot_nt(a, b):
    return lax.dot_general(a, b, (((1,), (1,)), ((), ())), preferred_element_type=F32)


def _dot_tn(a, b):
    return lax.dot_general(a, b, (((0,), (0,)), ((), ())), preferred_element_type=F32)


def _row(ref, i):
    return ref[i:i + 1, :]


def _rms(v):
    return lax.rsqrt(jnp.mean(v * v, axis=1, keepdims=True) + EPS)


def _colsum(v):
    return jnp.sum(v, axis=0, keepdims=True)


def _rowmean(v):
    return jnp.mean(v, axis=1, keepdims=True)


def _sigmoid(v):
    return 1.0 / (1.0 + jnp.exp(-v))


def _cast_bf16(w, kidx, dep, name):
    rows, cols = w.shape
    tr = min(rows, 256)

    def body(k_ref, w_ref, dep_ref, o_ref):
        o_ref[...] = w_ref[...].astype(BF)

    return pl.pallas_call(
        body, name=name,
        grid_spec=pltpu.PrefetchScalarGridSpec(
            num_scalar_prefetch=1, grid=(rows // tr,),
            in_specs=[pl.BlockSpec((tr, cols), lambda i, k_ref: (i, 0)), ANY],
            out_specs=pl.BlockSpec((None, tr, cols), lambda i, k_ref: (k_ref[0], i, 0))),
        out_shape=jax.ShapeDtypeStruct((N_CHIPS, rows, cols), BF),
        compiler_params=_cp(("parallel",)),
    )(kidx, w, dep)


def _adamw_math(w, g, m, v):
    m = ADAM_B1 * m + (1.0 - ADAM_B1) * g
    v = ADAM_B2 * v + (1.0 - ADAM_B2) * (g * g)
    m_hat = m / (1.0 - ADAM_B1 ** ADAM_STEP)
    v_hat = v / (1.0 - ADAM_B2 ** ADAM_STEP)
    delta = -ADAM_LR * (m_hat / (jnp.sqrt(v_hat) + ADAM_EPS) + ADAM_WD * w)
    return delta, m, v


def _adamw(w, m, v, g, name, dep=None):
    rows, cols = w.shape
    tr = min(rows, 128)

    def body(w_ref, m_ref, v_ref, g_ref, go_ref, d_ref, mo_ref, vo_ref):
        gg = g_ref[...]
        d, mm, vv = _adamw_math(w_ref[...], gg, m_ref[...], v_ref[...])
        go_ref[...] = gg
        d_ref[...] = d
        mo_ref[...] = mm
        vo_ref[...] = vv

    spec = pl.BlockSpec((tr, cols), lambda i: (i, 0))
    shp = jax.ShapeDtypeStruct((rows, cols), F32)
    body, in_specs, args = _add_dep(body, [spec] * 4, [w, m, v, g], dep)
    return pl.pallas_call(
        body, name=name, grid=(rows // tr,),
        in_specs=in_specs, out_specs=[spec] * 4, out_shape=[shp] * 4,
        compiler_params=_cp(("parallel",)),
    )(*args)


def _place():
    x, y, c = lax.axis_index("x"), lax.axis_index("y"), lax.axis_index("c")
    return x, y, c


HBM = pl.BlockSpec(memory_space=pltpu.HBM)
SEM = pl.BlockSpec(memory_space=pltpu.SEMAPHORE)
EFFECT = pltpu.SideEffectType.DATAFLOW_SIDE_EFFECTING


def _tie(token, small):
    return small + token[0, 0]


def _add_dep(body, in_specs, args, dep):
    if dep is None:
        return body, list(in_specs), list(args)
    n = len(args)

    def wrapped(*refs):
        body(*refs[:n], *refs[n + 1:])

    return wrapped, list(in_specs) + [ANY], list(args) + [dep]


def _split_call(name, bufs, run, old=None, after=None, new=0):
    nb = len(bufs)
    n_in = nb + (3 if old is not None else 0)

    def body(*refs):
        old_sems = (refs[nb], refs[nb + 1]) if old is not None else None
        new_sems = (refs[n_in], refs[n_in + 1]) if new else None
        run(refs[:nb], old_sems, new_sems)
        if new:
            refs[-1][...] = jnp.zeros_like(refs[-1])

    in_specs = [HBM] * nb + ([SEM, SEM, ANY] if old is not None else [])
    out_shape = [pltpu.SemaphoreType.DMA((new,))] * 2 if new else []
    out_specs = [SEM, SEM] if new else []
    out_shape += [pltpu.HBM(b.shape, b.dtype) for b in bufs]
    out_specs += [HBM] * nb
    if new:
        out_shape.append(jax.ShapeDtypeStruct((8, 128), F32))
        out_specs.append(pl.BlockSpec(memory_space=pltpu.VMEM))
    first = 2 if new else 0
    args = [pltpu.with_memory_space_constraint(b, pltpu.HBM) for b in bufs]
    if old is not None:
        args += [old[0], old[1], after]
    outs = pl.pallas_call(
        body, name=name, in_specs=tuple(in_specs), out_specs=tuple(out_specs), out_shape=tuple(out_shape),
        input_output_aliases={i: i + first for i in range(nb)},
        compiler_params=pltpu.CompilerParams(has_side_effects=EFFECT),
    )(*args)
    thru = list(outs[first:first + nb])
    if new:
        return thru, (outs[0], outs[1]), outs[-1]
    return thru, None, None


def _remote(part_src, part_dst, sems, i, to):
    return pltpu.make_async_remote_copy(src_ref=part_src, dst_ref=part_dst, send_sem=sems[0].at[i],
                                        recv_sem=sems[1].at[i], device_id=to, device_id_type=MESH)


def _other_chips(x, y):
    return [(1 - x, y), (x, 1 - y), (1 - x, 1 - y)]


def _gather_start(name, bufs):
    def run(refs, old, new):
        x, y, c = _place()
        k = 2 * x + y
        for w, ref in enumerate(refs):
            rh = bufs[w].shape[1] // 2
            part = ref.at[k, pl.ds(c * rh, rh)]
            for j, (cx, cy) in enumerate(_other_chips(x, y)):
                _remote(part, part, new, 3 * w + j, (cx, cy, c)).start()

    return _split_call(name, bufs, run, new=3 * len(bufs))


def _gather_pass(name, bufs, sems, after):
    def run(refs, old, new):
        x, y, c = _place()
        k = 2 * x + y
        sib = (x, y, 1 - c)
        for w, ref in enumerate(refs):
            rh = bufs[w].shape[1] // 2
            for j, (cx, cy) in enumerate(_other_chips(x, y)):
                land = ref.at[2 * cx + cy, pl.ds(c * rh, rh)]
                _remote(land, land, old, 3 * w + j, (cx, cy, c)).wait_recv()
                _remote(land, land, new, 3 * w + j, sib).start()
        for w, ref in enumerate(refs):
            rh = bufs[w].shape[1] // 2
            part = ref.at[k, pl.ds(c * rh, rh)]
            for j, (cx, cy) in enumerate(_other_chips(x, y)):
                _remote(part, part, old, 3 * w + j, (cx, cy, c)).wait_send()

    return _split_call(name, bufs, run, old=sems, after=after, new=3 * len(bufs))


def _gather_finish(name, bufs, sems, after):
    def run(refs, old, new):
        x, y, c = _place()
        sib = (x, y, 1 - c)
        for w, ref in enumerate(refs):
            rh = bufs[w].shape[1] // 2
            for j, (cx, cy) in enumerate(_other_chips(x, y)):
                sent = ref.at[2 * cx + cy, pl.ds(c * rh, rh)]
                _remote(sent, sent, old, 3 * w + j, sib).wait_send()
                land = ref.at[2 * cx + cy, pl.ds((1 - c) * rh, rh)]
                _remote(land, land, old, 3 * w + j, sib).wait_recv()

    return _split_call(name, bufs, run, old=sems, after=after)[0]


def _pair_send_start(name, grads):
    n = len(grads)
    lands = [lax.empty((N_CHIPS,) + g.shape[2:], F32) for g in grads]

    def run(refs, old, new):
        x, y, c = _place()
        for w in range(n):
            _remote(refs[w].at[:, 1 - c], refs[n + w], new, w, (x, y, 1 - c)).start()

    return _split_call(name, list(grads) + lands, run, new=n)


def _pair_send_wait(name, bufs, sems, after):
    n = len(bufs) // 2

    def run(refs, old, new):
        x, y, c = _place()
        for w in range(n):
            cp = _remote(refs[w].at[:, 1 - c], refs[n + w], old, w, (x, y, 1 - c))
            cp.wait_send()
            cp.wait_recv()

    thru = _split_call(name, bufs, run, old=sems, after=after)[0]
    return thru[:n], thru[n:]


def _pair_add(g, recv, cidx, name):
    _, _, rh, cols = g.shape
    tr = min(rh, 256)

    def body(c_ref, g_ref, r_ref, o_ref):
        o_ref[...] = (g_ref[...] + r_ref[...]).astype(BF)

    return pl.pallas_call(
        body, name=name,
        grid_spec=pltpu.PrefetchScalarGridSpec(
            num_scalar_prefetch=1, grid=(N_CHIPS, rh // tr),
            in_specs=[pl.BlockSpec((None, None, tr, cols), lambda s, i, c_ref: (s, c_ref[0], i, 0)),
                      pl.BlockSpec((None, tr, cols), lambda s, i, c_ref: (s, i, 0))],
            out_specs=pl.BlockSpec((None, tr, cols), lambda s, i, c_ref: (s, i, 0))),
        out_shape=jax.ShapeDtypeStruct((N_CHIPS, rh, cols), BF),
        compiler_params=_cp(("parallel", "parallel")),
    )(cidx, g, recv)


def _chip_send_start(name, sums):
    n = len(sums)
    lands = [lax.empty((3,) + s.shape[1:], BF) for s in sums]

    def run(refs, old, new):
        x, y, c = _place()
        for w in range(n):
            for j, (cx, cy) in enumerate(_other_chips(x, y)):
                _remote(refs[w].at[2 * cx + cy], refs[n + w].at[j], new, 3 * w + j, (cx, cy, c)).start()

    return _split_call(name, list(sums) + lands, run, new=3 * n)


def _chip_send_wait(name, bufs, sems, after):
    n = len(bufs) // 2

    def run(refs, old, new):
        x, y, c = _place()
        for w in range(n):
            for j, (cx, cy) in enumerate(_other_chips(x, y)):
                cp = _remote(refs[w].at[2 * cx + cy], refs[n + w].at[j], old, 3 * w + j, (cx, cy, c))
                cp.wait_send()
                cp.wait_recv()

    thru = _split_call(name, bufs, run, old=sems, after=after)[0]
    return thru[:n], thru[n:]


def _chip_add(own, parts, kidx, cidx, name):
    _, rh, cols = parts.shape
    tr = min(rh, 256)

    def body(k_ref, c_ref, own_ref, p_ref, o_ref):
        acc = own_ref[...].astype(F32)
        for s in range(3):
            acc = acc + p_ref[s].astype(F32)
        o_ref[...] = acc

    return pl.pallas_call(
        body, name=name,
        grid_spec=pltpu.PrefetchScalarGridSpec(
            num_scalar_prefetch=2, grid=(rh // tr,),
            in_specs=[pl.BlockSpec((None, tr, cols), lambda i, k_ref, c_ref: (k_ref[0], i, 0)),
                      pl.BlockSpec((3, tr, cols), lambda i, k_ref, c_ref: (0, i, 0))],
            out_specs=pl.BlockSpec((None, tr, cols), lambda i, k_ref, c_ref: (c_ref[0], i, 0))),
        out_shape=jax.ShapeDtypeStruct((2, rh, cols), F32),
        compiler_params=_cp(("parallel",)),
    )(kidx, cidx, own, parts)


def _pair_swap_start(name, bufs):
    def run(refs, old, new):
        x, y, c = _place()
        for w, ref in enumerate(refs):
            _remote(ref.at[c], ref.at[c], new, w, (x, y, 1 - c)).start()

    return _split_call(name, bufs, run, new=len(bufs))


def _pair_swap_wait(name, bufs, sems, after):
    def run(refs, old, new):
        x, y, c = _place()
        for w, ref in enumerate(refs):
            _remote(ref.at[c], ref.at[c], old, w, (x, y, 1 - c)).wait_send()
            _remote(ref.at[1 - c], ref.at[1 - c], old, w, (x, y, 1 - c)).wait_recv()

    return _split_call(name, bufs, run, old=sems, after=after)[0]


def _peers(x, y, c):
    out = []
    for code in range(1, 8):
        fx, fy, fc = (code >> 2) & 1, (code >> 1) & 1, code & 1
        px = 1 - x if fx else x
        py = 1 - y if fy else y
        pc = 1 - c if fc else c
        out.append((code, (px, py, pc)))
    return out


def _mod_exchange(c_row, ada_w, ada_b4):
    ncol = ada_w.shape[1]

    def body(c_ref, w_ref, b_ref, call_ref, mod_ref, part_ref, send_sems, recv_sems):
        x, y, c = _place()
        k = 2 * x + y
        me = 4 * x + 2 * y + c
        call_ref[pl.ds(me, 1), :] = c_ref[...]
        sends = []
        for code, peer in _peers(x, y, c):
            cp = pltpu.make_async_remote_copy(
                src_ref=c_ref, dst_ref=call_ref.at[pl.ds(me, 1), :],
                send_sem=send_sems.at[code], recv_sem=recv_sems.at[code],
                device_id=peer, device_id_type=MESH)
            cp.start()
            sends.append(cp)
        for code, (px, py, pc) in _peers(x, y, c):
            land = call_ref.at[pl.ds(4 * px + 2 * py + pc, 1), :]
            pltpu.make_async_remote_copy(
                src_ref=land, dst_ref=land, send_sem=send_sems.at[code], recv_sem=recv_sems.at[code],
                device_id=(px, py, pc), device_id_type=MESH).wait_recv()
        call = call_ref[...]
        act = call * _sigmoid(call)
        part = jnp.dot(act, w_ref[...], preferred_element_type=F32,
                       precision=lax.Precision.HIGHEST) + b_ref[pl.ds(k, 1), :]
        part_ref[...] = part
        mod_ref[pl.ds(k, 1), :] = part_ref[pl.ds(me, 1), :]
        chips = [(8 + j, peer) for j, (code, peer) in enumerate(_peers(x, y, c)) if code in (2, 4, 6)]
        for slot, (px, py, pc) in chips:
            cp = pltpu.make_async_remote_copy(
                src_ref=part_ref.at[pl.ds(4 * px + 2 * py + pc, 1), :], dst_ref=mod_ref.at[pl.ds(k, 1), :],
                send_sem=send_sems.at[slot], recv_sem=recv_sems.at[slot],
                device_id=(px, py, pc), device_id_type=MESH)
            cp.start()
            sends.append(cp)
        for slot, (px, py, pc) in chips:
            land = mod_ref.at[pl.ds(2 * px + py, 1), :]
            pltpu.make_async_remote_copy(
                src_ref=land, dst_ref=land, send_sem=send_sems.at[slot], recv_sem=recv_sems.at[slot],
                device_id=(px, py, pc), device_id_type=MESH).wait_recv()
        for cp in sends:
            cp.wait_send()

    vm = pl.BlockSpec(memory_space=pltpu.VMEM)
    return pl.pallas_call(
        body, name="mod_exchange",
        in_specs=[vm, vm, vm], out_specs=[vm, vm],
        out_shape=[jax.ShapeDtypeStruct((8, D_MODEL), F32), jax.ShapeDtypeStruct((N_CHIPS, ncol), F32)],
        scratch_shapes=[pltpu.VMEM((8, ncol), F32), pltpu.SemaphoreType.DMA((16,)),
                        pltpu.SemaphoreType.DMA((16,))],
        compiler_params=_cp(),
    )(c_row, ada_w, ada_b4)


def _small_exchange(payload, c_all, wsm, msm, vsm):
    ncol = 6 * D_MODEL // N_CHIPS

    def body(p_ref, call_ref, w_ref, m_ref, v_ref, gw_ref, g_ref, d_ref, mo_ref, vo_ref, loss_ref,
             all_ref, dm_ref, send_sems, recv_sems):
        x, y, c = _place()
        k = 2 * x + y
        me = 4 * x + 2 * y + c
        all_ref[:, pl.ds(me, 1), :] = p_ref[...]
        sends = []
        for code, peer in _peers(x, y, c):
            cp = pltpu.make_async_remote_copy(
                src_ref=p_ref, dst_ref=all_ref.at[:, pl.ds(me, 1), :],
                send_sem=send_sems.at[code], recv_sem=recv_sems.at[code],
                device_id=peer, device_id_type=MESH)
            cp.start()
            sends.append(cp)
        for code, (px, py, pc) in _peers(x, y, c):
            land = all_ref.at[:, pl.ds(4 * px + 2 * py + pc, 1), :]
            pltpu.make_async_remote_copy(
                src_ref=land, dst_ref=land, send_sem=send_sems.at[code], recv_sem=recv_sems.at[code],
                device_id=(px, py, pc), device_id_type=MESH).wait_recv()
        for cp in sends:
            cp.wait_send()
        tot = [_colsum(all_ref[r]) for r in range(N_PAY)]
        loss_ref[...] = jnp.sum(tot[P_LOSS], axis=1, keepdims=True)
        g_ref[...] = jnp.zeros_like(g_ref)
        for r in range(P_LOSS):
            g_ref[r:r + 1, :] = tot[r]
        g = g_ref[...]
        d, mm, vv = _adamw_math(w_ref[...], g, m_ref[...], v_ref[...])
        d_ref[...] = d
        mo_ref[...] = mm
        vo_ref[...] = vv
        half = D_MODEL // 2
        for kk in range(N_CHIPS):
            @pl.when(k == kk)
            def _():
                r0 = 3 * (kk // 2)
                if kk % 2 == 0:
                    dm_ref[:, :D_MODEL] = all_ref[r0]
                    dm_ref[:, D_MODEL:] = all_ref[r0 + 1][:, :half]
                else:
                    dm_ref[:, :half] = all_ref[r0 + 1][:, half:]
                    dm_ref[:, half:] = all_ref[r0 + 2]
        call = call_ref[...]
        act = call * _sigmoid(call)
        gw_ref[...] = lax.dot_general(act, dm_ref[...], (((0,), (0,)), ((), ())),
                                      preferred_element_type=F32, precision=lax.Precision.HIGHEST)

    vm = pl.BlockSpec(memory_space=pltpu.VMEM)
    small = jax.ShapeDtypeStruct((16, D_MODEL), F32)
    return pl.pallas_call(
        body, name="small_exchange",
        in_specs=[vm] * 5, out_specs=[vm] * 6,
        out_shape=[jax.ShapeDtypeStruct((D_MODEL, ncol), F32), small, small, small, small,
                   jax.ShapeDtypeStruct((1, 1), F32)],
        scratch_shapes=[pltpu.VMEM((N_PAY, 8, D_MODEL), F32), pltpu.VMEM((8, ncol), F32),
                        pltpu.SemaphoreType.DMA((8,)), pltpu.SemaphoreType.DMA((8,))],
        compiler_params=_cp(),
    )(payload, c_all, wsm, msm, vsm)


def _rope_tables(pos_col, inv_freq, sign):
    def body(p_ref, f_ref, s_ref, cos_ref, sin_ref):
        ang = p_ref[...].astype(F32) * f_ref[...]
        cos_ref[...] = jnp.cos(ang)
        sin_ref[...] = jnp.sin(ang) * s_ref[...]

    tr = 512
    shp = jax.ShapeDtypeStruct((SEQ, 128), F32)
    return pl.pallas_call(
        body, name="rope_tables", grid=(SEQ // tr,),
        in_specs=[pl.BlockSpec((tr, 1), lambda i: (i, 0)), pl.BlockSpec((1, 128), lambda i: (0, 0)),
                  pl.BlockSpec((1, 128), lambda i: (0, 0))],
        out_specs=[pl.BlockSpec((tr, 128), lambda i: (i, 0))] * 2, out_shape=[shp, shp],
        compiler_params=_cp(("parallel",)),
    )(pos_col, inv_freq, sign)


def _ln_proj(x, vecs, w_in4):
    tm = 512
    wc = w_in4.shape[2]

    def body(x_ref, vec_ref, w_ref, h_ref, proj_ref):
        @pl.when(pl.program_id(1) == 0)
        def _():
            xx = x_ref[...]
            g = _row(vec_ref, V_G1) * (1.0 + _row(vec_ref, V_SC1))
            h_ref[...] = (xx * _rms(xx) * g + _row(vec_ref, V_SH1)).astype(BF)
        proj_ref[...] = _dot(h_ref[...], w_ref[...]).astype(BF)

    return pl.pallas_call(
        body, name="ln_proj", grid=(SEQ // tm, N_CHIPS),
        in_specs=[pl.BlockSpec((tm, D_MODEL), lambda i, j: (i, 0)),
                  pl.BlockSpec((16, D_MODEL), lambda i, j: (0, 0)),
                  pl.BlockSpec((None, D_MODEL, wc), lambda i, j: (j, 0, 0))],
        out_specs=[pl.BlockSpec((tm, D_MODEL), lambda i, j: (i, 0)),
                   pl.BlockSpec((tm, wc), lambda i, j: (i, j))],
        out_shape=[jax.ShapeDtypeStruct((SEQ, D_MODEL), BF), jax.ShapeDtypeStruct((SEQ, D_IN), BF)],
        compiler_params=_cp(("parallel", "arbitrary")),
    )(x, vecs, w_in4)


def _lane_first(shape):
    lane = lax.broadcasted_iota(jnp.int32, shape, 1)
    return (lane & 32) == 0


def _rot(v, cos, sin_s):
    partner = jnp.where(_lane_first(v.shape), pltpu.roll(v, 96, 1), pltpu.roll(v, 32, 1))
    return v * cos + partner * sin_s


def _rot_t(dv, cos, sin_s):
    t = dv * sin_s
    partner = jnp.where(_lane_first(dv.shape), pltpu.roll(t, 96, 1), pltpu.roll(t, 32, 1))
    return dv * cos + partner


def _ret_masks(lg):
    t = RET_BLOCK
    ii = lax.broadcasted_iota(jnp.int32, (t, t), 0)
    jj = lax.broadcasted_iota(jnp.int32, (t, t), 1)
    dist = jnp.abs(ii - jj).astype(F32)
    future = (jj >> RET_CHUNK_SHIFT) > (ii >> RET_CHUNK_SHIFT)
    mask = jnp.where(future, 0.0, jnp.exp(lg * dist))
    ti = lax.broadcasted_iota(jnp.int32, (t, 1), 0).astype(F32)
    from_start = jnp.exp(lg * (ti + 1.0))
    to_end = jnp.exp(lg * (t - 1.0 - ti))
    whole = jnp.exp(jnp.full((1, 128), lg * t, F32))
    return mask, from_start, to_end, whole


def _head_lanes(shape, hh):
    lane = lax.broadcasted_iota(jnp.int32, shape, 1)
    return (lane >> 6) == hh


def _ret_specs():
    t = RET_BLOCK
    return dict(
        q=lambda f: pl.BlockSpec((t, 128), lambda p, n: (f(n), C_QR // 128 + p)),
        k=lambda f: pl.BlockSpec((t, 128), lambda p, n: (f(n), C_KR // 128 + p)),
        v=lambda f: pl.BlockSpec((t, 256), lambda p, n: (f(n), C_VR // 256 + p)),
        g=lambda f: pl.BlockSpec((t, 256), lambda p, n: (f(n), C_GR // 256 + p)),
        tab=lambda f: pl.BlockSpec((t, 128), lambda p, n: (f(n), 0)),
        wide=lambda f: pl.BlockSpec((t, 256), lambda p, n: (f(n), p)),
        narrow=lambda f: pl.BlockSpec((t, 128), lambda p, n: (f(n), p)),
        state=lambda f: pl.BlockSpec((None, None, 2, 128, 128), lambda p, n: (p, f(n), 0, 0, 0)),
    )


def _ret_fwd(proj, cos, sin_s, gn_g, log_gamma):
    t = RET_BLOCK
    nb = SEQ // t

    def body(lg_ref, q_ref, k_ref, v_ref, g_ref, cos_ref, sin_ref, gn_ref, o_ref, retg_ref, st_ref, state):
        p = pl.program_id(0)

        @pl.when(pl.program_id(1) == 0)
        def _():
            state[...] = jnp.zeros_like(state)

        cos, sn = cos_ref[...], sin_ref[...]
        q = _rot(q_ref[...].astype(F32), cos, sn)
        k = _rot(k_ref[...].astype(F32), cos, sn) * QK_SCALE
        for hh in range(2):
            lg = lg_ref[2 * p + hh]
            mask, from_start, to_end, whole = _ret_masks(lg)
            lanes = _head_lanes(q.shape, hh)
            qm = jnp.where(lanes, q, 0.0)
            km = jnp.where(lanes, k, 0.0)
            vh = v_ref[:, 128 * hh:128 * (hh + 1)]
            sc = _dot_nt(qm.astype(BF), km.astype(BF)) * mask
            st = state[hh]
            st_ref[hh] = st
            o = _dot(sc.astype(BF), vh) + _dot((qm * from_start).astype(BF), st.astype(BF))
            state[hh] = whole * st + _dot_tn((km * to_end).astype(BF), vh)
            d = o - _rowmean(o)
            nh = d * lax.rsqrt(_rowmean(d * d) + EPS)
            gr = g_ref[:, 128 * hh:128 * (hh + 1)].astype(F32)
            o_ref[:, 128 * hh:128 * (hh + 1)] = o
            retg_ref[:, 128 * hh:128 * (hh + 1)] = (
                gr * _sigmoid(gr) * nh * gn_ref[:, 128 * hh:128 * (hh + 1)]).astype(BF)

    sp = _ret_specs()
    ident = lambda n: n
    return pl.pallas_call(
        body, name="ret_fwd", grid=(N_PAIRS, nb),
        in_specs=[pl.BlockSpec(memory_space=pltpu.SMEM), sp["q"](ident), sp["k"](ident), sp["v"](ident),
                  sp["g"](ident), sp["tab"](ident), sp["tab"](ident),
                  pl.BlockSpec((1, 256), lambda p, n: (0, p))],
        out_specs=[sp["wide"](ident), sp["wide"](ident), sp["state"](ident)],
        out_shape=[jax.ShapeDtypeStruct((SEQ, D_MODEL), F32), jax.ShapeDtypeStruct((SEQ, D_MODEL), BF),
                   jax.ShapeDtypeStruct((N_PAIRS, nb, 2, 128, 128), F32)],
        scratch_shapes=[pltpu.VMEM((2, 128, 128), F32)],
        compiler_params=_cp(("parallel", "arbitrary")),
    )(log_gamma, proj, proj, proj, proj, cos, sin_s, gn_g)


def _ret_bwd(proj, cos, sin_s, dret, states, log_gamma):
    t = RET_BLOCK
    nb = SEQ // t

    def body(lg_ref, q_ref, k_ref, v_ref, cos_ref, sin_ref, do_ref, st_ref, dq_ref, dk_ref, dv_ref, dstate):
        p = pl.program_id(0)

        @pl.when(pl.program_id(1) == 0)
        def _():
            dstate[...] = jnp.zeros_like(dstate)

        cos, sn = cos_ref[...], sin_ref[...]
        q = _rot(q_ref[...].astype(F32), cos, sn)
        k = _rot(k_ref[...].astype(F32), cos, sn) * QK_SCALE
        dq_rot = jnp.zeros(q.shape, F32)
        dk_rot = jnp.zeros(q.shape, F32)
        for hh in range(2):
            lg = lg_ref[2 * p + hh]
            mask, from_start, to_end, whole = _ret_masks(lg)
            lanes = _head_lanes(q.shape, hh)
            qm = jnp.where(lanes, q, 0.0)
            km = jnp.where(lanes, k, 0.0)
            qb, kb = qm.astype(BF), km.astype(BF)
            vh = v_ref[:, 128 * hh:128 * (hh + 1)]
            do = do_ref[:, 128 * hh:128 * (hh + 1)]
            sc = (_dot_nt(qb, kb) * mask).astype(BF)
            st = st_ref[hh].astype(BF)
            dst = dstate[hh]
            dstb = dst.astype(BF)
            k_end = (km * to_end).astype(BF)
            q_start = (qm * from_start).astype(BF)
            dv_ref[:, 128 * hh:128 * (hh + 1)] = (_dot_tn(sc, do) + _dot(k_end, dstb)).astype(BF)
            dsc = (_dot_nt(do, vh) * mask).astype(BF)
            dq_h = _dot(dsc, kb) + _dot_nt(do, st) * from_start
            dq_rot = dq_rot + jnp.where(lanes, dq_h, 0.0)
            dk_rot = dk_rot + _dot_tn(dsc, qb) + _dot_nt(vh, dstb) * to_end
            dstate[hh] = whole * dst + _dot_tn(q_start, do)
        dq_ref[...] = _rot_t(dq_rot, cos, sn).astype(BF)
        dk_ref[...] = _rot_t(dk_rot * QK_SCALE, cos, sn).astype(BF)

    sp = _ret_specs()
    rev = lambda n: nb - 1 - n
    return pl.pallas_call(
        body, name="ret_bwd", grid=(N_PAIRS, nb),
        in_specs=[pl.BlockSpec(memory_space=pltpu.SMEM), sp["q"](rev), sp["k"](rev), sp["v"](rev),
                  sp["tab"](rev), sp["tab"](rev), sp["wide"](rev), sp["state"](rev)],
        out_specs=[sp["narrow"](rev), sp["narrow"](rev), sp["wide"](rev)],
        out_shape=[jax.ShapeDtypeStruct((SEQ, 512), BF), jax.ShapeDtypeStruct((SEQ, 512), BF),
                   jax.ShapeDtypeStruct((SEQ, D_MODEL), BF)],
        scratch_shapes=[pltpu.VMEM((2, 128, 128), F32)],
        compiler_params=_cp(("parallel", "arbitrary")),
    )(log_gamma, proj, proj, proj, cos, sin_s, dret, states)


def _stack_heads(v):
    return jnp.concatenate([jnp.where(_head_lanes(v.shape, hh), v, jnp.zeros_like(v)) for hh in range(2)], axis=0)


def _unstack_heads(v):
    t = v.shape[0] // 2
    return jnp.where(_head_lanes((t, v.shape[1]), 0), v[:t], v[t:])


def _sb_masks(t, heads):
    rr = lax.broadcasted_iota(jnp.int32, (t, t), 0)
    cc = lax.broadcasted_iota(jnp.int32, (t, t), 1)
    r2 = lax.broadcasted_iota(jnp.int32, (heads * t, t), 0) & (t - 1)
    c2 = lax.broadcasted_iota(jnp.int32, (heads * t, t), 1)
    return rr, cc, c2 < r2


def _split_dot2(v, tri):
    hi = v.astype(BF)
    lo = (v - hi.astype(F32)).astype(BF)
    both = _dot(jnp.concatenate([hi, lo], axis=0), tri)
    return both[:v.shape[0]] + both[v.shape[0]:]


def _log_one_minus_beta(z):
    return -(jnp.maximum(z, 0.0) + jnp.log(1.0 + jnp.exp(-jnp.abs(z))))


def _sb_fwd(proj):
    t, g = SB_BLOCK, SB_GROUP
    nq = SEQ // t
    rows = 2 * g * t

    def body(q_ref, k_ref, v_ref, o_ref, tot_ref, kt_ref):
        i = pl.program_id(1)

        @pl.when(i == 0)
        def _():
            for p in range(g):
                for jj in range(nq):
                    kt_ref[p, jj] = k_ref[jj * t:(jj + 1) * t, 128 * p:128 * (p + 1)].T

        q2 = [_stack_heads((q_ref[:, 128 * p:128 * (p + 1)].astype(F32) * QK_SCALE).astype(BF)) for p in range(g)]
        rr, cc, valid = _sb_masks(t, 2 * g)
        later = (rr > cc).astype(BF)

        def tile(j, carry, diagonal):
            acc, run = carry
            z = jnp.concatenate([_dot(q2[p], kt_ref[p, j]) for p in range(g)], axis=0)
            lm = _log_one_minus_beta(z)
            if diagonal:
                lm = jnp.where(valid, lm, 0.0)
            after = _split_dot2(lm, later)
            a = jnp.exp(z + lm + after + run)
            if diagonal:
                a = jnp.where(valid, a, 0.0)
            ab = a.astype(BF)
            keys = pl.ds(pl.multiple_of(j * t, t), t)
            av = jnp.concatenate([_dot(ab[2 * t * p:2 * t * (p + 1)], v_ref[keys, 128 * p:128 * (p + 1)])
                                  for p in range(g)], axis=0)
            return acc + av, run + after[:, 0:1] + lm[:, 0:1]

        carry = tile(i, (jnp.zeros((rows, 128), F32), jnp.zeros((rows, 1), F32)), True)
        acc, run = lax.fori_loop(0, i, lambda s, cr: tile(i - 1 - s, cr, False), carry)
        run = jnp.broadcast_to(run, (rows, 128))
        for p in range(g):
            o_ref[:, 128 * p:128 * (p + 1)] = _unstack_heads(acc[2 * t * p:2 * t * (p + 1)]).astype(BF)
            tot_ref[:, 128 * p:128 * (p + 1)] = _unstack_heads(run[2 * t * p:2 * t * (p + 1)])

    w = 128 * g
    return pl.pallas_call(
        body, name="sb_fwd", grid=(N_PAIRS // g, nq),
        in_specs=[pl.BlockSpec((t, w), lambda p, i: (i, C_QS // w + p)),
                  pl.BlockSpec((SEQ, w), lambda p, i: (0, C_KS // w + p)),
                  pl.BlockSpec((SEQ, w), lambda p, i: (0, C_VS // w + p))],
        out_specs=[pl.BlockSpec((t, w), lambda p, i: (i, p))] * 2,
        out_shape=[jax.ShapeDtypeStruct((SEQ, 512), BF), jax.ShapeDtypeStruct((SEQ, 512), F32)],
        scratch_shapes=[pltpu.VMEM((g, nq, 128, t), BF)],
        compiler_params=_cp(("parallel", "arbitrary")),
    )(proj, proj, proj)


def _sb_bwd(proj, dsb, tot, dep=None):
    t, g = SB_BLOCK, SB_GROUP
    nq = SEQ // t
    rows = 2 * g * t

    def body(q_ref, k_ref, v_ref, do_ref, tot_ref, dq_ref, dk_ref, dv_ref, kt_ref, vt_ref, dkt_acc, dvt_acc):
        i = pl.program_id(1)

        @pl.when(i == 0)
        def _():
            dkt_acc[...] = jnp.zeros_like(dkt_acc)
            dvt_acc[...] = jnp.zeros_like(dvt_acc)
            for p in range(g):
                for jj in range(nq):
                    kt_ref[p, jj] = k_ref[jj * t:(jj + 1) * t, 128 * p:128 * (p + 1)].T
                    vt_ref[p, jj] = v_ref[jj * t:(jj + 1) * t, 128 * p:128 * (p + 1)].T

        q2 = [_stack_heads((q_ref[:, 128 * p:128 * (p + 1)].astype(F32) * QK_SCALE).astype(BF)) for p in range(g)]
        do2 = [_stack_heads(do_ref[:, 128 * p:128 * (p + 1)]) for p in range(g)]
        q2t = [v.T for v in q2]
        do2t = [v.T for v in do2]
        tots = tot_ref[...]
        total = jnp.concatenate([tots[:, 64 * h:64 * h + 1] for h in range(2 * g)], axis=0)
        rr, cc, valid = _sb_masks(t, 2 * g)
        upto = (rr <= cc).astype(BF)
        before = (rr < cc).astype(BF)

        def part(v, p):
            return v[2 * t * p:2 * t * (p + 1)]

        def tile(j, carry, diagonal):
            dq, run_l, run_g = carry
            z = jnp.concatenate([_dot(q2[p], kt_ref[p, j]) for p in range(g)], axis=0)
            lm = _log_one_minus_beta(z)
            if diagonal:
                lm = jnp.where(valid, lm, 0.0)
            incl = _split_dot2(lm, upto)
            a = jnp.exp(z + lm + (total - (incl + run_l)))
            if diagonal:
                a = jnp.where(valid, a, 0.0)
            gg = a * jnp.concatenate([_dot(do2[p], vt_ref[p, j]) for p in range(g)], axis=0)
            excl = _split_dot2(gg, before)
            dz = gg * jnp.exp(lm) - (excl + run_g) * jnp.exp(z + lm)
            if diagonal:
                dz = jnp.where(valid, dz, 0.0)
            dzb = dz.astype(BF)
            ab = a.astype(BF)
            keys = pl.ds(pl.multiple_of(j * t, t), t)
            for p in range(g):
                dkt_acc[p, j] += _dot(q2t[p], part(dzb, p))
                dvt_acc[p, j] += _dot(do2t[p], part(ab, p))
            dq_t = jnp.concatenate([_dot(part(dzb, p), k_ref[keys, 128 * p:128 * (p + 1)]) for p in range(g)], axis=0)
            return (dq + dq_t, run_l + incl[:, t - 1:t], run_g + excl[:, t - 1:t] + gg[:, t - 1:t])

        zero = jnp.zeros((rows, 1), F32)
        carry = lax.fori_loop(0, i, lambda j, cr: tile(j, cr, False), (jnp.zeros((rows, 128), F32), zero, zero))
        dq = tile(i, carry, True)[0]
        for p in range(g):
            dq_ref[:, 128 * p:128 * (p + 1)] = (_unstack_heads(part(dq, p)) * QK_SCALE).astype(BF)

        @pl.when(i == nq - 1)
        def _():
            for p in range(g):
                for jj in range(nq):
                    dk_ref[jj * t:(jj + 1) * t, 128 * p:128 * (p + 1)] = dkt_acc[p, jj].T.astype(BF)
                    dv_ref[jj * t:(jj + 1) * t, 128 * p:128 * (p + 1)] = dvt_acc[p, jj].T.astype(BF)

    w = 128 * g
    tile_spec = pl.BlockSpec((t, w), lambda p, i: (i, p))
    col_spec = pl.BlockSpec((SEQ, w), lambda p, i: (0, p))
    shp = jax.ShapeDtypeStruct((SEQ, 512), BF)
    body, in_specs, args = _add_dep(
        body, [pl.BlockSpec((t, w), lambda p, i: (i, C_QS // w + p)),
               pl.BlockSpec((SEQ, w), lambda p, i: (0, C_KS // w + p)),
               pl.BlockSpec((SEQ, w), lambda p, i: (0, C_VS // w + p)),
               tile_spec, tile_spec],
        [proj, proj, proj, dsb, tot], dep)
    return pl.pallas_call(
        body, name="sb_bwd", grid=(N_PAIRS // g, nq),
        in_specs=in_specs,
        out_specs=[tile_spec, col_spec, col_spec],
        out_shape=[shp, shp, shp],
        scratch_shapes=[pltpu.VMEM((g, nq, 128, t), BF), pltpu.VMEM((g, nq, 128, t), BF),
                        pltpu.VMEM((g, nq, 128, t), F32), pltpu.VMEM((g, nq, 128, t), F32)],
        compiler_params=_cp(("parallel", "arbitrary")),
    )(*args)


def _mix(retg, sb, proj, w_ret, w_sb4):
    tm, tn = 512, 256

    def body(r_ref, s_ref, ar_ref, as_ref, wr_ref, ws_ref, mix_ref, rb_ref, sbp_ref):
        rb = _dot(r_ref[...], wr_ref[...])
        sbp = _dot(s_ref[...], ws_ref[...])
        mix = _sigmoid(ar_ref[...].astype(F32)) * rb + _sigmoid(as_ref[...].astype(F32)) * sbp
        mix_ref[...] = mix.astype(BF)
        rb_ref[...] = rb.astype(BF)
        sbp_ref[...] = sbp.astype(BF)

    out = pl.BlockSpec((tm, tn), lambda j, i: (i, j))
    shp = jax.ShapeDtypeStruct((SEQ, D_MODEL), BF)
    return pl.pallas_call(
        body, name="mix", grid=(D_MODEL // tn, SEQ // tm),
        in_specs=[pl.BlockSpec((tm, D_MODEL), lambda j, i: (i, 0)),
                  pl.BlockSpec((tm, 512), lambda j, i: (i, 0)),
                  pl.BlockSpec((tm, tn), lambda j, i: (i, C_AR // tn + j)),
                  pl.BlockSpec((tm, tn), lambda j, i: (i, C_AS // tn + j)),
                  pl.BlockSpec((D_MODEL, tn), lambda j, i: (0, j)),
                  pl.BlockSpec((None, 512, tn), lambda j, i: (j, 0, 0))],
        out_specs=[out, out, out], out_shape=[shp, shp, shp],
        compiler_params=_cp(("parallel", "parallel")),
    )(retg, sb, proj, proj, w_ret, w_sb4)


def _out_proj(mixed, x, vecs, w_out):
    tm = 256

    def body(m_ref, x_ref, vec_ref, w_ref, y_ref, h1_ref, h2_ref):
        y = _dot(m_ref[...], w_ref[...])
        h1 = x_ref[...] + _row(vec_ref, V_GT1) * (y * _rms(y)) * _row(vec_ref, V_G2)
        g = _row(vec_ref, V_G3) * (1.0 + _row(vec_ref, V_SC2))
        y_ref[...] = y
        h1_ref[...] = h1
        h2_ref[...] = (h1 * _rms(h1) * g + _row(vec_ref, V_SH2)).astype(BF)

    row = pl.BlockSpec((tm, D_MODEL), lambda i: (i, 0))
    f32 = jax.ShapeDtypeStruct((SEQ, D_MODEL), F32)
    return pl.pallas_call(
        body, name="out_proj", grid=(SEQ // tm,),
        in_specs=[row, row, pl.BlockSpec((16, D_MODEL), lambda i: (0, 0)),
                  pl.BlockSpec((D_MODEL, D_MODEL), lambda i: (0, 0))],
        out_specs=[row, row, row],
        out_shape=[f32, f32, jax.ShapeDtypeStruct((SEQ, D_MODEL), BF)],
        compiler_params=_cp(("parallel",)),
    )(mixed, x, vecs, w_out)


def _ffn_up(h2, w_ff14):
    tm = 512

    def body(h_ref, w_ref, u_ref, a_ref):
        u = _dot(h_ref[...], w_ref[...])
        r = jnp.maximum(u, 0.0)
        u_ref[...] = u.astype(BF)
        a_ref[...] = (r * r).astype(BF)

    out = pl.BlockSpec((tm, D_MODEL), lambda j, i: (i, j))
    shp = jax.ShapeDtypeStruct((SEQ, D_FF), BF)
    return pl.pallas_call(
        body, name="ffn_up", grid=(N_CHIPS, SEQ // tm),
        in_specs=[pl.BlockSpec((tm, D_MODEL), lambda j, i: (i, 0)),
                  pl.BlockSpec((None, D_MODEL, D_MODEL), lambda j, i: (j, 0, 0))],
        out_specs=[out, out], out_shape=[shp, shp],
        compiler_params=_cp(("parallel", "parallel")),
    )(h2, w_ff14)


def _ffn_down_loss(act, h1, target, vecs, w_ff2):
    tm = 256

    def body(a_ref, h1_ref, t_ref, vec_ref, w_ref, dout_ref, df_ref, st_ref):
        @pl.when(pl.program_id(0) == 0)
        def _():
            st_ref[...] = jnp.zeros_like(st_ref)

        f = _dot(a_ref[...], w_ref[...])
        r4 = _rms(f)
        fn = f * r4
        gt2, g4 = _row(vec_ref, V_GT2), _row(vec_ref, V_G4)
        diff = h1_ref[...] + gt2 * fn * g4 - t_ref[...]
        dout = diff * (1.0 / D_MODEL)
        dfn = dout * gt2 * g4
        dout_ref[...] = dout
        df_ref[...] = (r4 * (dfn - fn * _rowmean(dfn * fn))).astype(BF)
        st_ref[0:1, :] += _colsum(dout * fn * g4)
        st_ref[1:2, :] += _colsum(dout * gt2 * fn)
        st_ref[2:3, :] += _colsum(diff * diff) * (0.5 / D_MODEL)

    row = pl.BlockSpec((tm, D_MODEL), lambda i: (i, 0))
    return pl.pallas_call(
        body, name="ffn_down_loss", grid=(SEQ // tm,),
        in_specs=[pl.BlockSpec((tm, D_FF), lambda i: (i, 0)), row, row,
                  pl.BlockSpec((16, D_MODEL), lambda i: (0, 0)),
                  pl.BlockSpec((D_FF, D_MODEL), lambda i: (0, 0))],
        out_specs=[row, row, pl.BlockSpec((8, D_MODEL), lambda i: (0, 0))],
        out_shape=[jax.ShapeDtypeStruct((SEQ, D_MODEL), F32), jax.ShapeDtypeStruct((SEQ, D_MODEL), BF),
                   jax.ShapeDtypeStruct((8, D_MODEL), F32)],
        compiler_params=_cp(("arbitrary",)),
    )(act, h1, target, vecs, w_ff2)


def _ffn_down_bwd(df, u, w_ff2):
    tm = 256

    def body(df_ref, u_ref, w_ref, du_ref):
        da = _dot_nt(df_ref[...], w_ref[...])
        du_ref[...] = (da * (2.0 * jnp.maximum(u_ref[...].astype(F32), 0.0))).astype(BF)

    return pl.pallas_call(
        body, name="ffn_down_bwd", grid=(D_FF // 1024, SEQ // tm),
        in_specs=[pl.BlockSpec((tm, D_MODEL), lambda j, i: (i, 0)),
                  pl.BlockSpec((tm, 1024), lambda j, i: (i, j)),
                  pl.BlockSpec((1024, D_MODEL), lambda j, i: (j, 0))],
        out_specs=pl.BlockSpec((tm, 1024), lambda j, i: (i, j)),
        out_shape=jax.ShapeDtypeStruct((SEQ, D_FF), BF),
        compiler_params=_cp(("parallel", "parallel")),
    )(df, u, w_ff2)


def _ffn_up_bwd(du, h1, y, dout, vecs, w_ff14):
    tm = 256

    def body(du_ref, h1_ref, y_ref, dout_ref, vec_ref, w_ref, dh1_ref, dy_ref, st_ref, acc):
        kk = pl.program_id(1)

        @pl.when((pl.program_id(0) == 0) & (kk == 0))
        def _():
            st_ref[...] = jnp.zeros_like(st_ref)

        part = _dot_nt(du_ref[...], w_ref[...])

        @pl.when(kk == 0)
        def _():
            acc[...] = part

        @pl.when(kk > 0)
        def _():
            acc[...] += part

        @pl.when(kk == N_CHIPS - 1)
        def _():
            dh2 = acc[...]
            h1 = h1_ref[...]
            r3 = _rms(h1)
            hn3 = h1 * r3
            g3, sc2 = _row(vec_ref, V_G3), _row(vec_ref, V_SC2)
            dhn3 = dh2 * g3 * (1.0 + sc2)
            dh1 = dout_ref[...] + r3 * (dhn3 - hn3 * _rowmean(dhn3 * hn3))
            y = y_ref[...]
            r2 = _rms(y)
            yn = y * r2
            gt1, g2 = _row(vec_ref, V_GT1), _row(vec_ref, V_G2)
            dyn = dh1 * gt1 * g2
            dh1_ref[...] = dh1
            dy_ref[...] = (r2 * (dyn - yn * _rowmean(dyn * yn))).astype(BF)
            st_ref[0:1, :] += _colsum(dh2)
            st_ref[1:2, :] += _colsum(dh2 * hn3 * g3)
            st_ref[2:3, :] += _colsum(dh2 * hn3 * (1.0 + sc2))
            st_ref[3:4, :] += _colsum(dh1 * yn * g2)
            st_ref[4:5, :] += _colsum(dh1 * gt1 * yn)

    row = pl.BlockSpec((tm, D_MODEL), lambda i, k: (i, 0))
    return pl.pallas_call(
        body, name="ffn_up_bwd", grid=(SEQ // tm, N_CHIPS),
        in_specs=[pl.BlockSpec((tm, D_MODEL), lambda i, k: (i, k)), row, row, row,
                  pl.BlockSpec((16, D_MODEL), lambda i, k: (0, 0)),
                  pl.BlockSpec((None, D_MODEL, D_MODEL), lambda i, k: (k, 0, 0))],
        out_specs=[row, row, pl.BlockSpec((8, D_MODEL), lambda i, k: (0, 0))],
        out_shape=[jax.ShapeDtypeStruct((SEQ, D_MODEL), F32), jax.ShapeDtypeStruct((SEQ, D_MODEL), BF),
                   jax.ShapeDtypeStruct((8, D_MODEL), F32)],
        scratch_shapes=[pltpu.VMEM((tm, D_MODEL), F32)],
        compiler_params=_cp(("arbitrary", "arbitrary")),
    )(du, h1, y, dout, vecs, w_ff14)


def _mix_bwd(dy, proj, rb, sbp, w_out, w_sb4):
    tm, tn = 512, 256

    def body(dy_ref, ar_ref, as_ref, rb_ref, sbp_ref, wo_ref, ws_ref,
             drb_ref, dsbp_ref, dar_ref, das_ref, dsb_ref, acc):
        j = pl.program_id(1)
        dm = _dot_nt(dy_ref[...], wo_ref[...])
        sr = _sigmoid(ar_ref[...].astype(F32))
        ss = _sigmoid(as_ref[...].astype(F32))
        dsbp = (dm * ss).astype(BF)
        drb_ref[...] = (dm * sr).astype(BF)
        dsbp_ref[...] = dsbp
        dar_ref[...] = (dm * rb_ref[...].astype(F32) * sr * (1.0 - sr)).astype(BF)
        das_ref[...] = (dm * sbp_ref[...].astype(F32) * ss * (1.0 - ss)).astype(BF)
        part = _dot_nt(dsbp, ws_ref[...])

        @pl.when(j == 0)
        def _():
            acc[...] = part

        @pl.when(j > 0)
        def _():
            acc[...] += part

        @pl.when(j == D_MODEL // tn - 1)
        def _():
            dsb_ref[...] = acc[...].astype(BF)

    tile = pl.BlockSpec((tm, tn), lambda i, j: (i, j))
    shp = jax.ShapeDtypeStruct((SEQ, D_MODEL), BF)
    return pl.pallas_call(
        body, name="mix_bwd", grid=(SEQ // tm, D_MODEL // tn),
        in_specs=[pl.BlockSpec((tm, D_MODEL), lambda i, j: (i, 0)),
                  pl.BlockSpec((tm, tn), lambda i, j: (i, C_AR // tn + j)),
                  pl.BlockSpec((tm, tn), lambda i, j: (i, C_AS // tn + j)),
                  tile, tile,
                  pl.BlockSpec((tn, D_MODEL), lambda i, j: (j, 0)),
                  pl.BlockSpec((None, 512, tn), lambda i, j: (j, 0, 0))],
        out_specs=[tile, tile, tile, tile, pl.BlockSpec((tm, 512), lambda i, j: (i, 0))],
        out_shape=[shp, shp, shp, shp, jax.ShapeDtypeStruct((SEQ, 512), BF)],
        scratch_shapes=[pltpu.VMEM((tm, 512), F32)],
        compiler_params=_cp(("parallel", "arbitrary")),
    )(dy, proj, proj, rb, sbp, w_out, w_sb4)


def _ret_branch_bwd(drb, proj, o_raw, gn_g, w_ret):
    tm, tn = 512, 256

    def body(d_ref, g_ref, o_ref, gn_ref, w_ref, dret_ref, dgr_ref, st_ref):
        @pl.when(pl.program_id(1) == 0)
        def _():
            st_ref[...] = jnp.zeros_like(st_ref)

        dretg = _dot_nt(d_ref[...], w_ref[...])
        for gi in range(tn // 128):
            cols = slice(128 * gi, 128 * (gi + 1))
            o = o_ref[:, cols]
            d = o - _rowmean(o)
            rstd = lax.rsqrt(_rowmean(d * d) + EPS)
            nh = d * rstd
            gain = gn_ref[:, cols]
            gr = g_ref[:, cols].astype(F32)
            sg = _sigmoid(gr)
            dg = dretg[:, cols]
            dgn = dg * gr * sg
            dnh = dgn * gain
            dgr_ref[:, cols] = (dg * nh * gain * sg * (1.0 + gr * (1.0 - sg))).astype(BF)
            dret_ref[:, cols] = (rstd * (dnh - _rowmean(dnh) - nh * _rowmean(dnh * nh))).astype(BF)
            st_ref[0:1, cols] += _colsum(dgn * nh)

    tile = pl.BlockSpec((tm, tn), lambda j, i: (i, j))
    shp = jax.ShapeDtypeStruct((SEQ, D_MODEL), BF)
    return pl.pallas_call(
        body, name="ret_branch_bwd", grid=(D_MODEL // tn, SEQ // tm),
        in_specs=[pl.BlockSpec((tm, D_MODEL), lambda j, i: (i, 0)),
                  pl.BlockSpec((tm, tn), lambda j, i: (i, C_GR // tn + j)),
                  tile, pl.BlockSpec((1, tn), lambda j, i: (0, j)),
                  pl.BlockSpec((tn, D_MODEL), lambda j, i: (j, 0))],
        out_specs=[tile, tile, pl.BlockSpec((8, tn), lambda j, i: (0, j))],
        out_shape=[shp, shp, jax.ShapeDtypeStruct((8, D_MODEL), F32)],
        compiler_params=_cp(("parallel", "arbitrary")),
    )(drb, proj, o_raw, gn_g, w_ret)


def _in_proj_bwd(dproj, x, dh1, vecs, w_in4):
    tm = 512
    wc = w_in4.shape[2]

    def body(dp_ref, x_ref, dh1_ref, vec_ref, w_ref, dx_ref, st_ref, acc):
        kk = pl.program_id(1)

        @pl.when((pl.program_id(0) == 0) & (kk == 0))
        def _():
            st_ref[...] = jnp.zeros_like(st_ref)

        part = _dot_nt(dp_ref[...], w_ref[...])

        @pl.when(kk == 0)
        def _():
            acc[...] = part

        @pl.when(kk > 0)
        def _():
            acc[...] += part

        @pl.when(kk == N_CHIPS - 1)
        def _():
            dh = acc[...]
            xx = x_ref[...]
            r1 = _rms(xx)
            xn = xx * r1
            g1, sc1 = _row(vec_ref, V_G1), _row(vec_ref, V_SC1)
            dxn = dh * g1 * (1.0 + sc1)
            dx_ref[...] = dh1_ref[...] + r1 * (dxn - xn * _rowmean(dxn * xn))
            st_ref[0:1, :] += _colsum(dh)
            st_ref[1:2, :] += _colsum(dh * xn * g1)
            st_ref[2:3, :] += _colsum(dh * xn * (1.0 + sc1))

    row = pl.BlockSpec((tm, D_MODEL), lambda i, k: (i, 0))
    return pl.pallas_call(
        body, name="in_proj_bwd", grid=(SEQ // tm, N_CHIPS),
        in_specs=[pl.BlockSpec((tm, wc), lambda i, k: (i, k)), row, row,
                  pl.BlockSpec((16, D_MODEL), lambda i, k: (0, 0)),
                  pl.BlockSpec((None, D_MODEL, wc), lambda i, k: (k, 0, 0))],
        out_specs=[row, pl.BlockSpec((8, D_MODEL), lambda i, k: (0, 0))],
        out_shape=[jax.ShapeDtypeStruct((SEQ, D_MODEL), F32), jax.ShapeDtypeStruct((8, D_MODEL), F32)],
        scratch_shapes=[pltpu.VMEM((tm, D_MODEL), F32)],
        compiler_params=_cp(("arbitrary", "arbitrary")),
    )(dproj, x, dh1, vecs, w_in4)


def _weight_grad(a, b, ta, tb, col_sharded, name, dep=None):
    ka, nb_ = a.shape[1], b.shape[1]

    def body(a_ref, b_ref, o_ref):
        o_ref[...] = _dot_tn(a_ref[...], b_ref[...])

    body, in_specs, args = _add_dep(
        body, [pl.BlockSpec((SEQ, ta), lambda i, j: (0, i)), pl.BlockSpec((SEQ, tb), lambda i, j: (0, j))],
        [a, b], dep)

    if col_sharded:
        per = nb_ // N_CHIPS // tb
        out_shape = jax.ShapeDtypeStruct((N_CHIPS, ka, nb_ // N_CHIPS), F32)
        out_spec = pl.BlockSpec((None, ta, tb), lambda i, j: (j // per, i, j % per))
    else:
        per = ka // N_CHIPS // ta
        out_shape = jax.ShapeDtypeStruct((N_CHIPS, ka // N_CHIPS, nb_), F32)
        out_spec = pl.BlockSpec((None, ta, tb), lambda i, j: (i // per, i % per, j))
    return pl.pallas_call(
        body, name=name, grid=(ka // ta, nb_ // tb),
        in_specs=in_specs, out_specs=out_spec, out_shape=out_shape,
        compiler_params=_cp(("parallel", "parallel")),
    )(*args)


def _rope_constants():
    freq = np.float32(ROPE_BASE) ** (-np.arange(0, 64, 2, dtype=np.float32) / np.float32(64))
    inv = np.tile(freq.astype(np.float32), 4).reshape(1, 128)
    sign = np.tile(np.concatenate([-np.ones(32, np.float32), np.ones(32, np.float32)]), 2).reshape(1, 128)
    return jnp.asarray(inv), jnp.asarray(sign)


def _log_gamma():
    return jnp.asarray(np.log1p(-(2.0 ** (-5.0 - np.arange(8, dtype=np.float64)))).astype(np.float32))


def _halves(g):
    return g.reshape(N_CHIPS, 2, g.shape[1] // 2, g.shape[2])


def kernel(x, c, positions, ada_w, ada_b, pre_mix_g, post_mix_g, pre_ffn_g, post_ffn_g, w_in, ret_gn_g, w_ret_branch, w_sb_branch, w_out, w_ff1, w_ff2, loss_target, m_ada_w, m_ada_b, m_pre_mix_g, m_post_mix_g, m_pre_ffn_g, m_post_ffn_g, m_w_in, m_ret_gn_g, m_w_ret_branch, m_w_sb_branch, m_w_out, m_w_ff1, m_w_ff2, v_ada_w, v_ada_b, v_pre_mix_g, v_post_mix_g, v_pre_ffn_g, v_post_ffn_g, v_w_in, v_ret_gn_g, v_w_ret_branch, v_w_sb_branch, v_w_out, v_w_ff1, v_w_ff2):
    names = ["w_in", "w_ret", "w_sb", "w_out", "w_ff1", "w_ff2"]
    big = dict(zip(names, [w_in, w_ret_branch, w_sb_branch, w_out, w_ff1, w_ff2]))
    big_m = dict(zip(names, [m_w_in, m_w_ret_branch, m_w_sb_branch, m_w_out, m_w_ff1, m_w_ff2]))
    big_v = dict(zip(names, [v_w_in, v_w_ret_branch, v_w_sb_branch, v_w_out, v_w_ff1, v_w_ff2]))
    rest = names[1:]
    cidx = lax.axis_index("c").astype(jnp.int32).reshape(1)
    kidx = (2 * lax.axis_index("x") + lax.axis_index("y")).astype(jnp.int32).reshape(1)
    x0, target = x[0], loss_target[0]

    c_all, mod4 = _mod_exchange(c, ada_w[0], ada_b.reshape(N_CHIPS, -1))
    vecs = jnp.concatenate([mod4.reshape(6, D_MODEL), pre_mix_g, post_mix_g, pre_ffn_g, post_ffn_g,
                            jnp.zeros((6, D_MODEL), F32)], axis=0)

    buf_in, sem_in, tok_in = _gather_start("gather_in_start", [_cast_bf16(w_in[0], kidx, mod4, "cast_w_in")])
    buf_rest, sem_rest, tok_rest = _gather_start(
        "gather_rest_start", [_cast_bf16(big[nm][0], kidx, tok_in, "cast_" + nm) for nm in rest])
    inv_freq, sign = _rope_constants()
    lg = _log_gamma()
    cos, sin_s = _rope_tables(positions.reshape(SEQ, 1), _tie(tok_rest, inv_freq), sign)
    buf_in, sem_in, tok_in = _gather_pass("gather_in_pass", buf_in, sem_in, cos)
    (w_in4,) = _gather_finish("gather_in_finish", buf_in, sem_in, tok_in)

    h, proj = _ln_proj(x0, vecs, w_in4)
    sb, tot = _sb_fwd(proj)
    buf_rest, sem_rest, tok_rest = _gather_pass("gather_rest_pass", buf_rest, sem_rest, sb)
    o_raw, retg, states = _ret_fwd(proj, cos, sin_s, _tie(tok_rest, ret_gn_g), lg)
    w_ret4, w_sb4, w_out4, w_ff14, w_ff24 = _gather_finish("gather_rest_finish", buf_rest, sem_rest, retg)
    w_ret = w_ret4.reshape(D_MODEL, D_MODEL)
    w_out2 = w_out4.reshape(D_MODEL, D_MODEL)
    w_ff2_2 = w_ff24.reshape(D_FF, D_MODEL)
    mixed, rb, sbp = _mix(retg, sb, proj, w_ret, w_sb4)
    y, h1, h2 = _out_proj(mixed, x0, vecs, w_out2)
    u, act = _ffn_up(h2, w_ff14)
    dout, df, st_a = _ffn_down_loss(act, h1, target, vecs, w_ff2_2)

    du = _ffn_down_bwd(df, u, w_ff2_2)
    grads = {"w_ff2": _weight_grad(act, df, 512, 1024, False, "grad_w_ff2")}
    dh1, dy, st_b = _ffn_up_bwd(du, h1, y, dout, vecs, w_ff14)
    grads["w_ff1"] = _weight_grad(h2, du, 512, 1024, True, "grad_w_ff1")
    drb, dsbp, da_r, da_s, dsb = _mix_bwd(dy, proj, rb, sbp, w_out2, w_sb4)
    grads["w_out"] = _weight_grad(mixed, dy, 256, 1024, False, "grad_w_out")
    dret, dg_r, st_c = _ret_branch_bwd(drb, proj, o_raw, ret_gn_g, w_ret)
    grads["w_ret"] = _weight_grad(retg, drb, 256, 1024, False, "grad_w_ret")
    grads["w_sb"] = _weight_grad(sb, dsbp, 512, 256, True, "grad_w_sb")

    bufs, sems, tok = _pair_send_start("rs_rest_pair_send", [_halves(grads[nm]) for nm in rest])
    dq_r, dk_r, dv_r = _ret_bwd(proj, cos, sin_s, dret, states, _tie(tok, lg))
    mine, theirs = _pair_send_wait("rs_rest_pair_recv", bufs, sems, dq_r)
    pair_sums = [_pair_add(g, r, cidx, "pair_add_" + nm) for g, r, nm in zip(mine, theirs, rest)]
    bufs, sems, tok = _chip_send_start("rs_rest_chip_send", pair_sums)
    dq_s, dk_s, dv_s = _sb_bwd(proj, dsb, tot, dep=tok)
    own, parts = _chip_send_wait("rs_rest_chip_recv", bufs, sems, dq_s)
    sums = [_chip_add(o, p, kidx, cidx, "chip_add_" + nm) for o, p, nm in zip(own, parts, rest)]
    bufs, sems, tok = _pair_swap_start("rs_rest_pair_swap", sums)
    dproj = jnp.concatenate([dq_r, dk_r, dv_r, dg_r, dq_s, dk_s, dv_s, da_r, da_s], axis=1)
    g_in = _weight_grad(h, dproj, 512, 1664, True, "grad_w_in", dep=tok)
    full_rest = _pair_swap_wait("rs_rest_pair_swapped", bufs, sems, g_in)

    bufs, sems, tok = _pair_send_start("rs_in_pair_send", [_halves(g_in)])
    dx, st_d = _in_proj_bwd(dproj, x0, dh1, _tie(tok, vecs), w_in4)
    mine, theirs = _pair_send_wait("rs_in_pair_recv", bufs, sems, dx)
    bufs, sems, tok = _chip_send_start("rs_in_chip_send", [_pair_add(mine[0], theirs[0], cidx, "pair_add_w_in")])
    out = {}
    for nm, g in zip(rest, full_rest):
        w = big[nm][0]
        out[nm] = _adamw(w, big_m[nm][0], big_v[nm][0], g.reshape(w.shape), "adamw_" + nm, dep=tok)

    payload = jnp.concatenate([
        st_d[0:2], st_b[3:4], st_b[0:2], st_a[0:1],
        st_d[2:3], st_b[4:5], st_b[2:3], st_a[1:2],
        st_c[0:1], st_a[2:3]], axis=0)

    def table(b6, g5):
        return jnp.concatenate([b6.reshape(6, D_MODEL)] + g5 + [jnp.zeros((5, D_MODEL), F32)], axis=0)

    wsm = table(ada_b, [pre_mix_g, post_mix_g, pre_ffn_g, post_ffn_g, ret_gn_g])
    msm = table(m_ada_b, [m_pre_mix_g, m_post_mix_g, m_pre_ffn_g, m_post_ffn_g, m_ret_gn_g])
    vsm = table(v_ada_b, [v_pre_mix_g, v_post_mix_g, v_pre_ffn_g, v_post_ffn_g, v_ret_gn_g])
    g_ada, gsm, dsm, mosm, vosm, loss = _small_exchange(
        payload.reshape(N_PAY, 1, D_MODEL), c_all, wsm, msm, vsm)
    ada_out = _adamw(ada_w[0], m_ada_w[0], v_ada_w[0], g_ada, "adamw_ada_w")

    own, parts = _chip_send_wait("rs_in_chip_recv", bufs, sems, ada_out[1])
    bufs, sems, tok = _pair_swap_start(
        "rs_in_pair_swap", [_chip_add(own[0], parts[0], kidx, cidx, "chip_add_w_in")])
    (full_in,) = _pair_swap_wait("rs_in_pair_swapped", bufs, sems, tok)
    out["w_in"] = _adamw(w_in[0], m_w_in[0], v_w_in[0], full_in.reshape(w_in.shape[1:]), "adamw_w_in")

    def unpack(tab):
        return [tab[0:6].reshape(1, 6 * D_MODEL)] + [tab[6 + r:7 + r] for r in range(5)]

    def ordered(which):
        sm = unpack([gsm, dsm, mosm, vosm][which])
        bg = [out[nm][which][None] for nm in names]
        return [ada_out[which][None], sm[0], sm[1], sm[2], sm[3], sm[4], bg[0], sm[5]] + bg[1:]

    return (loss.reshape(()), dx[None], *ordered(0), *ordered(1), *ordered(2), *ordered(3))
```

```python
import functools

import numpy as np
import jax
import jax.numpy as jnp
from jax import lax
from jax.experimental import pallas as pl
from jax.experimental.pallas import tpu as pltpu

SEQ = 2048
D_MODEL = 1024
D_IN = 6656
D_FF = 4096
N_CHIPS = 4
EPS = 1e-6
ROPE_BASE = 10000.0
RET_BLOCK = 256
RET_CHUNK_SHIFT = 6
SB_BLOCK = 256
QK_SCALE = 0.125
N_PAIRS = 4
SB_GROUP = 2

ADAM_LR = 0.001
ADAM_B1 = 0.9
ADAM_B2 = 0.999
ADAM_EPS = 1e-08
ADAM_WD = 0.01
ADAM_STEP = 10

BF = jnp.bfloat16
F32 = jnp.float32
MESH = pl.DeviceIdType.MESH
VMEM_LIMIT = 56 * 1024 * 1024
ANY = pl.BlockSpec(memory_space=pl.ANY)

C_QR, C_KR, C_VR, C_GR, C_QS, C_KS, C_VS, C_AR, C_AS = 0, 512, 1024, 2048, 3072, 3584, 4096, 4608, 5632

V_SH1, V_SC1, V_GT1, V_SH2, V_SC2, V_GT2, V_G1, V_G2, V_G3, V_G4 = range(10)
P_DSH1, P_DSC1, P_DGT1, P_DSH2, P_DSC2, P_DGT2, P_DG1, P_DG2, P_DG3, P_DG4, P_DGN, P_LOSS = range(12)
N_PAY = 12


def _cp(sem=None, **kw):
    if sem is not None:
        kw["dimension_semantics"] = sem
    return pltpu.CompilerParams(vmem_limit_bytes=VMEM_LIMIT, **kw)


def _dot(a, b):
    return jnp.dot(a, b, preferred_element_type=F32)


def _dot_nt(a, b):
    return lax.dot_general(a, b, (((1,), (1,)), ((), ())), preferred_element_type=F32)


def _dot_tn(a, b):
    return lax.dot_general(a, b, (((0,), (0,)), ((), ())), preferred_element_type=F32)


def _row(ref, i):
    return ref[i:i + 1, :]


def _rms(v):
    return lax.rsqrt(jnp.mean(v * v, axis=1, keepdims=True) + EPS)


def _colsum(v):
    return jnp.sum(v, axis=0, keepdims=True)


def _rowmean(v):
    return jnp.mean(v, axis=1, keepdims=True)


def _sigmoid(v):
    return 1.0 / (1.0 + jnp.exp(-v))


def _cast_bf16(w, kidx, dep, name):
    rows, cols = w.shape
    tr = min(rows, 512)

    def body(k_ref, w_ref, dep_ref, o_ref):
        o_ref[...] = w_ref[...].astype(BF)

    return pl.pallas_call(
        body, name=name,
        grid_spec=pltpu.PrefetchScalarGridSpec(
            num_scalar_prefetch=1, grid=(rows // tr,),
            in_specs=[pl.BlockSpec((tr, cols), lambda i, k_ref: (i, 0)), ANY],
            out_specs=pl.BlockSpec((None, tr, cols), lambda i, k_ref: (k_ref[0], i, 0))),
        out_shape=jax.ShapeDtypeStruct((N_CHIPS, rows, cols), BF),
        compiler_params=_cp(("parallel",)),
    )(kidx, w, dep)


def _adamw_math(w, g, m, v):
    m = ADAM_B1 * m + (1.0 - ADAM_B1) * g
    v = ADAM_B2 * v + (1.0 - ADAM_B2) * (g * g)
    m_hat = m / (1.0 - ADAM_B1 ** ADAM_STEP)
    v_hat = v / (1.0 - ADAM_B2 ** ADAM_STEP)
    delta = -ADAM_LR * (m_hat / (jnp.sqrt(v_hat) + ADAM_EPS) + ADAM_WD * w)
    return delta, m, v


def _adamw(w, m, v, g, name, dep=None):
    rows, cols = w.shape
    tr = min(rows, 256)

    def body(w_ref, m_ref, v_ref, g_ref, go_ref, d_ref, mo_ref, vo_ref):
        gg = g_ref[...]
        d, mm, vv = _adamw_math(w_ref[...], gg, m_ref[...], v_ref[...])
        go_ref[...] = gg
        d_ref[...] = d
        mo_ref[...] = mm
        vo_ref[...] = vv

    spec = pl.BlockSpec((tr, cols), lambda i: (i, 0))
    shp = jax.ShapeDtypeStruct((rows, cols), F32)
    body, in_specs, args = _add_dep(body, [spec] * 4, [w, m, v, g], dep)
    return pl.pallas_call(
        body, name=name, grid=(rows // tr,),
        in_specs=in_specs, out_specs=[spec] * 4, out_shape=[shp] * 4,
        compiler_params=_cp(("parallel",)),
    )(*args)


def _place():
    x, y, c = lax.axis_index("x"), lax.axis_index("y"), lax.axis_index("c")
    return x, y, c


HBM = pl.BlockSpec(memory_space=pltpu.HBM)
SEM = pl.BlockSpec(memory_space=pltpu.SEMAPHORE)
EFFECT = pltpu.SideEffectType.DATAFLOW_SIDE_EFFECTING


def _tie(token, small):
    return small + token[0, 0]


def _add_dep(body, in_specs, args, dep):
    if dep is None:
        return body, list(in_specs), list(args)
    n = len(args)

    def wrapped(*refs):
        body(*refs[:n], *refs[n + 1:])

    return wrapped, list(in_specs) + [ANY], list(args) + [dep]


def _split_call(name, bufs, run, old=None, after=None, new=0):
    nb = len(bufs)
    n_in = nb + (3 if old is not None else 0)

    def body(*refs):
        old_sems = (refs[nb], refs[nb + 1]) if old is not None else None
        new_sems = (refs[n_in], refs[n_in + 1]) if new else None
        run(refs[:nb], old_sems, new_sems)
        if new:
            refs[-1][...] = jnp.zeros_like(refs[-1])

    in_specs = [HBM] * nb + ([SEM, SEM, ANY] if old is not None else [])
    out_shape = [pltpu.SemaphoreType.DMA((new,))] * 2 if new else []
    out_specs = [SEM, SEM] if new else []
    out_shape += [pltpu.HBM(b.shape, b.dtype) for b in bufs]
    out_specs += [HBM] * nb
    if new:
        out_shape.append(jax.ShapeDtypeStruct((8, 128), F32))
        out_specs.append(pl.BlockSpec(memory_space=pltpu.VMEM))
    first = 2 if new else 0
    args = [pltpu.with_memory_space_constraint(b, pltpu.HBM) for b in bufs]
    if old is not None:
        args += [old[0], old[1], after]
    outs = pl.pallas_call(
        body, name=name, in_specs=tuple(in_specs), out_specs=tuple(out_specs), out_shape=tuple(out_shape),
        input_output_aliases={i: i + first for i in range(nb)},
        compiler_params=pltpu.CompilerParams(has_side_effects=EFFECT),
    )(*args)
    thru = list(outs[first:first + nb])
    if new:
        return thru, (outs[0], outs[1]), outs[-1]
    return thru, None, None


def _remote(part_src, part_dst, sems, i, to):
    return pltpu.make_async_remote_copy(src_ref=part_src, dst_ref=part_dst, send_sem=sems[0].at[i],
                                        recv_sem=sems[1].at[i], device_id=to, device_id_type=MESH)


def _other_chips(x, y):
    return [(1 - x, y), (x, 1 - y), (1 - x, 1 - y)]


def _gather_start(name, bufs):
    def run(refs, old, new):
        x, y, c = _place()
        k = 2 * x + y
        for w, ref in enumerate(refs):
            rh = bufs[w].shape[1] // 2
            part = ref.at[k, pl.ds(c * rh, rh)]
            for j, (cx, cy) in enumerate(_other_chips(x, y)):
                _remote(part, part, new, 3 * w + j, (cx, cy, c)).start()

    return _split_call(name, bufs, run, new=3 * len(bufs))


def _gather_pass(name, bufs, sems, after):
    def run(refs, old, new):
        x, y, c = _place()
        k = 2 * x + y
        sib = (x, y, 1 - c)
        for w, ref in enumerate(refs):
            rh = bufs[w].shape[1] // 2
            for j, (cx, cy) in enumerate(_other_chips(x, y)):
                land = ref.at[2 * cx + cy, pl.ds(c * rh, rh)]
                _remote(land, land, old, 3 * w + j, (cx, cy, c)).wait_recv()
                _remote(land, land, new, 3 * w + j, sib).start()
        for w, ref in enumerate(refs):
            rh = bufs[w].shape[1] // 2
            part = ref.at[k, pl.ds(c * rh, rh)]
            for j, (cx, cy) in enumerate(_other_chips(x, y)):
                _remote(part, part, old, 3 * w + j, (cx, cy, c)).wait_send()

    return _split_call(name, bufs, run, old=sems, after=after, new=3 * len(bufs))


def _gather_finish(name, bufs, sems, after):
    def run(refs, old, new):
        x, y, c = _place()
        sib = (x, y, 1 - c)
        for w, ref in enumerate(refs):
            rh = bufs[w].shape[1] // 2
            for j, (cx, cy) in enumerate(_other_chips(x, y)):
                sent = ref.at[2 * cx + cy, pl.ds(c * rh, rh)]
                _remote(sent, sent, old, 3 * w + j, sib).wait_send()
                land = ref.at[2 * cx + cy, pl.ds((1 - c) * rh, rh)]
                _remote(land, land, old, 3 * w + j, sib).wait_recv()

    return _split_call(name, bufs, run, old=sems, after=after)[0]


def _pair_send_start(name, grads):
    n = len(grads)
    lands = [lax.empty((N_CHIPS,) + g.shape[2:], F32) for g in grads]

    def run(refs, old, new):
        x, y, c = _place()
        for w in range(n):
            _remote(refs[w].at[:, 1 - c], refs[n + w], new, w, (x, y, 1 - c)).start()

    return _split_call(name, list(grads) + lands, run, new=n)


def _pair_send_wait(name, bufs, sems, after):
    n = len(bufs) // 2

    def run(refs, old, new):
        x, y, c = _place()
        for w in range(n):
            cp = _remote(refs[w].at[:, 1 - c], refs[n + w], old, w, (x, y, 1 - c))
            cp.wait_send()
            cp.wait_recv()

    thru = _split_call(name, bufs, run, old=sems, after=after)[0]
    return thru[:n], thru[n:]


def _pair_add(g, recv, cidx, name):
    _, _, rh, cols = g.shape
    tr = min(rh, 256)

    def body(c_ref, g_ref, r_ref, o_ref):
        o_ref[...] = (g_ref[...] + r_ref[...]).astype(BF)

    return pl.pallas_call(
        body, name=name,
        grid_spec=pltpu.PrefetchScalarGridSpec(
            num_scalar_prefetch=1, grid=(rh // tr,),
            in_specs=[pl.BlockSpec((N_CHIPS, None, tr, cols), lambda i, c_ref: (0, c_ref[0], i, 0)),
                      pl.BlockSpec((N_CHIPS, tr, cols), lambda i, c_ref: (0, i, 0))],
            out_specs=pl.BlockSpec((N_CHIPS, tr, cols), lambda i, c_ref: (0, i, 0))),
        out_shape=jax.ShapeDtypeStruct((N_CHIPS, rh, cols), BF),
        compiler_params=_cp(("parallel",)),
    )(cidx, g, recv)


def _chip_send_start(name, sums):
    n = len(sums)
    lands = [lax.empty((3,) + s.shape[1:], BF) for s in sums]

    def run(refs, old, new):
        x, y, c = _place()
        for w in range(n):
            for j, (cx, cy) in enumerate(_other_chips(x, y)):
                _remote(refs[w].at[2 * cx + cy], refs[n + w].at[j], new, 3 * w + j, (cx, cy, c)).start()

    return _split_call(name, list(sums) + lands, run, new=3 * n)


def _chip_send_wait(name, bufs, sems, after):
    n = len(bufs) // 2

    def run(refs, old, new):
        x, y, c = _place()
        for w in range(n):
            for j, (cx, cy) in enumerate(_other_chips(x, y)):
                cp = _remote(refs[w].at[2 * cx + cy], refs[n + w].at[j], old, 3 * w + j, (cx, cy, c))
                cp.wait_send()
                cp.wait_recv()

    thru = _split_call(name, bufs, run, old=sems, after=after)[0]
    return thru[:n], thru[n:]


def _chip_add(own, parts, kidx, cidx, name):
    _, rh, cols = parts.shape
    tr = min(rh, 512)

    def body(k_ref, c_ref, own_ref, p_ref, o_ref):
        acc = own_ref[...].astype(F32)
        for s in range(3):
            acc = acc + p_ref[s].astype(F32)
        o_ref[...] = acc

    return pl.pallas_call(
        body, name=name,
        grid_spec=pltpu.PrefetchScalarGridSpec(
            num_scalar_prefetch=2, grid=(rh // tr,),
            in_specs=[pl.BlockSpec((None, tr, cols), lambda i, k_ref, c_ref: (k_ref[0], i, 0)),
                      pl.BlockSpec((3, tr, cols), lambda i, k_ref, c_ref: (0, i, 0))],
            out_specs=pl.BlockSpec((None, tr, cols), lambda i, k_ref, c_ref: (c_ref[0], i, 0))),
        out_shape=jax.ShapeDtypeStruct((2, rh, cols), F32),
        compiler_params=_cp(("parallel",)),
    )(kidx, cidx, own, parts)


def _pair_swap_start(name, bufs):
    def run(refs, old, new):
        x, y, c = _place()
        for w, ref in enumerate(refs):
            _remote(ref.at[c], ref.at[c], new, w, (x, y, 1 - c)).start()

    return _split_call(name, bufs, run, new=len(bufs))


def _pair_swap_wait(name, bufs, sems, after):
    def run(refs, old, new):
        x, y, c = _place()
        for w, ref in enumerate(refs):
            _remote(ref.at[c], ref.at[c], old, w, (x, y, 1 - c)).wait_send()
            _remote(ref.at[1 - c], ref.at[1 - c], old, w, (x, y, 1 - c)).wait_recv()

    return _split_call(name, bufs, run, old=sems, after=after)[0]


def _peers(x, y, c):
    out = []
    for code in range(1, 8):
        fx, fy, fc = (code >> 2) & 1, (code >> 1) & 1, code & 1
        px = 1 - x if fx else x
        py = 1 - y if fy else y
        pc = 1 - c if fc else c
        out.append((code, (px, py, pc)))
    return out


def _mod_exchange(c_row, ada_w, ada_b4):
    ncol = ada_w.shape[1]

    def body(c_ref, w_ref, b_ref, call_ref, mod_ref, part_ref, send_sems, recv_sems):
        x, y, c = _place()
        k = 2 * x + y
        me = 4 * x + 2 * y + c
        call_ref[pl.ds(me, 1), :] = c_ref[...]
        sends = []
        for code, peer in _peers(x, y, c):
            cp = pltpu.make_async_remote_copy(
                src_ref=c_ref, dst_ref=call_ref.at[pl.ds(me, 1), :],
                send_sem=send_sems.at[code], recv_sem=recv_sems.at[code],
                device_id=peer, device_id_type=MESH)
            cp.start()
            sends.append(cp)
        for code, (px, py, pc) in _peers(x, y, c):
            land = call_ref.at[pl.ds(4 * px + 2 * py + pc, 1), :]
            pltpu.make_async_remote_copy(
                src_ref=land, dst_ref=land, send_sem=send_sems.at[code], recv_sem=recv_sems.at[code],
                device_id=(px, py, pc), device_id_type=MESH).wait_recv()
        call = call_ref[...]
        act = call * _sigmoid(call)
        part = jnp.dot(act, w_ref[...], preferred_element_type=F32,
                       precision=lax.Precision.HIGHEST) + b_ref[pl.ds(k, 1), :]
        part_ref[...] = part
        mod_ref[pl.ds(k, 1), :] = part_ref[pl.ds(me, 1), :]
        chips = [(8 + j, peer) for j, (code, peer) in enumerate(_peers(x, y, c)) if code in (2, 4, 6)]
        for slot, (px, py, pc) in chips:
            cp = pltpu.make_async_remote_copy(
                src_ref=part_ref.at[pl.ds(4 * px + 2 * py + pc, 1), :], dst_ref=mod_ref.at[pl.ds(k, 1), :],
                send_sem=send_sems.at[slot], recv_sem=recv_sems.at[slot],
                device_id=(px, py, pc), device_id_type=MESH)
            cp.start()
            sends.append(cp)
        for slot, (px, py, pc) in chips:
            land = mod_ref.at[pl.ds(2 * px + py, 1), :]
            pltpu.make_async_remote_copy(
                src_ref=land, dst_ref=land, send_sem=send_sems.at[slot], recv_sem=recv_sems.at[slot],
                device_id=(px, py, pc), device_id_type=MESH).wait_recv()
        for cp in sends:
            cp.wait_send()

    vm = pl.BlockSpec(memory_space=pltpu.VMEM)
    return pl.pallas_call(
        body, name="mod_exchange",
        in_specs=[vm, vm, vm], out_specs=[vm, vm],
        out_shape=[jax.ShapeDtypeStruct((8, D_MODEL), F32), jax.ShapeDtypeStruct((N_CHIPS, ncol), F32)],
        scratch_shapes=[pltpu.VMEM((8, ncol), F32), pltpu.SemaphoreType.DMA((16,)),
                        pltpu.SemaphoreType.DMA((16,))],
        compiler_params=_cp(),
    )(c_row, ada_w, ada_b4)


def _small_exchange(payload, c_all, wsm, msm, vsm):
    ncol = 6 * D_MODEL // N_CHIPS

    def body(p_ref, call_ref, w_ref, m_ref, v_ref, gw_ref, g_ref, d_ref, mo_ref, vo_ref, loss_ref,
             all_ref, dm_ref, send_sems, recv_sems):
        x, y, c = _place()
        k = 2 * x + y
        me = 4 * x + 2 * y + c
        all_ref[:, pl.ds(me, 1), :] = p_ref[...]
        sends = []
        for code, peer in _peers(x, y, c):
            cp = pltpu.make_async_remote_copy(
                src_ref=p_ref, dst_ref=all_ref.at[:, pl.ds(me, 1), :],
                send_sem=send_sems.at[code], recv_sem=recv_sems.at[code],
                device_id=peer, device_id_type=MESH)
            cp.start()
            sends.append(cp)
        for code, (px, py, pc) in _peers(x, y, c):
            land = all_ref.at[:, pl.ds(4 * px + 2 * py + pc, 1), :]
            pltpu.make_async_remote_copy(
                src_ref=land, dst_ref=land, send_sem=send_sems.at[code], recv_sem=recv_sems.at[code],
                device_id=(px, py, pc), device_id_type=MESH).wait_recv()
        for cp in sends:
            cp.wait_send()
        tot = [_colsum(all_ref[r]) for r in range(N_PAY)]
        loss_ref[...] = jnp.sum(tot[P_LOSS], axis=1, keepdims=True)
        g_ref[...] = jnp.zeros_like(g_ref)
        for r in range(P_LOSS):
            g_ref[r:r + 1, :] = tot[r]
        g = g_ref[...]
        d, mm, vv = _adamw_math(w_ref[...], g, m_ref[...], v_ref[...])
        d_ref[...] = d
        mo_ref[...] = mm
        vo_ref[...] = vv
        half = D_MODEL // 2
        for kk in range(N_CHIPS):
            @pl.when(k == kk)
            def _():
                r0 = 3 * (kk // 2)
                if kk % 2 == 0:
                    dm_ref[:, :D_MODEL] = all_ref[r0]
                    dm_ref[:, D_MODEL:] = all_ref[r0 + 1][:, :half]
                else:
                    dm_ref[:, :half] = all_ref[r0 + 1][:, half:]
                    dm_ref[:, half:] = all_ref[r0 + 2]
        call = call_ref[...]
        act = call * _sigmoid(call)
        gw_ref[...] = lax.dot_general(act, dm_ref[...], (((0,), (0,)), ((), ())),
                                      preferred_element_type=F32, precision=lax.Precision.HIGHEST)

    vm = pl.BlockSpec(memory_space=pltpu.VMEM)
    small = jax.ShapeDtypeStruct((16, D_MODEL), F32)
    return pl.pallas_call(
        body, name="small_exchange",
        in_specs=[vm] * 5, out_specs=[vm] * 6,
        out_shape=[jax.ShapeDtypeStruct((D_MODEL, ncol), F32), small, small, small, small,
                   jax.ShapeDtypeStruct((1, 1), F32)],
        scratch_shapes=[pltpu.VMEM((N_PAY, 8, D_MODEL), F32), pltpu.VMEM((8, ncol), F32),
                        pltpu.SemaphoreType.DMA((8,)), pltpu.SemaphoreType.DMA((8,))],
        compiler_params=_cp(),
    )(payload, c_all, wsm, msm, vsm)


def _rope_tables(pos_col, inv_freq, sign):
    def body(p_ref, f_ref, s_ref, cos_ref, sin_ref):
        ang = p_ref[...].astype(F32) * f_ref[...]
        cos_ref[...] = jnp.cos(ang)
        sin_ref[...] = jnp.sin(ang) * s_ref[...]

    tr = 512
    shp = jax.ShapeDtypeStruct((SEQ, 128), F32)
    return pl.pallas_call(
        body, name="rope_tables", grid=(SEQ // tr,),
        in_specs=[pl.BlockSpec((tr, 1), lambda i: (i, 0)), pl.BlockSpec((1, 128), lambda i: (0, 0)),
                  pl.BlockSpec((1, 128), lambda i: (0, 0))],
        out_specs=[pl.BlockSpec((tr, 128), lambda i: (i, 0))] * 2, out_shape=[shp, shp],
        compiler_params=_cp(("parallel",)),
    )(pos_col, inv_freq, sign)


def _resident(shape):
    nd = len(shape)
    return pl.BlockSpec(shape, lambda *_: (0,) * nd, pipeline_mode=pl.Buffered(1))


def _ln_proj(x, vecs, w_in4):
    tm = 256
    wc = w_in4.shape[2]

    def body(x_ref, vec_ref, w_ref, h_ref, proj_ref):
        xx = x_ref[...]
        g = _row(vec_ref, V_G1) * (1.0 + _row(vec_ref, V_SC1))
        h = (xx * _rms(xx) * g + _row(vec_ref, V_SH1)).astype(BF)
        h_ref[...] = h
        for j in range(N_CHIPS):
            proj_ref[:, j * wc:(j + 1) * wc] = _dot(h, w_ref[j]).astype(BF)

    return pl.pallas_call(
        body, name="ln_proj", grid=(SEQ // tm,),
        in_specs=[pl.BlockSpec((tm, D_MODEL), lambda i: (i, 0)), _resident((16, D_MODEL)),
                  _resident(w_in4.shape)],
        out_specs=[pl.BlockSpec((tm, D_MODEL), lambda i: (i, 0)), pl.BlockSpec((tm, D_IN), lambda i: (i, 0))],
        out_shape=[jax.ShapeDtypeStruct((SEQ, D_MODEL), BF), jax.ShapeDtypeStruct((SEQ, D_IN), BF)],
        compiler_params=_cp(("parallel",)),
    )(x, vecs, w_in4)


def _lane_first(shape):
    lane = lax.broadcasted_iota(jnp.int32, shape, 1)
    return (lane & 32) == 0


def _rot(v, cos, sin_s):
    partner = jnp.where(_lane_first(v.shape), pltpu.roll(v, 96, 1), pltpu.roll(v, 32, 1))
    return v * cos + partner * sin_s


def _rot_t(dv, cos, sin_s):
    t = dv * sin_s
    partner = jnp.where(_lane_first(dv.shape), pltpu.roll(t, 96, 1), pltpu.roll(t, 32, 1))
    return dv * cos + partner


def _ret_masks(lg):
    t = RET_BLOCK
    ii = lax.broadcasted_iota(jnp.int32, (t, t), 0)
    jj = lax.broadcasted_iota(jnp.int32, (t, t), 1)
    dist = jnp.abs(ii - jj).astype(F32)
    future = (jj >> RET_CHUNK_SHIFT) > (ii >> RET_CHUNK_SHIFT)
    mask = jnp.where(future, 0.0, jnp.exp(lg * dist))
    ti = lax.broadcasted_iota(jnp.int32, (t, 1), 0).astype(F32)
    from_start = jnp.exp(lg * (ti + 1.0))
    to_end = jnp.exp(lg * (t - 1.0 - ti))
    whole = jnp.exp(jnp.full((1, 128), lg * t, F32))
    return mask, from_start, to_end, whole


def _head_lanes(shape, hh):
    lane = lax.broadcasted_iota(jnp.int32, shape, 1)
    return (lane >> 6) == hh


def _ret_specs():
    t = RET_BLOCK
    return dict(
        q=lambda f: pl.BlockSpec((t, 128), lambda p, n: (f(n), C_QR // 128 + p)),
        k=lambda f: pl.BlockSpec((t, 128), lambda p, n: (f(n), C_KR // 128 + p)),
        v=lambda f: pl.BlockSpec((t, 256), lambda p, n: (f(n), C_VR // 256 + p)),
        g=lambda f: pl.BlockSpec((t, 256), lambda p, n: (f(n), C_GR // 256 + p)),
        tab=lambda f: pl.BlockSpec((t, 128), lambda p, n: (f(n), 0)),
        wide=lambda f: pl.BlockSpec((t, 256), lambda p, n: (f(n), p)),
        narrow=lambda f: pl.BlockSpec((t, 128), lambda p, n: (f(n), p)),
        state=lambda f: pl.BlockSpec((None, None, 2, 128, 128), lambda p, n: (p, f(n), 0, 0, 0)),
    )


def _ret_fwd(proj, cos, sin_s, gn_g, log_gamma):
    t = RET_BLOCK
    nb = SEQ // t

    def body(lg_ref, q_ref, k_ref, v_ref, g_ref, cos_ref, sin_ref, gn_ref, o_ref, retg_ref, st_ref, state):
        p = pl.program_id(0)

        @pl.when(pl.program_id(1) == 0)
        def _():
            state[...] = jnp.zeros_like(state)

        cos, sn = cos_ref[...], sin_ref[...]
        q = _rot(q_ref[...].astype(F32), cos, sn)
        k = _rot(k_ref[...].astype(F32), cos, sn) * QK_SCALE
        for hh in range(2):
            lg = lg_ref[2 * p + hh]
            mask, from_start, to_end, whole = _ret_masks(lg)
            lanes = _head_lanes(q.shape, hh)
            qm = jnp.where(lanes, q, 0.0)
            km = jnp.where(lanes, k, 0.0)
            vh = v_ref[:, 128 * hh:128 * (hh + 1)]
            sc = _dot_nt(qm.astype(BF), km.astype(BF)) * mask
            st = state[hh]
            st_ref[hh] = st
            o = _dot(sc.astype(BF), vh) + _dot((qm * from_start).astype(BF), st.astype(BF))
            state[hh] = whole * st + _dot_tn((km * to_end).astype(BF), vh)
            d = o - _rowmean(o)
            nh = d * lax.rsqrt(_rowmean(d * d) + EPS)
            gr = g_ref[:, 128 * hh:128 * (hh + 1)].astype(F32)
            o_ref[:, 128 * hh:128 * (hh + 1)] = o
            retg_ref[:, 128 * hh:128 * (hh + 1)] = (
                gr * _sigmoid(gr) * nh * gn_ref[:, 128 * hh:128 * (hh + 1)]).astype(BF)

    sp = _ret_specs()
    ident = lambda n: n
    return pl.pallas_call(
        body, name="ret_fwd", grid=(N_PAIRS, nb),
        in_specs=[pl.BlockSpec(memory_space=pltpu.SMEM), sp["q"](ident), sp["k"](ident), sp["v"](ident),
                  sp["g"](ident), sp["tab"](ident), sp["tab"](ident),
                  pl.BlockSpec((1, 256), lambda p, n: (0, p))],
        out_specs=[sp["wide"](ident), sp["wide"](ident), sp["state"](ident)],
        out_shape=[jax.ShapeDtypeStruct((SEQ, D_MODEL), F32), jax.ShapeDtypeStruct((SEQ, D_MODEL), BF),
                   jax.ShapeDtypeStruct((N_PAIRS, nb, 2, 128, 128), F32)],
        scratch_shapes=[pltpu.VMEM((2, 128, 128), F32)],
        compiler_params=_cp(("parallel", "arbitrary")),
    )(log_gamma, proj, proj, proj, proj, cos, sin_s, gn_g)


def _ret_bwd(proj, cos, sin_s, dret, states, log_gamma):
    t = RET_BLOCK
    nb = SEQ // t

    def body(lg_ref, q_ref, k_ref, v_ref, cos_ref, sin_ref, do_ref, st_ref, dq_ref, dk_ref, dv_ref, dstate):
        p = pl.program_id(0)

        @pl.when(pl.program_id(1) == 0)
        def _():
            dstate[...] = jnp.zeros_like(dstate)

        cos, sn = cos_ref[...], sin_ref[...]
        q = _rot(q_ref[...].astype(F32), cos, sn)
        k = _rot(k_ref[...].astype(F32), cos, sn) * QK_SCALE
        dq_rot = jnp.zeros(q.shape, F32)
        dk_rot = jnp.zeros(q.shape, F32)
        for hh in range(2):
            lg = lg_ref[2 * p + hh]
            mask, from_start, to_end, whole = _ret_masks(lg)
            lanes = _head_lanes(q.shape, hh)
            qm = jnp.where(lanes, q, 0.0)
            km = jnp.where(lanes, k, 0.0)
            qb, kb = qm.astype(BF), km.astype(BF)
            vh = v_ref[:, 128 * hh:128 * (hh + 1)]
            do = do_ref[:, 128 * hh:128 * (hh + 1)]
            sc = (_dot_nt(qb, kb) * mask).astype(BF)
            st = st_ref[hh].astype(BF)
            dst = dstate[hh]
            dstb = dst.astype(BF)
            k_end = (km * to_end).astype(BF)
            q_start = (qm * from_start).astype(BF)
            dv_ref[:, 128 * hh:128 * (hh + 1)] = (_dot_tn(sc, do) + _dot(k_end, dstb)).astype(BF)
            dsc = (_dot_nt(do, vh) * mask).astype(BF)
            dq_h = _dot(dsc, kb) + _dot_nt(do, st) * from_start
            dq_rot = dq_rot + jnp.where(lanes, dq_h, 0.0)
            dk_rot = dk_rot + _dot_tn(dsc, qb) + _dot_nt(vh, dstb) * to_end
            dstate[hh] = whole * dst + _dot_tn(q_start, do)
        dq_ref[...] = _rot_t(dq_rot, cos, sn).astype(BF)
        dk_ref[...] = _rot_t(dk_rot * QK_SCALE, cos, sn).astype(BF)

    sp = _ret_specs()
    rev = lambda n: nb - 1 - n
    return pl.pallas_call(
        body, name="ret_bwd", grid=(N_PAIRS, nb),
        in_specs=[pl.BlockSpec(memory_space=pltpu.SMEM), sp["q"](rev), sp["k"](rev), sp["v"](rev),
                  sp["tab"](rev), sp["tab"](rev), sp["wide"](rev), sp["state"](rev)],
        out_specs=[sp["narrow"](rev), sp["narrow"](rev), sp["wide"](rev)],
        out_shape=[jax.ShapeDtypeStruct((SEQ, 512), BF), jax.ShapeDtypeStruct((SEQ, 512), BF),
                   jax.ShapeDtypeStruct((SEQ, D_MODEL), BF)],
        scratch_shapes=[pltpu.VMEM((2, 128, 128), F32)],
        compiler_params=_cp(("parallel", "arbitrary")),
    )(log_gamma, proj, proj, proj, cos, sin_s, dret, states)


def _stack_heads(v):
    return jnp.concatenate([jnp.where(_head_lanes(v.shape, hh), v, jnp.zeros_like(v)) for hh in range(2)], axis=0)


def _unstack_heads(v):
    t = v.shape[0] // 2
    return jnp.where(_head_lanes((t, v.shape[1]), 0), v[:t], v[t:])


def _sb_masks(t, heads):
    rr = lax.broadcasted_iota(jnp.int32, (t, t), 0)
    cc = lax.broadcasted_iota(jnp.int32, (t, t), 1)
    r2 = lax.broadcasted_iota(jnp.int32, (heads * t, t), 0) & (t - 1)
    c2 = lax.broadcasted_iota(jnp.int32, (heads * t, t), 1)
    return rr, cc, c2 < r2


def _split_dot2(v, tri):
    hi = v.astype(BF)
    lo = (v - hi.astype(F32)).astype(BF)
    both = _dot(jnp.concatenate([hi, lo], axis=0), tri)
    return both[:v.shape[0]] + both[v.shape[0]:]


def _log_one_minus_beta(z):
    return -(jnp.maximum(z, 0.0) + jnp.log(1.0 + jnp.exp(-jnp.abs(z))))


def _sb_fwd(proj):
    t, g = SB_BLOCK, SB_GROUP
    nq = SEQ // t
    rows = 2 * g * t

    def body(q_ref, k_ref, v_ref, o_ref, tot_ref, kt_ref):
        i = pl.program_id(1)

        @pl.when(i == 0)
        def _():
            for p in range(g):
                for jj in range(nq):
                    kt_ref[p, jj] = k_ref[jj * t:(jj + 1) * t, 128 * p:128 * (p + 1)].T

        q2 = [_stack_heads((q_ref[:, 128 * p:128 * (p + 1)].astype(F32) * QK_SCALE).astype(BF)) for p in range(g)]
        rr, cc, valid = _sb_masks(t, 2 * g)
        later = (rr > cc).astype(BF)

        def tile(j, carry, diagonal):
            acc, run = carry
            z = jnp.concatenate([_dot(q2[p], kt_ref[p, j]) for p in range(g)], axis=0)
            lm = _log_one_minus_beta(z)
            if diagonal:
                lm = jnp.where(valid, lm, 0.0)
            after = _split_dot2(lm, later)
            a = jnp.exp(z + lm + after + run)
            if diagonal:
                a = jnp.where(valid, a, 0.0)
            ab = a.astype(BF)
            keys = pl.ds(pl.multiple_of(j * t, t), t)
            av = jnp.concatenate([_dot(ab[2 * t * p:2 * t * (p + 1)], v_ref[keys, 128 * p:128 * (p + 1)])
                                  for p in range(g)], axis=0)
            return acc + av, run + after[:, 0:1] + lm[:, 0:1]

        carry = tile(i, (jnp.zeros((rows, 128), F32), jnp.zeros((rows, 1), F32)), True)
        acc, run = lax.fori_loop(0, i, lambda s, cr: tile(i - 1 - s, cr, False), carry)
        run = jnp.broadcast_to(run, (rows, 128))
        for p in range(g):
            o_ref[:, 128 * p:128 * (p + 1)] = _unstack_heads(acc[2 * t * p:2 * t * (p + 1)]).astype(BF)
            tot_ref[:, 128 * p:128 * (p + 1)] = _unstack_heads(run[2 * t * p:2 * t * (p + 1)])

    w = 128 * g
    return pl.pallas_call(
        body, name="sb_fwd", grid=(N_PAIRS // g, nq),
        in_specs=[pl.BlockSpec((t, w), lambda p, i: (i, C_QS // w + p)),
                  pl.BlockSpec((SEQ, w), lambda p, i: (0, C_KS // w + p)),
                  pl.BlockSpec((SEQ, w), lambda p, i: (0, C_VS // w + p))],
        out_specs=[pl.BlockSpec((t, w), lambda p, i: (i, p))] * 2,
        out_shape=[jax.ShapeDtypeStruct((SEQ, 512), BF), jax.ShapeDtypeStruct((SEQ, 512), F32)],
        scratch_shapes=[pltpu.VMEM((g, nq, 128, t), BF)],
        compiler_params=_cp(("parallel", "arbitrary")),
    )(proj, proj, proj)


def _sb_bwd(proj, dsb, tot, dep=None):
    t, g = SB_BLOCK, SB_GROUP
    nq = SEQ // t
    rows = 2 * g * t

    def body(q_ref, k_ref, v_ref, do_ref, tot_ref, dq_ref, dk_ref, dv_ref, kt_ref, vt_ref, dkt_acc, dvt_acc):
        i = pl.program_id(1)

        @pl.when(i == 0)
        def _():
            dkt_acc[...] = jnp.zeros_like(dkt_acc)
            dvt_acc[...] = jnp.zeros_like(dvt_acc)
            for p in range(g):
                for jj in range(nq):
                    kt_ref[p, jj] = k_ref[jj * t:(jj + 1) * t, 128 * p:128 * (p + 1)].T
                    vt_ref[p, jj] = v_ref[jj * t:(jj + 1) * t, 128 * p:128 * (p + 1)].T

        q2 = [_stack_heads((q_ref[:, 128 * p:128 * (p + 1)].astype(F32) * QK_SCALE).astype(BF)) for p in range(g)]
        do2 = [_stack_heads(do_ref[:, 128 * p:128 * (p + 1)]) for p in range(g)]
        q2t = [v.T for v in q2]
        do2t = [v.T for v in do2]
        tots = tot_ref[...]
        total = jnp.concatenate([tots[:, 64 * h:64 * h + 1] for h in range(2 * g)], axis=0)
        rr, cc, valid = _sb_masks(t, 2 * g)
        upto = (rr <= cc).astype(BF)
        before = (rr < cc).astype(BF)

        def part(v, p):
            return v[2 * t * p:2 * t * (p + 1)]

        def tile(j, carry, diagonal):
            dq, run_l, run_g = carry
            z = jnp.concatenate([_dot(q2[p], kt_ref[p, j]) for p in range(g)], axis=0)
            lm = _log_one_minus_beta(z)
            if diagonal:
                lm = jnp.where(valid, lm, 0.0)
            incl = _split_dot2(lm, upto)
            a = jnp.exp(z + lm + (total - (incl + run_l)))
            if diagonal:
                a = jnp.where(valid, a, 0.0)
            gg = a * jnp.concatenate([_dot(do2[p], vt_ref[p, j]) for p in range(g)], axis=0)
            excl = _split_dot2(gg, before)
            dz = gg * jnp.exp(lm) - (excl + run_g) * jnp.exp(z + lm)
            if diagonal:
                dz = jnp.where(valid, dz, 0.0)
            dzb = dz.astype(BF)
            ab = a.astype(BF)
            keys = pl.ds(pl.multiple_of(j * t, t), t)
            for p in range(g):
                dkt_acc[p, j] += _dot(q2t[p], part(dzb, p))
                dvt_acc[p, j] += _dot(do2t[p], part(ab, p))
            dq_t = jnp.concatenate([_dot(part(dzb, p), k_ref[keys, 128 * p:128 * (p + 1)]) for p in range(g)], axis=0)
            return (dq + dq_t, run_l + incl[:, t - 1:t], run_g + excl[:, t - 1:t] + gg[:, t - 1:t])

        zero = jnp.zeros((rows, 1), F32)
        carry = lax.fori_loop(0, i, lambda j, cr: tile(j, cr, False), (jnp.zeros((rows, 128), F32), zero, zero))
        dq = tile(i, carry, True)[0]
        for p in range(g):
            dq_ref[:, 128 * p:128 * (p + 1)] = (_unstack_heads(part(dq, p)) * QK_SCALE).astype(BF)

        @pl.when(i == nq - 1)
        def _():
            for p in range(g):
                for jj in range(nq):
                    dk_ref[jj * t:(jj + 1) * t, 128 * p:128 * (p + 1)] = dkt_acc[p, jj].T.astype(BF)
                    dv_ref[jj * t:(jj + 1) * t, 128 * p:128 * (p + 1)] = dvt_acc[p, jj].T.astype(BF)

    w = 128 * g
    tile_spec = pl.BlockSpec((t, w), lambda p, i: (i, p))
    col_spec = pl.BlockSpec((SEQ, w), lambda p, i: (0, p))
    shp = jax.ShapeDtypeStruct((SEQ, 512), BF)
    body, in_specs, args = _add_dep(
        body, [pl.BlockSpec((t, w), lambda p, i: (i, C_QS // w + p)),
               pl.BlockSpec((SEQ, w), lambda p, i: (0, C_KS // w + p)),
               pl.BlockSpec((SEQ, w), lambda p, i: (0, C_VS // w + p)),
               tile_spec, tile_spec],
        [proj, proj, proj, dsb, tot], dep)
    return pl.pallas_call(
        body, name="sb_bwd", grid=(N_PAIRS // g, nq),
        in_specs=in_specs,
        out_specs=[tile_spec, col_spec, col_spec],
        out_shape=[shp, shp, shp],
        scratch_shapes=[pltpu.VMEM((g, nq, 128, t), BF), pltpu.VMEM((g, nq, 128, t), BF),
                        pltpu.VMEM((g, nq, 128, t), F32), pltpu.VMEM((g, nq, 128, t), F32)],
        compiler_params=_cp(("parallel", "arbitrary")),
    )(*args)


def _mix(retg, sb, proj, w_ret, w_sb4):
    tm, tn = min(512, SEQ), 512

    def body(r_ref, s_ref, ar_ref, as_ref, wr_ref, ws_ref, mix_ref, rb_ref, sbp_ref):
        rb = _dot(r_ref[...], wr_ref[...])
        sbv = s_ref[...]
        sbp = jnp.concatenate([_dot(sbv, ws_ref[0]), _dot(sbv, ws_ref[1])], axis=1)
        mix = _sigmoid(ar_ref[...].astype(F32)) * rb + _sigmoid(as_ref[...].astype(F32)) * sbp
        mix_ref[...] = mix.astype(BF)
        rb_ref[...] = rb.astype(BF)
        sbp_ref[...] = sbp.astype(BF)

    out = pl.BlockSpec((tm, tn), lambda j, i: (i, j))
    shp = jax.ShapeDtypeStruct((SEQ, D_MODEL), BF)
    return pl.pallas_call(
        body, name="mix", grid=(D_MODEL // tn, SEQ // tm),
        in_specs=[pl.BlockSpec((tm, D_MODEL), lambda j, i: (i, 0)),
                  pl.BlockSpec((tm, 512), lambda j, i: (i, 0)),
                  pl.BlockSpec((tm, tn), lambda j, i: (i, C_AR // tn + j)),
                  pl.BlockSpec((tm, tn), lambda j, i: (i, C_AS // tn + j)),
                  pl.BlockSpec((D_MODEL, tn), lambda j, i: (0, j)),
                  pl.BlockSpec((2, 512, 256), lambda j, i: (j, 0, 0))],
        out_specs=[out, out, out], out_shape=[shp, shp, shp],
        compiler_params=_cp(("parallel", "parallel")),
    )(retg, sb, proj, proj, w_ret, w_sb4)


def _out_proj(mixed, x, vecs, w_out):
    tm = min(512, SEQ)

    def body(m_ref, x_ref, vec_ref, w_ref, y_ref, h1_ref, h2_ref):
        y = _dot(m_ref[...], w_ref[...])
        h1 = x_ref[...] + _row(vec_ref, V_GT1) * (y * _rms(y)) * _row(vec_ref, V_G2)
        g = _row(vec_ref, V_G3) * (1.0 + _row(vec_ref, V_SC2))
        y_ref[...] = y
        h1_ref[...] = h1
        h2_ref[...] = (h1 * _rms(h1) * g + _row(vec_ref, V_SH2)).astype(BF)

    row = pl.BlockSpec((tm, D_MODEL), lambda i: (i, 0))
    f32 = jax.ShapeDtypeStruct((SEQ, D_MODEL), F32)
    return pl.pallas_call(
        body, name="out_proj", grid=(SEQ // tm,),
        in_specs=[row, row, pl.BlockSpec((16, D_MODEL), lambda i: (0, 0)),
                  pl.BlockSpec((D_MODEL, D_MODEL), lambda i: (0, 0))],
        out_specs=[row, row, row],
        out_shape=[f32, f32, jax.ShapeDtypeStruct((SEQ, D_MODEL), BF)],
        compiler_params=_cp(("parallel",)),
    )(mixed, x, vecs, w_out)


def _ffn_up(h2, w_ff14):
    tm = min(1024, SEQ)

    def body(h_ref, w_ref, u_ref, a_ref):
        u = _dot(h_ref[...], w_ref[...])
        r = jnp.maximum(u, 0.0)
        u_ref[...] = u.astype(BF)
        a_ref[...] = (r * r).astype(BF)

    out = pl.BlockSpec((tm, D_MODEL), lambda j, i: (i, j))
    shp = jax.ShapeDtypeStruct((SEQ, D_FF), BF)
    return pl.pallas_call(
        body, name="ffn_up", grid=(N_CHIPS, SEQ // tm),
        in_specs=[pl.BlockSpec((tm, D_MODEL), lambda j, i: (i, 0)),
                  pl.BlockSpec((None, D_MODEL, D_MODEL), lambda j, i: (j, 0, 0))],
        out_specs=[out, out], out_shape=[shp, shp],
        compiler_params=_cp(("parallel", "parallel")),
    )(h2, w_ff14)


def _ffn_down_loss(act, h1, target, vecs, w_ff2):
    tm = min(512, SEQ)

    def body(a_ref, h1_ref, t_ref, vec_ref, w_ref, dout_ref, df_ref, st_ref):
        @pl.when(pl.program_id(0) == 0)
        def _():
            st_ref[...] = jnp.zeros_like(st_ref)

        f = _dot(a_ref[...], w_ref[...])
        r4 = _rms(f)
        fn = f * r4
        gt2, g4 = _row(vec_ref, V_GT2), _row(vec_ref, V_G4)
        diff = h1_ref[...] + gt2 * fn * g4 - t_ref[...]
        dout = diff * (1.0 / D_MODEL)
        dfn = dout * gt2 * g4
        dout_ref[...] = dout
        df_ref[...] = (r4 * (dfn - fn * _rowmean(dfn * fn))).astype(BF)
        st_ref[0:1, :] += _colsum(dout * fn * g4)
        st_ref[1:2, :] += _colsum(dout * gt2 * fn)
        st_ref[2:3, :] += _colsum(diff * diff) * (0.5 / D_MODEL)

    row = pl.BlockSpec((tm, D_MODEL), lambda i: (i, 0))
    return pl.pallas_call(
        body, name="ffn_down_loss", grid=(SEQ // tm,),
        in_specs=[pl.BlockSpec((tm, D_FF), lambda i: (i, 0)), row, row,
                  pl.BlockSpec((16, D_MODEL), lambda i: (0, 0)),
                  pl.BlockSpec((D_FF, D_MODEL), lambda i: (0, 0))],
        out_specs=[row, row, pl.BlockSpec((8, D_MODEL), lambda i: (0, 0))],
        out_shape=[jax.ShapeDtypeStruct((SEQ, D_MODEL), F32), jax.ShapeDtypeStruct((SEQ, D_MODEL), BF),
                   jax.ShapeDtypeStruct((8, D_MODEL), F32)],
        compiler_params=_cp(("arbitrary",)),
    )(act, h1, target, vecs, w_ff2)


def _ffn_down_bwd(df, u, w_ff2):
    tm, tn = min(512, SEQ), 2048

    def body(df_ref, u_ref, w_ref, du_ref):
        da = _dot_nt(df_ref[...], w_ref[...])
        du_ref[...] = (da * (2.0 * jnp.maximum(u_ref[...].astype(F32), 0.0))).astype(BF)

    return pl.pallas_call(
        body, name="ffn_down_bwd", grid=(D_FF // tn, SEQ // tm),
        in_specs=[pl.BlockSpec((tm, D_MODEL), lambda j, i: (i, 0)),
                  pl.BlockSpec((tm, tn), lambda j, i: (i, j)),
                  pl.BlockSpec((tn, D_MODEL), lambda j, i: (j, 0))],
        out_specs=pl.BlockSpec((tm, tn), lambda j, i: (i, j)),
        out_shape=jax.ShapeDtypeStruct((SEQ, D_FF), BF),
        compiler_params=_cp(("parallel", "parallel")),
    )(df, u, w_ff2)


def _ffn_up_bwd(du, h1, y, dout, vecs, w_ff14):
    tm = 256

    def body(du_ref, h1_ref, y_ref, dout_ref, vec_ref, w_ref, dh1_ref, dy_ref, st_ref):
        @pl.when(pl.program_id(0) == 0)
        def _():
            st_ref[...] = jnp.zeros_like(st_ref)

        dh2 = _dot_nt(du_ref[:, 0:D_MODEL], w_ref[0])
        for k in range(1, N_CHIPS):
            dh2 = dh2 + _dot_nt(du_ref[:, k * D_MODEL:(k + 1) * D_MODEL], w_ref[k])
        h1 = h1_ref[...]
        r3 = _rms(h1)
        hn3 = h1 * r3
        g3, sc2 = _row(vec_ref, V_G3), _row(vec_ref, V_SC2)
        dhn3 = dh2 * g3 * (1.0 + sc2)
        dh1 = dout_ref[...] + r3 * (dhn3 - hn3 * _rowmean(dhn3 * hn3))
        y = y_ref[...]
        r2 = _rms(y)
        yn = y * r2
        gt1, g2 = _row(vec_ref, V_GT1), _row(vec_ref, V_G2)
        dyn = dh1 * gt1 * g2
        dh1_ref[...] = dh1
        dy_ref[...] = (r2 * (dyn - yn * _rowmean(dyn * yn))).astype(BF)
        st_ref[0:1, :] += _colsum(dh2)
        st_ref[1:2, :] += _colsum(dh2 * hn3 * g3)
        st_ref[2:3, :] += _colsum(dh2 * hn3 * (1.0 + sc2))
        st_ref[3:4, :] += _colsum(dh1 * yn * g2)
        st_ref[4:5, :] += _colsum(dh1 * gt1 * yn)

    row = pl.BlockSpec((tm, D_MODEL), lambda i: (i, 0))
    return pl.pallas_call(
        body, name="ffn_up_bwd", grid=(SEQ // tm,),
        in_specs=[pl.BlockSpec((tm, D_FF), lambda i: (i, 0)), row, row, row,
                  _resident((16, D_MODEL)), _resident(w_ff14.shape)],
        out_specs=[row, row, pl.BlockSpec((8, D_MODEL), lambda i: (0, 0))],
        out_shape=[jax.ShapeDtypeStruct((SEQ, D_MODEL), F32), jax.ShapeDtypeStruct((SEQ, D_MODEL), BF),
                   jax.ShapeDtypeStruct((8, D_MODEL), F32)],
        compiler_params=_cp(("arbitrary",)),
    )(du, h1, y, dout, vecs, w_ff14)


def _mix_bwd(dy, proj, rb, sbp, w_out, w_sb4):
    tm, tn = min(512, SEQ), 512

    def body(dy_ref, ar_ref, as_ref, rb_ref, sbp_ref, wo_ref, ws_ref,
             drb_ref, dsbp_ref, dar_ref, das_ref, dsb_ref, acc):
        j = pl.program_id(1)
        dm = _dot_nt(dy_ref[...], wo_ref[...])
        sr = _sigmoid(ar_ref[...].astype(F32))
        ss = _sigmoid(as_ref[...].astype(F32))
        dsbp = (dm * ss).astype(BF)
        drb_ref[...] = (dm * sr).astype(BF)
        dsbp_ref[...] = dsbp
        dar_ref[...] = (dm * rb_ref[...].astype(F32) * sr * (1.0 - sr)).astype(BF)
        das_ref[...] = (dm * sbp_ref[...].astype(F32) * ss * (1.0 - ss)).astype(BF)
        part = _dot_nt(dsbp[:, :256], ws_ref[0]) + _dot_nt(dsbp[:, 256:], ws_ref[1])

        @pl.when(j == 0)
        def _():
            acc[...] = part

        @pl.when(j > 0)
        def _():
            dsb_ref[...] = (acc[...] + part).astype(BF)

    tile = pl.BlockSpec((tm, tn), lambda i, j: (i, j))
    shp = jax.ShapeDtypeStruct((SEQ, D_MODEL), BF)
    return pl.pallas_call(
        body, name="mix_bwd", grid=(SEQ // tm, D_MODEL // tn),
        in_specs=[pl.BlockSpec((tm, D_MODEL), lambda i, j: (i, 0)),
                  pl.BlockSpec((tm, tn), lambda i, j: (i, C_AR // tn + j)),
                  pl.BlockSpec((tm, tn), lambda i, j: (i, C_AS // tn + j)),
                  tile, tile,
                  pl.BlockSpec((tn, D_MODEL), lambda i, j: (j, 0)),
                  pl.BlockSpec((2, 512, 256), lambda i, j: (j, 0, 0))],
        out_specs=[tile, tile, tile, tile, pl.BlockSpec((tm, 512), lambda i, j: (i, 0))],
        out_shape=[shp, shp, shp, shp, jax.ShapeDtypeStruct((SEQ, 512), BF)],
        scratch_shapes=[pltpu.VMEM((tm, 512), F32)],
        compiler_params=_cp(("parallel", "arbitrary")),
    )(dy, proj, proj, rb, sbp, w_out, w_sb4)


def _ret_branch_bwd(drb, proj, o_raw, gn_g, w_ret):
    tm, tn = min(512, SEQ), 512

    def body(d_ref, g_ref, o_ref, gn_ref, w_ref, dret_ref, dgr_ref, st_ref):
        @pl.when(pl.program_id(1) == 0)
        def _():
            st_ref[...] = jnp.zeros_like(st_ref)

        dretg = _dot_nt(d_ref[...], w_ref[...])
        for gi in range(tn // 128):
            cols = slice(128 * gi, 128 * (gi + 1))
            o = o_ref[:, cols]
            d = o - _rowmean(o)
            rstd = lax.rsqrt(_rowmean(d * d) + EPS)
            nh = d * rstd
            gain = gn_ref[:, cols]
            gr = g_ref[:, cols].astype(F32)
            sg = _sigmoid(gr)
            dg = dretg[:, cols]
            dgn = dg * gr * sg
            dnh = dgn * gain
            dgr_ref[:, cols] = (dg * nh * gain * sg * (1.0 + gr * (1.0 - sg))).astype(BF)
            dret_ref[:, cols] = (rstd * (dnh - _rowmean(dnh) - nh * _rowmean(dnh * nh))).astype(BF)
            st_ref[0:1, cols] += _colsum(dgn * nh)

    tile = pl.BlockSpec((tm, tn), lambda j, i: (i, j))
    shp = jax.ShapeDtypeStruct((SEQ, D_MODEL), BF)
    return pl.pallas_call(
        body, name="ret_branch_bwd", grid=(D_MODEL // tn, SEQ // tm),
        in_specs=[pl.BlockSpec((tm, D_MODEL), lambda j, i: (i, 0)),
                  pl.BlockSpec((tm, tn), lambda j, i: (i, C_GR // tn + j)),
                  tile, pl.BlockSpec((1, tn), lambda j, i: (0, j)),
                  pl.BlockSpec((tn, D_MODEL), lambda j, i: (j, 0))],
        out_specs=[tile, tile, pl.BlockSpec((8, tn), lambda j, i: (0, j))],
        out_shape=[shp, shp, jax.ShapeDtypeStruct((8, D_MODEL), F32)],
        compiler_params=_cp(("parallel", "arbitrary")),
    )(drb, proj, o_raw, gn_g, w_ret)


def _in_proj_bwd(dproj, x, dh1, vecs, w_in4):
    tm = 256
    wc = w_in4.shape[2]

    def body(dp_ref, x_ref, dh1_ref, vec_ref, w_ref, dx_ref, st_ref):
        @pl.when(pl.program_id(0) == 0)
        def _():
            st_ref[...] = jnp.zeros_like(st_ref)

        dh = _dot_nt(dp_ref[:, 0:wc], w_ref[0])
        for k in range(1, N_CHIPS):
            dh = dh + _dot_nt(dp_ref[:, k * wc:(k + 1) * wc], w_ref[k])
        xx = x_ref[...]
        r1 = _rms(xx)
        xn = xx * r1
        g1, sc1 = _row(vec_ref, V_G1), _row(vec_ref, V_SC1)
        dxn = dh * g1 * (1.0 + sc1)
        dx_ref[...] = dh1_ref[...] + r1 * (dxn - xn * _rowmean(dxn * xn))
        st_ref[0:1, :] += _colsum(dh)
        st_ref[1:2, :] += _colsum(dh * xn * g1)
        st_ref[2:3, :] += _colsum(dh * xn * (1.0 + sc1))

    row = pl.BlockSpec((tm, D_MODEL), lambda i: (i, 0))
    return pl.pallas_call(
        body, name="in_proj_bwd", grid=(SEQ // tm,),
        in_specs=[pl.BlockSpec((tm, D_IN), lambda i: (i, 0)), row, row,
                  _resident((16, D_MODEL)), _resident(w_in4.shape)],
        out_specs=[row, pl.BlockSpec((8, D_MODEL), lambda i: (0, 0))],
        out_shape=[jax.ShapeDtypeStruct((SEQ, D_MODEL), F32), jax.ShapeDtypeStruct((8, D_MODEL), F32)],
        compiler_params=_cp(("arbitrary",)),
    )(dproj, x, dh1, vecs, w_in4)


def _weight_grad(a, b, ta, tb, col_sharded, name, dep=None):
    ka, nb_ = a.shape[1], b.shape[1]

    def body(a_ref, b_ref, o_ref):
        o_ref[...] = _dot_tn(a_ref[...], b_ref[...])

    body, in_specs, args = _add_dep(
        body, [pl.BlockSpec((SEQ, ta), lambda i, j: (0, i)), pl.BlockSpec((SEQ, tb), lambda i, j: (0, j))],
        [a, b], dep)

    if col_sharded:
        per = nb_ // N_CHIPS // tb
        out_shape = jax.ShapeDtypeStruct((N_CHIPS, ka, nb_ // N_CHIPS), F32)
        out_spec = pl.BlockSpec((None, ta, tb), lambda i, j: (j // per, i, j % per))
    else:
        per = ka // N_CHIPS // ta
        out_shape = jax.ShapeDtypeStruct((N_CHIPS, ka // N_CHIPS, nb_), F32)
        out_spec = pl.BlockSpec((None, ta, tb), lambda i, j: (i // per, i % per, j))
    return pl.pallas_call(
        body, name=name, grid=(ka // ta, nb_ // tb),
        in_specs=in_specs, out_specs=out_spec, out_shape=out_shape,
        compiler_params=_cp(("parallel", "parallel")),
    )(*args)


def _rope_constants():
    freq = np.float32(ROPE_BASE) ** (-np.arange(0, 64, 2, dtype=np.float32) / np.float32(64))
    inv = np.tile(freq.astype(np.float32), 4).reshape(1, 128)
    sign = np.tile(np.concatenate([-np.ones(32, np.float32), np.ones(32, np.float32)]), 2).reshape(1, 128)
    return jnp.asarray(inv), jnp.asarray(sign)


def _log_gamma():
    return jnp.asarray(np.log1p(-(2.0 ** (-5.0 - np.arange(8, dtype=np.float64)))).astype(np.float32))


def _halves(g):
    return g.reshape(N_CHIPS, 2, g.shape[1] // 2, g.shape[2])


def kernel(x, c, positions, ada_w, ada_b, pre_mix_g, post_mix_g, pre_ffn_g, post_ffn_g, w_in, ret_gn_g, w_ret_branch, w_sb_branch, w_out, w_ff1, w_ff2, loss_target, m_ada_w, m_ada_b, m_pre_mix_g, m_post_mix_g, m_pre_ffn_g, m_post_ffn_g, m_w_in, m_ret_gn_g, m_w_ret_branch, m_w_sb_branch, m_w_out, m_w_ff1, m_w_ff2, v_ada_w, v_ada_b, v_pre_mix_g, v_post_mix_g, v_pre_ffn_g, v_post_ffn_g, v_w_in, v_ret_gn_g, v_w_ret_branch, v_w_sb_branch, v_w_out, v_w_ff1, v_w_ff2):
    names = ["w_in", "w_ret", "w_sb", "w_out", "w_ff1", "w_ff2"]
    big = dict(zip(names, [w_in, w_ret_branch, w_sb_branch, w_out, w_ff1, w_ff2]))
    big_m = dict(zip(names, [m_w_in, m_w_ret_branch, m_w_sb_branch, m_w_out, m_w_ff1, m_w_ff2]))
    big_v = dict(zip(names, [v_w_in, v_w_ret_branch, v_w_sb_branch, v_w_out, v_w_ff1, v_w_ff2]))
    rest = names[1:]
    cidx = lax.axis_index("c").astype(jnp.int32).reshape(1)
    kidx = (2 * lax.axis_index("x") + lax.axis_index("y")).astype(jnp.int32).reshape(1)
    x0, target = x[0], loss_target[0]

    c_all, mod4 = _mod_exchange(c, ada_w[0], ada_b.reshape(N_CHIPS, -1))
    vecs = jnp.concatenate([mod4.reshape(6, D_MODEL), pre_mix_g, post_mix_g, pre_ffn_g, post_ffn_g,
                            jnp.zeros((6, D_MODEL), F32)], axis=0)

    buf_in, sem_in, tok_in = _gather_start("gather_in_start", [_cast_bf16(w_in[0], kidx, mod4, "cast_w_in")])
    buf_rest, sem_rest, tok_rest = _gather_start(
        "gather_rest_start", [_cast_bf16(big[nm][0], kidx, tok_in, "cast_" + nm) for nm in rest])
    inv_freq, sign = _rope_constants()
    lg = _log_gamma()
    cos, sin_s = _rope_tables(positions.reshape(SEQ, 1), _tie(tok_rest, inv_freq), sign)
    buf_in, sem_in, tok_in = _gather_pass("gather_in_pass", buf_in, sem_in, cos)
    (w_in4,) = _gather_finish("gather_in_finish", buf_in, sem_in, tok_in)

    h, proj = _ln_proj(x0, vecs, w_in4)
    sb, tot = _sb_fwd(proj)
    buf_rest, sem_rest, tok_rest = _gather_pass("gather_rest_pass", buf_rest, sem_rest, sb)
    o_raw, retg, states = _ret_fwd(proj, cos, sin_s, _tie(tok_rest, ret_gn_g), lg)
    w_ret4, w_sb4, w_out4, w_ff14, w_ff24 = _gather_finish("gather_rest_finish", buf_rest, sem_rest, retg)
    w_ret = w_ret4.reshape(D_MODEL, D_MODEL)
    w_out2 = w_out4.reshape(D_MODEL, D_MODEL)
    w_ff2_2 = w_ff24.reshape(D_FF, D_MODEL)
    mixed, rb, sbp = _mix(retg, sb, proj, w_ret, w_sb4)
    y, h1, h2 = _out_proj(mixed, x0, vecs, w_out2)
    u, act = _ffn_up(h2, w_ff14)
    dout, df, st_a = _ffn_down_loss(act, h1, target, vecs, w_ff2_2)

    du = _ffn_down_bwd(df, u, w_ff2_2)
    grads = {"w_ff2": _weight_grad(act, df, 512, 1024, False, "grad_w_ff2")}
    dh1, dy, st_b = _ffn_up_bwd(du, h1, y, dout, vecs, w_ff14)
    grads["w_ff1"] = _weight_grad(h2, du, 512, 1024, True, "grad_w_ff1")
    drb, dsbp, da_r, da_s, dsb = _mix_bwd(dy, proj, rb, sbp, w_out2, w_sb4)
    grads["w_out"] = _weight_grad(mixed, dy, 256, 1024, False, "grad_w_out")
    dret, dg_r, st_c = _ret_branch_bwd(drb, proj, o_raw, ret_gn_g, w_ret)
    grads["w_ret"] = _weight_grad(retg, drb, 256, 1024, False, "grad_w_ret")
    grads["w_sb"] = _weight_grad(sb, dsbp, 512, 256, True, "grad_w_sb")

    bufs, sems, tok = _pair_send_start("rs_rest_pair_send", [_halves(grads[nm]) for nm in rest])
    dq_r, dk_r, dv_r = _ret_bwd(proj, cos, sin_s, dret, states, _tie(tok, lg))
    mine, theirs = _pair_send_wait("rs_rest_pair_recv", bufs, sems, dq_r)
    pair_sums = [_pair_add(g, r, cidx, "pair_add_" + nm) for g, r, nm in zip(mine, theirs, rest)]
    bufs, sems, tok = _chip_send_start("rs_rest_chip_send", pair_sums)
    dq_s, dk_s, dv_s = _sb_bwd(proj, dsb, tot, dep=tok)
    own, parts = _chip_send_wait("rs_rest_chip_recv", bufs, sems, dq_s)
    sums = [_chip_add(o, p, kidx, cidx, "chip_add_" + nm) for o, p, nm in zip(own, parts, rest)]
    bufs, sems, tok = _pair_swap_start("rs_rest_pair_swap", sums)
    dproj = jnp.concatenate([dq_r, dk_r, dv_r, dg_r, dq_s, dk_s, dv_s, da_r, da_s], axis=1)
    g_in = _weight_grad(h, dproj, 512, 1664, True, "grad_w_in", dep=tok)
    full_rest = _pair_swap_wait("rs_rest_pair_swapped", bufs, sems, g_in)

    bufs, sems, tok = _pair_send_start("rs_in_pair_send", [_halves(g_in)])
    dx, st_d = _in_proj_bwd(dproj, x0, dh1, _tie(tok, vecs), w_in4)
    mine, theirs = _pair_send_wait("rs_in_pair_recv", bufs, sems, dx)
    bufs, sems, tok = _chip_send_start("rs_in_chip_send", [_pair_add(mine[0], theirs[0], cidx, "pair_add_w_in")])
    out = {}
    for nm, g in zip(rest, full_rest):
        w = big[nm][0]
        out[nm] = _adamw(w, big_m[nm][0], big_v[nm][0], g.reshape(w.shape), "adamw_" + nm, dep=tok)

    payload = jnp.concatenate([
        st_d[0:2], st_b[3:4], st_b[0:2], st_a[0:1],
        st_d[2:3], st_b[4:5], st_b[2:3], st_a[1:2],
        st_c[0:1], st_a[2:3]], axis=0)

    def table(b6, g5):
        return jnp.concatenate([b6.reshape(6, D_MODEL)] + g5 + [jnp.zeros((5, D_MODEL), F32)], axis=0)

    wsm = table(ada_b, [pre_mix_g, post_mix_g, pre_ffn_g, post_ffn_g, ret_gn_g])
    msm = table(m_ada_b, [m_pre_mix_g, m_post_mix_g, m_pre_ffn_g, m_post_ffn_g, m_ret_gn_g])
    vsm = table(v_ada_b, [v_pre_mix_g, v_post_mix_g, v_pre_ffn_g, v_post_ffn_g, v_ret_gn_g])
    g_ada, gsm, dsm, mosm, vosm, loss = _small_exchange(
        payload.reshape(N_PAY, 1, D_MODEL), c_all, wsm, msm, vsm)
    ada_out = _adamw(ada_w[0], m_ada_w[0], v_ada_w[0], g_ada, "adamw_ada_w")

    own, parts = _chip_send_wait("rs_in_chip_recv", bufs, sems, ada_out[1])
    bufs, sems, tok = _pair_swap_start(
        "rs_in_pair_swap", [_chip_add(own[0], parts[0], kidx, cidx, "chip_add_w_in")])
    (full_in,) = _pair_swap_wait("rs_in_pair_swapped", bufs, sems, tok)
    out["w_in"] = _adamw(w_in[0], m_w_in[0], v_w_in[0], full_in.reshape(w_in.shape[1:]), "adamw_w_in")

    def unpack(tab):
        return [tab[0:6].reshape(1, 6 * D_MODEL)] + [tab[6 + r:7 + r] for r in range(5)]

    def ordered(which):
        sm = unpack([gsm, dsm, mosm, vosm][which])
        bg = [out[nm][which][None] for nm in names]
        return [ada_out[which][None], sm[0], sm[1], sm[2], sm[3], sm[4], bg[0], sm[5]] + bg[1:]

    return (loss.reshape(()), dx[None], *ordered(0), *ordered(1), *ordered(2), *ordered(3))
```

```python
import functools

import numpy as np
import jax
import jax.numpy as jnp
from jax import lax
from jax.experimental import pallas as pl
from jax.experimental.pallas import tpu as pltpu

SEQ = 2048
D_MODEL = 1024
D_IN = 6656
D_FF = 4096
N_CHIPS = 4
EPS = 1e-6
ROPE_BASE = 10000.0
RET_BLOCK = 256
RET_CHUNK_SHIFT = 6
SB_BLOCK = 256
QK_SCALE = 0.125
N_PAIRS = 4
SB_GROUP = 2

ADAM_LR = 0.001
ADAM_B1 = 0.9
ADAM_B2 = 0.999
ADAM_EPS = 1e-08
ADAM_WD = 0.01
ADAM_STEP = 10

BF = jnp.bfloat16
F32 = jnp.float32
MESH = pl.DeviceIdType.MESH
VMEM_LIMIT = 56 * 1024 * 1024
ANY = pl.BlockSpec(memory_space=pl.ANY)

C_QR, C_KR, C_VR, C_GR, C_QS, C_KS, C_VS, C_AR, C_AS = 0, 512, 1024, 2048, 3072, 3584, 4096, 4608, 5632

V_SH1, V_SC1, V_GT1, V_SH2, V_SC2, V_GT2, V_G1, V_G2, V_G3, V_G4 = range(10)
P_DSH1, P_DSC1, P_DGT1, P_DSH2, P_DSC2, P_DGT2, P_DG1, P_DG2, P_DG3, P_DG4, P_DGN, P_LOSS = range(12)
N_PAY = 12


def _cp(sem=None, **kw):
    if sem is not None:
        kw["dimension_semantics"] = sem
    return pltpu.CompilerParams(vmem_limit_bytes=VMEM_LIMIT, **kw)


def _dot(a, b):
    return jnp.dot(a, b, preferred_element_type=F32)


def _dot_nt(a, b):
    return lax.dot_general(a, b, (((1,), (1,)), ((), ())), preferred_element_type=F32)


def _dot_tn(a, b):
    return lax.dot_general(a, b, (((0,), (0,)), ((), ())), preferred_element_type=F32)


def _row(ref, i):
    return ref[i:i + 1, :]


def _rms(v):
    return lax.rsqrt(jnp.mean(v * v, axis=1, keepdims=True) + EPS)


def _colsum(v):
    return jnp.sum(v, axis=0, keepdims=True)


def _rowmean(v):
    return jnp.mean(v, axis=1, keepdims=True)


def _sigmoid(v):
    return 1.0 / (1.0 + jnp.exp(-v))


def _cast_bf16(w, kidx, dep, name):
    rows, cols = w.shape
    tr = min(rows, 512)

    def body(k_ref, w_ref, dep_ref, o_ref):
        o_ref[...] = w_ref[...].astype(BF)

    return pl.pallas_call(
        body, name=name,
        grid_spec=pltpu.PrefetchScalarGridSpec(
            num_scalar_prefetch=1, grid=(rows // tr,),
            in_specs=[pl.BlockSpec((tr, cols), lambda i, k_ref: (i, 0)), ANY],
            out_specs=pl.BlockSpec((None, tr, cols), lambda i, k_ref: (k_ref[0], i, 0))),
        out_shape=jax.ShapeDtypeStruct((N_CHIPS, rows, cols), BF),
        compiler_params=_cp(("parallel",)),
    )(kidx, w, dep)


def _adamw_math(w, g, m, v):
    m = ADAM_B1 * m + (1.0 - ADAM_B1) * g
    v = ADAM_B2 * v + (1.0 - ADAM_B2) * (g * g)
    m_hat = m / (1.0 - ADAM_B1 ** ADAM_STEP)
    v_hat = v / (1.0 - ADAM_B2 ** ADAM_STEP)
    delta = -ADAM_LR * (m_hat / (jnp.sqrt(v_hat) + ADAM_EPS) + ADAM_WD * w)
    return delta, m, v


def _adamw(w, m, v, g, name, dep=None):
    rows, cols = w.shape
    tr = min(rows, 256)

    def body(w_ref, m_ref, v_ref, g_ref, go_ref, d_ref, mo_ref, vo_ref):
        gg = g_ref[...]
        d, mm, vv = _adamw_math(w_ref[...], gg, m_ref[...], v_ref[...])
        go_ref[...] = gg
        d_ref[...] = d
        mo_ref[...] = mm
        vo_ref[...] = vv

    spec = pl.BlockSpec((tr, cols), lambda i: (i, 0))
    shp = jax.ShapeDtypeStruct((rows, cols), F32)
    body, in_specs, args = _add_dep(body, [spec] * 4, [w, m, v, g], dep)
    return pl.pallas_call(
        body, name=name, grid=(rows // tr,),
        in_specs=in_specs, out_specs=[spec] * 4, out_shape=[shp] * 4,
        compiler_params=_cp(("parallel",)),
    )(*args)


def _place():
    x, y, c = lax.axis_index("x"), lax.axis_index("y"), lax.axis_index("c")
    return x, y, c


HBM = pl.BlockSpec(memory_space=pltpu.HBM)
SEM = pl.BlockSpec(memory_space=pltpu.SEMAPHORE)
EFFECT = pltpu.SideEffectType.DATAFLOW_SIDE_EFFECTING


def _tie(token, small):
    return small + token[0, 0]


def _add_dep(body, in_specs, args, dep):
    if dep is None:
        return body, list(in_specs), list(args)
    n = len(args)

    def wrapped(*refs):
        body(*refs[:n], *refs[n + 1:])

    return wrapped, list(in_specs) + [ANY], list(args) + [dep]


def _split_call(name, bufs, run, old=None, after=None, new=0):
    nb = len(bufs)
    n_in = nb + (3 if old is not None else 0)

    def body(*refs):
        old_sems = (refs[nb], refs[nb + 1]) if old is not None else None
        new_sems = (refs[n_in], refs[n_in + 1]) if new else None
        run(refs[:nb], old_sems, new_sems)
        if new:
            refs[-1][...] = jnp.zeros_like(refs[-1])

    in_specs = [HBM] * nb + ([SEM, SEM, ANY] if old is not None else [])
    out_shape = [pltpu.SemaphoreType.DMA((new,))] * 2 if new else []
    out_specs = [SEM, SEM] if new else []
    out_shape += [pltpu.HBM(b.shape, b.dtype) for b in bufs]
    out_specs += [HBM] * nb
    if new:
        out_shape.append(jax.ShapeDtypeStruct((8, 128), F32))
        out_specs.append(pl.BlockSpec(memory_space=pltpu.VMEM))
    first = 2 if new else 0
    args = [pltpu.with_memory_space_constraint(b, pltpu.HBM) for b in bufs]
    if old is not None:
        args += [old[0], old[1], after]
    outs = pl.pallas_call(
        body, name=name, in_specs=tuple(in_specs), out_specs=tuple(out_specs), out_shape=tuple(out_shape),
        input_output_aliases={i: i + first for i in range(nb)},
        compiler_params=pltpu.CompilerParams(has_side_effects=EFFECT),
    )(*args)
    thru = list(outs[first:first + nb])
    if new:
        return thru, (outs[0], outs[1]), outs[-1]
    return thru, None, None


def _remote(part_src, part_dst, sems, i, to):
    return pltpu.make_async_remote_copy(src_ref=part_src, dst_ref=part_dst, send_sem=sems[0].at[i],
                                        recv_sem=sems[1].at[i], device_id=to, device_id_type=MESH)


def _other_chips(x, y):
    return [(1 - x, y), (x, 1 - y), (1 - x, 1 - y)]


def _gather_start(name, bufs):
    def run(refs, old, new):
        x, y, c = _place()
        k = 2 * x + y
        for w, ref in enumerate(refs):
            rh = bufs[w].shape[1] // 2
            part = ref.at[k, pl.ds(c * rh, rh)]
            for j, (cx, cy) in enumerate(_other_chips(x, y)):
                _remote(part, part, new, 3 * w + j, (cx, cy, c)).start()

    return _split_call(name, bufs, run, new=3 * len(bufs))


def _gather_pass(name, bufs, sems, after):
    def run(refs, old, new):
        x, y, c = _place()
        k = 2 * x + y
        sib = (x, y, 1 - c)
        for w, ref in enumerate(refs):
            rh = bufs[w].shape[1] // 2
            for j, (cx, cy) in enumerate(_other_chips(x, y)):
                land = ref.at[2 * cx + cy, pl.ds(c * rh, rh)]
                _remote(land, land, old, 3 * w + j, (cx, cy, c)).wait_recv()
                _remote(land, land, new, 3 * w + j, sib).start()
        for w, ref in enumerate(refs):
            rh = bufs[w].shape[1] // 2
            part = ref.at[k, pl.ds(c * rh, rh)]
            for j, (cx, cy) in enumerate(_other_chips(x, y)):
                _remote(part, part, old, 3 * w + j, (cx, cy, c)).wait_send()

    return _split_call(name, bufs, run, old=sems, after=after, new=3 * len(bufs))


def _gather_finish(name, bufs, sems, after):
    def run(refs, old, new):
        x, y, c = _place()
        sib = (x, y, 1 - c)
        for w, ref in enumerate(refs):
            rh = bufs[w].shape[1] // 2
            for j, (cx, cy) in enumerate(_other_chips(x, y)):
                sent = ref.at[2 * cx + cy, pl.ds(c * rh, rh)]
                _remote(sent, sent, old, 3 * w + j, sib).wait_send()
                land = ref.at[2 * cx + cy, pl.ds((1 - c) * rh, rh)]
                _remote(land, land, old, 3 * w + j, sib).wait_recv()

    return _split_call(name, bufs, run, old=sems, after=after)[0]


def _pair_send_start(name, grads):
    n = len(grads)
    lands = [lax.empty((N_CHIPS,) + g.shape[2:], F32) for g in grads]

    def run(refs, old, new):
        x, y, c = _place()
        for w in range(n):
            _remote(refs[w].at[:, 1 - c], refs[n + w], new, w, (x, y, 1 - c)).start()

    return _split_call(name, list(grads) + lands, run, new=n)


def _pair_send_wait(name, bufs, sems, after):
    n = len(bufs) // 2

    def run(refs, old, new):
        x, y, c = _place()
        for w in range(n):
            cp = _remote(refs[w].at[:, 1 - c], refs[n + w], old, w, (x, y, 1 - c))
            cp.wait_send()
            cp.wait_recv()

    thru = _split_call(name, bufs, run, old=sems, after=after)[0]
    return thru[:n], thru[n:]


def _pair_add(g, recv, cidx, name):
    _, _, rh, cols = g.shape
    tr = min(rh, 256)

    def body(c_ref, g_ref, r_ref, o_ref):
        o_ref[...] = (g_ref[...] + r_ref[...]).astype(BF)

    return pl.pallas_call(
        body, name=name,
        grid_spec=pltpu.PrefetchScalarGridSpec(
            num_scalar_prefetch=1, grid=(rh // tr,),
            in_specs=[pl.BlockSpec((N_CHIPS, None, tr, cols), lambda i, c_ref: (0, c_ref[0], i, 0)),
                      pl.BlockSpec((N_CHIPS, tr, cols), lambda i, c_ref: (0, i, 0))],
            out_specs=pl.BlockSpec((N_CHIPS, tr, cols), lambda i, c_ref: (0, i, 0))),
        out_shape=jax.ShapeDtypeStruct((N_CHIPS, rh, cols), BF),
        compiler_params=_cp(("parallel",)),
    )(cidx, g, recv)


def _chip_send_start(name, sums):
    n = len(sums)
    lands = [lax.empty((3,) + s.shape[1:], BF) for s in sums]

    def run(refs, old, new):
        x, y, c = _place()
        for w in range(n):
            for j, (cx, cy) in enumerate(_other_chips(x, y)):
                _remote(refs[w].at[2 * cx + cy], refs[n + w].at[j], new, 3 * w + j, (cx, cy, c)).start()

    return _split_call(name, list(sums) + lands, run, new=3 * n)


def _chip_send_wait(name, bufs, sems, after):
    n = len(bufs) // 2

    def run(refs, old, new):
        x, y, c = _place()
        for w in range(n):
            for j, (cx, cy) in enumerate(_other_chips(x, y)):
                cp = _remote(refs[w].at[2 * cx + cy], refs[n + w].at[j], old, 3 * w + j, (cx, cy, c))
                cp.wait_send()
                cp.wait_recv()

    thru = _split_call(name, bufs, run, old=sems, after=after)[0]
    return thru[:n], thru[n:]


def _chip_add(own, parts, kidx, cidx, name):
    _, rh, cols = parts.shape
    tr = min(rh, 512)

    def body(k_ref, c_ref, own_ref, p_ref, o_ref):
        acc = own_ref[...].astype(F32)
        for s in range(3):
            acc = acc + p_ref[s].astype(F32)
        o_ref[...] = acc

    return pl.pallas_call(
        body, name=name,
        grid_spec=pltpu.PrefetchScalarGridSpec(
            num_scalar_prefetch=2, grid=(rh // tr,),
            in_specs=[pl.BlockSpec((None, tr, cols), lambda i, k_ref, c_ref: (k_ref[0], i, 0)),
                      pl.BlockSpec((3, tr, cols), lambda i, k_ref, c_ref: (0, i, 0))],
            out_specs=pl.BlockSpec((None, tr, cols), lambda i, k_ref, c_ref: (c_ref[0], i, 0))),
        out_shape=jax.ShapeDtypeStruct((2, rh, cols), F32),
        compiler_params=_cp(("parallel",)),
    )(kidx, cidx, own, parts)


def _pair_swap_start(name, bufs):
    def run(refs, old, new):
        x, y, c = _place()
        for w, ref in enumerate(refs):
            _remote(ref.at[c], ref.at[c], new, w, (x, y, 1 - c)).start()

    return _split_call(name, bufs, run, new=len(bufs))


def _pair_swap_wait(name, bufs, sems, after):
    def run(refs, old, new):
        x, y, c = _place()
        for w, ref in enumerate(refs):
            _remote(ref.at[c], ref.at[c], old, w, (x, y, 1 - c)).wait_send()
            _remote(ref.at[1 - c], ref.at[1 - c], old, w, (x, y, 1 - c)).wait_recv()

    return _split_call(name, bufs, run, old=sems, after=after)[0]


def _peers(x, y, c):
    out = []
    for code in range(1, 8):
        fx, fy, fc = (code >> 2) & 1, (code >> 1) & 1, code & 1
        px = 1 - x if fx else x
        py = 1 - y if fy else y
        pc = 1 - c if fc else c
        out.append((code, (px, py, pc)))
    return out


def _mod_exchange(c_row, ada_w, ada_b4):
    ncol = ada_w.shape[1]

    def body(c_ref, w_ref, b_ref, call_ref, mod_ref, part_ref, send_sems, recv_sems):
        x, y, c = _place()
        k = 2 * x + y
        me = 4 * x + 2 * y + c
        call_ref[pl.ds(me, 1), :] = c_ref[...]
        sends = []
        for code, peer in _peers(x, y, c):
            cp = pltpu.make_async_remote_copy(
                src_ref=c_ref, dst_ref=call_ref.at[pl.ds(me, 1), :],
                send_sem=send_sems.at[code], recv_sem=recv_sems.at[code],
                device_id=peer, device_id_type=MESH)
            cp.start()
            sends.append(cp)
        for code, (px, py, pc) in _peers(x, y, c):
            land = call_ref.at[pl.ds(4 * px + 2 * py + pc, 1), :]
            pltpu.make_async_remote_copy(
                src_ref=land, dst_ref=land, send_sem=send_sems.at[code], recv_sem=recv_sems.at[code],
                device_id=(px, py, pc), device_id_type=MESH).wait_recv()
        call = call_ref[...]
        act = call * _sigmoid(call)
        part = jnp.dot(act, w_ref[...], preferred_element_type=F32,
                       precision=lax.Precision.HIGHEST) + b_ref[pl.ds(k, 1), :]
        part_ref[...] = part
        mod_ref[pl.ds(k, 1), :] = part_ref[pl.ds(me, 1), :]
        chips = [(8 + j, peer) for j, (code, peer) in enumerate(_peers(x, y, c)) if code in (2, 4, 6)]
        for slot, (px, py, pc) in chips:
            cp = pltpu.make_async_remote_copy(
                src_ref=part_ref.at[pl.ds(4 * px + 2 * py + pc, 1), :], dst_ref=mod_ref.at[pl.ds(k, 1), :],
                send_sem=send_sems.at[slot], recv_sem=recv_sems.at[slot],
                device_id=(px, py, pc), device_id_type=MESH)
            cp.start()
            sends.append(cp)
        for slot, (px, py, pc) in chips:
            land = mod_ref.at[pl.ds(2 * px + py, 1), :]
            pltpu.make_async_remote_copy(
                src_ref=land, dst_ref=land, send_sem=send_sems.at[slot], recv_sem=recv_sems.at[slot],
                device_id=(px, py, pc), device_id_type=MESH).wait_recv()
        for cp in sends:
            cp.wait_send()

    vm = pl.BlockSpec(memory_space=pltpu.VMEM)
    return pl.pallas_call(
        body, name="mod_exchange",
        in_specs=[vm, vm, vm], out_specs=[vm, vm],
        out_shape=[jax.ShapeDtypeStruct((8, D_MODEL), F32), jax.ShapeDtypeStruct((N_CHIPS, ncol), F32)],
        scratch_shapes=[pltpu.VMEM((8, ncol), F32), pltpu.SemaphoreType.DMA((16,)),
                        pltpu.SemaphoreType.DMA((16,))],
        compiler_params=_cp(),
    )(c_row, ada_w, ada_b4)


def _small_exchange(payload, c_all, wsm, msm, vsm):
    ncol = 6 * D_MODEL // N_CHIPS

    def body(p_ref, call_ref, w_ref, m_ref, v_ref, gw_ref, g_ref, d_ref, mo_ref, vo_ref, loss_ref,
             all_ref, dm_ref, send_sems, recv_sems):
        x, y, c = _place()
        k = 2 * x + y
        me = 4 * x + 2 * y + c
        all_ref[:, pl.ds(me, 1), :] = p_ref[...]
        sends = []
        for code, peer in _peers(x, y, c):
            cp = pltpu.make_async_remote_copy(
                src_ref=p_ref, dst_ref=all_ref.at[:, pl.ds(me, 1), :],
                send_sem=send_sems.at[code], recv_sem=recv_sems.at[code],
                device_id=peer, device_id_type=MESH)
            cp.start()
            sends.append(cp)
        for code, (px, py, pc) in _peers(x, y, c):
            land = all_ref.at[:, pl.ds(4 * px + 2 * py + pc, 1), :]
            pltpu.make_async_remote_copy(
                src_ref=land, dst_ref=land, send_sem=send_sems.at[code], recv_sem=recv_sems.at[code],
                device_id=(px, py, pc), device_id_type=MESH).wait_recv()
        for cp in sends:
            cp.wait_send()
        tot = [_colsum(all_ref[r]) for r in range(N_PAY)]
        loss_ref[...] = jnp.sum(tot[P_LOSS], axis=1, keepdims=True)
        g_ref[...] = jnp.zeros_like(g_ref)
        for r in range(P_LOSS):
            g_ref[r:r + 1, :] = tot[r]
        g = g_ref[...]
        d, mm, vv = _adamw_math(w_ref[...], g, m_ref[...], v_ref[...])
        d_ref[...] = d
        mo_ref[...] = mm
        vo_ref[...] = vv
        half = D_MODEL // 2
        for kk in range(N_CHIPS):
            @pl.when(k == kk)
            def _():
                r0 = 3 * (kk // 2)
                if kk % 2 == 0:
                    dm_ref[:, :D_MODEL] = all_ref[r0]
                    dm_ref[:, D_MODEL:] = all_ref[r0 + 1][:, :half]
                else:
                    dm_ref[:, :half] = all_ref[r0 + 1][:, half:]
                    dm_ref[:, half:] = all_ref[r0 + 2]
        call = call_ref[...]
        act = call * _sigmoid(call)
        gw_ref[...] = lax.dot_general(act, dm_ref[...], (((0,), (0,)), ((), ())),
                                      preferred_element_type=F32, precision=lax.Precision.HIGHEST)

    vm = pl.BlockSpec(memory_space=pltpu.VMEM)
    small = jax.ShapeDtypeStruct((16, D_MODEL), F32)
    return pl.pallas_call(
        body, name="small_exchange",
        in_specs=[vm] * 5, out_specs=[vm] * 6,
        out_shape=[jax.ShapeDtypeStruct((D_MODEL, ncol), F32), small, small, small, small,
                   jax.ShapeDtypeStruct((1, 1), F32)],
        scratch_shapes=[pltpu.VMEM((N_PAY, 8, D_MODEL), F32), pltpu.VMEM((8, ncol), F32),
                        pltpu.SemaphoreType.DMA((8,)), pltpu.SemaphoreType.DMA((8,))],
        compiler_params=_cp(),
    )(payload, c_all, wsm, msm, vsm)


def _rope_tables(pos_col, inv_freq, sign):
    def body(p_ref, f_ref, s_ref, cos_ref, sin_ref):
        ang = p_ref[...].astype(F32) * f_ref[...]
        cos_ref[...] = jnp.cos(ang)
        sin_ref[...] = jnp.sin(ang) * s_ref[...]

    tr = 512
    shp = jax.ShapeDtypeStruct((SEQ, 128), F32)
    return pl.pallas_call(
        body, name="rope_tables", grid=(SEQ // tr,),
        in_specs=[pl.BlockSpec((tr, 1), lambda i: (i, 0)), pl.BlockSpec((1, 128), lambda i: (0, 0)),
                  pl.BlockSpec((1, 128), lambda i: (0, 0))],
        out_specs=[pl.BlockSpec((tr, 128), lambda i: (i, 0))] * 2, out_shape=[shp, shp],
        compiler_params=_cp(("parallel",)),
    )(pos_col, inv_freq, sign)


def _resident(shape):
    nd = len(shape)
    return pl.BlockSpec(shape, lambda *_: (0,) * nd, pipeline_mode=pl.Buffered(1))


def _ln_proj(x, vecs, w_in4):
    tm = 256
    wc = w_in4.shape[2]

    def body(x_ref, vec_ref, w_ref, h_ref, proj_ref):
        xx = x_ref[...]
        g = _row(vec_ref, V_G1) * (1.0 + _row(vec_ref, V_SC1))
        h = (xx * _rms(xx) * g + _row(vec_ref, V_SH1)).astype(BF)
        h_ref[...] = h
        for j in range(N_CHIPS):
            proj_ref[:, j * wc:(j + 1) * wc] = _dot(h, w_ref[j]).astype(BF)

    return pl.pallas_call(
        body, name="ln_proj", grid=(SEQ // tm,),
        in_specs=[pl.BlockSpec((tm, D_MODEL), lambda i: (i, 0)), _resident((16, D_MODEL)),
                  _resident(w_in4.shape)],
        out_specs=[pl.BlockSpec((tm, D_MODEL), lambda i: (i, 0)), pl.BlockSpec((tm, D_IN), lambda i: (i, 0))],
        out_shape=[jax.ShapeDtypeStruct((SEQ, D_MODEL), BF), jax.ShapeDtypeStruct((SEQ, D_IN), BF)],
        compiler_params=_cp(("parallel",)),
    )(x, vecs, w_in4)


def _lane_first(shape):
    lane = lax.broadcasted_iota(jnp.int32, shape, 1)
    return (lane & 32) == 0


def _rot(v, cos, sin_s):
    partner = jnp.where(_lane_first(v.shape), pltpu.roll(v, 96, 1), pltpu.roll(v, 32, 1))
    return v * cos + partner * sin_s


def _rot_t(dv, cos, sin_s):
    t = dv * sin_s
    partner = jnp.where(_lane_first(dv.shape), pltpu.roll(t, 96, 1), pltpu.roll(t, 32, 1))
    return dv * cos + partner


def _ret_masks(lg):
    t = RET_BLOCK
    ii = lax.broadcasted_iota(jnp.int32, (t, t), 0)
    jj = lax.broadcasted_iota(jnp.int32, (t, t), 1)
    dist = jnp.abs(ii - jj).astype(F32)
    future = (jj >> RET_CHUNK_SHIFT) > (ii >> RET_CHUNK_SHIFT)
    mask = jnp.where(future, 0.0, jnp.exp(lg * dist))
    ti = lax.broadcasted_iota(jnp.int32, (t, 1), 0).astype(F32)
    from_start = jnp.exp(lg * (ti + 1.0))
    to_end = jnp.exp(lg * (t - 1.0 - ti))
    whole = jnp.exp(jnp.full((1, 128), lg * t, F32))
    return mask, from_start, to_end, whole


def _head_lanes(shape, hh):
    lane = lax.broadcasted_iota(jnp.int32, shape, 1)
    return (lane >> 6) == hh


def _ret_specs():
    t = RET_BLOCK
    return dict(
        q=lambda f: pl.BlockSpec((t, 512), lambda n: (f(n), C_QR // 512)),
        k=lambda f: pl.BlockSpec((t, 512), lambda n: (f(n), C_KR // 512)),
        v=lambda f: pl.BlockSpec((t, D_MODEL), lambda n: (f(n), C_VR // D_MODEL)),
        g=lambda f: pl.BlockSpec((t, D_MODEL), lambda n: (f(n), C_GR // D_MODEL)),
        tab=lambda f: pl.BlockSpec((t, 128), lambda n: (f(n), 0)),
        wide=lambda f: pl.BlockSpec((t, D_MODEL), lambda n: (f(n), 0)),
        state=lambda f: pl.BlockSpec((N_PAIRS, None, 2, 128, 128), lambda n: (0, f(n), 0, 0, 0)),
    )


def _ret_fwd(proj, cos, sin_s, gn_g, log_gamma):
    t = RET_BLOCK
    nb = SEQ // t

    def body(lg_ref, q_ref, k_ref, v_ref, g_ref, cos_ref, sin_ref, gn_ref, o_ref, retg_ref, st_ref, state):
        @pl.when(pl.program_id(0) == 0)
        def _():
            state[...] = jnp.zeros_like(state)

        cos, sn = cos_ref[...], sin_ref[...]
        for p in range(N_PAIRS):
            q = _rot(q_ref[:, 128 * p:128 * (p + 1)].astype(F32), cos, sn)
            k = _rot(k_ref[:, 128 * p:128 * (p + 1)].astype(F32), cos, sn) * QK_SCALE
            for hh in range(2):
                cols = slice(256 * p + 128 * hh, 256 * p + 128 * (hh + 1))
                lg = lg_ref[2 * p + hh]
                mask, from_start, to_end, whole = _ret_masks(lg)
                lanes = _head_lanes(q.shape, hh)
                qm = jnp.where(lanes, q, 0.0)
                km = jnp.where(lanes, k, 0.0)
                vh = v_ref[:, cols]
                sc = _dot_nt(qm.astype(BF), km.astype(BF)) * mask
                st = state[p, hh]
                st_ref[p, hh] = st
                o = _dot(sc.astype(BF), vh) + _dot((qm * from_start).astype(BF), st.astype(BF))
                state[p, hh] = whole * st + _dot_tn((km * to_end).astype(BF), vh)
                d = o - _rowmean(o)
                nh = d * lax.rsqrt(_rowmean(d * d) + EPS)
                gr = g_ref[:, cols].astype(F32)
                o_ref[:, cols] = o
                retg_ref[:, cols] = (gr * _sigmoid(gr) * nh * gn_ref[:, cols]).astype(BF)

    sp = _ret_specs()
    ident = lambda n: n
    return pl.pallas_call(
        body, name="ret_fwd", grid=(nb,),
        in_specs=[pl.BlockSpec(memory_space=pltpu.SMEM), sp["q"](ident), sp["k"](ident), sp["v"](ident),
                  sp["g"](ident), sp["tab"](ident), sp["tab"](ident), _resident((1, D_MODEL))],
        out_specs=[sp["wide"](ident), sp["wide"](ident), sp["state"](ident)],
        out_shape=[jax.ShapeDtypeStruct((SEQ, D_MODEL), F32), jax.ShapeDtypeStruct((SEQ, D_MODEL), BF),
                   jax.ShapeDtypeStruct((N_PAIRS, nb, 2, 128, 128), F32)],
        scratch_shapes=[pltpu.VMEM((N_PAIRS, 2, 128, 128), F32)],
        compiler_params=_cp(("arbitrary",)),
    )(log_gamma, proj, proj, proj, proj, cos, sin_s, gn_g)


def _ret_bwd(proj, cos, sin_s, dret, states, log_gamma):
    t = RET_BLOCK
    nb = SEQ // t

    def body(lg_ref, q_ref, k_ref, v_ref, cos_ref, sin_ref, do_ref, st_ref, dqkv_ref, dstate):
        @pl.when(pl.program_id(0) == 0)
        def _():
            dstate[...] = jnp.zeros_like(dstate)

        cos, sn = cos_ref[...], sin_ref[...]
        for p in range(N_PAIRS):
            q = _rot(q_ref[:, 128 * p:128 * (p + 1)].astype(F32), cos, sn)
            k = _rot(k_ref[:, 128 * p:128 * (p + 1)].astype(F32), cos, sn) * QK_SCALE
            dq_rot = jnp.zeros(q.shape, F32)
            dk_rot = jnp.zeros(q.shape, F32)
            for hh in range(2):
                cols = slice(256 * p + 128 * hh, 256 * p + 128 * (hh + 1))
                lg = lg_ref[2 * p + hh]
                mask, from_start, to_end, whole = _ret_masks(lg)
                lanes = _head_lanes(q.shape, hh)
                qm = jnp.where(lanes, q, 0.0)
                km = jnp.where(lanes, k, 0.0)
                qb, kb = qm.astype(BF), km.astype(BF)
                vh = v_ref[:, cols]
                do = do_ref[:, cols]
                sc = (_dot_nt(qb, kb) * mask).astype(BF)
                st = st_ref[p, hh].astype(BF)
                dst = dstate[p, hh]
                dstb = dst.astype(BF)
                k_end = (km * to_end).astype(BF)
                q_start = (qm * from_start).astype(BF)
                dqkv_ref[:, C_VR + 256 * p + 128 * hh:C_VR + 256 * p + 128 * (hh + 1)] = (
                    _dot_tn(sc, do) + _dot(k_end, dstb)).astype(BF)
                dsc = (_dot_nt(do, vh) * mask).astype(BF)
                dq_h = _dot(dsc, kb) + _dot_nt(do, st) * from_start
                dq_rot = dq_rot + jnp.where(lanes, dq_h, 0.0)
                dk_rot = dk_rot + _dot_tn(dsc, qb) + _dot_nt(vh, dstb) * to_end
                dstate[p, hh] = whole * dst + _dot_tn(q_start, do)
            dqkv_ref[:, C_QR + 128 * p:C_QR + 128 * (p + 1)] = _rot_t(dq_rot, cos, sn).astype(BF)
            dqkv_ref[:, C_KR + 128 * p:C_KR + 128 * (p + 1)] = _rot_t(dk_rot * QK_SCALE, cos, sn).astype(BF)

    sp = _ret_specs()
    rev = lambda n: nb - 1 - n
    return pl.pallas_call(
        body, name="ret_bwd", grid=(nb,),
        in_specs=[pl.BlockSpec(memory_space=pltpu.SMEM), sp["q"](rev), sp["k"](rev), sp["v"](rev),
                  sp["tab"](rev), sp["tab"](rev), sp["wide"](rev), sp["state"](rev)],
        out_specs=pl.BlockSpec((t, C_GR), lambda n: (rev(n), 0)),
        out_shape=jax.ShapeDtypeStruct((SEQ, C_GR), BF),
        scratch_shapes=[pltpu.VMEM((N_PAIRS, 2, 128, 128), F32)],
        compiler_params=_cp(("arbitrary",)),
    )(log_gamma, proj, proj, proj, cos, sin_s, dret, states)


def _stack_heads(v):
    return jnp.concatenate([jnp.where(_head_lanes(v.shape, hh), v, jnp.zeros_like(v)) for hh in range(2)], axis=0)


def _unstack_heads(v):
    t = v.shape[0] // 2
    return jnp.where(_head_lanes((t, v.shape[1]), 0), v[:t], v[t:])


def _sb_masks(t, heads):
    rr = lax.broadcasted_iota(jnp.int32, (t, t), 0)
    cc = lax.broadcasted_iota(jnp.int32, (t, t), 1)
    r2 = lax.broadcasted_iota(jnp.int32, (heads * t, t), 0) & (t - 1)
    c2 = lax.broadcasted_iota(jnp.int32, (heads * t, t), 1)
    return rr, cc, c2 < r2


def _split_dot2(v, tri):
    return _dot(v.astype(BF), tri)


def _log_one_minus_beta(z):
    return -(jnp.maximum(z, 0.0) + jnp.log(1.0 + jnp.exp(-jnp.abs(z))))


def _sb_fwd(proj):
    t, g = SB_BLOCK, SB_GROUP
    nq = SEQ // t
    rows = 2 * g * t

    def body(q_ref, k_ref, v_ref, o_ref, tot_ref, kt_ref):
        i = pl.program_id(1)

        @pl.when(i == 0)
        def _():
            for p in range(g):
                for jj in range(nq):
                    kt_ref[p, jj] = k_ref[jj * t:(jj + 1) * t, 128 * p:128 * (p + 1)].T

        q2 = [_stack_heads((q_ref[:, 128 * p:128 * (p + 1)].astype(F32) * QK_SCALE).astype(BF)) for p in range(g)]
        rr, cc, valid = _sb_masks(t, 2 * g)
        later = (rr > cc).astype(BF)

        def tile(j, carry, diagonal):
            acc, run = carry
            z = jnp.concatenate([_dot(q2[p], kt_ref[p, j]) for p in range(g)], axis=0)
            lm = _log_one_minus_beta(z)
            if diagonal:
                lm = jnp.where(valid, lm, 0.0)
            after = _split_dot2(lm, later)
            a = jnp.exp(z + lm + after + run)
            if diagonal:
                a = jnp.where(valid, a, 0.0)
            ab = a.astype(BF)
            keys = pl.ds(pl.multiple_of(j * t, t), t)
            av = jnp.concatenate([_dot(ab[2 * t * p:2 * t * (p + 1)], v_ref[keys, 128 * p:128 * (p + 1)])
                                  for p in range(g)], axis=0)
            return acc + av, run + after[:, 0:1] + lm[:, 0:1]

        carry = tile(i, (jnp.zeros((rows, 128), F32), jnp.zeros((rows, 1), F32)), True)
        acc, run = lax.fori_loop(0, i, lambda s, cr: tile(i - 1 - s, cr, False), carry)
        run = jnp.broadcast_to(run, (rows, 128))
        for p in range(g):
            o_ref[:, 128 * p:128 * (p + 1)] = _unstack_heads(acc[2 * t * p:2 * t * (p + 1)]).astype(BF)
            tot_ref[:, 128 * p:128 * (p + 1)] = _unstack_heads(run[2 * t * p:2 * t * (p + 1)])

    w = 128 * g
    return pl.pallas_call(
        body, name="sb_fwd", grid=(N_PAIRS // g, nq),
        in_specs=[pl.BlockSpec((t, w), lambda p, i: (i, C_QS // w + p)),
                  pl.BlockSpec((SEQ, w), lambda p, i: (0, C_KS // w + p)),
                  pl.BlockSpec((SEQ, w), lambda p, i: (0, C_VS // w + p))],
        out_specs=[pl.BlockSpec((t, w), lambda p, i: (i, p))] * 2,
        out_shape=[jax.ShapeDtypeStruct((SEQ, 512), BF), jax.ShapeDtypeStruct((SEQ, 512), F32)],
        scratch_shapes=[pltpu.VMEM((g, nq, 128, t), BF)],
        compiler_params=_cp(("parallel", "arbitrary")),
    )(proj, proj, proj)


def _sb_bwd(proj, dsb, tot, dep=None):
    t, g = SB_BLOCK, SB_GROUP
    nq = SEQ // t
    rows = 2 * g * t

    def body(q_ref, k_ref, v_ref, do_ref, tot_ref, dq_ref, dk_ref, dv_ref, kt_ref, vt_ref, dkt_acc, dvt_acc):
        i = pl.program_id(1)

        @pl.when(i == 0)
        def _():
            dkt_acc[...] = jnp.zeros_like(dkt_acc)
            dvt_acc[...] = jnp.zeros_like(dvt_acc)
            for p in range(g):
                for jj in range(nq):
                    kt_ref[p, jj] = k_ref[jj * t:(jj + 1) * t, 128 * p:128 * (p + 1)].T
                    vt_ref[p, jj] = v_ref[jj * t:(jj + 1) * t, 128 * p:128 * (p + 1)].T

        q2 = [_stack_heads((q_ref[:, 128 * p:128 * (p + 1)].astype(F32) * QK_SCALE).astype(BF)) for p in range(g)]
        do2 = [_stack_heads(do_ref[:, 128 * p:128 * (p + 1)]) for p in range(g)]
        q2t = [v.T for v in q2]
        do2t = [v.T for v in do2]
        tots = tot_ref[...]
        total = jnp.concatenate([tots[:, 64 * h:64 * h + 1] for h in range(2 * g)], axis=0)
        rr, cc, valid = _sb_masks(t, 2 * g)
        upto = (rr <= cc).astype(BF)
        before = (rr < cc).astype(BF)

        def part(v, p):
            return v[2 * t * p:2 * t * (p + 1)]

        def tile(j, carry, diagonal):
            dq, run_l, run_g = carry
            z = jnp.concatenate([_dot(q2[p], kt_ref[p, j]) for p in range(g)], axis=0)
            lm = _log_one_minus_beta(z)
            if diagonal:
                lm = jnp.where(valid, lm, 0.0)
            incl = _split_dot2(lm, upto)
            a = jnp.exp(z + lm + (total - (incl + run_l)))
            if diagonal:
                a = jnp.where(valid, a, 0.0)
            gg = a * jnp.concatenate([_dot(do2[p], vt_ref[p, j]) for p in range(g)], axis=0)
            excl = _split_dot2(gg, before)
            dz = gg * jnp.exp(lm) - (excl + run_g) * jnp.exp(z + lm)
            if diagonal:
                dz = jnp.where(valid, dz, 0.0)
            dzb = dz.astype(BF)
            ab = a.astype(BF)
            keys = pl.ds(pl.multiple_of(j * t, t), t)
            for p in range(g):
                dkt_acc[p, j] += _dot(q2t[p], part(dzb, p))
                dvt_acc[p, j] += _dot(do2t[p], part(ab, p))
            dq_t = jnp.concatenate([_dot(part(dzb, p), k_ref[keys, 128 * p:128 * (p + 1)]) for p in range(g)], axis=0)
            return (dq + dq_t, run_l + incl[:, t - 1:t], run_g + excl[:, t - 1:t] + gg[:, t - 1:t])

        zero = jnp.zeros((rows, 1), F32)
        carry = lax.fori_loop(0, i, lambda j, cr: tile(j, cr, False), (jnp.zeros((rows, 128), F32), zero, zero))
        dq = tile(i, carry, True)[0]
        for p in range(g):
            dq_ref[:, 128 * p:128 * (p + 1)] = (_unstack_heads(part(dq, p)) * QK_SCALE).astype(BF)

        @pl.when(i == nq - 1)
        def _():
            for p in range(g):
                for jj in range(nq):
                    dk_ref[jj * t:(jj + 1) * t, 128 * p:128 * (p + 1)] = dkt_acc[p, jj].T.astype(BF)
                    dv_ref[jj * t:(jj + 1) * t, 128 * p:128 * (p + 1)] = dvt_acc[p, jj].T.astype(BF)

    w = 128 * g
    tile_spec = pl.BlockSpec((t, w), lambda p, i: (i, p))
    col_spec = pl.BlockSpec((SEQ, w), lambda p, i: (0, p))
    shp = jax.ShapeDtypeStruct((SEQ, 512), BF)
    body, in_specs, args = _add_dep(
        body, [pl.BlockSpec((t, w), lambda p, i: (i, C_QS // w + p)),
               pl.BlockSpec((SEQ, w), lambda p, i: (0, C_KS // w + p)),
               pl.BlockSpec((SEQ, w), lambda p, i: (0, C_VS // w + p)),
               tile_spec, tile_spec],
        [proj, proj, proj, dsb, tot], dep)
    return pl.pallas_call(
        body, name="sb_bwd", grid=(N_PAIRS // g, nq),
        in_specs=in_specs,
        out_specs=[tile_spec, col_spec, col_spec],
        out_shape=[shp, shp, shp],
        scratch_shapes=[pltpu.VMEM((g, nq, 128, t), BF), pltpu.VMEM((g, nq, 128, t), BF),
                        pltpu.VMEM((g, nq, 128, t), F32), pltpu.VMEM((g, nq, 128, t), F32)],
        compiler_params=_cp(("parallel", "arbitrary")),
    )(*args)


def _mix(retg, sb, proj, w_ret, w_sb4):
    tm, tn = min(512, SEQ), 512

    def body(r_ref, s_ref, ar_ref, as_ref, wr_ref, ws_ref, mix_ref, rb_ref, sbp_ref):
        rb = _dot(r_ref[...], wr_ref[...])
        sbv = s_ref[...]
        sbp = jnp.concatenate([_dot(sbv, ws_ref[0]), _dot(sbv, ws_ref[1])], axis=1)
        mix = _sigmoid(ar_ref[...].astype(F32)) * rb + _sigmoid(as_ref[...].astype(F32)) * sbp
        mix_ref[...] = mix.astype(BF)
        rb_ref[...] = rb.astype(BF)
        sbp_ref[...] = sbp.astype(BF)

    out = pl.BlockSpec((tm, tn), lambda j, i: (i, j))
    shp = jax.ShapeDtypeStruct((SEQ, D_MODEL), BF)
    return pl.pallas_call(
        body, name="mix", grid=(D_MODEL // tn, SEQ // tm),
        in_specs=[pl.BlockSpec((tm, D_MODEL), lambda j, i: (i, 0)),
                  pl.BlockSpec((tm, 512), lambda j, i: (i, 0)),
                  pl.BlockSpec((tm, tn), lambda j, i: (i, C_AR // tn + j)),
                  pl.BlockSpec((tm, tn), lambda j, i: (i, C_AS // tn + j)),
                  pl.BlockSpec((D_MODEL, tn), lambda j, i: (0, j)),
                  pl.BlockSpec((2, 512, 256), lambda j, i: (j, 0, 0))],
        out_specs=[out, out, out], out_shape=[shp, shp, shp],
        compiler_params=_cp(("parallel", "parallel")),
    )(retg, sb, proj, proj, w_ret, w_sb4)


def _out_proj(mixed, x, vecs, w_out):
    tm = min(512, SEQ)

    def body(m_ref, x_ref, vec_ref, w_ref, y_ref, h1_ref, h2_ref):
        y = _dot(m_ref[...], w_ref[...])
        h1 = x_ref[...] + _row(vec_ref, V_GT1) * (y * _rms(y)) * _row(vec_ref, V_G2)
        g = _row(vec_ref, V_G3) * (1.0 + _row(vec_ref, V_SC2))
        y_ref[...] = y
        h1_ref[...] = h1
        h2_ref[...] = (h1 * _rms(h1) * g + _row(vec_ref, V_SH2)).astype(BF)

    row = pl.BlockSpec((tm, D_MODEL), lambda i: (i, 0))
    f32 = jax.ShapeDtypeStruct((SEQ, D_MODEL), F32)
    return pl.pallas_call(
        body, name="out_proj", grid=(SEQ // tm,),
        in_specs=[row, row, pl.BlockSpec((16, D_MODEL), lambda i: (0, 0)),
                  pl.BlockSpec((D_MODEL, D_MODEL), lambda i: (0, 0))],
        out_specs=[row, row, row],
        out_shape=[f32, f32, jax.ShapeDtypeStruct((SEQ, D_MODEL), BF)],
        compiler_params=_cp(("parallel",)),
    )(mixed, x, vecs, w_out)


def _ffn_up(h2, w_ff14):
    tm = min(1024, SEQ)

    def body(h_ref, w_ref, u_ref, a_ref):
        u = _dot(h_ref[...], w_ref[...])
        r = jnp.maximum(u, 0.0)
        u_ref[...] = u.astype(BF)
        a_ref[...] = (r * r).astype(BF)

    out = pl.BlockSpec((tm, D_MODEL), lambda j, i: (i, j))
    shp = jax.ShapeDtypeStruct((SEQ, D_FF), BF)
    return pl.pallas_call(
        body, name="ffn_up", grid=(N_CHIPS, SEQ // tm),
        in_specs=[pl.BlockSpec((tm, D_MODEL), lambda j, i: (i, 0)),
                  pl.BlockSpec((None, D_MODEL, D_MODEL), lambda j, i: (j, 0, 0))],
        out_specs=[out, out], out_shape=[shp, shp],
        compiler_params=_cp(("parallel", "parallel")),
    )(h2, w_ff14)


def _ffn_down_loss(act, h1, target, vecs, w_ff2):
    tm = min(512, SEQ)

    def body(a_ref, h1_ref, t_ref, vec_ref, w_ref, dout_ref, df_ref, st_ref):
        @pl.when(pl.program_id(0) == 0)
        def _():
            st_ref[...] = jnp.zeros_like(st_ref)

        f = _dot(a_ref[...], w_ref[...])
        r4 = _rms(f)
        fn = f * r4
        gt2, g4 = _row(vec_ref, V_GT2), _row(vec_ref, V_G4)
        diff = h1_ref[...] + gt2 * fn * g4 - t_ref[...]
        dout = diff * (1.0 / D_MODEL)
        dfn = dout * gt2 * g4
        dout_ref[...] = dout
        df_ref[...] = (r4 * (dfn - fn * _rowmean(dfn * fn))).astype(BF)
        st_ref[0:1, :] += _colsum(dout * fn * g4)
        st_ref[1:2, :] += _colsum(dout * gt2 * fn)
        st_ref[2:3, :] += _colsum(diff * diff) * (0.5 / D_MODEL)

    row = pl.BlockSpec((tm, D_MODEL), lambda i: (i, 0))
    return pl.pallas_call(
        body, name="ffn_down_loss", grid=(SEQ // tm,),
        in_specs=[pl.BlockSpec((tm, D_FF), lambda i: (i, 0)), row, row,
                  pl.BlockSpec((16, D_MODEL), lambda i: (0, 0)),
                  pl.BlockSpec((D_FF, D_MODEL), lambda i: (0, 0))],
        out_specs=[row, row, pl.BlockSpec((8, D_MODEL), lambda i: (0, 0))],
        out_shape=[jax.ShapeDtypeStruct((SEQ, D_MODEL), F32), jax.ShapeDtypeStruct((SEQ, D_MODEL), BF),
                   jax.ShapeDtypeStruct((8, D_MODEL), F32)],
        compiler_params=_cp(("arbitrary",)),
    )(act, h1, target, vecs, w_ff2)


def _ffn_down_bwd(df, u, w_ff2):
    tm, tn = min(512, SEQ), 2048

    def body(df_ref, u_ref, w_ref, du_ref):
        da = _dot_nt(df_ref[...], w_ref[...])
        du_ref[...] = (da * (2.0 * jnp.maximum(u_ref[...].astype(F32), 0.0))).astype(BF)

    return pl.pallas_call(
        body, name="ffn_down_bwd", grid=(D_FF // tn, SEQ // tm),
        in_specs=[pl.BlockSpec((tm, D_MODEL), lambda j, i: (i, 0)),
                  pl.BlockSpec((tm, tn), lambda j, i: (i, j)),
                  pl.BlockSpec((tn, D_MODEL), lambda j, i: (j, 0))],
        out_specs=pl.BlockSpec((tm, tn), lambda j, i: (i, j)),
        out_shape=jax.ShapeDtypeStruct((SEQ, D_FF), BF),
        compiler_params=_cp(("parallel", "parallel")),
    )(df, u, w_ff2)


def _ffn_up_bwd(du, h1, y, dout, vecs, w_ff14):
    tm = 256

    def body(du_ref, h1_ref, y_ref, dout_ref, vec_ref, w_ref, dh1_ref, dy_ref, st_ref):
        @pl.when(pl.program_id(0) == 0)
        def _():
            st_ref[...] = jnp.zeros_like(st_ref)

        dh2 = _dot_nt(du_ref[:, 0:D_MODEL], w_ref[0])
        for k in range(1, N_CHIPS):
            dh2 = dh2 + _dot_nt(du_ref[:, k * D_MODEL:(k + 1) * D_MODEL], w_ref[k])
        h1 = h1_ref[...]
        r3 = _rms(h1)
        hn3 = h1 * r3
        g3, sc2 = _row(vec_ref, V_G3), _row(vec_ref, V_SC2)
        dhn3 = dh2 * g3 * (1.0 + sc2)
        dh1 = dout_ref[...] + r3 * (dhn3 - hn3 * _rowmean(dhn3 * hn3))
        y = y_ref[...]
        r2 = _rms(y)
        yn = y * r2
        gt1, g2 = _row(vec_ref, V_GT1), _row(vec_ref, V_G2)
        dyn = dh1 * gt1 * g2
        dh1_ref[...] = dh1
        dy_ref[...] = (r2 * (dyn - yn * _rowmean(dyn * yn))).astype(BF)
        st_ref[0:1, :] += _colsum(dh2)
        st_ref[1:2, :] += _colsum(dh2 * hn3 * g3)
        st_ref[2:3, :] += _colsum(dh2 * hn3 * (1.0 + sc2))
        st_ref[3:4, :] += _colsum(dh1 * yn * g2)
        st_ref[4:5, :] += _colsum(dh1 * gt1 * yn)

    row = pl.BlockSpec((tm, D_MODEL), lambda i: (i, 0))
    return pl.pallas_call(
        body, name="ffn_up_bwd", grid=(SEQ // tm,),
        in_specs=[pl.BlockSpec((tm, D_FF), lambda i: (i, 0)), row, row, row,
                  _resident((16, D_MODEL)), _resident(w_ff14.shape)],
        out_specs=[row, row, pl.BlockSpec((8, D_MODEL), lambda i: (0, 0))],
        out_shape=[jax.ShapeDtypeStruct((SEQ, D_MODEL), F32), jax.ShapeDtypeStruct((SEQ, D_MODEL), BF),
                   jax.ShapeDtypeStruct((8, D_MODEL), F32)],
        compiler_params=_cp(("arbitrary",)),
    )(du, h1, y, dout, vecs, w_ff14)


def _mix_bwd(dy, proj, rb, sbp, w_out, w_sb4):
    tm, half = 256, 512

    def body(dy_ref, ar0, ar1, as0, as1, rb_ref, sbp_ref, wo_ref, ws_ref, drb_ref, dsbp_ref, da_ref, dsb_ref):
        dm_all = _dot_nt(dy_ref[...], wo_ref[...])
        dsb = jnp.zeros((tm, 512), F32)
        for hf, (ar_ref, as_ref) in enumerate(((ar0, as0), (ar1, as1))):
            cols = slice(half * hf, half * (hf + 1))
            dm = dm_all[:, cols]
            sr = _sigmoid(ar_ref[...].astype(F32))
            ss = _sigmoid(as_ref[...].astype(F32))
            dsbp = (dm * ss).astype(BF)
            drb_ref[:, cols] = (dm * sr).astype(BF)
            dsbp_ref[:, cols] = dsbp
            da_ref[:, cols] = (dm * rb_ref[:, cols].astype(F32) * sr * (1.0 - sr)).astype(BF)
            da_ref[:, D_MODEL + half * hf:D_MODEL + half * (hf + 1)] = (
                dm * sbp_ref[:, cols].astype(F32) * ss * (1.0 - ss)).astype(BF)
            dsb = dsb + _dot_nt(dsbp[:, :256], ws_ref[2 * hf]) + _dot_nt(dsbp[:, 256:], ws_ref[2 * hf + 1])
        dsb_ref[...] = dsb.astype(BF)

    row = pl.BlockSpec((tm, D_MODEL), lambda i: (i, 0))
    gate = lambda c0: pl.BlockSpec((tm, half), lambda i: (i, c0 // half))
    shp = jax.ShapeDtypeStruct((SEQ, D_MODEL), BF)
    return pl.pallas_call(
        body, name="mix_bwd", grid=(SEQ // tm,),
        in_specs=[row, gate(C_AR), gate(C_AR + half), gate(C_AS), gate(C_AS + half), row, row,
                  _resident((D_MODEL, D_MODEL)), _resident(w_sb4.shape)],
        out_specs=[row, row, pl.BlockSpec((tm, 2 * D_MODEL), lambda i: (i, 0)), pl.BlockSpec((tm, 512), lambda i: (i, 0))],
        out_shape=[shp, shp, jax.ShapeDtypeStruct((SEQ, 2 * D_MODEL), BF), jax.ShapeDtypeStruct((SEQ, 512), BF)],
        compiler_params=_cp(("parallel",)),
    )(dy, proj, proj, proj, proj, rb, sbp, w_out, w_sb4)


def _ret_branch_bwd(drb, proj, o_raw, gn_g, w_ret):
    tm, tn = min(512, SEQ), 512

    def body(d_ref, g_ref, o_ref, gn_ref, w_ref, dret_ref, dgr_ref, st_ref):
        @pl.when(pl.program_id(1) == 0)
        def _():
            st_ref[...] = jnp.zeros_like(st_ref)

        dretg = _dot_nt(d_ref[...], w_ref[...])
        for gi in range(tn // 128):
            cols = slice(128 * gi, 128 * (gi + 1))
            o = o_ref[:, cols]
            d = o - _rowmean(o)
            rstd = lax.rsqrt(_rowmean(d * d) + EPS)
            nh = d * rstd
            gain = gn_ref[:, cols]
            gr = g_ref[:, cols].astype(F32)
            sg = _sigmoid(gr)
            dg = dretg[:, cols]
            dgn = dg * gr * sg
            dnh = dgn * gain
            dgr_ref[:, cols] = (dg * nh * gain * sg * (1.0 + gr * (1.0 - sg))).astype(BF)
            dret_ref[:, cols] = (rstd * (dnh - _rowmean(dnh) - nh * _rowmean(dnh * nh))).astype(BF)
            st_ref[0:1, cols] += _colsum(dgn * nh)

    tile = pl.BlockSpec((tm, tn), lambda j, i: (i, j))
    shp = jax.ShapeDtypeStruct((SEQ, D_MODEL), BF)
    return pl.pallas_call(
        body, name="ret_branch_bwd", grid=(D_MODEL // tn, SEQ // tm),
        in_specs=[pl.BlockSpec((tm, D_MODEL), lambda j, i: (i, 0)),
                  pl.BlockSpec((tm, tn), lambda j, i: (i, C_GR // tn + j)),
                  tile, pl.BlockSpec((1, tn), lambda j, i: (0, j)),
                  pl.BlockSpec((tn, D_MODEL), lambda j, i: (j, 0))],
        out_specs=[tile, tile, pl.BlockSpec((8, tn), lambda j, i: (0, j))],
        out_shape=[shp, shp, jax.ShapeDtypeStruct((8, D_MODEL), F32)],
        compiler_params=_cp(("parallel", "arbitrary")),
    )(drb, proj, o_raw, gn_g, w_ret)


def _dproj_segments(widths):
    wc = D_IN // N_CHIPS
    segs, start = [], 0
    for pi, width in enumerate(widths):
        lo = start
        while lo < start + width:
            j = lo // wc
            hi = min(start + width, (j + 1) * wc)
            segs.append((j, lo - j * wc, pi, lo - start, hi - lo))
            lo = hi
        start += width
    assert start == D_IN
    return segs


def _in_proj_bwd(pieces, x, dh1, vecs, w_in4):
    tm = 256
    n = len(pieces)
    segs = _dproj_segments([p.shape[1] for p in pieces])

    def body(*refs):
        x_ref, dh1_ref, vec_ref, w_ref, dx_ref, st_ref = refs[n:]

        @pl.when(pl.program_id(0) == 0)
        def _():
            st_ref[...] = jnp.zeros_like(st_ref)

        dh = jnp.zeros((tm, D_MODEL), F32)
        for j, so, pi, po, width in segs:
            dh = dh + _dot_nt(refs[pi][:, po:po + width], w_ref[j, :, so:so + width])
        xx = x_ref[...]
        r1 = _rms(xx)
        xn = xx * r1
        g1, sc1 = _row(vec_ref, V_G1), _row(vec_ref, V_SC1)
        dxn = dh * g1 * (1.0 + sc1)
        dx_ref[...] = dh1_ref[...] + r1 * (dxn - xn * _rowmean(dxn * xn))
        st_ref[0:1, :] += _colsum(dh)
        st_ref[1:2, :] += _colsum(dh * xn * g1)
        st_ref[2:3, :] += _colsum(dh * xn * (1.0 + sc1))

    row = pl.BlockSpec((tm, D_MODEL), lambda i: (i, 0))
    return pl.pallas_call(
        body, name="in_proj_bwd", grid=(SEQ // tm,),
        in_specs=[pl.BlockSpec((tm, p.shape[1]), lambda i: (i, 0)) for p in pieces] + [
            row, row, _resident((16, D_MODEL)), _resident(w_in4.shape)],
        out_specs=[row, pl.BlockSpec((8, D_MODEL), lambda i: (0, 0))],
        out_shape=[jax.ShapeDtypeStruct((SEQ, D_MODEL), F32), jax.ShapeDtypeStruct((8, D_MODEL), F32)],
        compiler_params=_cp(("arbitrary",)),
    )(*pieces, x, dh1, vecs, w_in4)


def _grad_w_in(h, pieces, dep=None):
    ta = 256
    n = len(pieces)
    segs = _dproj_segments([p.shape[1] for p in pieces])

    def body(*refs):
        h_ref, o_ref = refs[n], refs[n + 1]
        hh = h_ref[...]
        for j, so, pi, po, width in segs:
            o_ref[j, :, so:so + width] = _dot_tn(hh, refs[pi][:, po:po + width])

    body, in_specs, args = _add_dep(
        body, [_resident(p.shape) for p in pieces] + [pl.BlockSpec((SEQ, ta), lambda i: (0, i))],
        list(pieces) + [h], dep)
    return pl.pallas_call(
        body, name="grad_w_in", grid=(D_MODEL // ta,),
        in_specs=in_specs,
        out_specs=pl.BlockSpec((N_CHIPS, ta, D_IN // N_CHIPS), lambda i: (0, i, 0)),
        out_shape=jax.ShapeDtypeStruct((N_CHIPS, D_MODEL, D_IN // N_CHIPS), F32),
        compiler_params=_cp(("parallel",)),
    )(*args)


def _weight_grad(a, b, ta, tb, col_sharded, name, dep=None):
    ka, nb_ = a.shape[1], b.shape[1]

    def body(a_ref, b_ref, o_ref):
        o_ref[...] = _dot_tn(a_ref[...], b_ref[...])

    body, in_specs, args = _add_dep(
        body, [pl.BlockSpec((SEQ, ta), lambda i, j: (0, i)), pl.BlockSpec((SEQ, tb), lambda i, j: (0, j))],
        [a, b], dep)

    if col_sharded:
        per = nb_ // N_CHIPS // tb
        out_shape = jax.ShapeDtypeStruct((N_CHIPS, ka, nb_ // N_CHIPS), F32)
        out_spec = pl.BlockSpec((None, ta, tb), lambda i, j: (j // per, i, j % per))
    else:
        per = ka // N_CHIPS // ta
        out_shape = jax.ShapeDtypeStruct((N_CHIPS, ka // N_CHIPS, nb_), F32)
        out_spec = pl.BlockSpec((None, ta, tb), lambda i, j: (i // per, i % per, j))
    return pl.pallas_call(
        body, name=name, grid=(ka // ta, nb_ // tb),
        in_specs=in_specs, out_specs=out_spec, out_shape=out_shape,
        compiler_params=_cp(("parallel", "parallel")),
    )(*args)


def _rope_constants():
    freq = np.float32(ROPE_BASE) ** (-np.arange(0, 64, 2, dtype=np.float32) / np.float32(64))
    inv = np.tile(freq.astype(np.float32), 4).reshape(1, 128)
    sign = np.tile(np.concatenate([-np.ones(32, np.float32), np.ones(32, np.float32)]), 2).reshape(1, 128)
    return jnp.asarray(inv), jnp.asarray(sign)


def _log_gamma():
    return jnp.asarray(np.log1p(-(2.0 ** (-5.0 - np.arange(8, dtype=np.float64)))).astype(np.float32))


def _halves(g):
    return g.reshape(N_CHIPS, 2, g.shape[1] // 2, g.shape[2])


def kernel(x, c, positions, ada_w, ada_b, pre_mix_g, post_mix_g, pre_ffn_g, post_ffn_g, w_in, ret_gn_g, w_ret_branch, w_sb_branch, w_out, w_ff1, w_ff2, loss_target, m_ada_w, m_ada_b, m_pre_mix_g, m_post_mix_g, m_pre_ffn_g, m_post_ffn_g, m_w_in, m_ret_gn_g, m_w_ret_branch, m_w_sb_branch, m_w_out, m_w_ff1, m_w_ff2, v_ada_w, v_ada_b, v_pre_mix_g, v_post_mix_g, v_pre_ffn_g, v_post_ffn_g, v_w_in, v_ret_gn_g, v_w_ret_branch, v_w_sb_branch, v_w_out, v_w_ff1, v_w_ff2):
    names = ["w_in", "w_ret", "w_sb", "w_out", "w_ff1", "w_ff2"]
    big = dict(zip(names, [w_in, w_ret_branch, w_sb_branch, w_out, w_ff1, w_ff2]))
    big_m = dict(zip(names, [m_w_in, m_w_ret_branch, m_w_sb_branch, m_w_out, m_w_ff1, m_w_ff2]))
    big_v = dict(zip(names, [v_w_in, v_w_ret_branch, v_w_sb_branch, v_w_out, v_w_ff1, v_w_ff2]))
    rest = names[1:]
    cidx = lax.axis_index("c").astype(jnp.int32).reshape(1)
    kidx = (2 * lax.axis_index("x") + lax.axis_index("y")).astype(jnp.int32).reshape(1)
    x0, target = x[0], loss_target[0]

    c_all, mod4 = _mod_exchange(c, ada_w[0], ada_b.reshape(N_CHIPS, -1))
    vecs = jnp.concatenate([mod4.reshape(6, D_MODEL), pre_mix_g, post_mix_g, pre_ffn_g, post_ffn_g,
                            jnp.zeros((6, D_MODEL), F32)], axis=0)

    buf_in, sem_in, tok_in = _gather_start("gather_in_start", [_cast_bf16(w_in[0], kidx, mod4, "cast_w_in")])
    buf_rest, sem_rest, tok_rest = _gather_start(
        "gather_rest_start", [_cast_bf16(big[nm][0], kidx, tok_in, "cast_" + nm) for nm in rest])
    inv_freq, sign = _rope_constants()
    lg = _log_gamma()
    cos, sin_s = _rope_tables(positions.reshape(SEQ, 1), _tie(tok_rest, inv_freq), sign)
    buf_in, sem_in, tok_in = _gather_pass("gather_in_pass", buf_in, sem_in, cos)
    (w_in4,) = _gather_finish("gather_in_finish", buf_in, sem_in, tok_in)

    h, proj = _ln_proj(x0, vecs, w_in4)
    sb, tot = _sb_fwd(proj)
    buf_rest, sem_rest, tok_rest = _gather_pass("gather_rest_pass", buf_rest, sem_rest, sb)
    o_raw, retg, states = _ret_fwd(proj, cos, sin_s, _tie(tok_rest, ret_gn_g), lg)
    w_ret4, w_sb4, w_out4, w_ff14, w_ff24 = _gather_finish("gather_rest_finish", buf_rest, sem_rest, retg)
    w_ret = w_ret4.reshape(D_MODEL, D_MODEL)
    w_out2 = w_out4.reshape(D_MODEL, D_MODEL)
    w_ff2_2 = w_ff24.reshape(D_FF, D_MODEL)
    mixed, rb, sbp = _mix(retg, sb, proj, w_ret, w_sb4)
    y, h1, h2 = _out_proj(mixed, x0, vecs, w_out2)
    u, act = _ffn_up(h2, w_ff14)
    dout, df, st_a = _ffn_down_loss(act, h1, target, vecs, w_ff2_2)

    du = _ffn_down_bwd(df, u, w_ff2_2)
    grads = {"w_ff2": _weight_grad(act, df, 512, 1024, False, "grad_w_ff2")}
    dh1, dy, st_b = _ffn_up_bwd(du, h1, y, dout, vecs, w_ff14)
    grads["w_ff1"] = _weight_grad(h2, du, 512, 1024, True, "grad_w_ff1")
    drb, dsbp, da, dsb = _mix_bwd(dy, proj, rb, sbp, w_out2, w_sb4)
    grads["w_out"] = _weight_grad(mixed, dy, 256, 1024, False, "grad_w_out")
    dret, dg_r, st_c = _ret_branch_bwd(drb, proj, o_raw, ret_gn_g, w_ret)
    grads["w_ret"] = _weight_grad(retg, drb, 256, 1024, False, "grad_w_ret")
    grads["w_sb"] = _weight_grad(sb, dsbp, 512, 256, True, "grad_w_sb")

    bufs, sems, tok = _pair_send_start("rs_rest_pair_send", [_halves(grads[nm]) for nm in rest])
    dqkv_r = _ret_bwd(proj, cos, sin_s, dret, states, _tie(tok, lg))
    mine, theirs = _pair_send_wait("rs_rest_pair_recv", bufs, sems, dqkv_r)
    pair_sums = [_pair_add(g, r, cidx, "pair_add_" + nm) for g, r, nm in zip(mine, theirs, rest)]
    bufs, sems, tok = _chip_send_start("rs_rest_chip_send", pair_sums)
    dq_s, dk_s, dv_s = _sb_bwd(proj, dsb, tot, dep=tok)
    own, parts = _chip_send_wait("rs_rest_chip_recv", bufs, sems, dq_s)
    sums = [_chip_add(o, p, kidx, cidx, "chip_add_" + nm) for o, p, nm in zip(own, parts, rest)]
    bufs, sems, tok = _pair_swap_start("rs_rest_pair_swap", sums)
    dproj = [dqkv_r, dg_r, dq_s, dk_s, dv_s, da]
    g_in = _grad_w_in(h, dproj, dep=tok)
    full_rest = _pair_swap_wait("rs_rest_pair_swapped", bufs, sems, g_in)

    bufs, sems, tok = _pair_send_start("rs_in_pair_send", [_halves(g_in)])
    dx, st_d = _in_proj_bwd(dproj, x0, dh1, _tie(tok, vecs), w_in4)
    mine, theirs = _pair_send_wait("rs_in_pair_recv", bufs, sems, dx)
    bufs, sems, tok = _chip_send_start("rs_in_chip_send", [_pair_add(mine[0], theirs[0], cidx, "pair_add_w_in")])
    out = {}
    for nm, g in zip(rest, full_rest):
        w = big[nm][0]
        out[nm] = _adamw(w, big_m[nm][0], big_v[nm][0], g.reshape(w.shape), "adamw_" + nm, dep=tok)

    payload = jnp.concatenate([
        st_d[0:2], st_b[3:4], st_b[0:2], st_a[0:1],
        st_d[2:3], st_b[4:5], st_b[2:3], st_a[1:2],
        st_c[0:1], st_a[2:3]], axis=0)

    def table(b6, g5):
        return jnp.concatenate([b6.reshape(6, D_MODEL)] + g5 + [jnp.zeros((5, D_MODEL), F32)], axis=0)

    wsm = table(ada_b, [pre_mix_g, post_mix_g, pre_ffn_g, post_ffn_g, ret_gn_g])
    msm = table(m_ada_b, [m_pre_mix_g, m_post_mix_g, m_pre_ffn_g, m_post_ffn_g, m_ret_gn_g])
    vsm = table(v_ada_b, [v_pre_mix_g, v_post_mix_g, v_pre_ffn_g, v_post_ffn_g, v_ret_gn_g])
    g_ada, gsm, dsm, mosm, vosm, loss = _small_exchange(
        payload.reshape(N_PAY, 1, D_MODEL), c_all, wsm, msm, vsm)
    ada_out = _adamw(ada_w[0], m_ada_w[0], v_ada_w[0], g_ada, "adamw_ada_w")

    own, parts = _chip_send_wait("rs_in_chip_recv", bufs, sems, ada_out[1])
    bufs, sems, tok = _pair_swap_start(
        "rs_in_pair_swap", [_chip_add(own[0], parts[0], kidx, cidx, "chip_add_w_in")])
    (full_in,) = _pair_swap_wait("rs_in_pair_swapped", bufs, sems, tok)
    out["w_in"] = _adamw(w_in[0], m_w_in[0], v_w_in[0], full_in.reshape(w_in.shape[1:]), "adamw_w_in")

    def unpack(tab):
        return [tab[0:6].reshape(1, 6 * D_MODEL)] + [tab[6 + r:7 + r] for r in range(5)]

    def ordered(which):
        sm = unpack([gsm, dsm, mosm, vosm][which])
        bg = [out[nm][which][None] for nm in names]
        return [ada_out[which][None], sm[0], sm[1], sm[2], sm[3], sm[4], bg[0], sm[5]] + bg[1:]

    return (loss.reshape(()), dx[None], *ordered(0), *ordered(1), *ordered(2), *ordered(3))
```

```python
import functools

import numpy as np
import jax
import jax.numpy as jnp
from jax import lax
from jax.experimental import pallas as pl
from jax.experimental.pallas import tpu as pltpu

SEQ = 2048
D_MODEL = 1024
D_IN = 6656
D_FF = 4096
N_CHIPS = 4
EPS = 1e-6
ROPE_BASE = 10000.0
RET_BLOCK = 256
RET_CHUNK_SHIFT = 6
SB_BLOCK = 256
QK_SCALE = 0.125
N_PAIRS = 4
SB_GROUP = 2

ADAM_LR = 0.001
ADAM_B1 = 0.9
ADAM_B2 = 0.999
ADAM_EPS = 1e-08
ADAM_WD = 0.01
ADAM_STEP = 10

BF = jnp.bfloat16
F32 = jnp.float32
MESH = pl.DeviceIdType.MESH
VMEM_LIMIT = 56 * 1024 * 1024
ANY = pl.BlockSpec(memory_space=pl.ANY)

C_QR, C_KR, C_VR, C_GR, C_QS, C_KS, C_VS, C_AR, C_AS = 0, 512, 1024, 2048, 3072, 3584, 4096, 4608, 5632

V_SH1, V_SC1, V_GT1, V_SH2, V_SC2, V_GT2, V_G1, V_G2, V_G3, V_G4 = range(10)
P_DSH1, P_DSC1, P_DGT1, P_DSH2, P_DSC2, P_DGT2, P_DG1, P_DG2, P_DG3, P_DG4, P_DGN, P_LOSS = range(12)
N_PAY = 12


def _cp(sem=None, **kw):
    if sem is not None:
        kw["dimension_semantics"] = sem
    return pltpu.CompilerParams(vmem_limit_bytes=VMEM_LIMIT, **kw)


def _dot(a, b):
    return jnp.dot(a, b, preferred_element_type=F32)


def _dot_nt(a, b):
    return lax.dot_general(a, b, (((1,), (1,)), ((), ())), preferred_element_type=F32)


def _dot_tn(a, b):
    return lax.dot_general(a, b, (((0,), (0,)), ((), ())), preferred_element_type=F32)


def _row(ref, i):
    return ref[i:i + 1, :]


def _rms(v):
    return lax.rsqrt(jnp.mean(v * v, axis=1, keepdims=True) + EPS)


def _colsum(v):
    return jnp.sum(v, axis=0, keepdims=True)


def _rowmean(v):
    return jnp.mean(v, axis=1, keepdims=True)


def _sigmoid(v):
    return 1.0 / (1.0 + jnp.exp(-v))


def _cast_bf16(w, kidx, dep, name):
    rows, cols = w.shape
    tr = min(rows, 512)

    def body(k_ref, w_ref, dep_ref, o_ref):
        o_ref[...] = w_ref[...].astype(BF)

    return pl.pallas_call(
        body, name=name,
        grid_spec=pltpu.PrefetchScalarGridSpec(
            num_scalar_prefetch=1, grid=(rows // tr,),
            in_specs=[pl.BlockSpec((tr, cols), lambda i, k_ref: (i, 0)), ANY],
            out_specs=pl.BlockSpec((None, tr, cols), lambda i, k_ref: (k_ref[0], i, 0))),
        out_shape=jax.ShapeDtypeStruct((N_CHIPS, rows, cols), BF),
        compiler_params=_cp(("parallel",)),
    )(kidx, w, dep)


def _adamw_math(w, g, m, v):
    m = ADAM_B1 * m + (1.0 - ADAM_B1) * g
    v = ADAM_B2 * v + (1.0 - ADAM_B2) * (g * g)
    m_hat = m / (1.0 - ADAM_B1 ** ADAM_STEP)
    v_hat = v / (1.0 - ADAM_B2 ** ADAM_STEP)
    delta = -ADAM_LR * (m_hat / (jnp.sqrt(v_hat) + ADAM_EPS) + ADAM_WD * w)
    return delta, m, v


def _adamw(w, m, v, g, name, dep=None):
    rows, cols = w.shape
    tr = min(rows, 256)

    def body(w_ref, m_ref, v_ref, g_ref, go_ref, d_ref, mo_ref, vo_ref):
        gg = g_ref[...]
        d, mm, vv = _adamw_math(w_ref[...], gg, m_ref[...], v_ref[...])
        go_ref[...] = gg
        d_ref[...] = d
        mo_ref[...] = mm
        vo_ref[...] = vv

    spec = pl.BlockSpec((tr, cols), lambda i: (i, 0))
    shp = jax.ShapeDtypeStruct((rows, cols), F32)
    body, in_specs, args = _add_dep(body, [spec] * 4, [w, m, v, g], dep)
    return pl.pallas_call(
        body, name=name, grid=(rows // tr,),
        in_specs=in_specs, out_specs=[spec] * 4, out_shape=[shp] * 4,
        compiler_params=_cp(("parallel",)),
    )(*args)


def _place():
    x, y, c = lax.axis_index("x"), lax.axis_index("y"), lax.axis_index("c")
    return x, y, c


HBM = pl.BlockSpec(memory_space=pltpu.HBM)
SEM = pl.BlockSpec(memory_space=pltpu.SEMAPHORE)
EFFECT = pltpu.SideEffectType.DATAFLOW_SIDE_EFFECTING


def _tie(token, small):
    return small + token[0, 0]


def _add_dep(body, in_specs, args, dep):
    if dep is None:
        return body, list(in_specs), list(args)
    n = len(args)

    def wrapped(*refs):
        body(*refs[:n], *refs[n + 1:])

    return wrapped, list(in_specs) + [ANY], list(args) + [dep]


def _split_call(name, bufs, run, old=None, after=None, new=0):
    nb = len(bufs)
    n_old = 2 if old is not None else 0
    n_in = nb + n_old + (1 if after is not None else 0)

    def body(*refs):
        old_sems = (refs[nb], refs[nb + 1]) if old is not None else None
        new_sems = (refs[n_in], refs[n_in + 1]) if new else None
        run(refs[:nb], old_sems, new_sems)
        if new:
            refs[-1][...] = jnp.zeros_like(refs[-1])

    in_specs = [HBM] * nb + [SEM] * n_old + ([ANY] if after is not None else [])
    out_shape = [pltpu.SemaphoreType.DMA((new,))] * 2 if new else []
    out_specs = [SEM, SEM] if new else []
    out_shape += [pltpu.HBM(b.shape, b.dtype) for b in bufs]
    out_specs += [HBM] * nb
    if new:
        out_shape.append(jax.ShapeDtypeStruct((8, 128), F32))
        out_specs.append(pl.BlockSpec(memory_space=pltpu.VMEM))
    first = 2 if new else 0
    args = [pltpu.with_memory_space_constraint(b, pltpu.HBM) for b in bufs]
    if old is not None:
        args += [old[0], old[1]]
    if after is not None:
        args.append(after)
    outs = pl.pallas_call(
        body, name=name, in_specs=tuple(in_specs), out_specs=tuple(out_specs), out_shape=tuple(out_shape),
        input_output_aliases={i: i + first for i in range(nb)},
        compiler_params=pltpu.CompilerParams(has_side_effects=EFFECT),
    )(*args)
    thru = list(outs[first:first + nb])
    if new:
        return thru, (outs[0], outs[1]), outs[-1]
    return thru, None, None


def _remote(part_src, part_dst, sems, i, to):
    return pltpu.make_async_remote_copy(src_ref=part_src, dst_ref=part_dst, send_sem=sems[0].at[i],
                                        recv_sem=sems[1].at[i], device_id=to, device_id_type=MESH)


def _other_chips(x, y):
    return [(1 - x, y), (x, 1 - y), (1 - x, 1 - y)]


def _gather_start(name, bufs, after=None):
    def run(refs, old, new):
        x, y, c = _place()
        k = 2 * x + y
        for w, ref in enumerate(refs):
            rh = bufs[w].shape[1] // 2
            part = ref.at[k, pl.ds(c * rh, rh)]
            for j, (cx, cy) in enumerate(_other_chips(x, y)):
                _remote(part, part, new, 3 * w + j, (cx, cy, c)).start()

    return _split_call(name, bufs, run, after=after, new=3 * len(bufs))


def _gather_pass(name, bufs, sems, after):
    def run(refs, old, new):
        x, y, c = _place()
        k = 2 * x + y
        sib = (x, y, 1 - c)
        for w, ref in enumerate(refs):
            rh = bufs[w].shape[1] // 2
            for j, (cx, cy) in enumerate(_other_chips(x, y)):
                land = ref.at[2 * cx + cy, pl.ds(c * rh, rh)]
                _remote(land, land, old, 3 * w + j, (cx, cy, c)).wait_recv()
                _remote(land, land, new, 3 * w + j, sib).start()
        for w, ref in enumerate(refs):
            rh = bufs[w].shape[1] // 2
            part = ref.at[k, pl.ds(c * rh, rh)]
            for j, (cx, cy) in enumerate(_other_chips(x, y)):
                _remote(part, part, old, 3 * w + j, (cx, cy, c)).wait_send()

    return _split_call(name, bufs, run, old=sems, after=after, new=3 * len(bufs))


def _gather_finish(name, bufs, sems, after):
    def run(refs, old, new):
        x, y, c = _place()
        sib = (x, y, 1 - c)
        for w, ref in enumerate(refs):
            rh = bufs[w].shape[1] // 2
            for j, (cx, cy) in enumerate(_other_chips(x, y)):
                sent = ref.at[2 * cx + cy, pl.ds(c * rh, rh)]
                _remote(sent, sent, old, 3 * w + j, sib).wait_send()
                land = ref.at[2 * cx + cy, pl.ds((1 - c) * rh, rh)]
                _remote(land, land, old, 3 * w + j, sib).wait_recv()

    return _split_call(name, bufs, run, old=sems, after=after)[0]


def _pair_send_start(name, grads):
    n = len(grads)
    lands = [lax.empty((N_CHIPS,) + g.shape[2:], F32) for g in grads]

    def run(refs, old, new):
        x, y, c = _place()
        for w in range(n):
            _remote(refs[w].at[:, 1 - c], refs[n + w], new, w, (x, y, 1 - c)).start()

    return _split_call(name, list(grads) + lands, run, new=n)


def _pair_send_wait(name, bufs, sems, after):
    n = len(bufs) // 2

    def run(refs, old, new):
        x, y, c = _place()
        for w in range(n):
            cp = _remote(refs[w].at[:, 1 - c], refs[n + w], old, w, (x, y, 1 - c))
            cp.wait_send()
            cp.wait_recv()

    thru = _split_call(name, bufs, run, old=sems, after=after)[0]
    return thru[:n], thru[n:]


def _pair_add(g, recv, cidx, name):
    _, _, rh, cols = g.shape
    tr = min(rh, 256)

    def body(c_ref, g_ref, r_ref, o_ref):
        o_ref[...] = (g_ref[...] + r_ref[...]).astype(BF)

    return pl.pallas_call(
        body, name=name,
        grid_spec=pltpu.PrefetchScalarGridSpec(
            num_scalar_prefetch=1, grid=(rh // tr,),
            in_specs=[pl.BlockSpec((N_CHIPS, None, tr, cols), lambda i, c_ref: (0, c_ref[0], i, 0)),
                      pl.BlockSpec((N_CHIPS, tr, cols), lambda i, c_ref: (0, i, 0))],
            out_specs=pl.BlockSpec((N_CHIPS, tr, cols), lambda i, c_ref: (0, i, 0))),
        out_shape=jax.ShapeDtypeStruct((N_CHIPS, rh, cols), BF),
        compiler_params=_cp(("parallel",)),
    )(cidx, g, recv)


def _chip_send_start(name, sums):
    n = len(sums)
    lands = [lax.empty((3,) + s.shape[1:], BF) for s in sums]

    def run(refs, old, new):
        x, y, c = _place()
        for w in range(n):
            for j, (cx, cy) in enumerate(_other_chips(x, y)):
                _remote(refs[w].at[2 * cx + cy], refs[n + w].at[j], new, 3 * w + j, (cx, cy, c)).start()

    return _split_call(name, list(sums) + lands, run, new=3 * n)


def _chip_send_wait(name, bufs, sems, after):
    n = len(bufs) // 2

    def run(refs, old, new):
        x, y, c = _place()
        for w in range(n):
            for j, (cx, cy) in enumerate(_other_chips(x, y)):
                cp = _remote(refs[w].at[2 * cx + cy], refs[n + w].at[j], old, 3 * w + j, (cx, cy, c))
                cp.wait_send()
                cp.wait_recv()

    thru = _split_call(name, bufs, run, old=sems, after=after)[0]
    return thru[:n], thru[n:]


def _chip_add(own, parts, kidx, cidx, name):
    _, rh, cols = parts.shape
    tr = min(rh, 512)

    def body(k_ref, c_ref, own_ref, p_ref, o_ref):
        acc = own_ref[...].astype(F32)
        for s in range(3):
            acc = acc + p_ref[s].astype(F32)
        o_ref[...] = acc

    return pl.pallas_call(
        body, name=name,
        grid_spec=pltpu.PrefetchScalarGridSpec(
            num_scalar_prefetch=2, grid=(rh // tr,),
            in_specs=[pl.BlockSpec((None, tr, cols), lambda i, k_ref, c_ref: (k_ref[0], i, 0)),
                      pl.BlockSpec((3, tr, cols), lambda i, k_ref, c_ref: (0, i, 0))],
            out_specs=pl.BlockSpec((None, tr, cols), lambda i, k_ref, c_ref: (c_ref[0], i, 0))),
        out_shape=jax.ShapeDtypeStruct((2, rh, cols), F32),
        compiler_params=_cp(("parallel",)),
    )(kidx, cidx, own, parts)


def _pair_swap_start(name, bufs):
    def run(refs, old, new):
        x, y, c = _place()
        for w, ref in enumerate(refs):
            _remote(ref.at[c], ref.at[c], new, w, (x, y, 1 - c)).start()

    return _split_call(name, bufs, run, new=len(bufs))


def _pair_swap_wait(name, bufs, sems, after):
    def run(refs, old, new):
        x, y, c = _place()
        for w, ref in enumerate(refs):
            _remote(ref.at[c], ref.at[c], old, w, (x, y, 1 - c)).wait_send()
            _remote(ref.at[1 - c], ref.at[1 - c], old, w, (x, y, 1 - c)).wait_recv()

    return _split_call(name, bufs, run, old=sems, after=after)[0]


def _peers(x, y, c):
    out = []
    for code in range(1, 8):
        fx, fy, fc = (code >> 2) & 1, (code >> 1) & 1, code & 1
        px = 1 - x if fx else x
        py = 1 - y if fy else y
        pc = 1 - c if fc else c
        out.append((code, (px, py, pc)))
    return out


def _mod_exchange(c_row, ada_w, ada_b4, deps):
    ncol = ada_w.shape[1]

    def body(c_ref, w_ref, b_ref, *rest):
        call_ref, mod_ref, part_ref, send_sems, recv_sems = rest[len(deps):]
        x, y, c = _place()
        k = 2 * x + y
        me = 4 * x + 2 * y + c
        call_ref[pl.ds(me, 1), :] = c_ref[...]
        sends = []
        for code, peer in _peers(x, y, c):
            cp = pltpu.make_async_remote_copy(
                src_ref=c_ref, dst_ref=call_ref.at[pl.ds(me, 1), :],
                send_sem=send_sems.at[code], recv_sem=recv_sems.at[code],
                device_id=peer, device_id_type=MESH)
            cp.start()
            sends.append(cp)
        for code, (px, py, pc) in _peers(x, y, c):
            land = call_ref.at[pl.ds(4 * px + 2 * py + pc, 1), :]
            pltpu.make_async_remote_copy(
                src_ref=land, dst_ref=land, send_sem=send_sems.at[code], recv_sem=recv_sems.at[code],
                device_id=(px, py, pc), device_id_type=MESH).wait_recv()
        call = call_ref[...]
        act = call * _sigmoid(call)
        part = jnp.dot(act, w_ref[...], preferred_element_type=F32,
                       precision=lax.Precision.HIGHEST) + b_ref[pl.ds(k, 1), :]
        part_ref[...] = part
        mod_ref[pl.ds(k, 1), :] = part_ref[pl.ds(me, 1), :]
        chips = [(8 + j, peer) for j, (code, peer) in enumerate(_peers(x, y, c)) if code in (2, 4, 6)]
        for slot, (px, py, pc) in chips:
            cp = pltpu.make_async_remote_copy(
                src_ref=part_ref.at[pl.ds(4 * px + 2 * py + pc, 1), :], dst_ref=mod_ref.at[pl.ds(k, 1), :],
                send_sem=send_sems.at[slot], recv_sem=recv_sems.at[slot],
                device_id=(px, py, pc), device_id_type=MESH)
            cp.start()
            sends.append(cp)
        for slot, (px, py, pc) in chips:
            land = mod_ref.at[pl.ds(2 * px + py, 1), :]
            pltpu.make_async_remote_copy(
                src_ref=land, dst_ref=land, send_sem=send_sems.at[slot], recv_sem=recv_sems.at[slot],
                device_id=(px, py, pc), device_id_type=MESH).wait_recv()
        for cp in sends:
            cp.wait_send()

    vm = pl.BlockSpec(memory_space=pltpu.VMEM)
    return pl.pallas_call(
        body, name="mod_exchange",
        in_specs=[vm, vm, vm] + [ANY] * len(deps), out_specs=[vm, vm],
        out_shape=[jax.ShapeDtypeStruct((8, D_MODEL), F32), jax.ShapeDtypeStruct((N_CHIPS, ncol), F32)],
        scratch_shapes=[pltpu.VMEM((8, ncol), F32), pltpu.SemaphoreType.DMA((16,)),
                        pltpu.SemaphoreType.DMA((16,))],
        compiler_params=_cp(),
    )(c_row, ada_w, ada_b4, *deps)


def _small_exchange(payload, c_all, wsm, msm, vsm):
    ncol = 6 * D_MODEL // N_CHIPS

    def body(p_ref, call_ref, w_ref, m_ref, v_ref, gw_ref, g_ref, d_ref, mo_ref, vo_ref, loss_ref,
             all_ref, dm_ref, send_sems, recv_sems):
        x, y, c = _place()
        k = 2 * x + y
        me = 4 * x + 2 * y + c
        all_ref[:, pl.ds(me, 1), :] = p_ref[...]
        sends = []
        for code, peer in _peers(x, y, c):
            cp = pltpu.make_async_remote_copy(
                src_ref=p_ref, dst_ref=all_ref.at[:, pl.ds(me, 1), :],
                send_sem=send_sems.at[code], recv_sem=recv_sems.at[code],
                device_id=peer, device_id_type=MESH)
            cp.start()
            sends.append(cp)
        for code, (px, py, pc) in _peers(x, y, c):
            land = all_ref.at[:, pl.ds(4 * px + 2 * py + pc, 1), :]
            pltpu.make_async_remote_copy(
                src_ref=land, dst_ref=land, send_sem=send_sems.at[code], recv_sem=recv_sems.at[code],
                device_id=(px, py, pc), device_id_type=MESH).wait_recv()
        for cp in sends:
            cp.wait_send()
        tot = [_colsum(all_ref[r]) for r in range(N_PAY)]
        loss_ref[...] = jnp.sum(tot[P_LOSS], axis=1, keepdims=True)
        g_ref[...] = jnp.zeros_like(g_ref)
        for r in range(P_LOSS):
            g_ref[r:r + 1, :] = tot[r]
        g = g_ref[...]
        d, mm, vv = _adamw_math(w_ref[...], g, m_ref[...], v_ref[...])
        d_ref[...] = d
        mo_ref[...] = mm
        vo_ref[...] = vv
        half = D_MODEL // 2
        for kk in range(N_CHIPS):
            @pl.when(k == kk)
            def _():
                r0 = 3 * (kk // 2)
                if kk % 2 == 0:
                    dm_ref[:, :D_MODEL] = all_ref[r0]
                    dm_ref[:, D_MODEL:] = all_ref[r0 + 1][:, :half]
                else:
                    dm_ref[:, :half] = all_ref[r0 + 1][:, half:]
                    dm_ref[:, half:] = all_ref[r0 + 2]
        call = call_ref[...]
        act = call * _sigmoid(call)
        gw_ref[...] = lax.dot_general(act, dm_ref[...], (((0,), (0,)), ((), ())),
                                      preferred_element_type=F32, precision=lax.Precision.HIGHEST)

    vm = pl.BlockSpec(memory_space=pltpu.VMEM)
    small = jax.ShapeDtypeStruct((16, D_MODEL), F32)
    return pl.pallas_call(
        body, name="small_exchange",
        in_specs=[vm] * 5, out_specs=[vm] * 6,
        out_shape=[jax.ShapeDtypeStruct((D_MODEL, ncol), F32), small, small, small, small,
                   jax.ShapeDtypeStruct((1, 1), F32)],
        scratch_shapes=[pltpu.VMEM((N_PAY, 8, D_MODEL), F32), pltpu.VMEM((8, ncol), F32),
                        pltpu.SemaphoreType.DMA((8,)), pltpu.SemaphoreType.DMA((8,))],
        compiler_params=_cp(),
    )(payload, c_all, wsm, msm, vsm)


def _rope_tables(pos_col, inv_freq, sign):
    def body(p_ref, f_ref, s_ref, cos_ref, sin_ref):
        ang = p_ref[...].astype(F32) * f_ref[...]
        cos_ref[...] = jnp.cos(ang)
        sin_ref[...] = jnp.sin(ang) * s_ref[...]

    tr = 512
    shp = jax.ShapeDtypeStruct((SEQ, 128), F32)
    return pl.pallas_call(
        body, name="rope_tables", grid=(SEQ // tr,),
        in_specs=[pl.BlockSpec((tr, 1), lambda i: (i, 0)), pl.BlockSpec((1, 128), lambda i: (0, 0)),
                  pl.BlockSpec((1, 128), lambda i: (0, 0))],
        out_specs=[pl.BlockSpec((tr, 128), lambda i: (i, 0))] * 2, out_shape=[shp, shp],
        compiler_params=_cp(("parallel",)),
    )(pos_col, inv_freq, sign)


def _resident(shape):
    nd = len(shape)
    return pl.BlockSpec(shape, lambda *_: (0,) * nd, pipeline_mode=pl.Buffered(1))


def _ln_proj(x, vecs, w_in4):
    tm = 256
    wc = w_in4.shape[2]

    def body(x_ref, vec_ref, w_ref, h_ref, proj_ref):
        xx = x_ref[...]
        g = _row(vec_ref, V_G1) * (1.0 + _row(vec_ref, V_SC1))
        h = (xx * _rms(xx) * g + _row(vec_ref, V_SH1)).astype(BF)
        h_ref[...] = h
        for j in range(N_CHIPS):
            proj_ref[:, j * wc:(j + 1) * wc] = _dot(h, w_ref[j]).astype(BF)

    return pl.pallas_call(
        body, name="ln_proj", grid=(SEQ // tm,),
        in_specs=[pl.BlockSpec((tm, D_MODEL), lambda i: (i, 0)), _resident((16, D_MODEL)),
                  _resident(w_in4.shape)],
        out_specs=[pl.BlockSpec((tm, D_MODEL), lambda i: (i, 0)), pl.BlockSpec((tm, D_IN), lambda i: (i, 0))],
        out_shape=[jax.ShapeDtypeStruct((SEQ, D_MODEL), BF), jax.ShapeDtypeStruct((SEQ, D_IN), BF)],
        compiler_params=_cp(("parallel",)),
    )(x, vecs, w_in4)


def _lane_first(shape):
    lane = lax.broadcasted_iota(jnp.int32, shape, 1)
    return (lane & 32) == 0


def _rot(v, cos, sin_s):
    partner = jnp.where(_lane_first(v.shape), pltpu.roll(v, 96, 1), pltpu.roll(v, 32, 1))
    return v * cos + partner * sin_s


def _rot_t(dv, cos, sin_s):
    t = dv * sin_s
    partner = jnp.where(_lane_first(dv.shape), pltpu.roll(t, 96, 1), pltpu.roll(t, 32, 1))
    return dv * cos + partner


def _ret_masks(lg):
    t = RET_BLOCK
    ii = lax.broadcasted_iota(jnp.int32, (t, t), 0)
    jj = lax.broadcasted_iota(jnp.int32, (t, t), 1)
    dist = jnp.abs(ii - jj).astype(F32)
    future = (jj >> RET_CHUNK_SHIFT) > (ii >> RET_CHUNK_SHIFT)
    mask = jnp.where(future, 0.0, jnp.exp(lg * dist))
    ti = lax.broadcasted_iota(jnp.int32, (t, 1), 0).astype(F32)
    from_start = jnp.exp(lg * (ti + 1.0))
    to_end = jnp.exp(lg * (t - 1.0 - ti))
    whole = jnp.exp(jnp.full((1, 128), lg * t, F32))
    return mask, from_start, to_end, whole


def _head_lanes(shape, hh):
    lane = lax.broadcasted_iota(jnp.int32, shape, 1)
    return (lane >> 6) == hh


def _ret_specs():
    t = RET_BLOCK
    return dict(
        q=lambda f: pl.BlockSpec((t, 512), lambda n: (f(n), C_QR // 512)),
        k=lambda f: pl.BlockSpec((t, 512), lambda n: (f(n), C_KR // 512)),
        v=lambda f: pl.BlockSpec((t, D_MODEL), lambda n: (f(n), C_VR // D_MODEL)),
        g=lambda f: pl.BlockSpec((t, D_MODEL), lambda n: (f(n), C_GR // D_MODEL)),
        tab=lambda f: pl.BlockSpec((t, 128), lambda n: (f(n), 0)),
        wide=lambda f: pl.BlockSpec((t, D_MODEL), lambda n: (f(n), 0)),
        state=lambda f: pl.BlockSpec((N_PAIRS, None, 2, 128, 128), lambda n: (0, f(n), 0, 0, 0)),
    )


def _ret_fwd(proj, cos, sin_s, gn_g, log_gamma):
    t = RET_BLOCK
    nb = SEQ // t

    def body(lg_ref, q_ref, k_ref, v_ref, g_ref, cos_ref, sin_ref, gn_ref, o_ref, retg_ref, st_ref, state):
        @pl.when(pl.program_id(0) == 0)
        def _():
            state[...] = jnp.zeros_like(state)

        cos, sn = cos_ref[...], sin_ref[...]
        for p in range(N_PAIRS):
            q = _rot(q_ref[:, 128 * p:128 * (p + 1)].astype(F32), cos, sn)
            k = _rot(k_ref[:, 128 * p:128 * (p + 1)].astype(F32), cos, sn) * QK_SCALE
            for hh in range(2):
                cols = slice(256 * p + 128 * hh, 256 * p + 128 * (hh + 1))
                lg = lg_ref[2 * p + hh]
                mask, from_start, to_end, whole = _ret_masks(lg)
                lanes = _head_lanes(q.shape, hh)
                qm = jnp.where(lanes, q, 0.0)
                km = jnp.where(lanes, k, 0.0)
                vh = v_ref[:, cols]
                sc = _dot_nt(qm.astype(BF), km.astype(BF)) * mask
                st = state[p, hh]
                st_ref[p, hh] = st
                o = _dot(sc.astype(BF), vh) + _dot((qm * from_start).astype(BF), st.astype(BF))
                state[p, hh] = whole * st + _dot_tn((km * to_end).astype(BF), vh)
                d = o - _rowmean(o)
                nh = d * lax.rsqrt(_rowmean(d * d) + EPS)
                gr = g_ref[:, cols].astype(F32)
                o_ref[:, cols] = o
                retg_ref[:, cols] = (gr * _sigmoid(gr) * nh * gn_ref[:, cols]).astype(BF)

    sp = _ret_specs()
    ident = lambda n: n
    return pl.pallas_call(
        body, name="ret_fwd", grid=(nb,),
        in_specs=[pl.BlockSpec(memory_space=pltpu.SMEM), sp["q"](ident), sp["k"](ident), sp["v"](ident),
                  sp["g"](ident), sp["tab"](ident), sp["tab"](ident), _resident((1, D_MODEL))],
        out_specs=[sp["wide"](ident), sp["wide"](ident), sp["state"](ident)],
        out_shape=[jax.ShapeDtypeStruct((SEQ, D_MODEL), F32), jax.ShapeDtypeStruct((SEQ, D_MODEL), BF),
                   jax.ShapeDtypeStruct((N_PAIRS, nb, 2, 128, 128), F32)],
        scratch_shapes=[pltpu.VMEM((N_PAIRS, 2, 128, 128), F32)],
        compiler_params=_cp(("arbitrary",)),
    )(log_gamma, proj, proj, proj, proj, cos, sin_s, gn_g)


def _ret_bwd(proj, cos, sin_s, dret, states, log_gamma):
    t = RET_BLOCK
    nb = SEQ // t

    def body(lg_ref, q_ref, k_ref, v_ref, cos_ref, sin_ref, do_ref, st_ref, dqkv_ref, dstate):
        @pl.when(pl.program_id(0) == 0)
        def _():
            dstate[...] = jnp.zeros_like(dstate)

        cos, sn = cos_ref[...], sin_ref[...]
        for p in range(N_PAIRS):
            q = _rot(q_ref[:, 128 * p:128 * (p + 1)].astype(F32), cos, sn)
            k = _rot(k_ref[:, 128 * p:128 * (p + 1)].astype(F32), cos, sn) * QK_SCALE
            dq_rot = jnp.zeros(q.shape, F32)
            dk_rot = jnp.zeros(q.shape, F32)
            for hh in range(2):
                cols = slice(256 * p + 128 * hh, 256 * p + 128 * (hh + 1))
                lg = lg_ref[2 * p + hh]
                mask, from_start, to_end, whole = _ret_masks(lg)
                lanes = _head_lanes(q.shape, hh)
                qm = jnp.where(lanes, q, 0.0)
                km = jnp.where(lanes, k, 0.0)
                qb, kb = qm.astype(BF), km.astype(BF)
                vh = v_ref[:, cols]
                do = do_ref[:, cols]
                sc = (_dot_nt(qb, kb) * mask).astype(BF)
                st = st_ref[p, hh].astype(BF)
                dst = dstate[p, hh]
                dstb = dst.astype(BF)
                k_end = (km * to_end).astype(BF)
                q_start = (qm * from_start).astype(BF)
                dqkv_ref[:, C_VR + 256 * p + 128 * hh:C_VR + 256 * p + 128 * (hh + 1)] = (
                    _dot_tn(sc, do) + _dot(k_end, dstb)).astype(BF)
                dsc = (_dot_nt(do, vh) * mask).astype(BF)
                dq_h = _dot(dsc, kb) + _dot_nt(do, st) * from_start
                dq_rot = dq_rot + jnp.where(lanes, dq_h, 0.0)
                dk_rot = dk_rot + _dot_tn(dsc, qb) + _dot_nt(vh, dstb) * to_end
                dstate[p, hh] = whole * dst + _dot_tn(q_start, do)
            dqkv_ref[:, C_QR + 128 * p:C_QR + 128 * (p + 1)] = _rot_t(dq_rot, cos, sn).astype(BF)
            dqkv_ref[:, C_KR + 128 * p:C_KR + 128 * (p + 1)] = _rot_t(dk_rot * QK_SCALE, cos, sn).astype(BF)

    sp = _ret_specs()
    rev = lambda n: nb - 1 - n
    return pl.pallas_call(
        body, name="ret_bwd", grid=(nb,),
        in_specs=[pl.BlockSpec(memory_space=pltpu.SMEM), sp["q"](rev), sp["k"](rev), sp["v"](rev),
                  sp["tab"](rev), sp["tab"](rev), sp["wide"](rev), sp["state"](rev)],
        out_specs=pl.BlockSpec((t, C_GR), lambda n: (rev(n), 0)),
        out_shape=jax.ShapeDtypeStruct((SEQ, C_GR), BF),
        scratch_shapes=[pltpu.VMEM((N_PAIRS, 2, 128, 128), F32)],
        compiler_params=_cp(("arbitrary",)),
    )(log_gamma, proj, proj, proj, cos, sin_s, dret, states)


def _stack_heads(v):
    return jnp.concatenate([jnp.where(_head_lanes(v.shape, hh), v, jnp.zeros_like(v)) for hh in range(2)], axis=0)


def _unstack_heads(v):
    t = v.shape[0] // 2
    return jnp.where(_head_lanes((t, v.shape[1]), 0), v[:t], v[t:])


def _sb_masks(t, heads):
    rr = lax.broadcasted_iota(jnp.int32, (t, t), 0)
    cc = lax.broadcasted_iota(jnp.int32, (t, t), 1)
    r2 = lax.broadcasted_iota(jnp.int32, (heads * t, t), 0) & (t - 1)
    c2 = lax.broadcasted_iota(jnp.int32, (heads * t, t), 1)
    return rr, cc, c2 < r2


def _split_dot2(v, tri):
    return _dot(v.astype(BF), tri)


def _log_one_minus_beta(z):
    return -(jnp.maximum(z, 0.0) + jnp.log(1.0 + jnp.exp(-jnp.abs(z))))


def _sb_fwd(proj):
    t, g = SB_BLOCK, SB_GROUP
    nq = SEQ // t
    rows = 2 * g * t

    def body(q_ref, k_ref, v_ref, o_ref, tot_ref, kt_ref):
        i = pl.program_id(1)

        @pl.when(i == 0)
        def _():
            for p in range(g):
                for jj in range(nq):
                    kt_ref[p, jj] = k_ref[jj * t:(jj + 1) * t, 128 * p:128 * (p + 1)].T

        q2 = [_stack_heads((q_ref[:, 128 * p:128 * (p + 1)].astype(F32) * QK_SCALE).astype(BF)) for p in range(g)]
        rr, cc, valid = _sb_masks(t, 2 * g)
        later = (rr > cc).astype(BF)

        def tile(j, carry, diagonal):
            acc, run = carry
            z = jnp.concatenate([_dot(q2[p], kt_ref[p, j]) for p in range(g)], axis=0)
            lm = _log_one_minus_beta(z)
            if diagonal:
                lm = jnp.where(valid, lm, 0.0)
            after = _split_dot2(lm, later)
            a = jnp.exp(z + lm + after + run)
            if diagonal:
                a = jnp.where(valid, a, 0.0)
            ab = a.astype(BF)
            keys = pl.ds(pl.multiple_of(j * t, t), t)
            av = jnp.concatenate([_dot(ab[2 * t * p:2 * t * (p + 1)], v_ref[keys, 128 * p:128 * (p + 1)])
                                  for p in range(g)], axis=0)
            return acc + av, run + after[:, 0:1] + lm[:, 0:1]

        carry = tile(i, (jnp.zeros((rows, 128), F32), jnp.zeros((rows, 1), F32)), True)
        acc, run = lax.fori_loop(0, i, lambda s, cr: tile(i - 1 - s, cr, False), carry)
        run = jnp.broadcast_to(run, (rows, 128))
        for p in range(g):
            o_ref[:, 128 * p:128 * (p + 1)] = _unstack_heads(acc[2 * t * p:2 * t * (p + 1)]).astype(BF)
            tot_ref[:, 128 * p:128 * (p + 1)] = _unstack_heads(run[2 * t * p:2 * t * (p + 1)])

    w = 128 * g
    return pl.pallas_call(
        body, name="sb_fwd", grid=(N_PAIRS // g, nq),
        in_specs=[pl.BlockSpec((t, w), lambda p, i: (i, C_QS // w + p)),
                  pl.BlockSpec((SEQ, w), lambda p, i: (0, C_KS // w + p)),
                  pl.BlockSpec((SEQ, w), lambda p, i: (0, C_VS // w + p))],
        out_specs=[pl.BlockSpec((t, w), lambda p, i: (i, p))] * 2,
        out_shape=[jax.ShapeDtypeStruct((SEQ, 512), BF), jax.ShapeDtypeStruct((SEQ, 512), F32)],
        scratch_shapes=[pltpu.VMEM((g, nq, 128, t), BF)],
        compiler_params=_cp(("parallel", "arbitrary")),
    )(proj, proj, proj)


def _sb_bwd(proj, dsb, tot, dep=None):
    t, g = SB_BLOCK, SB_GROUP
    nq = SEQ // t
    rows = 2 * g * t

    def body(q_ref, k_ref, v_ref, do_ref, tot_ref, dq_ref, dk_ref, dv_ref, kt_ref, vt_ref, dkt_acc, dvt_acc):
        i = pl.program_id(1)

        @pl.when(i == 0)
        def _():
            dkt_acc[...] = jnp.zeros_like(dkt_acc)
            dvt_acc[...] = jnp.zeros_like(dvt_acc)
            for p in range(g):
                for jj in range(nq):
                    kt_ref[p, jj] = k_ref[jj * t:(jj + 1) * t, 128 * p:128 * (p + 1)].T
                    vt_ref[p, jj] = v_ref[jj * t:(jj + 1) * t, 128 * p:128 * (p + 1)].T

        q2 = [_stack_heads((q_ref[:, 128 * p:128 * (p + 1)].astype(F32) * QK_SCALE).astype(BF)) for p in range(g)]
        do2 = [_stack_heads(do_ref[:, 128 * p:128 * (p + 1)]) for p in range(g)]
        q2t = [v.T for v in q2]
        do2t = [v.T for v in do2]
        tots = tot_ref[...]
        total = jnp.concatenate([tots[:, 64 * h:64 * h + 1] for h in range(2 * g)], axis=0)
        rr, cc, valid = _sb_masks(t, 2 * g)
        upto = (rr <= cc).astype(BF)
        before = (rr < cc).astype(BF)

        def part(v, p):
            return v[2 * t * p:2 * t * (p + 1)]

        def tile(j, carry, diagonal):
            dq, run_l, run_g = carry
            z = jnp.concatenate([_dot(q2[p], kt_ref[p, j]) for p in range(g)], axis=0)
            lm = _log_one_minus_beta(z)
            if diagonal:
                lm = jnp.where(valid, lm, 0.0)
            incl = _split_dot2(lm, upto)
            a = jnp.exp(z + lm + (total - (incl + run_l)))
            if diagonal:
                a = jnp.where(valid, a, 0.0)
            gg = a * jnp.concatenate([_dot(do2[p], vt_ref[p, j]) for p in range(g)], axis=0)
            excl = _split_dot2(gg, before)
            dz = gg * jnp.exp(lm) - (excl + run_g) * jnp.exp(z + lm)
            if diagonal:
                dz = jnp.where(valid, dz, 0.0)
            dzb = dz.astype(BF)
            ab = a.astype(BF)
            keys = pl.ds(pl.multiple_of(j * t, t), t)
            for p in range(g):
                dkt_acc[p, j] += _dot(q2t[p], part(dzb, p))
                dvt_acc[p, j] += _dot(do2t[p], part(ab, p))
            dq_t = jnp.concatenate([_dot(part(dzb, p), k_ref[keys, 128 * p:128 * (p + 1)]) for p in range(g)], axis=0)
            return (dq + dq_t, run_l + incl[:, t - 1:t], run_g + excl[:, t - 1:t] + gg[:, t - 1:t])

        zero = jnp.zeros((rows, 1), F32)
        carry = lax.fori_loop(0, i, lambda j, cr: tile(j, cr, False), (jnp.zeros((rows, 128), F32), zero, zero))
        dq = tile(i, carry, True)[0]
        for p in range(g):
            dq_ref[:, 128 * p:128 * (p + 1)] = (_unstack_heads(part(dq, p)) * QK_SCALE).astype(BF)

        @pl.when(i == nq - 1)
        def _():
            for p in range(g):
                for jj in range(nq):
                    dk_ref[jj * t:(jj + 1) * t, 128 * p:128 * (p + 1)] = dkt_acc[p, jj].T.astype(BF)
                    dv_ref[jj * t:(jj + 1) * t, 128 * p:128 * (p + 1)] = dvt_acc[p, jj].T.astype(BF)

    w = 128 * g
    tile_spec = pl.BlockSpec((t, w), lambda p, i: (i, p))
    col_spec = pl.BlockSpec((SEQ, w), lambda p, i: (0, p))
    shp = jax.ShapeDtypeStruct((SEQ, 512), BF)
    body, in_specs, args = _add_dep(
        body, [pl.BlockSpec((t, w), lambda p, i: (i, C_QS // w + p)),
               pl.BlockSpec((SEQ, w), lambda p, i: (0, C_KS // w + p)),
               pl.BlockSpec((SEQ, w), lambda p, i: (0, C_VS // w + p)),
               tile_spec, tile_spec],
        [proj, proj, proj, dsb, tot], dep)
    return pl.pallas_call(
        body, name="sb_bwd", grid=(N_PAIRS // g, nq),
        in_specs=in_specs,
        out_specs=[tile_spec, col_spec, col_spec],
        out_shape=[shp, shp, shp],
        scratch_shapes=[pltpu.VMEM((g, nq, 128, t), BF), pltpu.VMEM((g, nq, 128, t), BF),
                        pltpu.VMEM((g, nq, 128, t), F32), pltpu.VMEM((g, nq, 128, t), F32)],
        compiler_params=_cp(("parallel", "arbitrary")),
    )(*args)


def _mix(retg, sb, proj, w_ret, w_sb4):
    tm, tn = min(512, SEQ), 512

    def body(r_ref, s_ref, ar_ref, as_ref, wr_ref, ws_ref, mix_ref, rb_ref, sbp_ref):
        rb = _dot(r_ref[...], wr_ref[...])
        sbv = s_ref[...]
        sbp = jnp.concatenate([_dot(sbv, ws_ref[0]), _dot(sbv, ws_ref[1])], axis=1)
        mix = _sigmoid(ar_ref[...].astype(F32)) * rb + _sigmoid(as_ref[...].astype(F32)) * sbp
        mix_ref[...] = mix.astype(BF)
        rb_ref[...] = rb.astype(BF)
        sbp_ref[...] = sbp.astype(BF)

    out = pl.BlockSpec((tm, tn), lambda j, i: (i, j))
    shp = jax.ShapeDtypeStruct((SEQ, D_MODEL), BF)
    return pl.pallas_call(
        body, name="mix", grid=(D_MODEL // tn, SEQ // tm),
        in_specs=[pl.BlockSpec((tm, D_MODEL), lambda j, i: (i, 0)),
                  pl.BlockSpec((tm, 512), lambda j, i: (i, 0)),
                  pl.BlockSpec((tm, tn), lambda j, i: (i, C_AR // tn + j)),
                  pl.BlockSpec((tm, tn), lambda j, i: (i, C_AS // tn + j)),
                  pl.BlockSpec((D_MODEL, tn), lambda j, i: (0, j)),
                  pl.BlockSpec((2, 512, 256), lambda j, i: (j, 0, 0))],
        out_specs=[out, out, out], out_shape=[shp, shp, shp],
        compiler_params=_cp(("parallel", "parallel")),
    )(retg, sb, proj, proj, w_ret, w_sb4)


def _out_proj(mixed, x, vecs, w_out):
    tm = min(512, SEQ)

    def body(m_ref, x_ref, vec_ref, w_ref, y_ref, h1_ref, h2_ref):
        y = _dot(m_ref[...], w_ref[...])
        h1 = x_ref[...] + _row(vec_ref, V_GT1) * (y * _rms(y)) * _row(vec_ref, V_G2)
        g = _row(vec_ref, V_G3) * (1.0 + _row(vec_ref, V_SC2))
        y_ref[...] = y
        h1_ref[...] = h1
        h2_ref[...] = (h1 * _rms(h1) * g + _row(vec_ref, V_SH2)).astype(BF)

    row = pl.BlockSpec((tm, D_MODEL), lambda i: (i, 0))
    f32 = jax.ShapeDtypeStruct((SEQ, D_MODEL), F32)
    return pl.pallas_call(
        body, name="out_proj", grid=(SEQ // tm,),
        in_specs=[row, row, pl.BlockSpec((16, D_MODEL), lambda i: (0, 0)),
                  pl.BlockSpec((D_MODEL, D_MODEL), lambda i: (0, 0))],
        out_specs=[row, row, row],
        out_shape=[f32, f32, jax.ShapeDtypeStruct((SEQ, D_MODEL), BF)],
        compiler_params=_cp(("parallel",)),
    )(mixed, x, vecs, w_out)


def _ffn_fwd_loss(h2, h1, target, vecs, w_ff14, w_ff2):
    tm = 256

    def body(h2_ref, h1_ref, t_ref, vec_ref, w1_ref, w2_ref, u_ref, a_ref, dout_ref, df_ref, st_ref):
        @pl.when(pl.program_id(0) == 0)
        def _():
            st_ref[...] = jnp.zeros_like(st_ref)

        hb = h2_ref[...]
        f = jnp.zeros((tm, D_MODEL), F32)
        for k in range(N_CHIPS):
            cols = slice(k * D_MODEL, (k + 1) * D_MODEL)
            u = _dot(hb, w1_ref[k])
            r = jnp.maximum(u, 0.0)
            act = (r * r).astype(BF)
            u_ref[:, cols] = u.astype(BF)
            a_ref[:, cols] = act
            f = f + _dot(act, w2_ref[cols, :])
        r4 = _rms(f)
        fn = f * r4
        gt2, g4 = _row(vec_ref, V_GT2), _row(vec_ref, V_G4)
        diff = h1_ref[...] + gt2 * fn * g4 - t_ref[...]
        dout = diff * (1.0 / D_MODEL)
        dfn = dout * gt2 * g4
        dout_ref[...] = dout
        df_ref[...] = (r4 * (dfn - fn * _rowmean(dfn * fn))).astype(BF)
        st_ref[0:1, :] += _colsum(dout * fn * g4)
        st_ref[1:2, :] += _colsum(dout * gt2 * fn)
        st_ref[2:3, :] += _colsum(diff * diff) * (0.5 / D_MODEL)

    row = pl.BlockSpec((tm, D_MODEL), lambda i: (i, 0))
    wide = pl.BlockSpec((tm, D_FF), lambda i: (i, 0))
    return pl.pallas_call(
        body, name="ffn_fwd_loss", grid=(SEQ // tm,),
        in_specs=[row, row, row, _resident((16, D_MODEL)), _resident(w_ff14.shape), _resident(w_ff2.shape)],
        out_specs=[wide, wide, row, row, pl.BlockSpec((8, D_MODEL), lambda i: (0, 0))],
        out_shape=[jax.ShapeDtypeStruct((SEQ, D_FF), BF), jax.ShapeDtypeStruct((SEQ, D_FF), BF),
                   jax.ShapeDtypeStruct((SEQ, D_MODEL), F32), jax.ShapeDtypeStruct((SEQ, D_MODEL), BF),
                   jax.ShapeDtypeStruct((8, D_MODEL), F32)],
        compiler_params=_cp(("arbitrary",)),
    )(h2, h1, target, vecs, w_ff14, w_ff2)


def _ffn_bwd(df, u, h1, y, dout, vecs, w_ff14, w_ff2):
    tm = 256

    def body(df_ref, u_ref, h1_ref, y_ref, dout_ref, vec_ref, w1_ref, w2_ref, du_ref, dh1_ref, dy_ref, st_ref):
        @pl.when(pl.program_id(0) == 0)
        def _():
            st_ref[...] = jnp.zeros_like(st_ref)

        dfb = df_ref[...]
        dh2 = jnp.zeros((tm, D_MODEL), F32)
        for k in range(N_CHIPS):
            cols = slice(k * D_MODEL, (k + 1) * D_MODEL)
            da = _dot_nt(dfb, w2_ref[cols, :])
            du = (da * (2.0 * jnp.maximum(u_ref[:, cols].astype(F32), 0.0))).astype(BF)
            du_ref[:, cols] = du
            dh2 = dh2 + _dot_nt(du, w1_ref[k])
        h1 = h1_ref[...]
        r3 = _rms(h1)
        hn3 = h1 * r3
        g3, sc2 = _row(vec_ref, V_G3), _row(vec_ref, V_SC2)
        dhn3 = dh2 * g3 * (1.0 + sc2)
        dh1 = dout_ref[...] + r3 * (dhn3 - hn3 * _rowmean(dhn3 * hn3))
        y = y_ref[...]
        r2 = _rms(y)
        yn = y * r2
        gt1, g2 = _row(vec_ref, V_GT1), _row(vec_ref, V_G2)
        dyn = dh1 * gt1 * g2
        dh1_ref[...] = dh1
        dy_ref[...] = (r2 * (dyn - yn * _rowmean(dyn * yn))).astype(BF)
        st_ref[0:1, :] += _colsum(dh2)
        st_ref[1:2, :] += _colsum(dh2 * hn3 * g3)
        st_ref[2:3, :] += _colsum(dh2 * hn3 * (1.0 + sc2))
        st_ref[3:4, :] += _colsum(dh1 * yn * g2)
        st_ref[4:5, :] += _colsum(dh1 * gt1 * yn)

    row = pl.BlockSpec((tm, D_MODEL), lambda i: (i, 0))
    wide = pl.BlockSpec((tm, D_FF), lambda i: (i, 0))
    return pl.pallas_call(
        body, name="ffn_bwd", grid=(SEQ // tm,),
        in_specs=[row, wide, row, row, row, _resident((16, D_MODEL)), _resident(w_ff14.shape),
                  _resident(w_ff2.shape)],
        out_specs=[wide, row, row, pl.BlockSpec((8, D_MODEL), lambda i: (0, 0))],
        out_shape=[jax.ShapeDtypeStruct((SEQ, D_FF), BF), jax.ShapeDtypeStruct((SEQ, D_MODEL), F32),
                   jax.ShapeDtypeStruct((SEQ, D_MODEL), BF), jax.ShapeDtypeStruct((8, D_MODEL), F32)],
        compiler_params=_cp(("arbitrary",)),
    )(df, u, h1, y, dout, vecs, w_ff14, w_ff2)


def _mix_bwd(dy, proj, rb, sbp, w_out, w_sb4):
    tm, half = 256, 512

    def body(dy_ref, ar0, ar1, as0, as1, rb_ref, sbp_ref, wo_ref, ws_ref, drb_ref, dsbp_ref, da_ref, dsb_ref):
        dm_all = _dot_nt(dy_ref[...], wo_ref[...])
        dsb = jnp.zeros((tm, 512), F32)
        for hf, (ar_ref, as_ref) in enumerate(((ar0, as0), (ar1, as1))):
            cols = slice(half * hf, half * (hf + 1))
            dm = dm_all[:, cols]
            sr = _sigmoid(ar_ref[...].astype(F32))
            ss = _sigmoid(as_ref[...].astype(F32))
            dsbp = (dm * ss).astype(BF)
            drb_ref[:, cols] = (dm * sr).astype(BF)
            dsbp_ref[:, cols] = dsbp
            da_ref[:, cols] = (dm * rb_ref[:, cols].astype(F32) * sr * (1.0 - sr)).astype(BF)
            da_ref[:, D_MODEL + half * hf:D_MODEL + half * (hf + 1)] = (
                dm * sbp_ref[:, cols].astype(F32) * ss * (1.0 - ss)).astype(BF)
            dsb = dsb + _dot_nt(dsbp[:, :256], ws_ref[2 * hf]) + _dot_nt(dsbp[:, 256:], ws_ref[2 * hf + 1])
        dsb_ref[...] = dsb.astype(BF)

    row = pl.BlockSpec((tm, D_MODEL), lambda i: (i, 0))
    gate = lambda c0: pl.BlockSpec((tm, half), lambda i: (i, c0 // half))
    shp = jax.ShapeDtypeStruct((SEQ, D_MODEL), BF)
    return pl.pallas_call(
        body, name="mix_bwd", grid=(SEQ // tm,),
        in_specs=[row, gate(C_AR), gate(C_AR + half), gate(C_AS), gate(C_AS + half), row, row,
                  _resident((D_MODEL, D_MODEL)), _resident(w_sb4.shape)],
        out_specs=[row, row, pl.BlockSpec((tm, 2 * D_MODEL), lambda i: (i, 0)), pl.BlockSpec((tm, 512), lambda i: (i, 0))],
        out_shape=[shp, shp, jax.ShapeDtypeStruct((SEQ, 2 * D_MODEL), BF), jax.ShapeDtypeStruct((SEQ, 512), BF)],
        compiler_params=_cp(("parallel",)),
    )(dy, proj, proj, proj, proj, rb, sbp, w_out, w_sb4)


def _ret_branch_bwd(drb, proj, o_raw, gn_g, w_ret):
    tm, tn = min(512, SEQ), 512

    def body(d_ref, g_ref, o_ref, gn_ref, w_ref, dret_ref, dgr_ref, st_ref):
        @pl.when(pl.program_id(1) == 0)
        def _():
            st_ref[...] = jnp.zeros_like(st_ref)

        dretg = _dot_nt(d_ref[...], w_ref[...])
        for gi in range(tn // 128):
            cols = slice(128 * gi, 128 * (gi + 1))
            o = o_ref[:, cols]
            d = o - _rowmean(o)
            rstd = lax.rsqrt(_rowmean(d * d) + EPS)
            nh = d * rstd
            gain = gn_ref[:, cols]
            gr = g_ref[:, cols].astype(F32)
            sg = _sigmoid(gr)
            dg = dretg[:, cols]
            dgn = dg * gr * sg
            dnh = dgn * gain
            dgr_ref[:, cols] = (dg * nh * gain * sg * (1.0 + gr * (1.0 - sg))).astype(BF)
            dret_ref[:, cols] = (rstd * (dnh - _rowmean(dnh) - nh * _rowmean(dnh * nh))).astype(BF)
            st_ref[0:1, cols] += _colsum(dgn * nh)

    tile = pl.BlockSpec((tm, tn), lambda j, i: (i, j))
    shp = jax.ShapeDtypeStruct((SEQ, D_MODEL), BF)
    return pl.pallas_call(
        body, name="ret_branch_bwd", grid=(D_MODEL // tn, SEQ // tm),
        in_specs=[pl.BlockSpec((tm, D_MODEL), lambda j, i: (i, 0)),
                  pl.BlockSpec((tm, tn), lambda j, i: (i, C_GR // tn + j)),
                  tile, pl.BlockSpec((1, tn), lambda j, i: (0, j)),
                  pl.BlockSpec((tn, D_MODEL), lambda j, i: (j, 0))],
        out_specs=[tile, tile, pl.BlockSpec((8, tn), lambda j, i: (0, j))],
        out_shape=[shp, shp, jax.ShapeDtypeStruct((8, D_MODEL), F32)],
        compiler_params=_cp(("parallel", "arbitrary")),
    )(drb, proj, o_raw, gn_g, w_ret)


def _dproj_segments(widths):
    wc = D_IN // N_CHIPS
    segs, start = [], 0
    for pi, width in enumerate(widths):
        lo = start
        while lo < start + width:
            j = lo // wc
            hi = min(start + width, (j + 1) * wc)
            segs.append((j, lo - j * wc, pi, lo - start, hi - lo))
            lo = hi
        start += width
    assert start == D_IN
    return segs


def _in_proj_bwd(pieces, x, dh1, vecs, w_in4):
    tm = 256
    n = len(pieces)
    segs = _dproj_segments([p.shape[1] for p in pieces])

    def body(*refs):
        x_ref, dh1_ref, vec_ref, w_ref, dx_ref, st_ref = refs[n:]

        @pl.when(pl.program_id(0) == 0)
        def _():
            st_ref[...] = jnp.zeros_like(st_ref)

        dh = jnp.zeros((tm, D_MODEL), F32)
        for j, so, pi, po, width in segs:
            dh = dh + _dot_nt(refs[pi][:, po:po + width], w_ref[j, :, so:so + width])
        xx = x_ref[...]
        r1 = _rms(xx)
        xn = xx * r1
        g1, sc1 = _row(vec_ref, V_G1), _row(vec_ref, V_SC1)
        dxn = dh * g1 * (1.0 + sc1)
        dx_ref[...] = dh1_ref[...] + r1 * (dxn - xn * _rowmean(dxn * xn))
        st_ref[0:1, :] += _colsum(dh)
        st_ref[1:2, :] += _colsum(dh * xn * g1)
        st_ref[2:3, :] += _colsum(dh * xn * (1.0 + sc1))

    row = pl.BlockSpec((tm, D_MODEL), lambda i: (i, 0))
    return pl.pallas_call(
        body, name="in_proj_bwd", grid=(SEQ // tm,),
        in_specs=[pl.BlockSpec((tm, p.shape[1]), lambda i: (i, 0)) for p in pieces] + [
            row, row, _resident((16, D_MODEL)), _resident(w_in4.shape)],
        out_specs=[row, pl.BlockSpec((8, D_MODEL), lambda i: (0, 0))],
        out_shape=[jax.ShapeDtypeStruct((SEQ, D_MODEL), F32), jax.ShapeDtypeStruct((8, D_MODEL), F32)],
        compiler_params=_cp(("arbitrary",)),
    )(*pieces, x, dh1, vecs, w_in4)


def _grad_w_in(h, pieces, dep=None):
    ta = 256
    n = len(pieces)
    segs = _dproj_segments([p.shape[1] for p in pieces])

    def body(*refs):
        h_ref, o_ref = refs[n], refs[n + 1]
        hh = h_ref[...]
        for j, so, pi, po, width in segs:
            o_ref[j, :, so:so + width] = _dot_tn(hh, refs[pi][:, po:po + width])

    body, in_specs, args = _add_dep(
        body, [_resident(p.shape) for p in pieces] + [pl.BlockSpec((SEQ, ta), lambda i: (0, i))],
        list(pieces) + [h], dep)
    return pl.pallas_call(
        body, name="grad_w_in", grid=(D_MODEL // ta,),
        in_specs=in_specs,
        out_specs=pl.BlockSpec((N_CHIPS, ta, D_IN // N_CHIPS), lambda i: (0, i, 0)),
        out_shape=jax.ShapeDtypeStruct((N_CHIPS, D_MODEL, D_IN // N_CHIPS), F32),
        compiler_params=_cp(("parallel",)),
    )(*args)


def _weight_grad(a, b, ta, tb, col_sharded, name, dep=None):
    ka, nb_ = a.shape[1], b.shape[1]

    def body(a_ref, b_ref, o_ref):
        o_ref[...] = _dot_tn(a_ref[...], b_ref[...])

    body, in_specs, args = _add_dep(
        body, [pl.BlockSpec((SEQ, ta), lambda i, j: (0, i)), pl.BlockSpec((SEQ, tb), lambda i, j: (0, j))],
        [a, b], dep)

    if col_sharded:
        per = nb_ // N_CHIPS // tb
        out_shape = jax.ShapeDtypeStruct((N_CHIPS, ka, nb_ // N_CHIPS), F32)
        out_spec = pl.BlockSpec((None, ta, tb), lambda i, j: (j // per, i, j % per))
    else:
        per = ka // N_CHIPS // ta
        out_shape = jax.ShapeDtypeStruct((N_CHIPS, ka // N_CHIPS, nb_), F32)
        out_spec = pl.BlockSpec((None, ta, tb), lambda i, j: (i // per, i % per, j))
    return pl.pallas_call(
        body, name=name, grid=(ka // ta, nb_ // tb),
        in_specs=in_specs, out_specs=out_spec, out_shape=out_shape,
        compiler_params=_cp(("parallel", "parallel")),
    )(*args)


def _rope_constants():
    freq = np.float32(ROPE_BASE) ** (-np.arange(0, 64, 2, dtype=np.float32) / np.float32(64))
    inv = np.tile(freq.astype(np.float32), 4).reshape(1, 128)
    sign = np.tile(np.concatenate([-np.ones(32, np.float32), np.ones(32, np.float32)]), 2).reshape(1, 128)
    return jnp.asarray(inv), jnp.asarray(sign)


def _log_gamma():
    return jnp.asarray(np.log1p(-(2.0 ** (-5.0 - np.arange(8, dtype=np.float64)))).astype(np.float32))


def _halves(g):
    return g.reshape(N_CHIPS, 2, g.shape[1] // 2, g.shape[2])


def kernel(x, c, positions, ada_w, ada_b, pre_mix_g, post_mix_g, pre_ffn_g, post_ffn_g, w_in, ret_gn_g, w_ret_branch, w_sb_branch, w_out, w_ff1, w_ff2, loss_target, m_ada_w, m_ada_b, m_pre_mix_g, m_post_mix_g, m_pre_ffn_g, m_post_ffn_g, m_w_in, m_ret_gn_g, m_w_ret_branch, m_w_sb_branch, m_w_out, m_w_ff1, m_w_ff2, v_ada_w, v_ada_b, v_pre_mix_g, v_post_mix_g, v_pre_ffn_g, v_post_ffn_g, v_w_in, v_ret_gn_g, v_w_ret_branch, v_w_sb_branch, v_w_out, v_w_ff1, v_w_ff2):
    names = ["w_in", "w_ret", "w_sb", "w_out", "w_ff1", "w_ff2"]
    big = dict(zip(names, [w_in, w_ret_branch, w_sb_branch, w_out, w_ff1, w_ff2]))
    big_m = dict(zip(names, [m_w_in, m_w_ret_branch, m_w_sb_branch, m_w_out, m_w_ff1, m_w_ff2]))
    big_v = dict(zip(names, [v_w_in, v_w_ret_branch, v_w_sb_branch, v_w_out, v_w_ff1, v_w_ff2]))
    rest = names[1:]
    cidx = lax.axis_index("c").astype(jnp.int32).reshape(1)
    kidx = (2 * lax.axis_index("x") + lax.axis_index("y")).astype(jnp.int32).reshape(1)
    x0, target = x[0], loss_target[0]

    buf_in, sem_in, tok_in = _gather_start("gather_in_start", [_cast_bf16(w_in[0], kidx, c, "cast_w_in")])
    rest_bufs = [_cast_bf16(big[nm][0], kidx, tok_in, "cast_" + nm) for nm in rest]
    inv_freq, sign = _rope_constants()
    lg = _log_gamma()
    cos, sin_s = _rope_tables(positions.reshape(SEQ, 1), _tie(tok_in, inv_freq), sign)
    c_all, mod4 = _mod_exchange(c, ada_w[0], ada_b.reshape(N_CHIPS, -1), rest_bufs + [cos])
    vecs = jnp.concatenate([mod4.reshape(6, D_MODEL), pre_mix_g, post_mix_g, pre_ffn_g, post_ffn_g,
                            jnp.zeros((6, D_MODEL), F32)], axis=0)
    buf_in, sem_in, tok_in = _gather_pass("gather_in_pass", buf_in, sem_in, vecs)
    buf_rest, sem_rest, tok_rest = _gather_start("gather_rest_start", rest_bufs, after=tok_in)
    (w_in4,) = _gather_finish("gather_in_finish", buf_in, sem_in, tok_rest)

    h, proj = _ln_proj(x0, vecs, w_in4)
    sb, tot = _sb_fwd(proj)
    buf_rest, sem_rest, tok_rest = _gather_pass("gather_rest_pass", buf_rest, sem_rest, sb)
    o_raw, retg, states = _ret_fwd(proj, cos, sin_s, _tie(tok_rest, ret_gn_g), lg)
    w_ret4, w_sb4, w_out4, w_ff14, w_ff24 = _gather_finish("gather_rest_finish", buf_rest, sem_rest, retg)
    w_ret = w_ret4.reshape(D_MODEL, D_MODEL)
    w_out2 = w_out4.reshape(D_MODEL, D_MODEL)
    w_ff2_2 = w_ff24.reshape(D_FF, D_MODEL)
    mixed, rb, sbp = _mix(retg, sb, proj, w_ret, w_sb4)
    y, h1, h2 = _out_proj(mixed, x0, vecs, w_out2)
    u, act, dout, df, st_a = _ffn_fwd_loss(h2, h1, target, vecs, w_ff14, w_ff2_2)

    du, dh1, dy, st_b = _ffn_bwd(df, u, h1, y, dout, vecs, w_ff14, w_ff2_2)
    grads = {"w_ff2": _weight_grad(act, df, 512, 1024, False, "grad_w_ff2")}
    grads["w_ff1"] = _weight_grad(h2, du, 512, 1024, True, "grad_w_ff1")
    drb, dsbp, da, dsb = _mix_bwd(dy, proj, rb, sbp, w_out2, w_sb4)
    grads["w_out"] = _weight_grad(mixed, dy, 256, 1024, False, "grad_w_out")
    dret, dg_r, st_c = _ret_branch_bwd(drb, proj, o_raw, ret_gn_g, w_ret)
    grads["w_ret"] = _weight_grad(retg, drb, 256, 1024, False, "grad_w_ret")
    grads["w_sb"] = _weight_grad(sb, dsbp, 512, 256, True, "grad_w_sb")

    bufs, sems, tok = _pair_send_start("rs_rest_pair_send", [_halves(grads[nm]) for nm in rest])
    dqkv_r = _ret_bwd(proj, cos, sin_s, dret, states, _tie(tok, lg))
    mine, theirs = _pair_send_wait("rs_rest_pair_recv", bufs, sems, dqkv_r)
    pair_sums = [_pair_add(g, r, cidx, "pair_add_" + nm) for g, r, nm in zip(mine, theirs, rest)]
    bufs, sems, tok = _chip_send_start("rs_rest_chip_send", pair_sums)
    dq_s, dk_s, dv_s = _sb_bwd(proj, dsb, tot, dep=tok)
    own, parts = _chip_send_wait("rs_rest_chip_recv", bufs, sems, dq_s)
    sums = [_chip_add(o, p, kidx, cidx, "chip_add_" + nm) for o, p, nm in zip(own, parts, rest)]
    bufs, sems, tok = _pair_swap_start("rs_rest_pair_swap", sums)
    dproj = [dqkv_r, dg_r, dq_s, dk_s, dv_s, da]
    g_in = _grad_w_in(h, dproj, dep=tok)
    full_rest = _pair_swap_wait("rs_rest_pair_swapped", bufs, sems, g_in)

    bufs, sems, tok = _pair_send_start("rs_in_pair_send", [_halves(g_in)])
    dx, st_d = _in_proj_bwd(dproj, x0, dh1, _tie(tok, vecs), w_in4)
    mine, theirs = _pair_send_wait("rs_in_pair_recv", bufs, sems, dx)
    bufs, sems, tok = _chip_send_start("rs_in_chip_send", [_pair_add(mine[0], theirs[0], cidx, "pair_add_w_in")])
    out = {}
    for nm, g in zip(rest, full_rest):
        w = big[nm][0]
        out[nm] = _adamw(w, big_m[nm][0], big_v[nm][0], g.reshape(w.shape), "adamw_" + nm, dep=tok)

    payload = jnp.concatenate([
        st_d[0:2], st_b[3:4], st_b[0:2], st_a[0:1],
        st_d[2:3], st_b[4:5], st_b[2:3], st_a[1:2],
        st_c[0:1], st_a[2:3]], axis=0)

    def table(b6, g5):
        return jnp.concatenate([b6.reshape(6, D_MODEL)] + g5 + [jnp.zeros((5, D_MODEL), F32)], axis=0)

    wsm = table(ada_b, [pre_mix_g, post_mix_g, pre_ffn_g, post_ffn_g, ret_gn_g])
    msm = table(m_ada_b, [m_pre_mix_g, m_post_mix_g, m_pre_ffn_g, m_post_ffn_g, m_ret_gn_g])
    vsm = table(v_ada_b, [v_pre_mix_g, v_post_mix_g, v_pre_ffn_g, v_post_ffn_g, v_ret_gn_g])
    g_ada, gsm, dsm, mosm, vosm, loss = _small_exchange(
        payload.reshape(N_PAY, 1, D_MODEL), c_all, wsm, msm, vsm)
    ada_out = _adamw(ada_w[0], m_ada_w[0], v_ada_w[0], g_ada, "adamw_ada_w")

    own, parts = _chip_send_wait("rs_in_chip_recv", bufs, sems, ada_out[1])
    bufs, sems, tok = _pair_swap_start(
        "rs_in_pair_swap", [_chip_add(own[0], parts[0], kidx, cidx, "chip_add_w_in")])
    (full_in,) = _pair_swap_wait("rs_in_pair_swapped", bufs, sems, tok)
    out["w_in"] = _adamw(w_in[0], m_w_in[0], v_w_in[0], full_in.reshape(w_in.shape[1:]), "adamw_w_in")

    def unpack(tab):
        return [tab[0:6].reshape(1, 6 * D_MODEL)] + [tab[6 + r:7 + r] for r in range(5)]

    def ordered(which):
        sm = unpack([gsm, dsm, mosm, vosm][which])
        bg = [out[nm][which][None] for nm in names]
        return [ada_out[which][None], sm[0], sm[1], sm[2], sm[3], sm[4], bg[0], sm[5]] + bg[1:]

    return (loss.reshape(()), dx[None], *ordered(0), *ordered(1), *ordered(2), *ordered(3))
```

```python
import functools

import numpy as np
import jax
import jax.numpy as jnp
from jax import lax
from jax.experimental import pallas as pl
from jax.experimental.pallas import tpu as pltpu

SEQ = 2048
D_MODEL = 1024
D_IN = 6656
D_FF = 4096
N_CHIPS = 4
EPS = 1e-6
ROPE_BASE = 10000.0
RET_BLOCK = 256
RET_CHUNK_SHIFT = 6
SB_BLOCK = 256
QK_SCALE = 0.125
N_PAIRS = 4
SB_GROUP = 2

ADAM_LR = 0.001
ADAM_B1 = 0.9
ADAM_B2 = 0.999
ADAM_EPS = 1e-08
ADAM_WD = 0.01
ADAM_STEP = 10

BF = jnp.bfloat16
F32 = jnp.float32
MESH = pl.DeviceIdType.MESH
VMEM_LIMIT = 56 * 1024 * 1024
ANY = pl.BlockSpec(memory_space=pl.ANY)

C_QR, C_KR, C_VR, C_GR, C_QS, C_KS, C_VS, C_AR, C_AS = 0, 512, 1024, 2048, 3072, 3584, 4096, 4608, 5632

V_SH1, V_SC1, V_GT1, V_SH2, V_SC2, V_GT2, V_G1, V_G2, V_G3, V_G4 = range(10)
P_DSH1, P_DSC1, P_DGT1, P_DSH2, P_DSC2, P_DGT2, P_DG1, P_DG2, P_DG3, P_DG4, P_DGN, P_LOSS = range(12)
N_PAY = 12


def _cp(sem=None, **kw):
    if sem is not None:
        kw["dimension_semantics"] = sem
    return pltpu.CompilerParams(vmem_limit_bytes=VMEM_LIMIT, **kw)


def _dot(a, b):
    return jnp.dot(a, b, preferred_element_type=F32)


def _dot_nt(a, b):
    return lax.dot_general(a, b, (((1,), (1,)), ((), ())), preferred_element_type=F32)


def _dot_tn(a, b):
    return lax.dot_general(a, b, (((0,), (0,)), ((), ())), preferred_element_type=F32)


def _row(ref, i):
    return ref[i:i + 1, :]


def _rms(v):
    return lax.rsqrt(jnp.mean(v * v, axis=1, keepdims=True) + EPS)


def _colsum(v):
    return jnp.sum(v, axis=0, keepdims=True)


def _rowmean(v):
    return jnp.mean(v, axis=1, keepdims=True)


def _sigmoid(v):
    return 1.0 / (1.0 + jnp.exp(-v))


def _cast_bf16(w, kidx, dep, name):
    rows, cols = w.shape
    tr = min(rows, 512)

    def body(k_ref, w_ref, dep_ref, o_ref):
        o_ref[...] = w_ref[...].astype(BF)

    return pl.pallas_call(
        body, name=name,
        grid_spec=pltpu.PrefetchScalarGridSpec(
            num_scalar_prefetch=1, grid=(rows // tr,),
            in_specs=[pl.BlockSpec((tr, cols), lambda i, k_ref: (i, 0)), ANY],
            out_specs=pl.BlockSpec((None, tr, cols), lambda i, k_ref: (k_ref[0], i, 0))),
        out_shape=jax.ShapeDtypeStruct((N_CHIPS, rows, cols), BF),
        compiler_params=_cp(("parallel",)),
    )(kidx, w, dep)


def _adamw_math(w, g, m, v):
    m = ADAM_B1 * m + (1.0 - ADAM_B1) * g
    v = ADAM_B2 * v + (1.0 - ADAM_B2) * (g * g)
    m_hat = m / (1.0 - ADAM_B1 ** ADAM_STEP)
    v_hat = v / (1.0 - ADAM_B2 ** ADAM_STEP)
    delta = -ADAM_LR * (m_hat / (jnp.sqrt(v_hat) + ADAM_EPS) + ADAM_WD * w)
    return delta, m, v


def _adamw(w, m, v, g, name, dep=None):
    rows, cols = w.shape
    tr = min(rows, 256)

    def body(w_ref, m_ref, v_ref, g_ref, go_ref, d_ref, mo_ref, vo_ref):
        gg = g_ref[...]
        d, mm, vv = _adamw_math(w_ref[...], gg, m_ref[...], v_ref[...])
        go_ref[...] = gg
        d_ref[...] = d
        mo_ref[...] = mm
        vo_ref[...] = vv

    spec = pl.BlockSpec((tr, cols), lambda i: (i, 0))
    shp = jax.ShapeDtypeStruct((rows, cols), F32)
    body, in_specs, args = _add_dep(body, [spec] * 4, [w, m, v, g], dep)
    return pl.pallas_call(
        body, name=name, grid=(rows // tr,),
        in_specs=in_specs, out_specs=[spec] * 4, out_shape=[shp] * 4,
        compiler_params=_cp(("parallel",)),
    )(*args)


def _place():
    x, y, c = lax.axis_index("x"), lax.axis_index("y"), lax.axis_index("c")
    return x, y, c


HBM = pl.BlockSpec(memory_space=pltpu.HBM)
SEM = pl.BlockSpec(memory_space=pltpu.SEMAPHORE)
EFFECT = pltpu.SideEffectType.DATAFLOW_SIDE_EFFECTING


def _tie(token, small):
    return small + token[0, 0]


def _add_dep(body, in_specs, args, dep):
    if dep is None:
        return body, list(in_specs), list(args)
    n = len(args)

    def wrapped(*refs):
        body(*refs[:n], *refs[n + 1:])

    return wrapped, list(in_specs) + [ANY], list(args) + [dep]


def _split_call(name, bufs, run, old=None, after=None, new=0):
    nb = len(bufs)
    n_old = 2 if old is not None else 0
    n_in = nb + n_old + (1 if after is not None else 0)

    def body(*refs):
        old_sems = (refs[nb], refs[nb + 1]) if old is not None else None
        new_sems = (refs[n_in], refs[n_in + 1]) if new else None
        run(refs[:nb], old_sems, new_sems)
        if new:
            refs[-1][...] = jnp.zeros_like(refs[-1])

    in_specs = [HBM] * nb + [SEM] * n_old + ([ANY] if after is not None else [])
    out_shape = [pltpu.SemaphoreType.DMA((new,))] * 2 if new else []
    out_specs = [SEM, SEM] if new else []
    out_shape += [pltpu.HBM(b.shape, b.dtype) for b in bufs]
    out_specs += [HBM] * nb
    if new:
        out_shape.append(jax.ShapeDtypeStruct((8, 128), F32))
        out_specs.append(pl.BlockSpec(memory_space=pltpu.VMEM))
    first = 2 if new else 0
    args = [pltpu.with_memory_space_constraint(b, pltpu.HBM) for b in bufs]
    if old is not None:
        args += [old[0], old[1]]
    if after is not None:
        args.append(after)
    outs = pl.pallas_call(
        body, name=name, in_specs=tuple(in_specs), out_specs=tuple(out_specs), out_shape=tuple(out_shape),
        input_output_aliases={i: i + first for i in range(nb)},
        compiler_params=pltpu.CompilerParams(has_side_effects=EFFECT),
    )(*args)
    thru = list(outs[first:first + nb])
    if new:
        return thru, (outs[0], outs[1]), outs[-1]
    return thru, None, None


def _remote(part_src, part_dst, sems, i, to):
    return pltpu.make_async_remote_copy(src_ref=part_src, dst_ref=part_dst, send_sem=sems[0].at[i],
                                        recv_sem=sems[1].at[i], device_id=to, device_id_type=MESH)


def _other_chips(x, y):
    return [(1 - x, y), (x, 1 - y), (1 - x, 1 - y)]


def _gather_start(name, bufs, after=None):
    def run(refs, old, new):
        x, y, c = _place()
        k = 2 * x + y
        for w, ref in enumerate(refs):
            rh = bufs[w].shape[1] // 2
            part = ref.at[k, pl.ds(c * rh, rh)]
            for j, (cx, cy) in enumerate(_other_chips(x, y)):
                _remote(part, part, new, 3 * w + j, (cx, cy, c)).start()

    return _split_call(name, bufs, run, after=after, new=3 * len(bufs))


def _gather_pass(name, bufs, sems, after):
    def run(refs, old, new):
        x, y, c = _place()
        k = 2 * x + y
        sib = (x, y, 1 - c)
        for w, ref in enumerate(refs):
            rh = bufs[w].shape[1] // 2
            for j, (cx, cy) in enumerate(_other_chips(x, y)):
                land = ref.at[2 * cx + cy, pl.ds(c * rh, rh)]
                _remote(land, land, old, 3 * w + j, (cx, cy, c)).wait_recv()
                _remote(land, land, new, 3 * w + j, sib).start()
        for w, ref in enumerate(refs):
            rh = bufs[w].shape[1] // 2
            part = ref.at[k, pl.ds(c * rh, rh)]
            for j, (cx, cy) in enumerate(_other_chips(x, y)):
                _remote(part, part, old, 3 * w + j, (cx, cy, c)).wait_send()

    return _split_call(name, bufs, run, old=sems, after=after, new=3 * len(bufs))


def _gather_finish(name, bufs, sems, after):
    def run(refs, old, new):
        x, y, c = _place()
        sib = (x, y, 1 - c)
        for w, ref in enumerate(refs):
            rh = bufs[w].shape[1] // 2
            for j, (cx, cy) in enumerate(_other_chips(x, y)):
                sent = ref.at[2 * cx + cy, pl.ds(c * rh, rh)]
                _remote(sent, sent, old, 3 * w + j, sib).wait_send()
                land = ref.at[2 * cx + cy, pl.ds((1 - c) * rh, rh)]
                _remote(land, land, old, 3 * w + j, sib).wait_recv()

    return _split_call(name, bufs, run, old=sems, after=after)[0]


def _pair_send_start(name, grads):
    n = len(grads)
    lands = [lax.empty((N_CHIPS,) + g.shape[2:], g.dtype) for g in grads]

    def run(refs, old, new):
        x, y, c = _place()
        for w in range(n):
            _remote(refs[w].at[:, 1 - c], refs[n + w], new, w, (x, y, 1 - c)).start()

    return _split_call(name, list(grads) + lands, run, new=n)


def _pair_send_wait(name, bufs, sems, after):
    n = len(bufs) // 2

    def run(refs, old, new):
        x, y, c = _place()
        for w in range(n):
            cp = _remote(refs[w].at[:, 1 - c], refs[n + w], old, w, (x, y, 1 - c))
            cp.wait_send()
            cp.wait_recv()

    thru = _split_call(name, bufs, run, old=sems, after=after)[0]
    return thru[:n], thru[n:]


def _pair_add(g, recv, cidx, name):
    _, _, rh, cols = g.shape
    tr = min(rh, 256)

    def body(c_ref, g_ref, r_ref, o_ref):
        o_ref[...] = (g_ref[...].astype(F32) + r_ref[...].astype(F32)).astype(BF)

    return pl.pallas_call(
        body, name=name,
        grid_spec=pltpu.PrefetchScalarGridSpec(
            num_scalar_prefetch=1, grid=(rh // tr,),
            in_specs=[pl.BlockSpec((N_CHIPS, None, tr, cols), lambda i, c_ref: (0, c_ref[0], i, 0)),
                      pl.BlockSpec((N_CHIPS, tr, cols), lambda i, c_ref: (0, i, 0))],
            out_specs=pl.BlockSpec((N_CHIPS, tr, cols), lambda i, c_ref: (0, i, 0))),
        out_shape=jax.ShapeDtypeStruct((N_CHIPS, rh, cols), BF),
        compiler_params=_cp(("parallel",)),
    )(cidx, g, recv)


def _chip_send_start(name, sums):
    n = len(sums)
    lands = [lax.empty((3,) + s.shape[1:], BF) for s in sums]

    def run(refs, old, new):
        x, y, c = _place()
        for w in range(n):
            for j, (cx, cy) in enumerate(_other_chips(x, y)):
                _remote(refs[w].at[2 * cx + cy], refs[n + w].at[j], new, 3 * w + j, (cx, cy, c)).start()

    return _split_call(name, list(sums) + lands, run, new=3 * n)


def _chip_send_wait(name, bufs, sems, after):
    n = len(bufs) // 2

    def run(refs, old, new):
        x, y, c = _place()
        for w in range(n):
            for j, (cx, cy) in enumerate(_other_chips(x, y)):
                cp = _remote(refs[w].at[2 * cx + cy], refs[n + w].at[j], old, 3 * w + j, (cx, cy, c))
                cp.wait_send()
                cp.wait_recv()

    thru = _split_call(name, bufs, run, old=sems, after=after)[0]
    return thru[:n], thru[n:]


def _chip_add(own, parts, kidx, cidx, name):
    _, rh, cols = parts.shape
    tr = min(rh, 512)

    def body(k_ref, c_ref, own_ref, p_ref, o_ref):
        acc = own_ref[...].astype(F32)
        for s in range(3):
            acc = acc + p_ref[s].astype(F32)
        o_ref[...] = acc

    return pl.pallas_call(
        body, name=name,
        grid_spec=pltpu.PrefetchScalarGridSpec(
            num_scalar_prefetch=2, grid=(rh // tr,),
            in_specs=[pl.BlockSpec((None, tr, cols), lambda i, k_ref, c_ref: (k_ref[0], i, 0)),
                      pl.BlockSpec((3, tr, cols), lambda i, k_ref, c_ref: (0, i, 0))],
            out_specs=pl.BlockSpec((None, tr, cols), lambda i, k_ref, c_ref: (c_ref[0], i, 0))),
        out_shape=jax.ShapeDtypeStruct((2, rh, cols), F32),
        compiler_params=_cp(("parallel",)),
    )(kidx, cidx, own, parts)


def _pair_swap_start(name, bufs):
    def run(refs, old, new):
        x, y, c = _place()
        for w, ref in enumerate(refs):
            _remote(ref.at[c], ref.at[c], new, w, (x, y, 1 - c)).start()

    return _split_call(name, bufs, run, new=len(bufs))


def _pair_swap_wait(name, bufs, sems, after):
    def run(refs, old, new):
        x, y, c = _place()
        for w, ref in enumerate(refs):
            _remote(ref.at[c], ref.at[c], old, w, (x, y, 1 - c)).wait_send()
            _remote(ref.at[1 - c], ref.at[1 - c], old, w, (x, y, 1 - c)).wait_recv()

    return _split_call(name, bufs, run, old=sems, after=after)[0]


def _peers(x, y, c):
    out = []
    for code in range(1, 8):
        fx, fy, fc = (code >> 2) & 1, (code >> 1) & 1, code & 1
        px = 1 - x if fx else x
        py = 1 - y if fy else y
        pc = 1 - c if fc else c
        out.append((code, (px, py, pc)))
    return out


def _mod_exchange(c_row, ada_w, ada_b4, deps):
    ncol = ada_w.shape[1]

    def body(c_ref, w_ref, b_ref, *rest):
        call_ref, mod_ref, part_ref, send_sems, recv_sems = rest[len(deps):]
        x, y, c = _place()
        k = 2 * x + y
        me = 4 * x + 2 * y + c
        call_ref[pl.ds(me, 1), :] = c_ref[...]
        sends = []
        for code, peer in _peers(x, y, c):
            cp = pltpu.make_async_remote_copy(
                src_ref=c_ref, dst_ref=call_ref.at[pl.ds(me, 1), :],
                send_sem=send_sems.at[code], recv_sem=recv_sems.at[code],
                device_id=peer, device_id_type=MESH)
            cp.start()
            sends.append(cp)
        for code, (px, py, pc) in _peers(x, y, c):
            land = call_ref.at[pl.ds(4 * px + 2 * py + pc, 1), :]
            pltpu.make_async_remote_copy(
                src_ref=land, dst_ref=land, send_sem=send_sems.at[code], recv_sem=recv_sems.at[code],
                device_id=(px, py, pc), device_id_type=MESH).wait_recv()
        call = call_ref[...]
        act = call * _sigmoid(call)
        part = jnp.dot(act, w_ref[...], preferred_element_type=F32,
                       precision=lax.Precision.HIGHEST) + b_ref[pl.ds(k, 1), :]
        part_ref[...] = part
        mod_ref[pl.ds(k, 1), :] = part_ref[pl.ds(me, 1), :]
        chips = [(8 + j, peer) for j, (code, peer) in enumerate(_peers(x, y, c)) if code in (2, 4, 6)]
        for slot, (px, py, pc) in chips:
            cp = pltpu.make_async_remote_copy(
                src_ref=part_ref.at[pl.ds(4 * px + 2 * py + pc, 1), :], dst_ref=mod_ref.at[pl.ds(k, 1), :],
                send_sem=send_sems.at[slot], recv_sem=recv_sems.at[slot],
                device_id=(px, py, pc), device_id_type=MESH)
            cp.start()
            sends.append(cp)
        for slot, (px, py, pc) in chips:
            land = mod_ref.at[pl.ds(2 * px + py, 1), :]
            pltpu.make_async_remote_copy(
                src_ref=land, dst_ref=land, send_sem=send_sems.at[slot], recv_sem=recv_sems.at[slot],
                device_id=(px, py, pc), device_id_type=MESH).wait_recv()
        for cp in sends:
            cp.wait_send()

    vm = pl.BlockSpec(memory_space=pltpu.VMEM)
    return pl.pallas_call(
        body, name="mod_exchange",
        in_specs=[vm, vm, vm] + [ANY] * len(deps), out_specs=[vm, vm],
        out_shape=[jax.ShapeDtypeStruct((8, D_MODEL), F32), jax.ShapeDtypeStruct((N_CHIPS, ncol), F32)],
        scratch_shapes=[pltpu.VMEM((8, ncol), F32), pltpu.SemaphoreType.DMA((16,)),
                        pltpu.SemaphoreType.DMA((16,))],
        compiler_params=_cp(),
    )(c_row, ada_w, ada_b4, *deps)


def _small_exchange(payload, c_all, wsm, msm, vsm):
    ncol = 6 * D_MODEL // N_CHIPS

    def body(p_ref, call_ref, w_ref, m_ref, v_ref, gw_ref, g_ref, d_ref, mo_ref, vo_ref, loss_ref,
             all_ref, dm_ref, send_sems, recv_sems):
        x, y, c = _place()
        k = 2 * x + y
        me = 4 * x + 2 * y + c
        all_ref[:, pl.ds(me, 1), :] = p_ref[...]
        sends = []
        for code, peer in _peers(x, y, c):
            cp = pltpu.make_async_remote_copy(
                src_ref=p_ref, dst_ref=all_ref.at[:, pl.ds(me, 1), :],
                send_sem=send_sems.at[code], recv_sem=recv_sems.at[code],
                device_id=peer, device_id_type=MESH)
            cp.start()
            sends.append(cp)
        for code, (px, py, pc) in _peers(x, y, c):
            land = all_ref.at[:, pl.ds(4 * px + 2 * py + pc, 1), :]
            pltpu.make_async_remote_copy(
                src_ref=land, dst_ref=land, send_sem=send_sems.at[code], recv_sem=recv_sems.at[code],
                device_id=(px, py, pc), device_id_type=MESH).wait_recv()
        for cp in sends:
            cp.wait_send()
        tot = [_colsum(all_ref[r]) for r in range(N_PAY)]
        loss_ref[...] = jnp.sum(tot[P_LOSS], axis=1, keepdims=True)
        g_ref[...] = jnp.zeros_like(g_ref)
        for r in range(P_LOSS):
            g_ref[r:r + 1, :] = tot[r]
        g = g_ref[...]
        d, mm, vv = _adamw_math(w_ref[...], g, m_ref[...], v_ref[...])
        d_ref[...] = d
        mo_ref[...] = mm
        vo_ref[...] = vv
        half = D_MODEL // 2
        for kk in range(N_CHIPS):
            @pl.when(k == kk)
            def _():
                r0 = 3 * (kk // 2)
                if kk % 2 == 0:
                    dm_ref[:, :D_MODEL] = all_ref[r0]
                    dm_ref[:, D_MODEL:] = all_ref[r0 + 1][:, :half]
                else:
                    dm_ref[:, :half] = all_ref[r0 + 1][:, half:]
                    dm_ref[:, half:] = all_ref[r0 + 2]
        call = call_ref[...]
        act = call * _sigmoid(call)
        gw_ref[...] = lax.dot_general(act, dm_ref[...], (((0,), (0,)), ((), ())),
                                      preferred_element_type=F32, precision=lax.Precision.HIGHEST)

    vm = pl.BlockSpec(memory_space=pltpu.VMEM)
    small = jax.ShapeDtypeStruct((16, D_MODEL), F32)
    return pl.pallas_call(
        body, name="small_exchange",
        in_specs=[vm] * 5, out_specs=[vm] * 6,
        out_shape=[jax.ShapeDtypeStruct((D_MODEL, ncol), F32), small, small, small, small,
                   jax.ShapeDtypeStruct((1, 1), F32)],
        scratch_shapes=[pltpu.VMEM((N_PAY, 8, D_MODEL), F32), pltpu.VMEM((8, ncol), F32),
                        pltpu.SemaphoreType.DMA((8,)), pltpu.SemaphoreType.DMA((8,))],
        compiler_params=_cp(),
    )(payload, c_all, wsm, msm, vsm)


def _rope_tables(pos_col, inv_freq, sign):
    def body(p_ref, f_ref, s_ref, cos_ref, sin_ref):
        ang = p_ref[...].astype(F32) * f_ref[...]
        cos_ref[...] = jnp.cos(ang)
        sin_ref[...] = jnp.sin(ang) * s_ref[...]

    tr = 512
    shp = jax.ShapeDtypeStruct((SEQ, 128), F32)
    return pl.pallas_call(
        body, name="rope_tables", grid=(SEQ // tr,),
        in_specs=[pl.BlockSpec((tr, 1), lambda i: (i, 0)), pl.BlockSpec((1, 128), lambda i: (0, 0)),
                  pl.BlockSpec((1, 128), lambda i: (0, 0))],
        out_specs=[pl.BlockSpec((tr, 128), lambda i: (i, 0))] * 2, out_shape=[shp, shp],
        compiler_params=_cp(("parallel",)),
    )(pos_col, inv_freq, sign)


def _resident(shape):
    nd = len(shape)
    return pl.BlockSpec(shape, lambda *_: (0,) * nd, pipeline_mode=pl.Buffered(1))


def _ln_proj(x, vecs, w_in4):
    tm = 256
    wc = w_in4.shape[2]

    def body(x_ref, vec_ref, w_ref, h_ref, proj_ref):
        xx = x_ref[...]
        g = _row(vec_ref, V_G1) * (1.0 + _row(vec_ref, V_SC1))
        h = (xx * _rms(xx) * g + _row(vec_ref, V_SH1)).astype(BF)
        h_ref[...] = h
        for j in range(N_CHIPS):
            proj_ref[:, j * wc:(j + 1) * wc] = _dot(h, w_ref[j]).astype(BF)

    return pl.pallas_call(
        body, name="ln_proj", grid=(SEQ // tm,),
        in_specs=[pl.BlockSpec((tm, D_MODEL), lambda i: (i, 0)), _resident((16, D_MODEL)),
                  _resident(w_in4.shape)],
        out_specs=[pl.BlockSpec((tm, D_MODEL), lambda i: (i, 0)), pl.BlockSpec((tm, D_IN), lambda i: (i, 0))],
        out_shape=[jax.ShapeDtypeStruct((SEQ, D_MODEL), BF), jax.ShapeDtypeStruct((SEQ, D_IN), BF)],
        compiler_params=_cp(("parallel",)),
    )(x, vecs, w_in4)


def _lane_first(shape):
    lane = lax.broadcasted_iota(jnp.int32, shape, 1)
    return (lane & 32) == 0


def _rot(v, cos, sin_s):
    partner = jnp.where(_lane_first(v.shape), pltpu.roll(v, 96, 1), pltpu.roll(v, 32, 1))
    return v * cos + partner * sin_s


def _rot_t(dv, cos, sin_s):
    t = dv * sin_s
    partner = jnp.where(_lane_first(dv.shape), pltpu.roll(t, 96, 1), pltpu.roll(t, 32, 1))
    return dv * cos + partner


def _ret_masks(lg):
    t = RET_BLOCK
    ii = lax.broadcasted_iota(jnp.int32, (t, t), 0)
    jj = lax.broadcasted_iota(jnp.int32, (t, t), 1)
    dist = jnp.abs(ii - jj).astype(F32)
    future = (jj >> RET_CHUNK_SHIFT) > (ii >> RET_CHUNK_SHIFT)
    mask = jnp.where(future, 0.0, jnp.exp(lg * dist))
    ti = lax.broadcasted_iota(jnp.int32, (t, 1), 0).astype(F32)
    from_start = jnp.exp(lg * (ti + 1.0))
    to_end = jnp.exp(lg * (t - 1.0 - ti))
    whole = jnp.exp(jnp.full((1, 128), lg * t, F32))
    return mask, from_start, to_end, whole


def _head_lanes(shape, hh):
    lane = lax.broadcasted_iota(jnp.int32, shape, 1)
    return (lane >> 6) == hh


def _ret_specs():
    t = RET_BLOCK
    return dict(
        q=lambda f: pl.BlockSpec((t, 512), lambda n: (f(n), C_QR // 512)),
        k=lambda f: pl.BlockSpec((t, 512), lambda n: (f(n), C_KR // 512)),
        v=lambda f: pl.BlockSpec((t, D_MODEL), lambda n: (f(n), C_VR // D_MODEL)),
        g=lambda f: pl.BlockSpec((t, D_MODEL), lambda n: (f(n), C_GR // D_MODEL)),
        tab=lambda f: pl.BlockSpec((t, 128), lambda n: (f(n), 0)),
        wide=lambda f: pl.BlockSpec((t, D_MODEL), lambda n: (f(n), 0)),
        state=lambda f: pl.BlockSpec((N_PAIRS, None, 2, 128, 128), lambda n: (0, f(n), 0, 0, 0)),
    )


def _ret_fwd(proj, cos, sin_s, gn_g, log_gamma):
    t = RET_BLOCK
    nb = SEQ // t

    def body(lg_ref, q_ref, k_ref, v_ref, g_ref, cos_ref, sin_ref, gn_ref, o_ref, retg_ref, st_ref, state):
        @pl.when(pl.program_id(0) == 0)
        def _():
            state[...] = jnp.zeros_like(state)

        cos, sn = cos_ref[...], sin_ref[...]
        for p in range(N_PAIRS):
            q = _rot(q_ref[:, 128 * p:128 * (p + 1)].astype(F32), cos, sn)
            k = _rot(k_ref[:, 128 * p:128 * (p + 1)].astype(F32), cos, sn) * QK_SCALE
            for hh in range(2):
                cols = slice(256 * p + 128 * hh, 256 * p + 128 * (hh + 1))
                lg = lg_ref[2 * p + hh]
                mask, from_start, to_end, whole = _ret_masks(lg)
                lanes = _head_lanes(q.shape, hh)
                qm = jnp.where(lanes, q, 0.0)
                km = jnp.where(lanes, k, 0.0)
                vh = v_ref[:, cols]
                sc = _dot_nt(qm.astype(BF), km.astype(BF)) * mask
                st = state[p, hh]
                st_ref[p, hh] = st
                o = _dot(sc.astype(BF), vh) + _dot((qm * from_start).astype(BF), st.astype(BF))
                state[p, hh] = whole * st + _dot_tn((km * to_end).astype(BF), vh)
                d = o - _rowmean(o)
                nh = d * lax.rsqrt(_rowmean(d * d) + EPS)
                gr = g_ref[:, cols].astype(F32)
                o_ref[:, cols] = o
                retg_ref[:, cols] = (gr * _sigmoid(gr) * nh * gn_ref[:, cols]).astype(BF)

    sp = _ret_specs()
    ident = lambda n: n
    return pl.pallas_call(
        body, name="ret_fwd", grid=(nb,),
        in_specs=[pl.BlockSpec(memory_space=pltpu.SMEM), sp["q"](ident), sp["k"](ident), sp["v"](ident),
                  sp["g"](ident), sp["tab"](ident), sp["tab"](ident), _resident((1, D_MODEL))],
        out_specs=[sp["wide"](ident), sp["wide"](ident), sp["state"](ident)],
        out_shape=[jax.ShapeDtypeStruct((SEQ, D_MODEL), F32), jax.ShapeDtypeStruct((SEQ, D_MODEL), BF),
                   jax.ShapeDtypeStruct((N_PAIRS, nb, 2, 128, 128), F32)],
        scratch_shapes=[pltpu.VMEM((N_PAIRS, 2, 128, 128), F32)],
        compiler_params=_cp(("arbitrary",)),
    )(log_gamma, proj, proj, proj, proj, cos, sin_s, gn_g)


def _ret_bwd(proj, cos, sin_s, dret, states, log_gamma):
    t = RET_BLOCK
    nb = SEQ // t

    def body(lg_ref, q_ref, k_ref, v_ref, cos_ref, sin_ref, do_ref, st_ref, dqkv_ref, dstate):
        @pl.when(pl.program_id(0) == 0)
        def _():
            dstate[...] = jnp.zeros_like(dstate)

        cos, sn = cos_ref[...], sin_ref[...]
        for p in range(N_PAIRS):
            q = _rot(q_ref[:, 128 * p:128 * (p + 1)].astype(F32), cos, sn)
            k = _rot(k_ref[:, 128 * p:128 * (p + 1)].astype(F32), cos, sn) * QK_SCALE
            dq_rot = jnp.zeros(q.shape, F32)
            dk_rot = jnp.zeros(q.shape, F32)
            for hh in range(2):
                cols = slice(256 * p + 128 * hh, 256 * p + 128 * (hh + 1))
                lg = lg_ref[2 * p + hh]
                mask, from_start, to_end, whole = _ret_masks(lg)
                lanes = _head_lanes(q.shape, hh)
                qm = jnp.where(lanes, q, 0.0)
                km = jnp.where(lanes, k, 0.0)
                qb, kb = qm.astype(BF), km.astype(BF)
                vh = v_ref[:, cols]
                do = do_ref[:, cols]
                sc = (_dot_nt(qb, kb) * mask).astype(BF)
                st = st_ref[p, hh].astype(BF)
                dst = dstate[p, hh]
                dstb = dst.astype(BF)
                k_end = (km * to_end).astype(BF)
                q_start = (qm * from_start).astype(BF)
                dqkv_ref[:, C_VR + 256 * p + 128 * hh:C_VR + 256 * p + 128 * (hh + 1)] = (
                    _dot_tn(sc, do) + _dot(k_end, dstb)).astype(BF)
                dsc = (_dot_nt(do, vh) * mask).astype(BF)
                dq_h = _dot(dsc, kb) + _dot_nt(do, st) * from_start
                dq_rot = dq_rot + jnp.where(lanes, dq_h, 0.0)
                dk_rot = dk_rot + _dot_tn(dsc, qb) + _dot_nt(vh, dstb) * to_end
                dstate[p, hh] = whole * dst + _dot_tn(q_start, do)
            dqkv_ref[:, C_QR + 128 * p:C_QR + 128 * (p + 1)] = _rot_t(dq_rot, cos, sn).astype(BF)
            dqkv_ref[:, C_KR + 128 * p:C_KR + 128 * (p + 1)] = _rot_t(dk_rot * QK_SCALE, cos, sn).astype(BF)

    sp = _ret_specs()
    rev = lambda n: nb - 1 - n
    return pl.pallas_call(
        body, name="ret_bwd", grid=(nb,),
        in_specs=[pl.BlockSpec(memory_space=pltpu.SMEM), sp["q"](rev), sp["k"](rev), sp["v"](rev),
                  sp["tab"](rev), sp["tab"](rev), sp["wide"](rev), sp["state"](rev)],
        out_specs=pl.BlockSpec((t, C_GR), lambda n: (rev(n), 0)),
        out_shape=jax.ShapeDtypeStruct((SEQ, C_GR), BF),
        scratch_shapes=[pltpu.VMEM((N_PAIRS, 2, 128, 128), F32)],
        compiler_params=_cp(("arbitrary",)),
    )(log_gamma, proj, proj, proj, cos, sin_s, dret, states)


def _stack_heads(v):
    return jnp.concatenate([jnp.where(_head_lanes(v.shape, hh), v, jnp.zeros_like(v)) for hh in range(2)], axis=0)


def _unstack_heads(v):
    t = v.shape[0] // 2
    return jnp.where(_head_lanes((t, v.shape[1]), 0), v[:t], v[t:])


def _sb_masks(t, heads):
    rr = lax.broadcasted_iota(jnp.int32, (t, t), 0)
    cc = lax.broadcasted_iota(jnp.int32, (t, t), 1)
    r2 = lax.broadcasted_iota(jnp.int32, (heads * t, t), 0) & (t - 1)
    c2 = lax.broadcasted_iota(jnp.int32, (heads * t, t), 1)
    return rr, cc, c2 < r2


def _split_dot2(v, tri):
    return _dot(v.astype(BF), tri)


def _log_one_minus_beta(z):
    return -(jnp.maximum(z, 0.0) + jnp.log(1.0 + jnp.exp(-jnp.abs(z))))


def _sb_fwd(proj):
    t, g = SB_BLOCK, SB_GROUP
    nq = SEQ // t
    rows = 2 * g * t

    def body(q_ref, k_ref, v_ref, o_ref, tot_ref, kt_ref):
        i = pl.program_id(1)

        @pl.when(i == 0)
        def _():
            for p in range(g):
                for jj in range(nq):
                    kt_ref[p, jj] = k_ref[jj * t:(jj + 1) * t, 128 * p:128 * (p + 1)].T

        q2 = [_stack_heads((q_ref[:, 128 * p:128 * (p + 1)].astype(F32) * QK_SCALE).astype(BF)) for p in range(g)]
        rr, cc, valid = _sb_masks(t, 2 * g)
        later = (rr > cc).astype(BF)

        def tile(j, carry, diagonal):
            acc, run = carry
            z = jnp.concatenate([_dot(q2[p], kt_ref[p, j]) for p in range(g)], axis=0)
            lm = _log_one_minus_beta(z)
            if diagonal:
                lm = jnp.where(valid, lm, 0.0)
            after = _split_dot2(lm, later)
            a = jnp.exp(z + lm + after + run)
            if diagonal:
                a = jnp.where(valid, a, 0.0)
            ab = a.astype(BF)
            keys = pl.ds(pl.multiple_of(j * t, t), t)
            av = jnp.concatenate([_dot(ab[2 * t * p:2 * t * (p + 1)], v_ref[keys, 128 * p:128 * (p + 1)])
                                  for p in range(g)], axis=0)
            return acc + av, run + after[:, 0:1] + lm[:, 0:1]

        carry = tile(i, (jnp.zeros((rows, 128), F32), jnp.zeros((rows, 1), F32)), True)
        acc, run = lax.fori_loop(0, i, lambda s, cr: tile(i - 1 - s, cr, False), carry)
        run = jnp.broadcast_to(run, (rows, 128))
        for p in range(g):
            o_ref[:, 128 * p:128 * (p + 1)] = _unstack_heads(acc[2 * t * p:2 * t * (p + 1)]).astype(BF)
            tot_ref[:, 128 * p:128 * (p + 1)] = _unstack_heads(run[2 * t * p:2 * t * (p + 1)])

    w = 128 * g
    return pl.pallas_call(
        body, name="sb_fwd", grid=(N_PAIRS // g, nq),
        in_specs=[pl.BlockSpec((t, w), lambda p, i: (i, C_QS // w + p)),
                  pl.BlockSpec((SEQ, w), lambda p, i: (0, C_KS // w + p)),
                  pl.BlockSpec((SEQ, w), lambda p, i: (0, C_VS // w + p))],
        out_specs=[pl.BlockSpec((t, w), lambda p, i: (i, p))] * 2,
        out_shape=[jax.ShapeDtypeStruct((SEQ, 512), BF), jax.ShapeDtypeStruct((SEQ, 512), F32)],
        scratch_shapes=[pltpu.VMEM((g, nq, 128, t), BF)],
        compiler_params=_cp(("parallel", "arbitrary")),
    )(proj, proj, proj)


def _sb_bwd(proj, dsb, tot, dep=None):
    t, g = SB_BLOCK, SB_GROUP
    nq = SEQ // t
    rows = 2 * g * t

    def body(q_ref, k_ref, v_ref, do_ref, tot_ref, dq_ref, dk_ref, dv_ref, kt_ref, vt_ref, dkt_acc, dvt_acc):
        i = pl.program_id(1)

        @pl.when(i == 0)
        def _():
            dkt_acc[...] = jnp.zeros_like(dkt_acc)
            dvt_acc[...] = jnp.zeros_like(dvt_acc)
            for p in range(g):
                for jj in range(nq):
                    kt_ref[p, jj] = k_ref[jj * t:(jj + 1) * t, 128 * p:128 * (p + 1)].T
                    vt_ref[p, jj] = v_ref[jj * t:(jj + 1) * t, 128 * p:128 * (p + 1)].T

        q2 = [_stack_heads((q_ref[:, 128 * p:128 * (p + 1)].astype(F32) * QK_SCALE).astype(BF)) for p in range(g)]
        do2 = [_stack_heads(do_ref[:, 128 * p:128 * (p + 1)]) for p in range(g)]
        q2t = [v.T for v in q2]
        do2t = [v.T for v in do2]
        tots = tot_ref[...]
        total = jnp.concatenate([tots[:, 64 * h:64 * h + 1] for h in range(2 * g)], axis=0)
        rr, cc, valid = _sb_masks(t, 2 * g)
        upto = (rr <= cc).astype(BF)
        before = (rr < cc).astype(BF)

        def part(v, p):
            return v[2 * t * p:2 * t * (p + 1)]

        def tile(j, carry, diagonal):
            dq, run_l, run_g = carry
            z = jnp.concatenate([_dot(q2[p], kt_ref[p, j]) for p in range(g)], axis=0)
            lm = _log_one_minus_beta(z)
            if diagonal:
                lm = jnp.where(valid, lm, 0.0)
            incl = _split_dot2(lm, upto)
            a = jnp.exp(z + lm + (total - (incl + run_l)))
            if diagonal:
                a = jnp.where(valid, a, 0.0)
            gg = a * jnp.concatenate([_dot(do2[p], vt_ref[p, j]) for p in range(g)], axis=0)
            excl = _split_dot2(gg, before)
            dz = gg * jnp.exp(lm) - (excl + run_g) * jnp.exp(z + lm)
            if diagonal:
                dz = jnp.where(valid, dz, 0.0)
            dzb = dz.astype(BF)
            ab = a.astype(BF)
            keys = pl.ds(pl.multiple_of(j * t, t), t)
            for p in range(g):
                dkt_acc[p, j] += _dot(q2t[p], part(dzb, p))
                dvt_acc[p, j] += _dot(do2t[p], part(ab, p))
            dq_t = jnp.concatenate([_dot(part(dzb, p), k_ref[keys, 128 * p:128 * (p + 1)]) for p in range(g)], axis=0)
            return (dq + dq_t, run_l + incl[:, t - 1:t], run_g + excl[:, t - 1:t] + gg[:, t - 1:t])

        zero = jnp.zeros((rows, 1), F32)
        carry = lax.fori_loop(0, i, lambda j, cr: tile(j, cr, False), (jnp.zeros((rows, 128), F32), zero, zero))
        dq = tile(i, carry, True)[0]
        for p in range(g):
            dq_ref[:, 128 * p:128 * (p + 1)] = (_unstack_heads(part(dq, p)) * QK_SCALE).astype(BF)

        @pl.when(i == nq - 1)
        def _():
            for p in range(g):
                for jj in range(nq):
                    dk_ref[jj * t:(jj + 1) * t, 128 * p:128 * (p + 1)] = dkt_acc[p, jj].T.astype(BF)
                    dv_ref[jj * t:(jj + 1) * t, 128 * p:128 * (p + 1)] = dvt_acc[p, jj].T.astype(BF)

    w = 128 * g
    tile_spec = pl.BlockSpec((t, w), lambda p, i: (i, p))
    col_spec = pl.BlockSpec((SEQ, w), lambda p, i: (0, p))
    shp = jax.ShapeDtypeStruct((SEQ, 512), BF)
    body, in_specs, args = _add_dep(
        body, [pl.BlockSpec((t, w), lambda p, i: (i, C_QS // w + p)),
               pl.BlockSpec((SEQ, w), lambda p, i: (0, C_KS // w + p)),
               pl.BlockSpec((SEQ, w), lambda p, i: (0, C_VS // w + p)),
               tile_spec, tile_spec],
        [proj, proj, proj, dsb, tot], dep)
    return pl.pallas_call(
        body, name="sb_bwd", grid=(N_PAIRS // g, nq),
        in_specs=in_specs,
        out_specs=[tile_spec, col_spec, col_spec],
        out_shape=[shp, shp, shp],
        scratch_shapes=[pltpu.VMEM((g, nq, 128, t), BF), pltpu.VMEM((g, nq, 128, t), BF),
                        pltpu.VMEM((g, nq, 128, t), F32), pltpu.VMEM((g, nq, 128, t), F32)],
        compiler_params=_cp(("parallel", "arbitrary")),
    )(*args)


def _mix(retg, sb, proj, w_ret, w_sb4):
    tm, tn = min(512, SEQ), 512

    def body(r_ref, s_ref, ar_ref, as_ref, wr_ref, ws_ref, mix_ref, rb_ref, sbp_ref):
        rb = _dot(r_ref[...], wr_ref[...])
        sbv = s_ref[...]
        sbp = jnp.concatenate([_dot(sbv, ws_ref[0]), _dot(sbv, ws_ref[1])], axis=1)
        mix = _sigmoid(ar_ref[...].astype(F32)) * rb + _sigmoid(as_ref[...].astype(F32)) * sbp
        mix_ref[...] = mix.astype(BF)
        rb_ref[...] = rb.astype(BF)
        sbp_ref[...] = sbp.astype(BF)

    out = pl.BlockSpec((tm, tn), lambda j, i: (i, j))
    shp = jax.ShapeDtypeStruct((SEQ, D_MODEL), BF)
    return pl.pallas_call(
        body, name="mix", grid=(D_MODEL // tn, SEQ // tm),
        in_specs=[pl.BlockSpec((tm, D_MODEL), lambda j, i: (i, 0)),
                  pl.BlockSpec((tm, 512), lambda j, i: (i, 0)),
                  pl.BlockSpec((tm, tn), lambda j, i: (i, C_AR // tn + j)),
                  pl.BlockSpec((tm, tn), lambda j, i: (i, C_AS // tn + j)),
                  pl.BlockSpec((D_MODEL, tn), lambda j, i: (0, j)),
                  pl.BlockSpec((2, 512, 256), lambda j, i: (j, 0, 0))],
        out_specs=[out, out, out], out_shape=[shp, shp, shp],
        compiler_params=_cp(("parallel", "parallel")),
    )(retg, sb, proj, proj, w_ret, w_sb4)


def _out_proj(mixed, x, vecs, w_out):
    tm = min(512, SEQ)

    def body(m_ref, x_ref, vec_ref, w_ref, y_ref, h1_ref, h2_ref):
        y = _dot(m_ref[...], w_ref[...])
        h1 = x_ref[...] + _row(vec_ref, V_GT1) * (y * _rms(y)) * _row(vec_ref, V_G2)
        g = _row(vec_ref, V_G3) * (1.0 + _row(vec_ref, V_SC2))
        y_ref[...] = y
        h1_ref[...] = h1
        h2_ref[...] = (h1 * _rms(h1) * g + _row(vec_ref, V_SH2)).astype(BF)

    row = pl.BlockSpec((tm, D_MODEL), lambda i: (i, 0))
    f32 = jax.ShapeDtypeStruct((SEQ, D_MODEL), F32)
    return pl.pallas_call(
        body, name="out_proj", grid=(SEQ // tm,),
        in_specs=[row, row, pl.BlockSpec((16, D_MODEL), lambda i: (0, 0)),
                  pl.BlockSpec((D_MODEL, D_MODEL), lambda i: (0, 0))],
        out_specs=[row, row, row],
        out_shape=[f32, f32, jax.ShapeDtypeStruct((SEQ, D_MODEL), BF)],
        compiler_params=_cp(("parallel",)),
    )(mixed, x, vecs, w_out)


def _ffn_fwd_loss(h2, h1, target, vecs, w_ff14, w_ff2):
    tm = 256

    def body(h2_ref, h1_ref, t_ref, vec_ref, w1_ref, w2_ref, u_ref, a_ref, dout_ref, df_ref, st_ref):
        @pl.when(pl.program_id(0) == 0)
        def _():
            st_ref[...] = jnp.zeros_like(st_ref)

        hb = h2_ref[...]
        f = jnp.zeros((tm, D_MODEL), F32)
        for k in range(N_CHIPS):
            cols = slice(k * D_MODEL, (k + 1) * D_MODEL)
            u = _dot(hb, w1_ref[k])
            r = jnp.maximum(u, 0.0)
            act = (r * r).astype(BF)
            u_ref[:, cols] = u.astype(BF)
            a_ref[:, cols] = act
            f = f + _dot(act, w2_ref[cols, :])
        r4 = _rms(f)
        fn = f * r4
        gt2, g4 = _row(vec_ref, V_GT2), _row(vec_ref, V_G4)
        diff = h1_ref[...] + gt2 * fn * g4 - t_ref[...]
        dout = diff * (1.0 / D_MODEL)
        dfn = dout * gt2 * g4
        dout_ref[...] = dout
        df_ref[...] = (r4 * (dfn - fn * _rowmean(dfn * fn))).astype(BF)
        st_ref[0:1, :] += _colsum(dout * fn * g4)
        st_ref[1:2, :] += _colsum(dout * gt2 * fn)
        st_ref[2:3, :] += _colsum(diff * diff) * (0.5 / D_MODEL)

    row = pl.BlockSpec((tm, D_MODEL), lambda i: (i, 0))
    wide = pl.BlockSpec((tm, D_FF), lambda i: (i, 0))
    return pl.pallas_call(
        body, name="ffn_fwd_loss", grid=(SEQ // tm,),
        in_specs=[row, row, row, _resident((16, D_MODEL)), _resident(w_ff14.shape), _resident(w_ff2.shape)],
        out_specs=[wide, wide, row, row, pl.BlockSpec((8, D_MODEL), lambda i: (0, 0))],
        out_shape=[jax.ShapeDtypeStruct((SEQ, D_FF), BF), jax.ShapeDtypeStruct((SEQ, D_FF), BF),
                   jax.ShapeDtypeStruct((SEQ, D_MODEL), F32), jax.ShapeDtypeStruct((SEQ, D_MODEL), BF),
                   jax.ShapeDtypeStruct((8, D_MODEL), F32)],
        compiler_params=_cp(("arbitrary",)),
    )(h2, h1, target, vecs, w_ff14, w_ff2)


def _ffn_bwd(df, u, h1, y, dout, vecs, w_ff14, w_ff2):
    tm = 256

    def body(df_ref, u_ref, h1_ref, y_ref, dout_ref, vec_ref, w1_ref, w2_ref, du_ref, dh1_ref, dy_ref, st_ref):
        @pl.when(pl.program_id(0) == 0)
        def _():
            st_ref[...] = jnp.zeros_like(st_ref)

        dfb = df_ref[...]
        dh2 = jnp.zeros((tm, D_MODEL), F32)
        for k in range(N_CHIPS):
            cols = slice(k * D_MODEL, (k + 1) * D_MODEL)
            da = _dot_nt(dfb, w2_ref[cols, :])
            du = (da * (2.0 * jnp.maximum(u_ref[:, cols].astype(F32), 0.0))).astype(BF)
            du_ref[:, cols] = du
            dh2 = dh2 + _dot_nt(du, w1_ref[k])
        h1 = h1_ref[...]
        r3 = _rms(h1)
        hn3 = h1 * r3
        g3, sc2 = _row(vec_ref, V_G3), _row(vec_ref, V_SC2)
        dhn3 = dh2 * g3 * (1.0 + sc2)
        dh1 = dout_ref[...] + r3 * (dhn3 - hn3 * _rowmean(dhn3 * hn3))
        y = y_ref[...]
        r2 = _rms(y)
        yn = y * r2
        gt1, g2 = _row(vec_ref, V_GT1), _row(vec_ref, V_G2)
        dyn = dh1 * gt1 * g2
        dh1_ref[...] = dh1
        dy_ref[...] = (r2 * (dyn - yn * _rowmean(dyn * yn))).astype(BF)
        st_ref[0:1, :] += _colsum(dh2)
        st_ref[1:2, :] += _colsum(dh2 * hn3 * g3)
        st_ref[2:3, :] += _colsum(dh2 * hn3 * (1.0 + sc2))
        st_ref[3:4, :] += _colsum(dh1 * yn * g2)
        st_ref[4:5, :] += _colsum(dh1 * gt1 * yn)

    row = pl.BlockSpec((tm, D_MODEL), lambda i: (i, 0))
    wide = pl.BlockSpec((tm, D_FF), lambda i: (i, 0))
    return pl.pallas_call(
        body, name="ffn_bwd", grid=(SEQ // tm,),
        in_specs=[row, wide, row, row, row, _resident((16, D_MODEL)), _resident(w_ff14.shape),
                  _resident(w_ff2.shape)],
        out_specs=[wide, row, row, pl.BlockSpec((8, D_MODEL), lambda i: (0, 0))],
        out_shape=[jax.ShapeDtypeStruct((SEQ, D_FF), BF), jax.ShapeDtypeStruct((SEQ, D_MODEL), F32),
                   jax.ShapeDtypeStruct((SEQ, D_MODEL), BF), jax.ShapeDtypeStruct((8, D_MODEL), F32)],
        compiler_params=_cp(("arbitrary",)),
    )(df, u, h1, y, dout, vecs, w_ff14, w_ff2)


def _mix_bwd(dy, proj, rb, sbp, w_out, w_sb4):
    tm, half = 256, 512

    def body(dy_ref, ar0, ar1, as0, as1, rb_ref, sbp_ref, wo_ref, ws_ref, drb_ref, dsbp_ref, da_ref, dsb_ref):
        dm_all = _dot_nt(dy_ref[...], wo_ref[...])
        dsb = jnp.zeros((tm, 512), F32)
        for hf, (ar_ref, as_ref) in enumerate(((ar0, as0), (ar1, as1))):
            cols = slice(half * hf, half * (hf + 1))
            dm = dm_all[:, cols]
            sr = _sigmoid(ar_ref[...].astype(F32))
            ss = _sigmoid(as_ref[...].astype(F32))
            dsbp = (dm * ss).astype(BF)
            drb_ref[:, cols] = (dm * sr).astype(BF)
            dsbp_ref[:, cols] = dsbp
            da_ref[:, cols] = (dm * rb_ref[:, cols].astype(F32) * sr * (1.0 - sr)).astype(BF)
            da_ref[:, D_MODEL + half * hf:D_MODEL + half * (hf + 1)] = (
                dm * sbp_ref[:, cols].astype(F32) * ss * (1.0 - ss)).astype(BF)
            dsb = dsb + _dot_nt(dsbp[:, :256], ws_ref[2 * hf]) + _dot_nt(dsbp[:, 256:], ws_ref[2 * hf + 1])
        dsb_ref[...] = dsb.astype(BF)

    row = pl.BlockSpec((tm, D_MODEL), lambda i: (i, 0))
    gate = lambda c0: pl.BlockSpec((tm, half), lambda i: (i, c0 // half))
    shp = jax.ShapeDtypeStruct((SEQ, D_MODEL), BF)
    return pl.pallas_call(
        body, name="mix_bwd", grid=(SEQ // tm,),
        in_specs=[row, gate(C_AR), gate(C_AR + half), gate(C_AS), gate(C_AS + half), row, row,
                  _resident((D_MODEL, D_MODEL)), _resident(w_sb4.shape)],
        out_specs=[row, row, pl.BlockSpec((tm, 2 * D_MODEL), lambda i: (i, 0)), pl.BlockSpec((tm, 512), lambda i: (i, 0))],
        out_shape=[shp, shp, jax.ShapeDtypeStruct((SEQ, 2 * D_MODEL), BF), jax.ShapeDtypeStruct((SEQ, 512), BF)],
        compiler_params=_cp(("parallel",)),
    )(dy, proj, proj, proj, proj, rb, sbp, w_out, w_sb4)


def _ret_branch_bwd(drb, proj, o_raw, gn_g, w_ret):
    tm, tn = min(512, SEQ), 512

    def body(d_ref, g_ref, o_ref, gn_ref, w_ref, dret_ref, dgr_ref, st_ref):
        @pl.when(pl.program_id(1) == 0)
        def _():
            st_ref[...] = jnp.zeros_like(st_ref)

        dretg = _dot_nt(d_ref[...], w_ref[...])
        for gi in range(tn // 128):
            cols = slice(128 * gi, 128 * (gi + 1))
            o = o_ref[:, cols]
            d = o - _rowmean(o)
            rstd = lax.rsqrt(_rowmean(d * d) + EPS)
            nh = d * rstd
            gain = gn_ref[:, cols]
            gr = g_ref[:, cols].astype(F32)
            sg = _sigmoid(gr)
            dg = dretg[:, cols]
            dgn = dg * gr * sg
            dnh = dgn * gain
            dgr_ref[:, cols] = (dg * nh * gain * sg * (1.0 + gr * (1.0 - sg))).astype(BF)
            dret_ref[:, cols] = (rstd * (dnh - _rowmean(dnh) - nh * _rowmean(dnh * nh))).astype(BF)
            st_ref[0:1, cols] += _colsum(dgn * nh)

    tile = pl.BlockSpec((tm, tn), lambda j, i: (i, j))
    shp = jax.ShapeDtypeStruct((SEQ, D_MODEL), BF)
    return pl.pallas_call(
        body, name="ret_branch_bwd", grid=(D_MODEL // tn, SEQ // tm),
        in_specs=[pl.BlockSpec((tm, D_MODEL), lambda j, i: (i, 0)),
                  pl.BlockSpec((tm, tn), lambda j, i: (i, C_GR // tn + j)),
                  tile, pl.BlockSpec((1, tn), lambda j, i: (0, j)),
                  pl.BlockSpec((tn, D_MODEL), lambda j, i: (j, 0))],
        out_specs=[tile, tile, pl.BlockSpec((8, tn), lambda j, i: (0, j))],
        out_shape=[shp, shp, jax.ShapeDtypeStruct((8, D_MODEL), F32)],
        compiler_params=_cp(("parallel", "arbitrary")),
    )(drb, proj, o_raw, gn_g, w_ret)


def _dproj_segments(widths):
    wc = D_IN // N_CHIPS
    segs, start = [], 0
    for pi, width in enumerate(widths):
        lo = start
        while lo < start + width:
            j = lo // wc
            hi = min(start + width, (j + 1) * wc)
            segs.append((j, lo - j * wc, pi, lo - start, hi - lo))
            lo = hi
        start += width
    assert start == D_IN
    return segs


def _in_proj_bwd(pieces, x, dh1, vecs, w_in4):
    tm = 256
    n = len(pieces)
    segs = _dproj_segments([p.shape[1] for p in pieces])

    def body(*refs):
        x_ref, dh1_ref, vec_ref, w_ref, dx_ref, st_ref = refs[n:]

        @pl.when(pl.program_id(0) == 0)
        def _():
            st_ref[...] = jnp.zeros_like(st_ref)

        dh = jnp.zeros((tm, D_MODEL), F32)
        for j, so, pi, po, width in segs:
            dh = dh + _dot_nt(refs[pi][:, po:po + width], w_ref[j, :, so:so + width])
        xx = x_ref[...]
        r1 = _rms(xx)
        xn = xx * r1
        g1, sc1 = _row(vec_ref, V_G1), _row(vec_ref, V_SC1)
        dxn = dh * g1 * (1.0 + sc1)
        dx_ref[...] = dh1_ref[...] + r1 * (dxn - xn * _rowmean(dxn * xn))
        st_ref[0:1, :] += _colsum(dh)
        st_ref[1:2, :] += _colsum(dh * xn * g1)
        st_ref[2:3, :] += _colsum(dh * xn * (1.0 + sc1))

    row = pl.BlockSpec((tm, D_MODEL), lambda i: (i, 0))
    return pl.pallas_call(
        body, name="in_proj_bwd", grid=(SEQ // tm,),
        in_specs=[pl.BlockSpec((tm, p.shape[1]), lambda i: (i, 0)) for p in pieces] + [
            row, row, _resident((16, D_MODEL)), _resident(w_in4.shape)],
        out_specs=[row, pl.BlockSpec((8, D_MODEL), lambda i: (0, 0))],
        out_shape=[jax.ShapeDtypeStruct((SEQ, D_MODEL), F32), jax.ShapeDtypeStruct((8, D_MODEL), F32)],
        compiler_params=_cp(("arbitrary",)),
    )(*pieces, x, dh1, vecs, w_in4)


def _grad_w_in(h, pieces, dep=None):
    ta = 256
    n = len(pieces)
    segs = _dproj_segments([p.shape[1] for p in pieces])

    def body(*refs):
        h_ref, o_ref = refs[n], refs[n + 1]
        hh = h_ref[...]
        for j, so, pi, po, width in segs:
            o_ref[j, :, so:so + width] = _dot_tn(hh, refs[pi][:, po:po + width]).astype(BF)

    body, in_specs, args = _add_dep(
        body, [_resident(p.shape) for p in pieces] + [pl.BlockSpec((SEQ, ta), lambda i: (0, i))],
        list(pieces) + [h], dep)
    return pl.pallas_call(
        body, name="grad_w_in", grid=(D_MODEL // ta,),
        in_specs=in_specs,
        out_specs=pl.BlockSpec((N_CHIPS, ta, D_IN // N_CHIPS), lambda i: (0, i, 0)),
        out_shape=jax.ShapeDtypeStruct((N_CHIPS, D_MODEL, D_IN // N_CHIPS), BF),
        compiler_params=_cp(("parallel",)),
    )(*args)


def _weight_grad(a, b, ta, tb, col_sharded, name, dep=None):
    ka, nb_ = a.shape[1], b.shape[1]

    def body(a_ref, b_ref, o_ref):
        o_ref[...] = _dot_tn(a_ref[...], b_ref[...]).astype(BF)

    body, in_specs, args = _add_dep(
        body, [pl.BlockSpec((SEQ, ta), lambda i, j: (0, i)), pl.BlockSpec((SEQ, tb), lambda i, j: (0, j))],
        [a, b], dep)

    if col_sharded:
        per = nb_ // N_CHIPS // tb
        out_shape = jax.ShapeDtypeStruct((N_CHIPS, ka, nb_ // N_CHIPS), BF)
        out_spec = pl.BlockSpec((None, ta, tb), lambda i, j: (j // per, i, j % per))
    else:
        per = ka // N_CHIPS // ta
        out_shape = jax.ShapeDtypeStruct((N_CHIPS, ka // N_CHIPS, nb_), BF)
        out_spec = pl.BlockSpec((None, ta, tb), lambda i, j: (i // per, i % per, j))
    return pl.pallas_call(
        body, name=name, grid=(ka // ta, nb_ // tb),
        in_specs=in_specs, out_specs=out_spec, out_shape=out_shape,
        compiler_params=_cp(("parallel", "parallel")),
    )(*args)


def _rope_constants():
    freq = np.float32(ROPE_BASE) ** (-np.arange(0, 64, 2, dtype=np.float32) / np.float32(64))
    inv = np.tile(freq.astype(np.float32), 4).reshape(1, 128)
    sign = np.tile(np.concatenate([-np.ones(32, np.float32), np.ones(32, np.float32)]), 2).reshape(1, 128)
    return jnp.asarray(inv), jnp.asarray(sign)


def _log_gamma():
    return jnp.asarray(np.log1p(-(2.0 ** (-5.0 - np.arange(8, dtype=np.float64)))).astype(np.float32))


def _halves(g):
    return g.reshape(N_CHIPS, 2, g.shape[1] // 2, g.shape[2])


def kernel(x, c, positions, ada_w, ada_b, pre_mix_g, post_mix_g, pre_ffn_g, post_ffn_g, w_in, ret_gn_g, w_ret_branch, w_sb_branch, w_out, w_ff1, w_ff2, loss_target, m_ada_w, m_ada_b, m_pre_mix_g, m_post_mix_g, m_pre_ffn_g, m_post_ffn_g, m_w_in, m_ret_gn_g, m_w_ret_branch, m_w_sb_branch, m_w_out, m_w_ff1, m_w_ff2, v_ada_w, v_ada_b, v_pre_mix_g, v_post_mix_g, v_pre_ffn_g, v_post_ffn_g, v_w_in, v_ret_gn_g, v_w_ret_branch, v_w_sb_branch, v_w_out, v_w_ff1, v_w_ff2):
    names = ["w_in", "w_ret", "w_sb", "w_out", "w_ff1", "w_ff2"]
    big = dict(zip(names, [w_in, w_ret_branch, w_sb_branch, w_out, w_ff1, w_ff2]))
    big_m = dict(zip(names, [m_w_in, m_w_ret_branch, m_w_sb_branch, m_w_out, m_w_ff1, m_w_ff2]))
    big_v = dict(zip(names, [v_w_in, v_w_ret_branch, v_w_sb_branch, v_w_out, v_w_ff1, v_w_ff2]))
    rest = names[1:]
    cidx = lax.axis_index("c").astype(jnp.int32).reshape(1)
    kidx = (2 * lax.axis_index("x") + lax.axis_index("y")).astype(jnp.int32).reshape(1)
    x0, target = x[0], loss_target[0]

    buf_in, sem_in, tok_in = _gather_start("gather_in_start", [_cast_bf16(w_in[0], kidx, c, "cast_w_in")])
    rest_bufs = [_cast_bf16(big[nm][0], kidx, tok_in, "cast_" + nm) for nm in rest]
    inv_freq, sign = _rope_constants()
    lg = _log_gamma()
    cos, sin_s = _rope_tables(positions.reshape(SEQ, 1), _tie(tok_in, inv_freq), sign)
    c_all, mod4 = _mod_exchange(c, ada_w[0], ada_b.reshape(N_CHIPS, -1), rest_bufs + [cos])
    vecs = jnp.concatenate([mod4.reshape(6, D_MODEL), pre_mix_g, post_mix_g, pre_ffn_g, post_ffn_g,
                            jnp.zeros((6, D_MODEL), F32)], axis=0)
    buf_in, sem_in, tok_in = _gather_pass("gather_in_pass", buf_in, sem_in, vecs)
    buf_rest, sem_rest, tok_rest = _gather_start("gather_rest_start", rest_bufs, after=tok_in)
    (w_in4,) = _gather_finish("gather_in_finish", buf_in, sem_in, tok_rest)

    h, proj = _ln_proj(x0, vecs, w_in4)
    sb, tot = _sb_fwd(proj)
    buf_rest, sem_rest, tok_rest = _gather_pass("gather_rest_pass", buf_rest, sem_rest, sb)
    o_raw, retg, states = _ret_fwd(proj, cos, sin_s, _tie(tok_rest, ret_gn_g), lg)
    w_ret4, w_sb4, w_out4, w_ff14, w_ff24 = _gather_finish("gather_rest_finish", buf_rest, sem_rest, retg)
    w_ret = w_ret4.reshape(D_MODEL, D_MODEL)
    w_out2 = w_out4.reshape(D_MODEL, D_MODEL)
    w_ff2_2 = w_ff24.reshape(D_FF, D_MODEL)
    mixed, rb, sbp = _mix(retg, sb, proj, w_ret, w_sb4)
    y, h1, h2 = _out_proj(mixed, x0, vecs, w_out2)
    u, act, dout, df, st_a = _ffn_fwd_loss(h2, h1, target, vecs, w_ff14, w_ff2_2)

    du, dh1, dy, st_b = _ffn_bwd(df, u, h1, y, dout, vecs, w_ff14, w_ff2_2)
    grads = {"w_ff2": _weight_grad(act, df, 512, 1024, False, "grad_w_ff2")}
    grads["w_ff1"] = _weight_grad(h2, du, 512, 1024, True, "grad_w_ff1")
    drb, dsbp, da, dsb = _mix_bwd(dy, proj, rb, sbp, w_out2, w_sb4)
    grads["w_out"] = _weight_grad(mixed, dy, 256, 1024, False, "grad_w_out")
    dret, dg_r, st_c = _ret_branch_bwd(drb, proj, o_raw, ret_gn_g, w_ret)
    grads["w_ret"] = _weight_grad(retg, drb, 256, 1024, False, "grad_w_ret")
    grads["w_sb"] = _weight_grad(sb, dsbp, 512, 256, True, "grad_w_sb")

    bufs, sems, tok = _pair_send_start("rs_rest_pair_send", [_halves(grads[nm]) for nm in rest])
    dqkv_r = _ret_bwd(proj, cos, sin_s, dret, states, _tie(tok, lg))
    mine, theirs = _pair_send_wait("rs_rest_pair_recv", bufs, sems, dqkv_r)
    pair_sums = [_pair_add(g, r, cidx, "pair_add_" + nm) for g, r, nm in zip(mine, theirs, rest)]
    bufs, sems, tok = _chip_send_start("rs_rest_chip_send", pair_sums)
    dq_s, dk_s, dv_s = _sb_bwd(proj, dsb, tot, dep=tok)
    own, parts = _chip_send_wait("rs_rest_chip_recv", bufs, sems, dq_s)
    sums = [_chip_add(o, p, kidx, cidx, "chip_add_" + nm) for o, p, nm in zip(own, parts, rest)]
    bufs, sems, tok = _pair_swap_start("rs_rest_pair_swap", sums)
    dproj = [dqkv_r, dg_r, dq_s, dk_s, dv_s, da]
    g_in = _grad_w_in(h, dproj, dep=tok)
    full_rest = _pair_swap_wait("rs_rest_pair_swapped", bufs, sems, g_in)

    def update(nm, g, dep):
        w = big[nm][0]
        return _adamw(w, big_m[nm][0], big_v[nm][0], g.reshape(w.shape), "adamw_" + nm, dep=dep)

    full_rest = dict(zip(rest, full_rest))
    bufs, sems, tok = _pair_send_start("rs_in_pair_send", [_halves(g_in)])
    out = {nm: update(nm, full_rest[nm], tok) for nm in ("w_ff2", "w_ff1")}
    mine, theirs = _pair_send_wait("rs_in_pair_recv", bufs, sems, out["w_ff1"][1])
    bufs, sems, tok = _chip_send_start("rs_in_chip_send", [_pair_add(mine[0], theirs[0], cidx, "pair_add_w_in")])
    dx, st_d = _in_proj_bwd(dproj, x0, dh1, _tie(tok, vecs), w_in4)
    out.update({nm: update(nm, full_rest[nm], dx) for nm in ("w_out", "w_sb", "w_ret")})

    payload = jnp.concatenate([
        st_d[0:2], st_b[3:4], st_b[0:2], st_a[0:1],
        st_d[2:3], st_b[4:5], st_b[2:3], st_a[1:2],
        st_c[0:1], st_a[2:3]], axis=0)

    def table(b6, g5):
        return jnp.concatenate([b6.reshape(6, D_MODEL)] + g5 + [jnp.zeros((5, D_MODEL), F32)], axis=0)

    wsm = table(ada_b, [pre_mix_g, post_mix_g, pre_ffn_g, post_ffn_g, ret_gn_g])
    msm = table(m_ada_b, [m_pre_mix_g, m_post_mix_g, m_pre_ffn_g, m_post_ffn_g, m_ret_gn_g])
    vsm = table(v_ada_b, [v_pre_mix_g, v_post_mix_g, v_pre_ffn_g, v_post_ffn_g, v_ret_gn_g])
    g_ada, gsm, dsm, mosm, vosm, loss = _small_exchange(
        payload.reshape(N_PAY, 1, D_MODEL), c_all, wsm, msm, vsm)
    ada_out = _adamw(ada_w[0], m_ada_w[0], v_ada_w[0], g_ada, "adamw_ada_w")

    own, parts = _chip_send_wait("rs_in_chip_recv", bufs, sems, ada_out[1])
    bufs, sems, tok = _pair_swap_start(
        "rs_in_pair_swap", [_chip_add(own[0], parts[0], kidx, cidx, "chip_add_w_in")])
    (full_in,) = _pair_swap_wait("rs_in_pair_swapped", bufs, sems, tok)
    out["w_in"] = _adamw(w_in[0], m_w_in[0], v_w_in[0], full_in.reshape(w_in.shape[1:]), "adamw_w_in")

    def unpack(tab):
        return [tab[0:6].reshape(1, 6 * D_MODEL)] + [tab[6 + r:7 + r] for r in range(5)]

    def ordered(which):
        sm = unpack([gsm, dsm, mosm, vosm][which])
        bg = [out[nm][which][None] for nm in names]
        return [ada_out[which][None], sm[0], sm[1], sm[2], sm[3], sm[4], bg[0], sm[5]] + bg[1:]

    return (loss.reshape(()), dx[None], *ordered(0), *ordered(1), *ordered(2), *ordered(3))
```

```python
import functools

import numpy as np
import jax
import jax.numpy as jnp
from jax import lax
from jax.experimental import pallas as pl
from jax.experimental.pallas import tpu as pltpu

SEQ = 2048
D_MODEL = 1024
D_IN = 6656
D_FF = 4096
N_CHIPS = 4
EPS = 1e-6
ROPE_BASE = 10000.0
RET_BLOCK = 256
RET_CHUNK_SHIFT = 6
SB_BLOCK = 256
QK_SCALE = 0.125
N_PAIRS = 4
SB_GROUP = 2

ADAM_LR = 0.001
ADAM_B1 = 0.9
ADAM_B2 = 0.999
ADAM_EPS = 1e-08
ADAM_WD = 0.01
ADAM_STEP = 10

BF = jnp.bfloat16
F32 = jnp.float32
MESH = pl.DeviceIdType.MESH
VMEM_LIMIT = 56 * 1024 * 1024
ANY = pl.BlockSpec(memory_space=pl.ANY)

C_QR, C_KR, C_VR, C_GR, C_QS, C_KS, C_VS, C_AR, C_AS = 0, 512, 1024, 2048, 3072, 3584, 4096, 4608, 5632

V_SH1, V_SC1, V_GT1, V_SH2, V_SC2, V_GT2, V_G1, V_G2, V_G3, V_G4 = range(10)
P_DSH1, P_DSC1, P_DGT1, P_DSH2, P_DSC2, P_DGT2, P_DG1, P_DG2, P_DG3, P_DG4, P_DGN, P_LOSS = range(12)
N_PAY = 12


def _cp(sem=None, **kw):
    if sem is not None:
        kw["dimension_semantics"] = sem
    return pltpu.CompilerParams(vmem_limit_bytes=VMEM_LIMIT, **kw)


def _dot(a, b):
    return jnp.dot(a, b, preferred_element_type=F32)


def _dot_nt(a, b):
    return lax.dot_general(a, b, (((1,), (1,)), ((), ())), preferred_element_type=F32)


def _dot_tn(a, b):
    return lax.dot_general(a, b, (((0,), (0,)), ((), ())), preferred_element_type=F32)


def _row(ref, i):
    return ref[i:i + 1, :]


def _rms(v):
    return lax.rsqrt(jnp.mean(v * v, axis=1, keepdims=True) + EPS)


def _colsum(v):
    return jnp.sum(v, axis=0, keepdims=True)


def _rowmean(v):
    return jnp.mean(v, axis=1, keepdims=True)


def _sigmoid(v):
    return 1.0 / (1.0 + jnp.exp(-v))


def _cast_bf16(w, kidx, dep, name):
    rows, cols = w.shape
    tr = min(rows, 512)

    def body(k_ref, w_ref, dep_ref, o_ref):
        o_ref[...] = w_ref[...].astype(BF)

    return pl.pallas_call(
        body, name=name,
        grid_spec=pltpu.PrefetchScalarGridSpec(
            num_scalar_prefetch=1, grid=(rows // tr,),
            in_specs=[pl.BlockSpec((tr, cols), lambda i, k_ref: (i, 0)), ANY],
            out_specs=pl.BlockSpec((None, tr, cols), lambda i, k_ref: (k_ref[0], i, 0))),
        out_shape=jax.ShapeDtypeStruct((N_CHIPS, rows, cols), BF),
        compiler_params=_cp(("parallel",)),
    )(kidx, w, dep)


def _adamw_math(w, g, m, v):
    m = ADAM_B1 * m + (1.0 - ADAM_B1) * g
    v = ADAM_B2 * v + (1.0 - ADAM_B2) * (g * g)
    m_hat = m / (1.0 - ADAM_B1 ** ADAM_STEP)
    v_hat = v / (1.0 - ADAM_B2 ** ADAM_STEP)
    delta = -ADAM_LR * (m_hat / (jnp.sqrt(v_hat) + ADAM_EPS) + ADAM_WD * w)
    return delta, m, v


def _adamw(w, m, v, g, name, dep=None):
    rows, cols = w.shape
    tr = min(rows, 256)

    def body(w_ref, m_ref, v_ref, g_ref, go_ref, d_ref, mo_ref, vo_ref):
        gg = g_ref[...]
        d, mm, vv = _adamw_math(w_ref[...], gg, m_ref[...], v_ref[...])
        go_ref[...] = gg
        d_ref[...] = d
        mo_ref[...] = mm
        vo_ref[...] = vv

    spec = pl.BlockSpec((tr, cols), lambda i: (i, 0))
    shp = jax.ShapeDtypeStruct((rows, cols), F32)
    body, in_specs, args = _add_dep(body, [spec] * 4, [w, m, v, g], dep)
    return pl.pallas_call(
        body, name=name, grid=(rows // tr,),
        in_specs=in_specs, out_specs=[spec] * 4, out_shape=[shp] * 4,
        compiler_params=_cp(("parallel",)),
    )(*args)


def _place():
    x, y, c = lax.axis_index("x"), lax.axis_index("y"), lax.axis_index("c")
    return x, y, c


HBM = pl.BlockSpec(memory_space=pltpu.HBM)
SEM = pl.BlockSpec(memory_space=pltpu.SEMAPHORE)
EFFECT = pltpu.SideEffectType.DATAFLOW_SIDE_EFFECTING


def _tie(token, small):
    return small + token[0, 0]


def _add_dep(body, in_specs, args, dep):
    if dep is None:
        return body, list(in_specs), list(args)
    n = len(args)

    def wrapped(*refs):
        body(*refs[:n], *refs[n + 1:])

    return wrapped, list(in_specs) + [ANY], list(args) + [dep]


def _split_call(name, bufs, run, old=None, after=None, new=0):
    nb = len(bufs)
    n_old = 2 if old is not None else 0
    n_in = nb + n_old + (1 if after is not None else 0)

    def body(*refs):
        old_sems = (refs[nb], refs[nb + 1]) if old is not None else None
        new_sems = (refs[n_in], refs[n_in + 1]) if new else None
        run(refs[:nb], old_sems, new_sems)
        if new:
            refs[-1][...] = jnp.zeros_like(refs[-1])

    in_specs = [HBM] * nb + [SEM] * n_old + ([ANY] if after is not None else [])
    out_shape = [pltpu.SemaphoreType.DMA((new,))] * 2 if new else []
    out_specs = [SEM, SEM] if new else []
    out_shape += [pltpu.HBM(b.shape, b.dtype) for b in bufs]
    out_specs += [HBM] * nb
    if new:
        out_shape.append(jax.ShapeDtypeStruct((8, 128), F32))
        out_specs.append(pl.BlockSpec(memory_space=pltpu.VMEM))
    first = 2 if new else 0
    args = [pltpu.with_memory_space_constraint(b, pltpu.HBM) for b in bufs]
    if old is not None:
        args += [old[0], old[1]]
    if after is not None:
        args.append(after)
    outs = pl.pallas_call(
        body, name=name, in_specs=tuple(in_specs), out_specs=tuple(out_specs), out_shape=tuple(out_shape),
        input_output_aliases={i: i + first for i in range(nb)},
        compiler_params=pltpu.CompilerParams(has_side_effects=EFFECT),
    )(*args)
    thru = list(outs[first:first + nb])
    if new:
        return thru, (outs[0], outs[1]), outs[-1]
    return thru, None, None


def _remote(part_src, part_dst, sems, i, to):
    return pltpu.make_async_remote_copy(src_ref=part_src, dst_ref=part_dst, send_sem=sems[0].at[i],
                                        recv_sem=sems[1].at[i], device_id=to, device_id_type=MESH)


def _other_chips(x, y):
    return [(1 - x, y), (x, 1 - y), (1 - x, 1 - y)]


def _gather_start(name, bufs, after=None):
    def run(refs, old, new):
        x, y, c = _place()
        k = 2 * x + y
        for w, ref in enumerate(refs):
            rh = bufs[w].shape[1] // 2
            part = ref.at[k, pl.ds(c * rh, rh)]
            for j, (cx, cy) in enumerate(_other_chips(x, y)):
                _remote(part, part, new, 3 * w + j, (cx, cy, c)).start()

    return _split_call(name, bufs, run, after=after, new=3 * len(bufs))


def _gather_pass(name, bufs, sems, after):
    def run(refs, old, new):
        x, y, c = _place()
        k = 2 * x + y
        sib = (x, y, 1 - c)
        for w, ref in enumerate(refs):
            rh = bufs[w].shape[1] // 2
            for j, (cx, cy) in enumerate(_other_chips(x, y)):
                land = ref.at[2 * cx + cy, pl.ds(c * rh, rh)]
                _remote(land, land, old, 3 * w + j, (cx, cy, c)).wait_recv()
                _remote(land, land, new, 3 * w + j, sib).start()
        for w, ref in enumerate(refs):
            rh = bufs[w].shape[1] // 2
            part = ref.at[k, pl.ds(c * rh, rh)]
            for j, (cx, cy) in enumerate(_other_chips(x, y)):
                _remote(part, part, old, 3 * w + j, (cx, cy, c)).wait_send()

    return _split_call(name, bufs, run, old=sems, after=after, new=3 * len(bufs))


def _gather_finish(name, bufs, sems, after):
    def run(refs, old, new):
        x, y, c = _place()
        sib = (x, y, 1 - c)
        for w, ref in enumerate(refs):
            rh = bufs[w].shape[1] // 2
            for j, (cx, cy) in enumerate(_other_chips(x, y)):
                sent = ref.at[2 * cx + cy, pl.ds(c * rh, rh)]
                _remote(sent, sent, old, 3 * w + j, sib).wait_send()
                land = ref.at[2 * cx + cy, pl.ds((1 - c) * rh, rh)]
                _remote(land, land, old, 3 * w + j, sib).wait_recv()

    return _split_call(name, bufs, run, old=sems, after=after)[0]


def _pair_send_start(name, grads):
    n = len(grads)
    lands = [lax.empty((N_CHIPS,) + g.shape[2:], g.dtype) for g in grads]

    def run(refs, old, new):
        x, y, c = _place()
        for w in range(n):
            _remote(refs[w].at[:, 1 - c], refs[n + w], new, w, (x, y, 1 - c)).start()

    return _split_call(name, list(grads) + lands, run, new=n)


def _pair_send_wait(name, bufs, sems, after):
    n = len(bufs) // 2

    def run(refs, old, new):
        x, y, c = _place()
        for w in range(n):
            cp = _remote(refs[w].at[:, 1 - c], refs[n + w], old, w, (x, y, 1 - c))
            cp.wait_send()
            cp.wait_recv()

    thru = _split_call(name, bufs, run, old=sems, after=after)[0]
    return thru[:n], thru[n:]


def _pair_add(g, recv, cidx, name):
    _, _, rh, cols = g.shape
    tr = min(rh, 256)

    def body(c_ref, g_ref, r_ref, o_ref):
        o_ref[...] = (g_ref[...].astype(F32) + r_ref[...].astype(F32)).astype(BF)

    return pl.pallas_call(
        body, name=name,
        grid_spec=pltpu.PrefetchScalarGridSpec(
            num_scalar_prefetch=1, grid=(rh // tr,),
            in_specs=[pl.BlockSpec((N_CHIPS, None, tr, cols), lambda i, c_ref: (0, c_ref[0], i, 0)),
                      pl.BlockSpec((N_CHIPS, tr, cols), lambda i, c_ref: (0, i, 0))],
            out_specs=pl.BlockSpec((N_CHIPS, tr, cols), lambda i, c_ref: (0, i, 0))),
        out_shape=jax.ShapeDtypeStruct((N_CHIPS, rh, cols), BF),
        compiler_params=_cp(("parallel",)),
    )(cidx, g, recv)


def _chip_send_start(name, sums):
    n = len(sums)
    lands = [lax.empty((3,) + s.shape[1:], BF) for s in sums]

    def run(refs, old, new):
        x, y, c = _place()
        for w in range(n):
            for j, (cx, cy) in enumerate(_other_chips(x, y)):
                _remote(refs[w].at[2 * cx + cy], refs[n + w].at[j], new, 3 * w + j, (cx, cy, c)).start()

    return _split_call(name, list(sums) + lands, run, new=3 * n)


def _chip_send_wait(name, bufs, sems, after):
    n = len(bufs) // 2

    def run(refs, old, new):
        x, y, c = _place()
        for w in range(n):
            for j, (cx, cy) in enumerate(_other_chips(x, y)):
                cp = _remote(refs[w].at[2 * cx + cy], refs[n + w].at[j], old, 3 * w + j, (cx, cy, c))
                cp.wait_send()
                cp.wait_recv()

    thru = _split_call(name, bufs, run, old=sems, after=after)[0]
    return thru[:n], thru[n:]


def _chip_add(own, parts, kidx, cidx, name):
    _, rh, cols = parts.shape
    tr = min(rh, 512)

    def body(k_ref, c_ref, own_ref, p_ref, o_ref):
        acc = own_ref[...].astype(F32)
        for s in range(3):
            acc = acc + p_ref[s].astype(F32)
        o_ref[...] = acc

    return pl.pallas_call(
        body, name=name,
        grid_spec=pltpu.PrefetchScalarGridSpec(
            num_scalar_prefetch=2, grid=(rh // tr,),
            in_specs=[pl.BlockSpec((None, tr, cols), lambda i, k_ref, c_ref: (k_ref[0], i, 0)),
                      pl.BlockSpec((3, tr, cols), lambda i, k_ref, c_ref: (0, i, 0))],
            out_specs=pl.BlockSpec((None, tr, cols), lambda i, k_ref, c_ref: (c_ref[0], i, 0))),
        out_shape=jax.ShapeDtypeStruct((2, rh, cols), F32),
        compiler_params=_cp(("parallel",)),
    )(kidx, cidx, own, parts)


def _pair_swap_start(name, bufs):
    def run(refs, old, new):
        x, y, c = _place()
        for w, ref in enumerate(refs):
            _remote(ref.at[c], ref.at[c], new, w, (x, y, 1 - c)).start()

    return _split_call(name, bufs, run, new=len(bufs))


def _pair_swap_wait(name, bufs, sems, after):
    def run(refs, old, new):
        x, y, c = _place()
        for w, ref in enumerate(refs):
            _remote(ref.at[c], ref.at[c], old, w, (x, y, 1 - c)).wait_send()
            _remote(ref.at[1 - c], ref.at[1 - c], old, w, (x, y, 1 - c)).wait_recv()

    return _split_call(name, bufs, run, old=sems, after=after)[0]


def _peers(x, y, c):
    out = []
    for code in range(1, 8):
        fx, fy, fc = (code >> 2) & 1, (code >> 1) & 1, code & 1
        px = 1 - x if fx else x
        py = 1 - y if fy else y
        pc = 1 - c if fc else c
        out.append((code, (px, py, pc)))
    return out


def _mod_exchange(c_row, ada_w, ada_b4, deps):
    ncol = ada_w.shape[1]

    def body(c_ref, w_ref, b_ref, *rest):
        call_ref, mod_ref, part_ref, send_sems, recv_sems = rest[len(deps):]
        x, y, c = _place()
        k = 2 * x + y
        me = 4 * x + 2 * y + c
        call_ref[pl.ds(me, 1), :] = c_ref[...]
        sends = []
        for code, peer in _peers(x, y, c):
            cp = pltpu.make_async_remote_copy(
                src_ref=c_ref, dst_ref=call_ref.at[pl.ds(me, 1), :],
                send_sem=send_sems.at[code], recv_sem=recv_sems.at[code],
                device_id=peer, device_id_type=MESH)
            cp.start()
            sends.append(cp)
        for code, (px, py, pc) in _peers(x, y, c):
            land = call_ref.at[pl.ds(4 * px + 2 * py + pc, 1), :]
            pltpu.make_async_remote_copy(
                src_ref=land, dst_ref=land, send_sem=send_sems.at[code], recv_sem=recv_sems.at[code],
                device_id=(px, py, pc), device_id_type=MESH).wait_recv()
        call = call_ref[...]
        act = call * _sigmoid(call)
        part = jnp.dot(act, w_ref[...], preferred_element_type=F32,
                       precision=lax.Precision.HIGHEST) + b_ref[pl.ds(k, 1), :]
        part_ref[...] = part
        mod_ref[pl.ds(k, 1), :] = part_ref[pl.ds(me, 1), :]
        chips = [(8 + j, peer) for j, (code, peer) in enumerate(_peers(x, y, c)) if code in (2, 4, 6)]
        for slot, (px, py, pc) in chips:
            cp = pltpu.make_async_remote_copy(
                src_ref=part_ref.at[pl.ds(4 * px + 2 * py + pc, 1), :], dst_ref=mod_ref.at[pl.ds(k, 1), :],
                send_sem=send_sems.at[slot], recv_sem=recv_sems.at[slot],
                device_id=(px, py, pc), device_id_type=MESH)
            cp.start()
            sends.append(cp)
        for slot, (px, py, pc) in chips:
            land = mod_ref.at[pl.ds(2 * px + py, 1), :]
            pltpu.make_async_remote_copy(
                src_ref=land, dst_ref=land, send_sem=send_sems.at[slot], recv_sem=recv_sems.at[slot],
                device_id=(px, py, pc), device_id_type=MESH).wait_recv()
        for cp in sends:
            cp.wait_send()

    vm = pl.BlockSpec(memory_space=pltpu.VMEM)
    return pl.pallas_call(
        body, name="mod_exchange",
        in_specs=[vm, vm, vm] + [ANY] * len(deps), out_specs=[vm, vm],
        out_shape=[jax.ShapeDtypeStruct((8, D_MODEL), F32), jax.ShapeDtypeStruct((N_CHIPS, ncol), F32)],
        scratch_shapes=[pltpu.VMEM((8, ncol), F32), pltpu.SemaphoreType.DMA((16,)),
                        pltpu.SemaphoreType.DMA((16,))],
        compiler_params=_cp(),
    )(c_row, ada_w, ada_b4, *deps)


def _small_exchange(payload, c_all, wsm, msm, vsm):
    ncol = 6 * D_MODEL // N_CHIPS

    def body(p_ref, call_ref, w_ref, m_ref, v_ref, gw_ref, g_ref, d_ref, mo_ref, vo_ref, loss_ref,
             all_ref, dm_ref, send_sems, recv_sems):
        x, y, c = _place()
        k = 2 * x + y
        me = 4 * x + 2 * y + c
        all_ref[:, pl.ds(me, 1), :] = p_ref[...]
        sends = []
        for code, peer in _peers(x, y, c):
            cp = pltpu.make_async_remote_copy(
                src_ref=p_ref, dst_ref=all_ref.at[:, pl.ds(me, 1), :],
                send_sem=send_sems.at[code], recv_sem=recv_sems.at[code],
                device_id=peer, device_id_type=MESH)
            cp.start()
            sends.append(cp)
        for code, (px, py, pc) in _peers(x, y, c):
            land = all_ref.at[:, pl.ds(4 * px + 2 * py + pc, 1), :]
            pltpu.make_async_remote_copy(
                src_ref=land, dst_ref=land, send_sem=send_sems.at[code], recv_sem=recv_sems.at[code],
                device_id=(px, py, pc), device_id_type=MESH).wait_recv()
        for cp in sends:
            cp.wait_send()
        tot = [_colsum(all_ref[r]) for r in range(N_PAY)]
        loss_ref[...] = jnp.sum(tot[P_LOSS], axis=1, keepdims=True)
        g_ref[...] = jnp.zeros_like(g_ref)
        for r in range(P_LOSS):
            g_ref[r:r + 1, :] = tot[r]
        g = g_ref[...]
        d, mm, vv = _adamw_math(w_ref[...], g, m_ref[...], v_ref[...])
        d_ref[...] = d
        mo_ref[...] = mm
        vo_ref[...] = vv
        half = D_MODEL // 2
        for kk in range(N_CHIPS):
            @pl.when(k == kk)
            def _():
                r0 = 3 * (kk // 2)
                if kk % 2 == 0:
                    dm_ref[:, :D_MODEL] = all_ref[r0]
                    dm_ref[:, D_MODEL:] = all_ref[r0 + 1][:, :half]
                else:
                    dm_ref[:, :half] = all_ref[r0 + 1][:, half:]
                    dm_ref[:, half:] = all_ref[r0 + 2]
        call = call_ref[...]
        act = call * _sigmoid(call)
        gw_ref[...] = lax.dot_general(act, dm_ref[...], (((0,), (0,)), ((), ())),
                                      preferred_element_type=F32, precision=lax.Precision.HIGHEST)

    vm = pl.BlockSpec(memory_space=pltpu.VMEM)
    small = jax.ShapeDtypeStruct((16, D_MODEL), F32)
    return pl.pallas_call(
        body, name="small_exchange",
        in_specs=[vm] * 5, out_specs=[vm] * 6,
        out_shape=[jax.ShapeDtypeStruct((D_MODEL, ncol), F32), small, small, small, small,
                   jax.ShapeDtypeStruct((1, 1), F32)],
        scratch_shapes=[pltpu.VMEM((N_PAY, 8, D_MODEL), F32), pltpu.VMEM((8, ncol), F32),
                        pltpu.SemaphoreType.DMA((8,)), pltpu.SemaphoreType.DMA((8,))],
        compiler_params=_cp(),
    )(payload, c_all, wsm, msm, vsm)


def _rope_tables(pos_col, inv_freq, sign):
    def body(p_ref, f_ref, s_ref, cos_ref, sin_ref):
        ang = p_ref[...].astype(F32) * f_ref[...]
        cos_ref[...] = jnp.cos(ang)
        sin_ref[...] = jnp.sin(ang) * s_ref[...]

    tr = 512
    shp = jax.ShapeDtypeStruct((SEQ, 128), F32)
    return pl.pallas_call(
        body, name="rope_tables", grid=(SEQ // tr,),
        in_specs=[pl.BlockSpec((tr, 1), lambda i: (i, 0)), pl.BlockSpec((1, 128), lambda i: (0, 0)),
                  pl.BlockSpec((1, 128), lambda i: (0, 0))],
        out_specs=[pl.BlockSpec((tr, 128), lambda i: (i, 0))] * 2, out_shape=[shp, shp],
        compiler_params=_cp(("parallel",)),
    )(pos_col, inv_freq, sign)


def _resident(shape):
    nd = len(shape)
    return pl.BlockSpec(shape, lambda *_: (0,) * nd, pipeline_mode=pl.Buffered(1))


def _ln_proj(x, vecs, w_in4):
    tm = 256
    wc = w_in4.shape[2]

    def body(x_ref, vec_ref, w_ref, h_ref, proj_ref):
        xx = x_ref[...]
        g = _row(vec_ref, V_G1) * (1.0 + _row(vec_ref, V_SC1))
        h = (xx * _rms(xx) * g + _row(vec_ref, V_SH1)).astype(BF)
        h_ref[...] = h
        for j in range(N_CHIPS):
            proj_ref[:, j * wc:(j + 1) * wc] = _dot(h, w_ref[j]).astype(BF)

    return pl.pallas_call(
        body, name="ln_proj", grid=(SEQ // tm,),
        in_specs=[pl.BlockSpec((tm, D_MODEL), lambda i: (i, 0)), _resident((16, D_MODEL)),
                  _resident(w_in4.shape)],
        out_specs=[pl.BlockSpec((tm, D_MODEL), lambda i: (i, 0)), pl.BlockSpec((tm, D_IN), lambda i: (i, 0))],
        out_shape=[jax.ShapeDtypeStruct((SEQ, D_MODEL), BF), jax.ShapeDtypeStruct((SEQ, D_IN), BF)],
        compiler_params=_cp(("parallel",)),
    )(x, vecs, w_in4)


def _lane_first(shape):
    lane = lax.broadcasted_iota(jnp.int32, shape, 1)
    return (lane & 32) == 0


def _rot(v, cos, sin_s):
    partner = jnp.where(_lane_first(v.shape), pltpu.roll(v, 96, 1), pltpu.roll(v, 32, 1))
    return v * cos + partner * sin_s


def _rot_t(dv, cos, sin_s):
    t = dv * sin_s
    partner = jnp.where(_lane_first(dv.shape), pltpu.roll(t, 96, 1), pltpu.roll(t, 32, 1))
    return dv * cos + partner


def _ret_masks(lg):
    t = RET_BLOCK
    ii = lax.broadcasted_iota(jnp.int32, (t, t), 0)
    jj = lax.broadcasted_iota(jnp.int32, (t, t), 1)
    dist = jnp.abs(ii - jj).astype(F32)
    future = (jj >> RET_CHUNK_SHIFT) > (ii >> RET_CHUNK_SHIFT)
    mask = jnp.where(future, 0.0, jnp.exp(lg * dist))
    ti = lax.broadcasted_iota(jnp.int32, (t, 1), 0).astype(F32)
    from_start = jnp.exp(lg * (ti + 1.0))
    to_end = jnp.exp(lg * (t - 1.0 - ti))
    whole = jnp.exp(jnp.full((1, 128), lg * t, F32))
    return mask, from_start, to_end, whole


def _head_lanes(shape, hh):
    lane = lax.broadcasted_iota(jnp.int32, shape, 1)
    return (lane >> 6) == hh


def _ret_specs():
    t = RET_BLOCK
    return dict(
        q=lambda f: pl.BlockSpec((t, 512), lambda n: (f(n), C_QR // 512)),
        k=lambda f: pl.BlockSpec((t, 512), lambda n: (f(n), C_KR // 512)),
        v=lambda f: pl.BlockSpec((t, D_MODEL), lambda n: (f(n), C_VR // D_MODEL)),
        g=lambda f: pl.BlockSpec((t, D_MODEL), lambda n: (f(n), C_GR // D_MODEL)),
        tab=lambda f: pl.BlockSpec((t, 128), lambda n: (f(n), 0)),
        wide=lambda f: pl.BlockSpec((t, D_MODEL), lambda n: (f(n), 0)),
        state=lambda f: pl.BlockSpec((N_PAIRS, None, 2, 128, 128), lambda n: (0, f(n), 0, 0, 0)),
    )


def _ret_fwd(proj, cos, sin_s, gn_g, log_gamma):
    t = RET_BLOCK
    nb = SEQ // t

    def body(lg_ref, q_ref, k_ref, v_ref, g_ref, cos_ref, sin_ref, gn_ref, o_ref, retg_ref, st_ref, state):
        @pl.when(pl.program_id(0) == 0)
        def _():
            state[...] = jnp.zeros_like(state)

        cos, sn = cos_ref[...], sin_ref[...]
        for p in range(N_PAIRS):
            q = _rot(q_ref[:, 128 * p:128 * (p + 1)].astype(F32), cos, sn)
            k = _rot(k_ref[:, 128 * p:128 * (p + 1)].astype(F32), cos, sn) * QK_SCALE
            for hh in range(2):
                cols = slice(256 * p + 128 * hh, 256 * p + 128 * (hh + 1))
                lg = lg_ref[2 * p + hh]
                mask, from_start, to_end, whole = _ret_masks(lg)
                lanes = _head_lanes(q.shape, hh)
                qm = jnp.where(lanes, q, 0.0)
                km = jnp.where(lanes, k, 0.0)
                vh = v_ref[:, cols]
                sc = _dot_nt(qm.astype(BF), km.astype(BF)) * mask
                st = state[p, hh]
                st_ref[p, hh] = st
                o = _dot(sc.astype(BF), vh) + _dot((qm * from_start).astype(BF), st.astype(BF))
                state[p, hh] = whole * st + _dot_tn((km * to_end).astype(BF), vh)
                d = o - _rowmean(o)
                nh = d * lax.rsqrt(_rowmean(d * d) + EPS)
                gr = g_ref[:, cols].astype(F32)
                o_ref[:, cols] = o
                retg_ref[:, cols] = (gr * _sigmoid(gr) * nh * gn_ref[:, cols]).astype(BF)

    sp = _ret_specs()
    ident = lambda n: n
    return pl.pallas_call(
        body, name="ret_fwd", grid=(nb,),
        in_specs=[pl.BlockSpec(memory_space=pltpu.SMEM), sp["q"](ident), sp["k"](ident), sp["v"](ident),
                  sp["g"](ident), sp["tab"](ident), sp["tab"](ident), _resident((1, D_MODEL))],
        out_specs=[sp["wide"](ident), sp["wide"](ident), sp["state"](ident)],
        out_shape=[jax.ShapeDtypeStruct((SEQ, D_MODEL), F32), jax.ShapeDtypeStruct((SEQ, D_MODEL), BF),
                   jax.ShapeDtypeStruct((N_PAIRS, nb, 2, 128, 128), F32)],
        scratch_shapes=[pltpu.VMEM((N_PAIRS, 2, 128, 128), F32)],
        compiler_params=_cp(("arbitrary",)),
    )(log_gamma, proj, proj, proj, proj, cos, sin_s, gn_g)


def _ret_bwd(proj, cos, sin_s, dret, states, log_gamma):
    t = RET_BLOCK
    nb = SEQ // t

    def body(lg_ref, q_ref, k_ref, v_ref, cos_ref, sin_ref, do_ref, st_ref, dqkv_ref, dstate):
        @pl.when(pl.program_id(0) == 0)
        def _():
            dstate[...] = jnp.zeros_like(dstate)

        cos, sn = cos_ref[...], sin_ref[...]
        for p in range(N_PAIRS):
            q = _rot(q_ref[:, 128 * p:128 * (p + 1)].astype(F32), cos, sn)
            k = _rot(k_ref[:, 128 * p:128 * (p + 1)].astype(F32), cos, sn) * QK_SCALE
            dq_rot = jnp.zeros(q.shape, F32)
            dk_rot = jnp.zeros(q.shape, F32)
            for hh in range(2):
                cols = slice(256 * p + 128 * hh, 256 * p + 128 * (hh + 1))
                lg = lg_ref[2 * p + hh]
                mask, from_start, to_end, whole = _ret_masks(lg)
                lanes = _head_lanes(q.shape, hh)
                qm = jnp.where(lanes, q, 0.0)
                km = jnp.where(lanes, k, 0.0)
                qb, kb = qm.astype(BF), km.astype(BF)
                vh = v_ref[:, cols]
                do = do_ref[:, cols]
                sc = (_dot_nt(qb, kb) * mask).astype(BF)
                st = st_ref[p, hh].astype(BF)
                dst = dstate[p, hh]
                dstb = dst.astype(BF)
                k_end = (km * to_end).astype(BF)
                q_start = (qm * from_start).astype(BF)
                dqkv_ref[:, C_VR + 256 * p + 128 * hh:C_VR + 256 * p + 128 * (hh + 1)] = (
                    _dot_tn(sc, do) + _dot(k_end, dstb)).astype(BF)
                dsc = (_dot_nt(do, vh) * mask).astype(BF)
                dq_h = _dot(dsc, kb) + _dot_nt(do, st) * from_start
                dq_rot = dq_rot + jnp.where(lanes, dq_h, 0.0)
                dk_rot = dk_rot + _dot_tn(dsc, qb) + _dot_nt(vh, dstb) * to_end
                dstate[p, hh] = whole * dst + _dot_tn(q_start, do)
            dqkv_ref[:, C_QR + 128 * p:C_QR + 128 * (p + 1)] = _rot_t(dq_rot, cos, sn).astype(BF)
            dqkv_ref[:, C_KR + 128 * p:C_KR + 128 * (p + 1)] = _rot_t(dk_rot * QK_SCALE, cos, sn).astype(BF)

    sp = _ret_specs()
    rev = lambda n: nb - 1 - n
    return pl.pallas_call(
        body, name="ret_bwd", grid=(nb,),
        in_specs=[pl.BlockSpec(memory_space=pltpu.SMEM), sp["q"](rev), sp["k"](rev), sp["v"](rev),
                  sp["tab"](rev), sp["tab"](rev), sp["wide"](rev), sp["state"](rev)],
        out_specs=pl.BlockSpec((t, C_GR), lambda n: (rev(n), 0)),
        out_shape=jax.ShapeDtypeStruct((SEQ, C_GR), BF),
        scratch_shapes=[pltpu.VMEM((N_PAIRS, 2, 128, 128), F32)],
        compiler_params=_cp(("arbitrary",)),
    )(log_gamma, proj, proj, proj, cos, sin_s, dret, states)


def _stack_heads(v):
    return jnp.concatenate([jnp.where(_head_lanes(v.shape, hh), v, jnp.zeros_like(v)) for hh in range(2)], axis=0)


def _unstack_heads(v):
    t = v.shape[0] // 2
    return jnp.where(_head_lanes((t, v.shape[1]), 0), v[:t], v[t:])


def _sb_masks(t, heads):
    rr = lax.broadcasted_iota(jnp.int32, (t, t), 0)
    cc = lax.broadcasted_iota(jnp.int32, (t, t), 1)
    r2 = lax.broadcasted_iota(jnp.int32, (heads * t, t), 0) & (t - 1)
    c2 = lax.broadcasted_iota(jnp.int32, (heads * t, t), 1)
    return rr, cc, c2 < r2


def _split_dot2(v, tri):
    return _dot(v.astype(BF), tri)


def _log_one_minus_beta(z):
    return -(jnp.maximum(z, 0.0) + jnp.log(1.0 + jnp.exp(-jnp.abs(z))))


def _sb_fwd(proj):
    t, g = SB_BLOCK, SB_GROUP
    nq = SEQ // t
    rows = 2 * g * t

    def body(q_ref, k_ref, v_ref, o_ref, tot_ref, kt_ref):
        i = pl.program_id(1)

        @pl.when(i == 0)
        def _():
            for p in range(g):
                for jj in range(nq):
                    kt_ref[p, jj] = k_ref[jj * t:(jj + 1) * t, 128 * p:128 * (p + 1)].T

        q2 = [_stack_heads((q_ref[:, 128 * p:128 * (p + 1)].astype(F32) * QK_SCALE).astype(BF)) for p in range(g)]
        rr, cc, valid = _sb_masks(t, 2 * g)
        later = (rr > cc).astype(BF)

        def tile(j, carry, diagonal):
            acc, run = carry
            z = jnp.concatenate([_dot(q2[p], kt_ref[p, j]) for p in range(g)], axis=0)
            lm = _log_one_minus_beta(z)
            if diagonal:
                lm = jnp.where(valid, lm, 0.0)
            after = _split_dot2(lm, later)
            a = jnp.exp(z + lm + after + run)
            if diagonal:
                a = jnp.where(valid, a, 0.0)
            ab = a.astype(BF)
            keys = pl.ds(pl.multiple_of(j * t, t), t)
            av = jnp.concatenate([_dot(ab[2 * t * p:2 * t * (p + 1)], v_ref[keys, 128 * p:128 * (p + 1)])
                                  for p in range(g)], axis=0)
            return acc + av, run + after[:, 0:1] + lm[:, 0:1]

        carry = tile(i, (jnp.zeros((rows, 128), F32), jnp.zeros((rows, 1), F32)), True)
        acc, run = lax.fori_loop(0, i, lambda s, cr: tile(i - 1 - s, cr, False), carry)
        run = jnp.broadcast_to(run, (rows, 128))
        for p in range(g):
            o_ref[:, 128 * p:128 * (p + 1)] = _unstack_heads(acc[2 * t * p:2 * t * (p + 1)]).astype(BF)
            tot_ref[:, 128 * p:128 * (p + 1)] = _unstack_heads(run[2 * t * p:2 * t * (p + 1)])

    w = 128 * g
    return pl.pallas_call(
        body, name="sb_fwd", grid=(N_PAIRS // g, nq),
        in_specs=[pl.BlockSpec((t, w), lambda p, i: (i, C_QS // w + p)),
                  pl.BlockSpec((SEQ, w), lambda p, i: (0, C_KS // w + p)),
                  pl.BlockSpec((SEQ, w), lambda p, i: (0, C_VS // w + p))],
        out_specs=[pl.BlockSpec((t, w), lambda p, i: (i, p))] * 2,
        out_shape=[jax.ShapeDtypeStruct((SEQ, 512), BF), jax.ShapeDtypeStruct((SEQ, 512), F32)],
        scratch_shapes=[pltpu.VMEM((g, nq, 128, t), BF)],
        compiler_params=_cp(("parallel", "arbitrary")),
    )(proj, proj, proj)


def _sb_bwd(proj, dsb, tot, dep=None):
    t, g = SB_BLOCK, SB_GROUP
    nq = SEQ // t
    rows = 2 * g * t

    def body(q_ref, k_ref, v_ref, do_ref, tot_ref, dq_ref, dk_ref, dv_ref, kt_ref, vt_ref, dkt_acc, dvt_acc):
        i = pl.program_id(1)

        @pl.when(i == 0)
        def _():
            dkt_acc[...] = jnp.zeros_like(dkt_acc)
            dvt_acc[...] = jnp.zeros_like(dvt_acc)
            for p in range(g):
                for jj in range(nq):
                    kt_ref[p, jj] = k_ref[jj * t:(jj + 1) * t, 128 * p:128 * (p + 1)].T
                    vt_ref[p, jj] = v_ref[jj * t:(jj + 1) * t, 128 * p:128 * (p + 1)].T

        q2 = [_stack_heads((q_ref[:, 128 * p:128 * (p + 1)].astype(F32) * QK_SCALE).astype(BF)) for p in range(g)]
        do2 = [_stack_heads(do_ref[:, 128 * p:128 * (p + 1)]) for p in range(g)]
        q2t = [v.T for v in q2]
        do2t = [v.T for v in do2]
        tots = tot_ref[...]
        total = jnp.concatenate([tots[:, 64 * h:64 * h + 1] for h in range(2 * g)], axis=0)
        rr, cc, valid = _sb_masks(t, 2 * g)
        upto = (rr <= cc).astype(BF)
        before = (rr < cc).astype(BF)

        def part(v, p):
            return v[2 * t * p:2 * t * (p + 1)]

        def tile(j, carry, diagonal):
            dq, run_l, run_g = carry
            z = jnp.concatenate([_dot(q2[p], kt_ref[p, j]) for p in range(g)], axis=0)
            lm = _log_one_minus_beta(z)
            if diagonal:
                lm = jnp.where(valid, lm, 0.0)
            incl = _split_dot2(lm, upto)
            a = jnp.exp(z + lm + (total - (incl + run_l)))
            if diagonal:
                a = jnp.where(valid, a, 0.0)
            gg = a * jnp.concatenate([_dot(do2[p], vt_ref[p, j]) for p in range(g)], axis=0)
            excl = _split_dot2(gg, before)
            dz = gg * jnp.exp(lm) - (excl + run_g) * jnp.exp(z + lm)
            if diagonal:
                dz = jnp.where(valid, dz, 0.0)
            dzb = dz.astype(BF)
            ab = a.astype(BF)
            keys = pl.ds(pl.multiple_of(j * t, t), t)
            for p in range(g):
                dkt_acc[p, j] += _dot(q2t[p], part(dzb, p))
                dvt_acc[p, j] += _dot(do2t[p], part(ab, p))
            dq_t = jnp.concatenate([_dot(part(dzb, p), k_ref[keys, 128 * p:128 * (p + 1)]) for p in range(g)], axis=0)
            return (dq + dq_t, run_l + incl[:, t - 1:t], run_g + excl[:, t - 1:t] + gg[:, t - 1:t])

        zero = jnp.zeros((rows, 1), F32)
        carry = lax.fori_loop(0, i, lambda j, cr: tile(j, cr, False), (jnp.zeros((rows, 128), F32), zero, zero))
        dq = tile(i, carry, True)[0]
        for p in range(g):
            dq_ref[:, 128 * p:128 * (p + 1)] = (_unstack_heads(part(dq, p)) * QK_SCALE).astype(BF)

        @pl.when(i == nq - 1)
        def _():
            for p in range(g):
                for jj in range(nq):
                    dk_ref[jj * t:(jj + 1) * t, 128 * p:128 * (p + 1)] = dkt_acc[p, jj].T.astype(BF)
                    dv_ref[jj * t:(jj + 1) * t, 128 * p:128 * (p + 1)] = dvt_acc[p, jj].T.astype(BF)

    w = 128 * g
    tile_spec = pl.BlockSpec((t, w), lambda p, i: (i, p))
    col_spec = pl.BlockSpec((SEQ, w), lambda p, i: (0, p))
    shp = jax.ShapeDtypeStruct((SEQ, 512), BF)
    body, in_specs, args = _add_dep(
        body, [pl.BlockSpec((t, w), lambda p, i: (i, C_QS // w + p)),
               pl.BlockSpec((SEQ, w), lambda p, i: (0, C_KS // w + p)),
               pl.BlockSpec((SEQ, w), lambda p, i: (0, C_VS // w + p)),
               tile_spec, tile_spec],
        [proj, proj, proj, dsb, tot], dep)
    return pl.pallas_call(
        body, name="sb_bwd", grid=(N_PAIRS // g, nq),
        in_specs=in_specs,
        out_specs=[tile_spec, col_spec, col_spec],
        out_shape=[shp, shp, shp],
        scratch_shapes=[pltpu.VMEM((g, nq, 128, t), BF), pltpu.VMEM((g, nq, 128, t), BF),
                        pltpu.VMEM((g, nq, 128, t), F32), pltpu.VMEM((g, nq, 128, t), F32)],
        compiler_params=_cp(("parallel", "arbitrary")),
    )(*args)


def _mix_out(retg, sb, proj, x, vecs, w_ret, w_sb4, w_out):
    tm, half = 256, 512

    def body(r_ref, s_ref, ar0, ar1, as0, as1, x_ref, vec_ref, wr_ref, ws_ref, wo_ref,
             mix_ref, rb_ref, sbp_ref, y_ref, h1_ref, h2_ref):
        rb = _dot(r_ref[...], wr_ref[...])
        sbv = s_ref[...]
        sbp = jnp.concatenate([_dot(sbv, ws_ref[k]) for k in range(N_CHIPS)], axis=1)
        gate_r = _sigmoid(jnp.concatenate([ar0[...], ar1[...]], axis=1).astype(F32))
        gate_s = _sigmoid(jnp.concatenate([as0[...], as1[...]], axis=1).astype(F32))
        mixed = (gate_r * rb + gate_s * sbp).astype(BF)
        mix_ref[...] = mixed
        rb_ref[...] = rb.astype(BF)
        sbp_ref[...] = sbp.astype(BF)
        y = _dot(mixed, wo_ref[...])
        h1 = x_ref[...] + _row(vec_ref, V_GT1) * (y * _rms(y)) * _row(vec_ref, V_G2)
        g = _row(vec_ref, V_G3) * (1.0 + _row(vec_ref, V_SC2))
        y_ref[...] = y
        h1_ref[...] = h1
        h2_ref[...] = (h1 * _rms(h1) * g + _row(vec_ref, V_SH2)).astype(BF)

    row = pl.BlockSpec((tm, D_MODEL), lambda i: (i, 0))
    gate = lambda c0: pl.BlockSpec((tm, half), lambda i: (i, c0 // half))
    bf = jax.ShapeDtypeStruct((SEQ, D_MODEL), BF)
    f32 = jax.ShapeDtypeStruct((SEQ, D_MODEL), F32)
    return pl.pallas_call(
        body, name="mix_out", grid=(SEQ // tm,),
        in_specs=[row, pl.BlockSpec((tm, 512), lambda i: (i, 0)), gate(C_AR), gate(C_AR + half), gate(C_AS),
                  gate(C_AS + half), row, _resident((16, D_MODEL)), _resident((D_MODEL, D_MODEL)),
                  _resident(w_sb4.shape), _resident((D_MODEL, D_MODEL))],
        out_specs=[row] * 6, out_shape=[bf, bf, bf, f32, f32, bf],
        compiler_params=_cp(("parallel",)),
    )(retg, sb, proj, proj, proj, proj, x, vecs, w_ret, w_sb4, w_out)


def _ffn_fwd_loss(h2, h1, target, vecs, w_ff14, w_ff2):
    tm = 256

    def body(h2_ref, h1_ref, t_ref, vec_ref, w1_ref, w2_ref, u_ref, a_ref, dout_ref, df_ref, st_ref):
        @pl.when(pl.program_id(0) == 0)
        def _():
            st_ref[...] = jnp.zeros_like(st_ref)

        hb = h2_ref[...]
        f = jnp.zeros((tm, D_MODEL), F32)
        for k in range(N_CHIPS):
            cols = slice(k * D_MODEL, (k + 1) * D_MODEL)
            u = _dot(hb, w1_ref[k])
            r = jnp.maximum(u, 0.0)
            act = (r * r).astype(BF)
            u_ref[:, cols] = u.astype(BF)
            a_ref[:, cols] = act
            f = f + _dot(act, w2_ref[cols, :])
        r4 = _rms(f)
        fn = f * r4
        gt2, g4 = _row(vec_ref, V_GT2), _row(vec_ref, V_G4)
        diff = h1_ref[...] + gt2 * fn * g4 - t_ref[...]
        dout = diff * (1.0 / D_MODEL)
        dfn = dout * gt2 * g4
        dout_ref[...] = dout
        df_ref[...] = (r4 * (dfn - fn * _rowmean(dfn * fn))).astype(BF)
        st_ref[0:1, :] += _colsum(dout * fn * g4)
        st_ref[1:2, :] += _colsum(dout * gt2 * fn)
        st_ref[2:3, :] += _colsum(diff * diff) * (0.5 / D_MODEL)

    row = pl.BlockSpec((tm, D_MODEL), lambda i: (i, 0))
    wide = pl.BlockSpec((tm, D_FF), lambda i: (i, 0))
    return pl.pallas_call(
        body, name="ffn_fwd_loss", grid=(SEQ // tm,),
        in_specs=[row, row, row, _resident((16, D_MODEL)), _resident(w_ff14.shape), _resident(w_ff2.shape)],
        out_specs=[wide, wide, row, row, pl.BlockSpec((8, D_MODEL), lambda i: (0, 0))],
        out_shape=[jax.ShapeDtypeStruct((SEQ, D_FF), BF), jax.ShapeDtypeStruct((SEQ, D_FF), BF),
                   jax.ShapeDtypeStruct((SEQ, D_MODEL), F32), jax.ShapeDtypeStruct((SEQ, D_MODEL), BF),
                   jax.ShapeDtypeStruct((8, D_MODEL), F32)],
        compiler_params=_cp(("arbitrary",)),
    )(h2, h1, target, vecs, w_ff14, w_ff2)


def _ffn_bwd(df, u, h1, y, dout, vecs, w_ff14, w_ff2):
    tm = 256

    def body(df_ref, u_ref, h1_ref, y_ref, dout_ref, vec_ref, w1_ref, w2_ref, du_ref, dh1_ref, dy_ref, st_ref):
        @pl.when(pl.program_id(0) == 0)
        def _():
            st_ref[...] = jnp.zeros_like(st_ref)

        dfb = df_ref[...]
        dh2 = jnp.zeros((tm, D_MODEL), F32)
        for k in range(N_CHIPS):
            cols = slice(k * D_MODEL, (k + 1) * D_MODEL)
            da = _dot_nt(dfb, w2_ref[cols, :])
            du = (da * (2.0 * jnp.maximum(u_ref[:, cols].astype(F32), 0.0))).astype(BF)
            du_ref[:, cols] = du
            dh2 = dh2 + _dot_nt(du, w1_ref[k])
        h1 = h1_ref[...]
        r3 = _rms(h1)
        hn3 = h1 * r3
        g3, sc2 = _row(vec_ref, V_G3), _row(vec_ref, V_SC2)
        dhn3 = dh2 * g3 * (1.0 + sc2)
        dh1 = dout_ref[...] + r3 * (dhn3 - hn3 * _rowmean(dhn3 * hn3))
        y = y_ref[...]
        r2 = _rms(y)
        yn = y * r2
        gt1, g2 = _row(vec_ref, V_GT1), _row(vec_ref, V_G2)
        dyn = dh1 * gt1 * g2
        dh1_ref[...] = dh1
        dy_ref[...] = (r2 * (dyn - yn * _rowmean(dyn * yn))).astype(BF)
        st_ref[0:1, :] += _colsum(dh2)
        st_ref[1:2, :] += _colsum(dh2 * hn3 * g3)
        st_ref[2:3, :] += _colsum(dh2 * hn3 * (1.0 + sc2))
        st_ref[3:4, :] += _colsum(dh1 * yn * g2)
        st_ref[4:5, :] += _colsum(dh1 * gt1 * yn)

    row = pl.BlockSpec((tm, D_MODEL), lambda i: (i, 0))
    wide = pl.BlockSpec((tm, D_FF), lambda i: (i, 0))
    return pl.pallas_call(
        body, name="ffn_bwd", grid=(SEQ // tm,),
        in_specs=[row, wide, row, row, row, _resident((16, D_MODEL)), _resident(w_ff14.shape),
                  _resident(w_ff2.shape)],
        out_specs=[wide, row, row, pl.BlockSpec((8, D_MODEL), lambda i: (0, 0))],
        out_shape=[jax.ShapeDtypeStruct((SEQ, D_FF), BF), jax.ShapeDtypeStruct((SEQ, D_MODEL), F32),
                   jax.ShapeDtypeStruct((SEQ, D_MODEL), BF), jax.ShapeDtypeStruct((8, D_MODEL), F32)],
        compiler_params=_cp(("arbitrary",)),
    )(df, u, h1, y, dout, vecs, w_ff14, w_ff2)


def _mix_ret_bwd(dy, proj, rb, sbp, o_raw, gn_g, w_out, w_sb4, w_ret):
    tm, half = 256, 512

    def body(dy_ref, ar0, ar1, as0, as1, rb_ref, sbp_ref, g_ref, o_ref, gn_ref, wo_ref, ws_ref, wr_ref,
             drb_ref, dsbp_ref, da_ref, dsb_ref, dret_ref, dgr_ref, st_ref):
        @pl.when(pl.program_id(0) == 0)
        def _():
            st_ref[...] = jnp.zeros_like(st_ref)

        dm_all = _dot_nt(dy_ref[...], wo_ref[...])
        dsb = jnp.zeros((tm, 512), F32)
        drbs = []
        for hf, (ar_ref, as_ref) in enumerate(((ar0, as0), (ar1, as1))):
            cols = slice(half * hf, half * (hf + 1))
            dm = dm_all[:, cols]
            sr = _sigmoid(ar_ref[...].astype(F32))
            ss = _sigmoid(as_ref[...].astype(F32))
            dsbp = (dm * ss).astype(BF)
            drbs.append((dm * sr).astype(BF))
            dsbp_ref[:, cols] = dsbp
            da_ref[:, cols] = (dm * rb_ref[:, cols].astype(F32) * sr * (1.0 - sr)).astype(BF)
            da_ref[:, D_MODEL + half * hf:D_MODEL + half * (hf + 1)] = (
                dm * sbp_ref[:, cols].astype(F32) * ss * (1.0 - ss)).astype(BF)
            dsb = dsb + _dot_nt(dsbp[:, :256], ws_ref[2 * hf]) + _dot_nt(dsbp[:, 256:], ws_ref[2 * hf + 1])
        dsb_ref[...] = dsb.astype(BF)
        drb = jnp.concatenate(drbs, axis=1)
        drb_ref[...] = drb
        dretg = _dot_nt(drb, wr_ref[...])
        for gi in range(D_MODEL // 128):
            cols = slice(128 * gi, 128 * (gi + 1))
            o = o_ref[:, cols]
            d = o - _rowmean(o)
            rstd = lax.rsqrt(_rowmean(d * d) + EPS)
            nh = d * rstd
            gain = gn_ref[:, cols]
            gr = g_ref[:, cols].astype(F32)
            sg = _sigmoid(gr)
            dg = dretg[:, cols]
            dgn = dg * gr * sg
            dnh = dgn * gain
            dgr_ref[:, cols] = (dg * nh * gain * sg * (1.0 + gr * (1.0 - sg))).astype(BF)
            dret_ref[:, cols] = (rstd * (dnh - _rowmean(dnh) - nh * _rowmean(dnh * nh))).astype(BF)
            st_ref[0:1, cols] += _colsum(dgn * nh)

    row = pl.BlockSpec((tm, D_MODEL), lambda i: (i, 0))
    gate = lambda c0: pl.BlockSpec((tm, half), lambda i: (i, c0 // half))
    shp = jax.ShapeDtypeStruct((SEQ, D_MODEL), BF)
    return pl.pallas_call(
        body, name="mix_ret_bwd", grid=(SEQ // tm,),
        in_specs=[row, gate(C_AR), gate(C_AR + half), gate(C_AS), gate(C_AS + half), row, row,
                  pl.BlockSpec((tm, D_MODEL), lambda i: (i, C_GR // D_MODEL)), row, _resident((1, D_MODEL)),
                  _resident((D_MODEL, D_MODEL)), _resident(w_sb4.shape), _resident((D_MODEL, D_MODEL))],
        out_specs=[row, row, pl.BlockSpec((tm, 2 * D_MODEL), lambda i: (i, 0)), pl.BlockSpec((tm, 512), lambda i: (i, 0)),
                   row, row, pl.BlockSpec((8, D_MODEL), lambda i: (0, 0))],
        out_shape=[shp, shp, jax.ShapeDtypeStruct((SEQ, 2 * D_MODEL), BF), jax.ShapeDtypeStruct((SEQ, 512), BF),
                   shp, shp, jax.ShapeDtypeStruct((8, D_MODEL), F32)],
        compiler_params=_cp(("arbitrary",)),
    )(dy, proj, proj, proj, proj, rb, sbp, proj, o_raw, gn_g, w_out, w_sb4, w_ret)


def _dproj_segments(widths):
    wc = D_IN // N_CHIPS
    segs, start = [], 0
    for pi, width in enumerate(widths):
        lo = start
        while lo < start + width:
            j = lo // wc
            hi = min(start + width, (j + 1) * wc)
            segs.append((j, lo - j * wc, pi, lo - start, hi - lo))
            lo = hi
        start += width
    assert start == D_IN
    return segs


def _in_proj_bwd(pieces, x, dh1, vecs, w_in4):
    tm = 256
    n = len(pieces)
    segs = _dproj_segments([p.shape[1] for p in pieces])

    def body(*refs):
        x_ref, dh1_ref, vec_ref, w_ref, dx_ref, st_ref = refs[n:]

        @pl.when(pl.program_id(0) == 0)
        def _():
            st_ref[...] = jnp.zeros_like(st_ref)

        dh = jnp.zeros((tm, D_MODEL), F32)
        for j, so, pi, po, width in segs:
            dh = dh + _dot_nt(refs[pi][:, po:po + width], w_ref[j, :, so:so + width])
        xx = x_ref[...]
        r1 = _rms(xx)
        xn = xx * r1
        g1, sc1 = _row(vec_ref, V_G1), _row(vec_ref, V_SC1)
        dxn = dh * g1 * (1.0 + sc1)
        dx_ref[...] = dh1_ref[...] + r1 * (dxn - xn * _rowmean(dxn * xn))
        st_ref[0:1, :] += _colsum(dh)
        st_ref[1:2, :] += _colsum(dh * xn * g1)
        st_ref[2:3, :] += _colsum(dh * xn * (1.0 + sc1))

    row = pl.BlockSpec((tm, D_MODEL), lambda i: (i, 0))
    return pl.pallas_call(
        body, name="in_proj_bwd", grid=(SEQ // tm,),
        in_specs=[pl.BlockSpec((tm, p.shape[1]), lambda i: (i, 0)) for p in pieces] + [
            row, row, _resident((16, D_MODEL)), _resident(w_in4.shape)],
        out_specs=[row, pl.BlockSpec((8, D_MODEL), lambda i: (0, 0))],
        out_shape=[jax.ShapeDtypeStruct((SEQ, D_MODEL), F32), jax.ShapeDtypeStruct((8, D_MODEL), F32)],
        compiler_params=_cp(("arbitrary",)),
    )(*pieces, x, dh1, vecs, w_in4)


def _grad_w_in(h, pieces, dep=None):
    ta = 256
    n = len(pieces)
    segs = _dproj_segments([p.shape[1] for p in pieces])

    def body(*refs):
        h_ref, o_ref = refs[n], refs[n + 1]
        hh = h_ref[...]
        for j, so, pi, po, width in segs:
            o_ref[j, :, so:so + width] = _dot_tn(hh, refs[pi][:, po:po + width]).astype(BF)

    body, in_specs, args = _add_dep(
        body, [_resident(p.shape) for p in pieces] + [pl.BlockSpec((SEQ, ta), lambda i: (0, i))],
        list(pieces) + [h], dep)
    return pl.pallas_call(
        body, name="grad_w_in", grid=(D_MODEL // ta,),
        in_specs=in_specs,
        out_specs=pl.BlockSpec((N_CHIPS, ta, D_IN // N_CHIPS), lambda i: (0, i, 0)),
        out_shape=jax.ShapeDtypeStruct((N_CHIPS, D_MODEL, D_IN // N_CHIPS), BF),
        compiler_params=_cp(("parallel",)),
    )(*args)


def _weight_grad(a, b, ta, tb, col_sharded, name, dep=None):
    ka, nb_ = a.shape[1], b.shape[1]

    def body(a_ref, b_ref, o_ref):
        o_ref[...] = _dot_tn(a_ref[...], b_ref[...]).astype(BF)

    body, in_specs, args = _add_dep(
        body, [pl.BlockSpec((SEQ, ta), lambda i, j: (0, i)), pl.BlockSpec((SEQ, tb), lambda i, j: (0, j))],
        [a, b], dep)

    if col_sharded:
        per = nb_ // N_CHIPS // tb
        out_shape = jax.ShapeDtypeStruct((N_CHIPS, ka, nb_ // N_CHIPS), BF)
        out_spec = pl.BlockSpec((None, ta, tb), lambda i, j: (j // per, i, j % per))
    else:
        per = ka // N_CHIPS // ta
        out_shape = jax.ShapeDtypeStruct((N_CHIPS, ka // N_CHIPS, nb_), BF)
        out_spec = pl.BlockSpec((None, ta, tb), lambda i, j: (i // per, i % per, j))
    return pl.pallas_call(
        body, name=name, grid=(ka // ta, nb_ // tb),
        in_specs=in_specs, out_specs=out_spec, out_shape=out_shape,
        compiler_params=_cp(("parallel", "parallel")),
    )(*args)


def _rope_constants():
    freq = np.float32(ROPE_BASE) ** (-np.arange(0, 64, 2, dtype=np.float32) / np.float32(64))
    inv = np.tile(freq.astype(np.float32), 4).reshape(1, 128)
    sign = np.tile(np.concatenate([-np.ones(32, np.float32), np.ones(32, np.float32)]), 2).reshape(1, 128)
    return jnp.asarray(inv), jnp.asarray(sign)


def _log_gamma():
    return jnp.asarray(np.log1p(-(2.0 ** (-5.0 - np.arange(8, dtype=np.float64)))).astype(np.float32))


def _halves(g):
    return g.reshape(N_CHIPS, 2, g.shape[1] // 2, g.shape[2])


def kernel(x, c, positions, ada_w, ada_b, pre_mix_g, post_mix_g, pre_ffn_g, post_ffn_g, w_in, ret_gn_g, w_ret_branch, w_sb_branch, w_out, w_ff1, w_ff2, loss_target, m_ada_w, m_ada_b, m_pre_mix_g, m_post_mix_g, m_pre_ffn_g, m_post_ffn_g, m_w_in, m_ret_gn_g, m_w_ret_branch, m_w_sb_branch, m_w_out, m_w_ff1, m_w_ff2, v_ada_w, v_ada_b, v_pre_mix_g, v_post_mix_g, v_pre_ffn_g, v_post_ffn_g, v_w_in, v_ret_gn_g, v_w_ret_branch, v_w_sb_branch, v_w_out, v_w_ff1, v_w_ff2):
    names = ["w_in", "w_ret", "w_sb", "w_out", "w_ff1", "w_ff2"]
    big = dict(zip(names, [w_in, w_ret_branch, w_sb_branch, w_out, w_ff1, w_ff2]))
    big_m = dict(zip(names, [m_w_in, m_w_ret_branch, m_w_sb_branch, m_w_out, m_w_ff1, m_w_ff2]))
    big_v = dict(zip(names, [v_w_in, v_w_ret_branch, v_w_sb_branch, v_w_out, v_w_ff1, v_w_ff2]))
    rest = names[1:]
    cidx = lax.axis_index("c").astype(jnp.int32).reshape(1)
    kidx = (2 * lax.axis_index("x") + lax.axis_index("y")).astype(jnp.int32).reshape(1)
    x0, target = x[0], loss_target[0]

    buf_in, sem_in, tok_in = _gather_start("gather_in_start", [_cast_bf16(w_in[0], kidx, c, "cast_w_in")])
    rest_bufs = [_cast_bf16(big[nm][0], kidx, tok_in, "cast_" + nm) for nm in rest]
    inv_freq, sign = _rope_constants()
    lg = _log_gamma()
    cos, sin_s = _rope_tables(positions.reshape(SEQ, 1), _tie(tok_in, inv_freq), sign)

    def table(b6, g5):
        return jnp.concatenate([b6.reshape(6, D_MODEL)] + g5 + [jnp.zeros((5, D_MODEL), F32)], axis=0)

    wsm = table(ada_b, [pre_mix_g, post_mix_g, pre_ffn_g, post_ffn_g, ret_gn_g])
    msm = table(m_ada_b, [m_pre_mix_g, m_post_mix_g, m_pre_ffn_g, m_post_ffn_g, m_ret_gn_g])
    vsm = table(v_ada_b, [v_pre_mix_g, v_post_mix_g, v_pre_ffn_g, v_post_ffn_g, v_ret_gn_g])
    c_all, mod4 = _mod_exchange(c, ada_w[0], ada_b.reshape(N_CHIPS, -1), rest_bufs + [cos, wsm, msm, vsm])
    vecs = jnp.concatenate([mod4.reshape(6, D_MODEL), pre_mix_g, post_mix_g, pre_ffn_g, post_ffn_g,
                            jnp.zeros((6, D_MODEL), F32)], axis=0)
    buf_in, sem_in, tok_in = _gather_pass("gather_in_pass", buf_in, sem_in, vecs)
    buf_rest, sem_rest, tok_rest = _gather_start("gather_rest_start", rest_bufs, after=tok_in)
    (w_in4,) = _gather_finish("gather_in_finish", buf_in, sem_in, tok_rest)

    h, proj = _ln_proj(x0, vecs, w_in4)
    sb, tot = _sb_fwd(proj)
    buf_rest, sem_rest, tok_rest = _gather_pass("gather_rest_pass", buf_rest, sem_rest, sb)
    o_raw, retg, states = _ret_fwd(proj, cos, sin_s, _tie(tok_rest, ret_gn_g), lg)
    w_ret4, w_sb4, w_out4, w_ff14, w_ff24 = _gather_finish("gather_rest_finish", buf_rest, sem_rest, retg)
    w_ret = w_ret4.reshape(D_MODEL, D_MODEL)
    w_out2 = w_out4.reshape(D_MODEL, D_MODEL)
    w_ff2_2 = w_ff24.reshape(D_FF, D_MODEL)
    mixed, rb, sbp, y, h1, h2 = _mix_out(retg, sb, proj, x0, vecs, w_ret, w_sb4, w_out2)
    u, act, dout, df, st_a = _ffn_fwd_loss(h2, h1, target, vecs, w_ff14, w_ff2_2)

    du, dh1, dy, st_b = _ffn_bwd(df, u, h1, y, dout, vecs, w_ff14, w_ff2_2)
    grads = {"w_ff2": _weight_grad(act, df, 512, 1024, False, "grad_w_ff2")}
    grads["w_ff1"] = _weight_grad(h2, du, 512, 1024, True, "grad_w_ff1")
    drb, dsbp, da, dsb, dret, dg_r, st_c = _mix_ret_bwd(dy, proj, rb, sbp, o_raw, ret_gn_g, w_out2, w_sb4, w_ret)
    grads["w_out"] = _weight_grad(mixed, dy, 256, 1024, False, "grad_w_out")
    grads["w_ret"] = _weight_grad(retg, drb, 256, 1024, False, "grad_w_ret")
    grads["w_sb"] = _weight_grad(sb, dsbp, 512, 256, True, "grad_w_sb")

    bufs, sems, tok = _pair_send_start("rs_rest_pair_send", [_halves(grads[nm]) for nm in rest])
    dqkv_r = _ret_bwd(proj, cos, sin_s, dret, states, _tie(tok, lg))
    mine, theirs = _pair_send_wait("rs_rest_pair_recv", bufs, sems, dqkv_r)
    pair_sums = [_pair_add(g, r, cidx, "pair_add_" + nm) for g, r, nm in zip(mine, theirs, rest)]
    bufs, sems, tok = _chip_send_start("rs_rest_chip_send", pair_sums)
    dq_s, dk_s, dv_s = _sb_bwd(proj, dsb, tot, dep=tok)
    own, parts = _chip_send_wait("rs_rest_chip_recv", bufs, sems, dq_s)
    sums = [_chip_add(o, p, kidx, cidx, "chip_add_" + nm) for o, p, nm in zip(own, parts, rest)]
    bufs, sems, tok = _pair_swap_start("rs_rest_pair_swap", sums)
    dproj = [dqkv_r, dg_r, dq_s, dk_s, dv_s, da]
    g_in = _grad_w_in(h, dproj, dep=tok)
    full_rest = _pair_swap_wait("rs_rest_pair_swapped", bufs, sems, g_in)

    def update(nm, g, dep):
        w = big[nm][0]
        return _adamw(w, big_m[nm][0], big_v[nm][0], g.reshape(w.shape), "adamw_" + nm, dep=dep)

    full_rest = dict(zip(rest, full_rest))
    bufs, sems, tok = _pair_send_start("rs_in_pair_send", [_halves(g_in)])
    out = {nm: update(nm, full_rest[nm], tok) for nm in ("w_ff2", "w_ff1")}
    mine, theirs = _pair_send_wait("rs_in_pair_recv", bufs, sems, out["w_ff1"][1])
    bufs, sems, tok = _chip_send_start("rs_in_chip_send", [_pair_add(mine[0], theirs[0], cidx, "pair_add_w_in")])
    dx, st_d = _in_proj_bwd(dproj, x0, dh1, _tie(tok, vecs), w_in4)
    out.update({nm: update(nm, full_rest[nm], dx) for nm in ("w_out", "w_sb", "w_ret")})

    payload = jnp.concatenate([
        st_d[0:2], st_b[3:4], st_b[0:2], st_a[0:1],
        st_d[2:3], st_b[4:5], st_b[2:3], st_a[1:2],
        st_c[0:1], st_a[2:3]], axis=0)

    g_ada, gsm, dsm, mosm, vosm, loss = _small_exchange(
        payload.reshape(N_PAY, 1, D_MODEL), c_all, wsm, msm, vsm)
    ada_out = _adamw(ada_w[0], m_ada_w[0], v_ada_w[0], g_ada, "adamw_ada_w")

    own, parts = _chip_send_wait("rs_in_chip_recv", bufs, sems, ada_out[1])
    bufs, sems, tok = _pair_swap_start(
        "rs_in_pair_swap", [_chip_add(own[0], parts[0], kidx, cidx, "chip_add_w_in")])
    (full_in,) = _pair_swap_wait("rs_in_pair_swapped", bufs, sems, tok)
    out["w_in"] = _adamw(w_in[0], m_w_in[0], v_w_in[0], full_in.reshape(w_in.shape[1:]), "adamw_w_in")

    def unpack(tab):
        return [tab[0:6].reshape(1, 6 * D_MODEL)] + [tab[6 + r:7 + r] for r in range(5)]

    def ordered(which):
        sm = unpack([gsm, dsm, mosm, vosm][which])
        bg = [out[nm][which][None] for nm in names]
        return [ada_out[which][None], sm[0], sm[1], sm[2], sm[3], sm[4], bg[0], sm[5]] + bg[1:]

    return (loss.reshape(()), dx[None], *ordered(0), *ordered(1), *ordered(2), *ordered(3))
```

```python
import functools

import numpy as np
import jax
import jax.numpy as jnp
from jax import lax
from jax.experimental import pallas as pl
from jax.experimental.pallas import tpu as pltpu

SEQ = 2048
D_MODEL = 1024
D_IN = 6656
D_FF = 4096
N_CHIPS = 4
EPS = 1e-6
ROPE_BASE = 10000.0
RET_BLOCK = 256
RET_CHUNK_SHIFT = 6
SB_BLOCK = 256
QK_SCALE = 0.125
N_PAIRS = 4
SB_GROUP = 2

ADAM_LR = 0.001
ADAM_B1 = 0.9
ADAM_B2 = 0.999
ADAM_EPS = 1e-08
ADAM_WD = 0.01
ADAM_STEP = 10

BF = jnp.bfloat16
F32 = jnp.float32
MESH = pl.DeviceIdType.MESH
VMEM_LIMIT = 56 * 1024 * 1024
ANY = pl.BlockSpec(memory_space=pl.ANY)

C_QR, C_KR, C_VR, C_GR, C_QS, C_KS, C_VS, C_AR, C_AS = 0, 512, 1024, 2048, 3072, 3584, 4096, 4608, 5632

V_SH1, V_SC1, V_GT1, V_SH2, V_SC2, V_GT2, V_G1, V_G2, V_G3, V_G4 = range(10)
P_DSH1, P_DSC1, P_DGT1, P_DSH2, P_DSC2, P_DGT2, P_DG1, P_DG2, P_DG3, P_DG4, P_DGN, P_LOSS = range(12)
N_PAY = 12


def _cp(sem=None, **kw):
    if sem is not None:
        kw["dimension_semantics"] = sem
    return pltpu.CompilerParams(vmem_limit_bytes=VMEM_LIMIT, **kw)


def _dot(a, b):
    return jnp.dot(a, b, preferred_element_type=F32)


def _dot_nt(a, b):
    return lax.dot_general(a, b, (((1,), (1,)), ((), ())), preferred_element_type=F32)


def _dot_tn(a, b):
    return lax.dot_general(a, b, (((0,), (0,)), ((), ())), preferred_element_type=F32)


def _row(ref, i):
    return ref[i:i + 1, :]


def _rms(v):
    return lax.rsqrt(jnp.mean(v * v, axis=1, keepdims=True) + EPS)


def _colsum(v):
    return jnp.sum(v, axis=0, keepdims=True)


def _rowmean(v):
    return jnp.mean(v, axis=1, keepdims=True)


def _sigmoid(v):
    return 1.0 / (1.0 + jnp.exp(-v))


def _cast_bf16(w, kidx, dep, name):
    rows, cols = w.shape
    tr = min(rows, 512)

    def body(k_ref, w_ref, dep_ref, o_ref):
        o_ref[...] = w_ref[...].astype(BF)

    return pl.pallas_call(
        body, name=name,
        grid_spec=pltpu.PrefetchScalarGridSpec(
            num_scalar_prefetch=1, grid=(rows // tr,),
            in_specs=[pl.BlockSpec((tr, cols), lambda i, k_ref: (i, 0)), ANY],
            out_specs=pl.BlockSpec((None, tr, cols), lambda i, k_ref: (k_ref[0], i, 0))),
        out_shape=jax.ShapeDtypeStruct((N_CHIPS, rows, cols), BF),
        compiler_params=_cp(("parallel",)),
    )(kidx, w, dep)


def _adamw_math(w, g, m, v):
    m = ADAM_B1 * m + (1.0 - ADAM_B1) * g
    v = ADAM_B2 * v + (1.0 - ADAM_B2) * (g * g)
    m_hat = m / (1.0 - ADAM_B1 ** ADAM_STEP)
    v_hat = v / (1.0 - ADAM_B2 ** ADAM_STEP)
    delta = -ADAM_LR * (m_hat / (jnp.sqrt(v_hat) + ADAM_EPS) + ADAM_WD * w)
    return delta, m, v


def _adamw(w, m, v, g, name, dep=None):
    rows, cols = w.shape
    tr = min(rows, 256)

    def body(w_ref, m_ref, v_ref, g_ref, go_ref, d_ref, mo_ref, vo_ref):
        gg = g_ref[...]
        d, mm, vv = _adamw_math(w_ref[...], gg, m_ref[...], v_ref[...])
        go_ref[...] = gg
        d_ref[...] = d
        mo_ref[...] = mm
        vo_ref[...] = vv

    spec = pl.BlockSpec((tr, cols), lambda i: (i, 0))
    shp = jax.ShapeDtypeStruct((rows, cols), F32)
    body, in_specs, args = _add_dep(body, [spec] * 4, [w, m, v, g], dep)
    return pl.pallas_call(
        body, name=name, grid=(rows // tr,),
        in_specs=in_specs, out_specs=[spec] * 4, out_shape=[shp] * 4,
        compiler_params=_cp(("parallel",)),
    )(*args)


def _place():
    x, y, c = lax.axis_index("x"), lax.axis_index("y"), lax.axis_index("c")
    return x, y, c


HBM = pl.BlockSpec(memory_space=pltpu.HBM)
SEM = pl.BlockSpec(memory_space=pltpu.SEMAPHORE)
EFFECT = pltpu.SideEffectType.DATAFLOW_SIDE_EFFECTING


def _tie(token, small):
    return small + token[0, 0]


def _add_dep(body, in_specs, args, dep):
    if dep is None:
        return body, list(in_specs), list(args)
    n = len(args)

    def wrapped(*refs):
        body(*refs[:n], *refs[n + 1:])

    return wrapped, list(in_specs) + [ANY], list(args) + [dep]


def _split_call(name, bufs, run, old=None, after=None, new=0):
    nb = len(bufs)
    n_old = 2 if old is not None else 0
    n_in = nb + n_old + (1 if after is not None else 0)

    def body(*refs):
        old_sems = (refs[nb], refs[nb + 1]) if old is not None else None
        new_sems = (refs[n_in], refs[n_in + 1]) if new else None
        run(refs[:nb], old_sems, new_sems)
        if new:
            refs[-1][...] = jnp.zeros_like(refs[-1])

    in_specs = [HBM] * nb + [SEM] * n_old + ([ANY] if after is not None else [])
    out_shape = [pltpu.SemaphoreType.DMA((new,))] * 2 if new else []
    out_specs = [SEM, SEM] if new else []
    out_shape += [pltpu.HBM(b.shape, b.dtype) for b in bufs]
    out_specs += [HBM] * nb
    if new:
        out_shape.append(jax.ShapeDtypeStruct((8, 128), F32))
        out_specs.append(pl.BlockSpec(memory_space=pltpu.VMEM))
    first = 2 if new else 0
    args = [pltpu.with_memory_space_constraint(b, pltpu.HBM) for b in bufs]
    if old is not None:
        args += [old[0], old[1]]
    if after is not None:
        args.append(after)
    outs = pl.pallas_call(
        body, name=name, in_specs=tuple(in_specs), out_specs=tuple(out_specs), out_shape=tuple(out_shape),
        input_output_aliases={i: i + first for i in range(nb)},
        compiler_params=pltpu.CompilerParams(has_side_effects=EFFECT),
    )(*args)
    thru = list(outs[first:first + nb])
    if new:
        return thru, (outs[0], outs[1]), outs[-1]
    return thru, None, None


def _remote(part_src, part_dst, sems, i, to):
    return pltpu.make_async_remote_copy(src_ref=part_src, dst_ref=part_dst, send_sem=sems[0].at[i],
                                        recv_sem=sems[1].at[i], device_id=to, device_id_type=MESH)


def _other_chips(x, y):
    return [(1 - x, y), (x, 1 - y), (1 - x, 1 - y)]


def _gather_start(name, bufs, after=None):
    def run(refs, old, new):
        x, y, c = _place()
        k = 2 * x + y
        for w, ref in enumerate(refs):
            rh = bufs[w].shape[1] // 2
            part = ref.at[k, pl.ds(c * rh, rh)]
            for j, (cx, cy) in enumerate(_other_chips(x, y)):
                _remote(part, part, new, 3 * w + j, (cx, cy, c)).start()

    return _split_call(name, bufs, run, after=after, new=3 * len(bufs))


def _gather_pass(name, bufs, sems, after):
    def run(refs, old, new):
        x, y, c = _place()
        k = 2 * x + y
        sib = (x, y, 1 - c)
        for w, ref in enumerate(refs):
            rh = bufs[w].shape[1] // 2
            for j, (cx, cy) in enumerate(_other_chips(x, y)):
                land = ref.at[2 * cx + cy, pl.ds(c * rh, rh)]
                _remote(land, land, old, 3 * w + j, (cx, cy, c)).wait_recv()
                _remote(land, land, new, 3 * w + j, sib).start()
        for w, ref in enumerate(refs):
            rh = bufs[w].shape[1] // 2
            part = ref.at[k, pl.ds(c * rh, rh)]
            for j, (cx, cy) in enumerate(_other_chips(x, y)):
                _remote(part, part, old, 3 * w + j, (cx, cy, c)).wait_send()

    return _split_call(name, bufs, run, old=sems, after=after, new=3 * len(bufs))


def _gather_finish(name, bufs, sems, after):
    def run(refs, old, new):
        x, y, c = _place()
        sib = (x, y, 1 - c)
        for w, ref in enumerate(refs):
            rh = bufs[w].shape[1] // 2
            for j, (cx, cy) in enumerate(_other_chips(x, y)):
                sent = ref.at[2 * cx + cy, pl.ds(c * rh, rh)]
                _remote(sent, sent, old, 3 * w + j, sib).wait_send()
                land = ref.at[2 * cx + cy, pl.ds((1 - c) * rh, rh)]
                _remote(land, land, old, 3 * w + j, sib).wait_recv()

    return _split_call(name, bufs, run, old=sems, after=after)[0]


def _pair_send_start(name, grads):
    n = len(grads)
    lands = [lax.empty((N_CHIPS,) + g.shape[2:], g.dtype) for g in grads]

    def run(refs, old, new):
        x, y, c = _place()
        for w in range(n):
            _remote(refs[w].at[:, 1 - c], refs[n + w], new, w, (x, y, 1 - c)).start()

    return _split_call(name, list(grads) + lands, run, new=n)


def _pair_send_wait(name, bufs, sems, after):
    n = len(bufs) // 2

    def run(refs, old, new):
        x, y, c = _place()
        for w in range(n):
            cp = _remote(refs[w].at[:, 1 - c], refs[n + w], old, w, (x, y, 1 - c))
            cp.wait_send()
            cp.wait_recv()

    thru = _split_call(name, bufs, run, old=sems, after=after)[0]
    return thru[:n], thru[n:]


def _pair_add(g, recv, cidx, name):
    _, _, rh, cols = g.shape
    tr = min(rh, 256)

    def body(c_ref, g_ref, r_ref, o_ref):
        o_ref[...] = (g_ref[...].astype(F32) + r_ref[...].astype(F32)).astype(BF)

    return pl.pallas_call(
        body, name=name,
        grid_spec=pltpu.PrefetchScalarGridSpec(
            num_scalar_prefetch=1, grid=(rh // tr,),
            in_specs=[pl.BlockSpec((N_CHIPS, None, tr, cols), lambda i, c_ref: (0, c_ref[0], i, 0)),
                      pl.BlockSpec((N_CHIPS, tr, cols), lambda i, c_ref: (0, i, 0))],
            out_specs=pl.BlockSpec((N_CHIPS, tr, cols), lambda i, c_ref: (0, i, 0))),
        out_shape=jax.ShapeDtypeStruct((N_CHIPS, rh, cols), BF),
        compiler_params=_cp(("parallel",)),
    )(cidx, g, recv)


def _chip_send_start(name, sums):
    n = len(sums)
    lands = [lax.empty((3,) + s.shape[1:], BF) for s in sums]

    def run(refs, old, new):
        x, y, c = _place()
        for w in range(n):
            for j, (cx, cy) in enumerate(_other_chips(x, y)):
                _remote(refs[w].at[2 * cx + cy], refs[n + w].at[j], new, 3 * w + j, (cx, cy, c)).start()

    return _split_call(name, list(sums) + lands, run, new=3 * n)


def _chip_send_wait(name, bufs, sems, after):
    n = len(bufs) // 2

    def run(refs, old, new):
        x, y, c = _place()
        for w in range(n):
            for j, (cx, cy) in enumerate(_other_chips(x, y)):
                cp = _remote(refs[w].at[2 * cx + cy], refs[n + w].at[j], old, 3 * w + j, (cx, cy, c))
                cp.wait_send()
                cp.wait_recv()

    thru = _split_call(name, bufs, run, old=sems, after=after)[0]
    return thru[:n], thru[n:]


def _chip_add(own, parts, kidx, cidx, name):
    _, rh, cols = parts.shape
    tr = min(rh, 512)

    def body(k_ref, c_ref, own_ref, p_ref, o_ref):
        acc = own_ref[...].astype(F32)
        for s in range(3):
            acc = acc + p_ref[s].astype(F32)
        o_ref[...] = acc

    return pl.pallas_call(
        body, name=name,
        grid_spec=pltpu.PrefetchScalarGridSpec(
            num_scalar_prefetch=2, grid=(rh // tr,),
            in_specs=[pl.BlockSpec((None, tr, cols), lambda i, k_ref, c_ref: (k_ref[0], i, 0)),
                      pl.BlockSpec((3, tr, cols), lambda i, k_ref, c_ref: (0, i, 0))],
            out_specs=pl.BlockSpec((None, tr, cols), lambda i, k_ref, c_ref: (c_ref[0], i, 0))),
        out_shape=jax.ShapeDtypeStruct((2, rh, cols), F32),
        compiler_params=_cp(("parallel",)),
    )(kidx, cidx, own, parts)


def _pair_swap_start(name, bufs):
    def run(refs, old, new):
        x, y, c = _place()
        for w, ref in enumerate(refs):
            _remote(ref.at[c], ref.at[c], new, w, (x, y, 1 - c)).start()

    return _split_call(name, bufs, run, new=len(bufs))


def _pair_swap_wait(name, bufs, sems, after):
    def run(refs, old, new):
        x, y, c = _place()
        for w, ref in enumerate(refs):
            _remote(ref.at[c], ref.at[c], old, w, (x, y, 1 - c)).wait_send()
            _remote(ref.at[1 - c], ref.at[1 - c], old, w, (x, y, 1 - c)).wait_recv()

    return _split_call(name, bufs, run, old=sems, after=after)[0]


def _peers(x, y, c):
    out = []
    for code in range(1, 8):
        fx, fy, fc = (code >> 2) & 1, (code >> 1) & 1, code & 1
        px = 1 - x if fx else x
        py = 1 - y if fy else y
        pc = 1 - c if fc else c
        out.append((code, (px, py, pc)))
    return out


def _mod_exchange(c_row, ada_w, ada_b4, deps):
    ncol = ada_w.shape[1]

    def body(c_ref, w_ref, b_ref, *rest):
        call_ref, mod_ref, part_ref, send_sems, recv_sems = rest[len(deps):]
        x, y, c = _place()
        k = 2 * x + y
        me = 4 * x + 2 * y + c
        call_ref[pl.ds(me, 1), :] = c_ref[...]
        sends = []
        for code, peer in _peers(x, y, c):
            cp = pltpu.make_async_remote_copy(
                src_ref=c_ref, dst_ref=call_ref.at[pl.ds(me, 1), :],
                send_sem=send_sems.at[code], recv_sem=recv_sems.at[code],
                device_id=peer, device_id_type=MESH)
            cp.start()
            sends.append(cp)
        for code, (px, py, pc) in _peers(x, y, c):
            land = call_ref.at[pl.ds(4 * px + 2 * py + pc, 1), :]
            pltpu.make_async_remote_copy(
                src_ref=land, dst_ref=land, send_sem=send_sems.at[code], recv_sem=recv_sems.at[code],
                device_id=(px, py, pc), device_id_type=MESH).wait_recv()
        call = call_ref[...]
        act = call * _sigmoid(call)
        part = jnp.dot(act, w_ref[...], preferred_element_type=F32,
                       precision=lax.Precision.HIGHEST) + b_ref[pl.ds(k, 1), :]
        part_ref[...] = part
        mod_ref[pl.ds(k, 1), :] = part_ref[pl.ds(me, 1), :]
        chips = [(8 + j, peer) for j, (code, peer) in enumerate(_peers(x, y, c)) if code in (2, 4, 6)]
        for slot, (px, py, pc) in chips:
            cp = pltpu.make_async_remote_copy(
                src_ref=part_ref.at[pl.ds(4 * px + 2 * py + pc, 1), :], dst_ref=mod_ref.at[pl.ds(k, 1), :],
                send_sem=send_sems.at[slot], recv_sem=recv_sems.at[slot],
                device_id=(px, py, pc), device_id_type=MESH)
            cp.start()
            sends.append(cp)
        for slot, (px, py, pc) in chips:
            land = mod_ref.at[pl.ds(2 * px + py, 1), :]
            pltpu.make_async_remote_copy(
                src_ref=land, dst_ref=land, send_sem=send_sems.at[slot], recv_sem=recv_sems.at[slot],
                device_id=(px, py, pc), device_id_type=MESH).wait_recv()
        for cp in sends:
            cp.wait_send()

    vm = pl.BlockSpec(memory_space=pltpu.VMEM)
    return pl.pallas_call(
        body, name="mod_exchange",
        in_specs=[vm, vm, vm] + [ANY] * len(deps), out_specs=[vm, vm],
        out_shape=[jax.ShapeDtypeStruct((8, D_MODEL), F32), jax.ShapeDtypeStruct((N_CHIPS, ncol), F32)],
        scratch_shapes=[pltpu.VMEM((8, ncol), F32), pltpu.SemaphoreType.DMA((16,)),
                        pltpu.SemaphoreType.DMA((16,))],
        compiler_params=_cp(),
    )(c_row, ada_w, ada_b4, *deps)


def _small_exchange(payload, c_all, wsm, msm, vsm):
    ncol = 6 * D_MODEL // N_CHIPS

    def body(p_ref, call_ref, w_ref, m_ref, v_ref, gw_ref, g_ref, d_ref, mo_ref, vo_ref, loss_ref,
             all_ref, dm_ref, send_sems, recv_sems):
        x, y, c = _place()
        k = 2 * x + y
        me = 4 * x + 2 * y + c
        all_ref[:, pl.ds(me, 1), :] = p_ref[...]
        sends = []
        for code, peer in _peers(x, y, c):
            cp = pltpu.make_async_remote_copy(
                src_ref=p_ref, dst_ref=all_ref.at[:, pl.ds(me, 1), :],
                send_sem=send_sems.at[code], recv_sem=recv_sems.at[code],
                device_id=peer, device_id_type=MESH)
            cp.start()
            sends.append(cp)
        for code, (px, py, pc) in _peers(x, y, c):
            land = all_ref.at[:, pl.ds(4 * px + 2 * py + pc, 1), :]
            pltpu.make_async_remote_copy(
                src_ref=land, dst_ref=land, send_sem=send_sems.at[code], recv_sem=recv_sems.at[code],
                device_id=(px, py, pc), device_id_type=MESH).wait_recv()
        for cp in sends:
            cp.wait_send()
        tot = [_colsum(all_ref[r]) for r in range(N_PAY)]
        loss_ref[...] = jnp.sum(tot[P_LOSS], axis=1, keepdims=True)
        g_ref[...] = jnp.zeros_like(g_ref)
        for r in range(P_LOSS):
            g_ref[r:r + 1, :] = tot[r]
        g = g_ref[...]
        d, mm, vv = _adamw_math(w_ref[...], g, m_ref[...], v_ref[...])
        d_ref[...] = d
        mo_ref[...] = mm
        vo_ref[...] = vv
        half = D_MODEL // 2
        for kk in range(N_CHIPS):
            @pl.when(k == kk)
            def _():
                r0 = 3 * (kk // 2)
                if kk % 2 == 0:
                    dm_ref[:, :D_MODEL] = all_ref[r0]
                    dm_ref[:, D_MODEL:] = all_ref[r0 + 1][:, :half]
                else:
                    dm_ref[:, :half] = all_ref[r0 + 1][:, half:]
                    dm_ref[:, half:] = all_ref[r0 + 2]
        call = call_ref[...]
        act = call * _sigmoid(call)
        gw_ref[...] = lax.dot_general(act, dm_ref[...], (((0,), (0,)), ((), ())),
                                      preferred_element_type=F32, precision=lax.Precision.HIGHEST)

    vm = pl.BlockSpec(memory_space=pltpu.VMEM)
    small = jax.ShapeDtypeStruct((16, D_MODEL), F32)
    return pl.pallas_call(
        body, name="small_exchange",
        in_specs=[vm] * 5, out_specs=[vm] * 6,
        out_shape=[jax.ShapeDtypeStruct((D_MODEL, ncol), F32), small, small, small, small,
                   jax.ShapeDtypeStruct((1, 1), F32)],
        scratch_shapes=[pltpu.VMEM((N_PAY, 8, D_MODEL), F32), pltpu.VMEM((8, ncol), F32),
                        pltpu.SemaphoreType.DMA((8,)), pltpu.SemaphoreType.DMA((8,))],
        compiler_params=_cp(),
    )(payload, c_all, wsm, msm, vsm)


def _rope_tables(pos_col, inv_freq, sign):
    def body(p_ref, f_ref, s_ref, cos_ref, sin_ref):
        ang = p_ref[...].astype(F32) * f_ref[...]
        cos_ref[...] = jnp.cos(ang)
        sin_ref[...] = jnp.sin(ang) * s_ref[...]

    tr = 512
    shp = jax.ShapeDtypeStruct((SEQ, 128), F32)
    return pl.pallas_call(
        body, name="rope_tables", grid=(SEQ // tr,),
        in_specs=[pl.BlockSpec((tr, 1), lambda i: (i, 0)), pl.BlockSpec((1, 128), lambda i: (0, 0)),
                  pl.BlockSpec((1, 128), lambda i: (0, 0))],
        out_specs=[pl.BlockSpec((tr, 128), lambda i: (i, 0))] * 2, out_shape=[shp, shp],
        compiler_params=_cp(("parallel",)),
    )(pos_col, inv_freq, sign)


def _resident(shape):
    nd = len(shape)
    return pl.BlockSpec(shape, lambda *_: (0,) * nd, pipeline_mode=pl.Buffered(1))


def _ln_proj(x, vecs, w_in4):
    tm = 256
    wc = w_in4.shape[2]

    def body(x_ref, vec_ref, w_ref, h_ref, proj_ref):
        xx = x_ref[...]
        g = _row(vec_ref, V_G1) * (1.0 + _row(vec_ref, V_SC1))
        h = (xx * _rms(xx) * g + _row(vec_ref, V_SH1)).astype(BF)
        h_ref[...] = h
        for j in range(N_CHIPS):
            proj_ref[:, j * wc:(j + 1) * wc] = _dot(h, w_ref[j]).astype(BF)

    return pl.pallas_call(
        body, name="ln_proj", grid=(SEQ // tm,),
        in_specs=[pl.BlockSpec((tm, D_MODEL), lambda i: (i, 0)), _resident((16, D_MODEL)),
                  _resident(w_in4.shape)],
        out_specs=[pl.BlockSpec((tm, D_MODEL), lambda i: (i, 0)), pl.BlockSpec((tm, D_IN), lambda i: (i, 0))],
        out_shape=[jax.ShapeDtypeStruct((SEQ, D_MODEL), BF), jax.ShapeDtypeStruct((SEQ, D_IN), BF)],
        compiler_params=_cp(("parallel",)),
    )(x, vecs, w_in4)


def _lane_first(shape):
    lane = lax.broadcasted_iota(jnp.int32, shape, 1)
    return (lane & 32) == 0


def _rot(v, cos, sin_s):
    partner = jnp.where(_lane_first(v.shape), pltpu.roll(v, 96, 1), pltpu.roll(v, 32, 1))
    return v * cos + partner * sin_s


def _rot_t(dv, cos, sin_s):
    t = dv * sin_s
    partner = jnp.where(_lane_first(dv.shape), pltpu.roll(t, 96, 1), pltpu.roll(t, 32, 1))
    return dv * cos + partner


def _ret_masks(lg):
    t = RET_BLOCK
    ii = lax.broadcasted_iota(jnp.int32, (t, t), 0)
    jj = lax.broadcasted_iota(jnp.int32, (t, t), 1)
    dist = jnp.abs(ii - jj).astype(F32)
    future = (jj >> RET_CHUNK_SHIFT) > (ii >> RET_CHUNK_SHIFT)
    mask = jnp.where(future, 0.0, jnp.exp(lg * dist))
    ti = lax.broadcasted_iota(jnp.int32, (t, 1), 0).astype(F32)
    from_start = jnp.exp(lg * (ti + 1.0))
    to_end = jnp.exp(lg * (t - 1.0 - ti))
    whole = jnp.exp(jnp.full((1, 128), lg * t, F32))
    return mask, from_start, to_end, whole


def _head_lanes(shape, hh):
    lane = lax.broadcasted_iota(jnp.int32, shape, 1)
    return (lane >> 6) == hh


def _ret_specs():
    t = RET_BLOCK
    return dict(
        q=lambda f: pl.BlockSpec((t, 512), lambda n: (f(n), C_QR // 512)),
        k=lambda f: pl.BlockSpec((t, 512), lambda n: (f(n), C_KR // 512)),
        v=lambda f: pl.BlockSpec((t, D_MODEL), lambda n: (f(n), C_VR // D_MODEL)),
        g=lambda f: pl.BlockSpec((t, D_MODEL), lambda n: (f(n), C_GR // D_MODEL)),
        tab=lambda f: pl.BlockSpec((t, 128), lambda n: (f(n), 0)),
        wide=lambda f: pl.BlockSpec((t, D_MODEL), lambda n: (f(n), 0)),
        state=lambda f: pl.BlockSpec((N_PAIRS, None, 2, 128, 128), lambda n: (0, f(n), 0, 0, 0)),
    )


def _ret_fwd(proj, cos, sin_s, gn_g, log_gamma):
    t = RET_BLOCK
    nb = SEQ // t

    def body(lg_ref, q_ref, k_ref, v_ref, g_ref, cos_ref, sin_ref, gn_ref, o_ref, retg_ref, st_ref, state):
        @pl.when(pl.program_id(0) == 0)
        def _():
            state[...] = jnp.zeros_like(state)

        cos, sn = cos_ref[...], sin_ref[...]
        for p in range(N_PAIRS):
            q = _rot(q_ref[:, 128 * p:128 * (p + 1)].astype(F32), cos, sn)
            k = _rot(k_ref[:, 128 * p:128 * (p + 1)].astype(F32), cos, sn) * QK_SCALE
            for hh in range(2):
                cols = slice(256 * p + 128 * hh, 256 * p + 128 * (hh + 1))
                lg = lg_ref[2 * p + hh]
                mask, from_start, to_end, whole = _ret_masks(lg)
                lanes = _head_lanes(q.shape, hh)
                qm = jnp.where(lanes, q, 0.0)
                km = jnp.where(lanes, k, 0.0)
                vh = v_ref[:, cols]
                sc = _dot_nt(qm.astype(BF), km.astype(BF)) * mask
                st = state[p, hh]
                st_ref[p, hh] = st
                o = _dot(sc.astype(BF), vh) + _dot((qm * from_start).astype(BF), st.astype(BF))
                state[p, hh] = whole * st + _dot_tn((km * to_end).astype(BF), vh)
                d = o - _rowmean(o)
                nh = d * lax.rsqrt(_rowmean(d * d) + EPS)
                gr = g_ref[:, cols].astype(F32)
                o_ref[:, cols] = o
                retg_ref[:, cols] = (gr * _sigmoid(gr) * nh * gn_ref[:, cols]).astype(BF)

    sp = _ret_specs()
    ident = lambda n: n
    return pl.pallas_call(
        body, name="ret_fwd", grid=(nb,),
        in_specs=[pl.BlockSpec(memory_space=pltpu.SMEM), sp["q"](ident), sp["k"](ident), sp["v"](ident),
                  sp["g"](ident), sp["tab"](ident), sp["tab"](ident), _resident((1, D_MODEL))],
        out_specs=[sp["wide"](ident), sp["wide"](ident), sp["state"](ident)],
        out_shape=[jax.ShapeDtypeStruct((SEQ, D_MODEL), F32), jax.ShapeDtypeStruct((SEQ, D_MODEL), BF),
                   jax.ShapeDtypeStruct((N_PAIRS, nb, 2, 128, 128), F32)],
        scratch_shapes=[pltpu.VMEM((N_PAIRS, 2, 128, 128), F32)],
        compiler_params=_cp(("arbitrary",)),
    )(log_gamma, proj, proj, proj, proj, cos, sin_s, gn_g)


def _ret_bwd(proj, cos, sin_s, dret, states, log_gamma):
    t = RET_BLOCK
    nb = SEQ // t

    def body(lg_ref, q_ref, k_ref, v_ref, cos_ref, sin_ref, do_ref, st_ref, dqkv_ref, dstate):
        @pl.when(pl.program_id(0) == 0)
        def _():
            dstate[...] = jnp.zeros_like(dstate)

        cos, sn = cos_ref[...], sin_ref[...]
        for p in range(N_PAIRS):
            q = _rot(q_ref[:, 128 * p:128 * (p + 1)].astype(F32), cos, sn)
            k = _rot(k_ref[:, 128 * p:128 * (p + 1)].astype(F32), cos, sn) * QK_SCALE
            dq_rot = jnp.zeros(q.shape, F32)
            dk_rot = jnp.zeros(q.shape, F32)
            for hh in range(2):
                cols = slice(256 * p + 128 * hh, 256 * p + 128 * (hh + 1))
                lg = lg_ref[2 * p + hh]
                mask, from_start, to_end, whole = _ret_masks(lg)
                lanes = _head_lanes(q.shape, hh)
                qm = jnp.where(lanes, q, 0.0)
                km = jnp.where(lanes, k, 0.0)
                qb, kb = qm.astype(BF), km.astype(BF)
                vh = v_ref[:, cols]
                do = do_ref[:, cols]
                sc = (_dot_nt(qb, kb) * mask).astype(BF)
                st = st_ref[p, hh].astype(BF)
                dst = dstate[p, hh]
                dstb = dst.astype(BF)
                k_end = (km * to_end).astype(BF)
                q_start = (qm * from_start).astype(BF)
                dqkv_ref[:, C_VR + 256 * p + 128 * hh:C_VR + 256 * p + 128 * (hh + 1)] = (
                    _dot_tn(sc, do) + _dot(k_end, dstb)).astype(BF)
                dsc = (_dot_nt(do, vh) * mask).astype(BF)
                dq_h = _dot(dsc, kb) + _dot_nt(do, st) * from_start
                dq_rot = dq_rot + jnp.where(lanes, dq_h, 0.0)
                dk_rot = dk_rot + _dot_tn(dsc, qb) + _dot_nt(vh, dstb) * to_end
                dstate[p, hh] = whole * dst + _dot_tn(q_start, do)
            dqkv_ref[:, C_QR + 128 * p:C_QR + 128 * (p + 1)] = _rot_t(dq_rot, cos, sn).astype(BF)
            dqkv_ref[:, C_KR + 128 * p:C_KR + 128 * (p + 1)] = _rot_t(dk_rot * QK_SCALE, cos, sn).astype(BF)

    sp = _ret_specs()
    rev = lambda n: nb - 1 - n
    return pl.pallas_call(
        body, name="ret_bwd", grid=(nb,),
        in_specs=[pl.BlockSpec(memory_space=pltpu.SMEM), sp["q"](rev), sp["k"](rev), sp["v"](rev),
                  sp["tab"](rev), sp["tab"](rev), sp["wide"](rev), sp["state"](rev)],
        out_specs=pl.BlockSpec((t, C_GR), lambda n: (rev(n), 0)),
        out_shape=jax.ShapeDtypeStruct((SEQ, C_GR), BF),
        scratch_shapes=[pltpu.VMEM((N_PAIRS, 2, 128, 128), F32)],
        compiler_params=_cp(("arbitrary",)),
    )(log_gamma, proj, proj, proj, cos, sin_s, dret, states)


def _stack_heads(v):
    return jnp.concatenate([jnp.where(_head_lanes(v.shape, hh), v, jnp.zeros_like(v)) for hh in range(2)], axis=0)


def _unstack_heads(v):
    t = v.shape[0] // 2
    return jnp.where(_head_lanes((t, v.shape[1]), 0), v[:t], v[t:])


def _sb_masks(t, heads):
    rr = lax.broadcasted_iota(jnp.int32, (t, t), 0)
    cc = lax.broadcasted_iota(jnp.int32, (t, t), 1)
    r2 = lax.broadcasted_iota(jnp.int32, (heads * t, t), 0) & (t - 1)
    c2 = lax.broadcasted_iota(jnp.int32, (heads * t, t), 1)
    return rr, cc, c2 < r2


def _split_dot2(v, tri):
    return _dot(v.astype(BF), tri)


def _log_one_minus_beta(z):
    return -(jnp.maximum(z, 0.0) + jnp.log(1.0 + jnp.exp(-jnp.abs(z))))


def _sb_fwd(proj):
    t, g = SB_BLOCK, SB_GROUP
    nq = SEQ // t
    rows = 2 * g * t

    def body(q_ref, k_ref, v_ref, o_ref, tot_ref, kt_ref):
        i = pl.program_id(1)

        @pl.when(i == 0)
        def _():
            for p in range(g):
                for jj in range(nq):
                    kt_ref[p, jj] = k_ref[jj * t:(jj + 1) * t, 128 * p:128 * (p + 1)].T

        q2 = [_stack_heads((q_ref[:, 128 * p:128 * (p + 1)].astype(F32) * QK_SCALE).astype(BF)) for p in range(g)]
        rr, cc, valid = _sb_masks(t, 2 * g)
        later = (rr > cc).astype(BF)

        def tile(j, carry, diagonal):
            acc, run = carry
            z = jnp.concatenate([_dot(q2[p], kt_ref[p, j]) for p in range(g)], axis=0)
            lm = _log_one_minus_beta(z)
            if diagonal:
                lm = jnp.where(valid, lm, 0.0)
            after = _split_dot2(lm, later)
            a = jnp.exp(z + lm + after + run)
            if diagonal:
                a = jnp.where(valid, a, 0.0)
            ab = a.astype(BF)
            keys = pl.ds(pl.multiple_of(j * t, t), t)
            av = jnp.concatenate([_dot(ab[2 * t * p:2 * t * (p + 1)], v_ref[keys, 128 * p:128 * (p + 1)])
                                  for p in range(g)], axis=0)
            return acc + av, run + after[:, 0:1] + lm[:, 0:1]

        carry = tile(i, (jnp.zeros((rows, 128), F32), jnp.zeros((rows, 1), F32)), True)
        acc, run = lax.fori_loop(0, i, lambda s, cr: tile(i - 1 - s, cr, False), carry)
        run = jnp.broadcast_to(run, (rows, 128))
        for p in range(g):
            o_ref[:, 128 * p:128 * (p + 1)] = _unstack_heads(acc[2 * t * p:2 * t * (p + 1)]).astype(BF)
            tot_ref[:, 128 * p:128 * (p + 1)] = _unstack_heads(run[2 * t * p:2 * t * (p + 1)])

    w = 128 * g
    return pl.pallas_call(
        body, name="sb_fwd", grid=(N_PAIRS // g, nq),
        in_specs=[pl.BlockSpec((t, w), lambda p, i: (i, C_QS // w + p)),
                  pl.BlockSpec((SEQ, w), lambda p, i: (0, C_KS // w + p)),
                  pl.BlockSpec((SEQ, w), lambda p, i: (0, C_VS // w + p))],
        out_specs=[pl.BlockSpec((t, w), lambda p, i: (i, p))] * 2,
        out_shape=[jax.ShapeDtypeStruct((SEQ, 512), BF), jax.ShapeDtypeStruct((SEQ, 512), F32)],
        scratch_shapes=[pltpu.VMEM((g, nq, 128, t), BF)],
        compiler_params=_cp(("parallel", "arbitrary")),
    )(proj, proj, proj)


def _sb_bwd(proj, dsb, tot, dep=None):
    t, g = SB_BLOCK, SB_GROUP
    nq = SEQ // t
    rows = 2 * g * t

    def body(q_ref, k_ref, v_ref, do_ref, tot_ref, dq_ref, dk_ref, dv_ref, kt_ref, vt_ref, dkt_acc, dvt_acc):
        i = pl.program_id(1)

        @pl.when(i == 0)
        def _():
            dkt_acc[...] = jnp.zeros_like(dkt_acc)
            dvt_acc[...] = jnp.zeros_like(dvt_acc)
            for p in range(g):
                for jj in range(nq):
                    kt_ref[p, jj] = k_ref[jj * t:(jj + 1) * t, 128 * p:128 * (p + 1)].T
                    vt_ref[p, jj] = v_ref[jj * t:(jj + 1) * t, 128 * p:128 * (p + 1)].T

        q2 = [_stack_heads((q_ref[:, 128 * p:128 * (p + 1)].astype(F32) * QK_SCALE).astype(BF)) for p in range(g)]
        do2 = [_stack_heads(do_ref[:, 128 * p:128 * (p + 1)]) for p in range(g)]
        q2t = [v.T for v in q2]
        do2t = [v.T for v in do2]
        tots = tot_ref[...]
        total = jnp.concatenate([tots[:, 64 * h:64 * h + 1] for h in range(2 * g)], axis=0)
        rr, cc, valid = _sb_masks(t, 2 * g)
        upto = (rr <= cc).astype(BF)
        before = (rr < cc).astype(BF)

        def part(v, p):
            return v[2 * t * p:2 * t * (p + 1)]

        def tile(j, carry, diagonal):
            dq, run_l, run_g = carry
            z = jnp.concatenate([_dot(q2[p], kt_ref[p, j]) for p in range(g)], axis=0)
            lm = _log_one_minus_beta(z)
            if diagonal:
                lm = jnp.where(valid, lm, 0.0)
            incl = _split_dot2(lm, upto)
            a = jnp.exp(z + lm + (total - (incl + run_l)))
            if diagonal:
                a = jnp.where(valid, a, 0.0)
            gg = a * jnp.concatenate([_dot(do2[p], vt_ref[p, j]) for p in range(g)], axis=0)
            excl = _split_dot2(gg, before)
            dz = gg * jnp.exp(lm) - (excl + run_g) * jnp.exp(z + lm)
            if diagonal:
                dz = jnp.where(valid, dz, 0.0)
            dzb = dz.astype(BF)
            ab = a.astype(BF)
            keys = pl.ds(pl.multiple_of(j * t, t), t)
            for p in range(g):
                dkt_acc[p, j] += _dot(q2t[p], part(dzb, p))
                dvt_acc[p, j] += _dot(do2t[p], part(ab, p))
            dq_t = jnp.concatenate([_dot(part(dzb, p), k_ref[keys, 128 * p:128 * (p + 1)]) for p in range(g)], axis=0)
            return (dq + dq_t, run_l + incl[:, t - 1:t], run_g + excl[:, t - 1:t] + gg[:, t - 1:t])

        zero = jnp.zeros((rows, 1), F32)
        carry = lax.fori_loop(0, i, lambda j, cr: tile(j, cr, False), (jnp.zeros((rows, 128), F32), zero, zero))
        dq = tile(i, carry, True)[0]
        for p in range(g):
            dq_ref[:, 128 * p:128 * (p + 1)] = (_unstack_heads(part(dq, p)) * QK_SCALE).astype(BF)

        @pl.when(i == nq - 1)
        def _():
            for p in range(g):
                for jj in range(nq):
                    dk_ref[jj * t:(jj + 1) * t, 128 * p:128 * (p + 1)] = dkt_acc[p, jj].T.astype(BF)
                    dv_ref[jj * t:(jj + 1) * t, 128 * p:128 * (p + 1)] = dvt_acc[p, jj].T.astype(BF)

    w = 128 * g
    tile_spec = pl.BlockSpec((t, w), lambda p, i: (i, p))
    col_spec = pl.BlockSpec((SEQ, w), lambda p, i: (0, p))
    shp = jax.ShapeDtypeStruct((SEQ, 512), BF)
    body, in_specs, args = _add_dep(
        body, [pl.BlockSpec((t, w), lambda p, i: (i, C_QS // w + p)),
               pl.BlockSpec((SEQ, w), lambda p, i: (0, C_KS // w + p)),
               pl.BlockSpec((SEQ, w), lambda p, i: (0, C_VS // w + p)),
               tile_spec, tile_spec],
        [proj, proj, proj, dsb, tot], dep)
    return pl.pallas_call(
        body, name="sb_bwd", grid=(N_PAIRS // g, nq),
        in_specs=in_specs,
        out_specs=[tile_spec, col_spec, col_spec],
        out_shape=[shp, shp, shp],
        scratch_shapes=[pltpu.VMEM((g, nq, 128, t), BF), pltpu.VMEM((g, nq, 128, t), BF),
                        pltpu.VMEM((g, nq, 128, t), F32), pltpu.VMEM((g, nq, 128, t), F32)],
        compiler_params=_cp(("parallel", "arbitrary")),
    )(*args)


def _mix_out(retg, sb, proj, x, vecs, w_ret, w_sb4, w_out):
    tm, half = 256, 512

    def body(r_ref, s_ref, ar0, ar1, as0, as1, x_ref, vec_ref, wr_ref, ws_ref, wo_ref,
             mix_ref, rb_ref, sbp_ref, y_ref, h1_ref, h2_ref):
        rb = _dot(r_ref[...], wr_ref[...])
        sbv = s_ref[...]
        sbp = jnp.concatenate([_dot(sbv, ws_ref[k]) for k in range(N_CHIPS)], axis=1)
        gate_r = _sigmoid(jnp.concatenate([ar0[...], ar1[...]], axis=1).astype(F32))
        gate_s = _sigmoid(jnp.concatenate([as0[...], as1[...]], axis=1).astype(F32))
        mixed = (gate_r * rb + gate_s * sbp).astype(BF)
        mix_ref[...] = mixed
        rb_ref[...] = rb.astype(BF)
        sbp_ref[...] = sbp.astype(BF)
        y = _dot(mixed, wo_ref[...])
        h1 = x_ref[...] + _row(vec_ref, V_GT1) * (y * _rms(y)) * _row(vec_ref, V_G2)
        g = _row(vec_ref, V_G3) * (1.0 + _row(vec_ref, V_SC2))
        y_ref[...] = y
        h1_ref[...] = h1
        h2_ref[...] = (h1 * _rms(h1) * g + _row(vec_ref, V_SH2)).astype(BF)

    row = pl.BlockSpec((tm, D_MODEL), lambda i: (i, 0))
    gate = lambda c0: pl.BlockSpec((tm, half), lambda i: (i, c0 // half))
    bf = jax.ShapeDtypeStruct((SEQ, D_MODEL), BF)
    f32 = jax.ShapeDtypeStruct((SEQ, D_MODEL), F32)
    return pl.pallas_call(
        body, name="mix_out", grid=(SEQ // tm,),
        in_specs=[row, pl.BlockSpec((tm, 512), lambda i: (i, 0)), gate(C_AR), gate(C_AR + half), gate(C_AS),
                  gate(C_AS + half), row, _resident((16, D_MODEL)), _resident((D_MODEL, D_MODEL)),
                  _resident(w_sb4.shape), _resident((D_MODEL, D_MODEL))],
        out_specs=[row] * 6, out_shape=[bf, bf, bf, f32, f32, bf],
        compiler_params=_cp(("parallel",)),
    )(retg, sb, proj, proj, proj, proj, x, vecs, w_ret, w_sb4, w_out)


def _ffn_fwd_loss(h2, h1, target, vecs, w_ff14, w_ff2):
    tm = 256

    def body(h2_ref, h1_ref, t_ref, vec_ref, w1_ref, w2_ref, u_ref, a_ref, dout_ref, df_ref, st_ref):
        @pl.when(pl.program_id(0) == 0)
        def _():
            st_ref[...] = jnp.zeros_like(st_ref)

        hb = h2_ref[...]
        f = jnp.zeros((tm, D_MODEL), F32)
        for k in range(N_CHIPS):
            cols = slice(k * D_MODEL, (k + 1) * D_MODEL)
            u = _dot(hb, w1_ref[k])
            r = jnp.maximum(u, 0.0)
            act = (r * r).astype(BF)
            u_ref[:, cols] = u.astype(BF)
            a_ref[:, cols] = act
            f = f + _dot(act, w2_ref[cols, :])
        r4 = _rms(f)
        fn = f * r4
        gt2, g4 = _row(vec_ref, V_GT2), _row(vec_ref, V_G4)
        diff = h1_ref[...] + gt2 * fn * g4 - t_ref[...]
        dout = diff * (1.0 / D_MODEL)
        dfn = dout * gt2 * g4
        dout_ref[...] = dout
        df_ref[...] = (r4 * (dfn - fn * _rowmean(dfn * fn))).astype(BF)
        st_ref[0:1, :] += _colsum(dout * fn * g4)
        st_ref[1:2, :] += _colsum(dout * gt2 * fn)
        st_ref[2:3, :] += _colsum(diff * diff) * (0.5 / D_MODEL)

    row = pl.BlockSpec((tm, D_MODEL), lambda i: (i, 0))
    wide = pl.BlockSpec((tm, D_FF), lambda i: (i, 0))
    return pl.pallas_call(
        body, name="ffn_fwd_loss", grid=(SEQ // tm,),
        in_specs=[row, row, row, _resident((16, D_MODEL)), _resident(w_ff14.shape), _resident(w_ff2.shape)],
        out_specs=[wide, wide, row, row, pl.BlockSpec((8, D_MODEL), lambda i: (0, 0))],
        out_shape=[jax.ShapeDtypeStruct((SEQ, D_FF), BF), jax.ShapeDtypeStruct((SEQ, D_FF), BF),
                   jax.ShapeDtypeStruct((SEQ, D_MODEL), F32), jax.ShapeDtypeStruct((SEQ, D_MODEL), BF),
                   jax.ShapeDtypeStruct((8, D_MODEL), F32)],
        compiler_params=_cp(("arbitrary",)),
    )(h2, h1, target, vecs, w_ff14, w_ff2)


def _ffn_bwd(df, u, h1, y, dout, vecs, w_ff14, w_ff2):
    tm = 256

    def body(df_ref, u_ref, h1_ref, y_ref, dout_ref, vec_ref, w1_ref, w2_ref, du_ref, dh1_ref, dy_ref, st_ref):
        @pl.when(pl.program_id(0) == 0)
        def _():
            st_ref[...] = jnp.zeros_like(st_ref)

        dfb = df_ref[...]
        dh2 = jnp.zeros((tm, D_MODEL), F32)
        for k in range(N_CHIPS):
            cols = slice(k * D_MODEL, (k + 1) * D_MODEL)
            da = _dot_nt(dfb, w2_ref[cols, :])
            du = (da * (2.0 * jnp.maximum(u_ref[:, cols].astype(F32), 0.0))).astype(BF)
            du_ref[:, cols] = du
            dh2 = dh2 + _dot_nt(du, w1_ref[k])
        h1 = h1_ref[...]
        r3 = _rms(h1)
        hn3 = h1 * r3
        g3, sc2 = _row(vec_ref, V_G3), _row(vec_ref, V_SC2)
        dhn3 = dh2 * g3 * (1.0 + sc2)
        dh1 = dout_ref[...] + r3 * (dhn3 - hn3 * _rowmean(dhn3 * hn3))
        y = y_ref[...]
        r2 = _rms(y)
        yn = y * r2
        gt1, g2 = _row(vec_ref, V_GT1), _row(vec_ref, V_G2)
        dyn = dh1 * gt1 * g2
        dh1_ref[...] = dh1
        dy_ref[...] = (r2 * (dyn - yn * _rowmean(dyn * yn))).astype(BF)
        st_ref[0:1, :] += _colsum(dh2)
        st_ref[1:2, :] += _colsum(dh2 * hn3 * g3)
        st_ref[2:3, :] += _colsum(dh2 * hn3 * (1.0 + sc2))
        st_ref[3:4, :] += _colsum(dh1 * yn * g2)
        st_ref[4:5, :] += _colsum(dh1 * gt1 * yn)

    row = pl.BlockSpec((tm, D_MODEL), lambda i: (i, 0))
    wide = pl.BlockSpec((tm, D_FF), lambda i: (i, 0))
    return pl.pallas_call(
        body, name="ffn_bwd", grid=(SEQ // tm,),
        in_specs=[row, wide, row, row, row, _resident((16, D_MODEL)), _resident(w_ff14.shape),
                  _resident(w_ff2.shape)],
        out_specs=[wide, row, row, pl.BlockSpec((8, D_MODEL), lambda i: (0, 0))],
        out_shape=[jax.ShapeDtypeStruct((SEQ, D_FF), BF), jax.ShapeDtypeStruct((SEQ, D_MODEL), F32),
                   jax.ShapeDtypeStruct((SEQ, D_MODEL), BF), jax.ShapeDtypeStruct((8, D_MODEL), F32)],
        compiler_params=_cp(("arbitrary",)),
    )(df, u, h1, y, dout, vecs, w_ff14, w_ff2)


def _mix_ret_bwd(dy, proj, rb, sbp, o_raw, gn_g, w_out, w_sb4, w_ret):
    tm, half = 256, 512

    def body(dy_ref, ar0, ar1, as0, as1, rb_ref, sbp_ref, g_ref, o_ref, gn_ref, wo_ref, ws_ref, wr_ref,
             drb_ref, dsbp_ref, da_ref, dsb_ref, dret_ref, dgr_ref, st_ref):
        @pl.when(pl.program_id(0) == 0)
        def _():
            st_ref[...] = jnp.zeros_like(st_ref)

        dm_all = _dot_nt(dy_ref[...], wo_ref[...])
        dsb = jnp.zeros((tm, 512), F32)
        drbs = []
        for hf, (ar_ref, as_ref) in enumerate(((ar0, as0), (ar1, as1))):
            cols = slice(half * hf, half * (hf + 1))
            dm = dm_all[:, cols]
            sr = _sigmoid(ar_ref[...].astype(F32))
            ss = _sigmoid(as_ref[...].astype(F32))
            dsbp = (dm * ss).astype(BF)
            drbs.append((dm * sr).astype(BF))
            dsbp_ref[:, cols] = dsbp
            da_ref[:, cols] = (dm * rb_ref[:, cols].astype(F32) * sr * (1.0 - sr)).astype(BF)
            da_ref[:, D_MODEL + half * hf:D_MODEL + half * (hf + 1)] = (
                dm * sbp_ref[:, cols].astype(F32) * ss * (1.0 - ss)).astype(BF)
            dsb = dsb + _dot_nt(dsbp[:, :256], ws_ref[2 * hf]) + _dot_nt(dsbp[:, 256:], ws_ref[2 * hf + 1])
        dsb_ref[...] = dsb.astype(BF)
        drb = jnp.concatenate(drbs, axis=1)
        drb_ref[...] = drb
        dretg = _dot_nt(drb, wr_ref[...])
        for gi in range(D_MODEL // 128):
            cols = slice(128 * gi, 128 * (gi + 1))
            o = o_ref[:, cols]
            d = o - _rowmean(o)
            rstd = lax.rsqrt(_rowmean(d * d) + EPS)
            nh = d * rstd
            gain = gn_ref[:, cols]
            gr = g_ref[:, cols].astype(F32)
            sg = _sigmoid(gr)
            dg = dretg[:, cols]
            dgn = dg * gr * sg
            dnh = dgn * gain
            dgr_ref[:, cols] = (dg * nh * gain * sg * (1.0 + gr * (1.0 - sg))).astype(BF)
            dret_ref[:, cols] = (rstd * (dnh - _rowmean(dnh) - nh * _rowmean(dnh * nh))).astype(BF)
            st_ref[0:1, cols] += _colsum(dgn * nh)

    row = pl.BlockSpec((tm, D_MODEL), lambda i: (i, 0))
    gate = lambda c0: pl.BlockSpec((tm, half), lambda i: (i, c0 // half))
    shp = jax.ShapeDtypeStruct((SEQ, D_MODEL), BF)
    return pl.pallas_call(
        body, name="mix_ret_bwd", grid=(SEQ // tm,),
        in_specs=[row, gate(C_AR), gate(C_AR + half), gate(C_AS), gate(C_AS + half), row, row,
                  pl.BlockSpec((tm, D_MODEL), lambda i: (i, C_GR // D_MODEL)), row, _resident((1, D_MODEL)),
                  _resident((D_MODEL, D_MODEL)), _resident(w_sb4.shape), _resident((D_MODEL, D_MODEL))],
        out_specs=[row, row, pl.BlockSpec((tm, 2 * D_MODEL), lambda i: (i, 0)), pl.BlockSpec((tm, 512), lambda i: (i, 0)),
                   row, row, pl.BlockSpec((8, D_MODEL), lambda i: (0, 0))],
        out_shape=[shp, shp, jax.ShapeDtypeStruct((SEQ, 2 * D_MODEL), BF), jax.ShapeDtypeStruct((SEQ, 512), BF),
                   shp, shp, jax.ShapeDtypeStruct((8, D_MODEL), F32)],
        compiler_params=_cp(("arbitrary",)),
    )(dy, proj, proj, proj, proj, rb, sbp, proj, o_raw, gn_g, w_out, w_sb4, w_ret)


def _dproj_segments(widths):
    wc = D_IN // N_CHIPS
    segs, start = [], 0
    for pi, width in enumerate(widths):
        lo = start
        while lo < start + width:
            j = lo // wc
            hi = min(start + width, (j + 1) * wc)
            segs.append((j, lo - j * wc, pi, lo - start, hi - lo))
            lo = hi
        start += width
    assert start == D_IN
    return segs


def _in_proj_bwd(pieces, x, dh1, vecs, w_in4, updates):
    tm = 256
    steps = SEQ // tm
    n, nu = len(pieces), len(updates)
    segs = _dproj_segments([p.shape[1] for p in pieces])

    def body(*refs):
        x_ref, dh1_ref, vec_ref, w_ref = refs[n:n + 4]
        upd_in = refs[n + 4:n + 4 + 4 * nu]
        dx_ref, st_ref = refs[n + 4 + 4 * nu:n + 6 + 4 * nu]
        upd_out = refs[n + 6 + 4 * nu:]
        for u in range(nu):
            w_u, m_u, v_u, g_u = upd_in[4 * u:4 * u + 4]
            go_u, d_u, mo_u, vo_u = upd_out[4 * u:4 * u + 4]
            gg = g_u[...]
            go_u[...] = gg
            d_u[...], mo_u[...], vo_u[...] = _adamw_math(w_u[...], gg, m_u[...], v_u[...])

        @pl.when(pl.program_id(0) == 0)
        def _():
            st_ref[...] = jnp.zeros_like(st_ref)

        dh = jnp.zeros((tm, D_MODEL), F32)
        for j, so, pi, po, width in segs:
            dh = dh + _dot_nt(refs[pi][:, po:po + width], w_ref[j, :, so:so + width])
        xx = x_ref[...]
        r1 = _rms(xx)
        xn = xx * r1
        g1, sc1 = _row(vec_ref, V_G1), _row(vec_ref, V_SC1)
        dxn = dh * g1 * (1.0 + sc1)
        dx_ref[...] = dh1_ref[...] + r1 * (dxn - xn * _rowmean(dxn * xn))
        st_ref[0:1, :] += _colsum(dh)
        st_ref[1:2, :] += _colsum(dh * xn * g1)
        st_ref[2:3, :] += _colsum(dh * xn * (1.0 + sc1))

    row = pl.BlockSpec((tm, D_MODEL), lambda i: (i, 0))
    upd_specs, upd_shapes, upd_args = [], [], []
    for arrays in updates:
        rows, cols = arrays[0].shape
        upd_specs += [pl.BlockSpec((rows // steps, cols), lambda i: (i, 0))] * 4
        upd_shapes += [jax.ShapeDtypeStruct((rows, cols), F32)] * 4
        upd_args += list(arrays)
    outs = pl.pallas_call(
        body, name="in_proj_bwd", grid=(steps,),
        in_specs=[pl.BlockSpec((tm, p.shape[1]), lambda i: (i, 0)) for p in pieces] + [
            row, row, _resident((16, D_MODEL)), _resident(w_in4.shape)] + upd_specs,
        out_specs=[row, pl.BlockSpec((8, D_MODEL), lambda i: (0, 0))] + upd_specs,
        out_shape=[jax.ShapeDtypeStruct((SEQ, D_MODEL), F32), jax.ShapeDtypeStruct((8, D_MODEL), F32)] + upd_shapes,
        compiler_params=_cp(("arbitrary",)),
    )(*pieces, x, dh1, vecs, w_in4, *upd_args)
    return outs[0], outs[1], [tuple(outs[2 + 4 * u:6 + 4 * u]) for u in range(nu)]


def _grad_w_in(h, pieces, dep=None):
    ta = 256
    n = len(pieces)
    segs = _dproj_segments([p.shape[1] for p in pieces])

    def body(*refs):
        h_ref, o_ref = refs[n], refs[n + 1]
        hh = h_ref[...]
        for j, so, pi, po, width in segs:
            o_ref[j, :, so:so + width] = _dot_tn(hh, refs[pi][:, po:po + width]).astype(BF)

    body, in_specs, args = _add_dep(
        body, [_resident(p.shape) for p in pieces] + [pl.BlockSpec((SEQ, ta), lambda i: (0, i))],
        list(pieces) + [h], dep)
    return pl.pallas_call(
        body, name="grad_w_in", grid=(D_MODEL // ta,),
        in_specs=in_specs,
        out_specs=pl.BlockSpec((N_CHIPS, ta, D_IN // N_CHIPS), lambda i: (0, i, 0)),
        out_shape=jax.ShapeDtypeStruct((N_CHIPS, D_MODEL, D_IN // N_CHIPS), BF),
        compiler_params=_cp(("parallel",)),
    )(*args)


def _weight_grad(a, b, ta, tb, col_sharded, name, dep=None):
    ka, nb_ = a.shape[1], b.shape[1]

    def body(a_ref, b_ref, o_ref):
        o_ref[...] = _dot_tn(a_ref[...], b_ref[...]).astype(BF)

    body, in_specs, args = _add_dep(
        body, [pl.BlockSpec((SEQ, ta), lambda i, j: (0, i)), pl.BlockSpec((SEQ, tb), lambda i, j: (0, j))],
        [a, b], dep)

    if col_sharded:
        per = nb_ // N_CHIPS // tb
        out_shape = jax.ShapeDtypeStruct((N_CHIPS, ka, nb_ // N_CHIPS), BF)
        out_spec = pl.BlockSpec((None, ta, tb), lambda i, j: (j // per, i, j % per))
    else:
        per = ka // N_CHIPS // ta
        out_shape = jax.ShapeDtypeStruct((N_CHIPS, ka // N_CHIPS, nb_), BF)
        out_spec = pl.BlockSpec((None, ta, tb), lambda i, j: (i // per, i % per, j))
    return pl.pallas_call(
        body, name=name, grid=(ka // ta, nb_ // tb),
        in_specs=in_specs, out_specs=out_spec, out_shape=out_shape,
        compiler_params=_cp(("parallel", "parallel")),
    )(*args)


def _rope_constants():
    freq = np.float32(ROPE_BASE) ** (-np.arange(0, 64, 2, dtype=np.float32) / np.float32(64))
    inv = np.tile(freq.astype(np.float32), 4).reshape(1, 128)
    sign = np.tile(np.concatenate([-np.ones(32, np.float32), np.ones(32, np.float32)]), 2).reshape(1, 128)
    return jnp.asarray(inv), jnp.asarray(sign)


def _log_gamma():
    return jnp.asarray(np.log1p(-(2.0 ** (-5.0 - np.arange(8, dtype=np.float64)))).astype(np.float32))


def _halves(g):
    return g.reshape(N_CHIPS, 2, g.shape[1] // 2, g.shape[2])


def kernel(x, c, positions, ada_w, ada_b, pre_mix_g, post_mix_g, pre_ffn_g, post_ffn_g, w_in, ret_gn_g, w_ret_branch, w_sb_branch, w_out, w_ff1, w_ff2, loss_target, m_ada_w, m_ada_b, m_pre_mix_g, m_post_mix_g, m_pre_ffn_g, m_post_ffn_g, m_w_in, m_ret_gn_g, m_w_ret_branch, m_w_sb_branch, m_w_out, m_w_ff1, m_w_ff2, v_ada_w, v_ada_b, v_pre_mix_g, v_post_mix_g, v_pre_ffn_g, v_post_ffn_g, v_w_in, v_ret_gn_g, v_w_ret_branch, v_w_sb_branch, v_w_out, v_w_ff1, v_w_ff2):
    names = ["w_in", "w_ret", "w_sb", "w_out", "w_ff1", "w_ff2"]
    big = dict(zip(names, [w_in, w_ret_branch, w_sb_branch, w_out, w_ff1, w_ff2]))
    big_m = dict(zip(names, [m_w_in, m_w_ret_branch, m_w_sb_branch, m_w_out, m_w_ff1, m_w_ff2]))
    big_v = dict(zip(names, [v_w_in, v_w_ret_branch, v_w_sb_branch, v_w_out, v_w_ff1, v_w_ff2]))
    rest = names[1:]
    cidx = lax.axis_index("c").astype(jnp.int32).reshape(1)
    kidx = (2 * lax.axis_index("x") + lax.axis_index("y")).astype(jnp.int32).reshape(1)
    x0, target = x[0], loss_target[0]

    buf_in, sem_in, tok_in = _gather_start("gather_in_start", [_cast_bf16(w_in[0], kidx, c, "cast_w_in")])
    rest_bufs = [_cast_bf16(big[nm][0], kidx, tok_in, "cast_" + nm) for nm in rest]
    inv_freq, sign = _rope_constants()
    lg = _log_gamma()
    cos, sin_s = _rope_tables(positions.reshape(SEQ, 1), _tie(tok_in, inv_freq), sign)

    def table(b6, g5):
        return jnp.concatenate([b6.reshape(6, D_MODEL)] + g5 + [jnp.zeros((5, D_MODEL), F32)], axis=0)

    wsm = table(ada_b, [pre_mix_g, post_mix_g, pre_ffn_g, post_ffn_g, ret_gn_g])
    msm = table(m_ada_b, [m_pre_mix_g, m_post_mix_g, m_pre_ffn_g, m_post_ffn_g, m_ret_gn_g])
    vsm = table(v_ada_b, [v_pre_mix_g, v_post_mix_g, v_pre_ffn_g, v_post_ffn_g, v_ret_gn_g])
    c_all, mod4 = _mod_exchange(c, ada_w[0], ada_b.reshape(N_CHIPS, -1), rest_bufs + [cos, wsm, msm, vsm])
    vecs = jnp.concatenate([mod4.reshape(6, D_MODEL), pre_mix_g, post_mix_g, pre_ffn_g, post_ffn_g,
                            jnp.zeros((6, D_MODEL), F32)], axis=0)
    buf_in, sem_in, tok_in = _gather_pass("gather_in_pass", buf_in, sem_in, vecs)
    buf_rest, sem_rest, tok_rest = _gather_start("gather_rest_start", rest_bufs, after=tok_in)
    (w_in4,) = _gather_finish("gather_in_finish", buf_in, sem_in, tok_rest)

    h, proj = _ln_proj(x0, vecs, w_in4)
    sb, tot = _sb_fwd(proj)
    buf_rest, sem_rest, tok_rest = _gather_pass("gather_rest_pass", buf_rest, sem_rest, sb)
    o_raw, retg, states = _ret_fwd(proj, cos, sin_s, _tie(tok_rest, ret_gn_g), lg)
    w_ret4, w_sb4, w_out4, w_ff14, w_ff24 = _gather_finish("gather_rest_finish", buf_rest, sem_rest, retg)
    w_ret = w_ret4.reshape(D_MODEL, D_MODEL)
    w_out2 = w_out4.reshape(D_MODEL, D_MODEL)
    w_ff2_2 = w_ff24.reshape(D_FF, D_MODEL)
    mixed, rb, sbp, y, h1, h2 = _mix_out(retg, sb, proj, x0, vecs, w_ret, w_sb4, w_out2)
    u, act, dout, df, st_a = _ffn_fwd_loss(h2, h1, target, vecs, w_ff14, w_ff2_2)

    du, dh1, dy, st_b = _ffn_bwd(df, u, h1, y, dout, vecs, w_ff14, w_ff2_2)
    grads = {"w_ff2": _weight_grad(act, df, 512, 1024, False, "grad_w_ff2")}
    grads["w_ff1"] = _weight_grad(h2, du, 512, 1024, True, "grad_w_ff1")
    drb, dsbp, da, dsb, dret, dg_r, st_c = _mix_ret_bwd(dy, proj, rb, sbp, o_raw, ret_gn_g, w_out2, w_sb4, w_ret)
    grads["w_out"] = _weight_grad(mixed, dy, 256, 1024, False, "grad_w_out")
    grads["w_ret"] = _weight_grad(retg, drb, 256, 1024, False, "grad_w_ret")
    grads["w_sb"] = _weight_grad(sb, dsbp, 512, 256, True, "grad_w_sb")

    bufs, sems, tok = _pair_send_start("rs_rest_pair_send", [_halves(grads[nm]) for nm in rest])
    dqkv_r = _ret_bwd(proj, cos, sin_s, dret, states, _tie(tok, lg))
    mine, theirs = _pair_send_wait("rs_rest_pair_recv", bufs, sems, dqkv_r)
    pair_sums = [_pair_add(g, r, cidx, "pair_add_" + nm) for g, r, nm in zip(mine, theirs, rest)]
    bufs, sems, tok = _chip_send_start("rs_rest_chip_send", pair_sums)
    dq_s, dk_s, dv_s = _sb_bwd(proj, dsb, tot, dep=tok)
    own, parts = _chip_send_wait("rs_rest_chip_recv", bufs, sems, dq_s)
    sums = [_chip_add(o, p, kidx, cidx, "chip_add_" + nm) for o, p, nm in zip(own, parts, rest)]
    bufs, sems, tok = _pair_swap_start("rs_rest_pair_swap", sums)
    dproj = [dqkv_r, dg_r, dq_s, dk_s, dv_s, da]
    g_in = _grad_w_in(h, dproj, dep=tok)
    full_rest = _pair_swap_wait("rs_rest_pair_swapped", bufs, sems, g_in)

    full_rest = dict(zip(rest, full_rest))
    bufs, sems, tok = _pair_send_start("rs_in_pair_send", [_halves(g_in)])
    out = {}
    for nm in ("w_out", "w_sb", "w_ret"):
        w = big[nm][0]
        out[nm] = _adamw(w, big_m[nm][0], big_v[nm][0], full_rest[nm].reshape(w.shape), "adamw_" + nm, dep=tok)
    mine, theirs = _pair_send_wait("rs_in_pair_recv", bufs, sems, out["w_ret"][1])
    bufs, sems, tok = _chip_send_start("rs_in_chip_send", [_pair_add(mine[0], theirs[0], cidx, "pair_add_w_in")])
    riding = ("w_ff2", "w_ff1")
    dx, st_d, updated = _in_proj_bwd(
        dproj, x0, dh1, _tie(tok, vecs), w_in4,
        [(big[nm][0], big_m[nm][0], big_v[nm][0], full_rest[nm].reshape(big[nm].shape[1:])) for nm in riding])
    out.update(zip(riding, updated))

    payload = jnp.concatenate([
        st_d[0:2], st_b[3:4], st_b[0:2], st_a[0:1],
        st_d[2:3], st_b[4:5], st_b[2:3], st_a[1:2],
        st_c[0:1], st_a[2:3]], axis=0)

    g_ada, gsm, dsm, mosm, vosm, loss = _small_exchange(
        payload.reshape(N_PAY, 1, D_MODEL), c_all, wsm, msm, vsm)
    ada_out = _adamw(ada_w[0], m_ada_w[0], v_ada_w[0], g_ada, "adamw_ada_w")

    own, parts = _chip_send_wait("rs_in_chip_recv", bufs, sems, ada_out[1])
    bufs, sems, tok = _pair_swap_start(
        "rs_in_pair_swap", [_chip_add(own[0], parts[0], kidx, cidx, "chip_add_w_in")])
    (full_in,) = _pair_swap_wait("rs_in_pair_swapped", bufs, sems, tok)
    out["w_in"] = _adamw(w_in[0], m_w_in[0], v_w_in[0], full_in.reshape(w_in.shape[1:]), "adamw_w_in")

    def unpack(tab):
        return [tab[0:6].reshape(1, 6 * D_MODEL)] + [tab[6 + r:7 + r] for r in range(5)]

    def ordered(which):
        sm = unpack([gsm, dsm, mosm, vosm][which])
        bg = [out[nm][which][None] for nm in names]
        return [ada_out[which][None], sm[0], sm[1], sm[2], sm[3], sm[4], bg[0], sm[5]] + bg[1:]

    return (loss.reshape(()), dx[None], *ordered(0), *ordered(1), *ordered(2), *ordered(3))
```

```python
import functools

import numpy as np
import jax
import jax.numpy as jnp
from jax import lax
from jax.experimental import pallas as pl
from jax.experimental.pallas import tpu as pltpu

SEQ = 2048
D_MODEL = 1024
D_IN = 6656
D_FF = 4096
N_CHIPS = 4
EPS = 1e-6
ROPE_BASE = 10000.0
RET_BLOCK = 256
RET_CHUNK_SHIFT = 6
SB_BLOCK = 256
QK_SCALE = 0.125
N_PAIRS = 4
SB_GROUP = 2

ADAM_LR = 0.001
ADAM_B1 = 0.9
ADAM_B2 = 0.999
ADAM_EPS = 1e-08
ADAM_WD = 0.01
ADAM_STEP = 10

BF = jnp.bfloat16
F32 = jnp.float32
MESH = pl.DeviceIdType.MESH
VMEM_LIMIT = 56 * 1024 * 1024
ANY = pl.BlockSpec(memory_space=pl.ANY)

C_QR, C_KR, C_VR, C_GR, C_QS, C_KS, C_VS, C_AR, C_AS = 0, 512, 1024, 2048, 3072, 3584, 4096, 4608, 5632

V_SH1, V_SC1, V_GT1, V_SH2, V_SC2, V_GT2, V_G1, V_G2, V_G3, V_G4 = range(10)
P_DSH1, P_DSC1, P_DGT1, P_DSH2, P_DSC2, P_DGT2, P_DG1, P_DG2, P_DG3, P_DG4, P_DGN, P_LOSS = range(12)
N_PAY = 12


def _cp(sem=None, **kw):
    if sem is not None:
        kw["dimension_semantics"] = sem
    return pltpu.CompilerParams(vmem_limit_bytes=VMEM_LIMIT, **kw)


def _dot(a, b):
    return jnp.dot(a, b, preferred_element_type=F32)


def _dot_nt(a, b):
    return lax.dot_general(a, b, (((1,), (1,)), ((), ())), preferred_element_type=F32)


def _dot_tn(a, b):
    return lax.dot_general(a, b, (((0,), (0,)), ((), ())), preferred_element_type=F32)


def _row(ref, i):
    return ref[i:i + 1, :]


def _rms(v):
    return lax.rsqrt(jnp.mean(v * v, axis=1, keepdims=True) + EPS)


def _colsum(v):
    return jnp.sum(v, axis=0, keepdims=True)


def _rowmean(v):
    return jnp.mean(v, axis=1, keepdims=True)


def _sigmoid(v):
    return 1.0 / (1.0 + jnp.exp(-v))


def _cast_bf16(w, kidx, dep, name):
    rows, cols = w.shape
    tr = min(rows, 512)

    def body(k_ref, w_ref, dep_ref, o_ref):
        o_ref[...] = w_ref[...].astype(BF)

    return pl.pallas_call(
        body, name=name,
        grid_spec=pltpu.PrefetchScalarGridSpec(
            num_scalar_prefetch=1, grid=(rows // tr,),
            in_specs=[pl.BlockSpec((tr, cols), lambda i, k_ref: (i, 0)), ANY],
            out_specs=pl.BlockSpec((None, tr, cols), lambda i, k_ref: (k_ref[0], i, 0))),
        out_shape=jax.ShapeDtypeStruct((N_CHIPS, rows, cols), BF),
        compiler_params=_cp(("parallel",)),
    )(kidx, w, dep)


def _adamw_math(w, g, m, v):
    m = ADAM_B1 * m + (1.0 - ADAM_B1) * g
    v = ADAM_B2 * v + (1.0 - ADAM_B2) * (g * g)
    m_hat = m / (1.0 - ADAM_B1 ** ADAM_STEP)
    v_hat = v / (1.0 - ADAM_B2 ** ADAM_STEP)
    delta = -ADAM_LR * (m_hat / (jnp.sqrt(v_hat) + ADAM_EPS) + ADAM_WD * w)
    return delta, m, v


def _adamw(w, m, v, g, name, dep=None):
    rows, cols = w.shape
    tr = min(rows, 256)

    def body(w_ref, m_ref, v_ref, g_ref, go_ref, d_ref, mo_ref, vo_ref):
        gg = g_ref[...]
        d, mm, vv = _adamw_math(w_ref[...], gg, m_ref[...], v_ref[...])
        go_ref[...] = gg
        d_ref[...] = d
        mo_ref[...] = mm
        vo_ref[...] = vv

    spec = pl.BlockSpec((tr, cols), lambda i: (i, 0))
    shp = jax.ShapeDtypeStruct((rows, cols), F32)
    body, in_specs, args = _add_dep(body, [spec] * 4, [w, m, v, g], dep)
    return pl.pallas_call(
        body, name=name, grid=(rows // tr,),
        in_specs=in_specs, out_specs=[spec] * 4, out_shape=[shp] * 4,
        compiler_params=_cp(("parallel",)),
    )(*args)


def _place():
    x, y, c = lax.axis_index("x"), lax.axis_index("y"), lax.axis_index("c")
    return x, y, c


HBM = pl.BlockSpec(memory_space=pltpu.HBM)
SEM = pl.BlockSpec(memory_space=pltpu.SEMAPHORE)
EFFECT = pltpu.SideEffectType.DATAFLOW_SIDE_EFFECTING


def _tie(token, small):
    return small + token[0, 0]


def _add_dep(body, in_specs, args, dep):
    if dep is None:
        return body, list(in_specs), list(args)
    n = len(args)

    def wrapped(*refs):
        body(*refs[:n], *refs[n + 1:])

    return wrapped, list(in_specs) + [ANY], list(args) + [dep]


def _split_call(name, bufs, run, old=None, after=None, new=0):
    nb = len(bufs)
    n_old = 2 if old is not None else 0
    n_in = nb + n_old + (1 if after is not None else 0)

    def body(*refs):
        old_sems = (refs[nb], refs[nb + 1]) if old is not None else None
        new_sems = (refs[n_in], refs[n_in + 1]) if new else None
        run(refs[:nb], old_sems, new_sems)
        if new:
            refs[-1][...] = jnp.zeros_like(refs[-1])

    in_specs = [HBM] * nb + [SEM] * n_old + ([ANY] if after is not None else [])
    out_shape = [pltpu.SemaphoreType.DMA((new,))] * 2 if new else []
    out_specs = [SEM, SEM] if new else []
    out_shape += [pltpu.HBM(b.shape, b.dtype) for b in bufs]
    out_specs += [HBM] * nb
    if new:
        out_shape.append(jax.ShapeDtypeStruct((8, 128), F32))
        out_specs.append(pl.BlockSpec(memory_space=pltpu.VMEM))
    first = 2 if new else 0
    args = [pltpu.with_memory_space_constraint(b, pltpu.HBM) for b in bufs]
    if old is not None:
        args += [old[0], old[1]]
    if after is not None:
        args.append(after)
    outs = pl.pallas_call(
        body, name=name, in_specs=tuple(in_specs), out_specs=tuple(out_specs), out_shape=tuple(out_shape),
        input_output_aliases={i: i + first for i in range(nb)},
        compiler_params=pltpu.CompilerParams(has_side_effects=EFFECT),
    )(*args)
    thru = list(outs[first:first + nb])
    if new:
        return thru, (outs[0], outs[1]), outs[-1]
    return thru, None, None


def _remote(part_src, part_dst, sems, i, to):
    return pltpu.make_async_remote_copy(src_ref=part_src, dst_ref=part_dst, send_sem=sems[0].at[i],
                                        recv_sem=sems[1].at[i], device_id=to, device_id_type=MESH)


def _other_chips(x, y):
    return [(1 - x, y), (x, 1 - y), (1 - x, 1 - y)]


def _gather_start(name, bufs, after=None):
    def run(refs, old, new):
        x, y, c = _place()
        k = 2 * x + y
        for w, ref in enumerate(refs):
            rh = bufs[w].shape[1] // 2
            part = ref.at[k, pl.ds(c * rh, rh)]
            for j, (cx, cy) in enumerate(_other_chips(x, y)):
                _remote(part, part, new, 3 * w + j, (cx, cy, c)).start()

    return _split_call(name, bufs, run, after=after, new=3 * len(bufs))


def _gather_pass(name, bufs, sems, after):
    def run(refs, old, new):
        x, y, c = _place()
        k = 2 * x + y
        sib = (x, y, 1 - c)
        for w, ref in enumerate(refs):
            rh = bufs[w].shape[1] // 2
            for j, (cx, cy) in enumerate(_other_chips(x, y)):
                land = ref.at[2 * cx + cy, pl.ds(c * rh, rh)]
                _remote(land, land, old, 3 * w + j, (cx, cy, c)).wait_recv()
                _remote(land, land, new, 3 * w + j, sib).start()
        for w, ref in enumerate(refs):
            rh = bufs[w].shape[1] // 2
            part = ref.at[k, pl.ds(c * rh, rh)]
            for j, (cx, cy) in enumerate(_other_chips(x, y)):
                _remote(part, part, old, 3 * w + j, (cx, cy, c)).wait_send()

    return _split_call(name, bufs, run, old=sems, after=after, new=3 * len(bufs))


def _gather_finish(name, bufs, sems, after):
    def run(refs, old, new):
        x, y, c = _place()
        sib = (x, y, 1 - c)
        for w, ref in enumerate(refs):
            rh = bufs[w].shape[1] // 2
            for j, (cx, cy) in enumerate(_other_chips(x, y)):
                sent = ref.at[2 * cx + cy, pl.ds(c * rh, rh)]
                _remote(sent, sent, old, 3 * w + j, sib).wait_send()
                land = ref.at[2 * cx + cy, pl.ds((1 - c) * rh, rh)]
                _remote(land, land, old, 3 * w + j, sib).wait_recv()

    return _split_call(name, bufs, run, old=sems, after=after)[0]


def _pair_send_start(name, grads):
    n = len(grads)
    lands = [lax.empty((N_CHIPS,) + g.shape[2:], g.dtype) for g in grads]

    def run(refs, old, new):
        x, y, c = _place()
        for w in range(n):
            _remote(refs[w].at[:, 1 - c], refs[n + w], new, w, (x, y, 1 - c)).start()

    return _split_call(name, list(grads) + lands, run, new=n)


def _pair_send_wait(name, bufs, sems, after):
    n = len(bufs) // 2

    def run(refs, old, new):
        x, y, c = _place()
        for w in range(n):
            cp = _remote(refs[w].at[:, 1 - c], refs[n + w], old, w, (x, y, 1 - c))
            cp.wait_send()
            cp.wait_recv()

    thru = _split_call(name, bufs, run, old=sems, after=after)[0]
    return thru[:n], thru[n:]


def _pair_add(g, recv, cidx, name):
    _, _, rh, cols = g.shape
    tr = min(rh, 256)

    def body(c_ref, g_ref, r_ref, o_ref):
        o_ref[...] = (g_ref[...].astype(F32) + r_ref[...].astype(F32)).astype(BF)

    return pl.pallas_call(
        body, name=name,
        grid_spec=pltpu.PrefetchScalarGridSpec(
            num_scalar_prefetch=1, grid=(rh // tr,),
            in_specs=[pl.BlockSpec((N_CHIPS, None, tr, cols), lambda i, c_ref: (0, c_ref[0], i, 0)),
                      pl.BlockSpec((N_CHIPS, tr, cols), lambda i, c_ref: (0, i, 0))],
            out_specs=pl.BlockSpec((N_CHIPS, tr, cols), lambda i, c_ref: (0, i, 0))),
        out_shape=jax.ShapeDtypeStruct((N_CHIPS, rh, cols), BF),
        compiler_params=_cp(("parallel",)),
    )(cidx, g, recv)


def _chip_send_start(name, sums):
    n = len(sums)
    lands = [lax.empty((3,) + s.shape[1:], BF) for s in sums]

    def run(refs, old, new):
        x, y, c = _place()
        for w in range(n):
            for j, (cx, cy) in enumerate(_other_chips(x, y)):
                _remote(refs[w].at[2 * cx + cy], refs[n + w].at[j], new, 3 * w + j, (cx, cy, c)).start()

    return _split_call(name, list(sums) + lands, run, new=3 * n)


def _chip_send_wait(name, bufs, sems, after):
    n = len(bufs) // 2

    def run(refs, old, new):
        x, y, c = _place()
        for w in range(n):
            for j, (cx, cy) in enumerate(_other_chips(x, y)):
                cp = _remote(refs[w].at[2 * cx + cy], refs[n + w].at[j], old, 3 * w + j, (cx, cy, c))
                cp.wait_send()
                cp.wait_recv()

    thru = _split_call(name, bufs, run, old=sems, after=after)[0]
    return thru[:n], thru[n:]


def _chip_add(own, parts, kidx, cidx, name):
    _, rh, cols = parts.shape
    tr = min(rh, 512)

    def body(k_ref, c_ref, own_ref, p_ref, o_ref):
        acc = own_ref[...].astype(F32)
        for s in range(3):
            acc = acc + p_ref[s].astype(F32)
        o_ref[...] = acc

    return pl.pallas_call(
        body, name=name,
        grid_spec=pltpu.PrefetchScalarGridSpec(
            num_scalar_prefetch=2, grid=(rh // tr,),
            in_specs=[pl.BlockSpec((None, tr, cols), lambda i, k_ref, c_ref: (k_ref[0], i, 0)),
                      pl.BlockSpec((3, tr, cols), lambda i, k_ref, c_ref: (0, i, 0))],
            out_specs=pl.BlockSpec((None, tr, cols), lambda i, k_ref, c_ref: (c_ref[0], i, 0))),
        out_shape=jax.ShapeDtypeStruct((2, rh, cols), F32),
        compiler_params=_cp(("parallel",)),
    )(kidx, cidx, own, parts)


def _pair_swap_start(name, bufs):
    def run(refs, old, new):
        x, y, c = _place()
        for w, ref in enumerate(refs):
            _remote(ref.at[c], ref.at[c], new, w, (x, y, 1 - c)).start()

    return _split_call(name, bufs, run, new=len(bufs))


def _pair_swap_wait(name, bufs, sems, after):
    def run(refs, old, new):
        x, y, c = _place()
        for w, ref in enumerate(refs):
            _remote(ref.at[c], ref.at[c], old, w, (x, y, 1 - c)).wait_send()
            _remote(ref.at[1 - c], ref.at[1 - c], old, w, (x, y, 1 - c)).wait_recv()

    return _split_call(name, bufs, run, old=sems, after=after)[0]


def _peers(x, y, c):
    out = []
    for code in range(1, 8):
        fx, fy, fc = (code >> 2) & 1, (code >> 1) & 1, code & 1
        px = 1 - x if fx else x
        py = 1 - y if fy else y
        pc = 1 - c if fc else c
        out.append((code, (px, py, pc)))
    return out


def _mod_exchange(c_row, ada_w, ada_b4, deps):
    ncol = ada_w.shape[1]

    def body(c_ref, w_ref, b_ref, *rest):
        call_ref, mod_ref, part_ref, send_sems, recv_sems = rest[len(deps):]
        x, y, c = _place()
        k = 2 * x + y
        me = 4 * x + 2 * y + c
        call_ref[pl.ds(me, 1), :] = c_ref[...]
        sends = []
        for code, peer in _peers(x, y, c):
            cp = pltpu.make_async_remote_copy(
                src_ref=c_ref, dst_ref=call_ref.at[pl.ds(me, 1), :],
                send_sem=send_sems.at[code], recv_sem=recv_sems.at[code],
                device_id=peer, device_id_type=MESH)
            cp.start()
            sends.append(cp)
        for code, (px, py, pc) in _peers(x, y, c):
            land = call_ref.at[pl.ds(4 * px + 2 * py + pc, 1), :]
            pltpu.make_async_remote_copy(
                src_ref=land, dst_ref=land, send_sem=send_sems.at[code], recv_sem=recv_sems.at[code],
                device_id=(px, py, pc), device_id_type=MESH).wait_recv()
        call = call_ref[...]
        act = call * _sigmoid(call)
        part = jnp.dot(act, w_ref[...], preferred_element_type=F32,
                       precision=lax.Precision.HIGHEST) + b_ref[pl.ds(k, 1), :]
        part_ref[...] = part
        mod_ref[pl.ds(k, 1), :] = part_ref[pl.ds(me, 1), :]
        chips = [(8 + j, peer) for j, (code, peer) in enumerate(_peers(x, y, c)) if code in (2, 4, 6)]
        for slot, (px, py, pc) in chips:
            cp = pltpu.make_async_remote_copy(
                src_ref=part_ref.at[pl.ds(4 * px + 2 * py + pc, 1), :], dst_ref=mod_ref.at[pl.ds(k, 1), :],
                send_sem=send_sems.at[slot], recv_sem=recv_sems.at[slot],
                device_id=(px, py, pc), device_id_type=MESH)
            cp.start()
            sends.append(cp)
        for slot, (px, py, pc) in chips:
            land = mod_ref.at[pl.ds(2 * px + py, 1), :]
            pltpu.make_async_remote_copy(
                src_ref=land, dst_ref=land, send_sem=send_sems.at[slot], recv_sem=recv_sems.at[slot],
                device_id=(px, py, pc), device_id_type=MESH).wait_recv()
        for cp in sends:
            cp.wait_send()

    vm = pl.BlockSpec(memory_space=pltpu.VMEM)
    return pl.pallas_call(
        body, name="mod_exchange",
        in_specs=[vm, vm, vm] + [ANY] * len(deps), out_specs=[vm, vm],
        out_shape=[jax.ShapeDtypeStruct((8, D_MODEL), F32), jax.ShapeDtypeStruct((N_CHIPS, ncol), F32)],
        scratch_shapes=[pltpu.VMEM((8, ncol), F32), pltpu.SemaphoreType.DMA((16,)),
                        pltpu.SemaphoreType.DMA((16,))],
        compiler_params=_cp(),
    )(c_row, ada_w, ada_b4, *deps)


def _small_exchange(payload, c_all, wsm, msm, vsm):
    ncol = 6 * D_MODEL // N_CHIPS

    def body(p_ref, call_ref, w_ref, m_ref, v_ref, gw_ref, g_ref, d_ref, mo_ref, vo_ref, loss_ref,
             all_ref, dm_ref, send_sems, recv_sems):
        x, y, c = _place()
        k = 2 * x + y
        me = 4 * x + 2 * y + c
        all_ref[:, pl.ds(me, 1), :] = p_ref[...]
        sends = []
        for code, peer in _peers(x, y, c):
            cp = pltpu.make_async_remote_copy(
                src_ref=p_ref, dst_ref=all_ref.at[:, pl.ds(me, 1), :],
                send_sem=send_sems.at[code], recv_sem=recv_sems.at[code],
                device_id=peer, device_id_type=MESH)
            cp.start()
            sends.append(cp)
        for code, (px, py, pc) in _peers(x, y, c):
            land = all_ref.at[:, pl.ds(4 * px + 2 * py + pc, 1), :]
            pltpu.make_async_remote_copy(
                src_ref=land, dst_ref=land, send_sem=send_sems.at[code], recv_sem=recv_sems.at[code],
                device_id=(px, py, pc), device_id_type=MESH).wait_recv()
        for cp in sends:
            cp.wait_send()
        tot = [_colsum(all_ref[r]) for r in range(N_PAY)]
        loss_ref[...] = jnp.sum(tot[P_LOSS], axis=1, keepdims=True)
        g_ref[...] = jnp.zeros_like(g_ref)
        for r in range(P_LOSS):
            g_ref[r:r + 1, :] = tot[r]
        g = g_ref[...]
        d, mm, vv = _adamw_math(w_ref[...], g, m_ref[...], v_ref[...])
        d_ref[...] = d
        mo_ref[...] = mm
        vo_ref[...] = vv
        half = D_MODEL // 2
        for kk in range(N_CHIPS):
            @pl.when(k == kk)
            def _():
                r0 = 3 * (kk // 2)
                if kk % 2 == 0:
                    dm_ref[:, :D_MODEL] = all_ref[r0]
                    dm_ref[:, D_MODEL:] = all_ref[r0 + 1][:, :half]
                else:
                    dm_ref[:, :half] = all_ref[r0 + 1][:, half:]
                    dm_ref[:, half:] = all_ref[r0 + 2]
        call = call_ref[...]
        act = call * _sigmoid(call)
        gw_ref[...] = lax.dot_general(act, dm_ref[...], (((0,), (0,)), ((), ())),
                                      preferred_element_type=F32, precision=lax.Precision.HIGHEST)

    vm = pl.BlockSpec(memory_space=pltpu.VMEM)
    small = jax.ShapeDtypeStruct((16, D_MODEL), F32)
    return pl.pallas_call(
        body, name="small_exchange",
        in_specs=[vm] * 5, out_specs=[vm] * 6,
        out_shape=[jax.ShapeDtypeStruct((D_MODEL, ncol), F32), small, small, small, small,
                   jax.ShapeDtypeStruct((1, 1), F32)],
        scratch_shapes=[pltpu.VMEM((N_PAY, 8, D_MODEL), F32), pltpu.VMEM((8, ncol), F32),
                        pltpu.SemaphoreType.DMA((8,)), pltpu.SemaphoreType.DMA((8,))],
        compiler_params=_cp(),
    )(payload, c_all, wsm, msm, vsm)


def _rope_tables(pos_col, inv_freq, sign):
    def body(p_ref, f_ref, s_ref, cos_ref, sin_ref):
        ang = p_ref[...].astype(F32) * f_ref[...]
        cos_ref[...] = jnp.cos(ang)
        sin_ref[...] = jnp.sin(ang) * s_ref[...]

    tr = 512
    shp = jax.ShapeDtypeStruct((SEQ, 128), F32)
    return pl.pallas_call(
        body, name="rope_tables", grid=(SEQ // tr,),
        in_specs=[pl.BlockSpec((tr, 1), lambda i: (i, 0)), pl.BlockSpec((1, 128), lambda i: (0, 0)),
                  pl.BlockSpec((1, 128), lambda i: (0, 0))],
        out_specs=[pl.BlockSpec((tr, 128), lambda i: (i, 0))] * 2, out_shape=[shp, shp],
        compiler_params=_cp(("parallel",)),
    )(pos_col, inv_freq, sign)


def _resident(shape):
    nd = len(shape)
    return pl.BlockSpec(shape, lambda *_: (0,) * nd, pipeline_mode=pl.Buffered(1))


def _stream_in(step, srcs, bufs, sems):
    copies = [pltpu.make_async_copy(src, buf, sems.at[k]) for k, (src, buf) in enumerate(zip(srcs, bufs))]

    @pl.when(step == 0)
    def _():
        for cp in copies:
            cp.start()

    def fetch(k):
        @pl.when(step == 0)
        def _():
            copies[k].wait()

        return bufs[k]

    return fetch


def _ln_proj(x, vecs, w_in4):
    tm = 256
    wc = w_in4.shape[2]

    def body(x_ref, vec_ref, w_hbm, h_ref, proj_ref, w_buf, sems):
        fetch = _stream_in(pl.program_id(0), [w_hbm.at[j] for j in range(N_CHIPS)],
                           [w_buf.at[j] for j in range(N_CHIPS)], sems)
        xx = x_ref[...]
        g = _row(vec_ref, V_G1) * (1.0 + _row(vec_ref, V_SC1))
        h = (xx * _rms(xx) * g + _row(vec_ref, V_SH1)).astype(BF)
        h_ref[...] = h
        for j in range(N_CHIPS):
            proj_ref[:, j * wc:(j + 1) * wc] = _dot(h, fetch(j)[...]).astype(BF)

    return pl.pallas_call(
        body, name="ln_proj", grid=(SEQ // tm,),
        in_specs=[pl.BlockSpec((tm, D_MODEL), lambda i: (i, 0)), _resident((16, D_MODEL)), ANY],
        out_specs=[pl.BlockSpec((tm, D_MODEL), lambda i: (i, 0)), pl.BlockSpec((tm, D_IN), lambda i: (i, 0))],
        out_shape=[jax.ShapeDtypeStruct((SEQ, D_MODEL), BF), jax.ShapeDtypeStruct((SEQ, D_IN), BF)],
        scratch_shapes=[pltpu.VMEM(w_in4.shape, BF), pltpu.SemaphoreType.DMA((N_CHIPS,))],
        compiler_params=_cp(("arbitrary",)),
    )(x, vecs, w_in4)


def _lane_first(shape):
    lane = lax.broadcasted_iota(jnp.int32, shape, 1)
    return (lane & 32) == 0


def _rot(v, cos, sin_s):
    partner = jnp.where(_lane_first(v.shape), pltpu.roll(v, 96, 1), pltpu.roll(v, 32, 1))
    return v * cos + partner * sin_s


def _rot_t(dv, cos, sin_s):
    t = dv * sin_s
    partner = jnp.where(_lane_first(dv.shape), pltpu.roll(t, 96, 1), pltpu.roll(t, 32, 1))
    return dv * cos + partner


def _ret_masks(lg):
    t = RET_BLOCK
    ii = lax.broadcasted_iota(jnp.int32, (t, t), 0)
    jj = lax.broadcasted_iota(jnp.int32, (t, t), 1)
    dist = jnp.abs(ii - jj).astype(F32)
    future = (jj >> RET_CHUNK_SHIFT) > (ii >> RET_CHUNK_SHIFT)
    mask = jnp.where(future, 0.0, jnp.exp(lg * dist))
    ti = lax.broadcasted_iota(jnp.int32, (t, 1), 0).astype(F32)
    from_start = jnp.exp(lg * (ti + 1.0))
    to_end = jnp.exp(lg * (t - 1.0 - ti))
    whole = jnp.exp(jnp.full((1, 128), lg * t, F32))
    return mask, from_start, to_end, whole


def _head_lanes(shape, hh):
    lane = lax.broadcasted_iota(jnp.int32, shape, 1)
    return (lane >> 6) == hh


def _ret_specs():
    t = RET_BLOCK
    return dict(
        q=lambda f: pl.BlockSpec((t, 512), lambda n: (f(n), C_QR // 512)),
        k=lambda f: pl.BlockSpec((t, 512), lambda n: (f(n), C_KR // 512)),
        v=lambda f: pl.BlockSpec((t, D_MODEL), lambda n: (f(n), C_VR // D_MODEL)),
        g=lambda f: pl.BlockSpec((t, D_MODEL), lambda n: (f(n), C_GR // D_MODEL)),
        tab=lambda f: pl.BlockSpec((t, 128), lambda n: (f(n), 0)),
        wide=lambda f: pl.BlockSpec((t, D_MODEL), lambda n: (f(n), 0)),
        state=lambda f: pl.BlockSpec((N_PAIRS, None, 2, 128, 128), lambda n: (0, f(n), 0, 0, 0)),
    )


def _ret_fwd(proj, cos, sin_s, gn_g, log_gamma):
    t = RET_BLOCK
    nb = SEQ // t

    def body(lg_ref, q_ref, k_ref, v_ref, g_ref, cos_ref, sin_ref, gn_ref, o_ref, retg_ref, st_ref, state):
        @pl.when(pl.program_id(0) == 0)
        def _():
            state[...] = jnp.zeros_like(state)

        cos, sn = cos_ref[...], sin_ref[...]
        for p in range(N_PAIRS):
            q = _rot(q_ref[:, 128 * p:128 * (p + 1)].astype(F32), cos, sn)
            k = _rot(k_ref[:, 128 * p:128 * (p + 1)].astype(F32), cos, sn) * QK_SCALE
            for hh in range(2):
                cols = slice(256 * p + 128 * hh, 256 * p + 128 * (hh + 1))
                lg = lg_ref[2 * p + hh]
                mask, from_start, to_end, whole = _ret_masks(lg)
                lanes = _head_lanes(q.shape, hh)
                qm = jnp.where(lanes, q, 0.0)
                km = jnp.where(lanes, k, 0.0)
                vh = v_ref[:, cols]
                sc = _dot_nt(qm.astype(BF), km.astype(BF)) * mask
                st = state[p, hh]
                st_ref[p, hh] = st
                o = _dot(sc.astype(BF), vh) + _dot((qm * from_start).astype(BF), st.astype(BF))
                state[p, hh] = whole * st + _dot_tn((km * to_end).astype(BF), vh)
                d = o - _rowmean(o)
                nh = d * lax.rsqrt(_rowmean(d * d) + EPS)
                gr = g_ref[:, cols].astype(F32)
                o_ref[:, cols] = o
                retg_ref[:, cols] = (gr * _sigmoid(gr) * nh * gn_ref[:, cols]).astype(BF)

    sp = _ret_specs()
    ident = lambda n: n
    return pl.pallas_call(
        body, name="ret_fwd", grid=(nb,),
        in_specs=[pl.BlockSpec(memory_space=pltpu.SMEM), sp["q"](ident), sp["k"](ident), sp["v"](ident),
                  sp["g"](ident), sp["tab"](ident), sp["tab"](ident), _resident((1, D_MODEL))],
        out_specs=[sp["wide"](ident), sp["wide"](ident), sp["state"](ident)],
        out_shape=[jax.ShapeDtypeStruct((SEQ, D_MODEL), F32), jax.ShapeDtypeStruct((SEQ, D_MODEL), BF),
                   jax.ShapeDtypeStruct((N_PAIRS, nb, 2, 128, 128), F32)],
        scratch_shapes=[pltpu.VMEM((N_PAIRS, 2, 128, 128), F32)],
        compiler_params=_cp(("arbitrary",)),
    )(log_gamma, proj, proj, proj, proj, cos, sin_s, gn_g)


def _ret_bwd(proj, cos, sin_s, dret, states, log_gamma):
    t = RET_BLOCK
    nb = SEQ // t

    def body(lg_ref, q_ref, k_ref, v_ref, cos_ref, sin_ref, do_ref, st_ref, dqkv_ref, dstate):
        @pl.when(pl.program_id(0) == 0)
        def _():
            dstate[...] = jnp.zeros_like(dstate)

        cos, sn = cos_ref[...], sin_ref[...]
        for p in range(N_PAIRS):
            q = _rot(q_ref[:, 128 * p:128 * (p + 1)].astype(F32), cos, sn)
            k = _rot(k_ref[:, 128 * p:128 * (p + 1)].astype(F32), cos, sn) * QK_SCALE
            dq_rot = jnp.zeros(q.shape, F32)
            dk_rot = jnp.zeros(q.shape, F32)
            for hh in range(2):
                cols = slice(256 * p + 128 * hh, 256 * p + 128 * (hh + 1))
                lg = lg_ref[2 * p + hh]
                mask, from_start, to_end, whole = _ret_masks(lg)
                lanes = _head_lanes(q.shape, hh)
                qm = jnp.where(lanes, q, 0.0)
                km = jnp.where(lanes, k, 0.0)
                qb, kb = qm.astype(BF), km.astype(BF)
                vh = v_ref[:, cols]
                do = do_ref[:, cols]
                sc = (_dot_nt(qb, kb) * mask).astype(BF)
                st = st_ref[p, hh].astype(BF)
                dst = dstate[p, hh]
                dstb = dst.astype(BF)
                k_end = (km * to_end).astype(BF)
                q_start = (qm * from_start).astype(BF)
                dqkv_ref[:, C_VR + 256 * p + 128 * hh:C_VR + 256 * p + 128 * (hh + 1)] = (
                    _dot_tn(sc, do) + _dot(k_end, dstb)).astype(BF)
                dsc = (_dot_nt(do, vh) * mask).astype(BF)
                dq_h = _dot(dsc, kb) + _dot_nt(do, st) * from_start
                dq_rot = dq_rot + jnp.where(lanes, dq_h, 0.0)
                dk_rot = dk_rot + _dot_tn(dsc, qb) + _dot_nt(vh, dstb) * to_end
                dstate[p, hh] = whole * dst + _dot_tn(q_start, do)
            dqkv_ref[:, C_QR + 128 * p:C_QR + 128 * (p + 1)] = _rot_t(dq_rot, cos, sn).astype(BF)
            dqkv_ref[:, C_KR + 128 * p:C_KR + 128 * (p + 1)] = _rot_t(dk_rot * QK_SCALE, cos, sn).astype(BF)

    sp = _ret_specs()
    rev = lambda n: nb - 1 - n
    return pl.pallas_call(
        body, name="ret_bwd", grid=(nb,),
        in_specs=[pl.BlockSpec(memory_space=pltpu.SMEM), sp["q"](rev), sp["k"](rev), sp["v"](rev),
                  sp["tab"](rev), sp["tab"](rev), sp["wide"](rev), sp["state"](rev)],
        out_specs=pl.BlockSpec((t, C_GR), lambda n: (rev(n), 0)),
        out_shape=jax.ShapeDtypeStruct((SEQ, C_GR), BF),
        scratch_shapes=[pltpu.VMEM((N_PAIRS, 2, 128, 128), F32)],
        compiler_params=_cp(("arbitrary",)),
    )(log_gamma, proj, proj, proj, cos, sin_s, dret, states)


def _stack_heads(v):
    return jnp.concatenate([jnp.where(_head_lanes(v.shape, hh), v, jnp.zeros_like(v)) for hh in range(2)], axis=0)


def _unstack_heads(v):
    t = v.shape[0] // 2
    return jnp.where(_head_lanes((t, v.shape[1]), 0), v[:t], v[t:])


def _sb_masks(t, heads):
    rr = lax.broadcasted_iota(jnp.int32, (t, t), 0)
    cc = lax.broadcasted_iota(jnp.int32, (t, t), 1)
    r2 = lax.broadcasted_iota(jnp.int32, (heads * t, t), 0) & (t - 1)
    c2 = lax.broadcasted_iota(jnp.int32, (heads * t, t), 1)
    return rr, cc, c2 < r2


def _split_dot2(v, tri):
    return _dot(v.astype(BF), tri)


def _log_one_minus_beta(z):
    return -(jnp.maximum(z, 0.0) + jnp.log(1.0 + jnp.exp(-jnp.abs(z))))


def _sb_fwd(proj):
    t, g = SB_BLOCK, SB_GROUP
    nq = SEQ // t
    rows = 2 * g * t

    def body(q_ref, k_ref, v_ref, o_ref, tot_ref, kt_ref):
        i = pl.program_id(1)

        @pl.when(i == 0)
        def _():
            for p in range(g):
                for jj in range(nq):
                    kt_ref[p, jj] = k_ref[jj * t:(jj + 1) * t, 128 * p:128 * (p + 1)].T

        q2 = [_stack_heads((q_ref[:, 128 * p:128 * (p + 1)].astype(F32) * QK_SCALE).astype(BF)) for p in range(g)]
        rr, cc, valid = _sb_masks(t, 2 * g)
        later = (rr > cc).astype(BF)

        def tile(j, carry, diagonal):
            acc, run = carry
            z = jnp.concatenate([_dot(q2[p], kt_ref[p, j]) for p in range(g)], axis=0)
            lm = _log_one_minus_beta(z)
            if diagonal:
                lm = jnp.where(valid, lm, 0.0)
            after = _split_dot2(lm, later)
            a = jnp.exp(z + lm + after + run)
            if diagonal:
                a = jnp.where(valid, a, 0.0)
            ab = a.astype(BF)
            keys = pl.ds(pl.multiple_of(j * t, t), t)
            av = jnp.concatenate([_dot(ab[2 * t * p:2 * t * (p + 1)], v_ref[keys, 128 * p:128 * (p + 1)])
                                  for p in range(g)], axis=0)
            return acc + av, run + after[:, 0:1] + lm[:, 0:1]

        carry = tile(i, (jnp.zeros((rows, 128), F32), jnp.zeros((rows, 1), F32)), True)
        acc, run = lax.fori_loop(0, i, lambda s, cr: tile(i - 1 - s, cr, False), carry)
        run = jnp.broadcast_to(run, (rows, 128))
        for p in range(g):
            o_ref[:, 128 * p:128 * (p + 1)] = _unstack_heads(acc[2 * t * p:2 * t * (p + 1)]).astype(BF)
            tot_ref[:, 128 * p:128 * (p + 1)] = _unstack_heads(run[2 * t * p:2 * t * (p + 1)])

    w = 128 * g
    return pl.pallas_call(
        body, name="sb_fwd", grid=(N_PAIRS // g, nq),
        in_specs=[pl.BlockSpec((t, w), lambda p, i: (i, C_QS // w + p)),
                  pl.BlockSpec((SEQ, w), lambda p, i: (0, C_KS // w + p)),
                  pl.BlockSpec((SEQ, w), lambda p, i: (0, C_VS // w + p))],
        out_specs=[pl.BlockSpec((t, w), lambda p, i: (i, p))] * 2,
        out_shape=[jax.ShapeDtypeStruct((SEQ, 512), BF), jax.ShapeDtypeStruct((SEQ, 512), F32)],
        scratch_shapes=[pltpu.VMEM((g, nq, 128, t), BF)],
        compiler_params=_cp(("parallel", "arbitrary")),
    )(proj, proj, proj)


def _sb_bwd(proj, dsb, tot, dep=None):
    t, g = SB_BLOCK, SB_GROUP
    nq = SEQ // t
    rows = 2 * g * t

    def body(q_ref, k_ref, v_ref, do_ref, tot_ref, dq_ref, dk_ref, dv_ref, kt_ref, vt_ref, dkt_acc, dvt_acc):
        i = pl.program_id(1)

        @pl.when(i == 0)
        def _():
            dkt_acc[...] = jnp.zeros_like(dkt_acc)
            dvt_acc[...] = jnp.zeros_like(dvt_acc)
            for p in range(g):
                for jj in range(nq):
                    kt_ref[p, jj] = k_ref[jj * t:(jj + 1) * t, 128 * p:128 * (p + 1)].T
                    vt_ref[p, jj] = v_ref[jj * t:(jj + 1) * t, 128 * p:128 * (p + 1)].T

        q2 = [_stack_heads((q_ref[:, 128 * p:128 * (p + 1)].astype(F32) * QK_SCALE).astype(BF)) for p in range(g)]
        do2 = [_stack_heads(do_ref[:, 128 * p:128 * (p + 1)]) for p in range(g)]
        q2t = [v.T for v in q2]
        do2t = [v.T for v in do2]
        tots = tot_ref[...]
        total = jnp.concatenate([tots[:, 64 * h:64 * h + 1] for h in range(2 * g)], axis=0)
        rr, cc, valid = _sb_masks(t, 2 * g)
        upto = (rr <= cc).astype(BF)
        before = (rr < cc).astype(BF)

        def part(v, p):
            return v[2 * t * p:2 * t * (p + 1)]

        def tile(j, carry, diagonal):
            dq, run_l, run_g = carry
            z = jnp.concatenate([_dot(q2[p], kt_ref[p, j]) for p in range(g)], axis=0)
            lm = _log_one_minus_beta(z)
            if diagonal:
                lm = jnp.where(valid, lm, 0.0)
            incl = _split_dot2(lm, upto)
            a = jnp.exp(z + lm + (total - (incl + run_l)))
            if diagonal:
                a = jnp.where(valid, a, 0.0)
            gg = a * jnp.concatenate([_dot(do2[p], vt_ref[p, j]) for p in range(g)], axis=0)
            excl = _split_dot2(gg, before)
            dz = gg * jnp.exp(lm) - (excl + run_g) * jnp.exp(z + lm)
            if diagonal:
                dz = jnp.where(valid, dz, 0.0)
            dzb = dz.astype(BF)
            ab = a.astype(BF)
            keys = pl.ds(pl.multiple_of(j * t, t), t)
            for p in range(g):
                dkt_acc[p, j] += _dot(q2t[p], part(dzb, p))
                dvt_acc[p, j] += _dot(do2t[p], part(ab, p))
            dq_t = jnp.concatenate([_dot(part(dzb, p), k_ref[keys, 128 * p:128 * (p + 1)]) for p in range(g)], axis=0)
            return (dq + dq_t, run_l + incl[:, t - 1:t], run_g + excl[:, t - 1:t] + gg[:, t - 1:t])

        zero = jnp.zeros((rows, 1), F32)
        carry = lax.fori_loop(0, i, lambda j, cr: tile(j, cr, False), (jnp.zeros((rows, 128), F32), zero, zero))
        dq = tile(i, carry, True)[0]
        for p in range(g):
            dq_ref[:, 128 * p:128 * (p + 1)] = (_unstack_heads(part(dq, p)) * QK_SCALE).astype(BF)

        @pl.when(i == nq - 1)
        def _():
            for p in range(g):
                for jj in range(nq):
                    dk_ref[jj * t:(jj + 1) * t, 128 * p:128 * (p + 1)] = dkt_acc[p, jj].T.astype(BF)
                    dv_ref[jj * t:(jj + 1) * t, 128 * p:128 * (p + 1)] = dvt_acc[p, jj].T.astype(BF)

    w = 128 * g
    tile_spec = pl.BlockSpec((t, w), lambda p, i: (i, p))
    col_spec = pl.BlockSpec((SEQ, w), lambda p, i: (0, p))
    shp = jax.ShapeDtypeStruct((SEQ, 512), BF)
    body, in_specs, args = _add_dep(
        body, [pl.BlockSpec((t, w), lambda p, i: (i, C_QS // w + p)),
               pl.BlockSpec((SEQ, w), lambda p, i: (0, C_KS // w + p)),
               pl.BlockSpec((SEQ, w), lambda p, i: (0, C_VS // w + p)),
               tile_spec, tile_spec],
        [proj, proj, proj, dsb, tot], dep)
    return pl.pallas_call(
        body, name="sb_bwd", grid=(N_PAIRS // g, nq),
        in_specs=in_specs,
        out_specs=[tile_spec, col_spec, col_spec],
        out_shape=[shp, shp, shp],
        scratch_shapes=[pltpu.VMEM((g, nq, 128, t), BF), pltpu.VMEM((g, nq, 128, t), BF),
                        pltpu.VMEM((g, nq, 128, t), F32), pltpu.VMEM((g, nq, 128, t), F32)],
        compiler_params=_cp(("parallel", "arbitrary")),
    )(*args)


def _mix_out(retg, sb, proj, x, vecs, w_ret, w_sb4, w_out):
    tm, half = 256, 512

    def body(r_ref, s_ref, ar0, ar1, as0, as1, x_ref, vec_ref, wr_ref, ws_ref, wo_ref,
             mix_ref, rb_ref, sbp_ref, y_ref, h1_ref, h2_ref):
        rb = _dot(r_ref[...], wr_ref[...])
        sbv = s_ref[...]
        sbp = jnp.concatenate([_dot(sbv, ws_ref[k]) for k in range(N_CHIPS)], axis=1)
        gate_r = _sigmoid(jnp.concatenate([ar0[...], ar1[...]], axis=1).astype(F32))
        gate_s = _sigmoid(jnp.concatenate([as0[...], as1[...]], axis=1).astype(F32))
        mixed = (gate_r * rb + gate_s * sbp).astype(BF)
        mix_ref[...] = mixed
        rb_ref[...] = rb.astype(BF)
        sbp_ref[...] = sbp.astype(BF)
        y = _dot(mixed, wo_ref[...])
        h1 = x_ref[...] + _row(vec_ref, V_GT1) * (y * _rms(y)) * _row(vec_ref, V_G2)
        g = _row(vec_ref, V_G3) * (1.0 + _row(vec_ref, V_SC2))
        y_ref[...] = y
        h1_ref[...] = h1
        h2_ref[...] = (h1 * _rms(h1) * g + _row(vec_ref, V_SH2)).astype(BF)

    row = pl.BlockSpec((tm, D_MODEL), lambda i: (i, 0))
    gate = lambda c0: pl.BlockSpec((tm, half), lambda i: (i, c0 // half))
    bf = jax.ShapeDtypeStruct((SEQ, D_MODEL), BF)
    f32 = jax.ShapeDtypeStruct((SEQ, D_MODEL), F32)
    return pl.pallas_call(
        body, name="mix_out", grid=(SEQ // tm,),
        in_specs=[row, pl.BlockSpec((tm, 512), lambda i: (i, 0)), gate(C_AR), gate(C_AR + half), gate(C_AS),
                  gate(C_AS + half), row, _resident((16, D_MODEL)), _resident((D_MODEL, D_MODEL)),
                  _resident(w_sb4.shape), _resident((D_MODEL, D_MODEL))],
        out_specs=[row] * 6, out_shape=[bf, bf, bf, f32, f32, bf],
        compiler_params=_cp(("parallel",)),
    )(retg, sb, proj, proj, proj, proj, x, vecs, w_ret, w_sb4, w_out)


def _ffn_fwd_loss(h2, h1, target, vecs, w_ff14, w_ff24):
    tm = 256

    def body(h2_ref, h1_ref, t_ref, vec_ref, w1_hbm, w2_hbm, u_ref, a_ref, dout_ref, df_ref, st_ref,
             w1_buf, w2_buf, sems):
        fetch = _stream_in(
            pl.program_id(0),
            [w.at[k] for k in range(N_CHIPS) for w in (w1_hbm, w2_hbm)],
            [w.at[k] for k in range(N_CHIPS) for w in (w1_buf, w2_buf)], sems)

        @pl.when(pl.program_id(0) == 0)
        def _():
            st_ref[...] = jnp.zeros_like(st_ref)

        hb = h2_ref[...]
        f = jnp.zeros((tm, D_MODEL), F32)
        for k in range(N_CHIPS):
            cols = slice(k * D_MODEL, (k + 1) * D_MODEL)
            u = _dot(hb, fetch(2 * k)[...])
            r = jnp.maximum(u, 0.0)
            act = (r * r).astype(BF)
            u_ref[:, cols] = u.astype(BF)
            a_ref[:, cols] = act
            f = f + _dot(act, fetch(2 * k + 1)[...])
        r4 = _rms(f)
        fn = f * r4
        gt2, g4 = _row(vec_ref, V_GT2), _row(vec_ref, V_G4)
        diff = h1_ref[...] + gt2 * fn * g4 - t_ref[...]
        dout = diff * (1.0 / D_MODEL)
        dfn = dout * gt2 * g4
        dout_ref[...] = dout
        df_ref[...] = (r4 * (dfn - fn * _rowmean(dfn * fn))).astype(BF)
        st_ref[0:1, :] += _colsum(dout * fn * g4)
        st_ref[1:2, :] += _colsum(dout * gt2 * fn)
        st_ref[2:3, :] += _colsum(diff * diff) * (0.5 / D_MODEL)

    row = pl.BlockSpec((tm, D_MODEL), lambda i: (i, 0))
    wide = pl.BlockSpec((tm, D_FF), lambda i: (i, 0))
    return pl.pallas_call(
        body, name="ffn_fwd_loss", grid=(SEQ // tm,),
        in_specs=[row, row, row, _resident((16, D_MODEL)), ANY, ANY],
        out_specs=[wide, wide, row, row, pl.BlockSpec((8, D_MODEL), lambda i: (0, 0))],
        out_shape=[jax.ShapeDtypeStruct((SEQ, D_FF), BF), jax.ShapeDtypeStruct((SEQ, D_FF), BF),
                   jax.ShapeDtypeStruct((SEQ, D_MODEL), F32), jax.ShapeDtypeStruct((SEQ, D_MODEL), BF),
                   jax.ShapeDtypeStruct((8, D_MODEL), F32)],
        scratch_shapes=[pltpu.VMEM(w_ff14.shape, BF), pltpu.VMEM(w_ff24.shape, BF),
                        pltpu.SemaphoreType.DMA((2 * N_CHIPS,))],
        compiler_params=_cp(("arbitrary",)),
    )(h2, h1, target, vecs, w_ff14, w_ff24)


def _ffn_bwd(df, u, h1, y, dout, vecs, w_ff14, w_ff24):
    tm = 256

    def body(df_ref, u_ref, h1_ref, y_ref, dout_ref, vec_ref, w1_hbm, w2_hbm, du_ref, dh1_ref, dy_ref, st_ref,
             w1_buf, w2_buf, sems):
        fetch = _stream_in(
            pl.program_id(0),
            [w.at[k] for k in range(N_CHIPS) for w in (w2_hbm, w1_hbm)],
            [w.at[k] for k in range(N_CHIPS) for w in (w2_buf, w1_buf)], sems)

        @pl.when(pl.program_id(0) == 0)
        def _():
            st_ref[...] = jnp.zeros_like(st_ref)

        dfb = df_ref[...]
        dh2 = jnp.zeros((tm, D_MODEL), F32)
        for k in range(N_CHIPS):
            cols = slice(k * D_MODEL, (k + 1) * D_MODEL)
            da = _dot_nt(dfb, fetch(2 * k)[...])
            du = (da * (2.0 * jnp.maximum(u_ref[:, cols].astype(F32), 0.0))).astype(BF)
            du_ref[:, cols] = du
            dh2 = dh2 + _dot_nt(du, fetch(2 * k + 1)[...])
        h1 = h1_ref[...]
        r3 = _rms(h1)
        hn3 = h1 * r3
        g3, sc2 = _row(vec_ref, V_G3), _row(vec_ref, V_SC2)
        dhn3 = dh2 * g3 * (1.0 + sc2)
        dh1 = dout_ref[...] + r3 * (dhn3 - hn3 * _rowmean(dhn3 * hn3))
        y = y_ref[...]
        r2 = _rms(y)
        yn = y * r2
        gt1, g2 = _row(vec_ref, V_GT1), _row(vec_ref, V_G2)
        dyn = dh1 * gt1 * g2
        dh1_ref[...] = dh1
        dy_ref[...] = (r2 * (dyn - yn * _rowmean(dyn * yn))).astype(BF)
        st_ref[0:1, :] += _colsum(dh2)
        st_ref[1:2, :] += _colsum(dh2 * hn3 * g3)
        st_ref[2:3, :] += _colsum(dh2 * hn3 * (1.0 + sc2))
        st_ref[3:4, :] += _colsum(dh1 * yn * g2)
        st_ref[4:5, :] += _colsum(dh1 * gt1 * yn)

    row = pl.BlockSpec((tm, D_MODEL), lambda i: (i, 0))
    wide = pl.BlockSpec((tm, D_FF), lambda i: (i, 0))
    return pl.pallas_call(
        body, name="ffn_bwd", grid=(SEQ // tm,),
        in_specs=[row, wide, row, row, row, _resident((16, D_MODEL)), ANY, ANY],
        out_specs=[wide, row, row, pl.BlockSpec((8, D_MODEL), lambda i: (0, 0))],
        out_shape=[jax.ShapeDtypeStruct((SEQ, D_FF), BF), jax.ShapeDtypeStruct((SEQ, D_MODEL), F32),
                   jax.ShapeDtypeStruct((SEQ, D_MODEL), BF), jax.ShapeDtypeStruct((8, D_MODEL), F32)],
        scratch_shapes=[pltpu.VMEM(w_ff14.shape, BF), pltpu.VMEM(w_ff24.shape, BF),
                        pltpu.SemaphoreType.DMA((2 * N_CHIPS,))],
        compiler_params=_cp(("arbitrary",)),
    )(df, u, h1, y, dout, vecs, w_ff14, w_ff24)


def _mix_ret_bwd(dy, proj, rb, sbp, o_raw, gn_g, w_out, w_sb4, w_ret):
    tm, half = 256, 512

    def body(dy_ref, ar0, ar1, as0, as1, rb_ref, sbp_ref, g_ref, o_ref, gn_ref, wo_ref, ws_ref, wr_ref,
             drb_ref, dsbp_ref, da_ref, dsb_ref, dret_ref, dgr_ref, st_ref):
        @pl.when(pl.program_id(0) == 0)
        def _():
            st_ref[...] = jnp.zeros_like(st_ref)

        dm_all = _dot_nt(dy_ref[...], wo_ref[...])
        dsb = jnp.zeros((tm, 512), F32)
        drbs = []
        for hf, (ar_ref, as_ref) in enumerate(((ar0, as0), (ar1, as1))):
            cols = slice(half * hf, half * (hf + 1))
            dm = dm_all[:, cols]
            sr = _sigmoid(ar_ref[...].astype(F32))
            ss = _sigmoid(as_ref[...].astype(F32))
            dsbp = (dm * ss).astype(BF)
            drbs.append((dm * sr).astype(BF))
            dsbp_ref[:, cols] = dsbp
            da_ref[:, cols] = (dm * rb_ref[:, cols].astype(F32) * sr * (1.0 - sr)).astype(BF)
            da_ref[:, D_MODEL + half * hf:D_MODEL + half * (hf + 1)] = (
                dm * sbp_ref[:, cols].astype(F32) * ss * (1.0 - ss)).astype(BF)
            dsb = dsb + _dot_nt(dsbp[:, :256], ws_ref[2 * hf]) + _dot_nt(dsbp[:, 256:], ws_ref[2 * hf + 1])
        dsb_ref[...] = dsb.astype(BF)
        drb = jnp.concatenate(drbs, axis=1)
        drb_ref[...] = drb
        dretg = _dot_nt(drb, wr_ref[...])
        for gi in range(D_MODEL // 128):
            cols = slice(128 * gi, 128 * (gi + 1))
            o = o_ref[:, cols]
            d = o - _rowmean(o)
            rstd = lax.rsqrt(_rowmean(d * d) + EPS)
            nh = d * rstd
            gain = gn_ref[:, cols]
            gr = g_ref[:, cols].astype(F32)
            sg = _sigmoid(gr)
            dg = dretg[:, cols]
            dgn = dg * gr * sg
            dnh = dgn * gain
            dgr_ref[:, cols] = (dg * nh * gain * sg * (1.0 + gr * (1.0 - sg))).astype(BF)
            dret_ref[:, cols] = (rstd * (dnh - _rowmean(dnh) - nh * _rowmean(dnh * nh))).astype(BF)
            st_ref[0:1, cols] += _colsum(dgn * nh)

    row = pl.BlockSpec((tm, D_MODEL), lambda i: (i, 0))
    gate = lambda c0: pl.BlockSpec((tm, half), lambda i: (i, c0 // half))
    shp = jax.ShapeDtypeStruct((SEQ, D_MODEL), BF)
    return pl.pallas_call(
        body, name="mix_ret_bwd", grid=(SEQ // tm,),
        in_specs=[row, gate(C_AR), gate(C_AR + half), gate(C_AS), gate(C_AS + half), row, row,
                  pl.BlockSpec((tm, D_MODEL), lambda i: (i, C_GR // D_MODEL)), row, _resident((1, D_MODEL)),
                  _resident((D_MODEL, D_MODEL)), _resident(w_sb4.shape), _resident((D_MODEL, D_MODEL))],
        out_specs=[row, row, pl.BlockSpec((tm, 2 * D_MODEL), lambda i: (i, 0)), pl.BlockSpec((tm, 512), lambda i: (i, 0)),
                   row, row, pl.BlockSpec((8, D_MODEL), lambda i: (0, 0))],
        out_shape=[shp, shp, jax.ShapeDtypeStruct((SEQ, 2 * D_MODEL), BF), jax.ShapeDtypeStruct((SEQ, 512), BF),
                   shp, shp, jax.ShapeDtypeStruct((8, D_MODEL), F32)],
        compiler_params=_cp(("arbitrary",)),
    )(dy, proj, proj, proj, proj, rb, sbp, proj, o_raw, gn_g, w_out, w_sb4, w_ret)


def _dproj_segments(widths):
    wc = D_IN // N_CHIPS
    segs, start = [], 0
    for pi, width in enumerate(widths):
        lo = start
        while lo < start + width:
            j = lo // wc
            hi = min(start + width, (j + 1) * wc)
            segs.append((j, lo - j * wc, pi, lo - start, hi - lo))
            lo = hi
        start += width
    assert start == D_IN
    return segs


def _in_proj_bwd(pieces, x, dh1, vecs, w_in4, updates):
    tm = 256
    steps = SEQ // tm
    n, nu = len(pieces), len(updates)
    segs = _dproj_segments([p.shape[1] for p in pieces])

    def body(*refs):
        x_ref, dh1_ref, vec_ref, w_hbm = refs[n:n + 4]
        upd_in = refs[n + 4:n + 4 + 4 * nu]
        dx_ref, st_ref = refs[n + 4 + 4 * nu:n + 6 + 4 * nu]
        upd_out = refs[n + 6 + 4 * nu:n + 6 + 8 * nu]
        w_buf, sems = refs[n + 6 + 8 * nu:]
        fetch = _stream_in(pl.program_id(0), [w_hbm.at[j] for j in range(N_CHIPS)],
                           [w_buf.at[j] for j in range(N_CHIPS)], sems)
        for u in range(nu):
            w_u, m_u, v_u, g_u = upd_in[4 * u:4 * u + 4]
            go_u, d_u, mo_u, vo_u = upd_out[4 * u:4 * u + 4]
            gg = g_u[...]
            go_u[...] = gg
            d_u[...], mo_u[...], vo_u[...] = _adamw_math(w_u[...], gg, m_u[...], v_u[...])

        @pl.when(pl.program_id(0) == 0)
        def _():
            st_ref[...] = jnp.zeros_like(st_ref)

        dh = jnp.zeros((tm, D_MODEL), F32)
        landed = {}
        for j, so, pi, po, width in segs:
            if j not in landed:
                landed[j] = fetch(j)
            dh = dh + _dot_nt(refs[pi][:, po:po + width], landed[j][:, so:so + width])
        xx = x_ref[...]
        r1 = _rms(xx)
        xn = xx * r1
        g1, sc1 = _row(vec_ref, V_G1), _row(vec_ref, V_SC1)
        dxn = dh * g1 * (1.0 + sc1)
        dx_ref[...] = dh1_ref[...] + r1 * (dxn - xn * _rowmean(dxn * xn))
        st_ref[0:1, :] += _colsum(dh)
        st_ref[1:2, :] += _colsum(dh * xn * g1)
        st_ref[2:3, :] += _colsum(dh * xn * (1.0 + sc1))

    row = pl.BlockSpec((tm, D_MODEL), lambda i: (i, 0))
    upd_specs, upd_shapes, upd_args = [], [], []
    for arrays in updates:
        rows, cols = arrays[0].shape
        upd_specs += [pl.BlockSpec((rows // steps, cols), lambda i: (i, 0))] * 4
        upd_shapes += [jax.ShapeDtypeStruct((rows, cols), F32)] * 4
        upd_args += list(arrays)
    outs = pl.pallas_call(
        body, name="in_proj_bwd", grid=(steps,),
        in_specs=[pl.BlockSpec((tm, p.shape[1]), lambda i: (i, 0)) for p in pieces] + [
            row, row, _resident((16, D_MODEL)), ANY] + upd_specs,
        out_specs=[row, pl.BlockSpec((8, D_MODEL), lambda i: (0, 0))] + upd_specs,
        out_shape=[jax.ShapeDtypeStruct((SEQ, D_MODEL), F32), jax.ShapeDtypeStruct((8, D_MODEL), F32)] + upd_shapes,
        scratch_shapes=[pltpu.VMEM(w_in4.shape, BF), pltpu.SemaphoreType.DMA((N_CHIPS,))],
        compiler_params=_cp(("arbitrary",)),
    )(*pieces, x, dh1, vecs, w_in4, *upd_args)
    return outs[0], outs[1], [tuple(outs[2 + 4 * u:6 + 4 * u]) for u in range(nu)]


def _grad_w_in(h, pieces, dep=None):
    ta = 256
    n = len(pieces)
    segs = _dproj_segments([p.shape[1] for p in pieces])

    def body(*refs):
        h_ref, o_ref = refs[n], refs[n + 1]
        fetch = _stream_in(pl.program_id(0), refs[:n], refs[n + 2:2 * n + 2], refs[2 * n + 2])
        hh = h_ref[...]
        landed = {}
        for j, so, pi, po, width in segs:
            if pi not in landed:
                landed[pi] = fetch(pi)
            o_ref[j, :, so:so + width] = _dot_tn(hh, landed[pi][:, po:po + width]).astype(BF)

    body, in_specs, args = _add_dep(
        body, [ANY] * n + [pl.BlockSpec((SEQ, ta), lambda i: (0, i))], list(pieces) + [h], dep)
    return pl.pallas_call(
        body, name="grad_w_in", grid=(D_MODEL // ta,),
        in_specs=in_specs,
        out_specs=pl.BlockSpec((N_CHIPS, ta, D_IN // N_CHIPS), lambda i: (0, i, 0)),
        out_shape=jax.ShapeDtypeStruct((N_CHIPS, D_MODEL, D_IN // N_CHIPS), BF),
        scratch_shapes=[pltpu.VMEM(p.shape, p.dtype) for p in pieces] + [pltpu.SemaphoreType.DMA((n,))],
        compiler_params=_cp(("arbitrary",)),
    )(*args)


def _weight_grad(a, b, ta, tb, col_sharded, name, dep=None):
    ka, nb_ = a.shape[1], b.shape[1]

    def body(a_ref, b_ref, o_ref):
        o_ref[...] = _dot_tn(a_ref[...], b_ref[...]).astype(BF)

    body, in_specs, args = _add_dep(
        body, [pl.BlockSpec((SEQ, ta), lambda i, j: (0, i)), pl.BlockSpec((SEQ, tb), lambda i, j: (0, j))],
        [a, b], dep)

    if col_sharded:
        per = nb_ // N_CHIPS // tb
        out_shape = jax.ShapeDtypeStruct((N_CHIPS, ka, nb_ // N_CHIPS), BF)
        out_spec = pl.BlockSpec((None, ta, tb), lambda i, j: (j // per, i, j % per))
    else:
        per = ka // N_CHIPS // ta
        out_shape = jax.ShapeDtypeStruct((N_CHIPS, ka // N_CHIPS, nb_), BF)
        out_spec = pl.BlockSpec((None, ta, tb), lambda i, j: (i // per, i % per, j))
    return pl.pallas_call(
        body, name=name, grid=(ka // ta, nb_ // tb),
        in_specs=in_specs, out_specs=out_spec, out_shape=out_shape,
        compiler_params=_cp(("parallel", "parallel")),
    )(*args)


def _rope_constants():
    freq = np.float32(ROPE_BASE) ** (-np.arange(0, 64, 2, dtype=np.float32) / np.float32(64))
    inv = np.tile(freq.astype(np.float32), 4).reshape(1, 128)
    sign = np.tile(np.concatenate([-np.ones(32, np.float32), np.ones(32, np.float32)]), 2).reshape(1, 128)
    return jnp.asarray(inv), jnp.asarray(sign)


def _log_gamma():
    return jnp.asarray(np.log1p(-(2.0 ** (-5.0 - np.arange(8, dtype=np.float64)))).astype(np.float32))


def _halves(g):
    return g.reshape(N_CHIPS, 2, g.shape[1] // 2, g.shape[2])


def kernel(x, c, positions, ada_w, ada_b, pre_mix_g, post_mix_g, pre_ffn_g, post_ffn_g, w_in, ret_gn_g, w_ret_branch, w_sb_branch, w_out, w_ff1, w_ff2, loss_target, m_ada_w, m_ada_b, m_pre_mix_g, m_post_mix_g, m_pre_ffn_g, m_post_ffn_g, m_w_in, m_ret_gn_g, m_w_ret_branch, m_w_sb_branch, m_w_out, m_w_ff1, m_w_ff2, v_ada_w, v_ada_b, v_pre_mix_g, v_post_mix_g, v_pre_ffn_g, v_post_ffn_g, v_w_in, v_ret_gn_g, v_w_ret_branch, v_w_sb_branch, v_w_out, v_w_ff1, v_w_ff2):
    names = ["w_in", "w_ret", "w_sb", "w_out", "w_ff1", "w_ff2"]
    big = dict(zip(names, [w_in, w_ret_branch, w_sb_branch, w_out, w_ff1, w_ff2]))
    big_m = dict(zip(names, [m_w_in, m_w_ret_branch, m_w_sb_branch, m_w_out, m_w_ff1, m_w_ff2]))
    big_v = dict(zip(names, [v_w_in, v_w_ret_branch, v_w_sb_branch, v_w_out, v_w_ff1, v_w_ff2]))
    rest = names[1:]
    cidx = lax.axis_index("c").astype(jnp.int32).reshape(1)
    kidx = (2 * lax.axis_index("x") + lax.axis_index("y")).astype(jnp.int32).reshape(1)
    x0, target = x[0], loss_target[0]

    buf_in, sem_in, tok_in = _gather_start("gather_in_start", [_cast_bf16(w_in[0], kidx, c, "cast_w_in")])
    rest_bufs = [_cast_bf16(big[nm][0], kidx, tok_in, "cast_" + nm) for nm in rest]
    inv_freq, sign = _rope_constants()
    lg = _log_gamma()
    cos, sin_s = _rope_tables(positions.reshape(SEQ, 1), _tie(tok_in, inv_freq), sign)

    def table(b6, g5):
        return jnp.concatenate([b6.reshape(6, D_MODEL)] + g5 + [jnp.zeros((5, D_MODEL), F32)], axis=0)

    wsm = table(ada_b, [pre_mix_g, post_mix_g, pre_ffn_g, post_ffn_g, ret_gn_g])
    msm = table(m_ada_b, [m_pre_mix_g, m_post_mix_g, m_pre_ffn_g, m_post_ffn_g, m_ret_gn_g])
    vsm = table(v_ada_b, [v_pre_mix_g, v_post_mix_g, v_pre_ffn_g, v_post_ffn_g, v_ret_gn_g])
    c_all, mod4 = _mod_exchange(c, ada_w[0], ada_b.reshape(N_CHIPS, -1), rest_bufs + [cos, wsm, msm, vsm])
    vecs = jnp.concatenate([mod4.reshape(6, D_MODEL), pre_mix_g, post_mix_g, pre_ffn_g, post_ffn_g,
                            jnp.zeros((6, D_MODEL), F32)], axis=0)
    buf_in, sem_in, tok_in = _gather_pass("gather_in_pass", buf_in, sem_in, vecs)
    buf_rest, sem_rest, tok_rest = _gather_start("gather_rest_start", rest_bufs, after=tok_in)
    (w_in4,) = _gather_finish("gather_in_finish", buf_in, sem_in, tok_rest)

    h, proj = _ln_proj(x0, vecs, w_in4)
    sb, tot = _sb_fwd(proj)
    buf_rest, sem_rest, tok_rest = _gather_pass("gather_rest_pass", buf_rest, sem_rest, sb)
    o_raw, retg, states = _ret_fwd(proj, cos, sin_s, _tie(tok_rest, ret_gn_g), lg)
    w_ret4, w_sb4, w_out4, w_ff14, w_ff24 = _gather_finish("gather_rest_finish", buf_rest, sem_rest, retg)
    w_ret = w_ret4.reshape(D_MODEL, D_MODEL)
    w_out2 = w_out4.reshape(D_MODEL, D_MODEL)
    mixed, rb, sbp, y, h1, h2 = _mix_out(retg, sb, proj, x0, vecs, w_ret, w_sb4, w_out2)
    u, act, dout, df, st_a = _ffn_fwd_loss(h2, h1, target, vecs, w_ff14, w_ff24)

    du, dh1, dy, st_b = _ffn_bwd(df, u, h1, y, dout, vecs, w_ff14, w_ff24)
    grads = {"w_ff2": _weight_grad(act, df, 512, 1024, False, "grad_w_ff2")}
    grads["w_ff1"] = _weight_grad(h2, du, 512, 1024, True, "grad_w_ff1")
    drb, dsbp, da, dsb, dret, dg_r, st_c = _mix_ret_bwd(dy, proj, rb, sbp, o_raw, ret_gn_g, w_out2, w_sb4, w_ret)
    grads["w_out"] = _weight_grad(mixed, dy, 256, 1024, False, "grad_w_out")
    grads["w_ret"] = _weight_grad(retg, drb, 256, 1024, False, "grad_w_ret")
    grads["w_sb"] = _weight_grad(sb, dsbp, 512, 256, True, "grad_w_sb")

    bufs, sems, tok = _pair_send_start("rs_rest_pair_send", [_halves(grads[nm]) for nm in rest])
    dqkv_r = _ret_bwd(proj, cos, sin_s, dret, states, _tie(tok, lg))
    mine, theirs = _pair_send_wait("rs_rest_pair_recv", bufs, sems, dqkv_r)
    pair_sums = [_pair_add(g, r, cidx, "pair_add_" + nm) for g, r, nm in zip(mine, theirs, rest)]
    bufs, sems, tok = _chip_send_start("rs_rest_chip_send", pair_sums)
    dq_s, dk_s, dv_s = _sb_bwd(proj, dsb, tot, dep=tok)
    own, parts = _chip_send_wait("rs_rest_chip_recv", bufs, sems, dq_s)
    sums = [_chip_add(o, p, kidx, cidx, "chip_add_" + nm) for o, p, nm in zip(own, parts, rest)]
    bufs, sems, tok = _pair_swap_start("rs_rest_pair_swap", sums)
    dproj = [dqkv_r, dg_r, dq_s, dk_s, dv_s, da]
    g_in = _grad_w_in(h, dproj, dep=tok)
    full_rest = _pair_swap_wait("rs_rest_pair_swapped", bufs, sems, g_in)

    full_rest = dict(zip(rest, full_rest))
    bufs, sems, tok = _pair_send_start("rs_in_pair_send", [_halves(g_in)])
    out = {}
    for nm in ("w_out", "w_sb", "w_ret"):
        w = big[nm][0]
        out[nm] = _adamw(w, big_m[nm][0], big_v[nm][0], full_rest[nm].reshape(w.shape), "adamw_" + nm, dep=tok)
    mine, theirs = _pair_send_wait("rs_in_pair_recv", bufs, sems, out["w_ret"][1])
    bufs, sems, tok = _chip_send_start("rs_in_chip_send", [_pair_add(mine[0], theirs[0], cidx, "pair_add_w_in")])
    riding = ("w_ff2", "w_ff1")
    dx, st_d, updated = _in_proj_bwd(
        dproj, x0, dh1, _tie(tok, vecs), w_in4,
        [(big[nm][0], big_m[nm][0], big_v[nm][0], full_rest[nm].reshape(big[nm].shape[1:])) for nm in riding])
    out.update(zip(riding, updated))

    payload = jnp.concatenate([
        st_d[0:2], st_b[3:4], st_b[0:2], st_a[0:1],
        st_d[2:3], st_b[4:5], st_b[2:3], st_a[1:2],
        st_c[0:1], st_a[2:3]], axis=0)

    g_ada, gsm, dsm, mosm, vosm, loss = _small_exchange(
        payload.reshape(N_PAY, 1, D_MODEL), c_all, wsm, msm, vsm)
    ada_out = _adamw(ada_w[0], m_ada_w[0], v_ada_w[0], g_ada, "adamw_ada_w")

    own, parts = _chip_send_wait("rs_in_chip_recv", bufs, sems, ada_out[1])
    bufs, sems, tok = _pair_swap_start(
        "rs_in_pair_swap", [_chip_add(own[0], parts[0], kidx, cidx, "chip_add_w_in")])
    (full_in,) = _pair_swap_wait("rs_in_pair_swapped", bufs, sems, tok)
    out["w_in"] = _adamw(w_in[0], m_w_in[0], v_w_in[0], full_in.reshape(w_in.shape[1:]), "adamw_w_in")

    def unpack(tab):
        return [tab[0:6].reshape(1, 6 * D_MODEL)] + [tab[6 + r:7 + r] for r in range(5)]

    def ordered(which):
        sm = unpack([gsm, dsm, mosm, vosm][which])
        bg = [out[nm][which][None] for nm in names]
        return [ada_out[which][None], sm[0], sm[1], sm[2], sm[3], sm[4], bg[0], sm[5]] + bg[1:]

    return (loss.reshape(()), dx[None], *ordered(0), *ordered(1), *ordered(2), *ordered(3))
```

```python
import functools

import numpy as np
import jax
import jax.numpy as jnp
from jax import lax
from jax.experimental import pallas as pl
from jax.experimental.pallas import tpu as pltpu

SEQ = 2048
D_MODEL = 1024
D_IN = 6656
D_FF = 4096
N_CHIPS = 4
EPS = 1e-6
ROPE_BASE = 10000.0
RET_BLOCK = 256
RET_CHUNK_SHIFT = 6
SB_BLOCK = 256
QK_SCALE = 0.125
N_PAIRS = 4
SB_GROUP = 2

ADAM_LR = 0.001
ADAM_B1 = 0.9
ADAM_B2 = 0.999
ADAM_EPS = 1e-08
ADAM_WD = 0.01
ADAM_STEP = 10

BF = jnp.bfloat16
F32 = jnp.float32
MESH = pl.DeviceIdType.MESH
VMEM_LIMIT = 56 * 1024 * 1024
ANY = pl.BlockSpec(memory_space=pl.ANY)

C_QR, C_KR, C_VR, C_GR, C_QS, C_KS, C_VS, C_AR, C_AS = 0, 512, 1024, 2048, 3072, 3584, 4096, 4608, 5632

V_SH1, V_SC1, V_GT1, V_SH2, V_SC2, V_GT2, V_G1, V_G2, V_G3, V_G4 = range(10)
P_DSH1, P_DSC1, P_DGT1, P_DSH2, P_DSC2, P_DGT2, P_DG1, P_DG2, P_DG3, P_DG4, P_DGN, P_LOSS = range(12)
N_PAY = 12


def _cp(sem=None, **kw):
    if sem is not None:
        kw["dimension_semantics"] = sem
    return pltpu.CompilerParams(vmem_limit_bytes=VMEM_LIMIT, **kw)


def _dot(a, b):
    return jnp.dot(a, b, preferred_element_type=F32)


def _dot_nt(a, b):
    return lax.dot_general(a, b, (((1,), (1,)), ((), ())), preferred_element_type=F32)


def _dot_tn(a, b):
    return lax.dot_general(a, b, (((0,), (0,)), ((), ())), preferred_element_type=F32)


def _row(ref, i):
    return ref[i:i + 1, :]


def _rms(v):
    return lax.rsqrt(jnp.mean(v * v, axis=1, keepdims=True) + EPS)


def _colsum(v):
    return jnp.sum(v, axis=0, keepdims=True)


def _rowmean(v):
    return jnp.mean(v, axis=1, keepdims=True)


def _sigmoid(v):
    return 1.0 / (1.0 + jnp.exp(-v))


def _cast_bf16(w, kidx, dep, name):
    rows, cols = w.shape
    tr = min(rows, 512)

    def body(k_ref, w_ref, dep_ref, o_ref):
        o_ref[...] = w_ref[...].astype(BF)

    return pl.pallas_call(
        body, name=name,
        grid_spec=pltpu.PrefetchScalarGridSpec(
            num_scalar_prefetch=1, grid=(rows // tr,),
            in_specs=[pl.BlockSpec((tr, cols), lambda i, k_ref: (i, 0)), ANY],
            out_specs=pl.BlockSpec((None, tr, cols), lambda i, k_ref: (k_ref[0], i, 0))),
        out_shape=jax.ShapeDtypeStruct((N_CHIPS, rows, cols), BF),
        compiler_params=_cp(("parallel",)),
    )(kidx, w, dep)


def _adamw_math(w, g, m, v):
    m = ADAM_B1 * m + (1.0 - ADAM_B1) * g
    v = ADAM_B2 * v + (1.0 - ADAM_B2) * (g * g)
    m_hat = m / (1.0 - ADAM_B1 ** ADAM_STEP)
    v_hat = v / (1.0 - ADAM_B2 ** ADAM_STEP)
    delta = -ADAM_LR * (m_hat / (jnp.sqrt(v_hat) + ADAM_EPS) + ADAM_WD * w)
    return delta, m, v


def _adamw(w, m, v, g, name, dep=None):
    rows, cols = w.shape
    tr = min(rows, 256)

    def body(w_ref, m_ref, v_ref, g_ref, go_ref, d_ref, mo_ref, vo_ref):
        gg = g_ref[...]
        d, mm, vv = _adamw_math(w_ref[...], gg, m_ref[...], v_ref[...])
        go_ref[...] = gg
        d_ref[...] = d
        mo_ref[...] = mm
        vo_ref[...] = vv

    spec = pl.BlockSpec((tr, cols), lambda i: (i, 0))
    shp = jax.ShapeDtypeStruct((rows, cols), F32)
    body, in_specs, args = _add_dep(body, [spec] * 4, [w, m, v, g], dep)
    return pl.pallas_call(
        body, name=name, grid=(rows // tr,),
        in_specs=in_specs, out_specs=[spec] * 4, out_shape=[shp] * 4,
        compiler_params=_cp(("parallel",)),
    )(*args)


def _place():
    x, y, c = lax.axis_index("x"), lax.axis_index("y"), lax.axis_index("c")
    return x, y, c


HBM = pl.BlockSpec(memory_space=pltpu.HBM)
SEM = pl.BlockSpec(memory_space=pltpu.SEMAPHORE)
EFFECT = pltpu.SideEffectType.DATAFLOW_SIDE_EFFECTING


def _add_dep(body, in_specs, args, dep):
    if dep is None:
        return body, list(in_specs), list(args)
    n = len(args)

    def wrapped(*refs):
        body(*refs[:n], *refs[n + 1:])

    return wrapped, list(in_specs) + [ANY], list(args) + [dep]


def _split_call(name, bufs, run, old=None, after=None, new=0):
    nb = len(bufs)
    n_old = 2 if old is not None else 0
    n_in = nb + n_old + (1 if after is not None else 0)

    def body(*refs):
        old_sems = (refs[nb], refs[nb + 1]) if old is not None else None
        new_sems = (refs[n_in], refs[n_in + 1]) if new else None
        run(refs[:nb], old_sems, new_sems)
        if new:
            refs[-1][...] = jnp.zeros_like(refs[-1])

    in_specs = [HBM] * nb + [SEM] * n_old + ([ANY] if after is not None else [])
    out_shape = [pltpu.SemaphoreType.DMA((new,))] * 2 if new else []
    out_specs = [SEM, SEM] if new else []
    out_shape += [pltpu.HBM(b.shape, b.dtype) for b in bufs]
    out_specs += [HBM] * nb
    if new:
        out_shape.append(jax.ShapeDtypeStruct((8, 128), F32))
        out_specs.append(pl.BlockSpec(memory_space=pltpu.VMEM))
    first = 2 if new else 0
    args = [pltpu.with_memory_space_constraint(b, pltpu.HBM) for b in bufs]
    if old is not None:
        args += [old[0], old[1]]
    if after is not None:
        args.append(after)
    outs = pl.pallas_call(
        body, name=name, in_specs=tuple(in_specs), out_specs=tuple(out_specs), out_shape=tuple(out_shape),
        input_output_aliases={i: i + first for i in range(nb)},
        compiler_params=pltpu.CompilerParams(has_side_effects=EFFECT),
    )(*args)
    thru = list(outs[first:first + nb])
    if new:
        return thru, (outs[0], outs[1]), outs[-1]
    return thru, None, None


def _remote(part_src, part_dst, sems, i, to):
    return pltpu.make_async_remote_copy(src_ref=part_src, dst_ref=part_dst, send_sem=sems[0].at[i],
                                        recv_sem=sems[1].at[i], device_id=to, device_id_type=MESH)


def _other_chips(x, y):
    return [(1 - x, y), (x, 1 - y), (1 - x, 1 - y)]


def _gather_start(name, bufs, after=None):
    def run(refs, old, new):
        x, y, c = _place()
        k = 2 * x + y
        for w, ref in enumerate(refs):
            rh = bufs[w].shape[1] // 2
            part = ref.at[k, pl.ds(c * rh, rh)]
            for j, (cx, cy) in enumerate(_other_chips(x, y)):
                _remote(part, part, new, 3 * w + j, (cx, cy, c)).start()

    return _split_call(name, bufs, run, after=after, new=3 * len(bufs))


def _gather_pass(name, bufs, sems, after):
    def run(refs, old, new):
        x, y, c = _place()
        k = 2 * x + y
        sib = (x, y, 1 - c)
        for w, ref in enumerate(refs):
            rh = bufs[w].shape[1] // 2
            for j, (cx, cy) in enumerate(_other_chips(x, y)):
                land = ref.at[2 * cx + cy, pl.ds(c * rh, rh)]
                _remote(land, land, old, 3 * w + j, (cx, cy, c)).wait_recv()
                _remote(land, land, new, 3 * w + j, sib).start()
        for w, ref in enumerate(refs):
            rh = bufs[w].shape[1] // 2
            part = ref.at[k, pl.ds(c * rh, rh)]
            for j, (cx, cy) in enumerate(_other_chips(x, y)):
                _remote(part, part, old, 3 * w + j, (cx, cy, c)).wait_send()

    return _split_call(name, bufs, run, old=sems, after=after, new=3 * len(bufs))


def _gather_finish(name, bufs, sems, after):
    def run(refs, old, new):
        x, y, c = _place()
        sib = (x, y, 1 - c)
        for w, ref in enumerate(refs):
            rh = bufs[w].shape[1] // 2
            for j, (cx, cy) in enumerate(_other_chips(x, y)):
                sent = ref.at[2 * cx + cy, pl.ds(c * rh, rh)]
                _remote(sent, sent, old, 3 * w + j, sib).wait_send()
                land = ref.at[2 * cx + cy, pl.ds((1 - c) * rh, rh)]
                _remote(land, land, old, 3 * w + j, sib).wait_recv()

    return _split_call(name, bufs, run, old=sems, after=after)[0]


def _pair_send_start(name, grads):
    n = len(grads)
    lands = [lax.empty((N_CHIPS,) + g.shape[2:], g.dtype) for g in grads]

    def run(refs, old, new):
        x, y, c = _place()
        for w in range(n):
            _remote(refs[w].at[:, 1 - c], refs[n + w], new, w, (x, y, 1 - c)).start()

    return _split_call(name, list(grads) + lands, run, new=n)


def _pair_send_wait(name, bufs, sems, after):
    n = len(bufs) // 2

    def run(refs, old, new):
        x, y, c = _place()
        for w in range(n):
            cp = _remote(refs[w].at[:, 1 - c], refs[n + w], old, w, (x, y, 1 - c))
            cp.wait_send()
            cp.wait_recv()

    thru = _split_call(name, bufs, run, old=sems, after=after)[0]
    return thru[:n], thru[n:]


def _pair_add(g, recv, cidx, name):
    _, _, rh, cols = g.shape
    tr = min(rh, 256)

    def body(c_ref, g_ref, r_ref, o_ref):
        o_ref[...] = (g_ref[...].astype(F32) + r_ref[...].astype(F32)).astype(BF)

    return pl.pallas_call(
        body, name=name,
        grid_spec=pltpu.PrefetchScalarGridSpec(
            num_scalar_prefetch=1, grid=(rh // tr,),
            in_specs=[pl.BlockSpec((N_CHIPS, None, tr, cols), lambda i, c_ref: (0, c_ref[0], i, 0)),
                      pl.BlockSpec((N_CHIPS, tr, cols), lambda i, c_ref: (0, i, 0))],
            out_specs=pl.BlockSpec((N_CHIPS, tr, cols), lambda i, c_ref: (0, i, 0))),
        out_shape=jax.ShapeDtypeStruct((N_CHIPS, rh, cols), BF),
        compiler_params=_cp(("parallel",)),
    )(cidx, g, recv)


def _chip_send_start(name, sums):
    n = len(sums)
    lands = [lax.empty((3,) + s.shape[1:], BF) for s in sums]

    def run(refs, old, new):
        x, y, c = _place()
        for w in range(n):
            for j, (cx, cy) in enumerate(_other_chips(x, y)):
                _remote(refs[w].at[2 * cx + cy], refs[n + w].at[j], new, 3 * w + j, (cx, cy, c)).start()

    return _split_call(name, list(sums) + lands, run, new=3 * n)


def _chip_send_wait(name, bufs, sems, after):
    n = len(bufs) // 2

    def run(refs, old, new):
        x, y, c = _place()
        for w in range(n):
            for j, (cx, cy) in enumerate(_other_chips(x, y)):
                cp = _remote(refs[w].at[2 * cx + cy], refs[n + w].at[j], old, 3 * w + j, (cx, cy, c))
                cp.wait_send()
                cp.wait_recv()

    thru = _split_call(name, bufs, run, old=sems, after=after)[0]
    return thru[:n], thru[n:]


def _chip_add(own, parts, kidx, cidx, name):
    _, rh, cols = parts.shape
    tr = min(rh, 512)

    def body(k_ref, c_ref, own_ref, p_ref, o_ref):
        acc = own_ref[...].astype(F32)
        for s in range(3):
            acc = acc + p_ref[s].astype(F32)
        o_ref[...] = acc

    return pl.pallas_call(
        body, name=name,
        grid_spec=pltpu.PrefetchScalarGridSpec(
            num_scalar_prefetch=2, grid=(rh // tr,),
            in_specs=[pl.BlockSpec((None, tr, cols), lambda i, k_ref, c_ref: (k_ref[0], i, 0)),
                      pl.BlockSpec((3, tr, cols), lambda i, k_ref, c_ref: (0, i, 0))],
            out_specs=pl.BlockSpec((None, tr, cols), lambda i, k_ref, c_ref: (c_ref[0], i, 0))),
        out_shape=jax.ShapeDtypeStruct((2, rh, cols), F32),
        compiler_params=_cp(("parallel",)),
    )(kidx, cidx, own, parts)


def _pair_swap_start(name, bufs):
    def run(refs, old, new):
        x, y, c = _place()
        for w, ref in enumerate(refs):
            _remote(ref.at[c], ref.at[c], new, w, (x, y, 1 - c)).start()

    return _split_call(name, bufs, run, new=len(bufs))


def _pair_swap_wait(name, bufs, sems, after):
    def run(refs, old, new):
        x, y, c = _place()
        for w, ref in enumerate(refs):
            _remote(ref.at[c], ref.at[c], old, w, (x, y, 1 - c)).wait_send()
            _remote(ref.at[1 - c], ref.at[1 - c], old, w, (x, y, 1 - c)).wait_recv()

    return _split_call(name, bufs, run, old=sems, after=after)[0]


def _peers(x, y, c):
    out = []
    for code in range(1, 8):
        fx, fy, fc = (code >> 2) & 1, (code >> 1) & 1, code & 1
        px = 1 - x if fx else x
        py = 1 - y if fy else y
        pc = 1 - c if fc else c
        out.append((code, (px, py, pc)))
    return out


def _mod_exchange(c_row, ada_w, ada_b4, deps):
    ncol = ada_w.shape[1]

    def body(c_ref, w_ref, b_ref, *rest):
        call_ref, mod_ref, part_ref, send_sems, recv_sems = rest[len(deps):]
        x, y, c = _place()
        k = 2 * x + y
        me = 4 * x + 2 * y + c
        call_ref[pl.ds(me, 1), :] = c_ref[...]
        sends = []
        for code, peer in _peers(x, y, c):
            cp = pltpu.make_async_remote_copy(
                src_ref=c_ref, dst_ref=call_ref.at[pl.ds(me, 1), :],
                send_sem=send_sems.at[code], recv_sem=recv_sems.at[code],
                device_id=peer, device_id_type=MESH)
            cp.start()
            sends.append(cp)
        for code, (px, py, pc) in _peers(x, y, c):
            land = call_ref.at[pl.ds(4 * px + 2 * py + pc, 1), :]
            pltpu.make_async_remote_copy(
                src_ref=land, dst_ref=land, send_sem=send_sems.at[code], recv_sem=recv_sems.at[code],
                device_id=(px, py, pc), device_id_type=MESH).wait_recv()
        call = call_ref[...]
        act = call * _sigmoid(call)
        part = jnp.dot(act, w_ref[...], preferred_element_type=F32,
                       precision=lax.Precision.HIGHEST) + b_ref[pl.ds(k, 1), :]
        part_ref[...] = part
        mod_ref[pl.ds(k, 1), :] = part_ref[pl.ds(me, 1), :]
        chips = [(8 + j, peer) for j, (code, peer) in enumerate(_peers(x, y, c)) if code in (2, 4, 6)]
        for slot, (px, py, pc) in chips:
            cp = pltpu.make_async_remote_copy(
                src_ref=part_ref.at[pl.ds(4 * px + 2 * py + pc, 1), :], dst_ref=mod_ref.at[pl.ds(k, 1), :],
                send_sem=send_sems.at[slot], recv_sem=recv_sems.at[slot],
                device_id=(px, py, pc), device_id_type=MESH)
            cp.start()
            sends.append(cp)
        for slot, (px, py, pc) in chips:
            land = mod_ref.at[pl.ds(2 * px + py, 1), :]
            pltpu.make_async_remote_copy(
                src_ref=land, dst_ref=land, send_sem=send_sems.at[slot], recv_sem=recv_sems.at[slot],
                device_id=(px, py, pc), device_id_type=MESH).wait_recv()
        for cp in sends:
            cp.wait_send()

    vm = pl.BlockSpec(memory_space=pltpu.VMEM)
    return pl.pallas_call(
        body, name="mod_exchange",
        in_specs=[vm, vm, vm] + [ANY] * len(deps), out_specs=[vm, vm],
        out_shape=[jax.ShapeDtypeStruct((8, D_MODEL), F32), jax.ShapeDtypeStruct((N_CHIPS, ncol), F32)],
        scratch_shapes=[pltpu.VMEM((8, ncol), F32), pltpu.SemaphoreType.DMA((16,)),
                        pltpu.SemaphoreType.DMA((16,))],
        compiler_params=_cp(),
    )(c_row, ada_w, ada_b4, *deps)


def _small_exchange(stats, rows, c_all, wsm, msm, vsm):
    ncol = 6 * D_MODEL // N_CHIPS
    ns = len(stats)

    def body(*refs):
        call_ref, w_ref, m_ref, v_ref, gw_ref, loss_ref = refs[ns:ns + 6]
        outs = refs[ns + 6:ns + 30]
        p_ref, g_ref, all_ref, dm_ref, send_sems, recv_sems = refs[ns + 30:]
        x, y, c = _place()
        k = 2 * x + y
        me = 4 * x + 2 * y + c
        for r, (tab, row) in enumerate(rows):
            p_ref[r] = refs[tab][row:row + 1, :]
        all_ref[:, pl.ds(me, 1), :] = p_ref[...]
        sends = []
        for code, peer in _peers(x, y, c):
            cp = pltpu.make_async_remote_copy(
                src_ref=p_ref, dst_ref=all_ref.at[:, pl.ds(me, 1), :],
                send_sem=send_sems.at[code], recv_sem=recv_sems.at[code],
                device_id=peer, device_id_type=MESH)
            cp.start()
            sends.append(cp)
        for code, (px, py, pc) in _peers(x, y, c):
            land = all_ref.at[:, pl.ds(4 * px + 2 * py + pc, 1), :]
            pltpu.make_async_remote_copy(
                src_ref=land, dst_ref=land, send_sem=send_sems.at[code], recv_sem=recv_sems.at[code],
                device_id=(px, py, pc), device_id_type=MESH).wait_recv()
        for cp in sends:
            cp.wait_send()
        tot = [_colsum(all_ref[r]) for r in range(N_PAY)]
        loss_ref[...] = jnp.sum(tot[P_LOSS], axis=1, keepdims=True)
        g_ref[...] = jnp.zeros_like(g_ref)
        for r in range(P_LOSS):
            g_ref[r:r + 1, :] = tot[r]
        g = g_ref[...]
        for kind, tab in enumerate((g,) + _adamw_math(w_ref[...], g, m_ref[...], v_ref[...])):
            for r in range(6):
                outs[6 * kind][:, r * D_MODEL:(r + 1) * D_MODEL] = tab[r:r + 1, :]
            for i in range(5):
                outs[6 * kind + 1 + i][...] = tab[6 + i:7 + i, :]
        half = D_MODEL // 2
        for kk in range(N_CHIPS):
            @pl.when(k == kk)
            def _():
                r0 = 3 * (kk // 2)
                if kk % 2 == 0:
                    dm_ref[:, :D_MODEL] = all_ref[r0]
                    dm_ref[:, D_MODEL:] = all_ref[r0 + 1][:, :half]
                else:
                    dm_ref[:, :half] = all_ref[r0 + 1][:, half:]
                    dm_ref[:, half:] = all_ref[r0 + 2]
        call = call_ref[...]
        act = call * _sigmoid(call)
        gw_ref[...] = lax.dot_general(act, dm_ref[...], (((0,), (0,)), ((), ())),
                                      preferred_element_type=F32, precision=lax.Precision.HIGHEST)

    vm = pl.BlockSpec(memory_space=pltpu.VMEM)
    vectors = [jax.ShapeDtypeStruct((1, 6 * D_MODEL), F32)] + [jax.ShapeDtypeStruct((1, D_MODEL), F32)] * 5
    outs = pl.pallas_call(
        body, name="small_exchange",
        in_specs=[vm] * (ns + 4), out_specs=[vm] * 26,
        out_shape=[jax.ShapeDtypeStruct((D_MODEL, ncol), F32), jax.ShapeDtypeStruct((1, 1), F32)] + vectors * 4,
        scratch_shapes=[pltpu.VMEM((N_PAY, 1, D_MODEL), F32), pltpu.VMEM((16, D_MODEL), F32),
                        pltpu.VMEM((N_PAY, 8, D_MODEL), F32), pltpu.VMEM((8, ncol), F32),
                        pltpu.SemaphoreType.DMA((8,)), pltpu.SemaphoreType.DMA((8,))],
        compiler_params=_cp(),
    )(*stats, c_all, wsm, msm, vsm)
    return outs[0], outs[1], [outs[2 + 6 * kind:8 + 6 * kind] for kind in range(4)]


def _rope_tables(pos_col, inv_freq, sign, dep=None):
    def body(p_ref, f_ref, s_ref, cos_ref, sin_ref):
        ang = p_ref[...].astype(F32) * f_ref[...]
        cos_ref[...] = jnp.cos(ang)
        sin_ref[...] = jnp.sin(ang) * s_ref[...]

    tr = 512
    shp = jax.ShapeDtypeStruct((SEQ, 128), F32)
    body, in_specs, args = _add_dep(
        body, [pl.BlockSpec((tr, 1), lambda i: (i, 0)), pl.BlockSpec((1, 128), lambda i: (0, 0)),
               pl.BlockSpec((1, 128), lambda i: (0, 0))], [pos_col, inv_freq, sign], dep)
    return pl.pallas_call(
        body, name="rope_tables", grid=(SEQ // tr,),
        in_specs=in_specs,
        out_specs=[pl.BlockSpec((tr, 128), lambda i: (i, 0))] * 2, out_shape=[shp, shp],
        compiler_params=_cp(("parallel",)),
    )(*args)


def _resident(shape):
    nd = len(shape)
    return pl.BlockSpec(shape, lambda *_: (0,) * nd, pipeline_mode=pl.Buffered(1))


def _ln_proj(x, vecs, w_in4):
    tm = 256
    wc = w_in4.shape[2]

    def body(x_ref, vec_ref, w_ref, h_ref, proj_ref):
        xx = x_ref[...]
        g = _row(vec_ref, V_G1) * (1.0 + _row(vec_ref, V_SC1))
        h = (xx * _rms(xx) * g + _row(vec_ref, V_SH1)).astype(BF)
        h_ref[...] = h
        for j in range(N_CHIPS):
            proj_ref[:, j * wc:(j + 1) * wc] = _dot(h, w_ref[j]).astype(BF)

    return pl.pallas_call(
        body, name="ln_proj", grid=(SEQ // tm,),
        in_specs=[pl.BlockSpec((tm, D_MODEL), lambda i: (i, 0)), _resident((16, D_MODEL)),
                  _resident(w_in4.shape)],
        out_specs=[pl.BlockSpec((tm, D_MODEL), lambda i: (i, 0)), pl.BlockSpec((tm, D_IN), lambda i: (i, 0))],
        out_shape=[jax.ShapeDtypeStruct((SEQ, D_MODEL), BF), jax.ShapeDtypeStruct((SEQ, D_IN), BF)],
        compiler_params=_cp(("parallel",)),
    )(x, vecs, w_in4)


def _lane_first(shape):
    lane = lax.broadcasted_iota(jnp.int32, shape, 1)
    return (lane & 32) == 0


def _rot(v, cos, sin_s):
    partner = jnp.where(_lane_first(v.shape), pltpu.roll(v, 96, 1), pltpu.roll(v, 32, 1))
    return v * cos + partner * sin_s


def _rot_t(dv, cos, sin_s):
    t = dv * sin_s
    partner = jnp.where(_lane_first(dv.shape), pltpu.roll(t, 96, 1), pltpu.roll(t, 32, 1))
    return dv * cos + partner


def _ret_mask(lg):
    t = RET_BLOCK
    ii = lax.broadcasted_iota(jnp.int32, (t, t), 0)
    jj = lax.broadcasted_iota(jnp.int32, (t, t), 1)
    dist = jnp.abs(ii - jj).astype(F32)
    future = (jj >> RET_CHUNK_SHIFT) > (ii >> RET_CHUNK_SHIFT)
    return jnp.where(future, 0.0, jnp.exp(lg * dist))


def _ret_masks(lg, mask_ref, head):
    t = RET_BLOCK
    mask = mask_ref[head]
    ti = lax.broadcasted_iota(jnp.int32, (t, 1), 0).astype(F32)
    from_start = jnp.exp(lg * (ti + 1.0))
    to_end = jnp.exp(lg * (t - 1.0 - ti))
    whole = jnp.exp(jnp.full((1, 128), lg * t, F32))
    return mask, from_start, to_end, whole


def _head_lanes(shape, hh):
    lane = lax.broadcasted_iota(jnp.int32, shape, 1)
    return (lane >> 6) == hh


def _ret_specs():
    t = RET_BLOCK
    return dict(
        q=lambda f: pl.BlockSpec((t, 512), lambda n: (f(n), C_QR // 512)),
        k=lambda f: pl.BlockSpec((t, 512), lambda n: (f(n), C_KR // 512)),
        v=lambda f: pl.BlockSpec((t, D_MODEL), lambda n: (f(n), C_VR // D_MODEL)),
        g=lambda f: pl.BlockSpec((t, D_MODEL), lambda n: (f(n), C_GR // D_MODEL)),
        tab=lambda f: pl.BlockSpec((t, 128), lambda n: (f(n), 0)),
        wide=lambda f: pl.BlockSpec((t, D_MODEL), lambda n: (f(n), 0)),
        state=lambda f: pl.BlockSpec((N_PAIRS, None, 2, 128, 128), lambda n: (0, f(n), 0, 0, 0)),
    )


def _ret_fwd(proj, cos, sin_s, gn_g, log_gamma, dep=None):
    t = RET_BLOCK
    nb = SEQ // t

    def body(lg_ref, q_ref, k_ref, v_ref, g_ref, cos_ref, sin_ref, gn_ref, o_ref, retg_ref, st_ref, state, masks):
        @pl.when(pl.program_id(0) == 0)
        def _():
            state[...] = jnp.zeros_like(state)
            for head in range(2 * N_PAIRS):
                masks[head] = _ret_mask(lg_ref[head])

        cos, sn = cos_ref[...], sin_ref[...]
        for p in range(N_PAIRS):
            q = _rot(q_ref[:, 128 * p:128 * (p + 1)].astype(F32), cos, sn)
            k = _rot(k_ref[:, 128 * p:128 * (p + 1)].astype(F32), cos, sn) * QK_SCALE
            for hh in range(2):
                cols = slice(256 * p + 128 * hh, 256 * p + 128 * (hh + 1))
                lg = lg_ref[2 * p + hh]
                mask, from_start, to_end, whole = _ret_masks(lg, masks, 2 * p + hh)
                lanes = _head_lanes(q.shape, hh)
                qm = jnp.where(lanes, q, 0.0)
                km = jnp.where(lanes, k, 0.0)
                vh = v_ref[:, cols]
                sc = _dot_nt(qm.astype(BF), km.astype(BF)) * mask
                st = state[p, hh]
                st_ref[p, hh] = st
                o = _dot(sc.astype(BF), vh) + _dot((qm * from_start).astype(BF), st.astype(BF))
                state[p, hh] = whole * st + _dot_tn((km * to_end).astype(BF), vh)
                d = o - _rowmean(o)
                nh = d * lax.rsqrt(_rowmean(d * d) + EPS)
                gr = g_ref[:, cols].astype(F32)
                o_ref[:, cols] = o
                retg_ref[:, cols] = (gr * _sigmoid(gr) * nh * gn_ref[:, cols]).astype(BF)

    sp = _ret_specs()
    ident = lambda n: n
    body, in_specs, args = _add_dep(
        body, [pl.BlockSpec(memory_space=pltpu.SMEM), sp["q"](ident), sp["k"](ident), sp["v"](ident),
               sp["g"](ident), sp["tab"](ident), sp["tab"](ident), _resident((1, D_MODEL))],
        [log_gamma, proj, proj, proj, proj, cos, sin_s, gn_g], dep)
    return pl.pallas_call(
        body, name="ret_fwd", grid=(nb,),
        in_specs=in_specs,
        out_specs=[sp["wide"](ident), sp["wide"](ident), sp["state"](ident)],
        out_shape=[jax.ShapeDtypeStruct((SEQ, D_MODEL), F32), jax.ShapeDtypeStruct((SEQ, D_MODEL), BF),
                   jax.ShapeDtypeStruct((N_PAIRS, nb, 2, 128, 128), F32)],
        scratch_shapes=[pltpu.VMEM((N_PAIRS, 2, 128, 128), F32),
                        pltpu.VMEM((2 * N_PAIRS, RET_BLOCK, RET_BLOCK), F32)],
        compiler_params=_cp(("arbitrary",)),
    )(*args)


def _ret_bwd(proj, cos, sin_s, dret, states, log_gamma, dep=None):
    t = RET_BLOCK
    nb = SEQ // t

    def body(lg_ref, q_ref, k_ref, v_ref, cos_ref, sin_ref, do_ref, st_ref, dqkv_ref, dstate, masks):
        @pl.when(pl.program_id(0) == 0)
        def _():
            dstate[...] = jnp.zeros_like(dstate)
            for head in range(2 * N_PAIRS):
                masks[head] = _ret_mask(lg_ref[head])

        cos, sn = cos_ref[...], sin_ref[...]
        for p in range(N_PAIRS):
            q = _rot(q_ref[:, 128 * p:128 * (p + 1)].astype(F32), cos, sn)
            k = _rot(k_ref[:, 128 * p:128 * (p + 1)].astype(F32), cos, sn) * QK_SCALE
            dq_rot = jnp.zeros(q.shape, F32)
            dk_rot = jnp.zeros(q.shape, F32)
            for hh in range(2):
                cols = slice(256 * p + 128 * hh, 256 * p + 128 * (hh + 1))
                lg = lg_ref[2 * p + hh]
                mask, from_start, to_end, whole = _ret_masks(lg, masks, 2 * p + hh)
                lanes = _head_lanes(q.shape, hh)
                qm = jnp.where(lanes, q, 0.0)
                km = jnp.where(lanes, k, 0.0)
                qb, kb = qm.astype(BF), km.astype(BF)
                vh = v_ref[:, cols]
                do = do_ref[:, cols]
                sc = (_dot_nt(qb, kb) * mask).astype(BF)
                st = st_ref[p, hh].astype(BF)
                dst = dstate[p, hh]
                dstb = dst.astype(BF)
                k_end = (km * to_end).astype(BF)
                q_start = (qm * from_start).astype(BF)
                dqkv_ref[:, C_VR + 256 * p + 128 * hh:C_VR + 256 * p + 128 * (hh + 1)] = (
                    _dot_tn(sc, do) + _dot(k_end, dstb)).astype(BF)
                dsc = (_dot_nt(do, vh) * mask).astype(BF)
                dq_h = _dot(dsc, kb) + _dot_nt(do, st) * from_start
                dq_rot = dq_rot + jnp.where(lanes, dq_h, 0.0)
                dk_rot = dk_rot + _dot_tn(dsc, qb) + _dot_nt(vh, dstb) * to_end
                dstate[p, hh] = whole * dst + _dot_tn(q_start, do)
            dqkv_ref[:, C_QR + 128 * p:C_QR + 128 * (p + 1)] = _rot_t(dq_rot, cos, sn).astype(BF)
            dqkv_ref[:, C_KR + 128 * p:C_KR + 128 * (p + 1)] = _rot_t(dk_rot * QK_SCALE, cos, sn).astype(BF)

    sp = _ret_specs()
    rev = lambda n: nb - 1 - n
    body, in_specs, args = _add_dep(
        body, [pl.BlockSpec(memory_space=pltpu.SMEM), sp["q"](rev), sp["k"](rev), sp["v"](rev),
               sp["tab"](rev), sp["tab"](rev), sp["wide"](rev), sp["state"](rev)],
        [log_gamma, proj, proj, proj, cos, sin_s, dret, states], dep)
    return pl.pallas_call(
        body, name="ret_bwd", grid=(nb,),
        in_specs=in_specs,
        out_specs=pl.BlockSpec((t, C_GR), lambda n: (rev(n), 0)),
        out_shape=jax.ShapeDtypeStruct((SEQ, C_GR), BF),
        scratch_shapes=[pltpu.VMEM((N_PAIRS, 2, 128, 128), F32),
                        pltpu.VMEM((2 * N_PAIRS, RET_BLOCK, RET_BLOCK), F32)],
        compiler_params=_cp(("arbitrary",)),
    )(*args)


def _stack_heads(v):
    return jnp.concatenate([jnp.where(_head_lanes(v.shape, hh), v, jnp.zeros_like(v)) for hh in range(2)], axis=0)


def _unstack_heads(v):
    t = v.shape[0] // 2
    return jnp.where(_head_lanes((t, v.shape[1]), 0), v[:t], v[t:])


def _sb_masks(t, heads):
    rr = lax.broadcasted_iota(jnp.int32, (t, t), 0)
    cc = lax.broadcasted_iota(jnp.int32, (t, t), 1)
    r2 = lax.broadcasted_iota(jnp.int32, (heads * t, t), 0) & (t - 1)
    c2 = lax.broadcasted_iota(jnp.int32, (heads * t, t), 1)
    return rr, cc, c2 < r2


def _split_dot2(v, tri):
    return _dot(v.astype(BF), tri)


def _log_one_minus_beta(z):
    return -(jnp.maximum(z, 0.0) + jnp.log(1.0 + jnp.exp(-jnp.abs(z))))


def _sb_fwd(proj):
    t, g = SB_BLOCK, SB_GROUP
    nq = SEQ // t
    rows = 2 * g * t

    def body(q_ref, k_ref, v_ref, o_ref, tot_ref, kt_ref):
        i = pl.program_id(1)

        @pl.when(i == 0)
        def _():
            for p in range(g):
                for jj in range(nq):
                    kt_ref[p, jj] = k_ref[jj * t:(jj + 1) * t, 128 * p:128 * (p + 1)].T

        q2 = [_stack_heads((q_ref[:, 128 * p:128 * (p + 1)].astype(F32) * QK_SCALE).astype(BF)) for p in range(g)]
        rr, cc, valid = _sb_masks(t, 2 * g)
        later = (rr > cc).astype(BF)

        def tile(j, carry, diagonal):
            acc, run = carry
            z = jnp.concatenate([_dot(q2[p], kt_ref[p, j]) for p in range(g)], axis=0)
            lm = _log_one_minus_beta(z)
            if diagonal:
                lm = jnp.where(valid, lm, 0.0)
            after = _split_dot2(lm, later)
            a = jnp.exp(z + lm + after + run)
            if diagonal:
                a = jnp.where(valid, a, 0.0)
            ab = a.astype(BF)
            keys = pl.ds(pl.multiple_of(j * t, t), t)
            av = jnp.concatenate([_dot(ab[2 * t * p:2 * t * (p + 1)], v_ref[keys, 128 * p:128 * (p + 1)])
                                  for p in range(g)], axis=0)
            return acc + av, run + after[:, 0:1] + lm[:, 0:1]

        carry = tile(i, (jnp.zeros((rows, 128), F32), jnp.zeros((rows, 1), F32)), True)
        acc, run = lax.fori_loop(0, i, lambda s, cr: tile(i - 1 - s, cr, False), carry)
        run = jnp.broadcast_to(run, (rows, 128))
        for p in range(g):
            o_ref[:, 128 * p:128 * (p + 1)] = _unstack_heads(acc[2 * t * p:2 * t * (p + 1)]).astype(BF)
            tot_ref[:, 128 * p:128 * (p + 1)] = _unstack_heads(run[2 * t * p:2 * t * (p + 1)])

    w = 128 * g
    return pl.pallas_call(
        body, name="sb_fwd", grid=(N_PAIRS // g, nq),
        in_specs=[pl.BlockSpec((t, w), lambda p, i: (i, C_QS // w + p)),
                  pl.BlockSpec((SEQ, w), lambda p, i: (0, C_KS // w + p)),
                  pl.BlockSpec((SEQ, w), lambda p, i: (0, C_VS // w + p))],
        out_specs=[pl.BlockSpec((t, w), lambda p, i: (i, p))] * 2,
        out_shape=[jax.ShapeDtypeStruct((SEQ, 512), BF), jax.ShapeDtypeStruct((SEQ, 512), F32)],
        scratch_shapes=[pltpu.VMEM((g, nq, 128, t), BF)],
        compiler_params=_cp(("parallel", "arbitrary")),
    )(proj, proj, proj)


def _sb_bwd(proj, dsb, tot, dep=None):
    t, g = SB_BLOCK, SB_GROUP
    nq = SEQ // t
    rows = 2 * g * t

    def body(q_ref, k_ref, v_ref, do_ref, tot_ref, dq_ref, dk_ref, dv_ref, kt_ref, vt_ref, dkt_acc, dvt_acc):
        i = pl.program_id(1)

        @pl.when(i == 0)
        def _():
            dkt_acc[...] = jnp.zeros_like(dkt_acc)
            dvt_acc[...] = jnp.zeros_like(dvt_acc)
            for p in range(g):
                for jj in range(nq):
                    kt_ref[p, jj] = k_ref[jj * t:(jj + 1) * t, 128 * p:128 * (p + 1)].T
                    vt_ref[p, jj] = v_ref[jj * t:(jj + 1) * t, 128 * p:128 * (p + 1)].T

        q2 = [_stack_heads((q_ref[:, 128 * p:128 * (p + 1)].astype(F32) * QK_SCALE).astype(BF)) for p in range(g)]
        do2 = [_stack_heads(do_ref[:, 128 * p:128 * (p + 1)]) for p in range(g)]
        q2t = [v.T for v in q2]
        do2t = [v.T for v in do2]
        tots = tot_ref[...]
        total = jnp.concatenate([tots[:, 64 * h:64 * h + 1] for h in range(2 * g)], axis=0)
        rr, cc, valid = _sb_masks(t, 2 * g)
        upto = (rr <= cc).astype(BF)
        before = (rr < cc).astype(BF)

        def part(v, p):
            return v[2 * t * p:2 * t * (p + 1)]

        def tile(j, carry, diagonal):
            dq, run_l, run_g = carry
            z = jnp.concatenate([_dot(q2[p], kt_ref[p, j]) for p in range(g)], axis=0)
            lm = _log_one_minus_beta(z)
            if diagonal:
                lm = jnp.where(valid, lm, 0.0)
            incl = _split_dot2(lm, upto)
            a = jnp.exp(z + lm + (total - (incl + run_l)))
            if diagonal:
                a = jnp.where(valid, a, 0.0)
            gg = a * jnp.concatenate([_dot(do2[p], vt_ref[p, j]) for p in range(g)], axis=0)
            excl = _split_dot2(gg, before)
            dz = gg * jnp.exp(lm) - (excl + run_g) * jnp.exp(z + lm)
            if diagonal:
                dz = jnp.where(valid, dz, 0.0)
            dzb = dz.astype(BF)
            ab = a.astype(BF)
            keys = pl.ds(pl.multiple_of(j * t, t), t)
            for p in range(g):
                dkt_acc[p, j] += _dot(q2t[p], part(dzb, p))
                dvt_acc[p, j] += _dot(do2t[p], part(ab, p))
            dq_t = jnp.concatenate([_dot(part(dzb, p), k_ref[keys, 128 * p:128 * (p + 1)]) for p in range(g)], axis=0)
            return (dq + dq_t, run_l + incl[:, t - 1:t], run_g + excl[:, t - 1:t] + gg[:, t - 1:t])

        zero = jnp.zeros((rows, 1), F32)
        carry = lax.fori_loop(0, i, lambda j, cr: tile(j, cr, False), (jnp.zeros((rows, 128), F32), zero, zero))
        dq = tile(i, carry, True)[0]
        for p in range(g):
            dq_ref[:, 128 * p:128 * (p + 1)] = (_unstack_heads(part(dq, p)) * QK_SCALE).astype(BF)

        @pl.when(i == nq - 1)
        def _():
            for p in range(g):
                for jj in range(nq):
                    dk_ref[jj * t:(jj + 1) * t, 128 * p:128 * (p + 1)] = dkt_acc[p, jj].T.astype(BF)
                    dv_ref[jj * t:(jj + 1) * t, 128 * p:128 * (p + 1)] = dvt_acc[p, jj].T.astype(BF)

    w = 128 * g
    tile_spec = pl.BlockSpec((t, w), lambda p, i: (i, p))
    col_spec = pl.BlockSpec((SEQ, w), lambda p, i: (0, p))
    shp = jax.ShapeDtypeStruct((SEQ, 512), BF)
    body, in_specs, args = _add_dep(
        body, [pl.BlockSpec((t, w), lambda p, i: (i, C_QS // w + p)),
               pl.BlockSpec((SEQ, w), lambda p, i: (0, C_KS // w + p)),
               pl.BlockSpec((SEQ, w), lambda p, i: (0, C_VS // w + p)),
               tile_spec, tile_spec],
        [proj, proj, proj, dsb, tot], dep)
    return pl.pallas_call(
        body, name="sb_bwd", grid=(N_PAIRS // g, nq),
        in_specs=in_specs,
        out_specs=[tile_spec, col_spec, col_spec],
        out_shape=[shp, shp, shp],
        scratch_shapes=[pltpu.VMEM((g, nq, 128, t), BF), pltpu.VMEM((g, nq, 128, t), BF),
                        pltpu.VMEM((g, nq, 128, t), F32), pltpu.VMEM((g, nq, 128, t), F32)],
        compiler_params=_cp(("parallel", "arbitrary")),
    )(*args)


def _mix_out(retg, sb, proj, x, vecs, w_ret, w_sb4, w_out):
    tm, half = 256, 512

    def body(r_ref, s_ref, ar0, ar1, as0, as1, x_ref, vec_ref, wr_ref, ws_ref, wo_ref,
             mix_ref, rb_ref, sbp_ref, y_ref, h1_ref, h2_ref):
        rb = _dot(r_ref[...], wr_ref[...])
        sbv = s_ref[...]
        sbp = jnp.concatenate([_dot(sbv, ws_ref[k]) for k in range(N_CHIPS)], axis=1)
        gate_r = _sigmoid(jnp.concatenate([ar0[...], ar1[...]], axis=1).astype(F32))
        gate_s = _sigmoid(jnp.concatenate([as0[...], as1[...]], axis=1).astype(F32))
        mixed = (gate_r * rb + gate_s * sbp).astype(BF)
        mix_ref[...] = mixed
        rb_ref[...] = rb.astype(BF)
        sbp_ref[...] = sbp.astype(BF)
        y = _dot(mixed, wo_ref[...])
        h1 = x_ref[...] + _row(vec_ref, V_GT1) * (y * _rms(y)) * _row(vec_ref, V_G2)
        g = _row(vec_ref, V_G3) * (1.0 + _row(vec_ref, V_SC2))
        y_ref[...] = y
        h1_ref[...] = h1
        h2_ref[...] = (h1 * _rms(h1) * g + _row(vec_ref, V_SH2)).astype(BF)

    row = pl.BlockSpec((tm, D_MODEL), lambda i: (i, 0))
    gate = lambda c0: pl.BlockSpec((tm, half), lambda i: (i, c0 // half))
    bf = jax.ShapeDtypeStruct((SEQ, D_MODEL), BF)
    f32 = jax.ShapeDtypeStruct((SEQ, D_MODEL), F32)
    return pl.pallas_call(
        body, name="mix_out", grid=(SEQ // tm,),
        in_specs=[row, pl.BlockSpec((tm, 512), lambda i: (i, 0)), gate(C_AR), gate(C_AR + half), gate(C_AS),
                  gate(C_AS + half), row, _resident((16, D_MODEL)), _resident((D_MODEL, D_MODEL)),
                  _resident(w_sb4.shape), _resident((D_MODEL, D_MODEL))],
        out_specs=[row] * 6, out_shape=[bf, bf, bf, f32, f32, bf],
        compiler_params=_cp(("parallel",)),
    )(retg, sb, proj, proj, proj, proj, x, vecs, w_ret, w_sb4, w_out)


def _ffn_fwd_loss(h2, h1, target, vecs, w_ff14, w_ff24):
    tm = 256

    def body(h2_ref, h1_ref, t_ref, vec_ref, w1_ref, w2_ref, u_ref, a_ref, dout_ref, df_ref, st_ref):
        @pl.when(pl.program_id(0) == 0)
        def _():
            st_ref[...] = jnp.zeros_like(st_ref)

        hb = h2_ref[...]
        f = jnp.zeros((tm, D_MODEL), F32)
        for k in range(N_CHIPS):
            cols = slice(k * D_MODEL, (k + 1) * D_MODEL)
            u = _dot(hb, w1_ref[k])
            r = jnp.maximum(u, 0.0)
            act = (r * r).astype(BF)
            u_ref[:, cols] = u.astype(BF)
            a_ref[:, cols] = act
            f = f + _dot(act, w2_ref[k])
        r4 = _rms(f)
        fn = f * r4
        gt2, g4 = _row(vec_ref, V_GT2), _row(vec_ref, V_G4)
        diff = h1_ref[...] + gt2 * fn * g4 - t_ref[...]
        dout = diff * (1.0 / D_MODEL)
        dfn = dout * gt2 * g4
        dout_ref[...] = dout
        df_ref[...] = (r4 * (dfn - fn * _rowmean(dfn * fn))).astype(BF)
        st_ref[0:1, :] += _colsum(dout * fn * g4)
        st_ref[1:2, :] += _colsum(dout * gt2 * fn)
        st_ref[2:3, :] += _colsum(diff * diff) * (0.5 / D_MODEL)

    row = pl.BlockSpec((tm, D_MODEL), lambda i: (i, 0))
    wide = pl.BlockSpec((tm, D_FF), lambda i: (i, 0))
    return pl.pallas_call(
        body, name="ffn_fwd_loss", grid=(SEQ // tm,),
        in_specs=[row, row, row, _resident((16, D_MODEL)), _resident(w_ff14.shape), _resident(w_ff24.shape)],
        out_specs=[wide, wide, row, row, pl.BlockSpec((8, D_MODEL), lambda i: (0, 0))],
        out_shape=[jax.ShapeDtypeStruct((SEQ, D_FF), BF), jax.ShapeDtypeStruct((SEQ, D_FF), BF),
                   jax.ShapeDtypeStruct((SEQ, D_MODEL), F32), jax.ShapeDtypeStruct((SEQ, D_MODEL), BF),
                   jax.ShapeDtypeStruct((8, D_MODEL), F32)],
        compiler_params=_cp(("arbitrary",)),
    )(h2, h1, target, vecs, w_ff14, w_ff24)


def _ffn_bwd(df, u, h1, y, dout, vecs, w_ff14, w_ff24):
    tm = 256

    def body(df_ref, u_ref, h1_ref, y_ref, dout_ref, vec_ref, w1_ref, w2_ref, du_ref, dh1_ref, dy_ref, st_ref):
        @pl.when(pl.program_id(0) == 0)
        def _():
            st_ref[...] = jnp.zeros_like(st_ref)

        dfb = df_ref[...]
        dh2 = jnp.zeros((tm, D_MODEL), F32)
        for k in range(N_CHIPS):
            cols = slice(k * D_MODEL, (k + 1) * D_MODEL)
            da = _dot_nt(dfb, w2_ref[k])
            du = (da * (2.0 * jnp.maximum(u_ref[:, cols].astype(F32), 0.0))).astype(BF)
            du_ref[:, cols] = du
            dh2 = dh2 + _dot_nt(du, w1_ref[k])
        h1 = h1_ref[...]
        r3 = _rms(h1)
        hn3 = h1 * r3
        g3, sc2 = _row(vec_ref, V_G3), _row(vec_ref, V_SC2)
        dhn3 = dh2 * g3 * (1.0 + sc2)
        dh1 = dout_ref[...] + r3 * (dhn3 - hn3 * _rowmean(dhn3 * hn3))
        y = y_ref[...]
        r2 = _rms(y)
        yn = y * r2
        gt1, g2 = _row(vec_ref, V_GT1), _row(vec_ref, V_G2)
        dyn = dh1 * gt1 * g2
        dh1_ref[...] = dh1
        dy_ref[...] = (r2 * (dyn - yn * _rowmean(dyn * yn))).astype(BF)
        st_ref[0:1, :] += _colsum(dh2)
        st_ref[1:2, :] += _colsum(dh2 * hn3 * g3)
        st_ref[2:3, :] += _colsum(dh2 * hn3 * (1.0 + sc2))
        st_ref[3:4, :] += _colsum(dh1 * yn * g2)
        st_ref[4:5, :] += _colsum(dh1 * gt1 * yn)

    row = pl.BlockSpec((tm, D_MODEL), lambda i: (i, 0))
    wide = pl.BlockSpec((tm, D_FF), lambda i: (i, 0))
    return pl.pallas_call(
        body, name="ffn_bwd", grid=(SEQ // tm,),
        in_specs=[row, wide, row, row, row, _resident((16, D_MODEL)), _resident(w_ff14.shape),
                  _resident(w_ff24.shape)],
        out_specs=[wide, row, row, pl.BlockSpec((8, D_MODEL), lambda i: (0, 0))],
        out_shape=[jax.ShapeDtypeStruct((SEQ, D_FF), BF), jax.ShapeDtypeStruct((SEQ, D_MODEL), F32),
                   jax.ShapeDtypeStruct((SEQ, D_MODEL), BF), jax.ShapeDtypeStruct((8, D_MODEL), F32)],
        compiler_params=_cp(("arbitrary",)),
    )(df, u, h1, y, dout, vecs, w_ff14, w_ff24)


def _mix_ret_bwd(dy, proj, rb, sbp, o_raw, gn_g, w_out, w_sb4, w_ret):
    tm, half = 256, 512

    def body(dy_ref, ar0, ar1, as0, as1, rb_ref, sbp_ref, g_ref, o_ref, gn_ref, wo_ref, ws_ref, wr_ref,
             drb_ref, dsbp_ref, da_ref, dsb_ref, dret_ref, dgr_ref, st_ref):
        @pl.when(pl.program_id(0) == 0)
        def _():
            st_ref[...] = jnp.zeros_like(st_ref)

        dm_all = _dot_nt(dy_ref[...], wo_ref[...])
        dsb = jnp.zeros((tm, 512), F32)
        drbs = []
        for hf, (ar_ref, as_ref) in enumerate(((ar0, as0), (ar1, as1))):
            cols = slice(half * hf, half * (hf + 1))
            dm = dm_all[:, cols]
            sr = _sigmoid(ar_ref[...].astype(F32))
            ss = _sigmoid(as_ref[...].astype(F32))
            dsbp = (dm * ss).astype(BF)
            drbs.append((dm * sr).astype(BF))
            dsbp_ref[:, cols] = dsbp
            da_ref[:, cols] = (dm * rb_ref[:, cols].astype(F32) * sr * (1.0 - sr)).astype(BF)
            da_ref[:, D_MODEL + half * hf:D_MODEL + half * (hf + 1)] = (
                dm * sbp_ref[:, cols].astype(F32) * ss * (1.0 - ss)).astype(BF)
            dsb = dsb + _dot_nt(dsbp[:, :256], ws_ref[2 * hf]) + _dot_nt(dsbp[:, 256:], ws_ref[2 * hf + 1])
        dsb_ref[...] = dsb.astype(BF)
        drb = jnp.concatenate(drbs, axis=1)
        drb_ref[...] = drb
        dretg = _dot_nt(drb, wr_ref[...])
        for gi in range(D_MODEL // 128):
            cols = slice(128 * gi, 128 * (gi + 1))
            o = o_ref[:, cols]
            d = o - _rowmean(o)
            rstd = lax.rsqrt(_rowmean(d * d) + EPS)
            nh = d * rstd
            gain = gn_ref[:, cols]
            gr = g_ref[:, cols].astype(F32)
            sg = _sigmoid(gr)
            dg = dretg[:, cols]
            dgn = dg * gr * sg
            dnh = dgn * gain
            dgr_ref[:, cols] = (dg * nh * gain * sg * (1.0 + gr * (1.0 - sg))).astype(BF)
            dret_ref[:, cols] = (rstd * (dnh - _rowmean(dnh) - nh * _rowmean(dnh * nh))).astype(BF)
            st_ref[0:1, cols] += _colsum(dgn * nh)

    row = pl.BlockSpec((tm, D_MODEL), lambda i: (i, 0))
    gate = lambda c0: pl.BlockSpec((tm, half), lambda i: (i, c0 // half))
    shp = jax.ShapeDtypeStruct((SEQ, D_MODEL), BF)
    return pl.pallas_call(
        body, name="mix_ret_bwd", grid=(SEQ // tm,),
        in_specs=[row, gate(C_AR), gate(C_AR + half), gate(C_AS), gate(C_AS + half), row, row,
                  pl.BlockSpec((tm, D_MODEL), lambda i: (i, C_GR // D_MODEL)), row, _resident((1, D_MODEL)),
                  _resident((D_MODEL, D_MODEL)), _resident(w_sb4.shape), _resident((D_MODEL, D_MODEL))],
        out_specs=[row, row, pl.BlockSpec((tm, 2 * D_MODEL), lambda i: (i, 0)), pl.BlockSpec((tm, 512), lambda i: (i, 0)),
                   row, row, pl.BlockSpec((8, D_MODEL), lambda i: (0, 0))],
        out_shape=[shp, shp, jax.ShapeDtypeStruct((SEQ, 2 * D_MODEL), BF), jax.ShapeDtypeStruct((SEQ, 512), BF),
                   shp, shp, jax.ShapeDtypeStruct((8, D_MODEL), F32)],
        compiler_params=_cp(("arbitrary",)),
    )(dy, proj, proj, proj, proj, rb, sbp, proj, o_raw, gn_g, w_out, w_sb4, w_ret)


def _dproj_segments(widths):
    wc = D_IN // N_CHIPS
    segs, start = [], 0
    for pi, width in enumerate(widths):
        lo = start
        while lo < start + width:
            j = lo // wc
            hi = min(start + width, (j + 1) * wc)
            segs.append((j, lo - j * wc, pi, lo - start, hi - lo))
            lo = hi
        start += width
    assert start == D_IN
    return segs


def _in_proj_bwd(pieces, x, dh1, vecs, w_in4, updates, dep=None):
    tm = 256
    steps = SEQ // tm
    n, nu = len(pieces), len(updates)
    segs = _dproj_segments([p.shape[1] for p in pieces])

    def body(*refs):
        x_ref, dh1_ref, vec_ref, w_ref = refs[n:n + 4]
        upd_in = refs[n + 4:n + 4 + 4 * nu]
        dx_ref, st_ref = refs[n + 4 + 4 * nu:n + 6 + 4 * nu]
        upd_out = refs[n + 6 + 4 * nu:]
        for u in range(nu):
            w_u, m_u, v_u, g_u = upd_in[4 * u:4 * u + 4]
            go_u, d_u, mo_u, vo_u = upd_out[4 * u:4 * u + 4]
            gg = g_u[...]
            go_u[...] = gg
            d_u[...], mo_u[...], vo_u[...] = _adamw_math(w_u[...], gg, m_u[...], v_u[...])

        @pl.when(pl.program_id(0) == 0)
        def _():
            st_ref[...] = jnp.zeros_like(st_ref)

        dh = jnp.zeros((tm, D_MODEL), F32)
        for j, so, pi, po, width in segs:
            dh = dh + _dot_nt(refs[pi][:, po:po + width], w_ref[j, :, so:so + width])
        xx = x_ref[...]
        r1 = _rms(xx)
        xn = xx * r1
        g1, sc1 = _row(vec_ref, V_G1), _row(vec_ref, V_SC1)
        dxn = dh * g1 * (1.0 + sc1)
        dx_ref[...] = dh1_ref[...] + r1 * (dxn - xn * _rowmean(dxn * xn))
        st_ref[0:1, :] += _colsum(dh)
        st_ref[1:2, :] += _colsum(dh * xn * g1)
        st_ref[2:3, :] += _colsum(dh * xn * (1.0 + sc1))

    row = pl.BlockSpec((tm, D_MODEL), lambda i: (i, 0))
    upd_specs, upd_shapes, upd_args = [], [], []
    for arrays in updates:
        rows, cols = arrays[0].shape
        upd_specs += [pl.BlockSpec((rows // steps, cols), lambda i: (i, 0))] * 4
        upd_shapes += [jax.ShapeDtypeStruct((rows, cols), F32)] * 4
        upd_args += list(arrays)
    body, in_specs, args = _add_dep(
        body, [pl.BlockSpec((tm, p.shape[1]), lambda i: (i, 0)) for p in pieces] + [
            row, row, _resident((16, D_MODEL)), _resident(w_in4.shape)] + upd_specs,
        list(pieces) + [x, dh1, vecs, w_in4] + upd_args, dep)
    outs = pl.pallas_call(
        body, name="in_proj_bwd", grid=(steps,),
        in_specs=in_specs,
        out_specs=[row, pl.BlockSpec((8, D_MODEL), lambda i: (0, 0))] + upd_specs,
        out_shape=[jax.ShapeDtypeStruct((SEQ, D_MODEL), F32), jax.ShapeDtypeStruct((8, D_MODEL), F32)] + upd_shapes,
        compiler_params=_cp(("arbitrary",)),
    )(*args)
    return outs[0], outs[1], [tuple(outs[2 + 4 * u:6 + 4 * u]) for u in range(nu)]


def _grad_w_in(h, pieces, dep=None):
    ta = 256
    n = len(pieces)
    segs = _dproj_segments([p.shape[1] for p in pieces])

    def body(*refs):
        h_ref, o_ref = refs[n], refs[n + 1]
        hh = h_ref[...]
        for j, so, pi, po, width in segs:
            o_ref[j, :, so:so + width] = _dot_tn(hh, refs[pi][:, po:po + width]).astype(BF)

    body, in_specs, args = _add_dep(
        body, [_resident(p.shape) for p in pieces] + [pl.BlockSpec((SEQ, ta), lambda i: (0, i))],
        list(pieces) + [h], dep)
    return pl.pallas_call(
        body, name="grad_w_in", grid=(D_MODEL // ta,),
        in_specs=in_specs,
        out_specs=pl.BlockSpec((N_CHIPS, ta, D_IN // N_CHIPS), lambda i: (0, i, 0)),
        out_shape=jax.ShapeDtypeStruct((N_CHIPS, D_MODEL, D_IN // N_CHIPS), BF),
        compiler_params=_cp(("parallel",)),
    )(*args)


def _weight_grad(a, b, ta, tb, col_sharded, name, dep=None):
    ka, nb_ = a.shape[1], b.shape[1]

    def body(a_ref, b_ref, o_ref):
        o_ref[...] = _dot_tn(a_ref[...], b_ref[...]).astype(BF)

    body, in_specs, args = _add_dep(
        body, [pl.BlockSpec((SEQ, ta), lambda i, j: (0, i)), pl.BlockSpec((SEQ, tb), lambda i, j: (0, j))],
        [a, b], dep)

    if col_sharded:
        per = nb_ // N_CHIPS // tb
        out_shape = jax.ShapeDtypeStruct((N_CHIPS, ka, nb_ // N_CHIPS), BF)
        out_spec = pl.BlockSpec((None, ta, tb), lambda i, j: (j // per, i, j % per))
    else:
        per = ka // N_CHIPS // ta
        out_shape = jax.ShapeDtypeStruct((N_CHIPS, ka // N_CHIPS, nb_), BF)
        out_spec = pl.BlockSpec((None, ta, tb), lambda i, j: (i // per, i % per, j))
    return pl.pallas_call(
        body, name=name, grid=(ka // ta, nb_ // tb),
        in_specs=in_specs, out_specs=out_spec, out_shape=out_shape,
        compiler_params=_cp(("parallel", "parallel")),
    )(*args)


def _rope_constants():
    freq = np.float32(ROPE_BASE) ** (-np.arange(0, 64, 2, dtype=np.float32) / np.float32(64))
    inv = np.tile(freq.astype(np.float32), 4).reshape(1, 128)
    sign = np.tile(np.concatenate([-np.ones(32, np.float32), np.ones(32, np.float32)]), 2).reshape(1, 128)
    return jnp.asarray(inv), jnp.asarray(sign)


def _log_gamma():
    return jnp.asarray(np.log1p(-(2.0 ** (-5.0 - np.arange(8, dtype=np.float64)))).astype(np.float32))


def _halves(g):
    return g.reshape(N_CHIPS, 2, g.shape[1] // 2, g.shape[2])


def kernel(x, c, positions, ada_w, ada_b, pre_mix_g, post_mix_g, pre_ffn_g, post_ffn_g, w_in, ret_gn_g, w_ret_branch, w_sb_branch, w_out, w_ff1, w_ff2, loss_target, m_ada_w, m_ada_b, m_pre_mix_g, m_post_mix_g, m_pre_ffn_g, m_post_ffn_g, m_w_in, m_ret_gn_g, m_w_ret_branch, m_w_sb_branch, m_w_out, m_w_ff1, m_w_ff2, v_ada_w, v_ada_b, v_pre_mix_g, v_post_mix_g, v_pre_ffn_g, v_post_ffn_g, v_w_in, v_ret_gn_g, v_w_ret_branch, v_w_sb_branch, v_w_out, v_w_ff1, v_w_ff2):
    names = ["w_in", "w_ret", "w_sb", "w_out", "w_ff1", "w_ff2"]
    big = dict(zip(names, [w_in, w_ret_branch, w_sb_branch, w_out, w_ff1, w_ff2]))
    big_m = dict(zip(names, [m_w_in, m_w_ret_branch, m_w_sb_branch, m_w_out, m_w_ff1, m_w_ff2]))
    big_v = dict(zip(names, [v_w_in, v_w_ret_branch, v_w_sb_branch, v_w_out, v_w_ff1, v_w_ff2]))
    rest = names[1:]
    cidx = lax.axis_index("c").astype(jnp.int32).reshape(1)
    kidx = (2 * lax.axis_index("x") + lax.axis_index("y")).astype(jnp.int32).reshape(1)
    x0, target = x[0], loss_target[0]

    buf_in, sem_in, tok_in = _gather_start("gather_in_start", [_cast_bf16(w_in[0], kidx, c, "cast_w_in")])
    rest_bufs = [_cast_bf16(big[nm][0], kidx, tok_in, "cast_" + nm) for nm in rest]
    inv_freq, sign = _rope_constants()
    lg = _log_gamma()
    cos, sin_s = _rope_tables(positions.reshape(SEQ, 1), inv_freq, sign, dep=tok_in)

    def table(b6, g5):
        return jnp.concatenate([b6.reshape(6, D_MODEL)] + g5 + [jnp.zeros((5, D_MODEL), F32)], axis=0)

    wsm = table(ada_b, [pre_mix_g, post_mix_g, pre_ffn_g, post_ffn_g, ret_gn_g])
    msm = table(m_ada_b, [m_pre_mix_g, m_post_mix_g, m_pre_ffn_g, m_post_ffn_g, m_ret_gn_g])
    vsm = table(v_ada_b, [v_pre_mix_g, v_post_mix_g, v_pre_ffn_g, v_post_ffn_g, v_ret_gn_g])
    c_all, mod4 = _mod_exchange(c, ada_w[0], ada_b.reshape(N_CHIPS, -1), rest_bufs + [cos, wsm, msm, vsm])
    vecs = jnp.concatenate([mod4.reshape(6, D_MODEL), pre_mix_g, post_mix_g, pre_ffn_g, post_ffn_g,
                            jnp.zeros((6, D_MODEL), F32)], axis=0)
    buf_in, sem_in, tok_in = _gather_pass("gather_in_pass", buf_in, sem_in, vecs)
    buf_rest, sem_rest, tok_rest = _gather_start("gather_rest_start", rest_bufs, after=tok_in)
    (w_in4,) = _gather_finish("gather_in_finish", buf_in, sem_in, tok_rest)

    h, proj = _ln_proj(x0, vecs, w_in4)
    sb, tot = _sb_fwd(proj)
    buf_rest, sem_rest, tok_rest = _gather_pass("gather_rest_pass", buf_rest, sem_rest, sb)
    o_raw, retg, states = _ret_fwd(proj, cos, sin_s, ret_gn_g, lg, dep=tok_rest)
    w_ret4, w_sb4, w_out4, w_ff14, w_ff24 = _gather_finish("gather_rest_finish", buf_rest, sem_rest, retg)
    w_ret = w_ret4.reshape(D_MODEL, D_MODEL)
    w_out2 = w_out4.reshape(D_MODEL, D_MODEL)
    mixed, rb, sbp, y, h1, h2 = _mix_out(retg, sb, proj, x0, vecs, w_ret, w_sb4, w_out2)
    u, act, dout, df, st_a = _ffn_fwd_loss(h2, h1, target, vecs, w_ff14, w_ff24)

    du, dh1, dy, st_b = _ffn_bwd(df, u, h1, y, dout, vecs, w_ff14, w_ff24)
    grads = {"w_ff2": _weight_grad(act, df, 512, 1024, False, "grad_w_ff2")}
    grads["w_ff1"] = _weight_grad(h2, du, 512, 1024, True, "grad_w_ff1")
    drb, dsbp, da, dsb, dret, dg_r, st_c = _mix_ret_bwd(dy, proj, rb, sbp, o_raw, ret_gn_g, w_out2, w_sb4, w_ret)
    grads["w_out"] = _weight_grad(mixed, dy, 256, 1024, False, "grad_w_out")
    grads["w_ret"] = _weight_grad(retg, drb, 256, 1024, False, "grad_w_ret")
    grads["w_sb"] = _weight_grad(sb, dsbp, 512, 256, True, "grad_w_sb")

    bufs, sems, tok = _pair_send_start("rs_rest_pair_send", [_halves(grads[nm]) for nm in rest])
    dqkv_r = _ret_bwd(proj, cos, sin_s, dret, states, lg, dep=tok)
    mine, theirs = _pair_send_wait("rs_rest_pair_recv", bufs, sems, dqkv_r)
    pair_sums = [_pair_add(g, r, cidx, "pair_add_" + nm) for g, r, nm in zip(mine, theirs, rest)]
    bufs, sems, tok = _chip_send_start("rs_rest_chip_send", pair_sums)
    dq_s, dk_s, dv_s = _sb_bwd(proj, dsb, tot, dep=tok)
    own, parts = _chip_send_wait("rs_rest_chip_recv", bufs, sems, dq_s)
    sums = [_chip_add(o, p, kidx, cidx, "chip_add_" + nm) for o, p, nm in zip(own, parts, rest)]
    bufs, sems, tok = _pair_swap_start("rs_rest_pair_swap", sums)
    dproj = [dqkv_r, dg_r, dq_s, dk_s, dv_s, da]
    g_in = _grad_w_in(h, dproj, dep=tok)
    full_rest = _pair_swap_wait("rs_rest_pair_swapped", bufs, sems, g_in)

    full_rest = dict(zip(rest, full_rest))
    bufs, sems, tok = _pair_send_start("rs_in_pair_send", [_halves(g_in)])
    out = {}
    for nm in ("w_out", "w_sb", "w_ret"):
        w = big[nm][0]
        out[nm] = _adamw(w, big_m[nm][0], big_v[nm][0], full_rest[nm].reshape(w.shape), "adamw_" + nm, dep=tok)
    mine, theirs = _pair_send_wait("rs_in_pair_recv", bufs, sems, out["w_ret"][1])
    bufs, sems, tok = _chip_send_start("rs_in_chip_send", [_pair_add(mine[0], theirs[0], cidx, "pair_add_w_in")])
    riding = ("w_ff2", "w_ff1")
    dx, st_d, updated = _in_proj_bwd(
        dproj, x0, dh1, vecs, w_in4,
        [(big[nm][0], big_m[nm][0], big_v[nm][0], full_rest[nm].reshape(big[nm].shape[1:])) for nm in riding],
        dep=tok)
    out.update(zip(riding, updated))

    a_, b_, c_, d_ = range(4)
    payload_rows = [(d_, 0), (d_, 1), (b_, 3), (b_, 0), (b_, 1), (a_, 0),
                    (d_, 2), (b_, 4), (b_, 2), (a_, 1), (c_, 0), (a_, 2)]
    g_ada, loss, small = _small_exchange([st_a, st_b, st_c, st_d], payload_rows, c_all, wsm, msm, vsm)
    ada_out = _adamw(ada_w[0], m_ada_w[0], v_ada_w[0], g_ada, "adamw_ada_w")

    own, parts = _chip_send_wait("rs_in_chip_recv", bufs, sems, ada_out[1])
    bufs, sems, tok = _pair_swap_start(
        "rs_in_pair_swap", [_chip_add(own[0], parts[0], kidx, cidx, "chip_add_w_in")])
    (full_in,) = _pair_swap_wait("rs_in_pair_swapped", bufs, sems, tok)
    out["w_in"] = _adamw(w_in[0], m_w_in[0], v_w_in[0], full_in.reshape(w_in.shape[1:]), "adamw_w_in")

    def ordered(which):
        sm = small[which]
        bg = [out[nm][which][None] for nm in names]
        return [ada_out[which][None], sm[0], sm[1], sm[2], sm[3], sm[4], bg[0], sm[5]] + bg[1:]

    return (loss.reshape(()), dx[None], *ordered(0), *ordered(1), *ordered(2), *ordered(3))
```

```python
import functools

import numpy as np
import jax
import jax.numpy as jnp
from jax import lax
from jax.experimental import pallas as pl
from jax.experimental.pallas import tpu as pltpu

SEQ = 2048
D_MODEL = 1024
D_IN = 6656
D_FF = 4096
N_CHIPS = 4
EPS = 1e-6
ROPE_BASE = 10000.0
RET_BLOCK = 256
RET_CHUNK_SHIFT = 6
SB_BLOCK = 256
QK_SCALE = 0.125
N_PAIRS = 4
SB_GROUP = 4

ADAM_LR = 0.001
ADAM_B1 = 0.9
ADAM_B2 = 0.999
ADAM_EPS = 1e-08
ADAM_WD = 0.01
ADAM_STEP = 10

BF = jnp.bfloat16
F32 = jnp.float32
MESH = pl.DeviceIdType.MESH
VMEM_LIMIT = 56 * 1024 * 1024
ANY = pl.BlockSpec(memory_space=pl.ANY)

C_QR, C_KR, C_VR, C_GR, C_QS, C_KS, C_VS, C_AR, C_AS = 0, 512, 1024, 2048, 3072, 3584, 4096, 4608, 5632

V_SH1, V_SC1, V_GT1, V_SH2, V_SC2, V_GT2, V_G1, V_G2, V_G3, V_G4 = range(10)
P_DSH1, P_DSC1, P_DGT1, P_DSH2, P_DSC2, P_DGT2, P_DG1, P_DG2, P_DG3, P_DG4, P_DGN, P_LOSS = range(12)
N_PAY = 12


def _cp(sem=None, **kw):
    if sem is not None:
        kw["dimension_semantics"] = sem
    return pltpu.CompilerParams(vmem_limit_bytes=VMEM_LIMIT, **kw)


def _dot(a, b):
    return jnp.dot(a, b, preferred_element_type=F32)


def _dot_nt(a, b):
    return lax.dot_general(a, b, (((1,), (1,)), ((), ())), preferred_element_type=F32)


def _dot_tn(a, b):
    return lax.dot_general(a, b, (((0,), (0,)), ((), ())), preferred_element_type=F32)


def _row(ref, i):
    return ref[i:i + 1, :]


def _rms(v):
    return lax.rsqrt(jnp.mean(v * v, axis=1, keepdims=True) + EPS)


def _colsum(v):
    return jnp.sum(v, axis=0, keepdims=True)


def _rowmean(v):
    return jnp.mean(v, axis=1, keepdims=True)


def _sigmoid(v):
    return 1.0 / (1.0 + jnp.exp(-v))


def _cast_bf16(w, kidx, dep, name):
    rows, cols = w.shape
    tr = min(rows, 512)

    def body(k_ref, w_ref, dep_ref, o_ref):
        o_ref[...] = w_ref[...].astype(BF)

    return pl.pallas_call(
        body, name=name,
        grid_spec=pltpu.PrefetchScalarGridSpec(
            num_scalar_prefetch=1, grid=(rows // tr,),
            in_specs=[pl.BlockSpec((tr, cols), lambda i, k_ref: (i, 0)), ANY],
            out_specs=pl.BlockSpec((None, tr, cols), lambda i, k_ref: (k_ref[0], i, 0))),
        out_shape=jax.ShapeDtypeStruct((N_CHIPS, rows, cols), BF),
        compiler_params=_cp(("parallel",)),
    )(kidx, w, dep)


def _adamw_math(w, g, m, v):
    m = ADAM_B1 * m + (1.0 - ADAM_B1) * g
    v = ADAM_B2 * v + (1.0 - ADAM_B2) * (g * g)
    m_hat = m / (1.0 - ADAM_B1 ** ADAM_STEP)
    v_hat = v / (1.0 - ADAM_B2 ** ADAM_STEP)
    delta = -ADAM_LR * (m_hat / (jnp.sqrt(v_hat) + ADAM_EPS) + ADAM_WD * w)
    return delta, m, v


def _adamw(w, m, v, g, name, dep=None):
    rows, cols = w.shape
    tr = min(rows, 256)

    def body(w_ref, m_ref, v_ref, g_ref, go_ref, d_ref, mo_ref, vo_ref):
        gg = g_ref[...]
        d, mm, vv = _adamw_math(w_ref[...], gg, m_ref[...], v_ref[...])
        go_ref[...] = gg
        d_ref[...] = d
        mo_ref[...] = mm
        vo_ref[...] = vv

    spec = pl.BlockSpec((tr, cols), lambda i: (i, 0))
    shp = jax.ShapeDtypeStruct((rows, cols), F32)
    body, in_specs, args = _add_dep(body, [spec] * 4, [w, m, v, g], dep)
    return pl.pallas_call(
        body, name=name, grid=(rows // tr,),
        in_specs=in_specs, out_specs=[spec] * 4, out_shape=[shp] * 4,
        compiler_params=_cp(("parallel",)),
    )(*args)


def _place():
    x, y, c = lax.axis_index("x"), lax.axis_index("y"), lax.axis_index("c")
    return x, y, c


HBM = pl.BlockSpec(memory_space=pltpu.HBM)
SEM = pl.BlockSpec(memory_space=pltpu.SEMAPHORE)
EFFECT = pltpu.SideEffectType.DATAFLOW_SIDE_EFFECTING


def _add_dep(body, in_specs, args, dep):
    if dep is None:
        return body, list(in_specs), list(args)
    n = len(args)

    def wrapped(*refs):
        body(*refs[:n], *refs[n + 1:])

    return wrapped, list(in_specs) + [ANY], list(args) + [dep]


def _split_call(name, bufs, run, old=None, after=None, new=0):
    nb = len(bufs)
    n_old = 2 if old is not None else 0
    n_in = nb + n_old + (1 if after is not None else 0)

    def body(*refs):
        old_sems = (refs[nb], refs[nb + 1]) if old is not None else None
        new_sems = (refs[n_in], refs[n_in + 1]) if new else None
        run(refs[:nb], old_sems, new_sems)
        if new:
            refs[-1][...] = jnp.zeros_like(refs[-1])

    in_specs = [HBM] * nb + [SEM] * n_old + ([ANY] if after is not None else [])
    out_shape = [pltpu.SemaphoreType.DMA((new,))] * 2 if new else []
    out_specs = [SEM, SEM] if new else []
    out_shape += [pltpu.HBM(b.shape, b.dtype) for b in bufs]
    out_specs += [HBM] * nb
    if new:
        out_shape.append(jax.ShapeDtypeStruct((8, 128), F32))
        out_specs.append(pl.BlockSpec(memory_space=pltpu.VMEM))
    first = 2 if new else 0
    args = [pltpu.with_memory_space_constraint(b, pltpu.HBM) for b in bufs]
    if old is not None:
        args += [old[0], old[1]]
    if after is not None:
        args.append(after)
    outs = pl.pallas_call(
        body, name=name, in_specs=tuple(in_specs), out_specs=tuple(out_specs), out_shape=tuple(out_shape),
        input_output_aliases={i: i + first for i in range(nb)},
        compiler_params=pltpu.CompilerParams(has_side_effects=EFFECT),
    )(*args)
    thru = list(outs[first:first + nb])
    if new:
        return thru, (outs[0], outs[1]), outs[-1]
    return thru, None, None


def _remote(part_src, part_dst, sems, i, to):
    return pltpu.make_async_remote_copy(src_ref=part_src, dst_ref=part_dst, send_sem=sems[0].at[i],
                                        recv_sem=sems[1].at[i], device_id=to, device_id_type=MESH)


def _other_chips(x, y):
    return [(1 - x, y), (x, 1 - y), (1 - x, 1 - y)]


def _gather_start(name, bufs, after=None):
    def run(refs, old, new):
        x, y, c = _place()
        k = 2 * x + y
        for w, ref in enumerate(refs):
            rh = bufs[w].shape[1] // 2
            part = ref.at[k, pl.ds(c * rh, rh)]
            for j, (cx, cy) in enumerate(_other_chips(x, y)):
                _remote(part, part, new, 3 * w + j, (cx, cy, c)).start()

    return _split_call(name, bufs, run, after=after, new=3 * len(bufs))


def _gather_pass(name, bufs, sems, after):
    def run(refs, old, new):
        x, y, c = _place()
        k = 2 * x + y
        sib = (x, y, 1 - c)
        for w, ref in enumerate(refs):
            rh = bufs[w].shape[1] // 2
            for j, (cx, cy) in enumerate(_other_chips(x, y)):
                land = ref.at[2 * cx + cy, pl.ds(c * rh, rh)]
                _remote(land, land, old, 3 * w + j, (cx, cy, c)).wait_recv()
                _remote(land, land, new, 3 * w + j, sib).start()
        for w, ref in enumerate(refs):
            rh = bufs[w].shape[1] // 2
            part = ref.at[k, pl.ds(c * rh, rh)]
            for j, (cx, cy) in enumerate(_other_chips(x, y)):
                _remote(part, part, old, 3 * w + j, (cx, cy, c)).wait_send()

    return _split_call(name, bufs, run, old=sems, after=after, new=3 * len(bufs))


def _gather_finish(name, bufs, sems, after):
    def run(refs, old, new):
        x, y, c = _place()
        sib = (x, y, 1 - c)
        for w, ref in enumerate(refs):
            rh = bufs[w].shape[1] // 2
            for j, (cx, cy) in enumerate(_other_chips(x, y)):
                sent = ref.at[2 * cx + cy, pl.ds(c * rh, rh)]
                _remote(sent, sent, old, 3 * w + j, sib).wait_send()
                land = ref.at[2 * cx + cy, pl.ds((1 - c) * rh, rh)]
                _remote(land, land, old, 3 * w + j, sib).wait_recv()

    return _split_call(name, bufs, run, old=sems, after=after)[0]


def _pair_send_start(name, grads):
    n = len(grads)
    lands = [lax.empty((N_CHIPS,) + g.shape[2:], g.dtype) for g in grads]

    def run(refs, old, new):
        x, y, c = _place()
        for w in range(n):
            _remote(refs[w].at[:, 1 - c], refs[n + w], new, w, (x, y, 1 - c)).start()

    return _split_call(name, list(grads) + lands, run, new=n)


def _pair_send_wait(name, bufs, sems, after):
    n = len(bufs) // 2

    def run(refs, old, new):
        x, y, c = _place()
        for w in range(n):
            cp = _remote(refs[w].at[:, 1 - c], refs[n + w], old, w, (x, y, 1 - c))
            cp.wait_send()
            cp.wait_recv()

    thru = _split_call(name, bufs, run, old=sems, after=after)[0]
    return thru[:n], thru[n:]


def _pair_add(g, recv, cidx, name):
    _, _, rh, cols = g.shape
    tr = min(rh, 256)

    def body(c_ref, g_ref, r_ref, o_ref):
        o_ref[...] = (g_ref[...].astype(F32) + r_ref[...].astype(F32)).astype(BF)

    return pl.pallas_call(
        body, name=name,
        grid_spec=pltpu.PrefetchScalarGridSpec(
            num_scalar_prefetch=1, grid=(rh // tr,),
            in_specs=[pl.BlockSpec((N_CHIPS, None, tr, cols), lambda i, c_ref: (0, c_ref[0], i, 0)),
                      pl.BlockSpec((N_CHIPS, tr, cols), lambda i, c_ref: (0, i, 0))],
            out_specs=pl.BlockSpec((N_CHIPS, tr, cols), lambda i, c_ref: (0, i, 0))),
        out_shape=jax.ShapeDtypeStruct((N_CHIPS, rh, cols), BF),
        compiler_params=_cp(("parallel",)),
    )(cidx, g, recv)


def _chip_send_start(name, sums):
    n = len(sums)
    lands = [lax.empty((3,) + s.shape[1:], BF) for s in sums]

    def run(refs, old, new):
        x, y, c = _place()
        for w in range(n):
            for j, (cx, cy) in enumerate(_other_chips(x, y)):
                _remote(refs[w].at[2 * cx + cy], refs[n + w].at[j], new, 3 * w + j, (cx, cy, c)).start()

    return _split_call(name, list(sums) + lands, run, new=3 * n)


def _chip_send_wait(name, bufs, sems, after):
    n = len(bufs) // 2

    def run(refs, old, new):
        x, y, c = _place()
        for w in range(n):
            for j, (cx, cy) in enumerate(_other_chips(x, y)):
                cp = _remote(refs[w].at[2 * cx + cy], refs[n + w].at[j], old, 3 * w + j, (cx, cy, c))
                cp.wait_send()
                cp.wait_recv()

    thru = _split_call(name, bufs, run, old=sems, after=after)[0]
    return thru[:n], thru[n:]


def _chip_add(own, parts, kidx, cidx, name):
    _, rh, cols = parts.shape
    tr = min(rh, 512)

    def body(k_ref, c_ref, own_ref, p_ref, o_ref):
        acc = own_ref[...].astype(F32)
        for s in range(3):
            acc = acc + p_ref[s].astype(F32)
        o_ref[...] = acc

    return pl.pallas_call(
        body, name=name,
        grid_spec=pltpu.PrefetchScalarGridSpec(
            num_scalar_prefetch=2, grid=(rh // tr,),
            in_specs=[pl.BlockSpec((None, tr, cols), lambda i, k_ref, c_ref: (k_ref[0], i, 0)),
                      pl.BlockSpec((3, tr, cols), lambda i, k_ref, c_ref: (0, i, 0))],
            out_specs=pl.BlockSpec((None, tr, cols), lambda i, k_ref, c_ref: (c_ref[0], i, 0))),
        out_shape=jax.ShapeDtypeStruct((2, rh, cols), F32),
        compiler_params=_cp(("parallel",)),
    )(kidx, cidx, own, parts)


def _pair_swap_start(name, bufs):
    def run(refs, old, new):
        x, y, c = _place()
        for w, ref in enumerate(refs):
            _remote(ref.at[c], ref.at[c], new, w, (x, y, 1 - c)).start()

    return _split_call(name, bufs, run, new=len(bufs))


def _pair_swap_wait(name, bufs, sems, after):
    def run(refs, old, new):
        x, y, c = _place()
        for w, ref in enumerate(refs):
            _remote(ref.at[c], ref.at[c], old, w, (x, y, 1 - c)).wait_send()
            _remote(ref.at[1 - c], ref.at[1 - c], old, w, (x, y, 1 - c)).wait_recv()

    return _split_call(name, bufs, run, old=sems, after=after)[0]


def _peers(x, y, c):
    out = []
    for code in range(1, 8):
        fx, fy, fc = (code >> 2) & 1, (code >> 1) & 1, code & 1
        px = 1 - x if fx else x
        py = 1 - y if fy else y
        pc = 1 - c if fc else c
        out.append((code, (px, py, pc)))
    return out


def _mod_exchange(c_row, ada_w, ada_b4, deps):
    ncol = ada_w.shape[1]

    def body(c_ref, w_ref, b_ref, *rest):
        call_ref, mod_ref, part_ref, send_sems, recv_sems = rest[len(deps):]
        x, y, c = _place()
        k = 2 * x + y
        me = 4 * x + 2 * y + c
        call_ref[pl.ds(me, 1), :] = c_ref[...]
        sends = []
        for code, peer in _peers(x, y, c):
            cp = pltpu.make_async_remote_copy(
                src_ref=c_ref, dst_ref=call_ref.at[pl.ds(me, 1), :],
                send_sem=send_sems.at[code], recv_sem=recv_sems.at[code],
                device_id=peer, device_id_type=MESH)
            cp.start()
            sends.append(cp)
        for code, (px, py, pc) in _peers(x, y, c):
            land = call_ref.at[pl.ds(4 * px + 2 * py + pc, 1), :]
            pltpu.make_async_remote_copy(
                src_ref=land, dst_ref=land, send_sem=send_sems.at[code], recv_sem=recv_sems.at[code],
                device_id=(px, py, pc), device_id_type=MESH).wait_recv()
        call = call_ref[...]
        act = call * _sigmoid(call)
        part = jnp.dot(act, w_ref[...], preferred_element_type=F32,
                       precision=lax.Precision.HIGHEST) + b_ref[pl.ds(k, 1), :]
        part_ref[...] = part
        mod_ref[pl.ds(k, 1), :] = part_ref[pl.ds(me, 1), :]
        chips = [(8 + j, peer) for j, (code, peer) in enumerate(_peers(x, y, c)) if code in (2, 4, 6)]
        for slot, (px, py, pc) in chips:
            cp = pltpu.make_async_remote_copy(
                src_ref=part_ref.at[pl.ds(4 * px + 2 * py + pc, 1), :], dst_ref=mod_ref.at[pl.ds(k, 1), :],
                send_sem=send_sems.at[slot], recv_sem=recv_sems.at[slot],
                device_id=(px, py, pc), device_id_type=MESH)
            cp.start()
            sends.append(cp)
        for slot, (px, py, pc) in chips:
            land = mod_ref.at[pl.ds(2 * px + py, 1), :]
            pltpu.make_async_remote_copy(
                src_ref=land, dst_ref=land, send_sem=send_sems.at[slot], recv_sem=recv_sems.at[slot],
                device_id=(px, py, pc), device_id_type=MESH).wait_recv()
        for cp in sends:
            cp.wait_send()

    vm = pl.BlockSpec(memory_space=pltpu.VMEM)
    return pl.pallas_call(
        body, name="mod_exchange",
        in_specs=[vm, vm, vm] + [ANY] * len(deps), out_specs=[vm, vm],
        out_shape=[jax.ShapeDtypeStruct((8, D_MODEL), F32), jax.ShapeDtypeStruct((N_CHIPS, ncol), F32)],
        scratch_shapes=[pltpu.VMEM((8, ncol), F32), pltpu.SemaphoreType.DMA((16,)),
                        pltpu.SemaphoreType.DMA((16,))],
        compiler_params=_cp(),
    )(c_row, ada_w, ada_b4, *deps)


def _small_exchange(stats, rows, c_all, wsm, msm, vsm):
    ncol = 6 * D_MODEL // N_CHIPS
    ns = len(stats)

    def body(*refs):
        call_ref, w_ref, m_ref, v_ref, gw_ref, loss_ref = refs[ns:ns + 6]
        outs = refs[ns + 6:ns + 30]
        p_ref, g_ref, all_ref, dm_ref, send_sems, recv_sems = refs[ns + 30:]
        x, y, c = _place()
        k = 2 * x + y
        me = 4 * x + 2 * y + c
        for r, (tab, row) in enumerate(rows):
            p_ref[r] = refs[tab][row:row + 1, :]
        all_ref[:, pl.ds(me, 1), :] = p_ref[...]
        sends = []
        for code, peer in _peers(x, y, c):
            cp = pltpu.make_async_remote_copy(
                src_ref=p_ref, dst_ref=all_ref.at[:, pl.ds(me, 1), :],
                send_sem=send_sems.at[code], recv_sem=recv_sems.at[code],
                device_id=peer, device_id_type=MESH)
            cp.start()
            sends.append(cp)
        for code, (px, py, pc) in _peers(x, y, c):
            land = all_ref.at[:, pl.ds(4 * px + 2 * py + pc, 1), :]
            pltpu.make_async_remote_copy(
                src_ref=land, dst_ref=land, send_sem=send_sems.at[code], recv_sem=recv_sems.at[code],
                device_id=(px, py, pc), device_id_type=MESH).wait_recv()
        for cp in sends:
            cp.wait_send()
        tot = [_colsum(all_ref[r]) for r in range(N_PAY)]
        loss_ref[...] = jnp.sum(tot[P_LOSS], axis=1, keepdims=True)
        g_ref[...] = jnp.zeros_like(g_ref)
        for r in range(P_LOSS):
            g_ref[r:r + 1, :] = tot[r]
        g = g_ref[...]
        for kind, tab in enumerate((g,) + _adamw_math(w_ref[...], g, m_ref[...], v_ref[...])):
            for r in range(6):
                outs[6 * kind][:, r * D_MODEL:(r + 1) * D_MODEL] = tab[r:r + 1, :]
            for i in range(5):
                outs[6 * kind + 1 + i][...] = tab[6 + i:7 + i, :]
        half = D_MODEL // 2
        for kk in range(N_CHIPS):
            @pl.when(k == kk)
            def _():
                r0 = 3 * (kk // 2)
                if kk % 2 == 0:
                    dm_ref[:, :D_MODEL] = all_ref[r0]
                    dm_ref[:, D_MODEL:] = all_ref[r0 + 1][:, :half]
                else:
                    dm_ref[:, :half] = all_ref[r0 + 1][:, half:]
                    dm_ref[:, half:] = all_ref[r0 + 2]
        call = call_ref[...]
        act = call * _sigmoid(call)
        gw_ref[...] = lax.dot_general(act, dm_ref[...], (((0,), (0,)), ((), ())),
                                      preferred_element_type=F32, precision=lax.Precision.HIGHEST)

    vm = pl.BlockSpec(memory_space=pltpu.VMEM)
    vectors = [jax.ShapeDtypeStruct((1, 6 * D_MODEL), F32)] + [jax.ShapeDtypeStruct((1, D_MODEL), F32)] * 5
    outs = pl.pallas_call(
        body, name="small_exchange",
        in_specs=[vm] * (ns + 4), out_specs=[vm] * 26,
        out_shape=[jax.ShapeDtypeStruct((D_MODEL, ncol), F32), jax.ShapeDtypeStruct((1, 1), F32)] + vectors * 4,
        scratch_shapes=[pltpu.VMEM((N_PAY, 1, D_MODEL), F32), pltpu.VMEM((16, D_MODEL), F32),
                        pltpu.VMEM((N_PAY, 8, D_MODEL), F32), pltpu.VMEM((8, ncol), F32),
                        pltpu.SemaphoreType.DMA((8,)), pltpu.SemaphoreType.DMA((8,))],
        compiler_params=_cp(),
    )(*stats, c_all, wsm, msm, vsm)
    return outs[0], outs[1], [outs[2 + 6 * kind:8 + 6 * kind] for kind in range(4)]


def _rope_tables(pos_col, inv_freq, sign, dep=None):
    def body(p_ref, f_ref, s_ref, cos_ref, sin_ref):
        ang = p_ref[...].astype(F32) * f_ref[...]
        cos_ref[...] = jnp.cos(ang)
        sin_ref[...] = jnp.sin(ang) * s_ref[...]

    tr = 512
    shp = jax.ShapeDtypeStruct((SEQ, 128), F32)
    body, in_specs, args = _add_dep(
        body, [pl.BlockSpec((tr, 1), lambda i: (i, 0)), pl.BlockSpec((1, 128), lambda i: (0, 0)),
               pl.BlockSpec((1, 128), lambda i: (0, 0))], [pos_col, inv_freq, sign], dep)
    return pl.pallas_call(
        body, name="rope_tables", grid=(SEQ // tr,),
        in_specs=in_specs,
        out_specs=[pl.BlockSpec((tr, 128), lambda i: (i, 0))] * 2, out_shape=[shp, shp],
        compiler_params=_cp(("parallel",)),
    )(*args)


def _resident(shape):
    nd = len(shape)
    return pl.BlockSpec(shape, lambda *_: (0,) * nd, pipeline_mode=pl.Buffered(1))


def _ln_proj(x, vecs, w_in4):
    tm = 256
    wc = w_in4.shape[2]

    def body(x_ref, vec_ref, w_ref, h_ref, proj_ref):
        xx = x_ref[...]
        g = _row(vec_ref, V_G1) * (1.0 + _row(vec_ref, V_SC1))
        h = (xx * _rms(xx) * g + _row(vec_ref, V_SH1)).astype(BF)
        h_ref[...] = h
        for j in range(N_CHIPS):
            proj_ref[:, j * wc:(j + 1) * wc] = _dot(h, w_ref[j]).astype(BF)

    return pl.pallas_call(
        body, name="ln_proj", grid=(SEQ // tm,),
        in_specs=[pl.BlockSpec((tm, D_MODEL), lambda i: (i, 0)), _resident((16, D_MODEL)),
                  _resident(w_in4.shape)],
        out_specs=[pl.BlockSpec((tm, D_MODEL), lambda i: (i, 0)), pl.BlockSpec((tm, D_IN), lambda i: (i, 0))],
        out_shape=[jax.ShapeDtypeStruct((SEQ, D_MODEL), BF), jax.ShapeDtypeStruct((SEQ, D_IN), BF)],
        compiler_params=_cp(("parallel",)),
    )(x, vecs, w_in4)


def _lane_first(shape):
    lane = lax.broadcasted_iota(jnp.int32, shape, 1)
    return (lane & 32) == 0


def _rot(v, cos, sin_s):
    partner = jnp.where(_lane_first(v.shape), pltpu.roll(v, 96, 1), pltpu.roll(v, 32, 1))
    return v * cos + partner * sin_s


def _rot_t(dv, cos, sin_s):
    t = dv * sin_s
    partner = jnp.where(_lane_first(dv.shape), pltpu.roll(t, 96, 1), pltpu.roll(t, 32, 1))
    return dv * cos + partner


def _ret_mask(lg):
    t = RET_BLOCK
    ii = lax.broadcasted_iota(jnp.int32, (t, t), 0)
    jj = lax.broadcasted_iota(jnp.int32, (t, t), 1)
    dist = jnp.abs(ii - jj).astype(F32)
    future = (jj >> RET_CHUNK_SHIFT) > (ii >> RET_CHUNK_SHIFT)
    return jnp.where(future, 0.0, jnp.exp(lg * dist))


def _ret_masks(lg, mask_ref, head):
    t = RET_BLOCK
    mask = mask_ref[head]
    ti = lax.broadcasted_iota(jnp.int32, (t, 1), 0).astype(F32)
    from_start = jnp.exp(lg * (ti + 1.0))
    to_end = jnp.exp(lg * (t - 1.0 - ti))
    whole = jnp.exp(jnp.full((1, 128), lg * t, F32))
    return mask, from_start, to_end, whole


def _head_lanes(shape, hh):
    lane = lax.broadcasted_iota(jnp.int32, shape, 1)
    return (lane >> 6) == hh


def _ret_specs():
    t = RET_BLOCK
    return dict(
        q=lambda f: pl.BlockSpec((t, 512), lambda n: (f(n), C_QR // 512)),
        k=lambda f: pl.BlockSpec((t, 512), lambda n: (f(n), C_KR // 512)),
        v=lambda f: pl.BlockSpec((t, D_MODEL), lambda n: (f(n), C_VR // D_MODEL)),
        g=lambda f: pl.BlockSpec((t, D_MODEL), lambda n: (f(n), C_GR // D_MODEL)),
        tab=lambda f: pl.BlockSpec((t, 128), lambda n: (f(n), 0)),
        wide=lambda f: pl.BlockSpec((t, D_MODEL), lambda n: (f(n), 0)),
        state=lambda f: pl.BlockSpec((N_PAIRS, None, 2, 128, 128), lambda n: (0, f(n), 0, 0, 0)),
    )


def _ret_fwd(proj, cos, sin_s, gn_g, log_gamma, dep=None):
    t = RET_BLOCK
    nb = SEQ // t

    def body(lg_ref, q_ref, k_ref, v_ref, g_ref, cos_ref, sin_ref, gn_ref, o_ref, retg_ref, st_ref, state, masks):
        @pl.when(pl.program_id(0) == 0)
        def _():
            state[...] = jnp.zeros_like(state)
            for head in range(2 * N_PAIRS):
                masks[head] = _ret_mask(lg_ref[head])

        cos, sn = cos_ref[...], sin_ref[...]
        for p in range(N_PAIRS):
            q = _rot(q_ref[:, 128 * p:128 * (p + 1)].astype(F32), cos, sn)
            k = _rot(k_ref[:, 128 * p:128 * (p + 1)].astype(F32), cos, sn) * QK_SCALE
            for hh in range(2):
                cols = slice(256 * p + 128 * hh, 256 * p + 128 * (hh + 1))
                lg = lg_ref[2 * p + hh]
                mask, from_start, to_end, whole = _ret_masks(lg, masks, 2 * p + hh)
                lanes = _head_lanes(q.shape, hh)
                qm = jnp.where(lanes, q, 0.0)
                km = jnp.where(lanes, k, 0.0)
                vh = v_ref[:, cols]
                sc = _dot_nt(qm.astype(BF), km.astype(BF)) * mask
                st = state[p, hh]
                st_ref[p, hh] = st
                o = _dot(sc.astype(BF), vh) + _dot((qm * from_start).astype(BF), st.astype(BF))
                state[p, hh] = whole * st + _dot_tn((km * to_end).astype(BF), vh)
                d = o - _rowmean(o)
                nh = d * lax.rsqrt(_rowmean(d * d) + EPS)
                gr = g_ref[:, cols].astype(F32)
                o_ref[:, cols] = o
                retg_ref[:, cols] = (gr * _sigmoid(gr) * nh * gn_ref[:, cols]).astype(BF)

    sp = _ret_specs()
    ident = lambda n: n
    body, in_specs, args = _add_dep(
        body, [pl.BlockSpec(memory_space=pltpu.SMEM), sp["q"](ident), sp["k"](ident), sp["v"](ident),
               sp["g"](ident), sp["tab"](ident), sp["tab"](ident), _resident((1, D_MODEL))],
        [log_gamma, proj, proj, proj, proj, cos, sin_s, gn_g], dep)
    return pl.pallas_call(
        body, name="ret_fwd", grid=(nb,),
        in_specs=in_specs,
        out_specs=[sp["wide"](ident), sp["wide"](ident), sp["state"](ident)],
        out_shape=[jax.ShapeDtypeStruct((SEQ, D_MODEL), F32), jax.ShapeDtypeStruct((SEQ, D_MODEL), BF),
                   jax.ShapeDtypeStruct((N_PAIRS, nb, 2, 128, 128), F32)],
        scratch_shapes=[pltpu.VMEM((N_PAIRS, 2, 128, 128), F32),
                        pltpu.VMEM((2 * N_PAIRS, RET_BLOCK, RET_BLOCK), F32)],
        compiler_params=_cp(("arbitrary",)),
    )(*args)


def _ret_bwd(proj, cos, sin_s, dret, states, log_gamma, dep=None):
    t = RET_BLOCK
    nb = SEQ // t

    def body(lg_ref, q_ref, k_ref, v_ref, cos_ref, sin_ref, do_ref, st_ref, dqkv_ref, dstate, masks):
        @pl.when(pl.program_id(0) == 0)
        def _():
            dstate[...] = jnp.zeros_like(dstate)
            for head in range(2 * N_PAIRS):
                masks[head] = _ret_mask(lg_ref[head])

        cos, sn = cos_ref[...], sin_ref[...]
        for p in range(N_PAIRS):
            q = _rot(q_ref[:, 128 * p:128 * (p + 1)].astype(F32), cos, sn)
            k = _rot(k_ref[:, 128 * p:128 * (p + 1)].astype(F32), cos, sn) * QK_SCALE
            dq_rot = jnp.zeros(q.shape, F32)
            dk_rot = jnp.zeros(q.shape, F32)
            for hh in range(2):
                cols = slice(256 * p + 128 * hh, 256 * p + 128 * (hh + 1))
                lg = lg_ref[2 * p + hh]
                mask, from_start, to_end, whole = _ret_masks(lg, masks, 2 * p + hh)
                lanes = _head_lanes(q.shape, hh)
                qm = jnp.where(lanes, q, 0.0)
                km = jnp.where(lanes, k, 0.0)
                qb, kb = qm.astype(BF), km.astype(BF)
                vh = v_ref[:, cols]
                do = do_ref[:, cols]
                sc = (_dot_nt(qb, kb) * mask).astype(BF)
                st = st_ref[p, hh].astype(BF)
                dst = dstate[p, hh]
                dstb = dst.astype(BF)
                k_end = (km * to_end).astype(BF)
                q_start = (qm * from_start).astype(BF)
                dqkv_ref[:, C_VR + 256 * p + 128 * hh:C_VR + 256 * p + 128 * (hh + 1)] = (
                    _dot_tn(sc, do) + _dot(k_end, dstb)).astype(BF)
                dsc = (_dot_nt(do, vh) * mask).astype(BF)
                dq_h = _dot(dsc, kb) + _dot_nt(do, st) * from_start
                dq_rot = dq_rot + jnp.where(lanes, dq_h, 0.0)
                dk_rot = dk_rot + _dot_tn(dsc, qb) + _dot_nt(vh, dstb) * to_end
                dstate[p, hh] = whole * dst + _dot_tn(q_start, do)
            dqkv_ref[:, C_QR + 128 * p:C_QR + 128 * (p + 1)] = _rot_t(dq_rot, cos, sn).astype(BF)
            dqkv_ref[:, C_KR + 128 * p:C_KR + 128 * (p + 1)] = _rot_t(dk_rot * QK_SCALE, cos, sn).astype(BF)

    sp = _ret_specs()
    rev = lambda n: nb - 1 - n
    body, in_specs, args = _add_dep(
        body, [pl.BlockSpec(memory_space=pltpu.SMEM), sp["q"](rev), sp["k"](rev), sp["v"](rev),
               sp["tab"](rev), sp["tab"](rev), sp["wide"](rev), sp["state"](rev)],
        [log_gamma, proj, proj, proj, cos, sin_s, dret, states], dep)
    return pl.pallas_call(
        body, name="ret_bwd", grid=(nb,),
        in_specs=in_specs,
        out_specs=pl.BlockSpec((t, C_GR), lambda n: (rev(n), 0)),
        out_shape=jax.ShapeDtypeStruct((SEQ, C_GR), BF),
        scratch_shapes=[pltpu.VMEM((N_PAIRS, 2, 128, 128), F32),
                        pltpu.VMEM((2 * N_PAIRS, RET_BLOCK, RET_BLOCK), F32)],
        compiler_params=_cp(("arbitrary",)),
    )(*args)


def _stack_heads(v):
    return jnp.concatenate([jnp.where(_head_lanes(v.shape, hh), v, jnp.zeros_like(v)) for hh in range(2)], axis=0)


def _unstack_heads(v):
    t = v.shape[0] // 2
    return jnp.where(_head_lanes((t, v.shape[1]), 0), v[:t], v[t:])


def _sb_masks(t, heads):
    rr = lax.broadcasted_iota(jnp.int32, (t, t), 0)
    cc = lax.broadcasted_iota(jnp.int32, (t, t), 1)
    r2 = lax.broadcasted_iota(jnp.int32, (heads * t, t), 0) & (t - 1)
    c2 = lax.broadcasted_iota(jnp.int32, (heads * t, t), 1)
    return rr, cc, c2 < r2


def _split_dot2(v, tri):
    return _dot(v.astype(BF), tri)


def _log_one_minus_beta(z):
    return -(jnp.maximum(z, 0.0) + jnp.log(1.0 + jnp.exp(-jnp.abs(z))))


def _sb_fwd(proj):
    t, g = SB_BLOCK, SB_GROUP
    nq = SEQ // t
    rows = 2 * g * t

    def body(q_ref, k_ref, v_ref, o_ref, tot_ref, kt_ref):
        i = pl.program_id(1)

        @pl.when(i == 0)
        def _():
            for p in range(g):
                for jj in range(nq):
                    kt_ref[p, jj] = k_ref[jj * t:(jj + 1) * t, 128 * p:128 * (p + 1)].T

        q2 = [_stack_heads((q_ref[:, 128 * p:128 * (p + 1)].astype(F32) * QK_SCALE).astype(BF)) for p in range(g)]
        rr, cc, valid = _sb_masks(t, 2 * g)
        later = (rr > cc).astype(BF)

        def tile(j, carry, diagonal):
            acc, run = carry
            z = jnp.concatenate([_dot(q2[p], kt_ref[p, j]) for p in range(g)], axis=0)
            lm = _log_one_minus_beta(z)
            if diagonal:
                lm = jnp.where(valid, lm, 0.0)
            after = _split_dot2(lm, later)
            a = jnp.exp(z + lm + after + run)
            if diagonal:
                a = jnp.where(valid, a, 0.0)
            ab = a.astype(BF)
            keys = pl.ds(pl.multiple_of(j * t, t), t)
            av = jnp.concatenate([_dot(ab[2 * t * p:2 * t * (p + 1)], v_ref[keys, 128 * p:128 * (p + 1)])
                                  for p in range(g)], axis=0)
            return acc + av, run + after[:, 0:1] + lm[:, 0:1]

        carry = tile(i, (jnp.zeros((rows, 128), F32), jnp.zeros((rows, 1), F32)), True)
        acc, run = lax.fori_loop(0, i, lambda s, cr: tile(i - 1 - s, cr, False), carry)
        run = jnp.broadcast_to(run, (rows, 128))
        for p in range(g):
            o_ref[:, 128 * p:128 * (p + 1)] = _unstack_heads(acc[2 * t * p:2 * t * (p + 1)]).astype(BF)
            tot_ref[:, 128 * p:128 * (p + 1)] = _unstack_heads(run[2 * t * p:2 * t * (p + 1)])

    w = 128 * g
    return pl.pallas_call(
        body, name="sb_fwd", grid=(N_PAIRS // g, nq),
        in_specs=[pl.BlockSpec((t, w), lambda p, i: (i, C_QS // w + p)),
                  pl.BlockSpec((SEQ, w), lambda p, i: (0, C_KS // w + p)),
                  pl.BlockSpec((SEQ, w), lambda p, i: (0, C_VS // w + p))],
        out_specs=[pl.BlockSpec((t, w), lambda p, i: (i, p))] * 2,
        out_shape=[jax.ShapeDtypeStruct((SEQ, 512), BF), jax.ShapeDtypeStruct((SEQ, 512), F32)],
        scratch_shapes=[pltpu.VMEM((g, nq, 128, t), BF)],
        compiler_params=_cp(("parallel", "arbitrary")),
    )(proj, proj, proj)


def _sb_bwd(proj, dsb, tot, dep=None):
    t, g = SB_BLOCK, SB_GROUP
    nq = SEQ // t
    rows = 2 * g * t

    def body(q_ref, k_ref, v_ref, do_ref, tot_ref, dq_ref, dk_ref, dv_ref, kt_ref, vt_ref, dkt_acc, dvt_acc):
        i = pl.program_id(1)

        @pl.when(i == 0)
        def _():
            dkt_acc[...] = jnp.zeros_like(dkt_acc)
            dvt_acc[...] = jnp.zeros_like(dvt_acc)
            for p in range(g):
                for jj in range(nq):
                    kt_ref[p, jj] = k_ref[jj * t:(jj + 1) * t, 128 * p:128 * (p + 1)].T
                    vt_ref[p, jj] = v_ref[jj * t:(jj + 1) * t, 128 * p:128 * (p + 1)].T

        q2 = [_stack_heads((q_ref[:, 128 * p:128 * (p + 1)].astype(F32) * QK_SCALE).astype(BF)) for p in range(g)]
        do2 = [_stack_heads(do_ref[:, 128 * p:128 * (p + 1)]) for p in range(g)]
        q2t = [v.T for v in q2]
        do2t = [v.T for v in do2]
        tots = tot_ref[...]
        total = jnp.concatenate([tots[:, 64 * h:64 * h + 1] for h in range(2 * g)], axis=0)
        rr, cc, valid = _sb_masks(t, 2 * g)
        upto = (rr <= cc).astype(BF)
        before = (rr < cc).astype(BF)

        def part(v, p):
            return v[2 * t * p:2 * t * (p + 1)]

        def tile(j, carry, diagonal):
            dq, run_l, run_g = carry
            z = jnp.concatenate([_dot(q2[p], kt_ref[p, j]) for p in range(g)], axis=0)
            lm = _log_one_minus_beta(z)
            if diagonal:
                lm = jnp.where(valid, lm, 0.0)
            incl = _split_dot2(lm, upto)
            a = jnp.exp(z + lm + (total - (incl + run_l)))
            if diagonal:
                a = jnp.where(valid, a, 0.0)
            gg = a * jnp.concatenate([_dot(do2[p], vt_ref[p, j]) for p in range(g)], axis=0)
            excl = _split_dot2(gg, before)
            dz = gg * jnp.exp(lm) - (excl + run_g) * jnp.exp(z + lm)
            if diagonal:
                dz = jnp.where(valid, dz, 0.0)
            dzb = dz.astype(BF)
            ab = a.astype(BF)
            keys = pl.ds(pl.multiple_of(j * t, t), t)
            for p in range(g):
                dkt_acc[p, j] += _dot(q2t[p], part(dzb, p))
                dvt_acc[p, j] += _dot(do2t[p], part(ab, p))
            dq_t = jnp.concatenate([_dot(part(dzb, p), k_ref[keys, 128 * p:128 * (p + 1)]) for p in range(g)], axis=0)
            return (dq + dq_t, run_l + incl[:, t - 1:t], run_g + excl[:, t - 1:t] + gg[:, t - 1:t])

        zero = jnp.zeros((rows, 1), F32)
        carry = lax.fori_loop(0, i, lambda j, cr: tile(j, cr, False), (jnp.zeros((rows, 128), F32), zero, zero))
        dq = tile(i, carry, True)[0]
        for p in range(g):
            dq_ref[:, 128 * p:128 * (p + 1)] = (_unstack_heads(part(dq, p)) * QK_SCALE).astype(BF)

        @pl.when(i == nq - 1)
        def _():
            for p in range(g):
                for jj in range(nq):
                    dk_ref[jj * t:(jj + 1) * t, 128 * p:128 * (p + 1)] = dkt_acc[p, jj].T.astype(BF)
                    dv_ref[jj * t:(jj + 1) * t, 128 * p:128 * (p + 1)] = dvt_acc[p, jj].T.astype(BF)

    w = 128 * g
    tile_spec = pl.BlockSpec((t, w), lambda p, i: (i, p))
    col_spec = pl.BlockSpec((SEQ, w), lambda p, i: (0, p))
    shp = jax.ShapeDtypeStruct((SEQ, 512), BF)
    body, in_specs, args = _add_dep(
        body, [pl.BlockSpec((t, w), lambda p, i: (i, C_QS // w + p)),
               pl.BlockSpec((SEQ, w), lambda p, i: (0, C_KS // w + p)),
               pl.BlockSpec((SEQ, w), lambda p, i: (0, C_VS // w + p)),
               tile_spec, tile_spec],
        [proj, proj, proj, dsb, tot], dep)
    return pl.pallas_call(
        body, name="sb_bwd", grid=(N_PAIRS // g, nq),
        in_specs=in_specs,
        out_specs=[tile_spec, col_spec, col_spec],
        out_shape=[shp, shp, shp],
        scratch_shapes=[pltpu.VMEM((g, nq, 128, t), BF), pltpu.VMEM((g, nq, 128, t), BF),
                        pltpu.VMEM((g, nq, 128, t), F32), pltpu.VMEM((g, nq, 128, t), F32)],
        compiler_params=_cp(("parallel", "arbitrary")),
    )(*args)


def _mix_out(retg, sb, proj, x, vecs, w_ret, w_sb4, w_out):
    tm, half = 256, 512

    def body(r_ref, s_ref, ar0, ar1, as0, as1, x_ref, vec_ref, wr_ref, ws_ref, wo_ref,
             mix_ref, rb_ref, sbp_ref, y_ref, h1_ref, h2_ref):
        rb = _dot(r_ref[...], wr_ref[...])
        sbv = s_ref[...]
        sbp = jnp.concatenate([_dot(sbv, ws_ref[k]) for k in range(N_CHIPS)], axis=1)
        gate_r = _sigmoid(jnp.concatenate([ar0[...], ar1[...]], axis=1).astype(F32))
        gate_s = _sigmoid(jnp.concatenate([as0[...], as1[...]], axis=1).astype(F32))
        mixed = (gate_r * rb + gate_s * sbp).astype(BF)
        mix_ref[...] = mixed
        rb_ref[...] = rb.astype(BF)
        sbp_ref[...] = sbp.astype(BF)
        y = _dot(mixed, wo_ref[...])
        h1 = x_ref[...] + _row(vec_ref, V_GT1) * (y * _rms(y)) * _row(vec_ref, V_G2)
        g = _row(vec_ref, V_G3) * (1.0 + _row(vec_ref, V_SC2))
        y_ref[...] = y
        h1_ref[...] = h1
        h2_ref[...] = (h1 * _rms(h1) * g + _row(vec_ref, V_SH2)).astype(BF)

    row = pl.BlockSpec((tm, D_MODEL), lambda i: (i, 0))
    gate = lambda c0: pl.BlockSpec((tm, half), lambda i: (i, c0 // half))
    bf = jax.ShapeDtypeStruct((SEQ, D_MODEL), BF)
    f32 = jax.ShapeDtypeStruct((SEQ, D_MODEL), F32)
    return pl.pallas_call(
        body, name="mix_out", grid=(SEQ // tm,),
        in_specs=[row, pl.BlockSpec((tm, 512), lambda i: (i, 0)), gate(C_AR), gate(C_AR + half), gate(C_AS),
                  gate(C_AS + half), row, _resident((16, D_MODEL)), _resident((D_MODEL, D_MODEL)),
                  _resident(w_sb4.shape), _resident((D_MODEL, D_MODEL))],
        out_specs=[row] * 6, out_shape=[bf, bf, bf, f32, f32, bf],
        compiler_params=_cp(("parallel",)),
    )(retg, sb, proj, proj, proj, proj, x, vecs, w_ret, w_sb4, w_out)


def _ffn_fwd_loss(h2, h1, target, vecs, w_ff14, w_ff24):
    tm = 256

    def body(h2_ref, h1_ref, t_ref, vec_ref, w1_ref, w2_ref, u_ref, a_ref, dout_ref, df_ref, st_ref):
        @pl.when(pl.program_id(0) == 0)
        def _():
            st_ref[...] = jnp.zeros_like(st_ref)

        hb = h2_ref[...]
        f = jnp.zeros((tm, D_MODEL), F32)
        for k in range(N_CHIPS):
            cols = slice(k * D_MODEL, (k + 1) * D_MODEL)
            u = _dot(hb, w1_ref[k])
            r = jnp.maximum(u, 0.0)
            act = (r * r).astype(BF)
            u_ref[:, cols] = u.astype(BF)
            a_ref[:, cols] = act
            f = f + _dot(act, w2_ref[k])
        r4 = _rms(f)
        fn = f * r4
        gt2, g4 = _row(vec_ref, V_GT2), _row(vec_ref, V_G4)
        diff = h1_ref[...] + gt2 * fn * g4 - t_ref[...]
        dout = diff * (1.0 / D_MODEL)
        dfn = dout * gt2 * g4
        dout_ref[...] = dout
        df_ref[...] = (r4 * (dfn - fn * _rowmean(dfn * fn))).astype(BF)
        st_ref[0:1, :] += _colsum(dout * fn * g4)
        st_ref[1:2, :] += _colsum(dout * gt2 * fn)
        st_ref[2:3, :] += _colsum(diff * diff) * (0.5 / D_MODEL)

    row = pl.BlockSpec((tm, D_MODEL), lambda i: (i, 0))
    wide = pl.BlockSpec((tm, D_FF), lambda i: (i, 0))
    return pl.pallas_call(
        body, name="ffn_fwd_loss", grid=(SEQ // tm,),
        in_specs=[row, row, row, _resident((16, D_MODEL)), _resident(w_ff14.shape), _resident(w_ff24.shape)],
        out_specs=[wide, wide, row, row, pl.BlockSpec((8, D_MODEL), lambda i: (0, 0))],
        out_shape=[jax.ShapeDtypeStruct((SEQ, D_FF), BF), jax.ShapeDtypeStruct((SEQ, D_FF), BF),
                   jax.ShapeDtypeStruct((SEQ, D_MODEL), F32), jax.ShapeDtypeStruct((SEQ, D_MODEL), BF),
                   jax.ShapeDtypeStruct((8, D_MODEL), F32)],
        compiler_params=_cp(("arbitrary",)),
    )(h2, h1, target, vecs, w_ff14, w_ff24)


def _ffn_bwd(df, u, h1, y, dout, vecs, w_ff14, w_ff24):
    tm = 256

    def body(df_ref, u_ref, h1_ref, y_ref, dout_ref, vec_ref, w1_ref, w2_ref, du_ref, dh1_ref, dy_ref, st_ref):
        @pl.when(pl.program_id(0) == 0)
        def _():
            st_ref[...] = jnp.zeros_like(st_ref)

        dfb = df_ref[...]
        dh2 = jnp.zeros((tm, D_MODEL), F32)
        for k in range(N_CHIPS):
            cols = slice(k * D_MODEL, (k + 1) * D_MODEL)
            da = _dot_nt(dfb, w2_ref[k])
            du = (da * (2.0 * jnp.maximum(u_ref[:, cols].astype(F32), 0.0))).astype(BF)
            du_ref[:, cols] = du
            dh2 = dh2 + _dot_nt(du, w1_ref[k])
        h1 = h1_ref[...]
        r3 = _rms(h1)
        hn3 = h1 * r3
        g3, sc2 = _row(vec_ref, V_G3), _row(vec_ref, V_SC2)
        dhn3 = dh2 * g3 * (1.0 + sc2)
        dh1 = dout_ref[...] + r3 * (dhn3 - hn3 * _rowmean(dhn3 * hn3))
        y = y_ref[...]
        r2 = _rms(y)
        yn = y * r2
        gt1, g2 = _row(vec_ref, V_GT1), _row(vec_ref, V_G2)
        dyn = dh1 * gt1 * g2
        dh1_ref[...] = dh1
        dy_ref[...] = (r2 * (dyn - yn * _rowmean(dyn * yn))).astype(BF)
        st_ref[0:1, :] += _colsum(dh2)
        st_ref[1:2, :] += _colsum(dh2 * hn3 * g3)
        st_ref[2:3, :] += _colsum(dh2 * hn3 * (1.0 + sc2))
        st_ref[3:4, :] += _colsum(dh1 * yn * g2)
        st_ref[4:5, :] += _colsum(dh1 * gt1 * yn)

    row = pl.BlockSpec((tm, D_MODEL), lambda i: (i, 0))
    wide = pl.BlockSpec((tm, D_FF), lambda i: (i, 0))
    return pl.pallas_call(
        body, name="ffn_bwd", grid=(SEQ // tm,),
        in_specs=[row, wide, row, row, row, _resident((16, D_MODEL)), _resident(w_ff14.shape),
                  _resident(w_ff24.shape)],
        out_specs=[wide, row, row, pl.BlockSpec((8, D_MODEL), lambda i: (0, 0))],
        out_shape=[jax.ShapeDtypeStruct((SEQ, D_FF), BF), jax.ShapeDtypeStruct((SEQ, D_MODEL), F32),
                   jax.ShapeDtypeStruct((SEQ, D_MODEL), BF), jax.ShapeDtypeStruct((8, D_MODEL), F32)],
        compiler_params=_cp(("arbitrary",)),
    )(df, u, h1, y, dout, vecs, w_ff14, w_ff24)


def _mix_ret_bwd(dy, proj, rb, sbp, o_raw, gn_g, w_out, w_sb4, w_ret):
    tm, half = 256, 512

    def body(dy_ref, ar0, ar1, as0, as1, rb_ref, sbp_ref, g_ref, o_ref, gn_ref, wo_ref, ws_ref, wr_ref,
             drb_ref, dsbp_ref, da_ref, dsb_ref, dret_ref, dgr_ref, st_ref):
        @pl.when(pl.program_id(0) == 0)
        def _():
            st_ref[...] = jnp.zeros_like(st_ref)

        dm_all = _dot_nt(dy_ref[...], wo_ref[...])
        dsb = jnp.zeros((tm, 512), F32)
        drbs = []
        for hf, (ar_ref, as_ref) in enumerate(((ar0, as0), (ar1, as1))):
            cols = slice(half * hf, half * (hf + 1))
            dm = dm_all[:, cols]
            sr = _sigmoid(ar_ref[...].astype(F32))
            ss = _sigmoid(as_ref[...].astype(F32))
            dsbp = (dm * ss).astype(BF)
            drbs.append((dm * sr).astype(BF))
            dsbp_ref[:, cols] = dsbp
            da_ref[:, cols] = (dm * rb_ref[:, cols].astype(F32) * sr * (1.0 - sr)).astype(BF)
            da_ref[:, D_MODEL + half * hf:D_MODEL + half * (hf + 1)] = (
                dm * sbp_ref[:, cols].astype(F32) * ss * (1.0 - ss)).astype(BF)
            dsb = dsb + _dot_nt(dsbp[:, :256], ws_ref[2 * hf]) + _dot_nt(dsbp[:, 256:], ws_ref[2 * hf + 1])
        dsb_ref[...] = dsb.astype(BF)
        drb = jnp.concatenate(drbs, axis=1)
        drb_ref[...] = drb
        dretg = _dot_nt(drb, wr_ref[...])
        for gi in range(D_MODEL // 128):
            cols = slice(128 * gi, 128 * (gi + 1))
            o = o_ref[:, cols]
            d = o - _rowmean(o)
            rstd = lax.rsqrt(_rowmean(d * d) + EPS)
            nh = d * rstd
            gain = gn_ref[:, cols]
            gr = g_ref[:, cols].astype(F32)
            sg = _sigmoid(gr)
            dg = dretg[:, cols]
            dgn = dg * gr * sg
            dnh = dgn * gain
            dgr_ref[:, cols] = (dg * nh * gain * sg * (1.0 + gr * (1.0 - sg))).astype(BF)
            dret_ref[:, cols] = (rstd * (dnh - _rowmean(dnh) - nh * _rowmean(dnh * nh))).astype(BF)
            st_ref[0:1, cols] += _colsum(dgn * nh)

    row = pl.BlockSpec((tm, D_MODEL), lambda i: (i, 0))
    gate = lambda c0: pl.BlockSpec((tm, half), lambda i: (i, c0 // half))
    shp = jax.ShapeDtypeStruct((SEQ, D_MODEL), BF)
    return pl.pallas_call(
        body, name="mix_ret_bwd", grid=(SEQ // tm,),
        in_specs=[row, gate(C_AR), gate(C_AR + half), gate(C_AS), gate(C_AS + half), row, row,
                  pl.BlockSpec((tm, D_MODEL), lambda i: (i, C_GR // D_MODEL)), row, _resident((1, D_MODEL)),
                  _resident((D_MODEL, D_MODEL)), _resident(w_sb4.shape), _resident((D_MODEL, D_MODEL))],
        out_specs=[row, row, pl.BlockSpec((tm, 2 * D_MODEL), lambda i: (i, 0)), pl.BlockSpec((tm, 512), lambda i: (i, 0)),
                   row, row, pl.BlockSpec((8, D_MODEL), lambda i: (0, 0))],
        out_shape=[shp, shp, jax.ShapeDtypeStruct((SEQ, 2 * D_MODEL), BF), jax.ShapeDtypeStruct((SEQ, 512), BF),
                   shp, shp, jax.ShapeDtypeStruct((8, D_MODEL), F32)],
        compiler_params=_cp(("arbitrary",)),
    )(dy, proj, proj, proj, proj, rb, sbp, proj, o_raw, gn_g, w_out, w_sb4, w_ret)


def _dproj_segments(widths):
    wc = D_IN // N_CHIPS
    segs, start = [], 0
    for pi, width in enumerate(widths):
        lo = start
        while lo < start + width:
            j = lo // wc
            hi = min(start + width, (j + 1) * wc)
            segs.append((j, lo - j * wc, pi, lo - start, hi - lo))
            lo = hi
        start += width
    assert start == D_IN
    return segs


def _in_proj_bwd(pieces, x, dh1, vecs, w_in4, updates, dep=None):
    tm = 256
    steps = SEQ // tm
    n, nu = len(pieces), len(updates)
    segs = _dproj_segments([p.shape[1] for p in pieces])

    def body(*refs):
        x_ref, dh1_ref, vec_ref, w_ref = refs[n:n + 4]
        upd_in = refs[n + 4:n + 4 + 4 * nu]
        dx_ref, st_ref = refs[n + 4 + 4 * nu:n + 6 + 4 * nu]
        upd_out = refs[n + 6 + 4 * nu:]
        for u in range(nu):
            w_u, m_u, v_u, g_u = upd_in[4 * u:4 * u + 4]
            go_u, d_u, mo_u, vo_u = upd_out[4 * u:4 * u + 4]
            gg = g_u[...]
            go_u[...] = gg
            d_u[...], mo_u[...], vo_u[...] = _adamw_math(w_u[...], gg, m_u[...], v_u[...])

        @pl.when(pl.program_id(0) == 0)
        def _():
            st_ref[...] = jnp.zeros_like(st_ref)

        dh = jnp.zeros((tm, D_MODEL), F32)
        for j, so, pi, po, width in segs:
            dh = dh + _dot_nt(refs[pi][:, po:po + width], w_ref[j, :, so:so + width])
        xx = x_ref[...]
        r1 = _rms(xx)
        xn = xx * r1
        g1, sc1 = _row(vec_ref, V_G1), _row(vec_ref, V_SC1)
        dxn = dh * g1 * (1.0 + sc1)
        dx_ref[...] = dh1_ref[...] + r1 * (dxn - xn * _rowmean(dxn * xn))
        st_ref[0:1, :] += _colsum(dh)
        st_ref[1:2, :] += _colsum(dh * xn * g1)
        st_ref[2:3, :] += _colsum(dh * xn * (1.0 + sc1))

    row = pl.BlockSpec((tm, D_MODEL), lambda i: (i, 0))
    upd_specs, upd_shapes, upd_args = [], [], []
    for arrays in updates:
        rows, cols = arrays[0].shape
        upd_specs += [pl.BlockSpec((rows // steps, cols), lambda i: (i, 0))] * 4
        upd_shapes += [jax.ShapeDtypeStruct((rows, cols), F32)] * 4
        upd_args += list(arrays)
    body, in_specs, args = _add_dep(
        body, [pl.BlockSpec((tm, p.shape[1]), lambda i: (i, 0)) for p in pieces] + [
            row, row, _resident((16, D_MODEL)), _resident(w_in4.shape)] + upd_specs,
        list(pieces) + [x, dh1, vecs, w_in4] + upd_args, dep)
    outs = pl.pallas_call(
        body, name="in_proj_bwd", grid=(steps,),
        in_specs=in_specs,
        out_specs=[row, pl.BlockSpec((8, D_MODEL), lambda i: (0, 0))] + upd_specs,
        out_shape=[jax.ShapeDtypeStruct((SEQ, D_MODEL), F32), jax.ShapeDtypeStruct((8, D_MODEL), F32)] + upd_shapes,
        compiler_params=_cp(("arbitrary",)),
    )(*args)
    return outs[0], outs[1], [tuple(outs[2 + 4 * u:6 + 4 * u]) for u in range(nu)]


def _grad_w_in(h, pieces, dep=None):
    ta = 256
    n = len(pieces)
    segs = _dproj_segments([p.shape[1] for p in pieces])

    def body(*refs):
        h_ref, o_ref = refs[n], refs[n + 1]
        hh = h_ref[...]
        for j, so, pi, po, width in segs:
            o_ref[j, :, so:so + width] = _dot_tn(hh, refs[pi][:, po:po + width]).astype(BF)

    body, in_specs, args = _add_dep(
        body, [_resident(p.shape) for p in pieces] + [pl.BlockSpec((SEQ, ta), lambda i: (0, i))],
        list(pieces) + [h], dep)
    return pl.pallas_call(
        body, name="grad_w_in", grid=(D_MODEL // ta,),
        in_specs=in_specs,
        out_specs=pl.BlockSpec((N_CHIPS, ta, D_IN // N_CHIPS), lambda i: (0, i, 0)),
        out_shape=jax.ShapeDtypeStruct((N_CHIPS, D_MODEL, D_IN // N_CHIPS), BF),
        compiler_params=_cp(("parallel",)),
    )(*args)


def _weight_grad(a, b, ta, tb, col_sharded, name, dep=None):
    ka, nb_ = a.shape[1], b.shape[1]

    def body(a_ref, b_ref, o_ref):
        o_ref[...] = _dot_tn(a_ref[...], b_ref[...]).astype(BF)

    body, in_specs, args = _add_dep(
        body, [pl.BlockSpec((SEQ, ta), lambda i, j: (0, i)), pl.BlockSpec((SEQ, tb), lambda i, j: (0, j))],
        [a, b], dep)

    if col_sharded:
        per = nb_ // N_CHIPS // tb
        out_shape = jax.ShapeDtypeStruct((N_CHIPS, ka, nb_ // N_CHIPS), BF)
        out_spec = pl.BlockSpec((None, ta, tb), lambda i, j: (j // per, i, j % per))
    else:
        per = ka // N_CHIPS // ta
        out_shape = jax.ShapeDtypeStruct((N_CHIPS, ka // N_CHIPS, nb_), BF)
        out_spec = pl.BlockSpec((None, ta, tb), lambda i, j: (i // per, i % per, j))
    return pl.pallas_call(
        body, name=name, grid=(ka // ta, nb_ // tb),
        in_specs=in_specs, out_specs=out_spec, out_shape=out_shape,
        compiler_params=_cp(("parallel", "parallel")),
    )(*args)


def _rope_constants():
    freq = np.float32(ROPE_BASE) ** (-np.arange(0, 64, 2, dtype=np.float32) / np.float32(64))
    inv = np.tile(freq.astype(np.float32), 4).reshape(1, 128)
    sign = np.tile(np.concatenate([-np.ones(32, np.float32), np.ones(32, np.float32)]), 2).reshape(1, 128)
    return jnp.asarray(inv), jnp.asarray(sign)


def _log_gamma():
    return jnp.asarray(np.log1p(-(2.0 ** (-5.0 - np.arange(8, dtype=np.float64)))).astype(np.float32))


def _halves(g):
    return g.reshape(N_CHIPS, 2, g.shape[1] // 2, g.shape[2])


def kernel(x, c, positions, ada_w, ada_b, pre_mix_g, post_mix_g, pre_ffn_g, post_ffn_g, w_in, ret_gn_g, w_ret_branch, w_sb_branch, w_out, w_ff1, w_ff2, loss_target, m_ada_w, m_ada_b, m_pre_mix_g, m_post_mix_g, m_pre_ffn_g, m_post_ffn_g, m_w_in, m_ret_gn_g, m_w_ret_branch, m_w_sb_branch, m_w_out, m_w_ff1, m_w_ff2, v_ada_w, v_ada_b, v_pre_mix_g, v_post_mix_g, v_pre_ffn_g, v_post_ffn_g, v_w_in, v_ret_gn_g, v_w_ret_branch, v_w_sb_branch, v_w_out, v_w_ff1, v_w_ff2):
    names = ["w_in", "w_ret", "w_sb", "w_out", "w_ff1", "w_ff2"]
    big = dict(zip(names, [w_in, w_ret_branch, w_sb_branch, w_out, w_ff1, w_ff2]))
    big_m = dict(zip(names, [m_w_in, m_w_ret_branch, m_w_sb_branch, m_w_out, m_w_ff1, m_w_ff2]))
    big_v = dict(zip(names, [v_w_in, v_w_ret_branch, v_w_sb_branch, v_w_out, v_w_ff1, v_w_ff2]))
    rest = names[1:]
    cidx = lax.axis_index("c").astype(jnp.int32).reshape(1)
    kidx = (2 * lax.axis_index("x") + lax.axis_index("y")).astype(jnp.int32).reshape(1)
    x0, target = x[0], loss_target[0]

    buf_in, sem_in, tok_in = _gather_start("gather_in_start", [_cast_bf16(w_in[0], kidx, c, "cast_w_in")])
    rest_bufs = [_cast_bf16(big[nm][0], kidx, tok_in, "cast_" + nm) for nm in rest]
    inv_freq, sign = _rope_constants()
    lg = _log_gamma()
    cos, sin_s = _rope_tables(positions.reshape(SEQ, 1), inv_freq, sign, dep=tok_in)

    def table(b6, g5):
        return jnp.concatenate([b6.reshape(6, D_MODEL)] + g5 + [jnp.zeros((5, D_MODEL), F32)], axis=0)

    wsm = table(ada_b, [pre_mix_g, post_mix_g, pre_ffn_g, post_ffn_g, ret_gn_g])
    msm = table(m_ada_b, [m_pre_mix_g, m_post_mix_g, m_pre_ffn_g, m_post_ffn_g, m_ret_gn_g])
    vsm = table(v_ada_b, [v_pre_mix_g, v_post_mix_g, v_pre_ffn_g, v_post_ffn_g, v_ret_gn_g])
    c_all, mod4 = _mod_exchange(c, ada_w[0], ada_b.reshape(N_CHIPS, -1), rest_bufs + [cos, wsm, msm, vsm])
    vecs = jnp.concatenate([mod4.reshape(6, D_MODEL), pre_mix_g, post_mix_g, pre_ffn_g, post_ffn_g,
                            jnp.zeros((6, D_MODEL), F32)], axis=0)
    buf_in, sem_in, tok_in = _gather_pass("gather_in_pass", buf_in, sem_in, vecs)
    buf_rest, sem_rest, tok_rest = _gather_start("gather_rest_start", rest_bufs, after=tok_in)
    (w_in4,) = _gather_finish("gather_in_finish", buf_in, sem_in, tok_rest)

    h, proj = _ln_proj(x0, vecs, w_in4)
    sb, tot = _sb_fwd(proj)
    buf_rest, sem_rest, tok_rest = _gather_pass("gather_rest_pass", buf_rest, sem_rest, sb)
    o_raw, retg, states = _ret_fwd(proj, cos, sin_s, ret_gn_g, lg, dep=tok_rest)
    w_ret4, w_sb4, w_out4, w_ff14, w_ff24 = _gather_finish("gather_rest_finish", buf_rest, sem_rest, retg)
    w_ret = w_ret4.reshape(D_MODEL, D_MODEL)
    w_out2 = w_out4.reshape(D_MODEL, D_MODEL)
    mixed, rb, sbp, y, h1, h2 = _mix_out(retg, sb, proj, x0, vecs, w_ret, w_sb4, w_out2)
    u, act, dout, df, st_a = _ffn_fwd_loss(h2, h1, target, vecs, w_ff14, w_ff24)

    du, dh1, dy, st_b = _ffn_bwd(df, u, h1, y, dout, vecs, w_ff14, w_ff24)
    grads = {"w_ff2": _weight_grad(act, df, 512, 1024, False, "grad_w_ff2")}
    grads["w_ff1"] = _weight_grad(h2, du, 512, 1024, True, "grad_w_ff1")
    drb, dsbp, da, dsb, dret, dg_r, st_c = _mix_ret_bwd(dy, proj, rb, sbp, o_raw, ret_gn_g, w_out2, w_sb4, w_ret)
    grads["w_out"] = _weight_grad(mixed, dy, 256, 1024, False, "grad_w_out")
    grads["w_ret"] = _weight_grad(retg, drb, 256, 1024, False, "grad_w_ret")
    grads["w_sb"] = _weight_grad(sb, dsbp, 512, 256, True, "grad_w_sb")

    bufs, sems, tok = _pair_send_start("rs_rest_pair_send", [_halves(grads[nm]) for nm in rest])
    dqkv_r = _ret_bwd(proj, cos, sin_s, dret, states, lg, dep=tok)
    mine, theirs = _pair_send_wait("rs_rest_pair_recv", bufs, sems, dqkv_r)
    pair_sums = [_pair_add(g, r, cidx, "pair_add_" + nm) for g, r, nm in zip(mine, theirs, rest)]
    bufs, sems, tok = _chip_send_start("rs_rest_chip_send", pair_sums)
    dq_s, dk_s, dv_s = _sb_bwd(proj, dsb, tot, dep=tok)
    own, parts = _chip_send_wait("rs_rest_chip_recv", bufs, sems, dq_s)
    sums = [_chip_add(o, p, kidx, cidx, "chip_add_" + nm) for o, p, nm in zip(own, parts, rest)]
    bufs, sems, tok = _pair_swap_start("rs_rest_pair_swap", sums)
    dproj = [dqkv_r, dg_r, dq_s, dk_s, dv_s, da]
    g_in = _grad_w_in(h, dproj, dep=tok)
    full_rest = _pair_swap_wait("rs_rest_pair_swapped", bufs, sems, g_in)

    full_rest = dict(zip(rest, full_rest))
    bufs, sems, tok = _pair_send_start("rs_in_pair_send", [_halves(g_in)])
    out = {}
    for nm in ("w_out", "w_sb", "w_ret"):
        w = big[nm][0]
        out[nm] = _adamw(w, big_m[nm][0], big_v[nm][0], full_rest[nm].reshape(w.shape), "adamw_" + nm, dep=tok)
    mine, theirs = _pair_send_wait("rs_in_pair_recv", bufs, sems, out["w_ret"][1])
    bufs, sems, tok = _chip_send_start("rs_in_chip_send", [_pair_add(mine[0], theirs[0], cidx, "pair_add_w_in")])
    riding = ("w_ff2", "w_ff1")
    dx, st_d, updated = _in_proj_bwd(
        dproj, x0, dh1, vecs, w_in4,
        [(big[nm][0], big_m[nm][0], big_v[nm][0], full_rest[nm].reshape(big[nm].shape[1:])) for nm in riding],
        dep=tok)
    out.update(zip(riding, updated))

    a_, b_, c_, d_ = range(4)
    payload_rows = [(d_, 0), (d_, 1), (b_, 3), (b_, 0), (b_, 1), (a_, 0),
                    (d_, 2), (b_, 4), (b_, 2), (a_, 1), (c_, 0), (a_, 2)]
    g_ada, loss, small = _small_exchange([st_a, st_b, st_c, st_d], payload_rows, c_all, wsm, msm, vsm)
    ada_out = _adamw(ada_w[0], m_ada_w[0], v_ada_w[0], g_ada, "adamw_ada_w")

    own, parts = _chip_send_wait("rs_in_chip_recv", bufs, sems, ada_out[1])
    bufs, sems, tok = _pair_swap_start(
        "rs_in_pair_swap", [_chip_add(own[0], parts[0], kidx, cidx, "chip_add_w_in")])
    (full_in,) = _pair_swap_wait("rs_in_pair_swapped", bufs, sems, tok)
    out["w_in"] = _adamw(w_in[0], m_w_in[0], v_w_in[0], full_in.reshape(w_in.shape[1:]), "adamw_w_in")

    def ordered(which):
        sm = small[which]
        bg = [out[nm][which][None] for nm in names]
        return [ada_out[which][None], sm[0], sm[1], sm[2], sm[3], sm[4], bg[0], sm[5]] + bg[1:]

    return (loss.reshape(()), dx[None], *ordered(0), *ordered(1), *ordered(2), *ordered(3))
```

```python
import functools

import numpy as np
import jax
import jax.numpy as jnp
from jax import lax
from jax.experimental import pallas as pl
from jax.experimental.pallas import tpu as pltpu

SEQ = 2048
D_MODEL = 1024
D_IN = 6656
D_FF = 4096
N_CHIPS = 4
EPS = 1e-6
ROPE_BASE = 10000.0
RET_BLOCK = 256
RET_CHUNK_SHIFT = 6
SB_BLOCK = 256
QK_SCALE = 0.125
N_PAIRS = 4
SB_GROUP = 4

ADAM_LR = 0.001
ADAM_B1 = 0.9
ADAM_B2 = 0.999
ADAM_EPS = 1e-08
ADAM_WD = 0.01
ADAM_STEP = 10

BF = jnp.bfloat16
F32 = jnp.float32
MESH = pl.DeviceIdType.MESH
VMEM_LIMIT = 56 * 1024 * 1024
ANY = pl.BlockSpec(memory_space=pl.ANY)

C_QR, C_KR, C_VR, C_GR, C_QS, C_KS, C_VS, C_AR, C_AS = 0, 512, 1024, 2048, 3072, 3584, 4096, 4608, 5632

V_SH1, V_SC1, V_GT1, V_SH2, V_SC2, V_GT2, V_G1, V_G2, V_G3, V_G4 = range(10)
P_DSH1, P_DSC1, P_DGT1, P_DSH2, P_DSC2, P_DGT2, P_DG1, P_DG2, P_DG3, P_DG4, P_DGN, P_LOSS = range(12)
N_PAY = 12


def _cp(sem=None, **kw):
    if sem is not None:
        kw["dimension_semantics"] = sem
    return pltpu.CompilerParams(vmem_limit_bytes=VMEM_LIMIT, **kw)


def _dot(a, b):
    return jnp.dot(a, b, preferred_element_type=F32)


def _dot_nt(a, b):
    return lax.dot_general(a, b, (((1,), (1,)), ((), ())), preferred_element_type=F32)


def _dot_tn(a, b):
    return lax.dot_general(a, b, (((0,), (0,)), ((), ())), preferred_element_type=F32)


def _row(ref, i):
    return ref[i:i + 1, :]


def _rms(v):
    return lax.rsqrt(jnp.mean(v * v, axis=1, keepdims=True) + EPS)


def _colsum(v):
    return jnp.sum(v, axis=0, keepdims=True)


def _rowmean(v):
    return jnp.mean(v, axis=1, keepdims=True)


def _sigmoid(v):
    return 1.0 / (1.0 + jnp.exp(-v))


def _cast_bf16(w, kidx, dep, name):
    rows, cols = w.shape
    tr = min(rows, 512)

    def body(k_ref, w_ref, dep_ref, o_ref):
        o_ref[...] = w_ref[...].astype(BF)

    return pl.pallas_call(
        body, name=name,
        grid_spec=pltpu.PrefetchScalarGridSpec(
            num_scalar_prefetch=1, grid=(rows // tr,),
            in_specs=[pl.BlockSpec((tr, cols), lambda i, k_ref: (i, 0)), ANY],
            out_specs=pl.BlockSpec((None, tr, cols), lambda i, k_ref: (k_ref[0], i, 0))),
        out_shape=jax.ShapeDtypeStruct((N_CHIPS, rows, cols), BF),
        compiler_params=_cp(("parallel",)),
    )(kidx, w, dep)


def _adamw_math(w, g, m, v):
    m = ADAM_B1 * m + (1.0 - ADAM_B1) * g
    v = ADAM_B2 * v + (1.0 - ADAM_B2) * (g * g)
    m_hat = m / (1.0 - ADAM_B1 ** ADAM_STEP)
    v_hat = v / (1.0 - ADAM_B2 ** ADAM_STEP)
    delta = -ADAM_LR * (m_hat / (jnp.sqrt(v_hat) + ADAM_EPS) + ADAM_WD * w)
    return delta, m, v


def _adamw(w, m, v, g, name, dep=None):
    rows, cols = w.shape
    tr = min(rows, 256)

    def body(w_ref, m_ref, v_ref, g_ref, go_ref, d_ref, mo_ref, vo_ref):
        gg = g_ref[...]
        d, mm, vv = _adamw_math(w_ref[...], gg, m_ref[...], v_ref[...])
        go_ref[...] = gg
        d_ref[...] = d
        mo_ref[...] = mm
        vo_ref[...] = vv

    spec = pl.BlockSpec((tr, cols), lambda i: (i, 0))
    shp = jax.ShapeDtypeStruct((rows, cols), F32)
    body, in_specs, args = _add_dep(body, [spec] * 4, [w, m, v, g], dep)
    return pl.pallas_call(
        body, name=name, grid=(rows // tr,),
        in_specs=in_specs, out_specs=[spec] * 4, out_shape=[shp] * 4,
        compiler_params=_cp(("parallel",)),
    )(*args)


def _place():
    x, y, c = lax.axis_index("x"), lax.axis_index("y"), lax.axis_index("c")
    return x, y, c


HBM = pl.BlockSpec(memory_space=pltpu.HBM)
SEM = pl.BlockSpec(memory_space=pltpu.SEMAPHORE)
EFFECT = pltpu.SideEffectType.DATAFLOW_SIDE_EFFECTING


def _add_dep(body, in_specs, args, dep):
    if dep is None:
        return body, list(in_specs), list(args)
    n = len(args)

    def wrapped(*refs):
        body(*refs[:n], *refs[n + 1:])

    return wrapped, list(in_specs) + [ANY], list(args) + [dep]


def _split_call(name, bufs, run, old=None, after=None, new=0):
    nb = len(bufs)
    n_old = 2 if old is not None else 0
    n_in = nb + n_old + (1 if after is not None else 0)

    def body(*refs):
        old_sems = (refs[nb], refs[nb + 1]) if old is not None else None
        new_sems = (refs[n_in], refs[n_in + 1]) if new else None
        run(refs[:nb], old_sems, new_sems)
        if new:
            refs[-1][...] = jnp.zeros_like(refs[-1])

    in_specs = [HBM] * nb + [SEM] * n_old + ([ANY] if after is not None else [])
    out_shape = [pltpu.SemaphoreType.DMA((new,))] * 2 if new else []
    out_specs = [SEM, SEM] if new else []
    out_shape += [pltpu.HBM(b.shape, b.dtype) for b in bufs]
    out_specs += [HBM] * nb
    if new:
        out_shape.append(jax.ShapeDtypeStruct((8, 128), F32))
        out_specs.append(pl.BlockSpec(memory_space=pltpu.VMEM))
    first = 2 if new else 0
    args = [pltpu.with_memory_space_constraint(b, pltpu.HBM) for b in bufs]
    if old is not None:
        args += [old[0], old[1]]
    if after is not None:
        args.append(after)
    outs = pl.pallas_call(
        body, name=name, in_specs=tuple(in_specs), out_specs=tuple(out_specs), out_shape=tuple(out_shape),
        input_output_aliases={i: i + first for i in range(nb)},
        compiler_params=pltpu.CompilerParams(has_side_effects=EFFECT),
    )(*args)
    thru = list(outs[first:first + nb])
    if new:
        return thru, (outs[0], outs[1]), outs[-1]
    return thru, None, None


def _remote(part_src, part_dst, sems, i, to):
    return pltpu.make_async_remote_copy(src_ref=part_src, dst_ref=part_dst, send_sem=sems[0].at[i],
                                        recv_sem=sems[1].at[i], device_id=to, device_id_type=MESH)


def _other_chips(x, y):
    return [(1 - x, y), (x, 1 - y), (1 - x, 1 - y)]


def _gather_start(name, bufs, after=None):
    def run(refs, old, new):
        x, y, c = _place()
        k = 2 * x + y
        for w, ref in enumerate(refs):
            rh = bufs[w].shape[1] // 2
            part = ref.at[k, pl.ds(c * rh, rh)]
            for j, (cx, cy) in enumerate(_other_chips(x, y)):
                _remote(part, part, new, 3 * w + j, (cx, cy, c)).start()

    return _split_call(name, bufs, run, after=after, new=3 * len(bufs))


def _gather_pass(name, bufs, sems, after):
    def run(refs, old, new):
        x, y, c = _place()
        k = 2 * x + y
        sib = (x, y, 1 - c)
        for w, ref in enumerate(refs):
            rh = bufs[w].shape[1] // 2
            for j, (cx, cy) in enumerate(_other_chips(x, y)):
                land = ref.at[2 * cx + cy, pl.ds(c * rh, rh)]
                _remote(land, land, old, 3 * w + j, (cx, cy, c)).wait_recv()
                _remote(land, land, new, 3 * w + j, sib).start()
        for w, ref in enumerate(refs):
            rh = bufs[w].shape[1] // 2
            part = ref.at[k, pl.ds(c * rh, rh)]
            for j, (cx, cy) in enumerate(_other_chips(x, y)):
                _remote(part, part, old, 3 * w + j, (cx, cy, c)).wait_send()

    return _split_call(name, bufs, run, old=sems, after=after, new=3 * len(bufs))


def _gather_finish(name, bufs, sems, after):
    def run(refs, old, new):
        x, y, c = _place()
        sib = (x, y, 1 - c)
        for w, ref in enumerate(refs):
            rh = bufs[w].shape[1] // 2
            for j, (cx, cy) in enumerate(_other_chips(x, y)):
                sent = ref.at[2 * cx + cy, pl.ds(c * rh, rh)]
                _remote(sent, sent, old, 3 * w + j, sib).wait_send()
                land = ref.at[2 * cx + cy, pl.ds((1 - c) * rh, rh)]
                _remote(land, land, old, 3 * w + j, sib).wait_recv()

    return _split_call(name, bufs, run, old=sems, after=after)[0]


def _pair_send_start(name, grads):
    n = len(grads)
    lands = [lax.empty((N_CHIPS,) + g.shape[2:], g.dtype) for g in grads]

    def run(refs, old, new):
        x, y, c = _place()
        for w in range(n):
            _remote(refs[w].at[:, 1 - c], refs[n + w], new, w, (x, y, 1 - c)).start()

    return _split_call(name, list(grads) + lands, run, new=n)


def _pair_send_wait(name, bufs, sems, after):
    n = len(bufs) // 2

    def run(refs, old, new):
        x, y, c = _place()
        for w in range(n):
            cp = _remote(refs[w].at[:, 1 - c], refs[n + w], old, w, (x, y, 1 - c))
            cp.wait_send()
            cp.wait_recv()

    thru = _split_call(name, bufs, run, old=sems, after=after)[0]
    return thru[:n], thru[n:]


def _pair_add(g, recv, cidx, name):
    _, _, rh, cols = g.shape
    tr = min(rh, 256)

    def body(c_ref, g_ref, r_ref, o_ref):
        o_ref[...] = (g_ref[...].astype(F32) + r_ref[...].astype(F32)).astype(BF)

    return pl.pallas_call(
        body, name=name,
        grid_spec=pltpu.PrefetchScalarGridSpec(
            num_scalar_prefetch=1, grid=(rh // tr,),
            in_specs=[pl.BlockSpec((N_CHIPS, None, tr, cols), lambda i, c_ref: (0, c_ref[0], i, 0)),
                      pl.BlockSpec((N_CHIPS, tr, cols), lambda i, c_ref: (0, i, 0))],
            out_specs=pl.BlockSpec((N_CHIPS, tr, cols), lambda i, c_ref: (0, i, 0))),
        out_shape=jax.ShapeDtypeStruct((N_CHIPS, rh, cols), BF),
        compiler_params=_cp(("parallel",)),
    )(cidx, g, recv)


def _chip_send_start(name, sums):
    n = len(sums)
    lands = [lax.empty((3,) + s.shape[1:], BF) for s in sums]

    def run(refs, old, new):
        x, y, c = _place()
        for w in range(n):
            for j, (cx, cy) in enumerate(_other_chips(x, y)):
                _remote(refs[w].at[2 * cx + cy], refs[n + w].at[j], new, 3 * w + j, (cx, cy, c)).start()

    return _split_call(name, list(sums) + lands, run, new=3 * n)


def _chip_send_wait(name, bufs, sems, after):
    n = len(bufs) // 2

    def run(refs, old, new):
        x, y, c = _place()
        for w in range(n):
            for j, (cx, cy) in enumerate(_other_chips(x, y)):
                cp = _remote(refs[w].at[2 * cx + cy], refs[n + w].at[j], old, 3 * w + j, (cx, cy, c))
                cp.wait_send()
                cp.wait_recv()

    thru = _split_call(name, bufs, run, old=sems, after=after)[0]
    return thru[:n], thru[n:]


def _chip_add(own, parts, kidx, cidx, name):
    _, rh, cols = parts.shape
    tr = min(rh, 512)

    def body(k_ref, c_ref, own_ref, p_ref, o_ref):
        acc = own_ref[...].astype(F32)
        for s in range(3):
            acc = acc + p_ref[s].astype(F32)
        o_ref[...] = acc

    return pl.pallas_call(
        body, name=name,
        grid_spec=pltpu.PrefetchScalarGridSpec(
            num_scalar_prefetch=2, grid=(rh // tr,),
            in_specs=[pl.BlockSpec((None, tr, cols), lambda i, k_ref, c_ref: (k_ref[0], i, 0)),
                      pl.BlockSpec((3, tr, cols), lambda i, k_ref, c_ref: (0, i, 0))],
            out_specs=pl.BlockSpec((None, tr, cols), lambda i, k_ref, c_ref: (c_ref[0], i, 0))),
        out_shape=jax.ShapeDtypeStruct((2, rh, cols), F32),
        compiler_params=_cp(("parallel",)),
    )(kidx, cidx, own, parts)


def _pair_swap_start(name, bufs):
    def run(refs, old, new):
        x, y, c = _place()
        for w, ref in enumerate(refs):
            _remote(ref.at[c], ref.at[c], new, w, (x, y, 1 - c)).start()

    return _split_call(name, bufs, run, new=len(bufs))


def _pair_swap_wait(name, bufs, sems, after):
    def run(refs, old, new):
        x, y, c = _place()
        for w, ref in enumerate(refs):
            _remote(ref.at[c], ref.at[c], old, w, (x, y, 1 - c)).wait_send()
            _remote(ref.at[1 - c], ref.at[1 - c], old, w, (x, y, 1 - c)).wait_recv()

    return _split_call(name, bufs, run, old=sems, after=after)[0]


def _peers(x, y, c):
    out = []
    for code in range(1, 8):
        fx, fy, fc = (code >> 2) & 1, (code >> 1) & 1, code & 1
        px = 1 - x if fx else x
        py = 1 - y if fy else y
        pc = 1 - c if fc else c
        out.append((code, (px, py, pc)))
    return out


def _mod_exchange(c_row, ada_w, ada_b4, deps):
    ncol = ada_w.shape[1]

    def body(c_ref, w_ref, b_ref, *rest):
        call_ref, mod_ref, part_ref, send_sems, recv_sems = rest[len(deps):]
        x, y, c = _place()
        k = 2 * x + y
        me = 4 * x + 2 * y + c
        call_ref[pl.ds(me, 1), :] = c_ref[...]
        sends = []
        for code, peer in _peers(x, y, c):
            cp = pltpu.make_async_remote_copy(
                src_ref=c_ref, dst_ref=call_ref.at[pl.ds(me, 1), :],
                send_sem=send_sems.at[code], recv_sem=recv_sems.at[code],
                device_id=peer, device_id_type=MESH)
            cp.start()
            sends.append(cp)
        for code, (px, py, pc) in _peers(x, y, c):
            land = call_ref.at[pl.ds(4 * px + 2 * py + pc, 1), :]
            pltpu.make_async_remote_copy(
                src_ref=land, dst_ref=land, send_sem=send_sems.at[code], recv_sem=recv_sems.at[code],
                device_id=(px, py, pc), device_id_type=MESH).wait_recv()
        call = call_ref[...]
        act = call * _sigmoid(call)
        part = jnp.dot(act, w_ref[...], preferred_element_type=F32,
                       precision=lax.Precision.HIGHEST) + b_ref[pl.ds(k, 1), :]
        part_ref[...] = part
        mod_ref[pl.ds(k, 1), :] = part_ref[pl.ds(me, 1), :]
        chips = [(8 + j, peer) for j, (code, peer) in enumerate(_peers(x, y, c)) if code in (2, 4, 6)]
        for slot, (px, py, pc) in chips:
            cp = pltpu.make_async_remote_copy(
                src_ref=part_ref.at[pl.ds(4 * px + 2 * py + pc, 1), :], dst_ref=mod_ref.at[pl.ds(k, 1), :],
                send_sem=send_sems.at[slot], recv_sem=recv_sems.at[slot],
                device_id=(px, py, pc), device_id_type=MESH)
            cp.start()
            sends.append(cp)
        for slot, (px, py, pc) in chips:
            land = mod_ref.at[pl.ds(2 * px + py, 1), :]
            pltpu.make_async_remote_copy(
                src_ref=land, dst_ref=land, send_sem=send_sems.at[slot], recv_sem=recv_sems.at[slot],
                device_id=(px, py, pc), device_id_type=MESH).wait_recv()
        for cp in sends:
            cp.wait_send()

    vm = pl.BlockSpec(memory_space=pltpu.VMEM)
    return pl.pallas_call(
        body, name="mod_exchange",
        in_specs=[vm, vm, vm] + [ANY] * len(deps), out_specs=[vm, vm],
        out_shape=[jax.ShapeDtypeStruct((8, D_MODEL), F32), jax.ShapeDtypeStruct((N_CHIPS, ncol), F32)],
        scratch_shapes=[pltpu.VMEM((8, ncol), F32), pltpu.SemaphoreType.DMA((16,)),
                        pltpu.SemaphoreType.DMA((16,))],
        compiler_params=_cp(),
    )(c_row, ada_w, ada_b4, *deps)


def _small_exchange(stats, rows, c_all, wsm, msm, vsm):
    ncol = 6 * D_MODEL // N_CHIPS
    ns = len(stats)

    def body(*refs):
        call_ref, w_ref, m_ref, v_ref, gw_ref, loss_ref = refs[ns:ns + 6]
        outs = refs[ns + 6:ns + 30]
        p_ref, g_ref, all_ref, dm_ref, send_sems, recv_sems = refs[ns + 30:]
        x, y, c = _place()
        k = 2 * x + y
        me = 4 * x + 2 * y + c
        for r, (tab, row) in enumerate(rows):
            p_ref[r] = refs[tab][row:row + 1, :]
        all_ref[:, pl.ds(me, 1), :] = p_ref[...]
        sends = []
        for code, peer in _peers(x, y, c):
            cp = pltpu.make_async_remote_copy(
                src_ref=p_ref, dst_ref=all_ref.at[:, pl.ds(me, 1), :],
                send_sem=send_sems.at[code], recv_sem=recv_sems.at[code],
                device_id=peer, device_id_type=MESH)
            cp.start()
            sends.append(cp)
        for code, (px, py, pc) in _peers(x, y, c):
            land = all_ref.at[:, pl.ds(4 * px + 2 * py + pc, 1), :]
            pltpu.make_async_remote_copy(
                src_ref=land, dst_ref=land, send_sem=send_sems.at[code], recv_sem=recv_sems.at[code],
                device_id=(px, py, pc), device_id_type=MESH).wait_recv()
        for cp in sends:
            cp.wait_send()
        tot = [_colsum(all_ref[r]) for r in range(N_PAY)]
        loss_ref[...] = jnp.sum(tot[P_LOSS], axis=1, keepdims=True)
        g_ref[...] = jnp.zeros_like(g_ref)
        for r in range(P_LOSS):
            g_ref[r:r + 1, :] = tot[r]
        g = g_ref[...]
        for kind, tab in enumerate((g,) + _adamw_math(w_ref[...], g, m_ref[...], v_ref[...])):
            for r in range(6):
                outs[6 * kind][:, r * D_MODEL:(r + 1) * D_MODEL] = tab[r:r + 1, :]
            for i in range(5):
                outs[6 * kind + 1 + i][...] = tab[6 + i:7 + i, :]
        half = D_MODEL // 2
        for kk in range(N_CHIPS):
            @pl.when(k == kk)
            def _():
                r0 = 3 * (kk // 2)
                if kk % 2 == 0:
                    dm_ref[:, :D_MODEL] = all_ref[r0]
                    dm_ref[:, D_MODEL:] = all_ref[r0 + 1][:, :half]
                else:
                    dm_ref[:, :half] = all_ref[r0 + 1][:, half:]
                    dm_ref[:, half:] = all_ref[r0 + 2]
        call = call_ref[...]
        act = call * _sigmoid(call)
        gw_ref[...] = lax.dot_general(act, dm_ref[...], (((0,), (0,)), ((), ())),
                                      preferred_element_type=F32, precision=lax.Precision.HIGHEST)

    vm = pl.BlockSpec(memory_space=pltpu.VMEM)
    vectors = [jax.ShapeDtypeStruct((1, 6 * D_MODEL), F32)] + [jax.ShapeDtypeStruct((1, D_MODEL), F32)] * 5
    outs = pl.pallas_call(
        body, name="small_exchange",
        in_specs=[vm] * (ns + 4), out_specs=[vm] * 26,
        out_shape=[jax.ShapeDtypeStruct((D_MODEL, ncol), F32), jax.ShapeDtypeStruct((1, 1), F32)] + vectors * 4,
        scratch_shapes=[pltpu.VMEM((N_PAY, 1, D_MODEL), F32), pltpu.VMEM((16, D_MODEL), F32),
                        pltpu.VMEM((N_PAY, 8, D_MODEL), F32), pltpu.VMEM((8, ncol), F32),
                        pltpu.SemaphoreType.DMA((8,)), pltpu.SemaphoreType.DMA((8,))],
        compiler_params=_cp(),
    )(*stats, c_all, wsm, msm, vsm)
    return outs[0], outs[1], [outs[2 + 6 * kind:8 + 6 * kind] for kind in range(4)]


def _rope_tables(pos_col, inv_freq, sign, dep=None):
    def body(p_ref, f_ref, s_ref, cos_ref, sin_ref):
        ang = p_ref[...].astype(F32) * f_ref[...]
        cos_ref[...] = jnp.cos(ang)
        sin_ref[...] = jnp.sin(ang) * s_ref[...]

    tr = 512
    shp = jax.ShapeDtypeStruct((SEQ, 128), F32)
    body, in_specs, args = _add_dep(
        body, [pl.BlockSpec((tr, 1), lambda i: (i, 0)), pl.BlockSpec((1, 128), lambda i: (0, 0)),
               pl.BlockSpec((1, 128), lambda i: (0, 0))], [pos_col, inv_freq, sign], dep)
    return pl.pallas_call(
        body, name="rope_tables", grid=(SEQ // tr,),
        in_specs=in_specs,
        out_specs=[pl.BlockSpec((tr, 128), lambda i: (i, 0))] * 2, out_shape=[shp, shp],
        compiler_params=_cp(("parallel",)),
    )(*args)


def _resident(shape):
    nd = len(shape)
    return pl.BlockSpec(shape, lambda *_: (0,) * nd, pipeline_mode=pl.Buffered(1))


def _ln_proj(x, vecs, w_in4):
    tm = min(512, SEQ)
    wc = w_in4.shape[2]

    def body(x_ref, vec_ref, w_ref, h_ref, proj_ref):
        xx = x_ref[...]
        g = _row(vec_ref, V_G1) * (1.0 + _row(vec_ref, V_SC1))
        h = (xx * _rms(xx) * g + _row(vec_ref, V_SH1)).astype(BF)
        h_ref[...] = h
        for j in range(N_CHIPS):
            proj_ref[:, j * wc:(j + 1) * wc] = _dot(h, w_ref[j]).astype(BF)

    return pl.pallas_call(
        body, name="ln_proj", grid=(SEQ // tm,),
        in_specs=[pl.BlockSpec((tm, D_MODEL), lambda i: (i, 0)), _resident((16, D_MODEL)),
                  _resident(w_in4.shape)],
        out_specs=[pl.BlockSpec((tm, D_MODEL), lambda i: (i, 0)), pl.BlockSpec((tm, D_IN), lambda i: (i, 0))],
        out_shape=[jax.ShapeDtypeStruct((SEQ, D_MODEL), BF), jax.ShapeDtypeStruct((SEQ, D_IN), BF)],
        compiler_params=_cp(("parallel",)),
    )(x, vecs, w_in4)


def _lane_first(shape):
    lane = lax.broadcasted_iota(jnp.int32, shape, 1)
    return (lane & 32) == 0


def _rot(v, cos, sin_s):
    partner = jnp.where(_lane_first(v.shape), pltpu.roll(v, 96, 1), pltpu.roll(v, 32, 1))
    return v * cos + partner * sin_s


def _rot_t(dv, cos, sin_s):
    t = dv * sin_s
    partner = jnp.where(_lane_first(dv.shape), pltpu.roll(t, 96, 1), pltpu.roll(t, 32, 1))
    return dv * cos + partner


def _ret_mask(lg):
    t = RET_BLOCK
    ii = lax.broadcasted_iota(jnp.int32, (t, t), 0)
    jj = lax.broadcasted_iota(jnp.int32, (t, t), 1)
    dist = jnp.abs(ii - jj).astype(F32)
    future = (jj >> RET_CHUNK_SHIFT) > (ii >> RET_CHUNK_SHIFT)
    return jnp.where(future, 0.0, jnp.exp(lg * dist))


def _ret_masks(lg, mask_ref, head):
    t = RET_BLOCK
    mask = mask_ref[head]
    ti = lax.broadcasted_iota(jnp.int32, (t, 1), 0).astype(F32)
    from_start = jnp.exp(lg * (ti + 1.0))
    to_end = jnp.exp(lg * (t - 1.0 - ti))
    whole = jnp.exp(jnp.full((1, 128), lg * t, F32))
    return mask, from_start, to_end, whole


def _head_lanes(shape, hh):
    lane = lax.broadcasted_iota(jnp.int32, shape, 1)
    return (lane >> 6) == hh


def _ret_specs():
    t = RET_BLOCK
    return dict(
        q=lambda f: pl.BlockSpec((t, 512), lambda n: (f(n), C_QR // 512)),
        k=lambda f: pl.BlockSpec((t, 512), lambda n: (f(n), C_KR // 512)),
        v=lambda f: pl.BlockSpec((t, D_MODEL), lambda n: (f(n), C_VR // D_MODEL)),
        g=lambda f: pl.BlockSpec((t, D_MODEL), lambda n: (f(n), C_GR // D_MODEL)),
        tab=lambda f: pl.BlockSpec((t, 128), lambda n: (f(n), 0)),
        wide=lambda f: pl.BlockSpec((t, D_MODEL), lambda n: (f(n), 0)),
        state=lambda f: pl.BlockSpec((N_PAIRS, None, 2, 128, 128), lambda n: (0, f(n), 0, 0, 0)),
    )


def _ret_fwd(proj, cos, sin_s, gn_g, log_gamma, dep=None):
    t = RET_BLOCK
    nb = SEQ // t

    def body(lg_ref, q_ref, k_ref, v_ref, g_ref, cos_ref, sin_ref, gn_ref, o_ref, retg_ref, st_ref, state, masks):
        @pl.when(pl.program_id(0) == 0)
        def _():
            state[...] = jnp.zeros_like(state)
            for head in range(2 * N_PAIRS):
                masks[head] = _ret_mask(lg_ref[head])

        cos, sn = cos_ref[...], sin_ref[...]
        for p in range(N_PAIRS):
            q = _rot(q_ref[:, 128 * p:128 * (p + 1)].astype(F32), cos, sn)
            k = _rot(k_ref[:, 128 * p:128 * (p + 1)].astype(F32), cos, sn) * QK_SCALE
            for hh in range(2):
                cols = slice(256 * p + 128 * hh, 256 * p + 128 * (hh + 1))
                lg = lg_ref[2 * p + hh]
                mask, from_start, to_end, whole = _ret_masks(lg, masks, 2 * p + hh)
                lanes = _head_lanes(q.shape, hh)
                qm = jnp.where(lanes, q, 0.0)
                km = jnp.where(lanes, k, 0.0)
                vh = v_ref[:, cols]
                sc = _dot_nt(qm.astype(BF), km.astype(BF)) * mask
                st = state[p, hh]
                st_ref[p, hh] = st
                o = _dot(sc.astype(BF), vh) + _dot((qm * from_start).astype(BF), st.astype(BF))
                state[p, hh] = whole * st + _dot_tn((km * to_end).astype(BF), vh)
                d = o - _rowmean(o)
                nh = d * lax.rsqrt(_rowmean(d * d) + EPS)
                gr = g_ref[:, cols].astype(F32)
                o_ref[:, cols] = o
                retg_ref[:, cols] = (gr * _sigmoid(gr) * nh * gn_ref[:, cols]).astype(BF)

    sp = _ret_specs()
    ident = lambda n: n
    body, in_specs, args = _add_dep(
        body, [pl.BlockSpec(memory_space=pltpu.SMEM), sp["q"](ident), sp["k"](ident), sp["v"](ident),
               sp["g"](ident), sp["tab"](ident), sp["tab"](ident), _resident((1, D_MODEL))],
        [log_gamma, proj, proj, proj, proj, cos, sin_s, gn_g], dep)
    return pl.pallas_call(
        body, name="ret_fwd", grid=(nb,),
        in_specs=in_specs,
        out_specs=[sp["wide"](ident), sp["wide"](ident), sp["state"](ident)],
        out_shape=[jax.ShapeDtypeStruct((SEQ, D_MODEL), F32), jax.ShapeDtypeStruct((SEQ, D_MODEL), BF),
                   jax.ShapeDtypeStruct((N_PAIRS, nb, 2, 128, 128), F32)],
        scratch_shapes=[pltpu.VMEM((N_PAIRS, 2, 128, 128), F32),
                        pltpu.VMEM((2 * N_PAIRS, RET_BLOCK, RET_BLOCK), F32)],
        compiler_params=_cp(("arbitrary",)),
    )(*args)


def _ret_bwd(proj, cos, sin_s, dret, states, log_gamma, dep=None):
    t = RET_BLOCK
    nb = SEQ // t

    def body(lg_ref, q_ref, k_ref, v_ref, cos_ref, sin_ref, do_ref, st_ref, dqkv_ref, dstate, masks):
        @pl.when(pl.program_id(0) == 0)
        def _():
            dstate[...] = jnp.zeros_like(dstate)
            for head in range(2 * N_PAIRS):
                masks[head] = _ret_mask(lg_ref[head])

        cos, sn = cos_ref[...], sin_ref[...]
        for p in range(N_PAIRS):
            q = _rot(q_ref[:, 128 * p:128 * (p + 1)].astype(F32), cos, sn)
            k = _rot(k_ref[:, 128 * p:128 * (p + 1)].astype(F32), cos, sn) * QK_SCALE
            dq_rot = jnp.zeros(q.shape, F32)
            dk_rot = jnp.zeros(q.shape, F32)
            for hh in range(2):
                cols = slice(256 * p + 128 * hh, 256 * p + 128 * (hh + 1))
                lg = lg_ref[2 * p + hh]
                mask, from_start, to_end, whole = _ret_masks(lg, masks, 2 * p + hh)
                lanes = _head_lanes(q.shape, hh)
                qm = jnp.where(lanes, q, 0.0)
                km = jnp.where(lanes, k, 0.0)
                qb, kb = qm.astype(BF), km.astype(BF)
                vh = v_ref[:, cols]
                do = do_ref[:, cols]
                sc = (_dot_nt(qb, kb) * mask).astype(BF)
                st = st_ref[p, hh].astype(BF)
                dst = dstate[p, hh]
                dstb = dst.astype(BF)
                k_end = (km * to_end).astype(BF)
                q_start = (qm * from_start).astype(BF)
                dqkv_ref[:, C_VR + 256 * p + 128 * hh:C_VR + 256 * p + 128 * (hh + 1)] = (
                    _dot_tn(sc, do) + _dot(k_end, dstb)).astype(BF)
                dsc = (_dot_nt(do, vh) * mask).astype(BF)
                dq_h = _dot(dsc, kb) + _dot_nt(do, st) * from_start
                dq_rot = dq_rot + jnp.where(lanes, dq_h, 0.0)
                dk_rot = dk_rot + _dot_tn(dsc, qb) + _dot_nt(vh, dstb) * to_end
                dstate[p, hh] = whole * dst + _dot_tn(q_start, do)
            dqkv_ref[:, C_QR + 128 * p:C_QR + 128 * (p + 1)] = _rot_t(dq_rot, cos, sn).astype(BF)
            dqkv_ref[:, C_KR + 128 * p:C_KR + 128 * (p + 1)] = _rot_t(dk_rot * QK_SCALE, cos, sn).astype(BF)

    sp = _ret_specs()
    rev = lambda n: nb - 1 - n
    body, in_specs, args = _add_dep(
        body, [pl.BlockSpec(memory_space=pltpu.SMEM), sp["q"](rev), sp["k"](rev), sp["v"](rev),
               sp["tab"](rev), sp["tab"](rev), sp["wide"](rev), sp["state"](rev)],
        [log_gamma, proj, proj, proj, cos, sin_s, dret, states], dep)
    return pl.pallas_call(
        body, name="ret_bwd", grid=(nb,),
        in_specs=in_specs,
        out_specs=pl.BlockSpec((t, C_GR), lambda n: (rev(n), 0)),
        out_shape=jax.ShapeDtypeStruct((SEQ, C_GR), BF),
        scratch_shapes=[pltpu.VMEM((N_PAIRS, 2, 128, 128), F32),
                        pltpu.VMEM((2 * N_PAIRS, RET_BLOCK, RET_BLOCK), F32)],
        compiler_params=_cp(("arbitrary",)),
    )(*args)


def _stack_heads(v):
    return jnp.concatenate([jnp.where(_head_lanes(v.shape, hh), v, jnp.zeros_like(v)) for hh in range(2)], axis=0)


def _unstack_heads(v):
    t = v.shape[0] // 2
    return jnp.where(_head_lanes((t, v.shape[1]), 0), v[:t], v[t:])


def _sb_masks(t, heads):
    rr = lax.broadcasted_iota(jnp.int32, (t, t), 0)
    cc = lax.broadcasted_iota(jnp.int32, (t, t), 1)
    r2 = lax.broadcasted_iota(jnp.int32, (heads * t, t), 0) & (t - 1)
    c2 = lax.broadcasted_iota(jnp.int32, (heads * t, t), 1)
    return rr, cc, c2 < r2


def _split_dot2(v, tri):
    return _dot(v.astype(BF), tri)


def _log_one_minus_beta(z):
    return -(jnp.maximum(z, 0.0) + jnp.log(1.0 + jnp.exp(-jnp.abs(z))))


def _sb_fwd(proj):
    t, g = SB_BLOCK, SB_GROUP
    nq = SEQ // t
    rows = 2 * g * t

    def body(q_ref, k_ref, v_ref, o_ref, tot_ref, kt_ref):
        i = pl.program_id(1)

        @pl.when(i == 0)
        def _():
            for p in range(g):
                for jj in range(nq):
                    kt_ref[p, jj] = k_ref[jj * t:(jj + 1) * t, 128 * p:128 * (p + 1)].T

        q2 = [_stack_heads((q_ref[:, 128 * p:128 * (p + 1)].astype(F32) * QK_SCALE).astype(BF)) for p in range(g)]
        rr, cc, valid = _sb_masks(t, 2 * g)
        later = (rr > cc).astype(BF)

        def tile(j, carry, diagonal):
            acc, run = carry
            z = jnp.concatenate([_dot(q2[p], kt_ref[p, j]) for p in range(g)], axis=0)
            lm = _log_one_minus_beta(z)
            if diagonal:
                lm = jnp.where(valid, lm, 0.0)
            after = _split_dot2(lm, later)
            a = jnp.exp(z + lm + after + run)
            if diagonal:
                a = jnp.where(valid, a, 0.0)
            ab = a.astype(BF)
            keys = pl.ds(pl.multiple_of(j * t, t), t)
            av = jnp.concatenate([_dot(ab[2 * t * p:2 * t * (p + 1)], v_ref[keys, 128 * p:128 * (p + 1)])
                                  for p in range(g)], axis=0)
            return acc + av, run + after[:, 0:1] + lm[:, 0:1]

        carry = tile(i, (jnp.zeros((rows, 128), F32), jnp.zeros((rows, 1), F32)), True)
        acc, run = lax.fori_loop(0, i, lambda s, cr: tile(i - 1 - s, cr, False), carry)
        run = jnp.broadcast_to(run, (rows, 128))
        for p in range(g):
            o_ref[:, 128 * p:128 * (p + 1)] = _unstack_heads(acc[2 * t * p:2 * t * (p + 1)]).astype(BF)
            tot_ref[:, 128 * p:128 * (p + 1)] = _unstack_heads(run[2 * t * p:2 * t * (p + 1)])

    w = 128 * g
    return pl.pallas_call(
        body, name="sb_fwd", grid=(N_PAIRS // g, nq),
        in_specs=[pl.BlockSpec((t, w), lambda p, i: (i, C_QS // w + p)),
                  pl.BlockSpec((SEQ, w), lambda p, i: (0, C_KS // w + p)),
                  pl.BlockSpec((SEQ, w), lambda p, i: (0, C_VS // w + p))],
        out_specs=[pl.BlockSpec((t, w), lambda p, i: (i, p))] * 2,
        out_shape=[jax.ShapeDtypeStruct((SEQ, 512), BF), jax.ShapeDtypeStruct((SEQ, 512), F32)],
        scratch_shapes=[pltpu.VMEM((g, nq, 128, t), BF)],
        compiler_params=_cp(("parallel", "arbitrary")),
    )(proj, proj, proj)


def _sb_bwd(proj, dsb, tot, dep=None):
    t, g = SB_BLOCK, SB_GROUP
    nq = SEQ // t
    rows = 2 * g * t

    def body(q_ref, k_ref, v_ref, do_ref, tot_ref, dq_ref, dk_ref, dv_ref, kt_ref, vt_ref, dkt_acc, dvt_acc):
        i = pl.program_id(1)

        @pl.when(i == 0)
        def _():
            dkt_acc[...] = jnp.zeros_like(dkt_acc)
            dvt_acc[...] = jnp.zeros_like(dvt_acc)
            for p in range(g):
                for jj in range(nq):
                    kt_ref[p, jj] = k_ref[jj * t:(jj + 1) * t, 128 * p:128 * (p + 1)].T
                    vt_ref[p, jj] = v_ref[jj * t:(jj + 1) * t, 128 * p:128 * (p + 1)].T

        q2 = [_stack_heads((q_ref[:, 128 * p:128 * (p + 1)].astype(F32) * QK_SCALE).astype(BF)) for p in range(g)]
        do2 = [_stack_heads(do_ref[:, 128 * p:128 * (p + 1)]) for p in range(g)]
        q2t = [v.T for v in q2]
        do2t = [v.T for v in do2]
        tots = tot_ref[...]
        total = jnp.concatenate([tots[:, 64 * h:64 * h + 1] for h in range(2 * g)], axis=0)
        rr, cc, valid = _sb_masks(t, 2 * g)
        upto = (rr <= cc).astype(BF)
        before = (rr < cc).astype(BF)

        def part(v, p):
            return v[2 * t * p:2 * t * (p + 1)]

        def tile(j, carry, diagonal):
            dq, run_l, run_g = carry
            z = jnp.concatenate([_dot(q2[p], kt_ref[p, j]) for p in range(g)], axis=0)
            lm = _log_one_minus_beta(z)
            if diagonal:
                lm = jnp.where(valid, lm, 0.0)
            incl = _split_dot2(lm, upto)
            a = jnp.exp(z + lm + (total - (incl + run_l)))
            if diagonal:
                a = jnp.where(valid, a, 0.0)
            gg = a * jnp.concatenate([_dot(do2[p], vt_ref[p, j]) for p in range(g)], axis=0)
            excl = _split_dot2(gg, before)
            dz = gg * jnp.exp(lm) - (excl + run_g) * jnp.exp(z + lm)
            if diagonal:
                dz = jnp.where(valid, dz, 0.0)
            dzb = dz.astype(BF)
            ab = a.astype(BF)
            keys = pl.ds(pl.multiple_of(j * t, t), t)
            for p in range(g):
                dkt_acc[p, j] += _dot(q2t[p], part(dzb, p))
                dvt_acc[p, j] += _dot(do2t[p], part(ab, p))
            dq_t = jnp.concatenate([_dot(part(dzb, p), k_ref[keys, 128 * p:128 * (p + 1)]) for p in range(g)], axis=0)
            return (dq + dq_t, run_l + incl[:, t - 1:t], run_g + excl[:, t - 1:t] + gg[:, t - 1:t])

        zero = jnp.zeros((rows, 1), F32)
        carry = lax.fori_loop(0, i, lambda j, cr: tile(j, cr, False), (jnp.zeros((rows, 128), F32), zero, zero))
        dq = tile(i, carry, True)[0]
        for p in range(g):
            dq_ref[:, 128 * p:128 * (p + 1)] = (_unstack_heads(part(dq, p)) * QK_SCALE).astype(BF)

        @pl.when(i == nq - 1)
        def _():
            for p in range(g):
                for jj in range(nq):
                    dk_ref[jj * t:(jj + 1) * t, 128 * p:128 * (p + 1)] = dkt_acc[p, jj].T.astype(BF)
                    dv_ref[jj * t:(jj + 1) * t, 128 * p:128 * (p + 1)] = dvt_acc[p, jj].T.astype(BF)

    w = 128 * g
    tile_spec = pl.BlockSpec((t, w), lambda p, i: (i, p))
    col_spec = pl.BlockSpec((SEQ, w), lambda p, i: (0, p))
    shp = jax.ShapeDtypeStruct((SEQ, 512), BF)
    body, in_specs, args = _add_dep(
        body, [pl.BlockSpec((t, w), lambda p, i: (i, C_QS // w + p)),
               pl.BlockSpec((SEQ, w), lambda p, i: (0, C_KS // w + p)),
               pl.BlockSpec((SEQ, w), lambda p, i: (0, C_VS // w + p)),
               tile_spec, tile_spec],
        [proj, proj, proj, dsb, tot], dep)
    return pl.pallas_call(
        body, name="sb_bwd", grid=(N_PAIRS // g, nq),
        in_specs=in_specs,
        out_specs=[tile_spec, col_spec, col_spec],
        out_shape=[shp, shp, shp],
        scratch_shapes=[pltpu.VMEM((g, nq, 128, t), BF), pltpu.VMEM((g, nq, 128, t), BF),
                        pltpu.VMEM((g, nq, 128, t), F32), pltpu.VMEM((g, nq, 128, t), F32)],
        compiler_params=_cp(("parallel", "arbitrary")),
    )(*args)


def _mix_out(retg, sb, proj, x, vecs, w_ret, w_sb4, w_out):
    tm, half = min(512, SEQ), 512

    def body(r_ref, s_ref, ar0, ar1, as0, as1, x_ref, vec_ref, wr_ref, ws_ref, wo_ref,
             mix_ref, rb_ref, sbp_ref, y_ref, h1_ref, h2_ref):
        rb = _dot(r_ref[...], wr_ref[...])
        sbv = s_ref[...]
        sbp = jnp.concatenate([_dot(sbv, ws_ref[k]) for k in range(N_CHIPS)], axis=1)
        gate_r = _sigmoid(jnp.concatenate([ar0[...], ar1[...]], axis=1).astype(F32))
        gate_s = _sigmoid(jnp.concatenate([as0[...], as1[...]], axis=1).astype(F32))
        mixed = (gate_r * rb + gate_s * sbp).astype(BF)
        mix_ref[...] = mixed
        rb_ref[...] = rb.astype(BF)
        sbp_ref[...] = sbp.astype(BF)
        y = _dot(mixed, wo_ref[...])
        h1 = x_ref[...] + _row(vec_ref, V_GT1) * (y * _rms(y)) * _row(vec_ref, V_G2)
        g = _row(vec_ref, V_G3) * (1.0 + _row(vec_ref, V_SC2))
        y_ref[...] = y
        h1_ref[...] = h1
        h2_ref[...] = (h1 * _rms(h1) * g + _row(vec_ref, V_SH2)).astype(BF)

    row = pl.BlockSpec((tm, D_MODEL), lambda i: (i, 0))
    gate = lambda c0: pl.BlockSpec((tm, half), lambda i: (i, c0 // half))
    bf = jax.ShapeDtypeStruct((SEQ, D_MODEL), BF)
    f32 = jax.ShapeDtypeStruct((SEQ, D_MODEL), F32)
    return pl.pallas_call(
        body, name="mix_out", grid=(SEQ // tm,),
        in_specs=[row, pl.BlockSpec((tm, 512), lambda i: (i, 0)), gate(C_AR), gate(C_AR + half), gate(C_AS),
                  gate(C_AS + half), row, _resident((16, D_MODEL)), _resident((D_MODEL, D_MODEL)),
                  _resident(w_sb4.shape), _resident((D_MODEL, D_MODEL))],
        out_specs=[row] * 6, out_shape=[bf, bf, bf, f32, f32, bf],
        compiler_params=_cp(("parallel",)),
    )(retg, sb, proj, proj, proj, proj, x, vecs, w_ret, w_sb4, w_out)


def _ffn_fwd_loss(h2, h1, target, vecs, w_ff14, w_ff24):
    tm = 256

    def body(h2_ref, h1_ref, t_ref, vec_ref, w1_ref, w2_ref, u_ref, a_ref, dout_ref, df_ref, st_ref):
        @pl.when(pl.program_id(0) == 0)
        def _():
            st_ref[...] = jnp.zeros_like(st_ref)

        hb = h2_ref[...]
        f = jnp.zeros((tm, D_MODEL), F32)
        for k in range(N_CHIPS):
            cols = slice(k * D_MODEL, (k + 1) * D_MODEL)
            u = _dot(hb, w1_ref[k])
            r = jnp.maximum(u, 0.0)
            act = (r * r).astype(BF)
            u_ref[:, cols] = u.astype(BF)
            a_ref[:, cols] = act
            f = f + _dot(act, w2_ref[k])
        r4 = _rms(f)
        fn = f * r4
        gt2, g4 = _row(vec_ref, V_GT2), _row(vec_ref, V_G4)
        diff = h1_ref[...] + gt2 * fn * g4 - t_ref[...]
        dout = diff * (1.0 / D_MODEL)
        dfn = dout * gt2 * g4
        dout_ref[...] = dout
        df_ref[...] = (r4 * (dfn - fn * _rowmean(dfn * fn))).astype(BF)
        st_ref[0:1, :] += _colsum(dout * fn * g4)
        st_ref[1:2, :] += _colsum(dout * gt2 * fn)
        st_ref[2:3, :] += _colsum(diff * diff) * (0.5 / D_MODEL)

    row = pl.BlockSpec((tm, D_MODEL), lambda i: (i, 0))
    wide = pl.BlockSpec((tm, D_FF), lambda i: (i, 0))
    return pl.pallas_call(
        body, name="ffn_fwd_loss", grid=(SEQ // tm,),
        in_specs=[row, row, row, _resident((16, D_MODEL)), _resident(w_ff14.shape), _resident(w_ff24.shape)],
        out_specs=[wide, wide, row, row, pl.BlockSpec((8, D_MODEL), lambda i: (0, 0))],
        out_shape=[jax.ShapeDtypeStruct((SEQ, D_FF), BF), jax.ShapeDtypeStruct((SEQ, D_FF), BF),
                   jax.ShapeDtypeStruct((SEQ, D_MODEL), F32), jax.ShapeDtypeStruct((SEQ, D_MODEL), BF),
                   jax.ShapeDtypeStruct((8, D_MODEL), F32)],
        compiler_params=_cp(("arbitrary",)),
    )(h2, h1, target, vecs, w_ff14, w_ff24)


def _ffn_bwd(df, u, h1, y, dout, vecs, w_ff14, w_ff24):
    tm = 256

    def body(df_ref, u_ref, h1_ref, y_ref, dout_ref, vec_ref, w1_ref, w2_ref, du_ref, dh1_ref, dy_ref, st_ref):
        @pl.when(pl.program_id(0) == 0)
        def _():
            st_ref[...] = jnp.zeros_like(st_ref)

        dfb = df_ref[...]
        dh2 = jnp.zeros((tm, D_MODEL), F32)
        for k in range(N_CHIPS):
            cols = slice(k * D_MODEL, (k + 1) * D_MODEL)
            da = _dot_nt(dfb, w2_ref[k])
            du = (da * (2.0 * jnp.maximum(u_ref[:, cols].astype(F32), 0.0))).astype(BF)
            du_ref[:, cols] = du
            dh2 = dh2 + _dot_nt(du, w1_ref[k])
        h1 = h1_ref[...]
        r3 = _rms(h1)
        hn3 = h1 * r3
        g3, sc2 = _row(vec_ref, V_G3), _row(vec_ref, V_SC2)
        dhn3 = dh2 * g3 * (1.0 + sc2)
        dh1 = dout_ref[...] + r3 * (dhn3 - hn3 * _rowmean(dhn3 * hn3))
        y = y_ref[...]
        r2 = _rms(y)
        yn = y * r2
        gt1, g2 = _row(vec_ref, V_GT1), _row(vec_ref, V_G2)
        dyn = dh1 * gt1 * g2
        dh1_ref[...] = dh1
        dy_ref[...] = (r2 * (dyn - yn * _rowmean(dyn * yn))).astype(BF)
        st_ref[0:1, :] += _colsum(dh2)
        st_ref[1:2, :] += _colsum(dh2 * hn3 * g3)
        st_ref[2:3, :] += _colsum(dh2 * hn3 * (1.0 + sc2))
        st_ref[3:4, :] += _colsum(dh1 * yn * g2)
        st_ref[4:5, :] += _colsum(dh1 * gt1 * yn)

    row = pl.BlockSpec((tm, D_MODEL), lambda i: (i, 0))
    wide = pl.BlockSpec((tm, D_FF), lambda i: (i, 0))
    return pl.pallas_call(
        body, name="ffn_bwd", grid=(SEQ // tm,),
        in_specs=[row, wide, row, row, row, _resident((16, D_MODEL)), _resident(w_ff14.shape),
                  _resident(w_ff24.shape)],
        out_specs=[wide, row, row, pl.BlockSpec((8, D_MODEL), lambda i: (0, 0))],
        out_shape=[jax.ShapeDtypeStruct((SEQ, D_FF), BF), jax.ShapeDtypeStruct((SEQ, D_MODEL), F32),
                   jax.ShapeDtypeStruct((SEQ, D_MODEL), BF), jax.ShapeDtypeStruct((8, D_MODEL), F32)],
        compiler_params=_cp(("arbitrary",)),
    )(df, u, h1, y, dout, vecs, w_ff14, w_ff24)


def _mix_ret_bwd(dy, proj, rb, sbp, o_raw, gn_g, w_out, w_sb4, w_ret):
    tm, half = min(512, SEQ), 512

    def body(dy_ref, ar0, ar1, as0, as1, rb_ref, sbp_ref, g_ref, o_ref, gn_ref, wo_ref, ws_ref, wr_ref,
             drb_ref, dsbp_ref, da_ref, dsb_ref, dret_ref, dgr_ref, st_ref):
        @pl.when(pl.program_id(0) == 0)
        def _():
            st_ref[...] = jnp.zeros_like(st_ref)

        dm_all = _dot_nt(dy_ref[...], wo_ref[...])
        dsb = jnp.zeros((tm, 512), F32)
        drbs = []
        for hf, (ar_ref, as_ref) in enumerate(((ar0, as0), (ar1, as1))):
            cols = slice(half * hf, half * (hf + 1))
            dm = dm_all[:, cols]
            sr = _sigmoid(ar_ref[...].astype(F32))
            ss = _sigmoid(as_ref[...].astype(F32))
            dsbp = (dm * ss).astype(BF)
            drbs.append((dm * sr).astype(BF))
            dsbp_ref[:, cols] = dsbp
            da_ref[:, cols] = (dm * rb_ref[:, cols].astype(F32) * sr * (1.0 - sr)).astype(BF)
            da_ref[:, D_MODEL + half * hf:D_MODEL + half * (hf + 1)] = (
                dm * sbp_ref[:, cols].astype(F32) * ss * (1.0 - ss)).astype(BF)
            dsb = dsb + _dot_nt(dsbp[:, :256], ws_ref[2 * hf]) + _dot_nt(dsbp[:, 256:], ws_ref[2 * hf + 1])
        dsb_ref[...] = dsb.astype(BF)
        drb = jnp.concatenate(drbs, axis=1)
        drb_ref[...] = drb
        dretg = _dot_nt(drb, wr_ref[...])
        for gi in range(D_MODEL // 128):
            cols = slice(128 * gi, 128 * (gi + 1))
            o = o_ref[:, cols]
            d = o - _rowmean(o)
            rstd = lax.rsqrt(_rowmean(d * d) + EPS)
            nh = d * rstd
            gain = gn_ref[:, cols]
            gr = g_ref[:, cols].astype(F32)
            sg = _sigmoid(gr)
            dg = dretg[:, cols]
            dgn = dg * gr * sg
            dnh = dgn * gain
            dgr_ref[:, cols] = (dg * nh * gain * sg * (1.0 + gr * (1.0 - sg))).astype(BF)
            dret_ref[:, cols] = (rstd * (dnh - _rowmean(dnh) - nh * _rowmean(dnh * nh))).astype(BF)
            st_ref[0:1, cols] += _colsum(dgn * nh)

    row = pl.BlockSpec((tm, D_MODEL), lambda i: (i, 0))
    gate = lambda c0: pl.BlockSpec((tm, half), lambda i: (i, c0 // half))
    shp = jax.ShapeDtypeStruct((SEQ, D_MODEL), BF)
    return pl.pallas_call(
        body, name="mix_ret_bwd", grid=(SEQ // tm,),
        in_specs=[row, gate(C_AR), gate(C_AR + half), gate(C_AS), gate(C_AS + half), row, row,
                  pl.BlockSpec((tm, D_MODEL), lambda i: (i, C_GR // D_MODEL)), row, _resident((1, D_MODEL)),
                  _resident((D_MODEL, D_MODEL)), _resident(w_sb4.shape), _resident((D_MODEL, D_MODEL))],
        out_specs=[row, row, pl.BlockSpec((tm, 2 * D_MODEL), lambda i: (i, 0)), pl.BlockSpec((tm, 512), lambda i: (i, 0)),
                   row, row, pl.BlockSpec((8, D_MODEL), lambda i: (0, 0))],
        out_shape=[shp, shp, jax.ShapeDtypeStruct((SEQ, 2 * D_MODEL), BF), jax.ShapeDtypeStruct((SEQ, 512), BF),
                   shp, shp, jax.ShapeDtypeStruct((8, D_MODEL), F32)],
        compiler_params=_cp(("arbitrary",)),
    )(dy, proj, proj, proj, proj, rb, sbp, proj, o_raw, gn_g, w_out, w_sb4, w_ret)


def _dproj_segments(widths):
    wc = D_IN // N_CHIPS
    segs, start = [], 0
    for pi, width in enumerate(widths):
        lo = start
        while lo < start + width:
            j = lo // wc
            hi = min(start + width, (j + 1) * wc)
            segs.append((j, lo - j * wc, pi, lo - start, hi - lo))
            lo = hi
        start += width
    assert start == D_IN
    return segs


def _in_proj_bwd(pieces, x, dh1, vecs, w_in4, updates, dep=None):
    tm = 256
    steps = SEQ // tm
    n, nu = len(pieces), len(updates)
    segs = _dproj_segments([p.shape[1] for p in pieces])

    def body(*refs):
        x_ref, dh1_ref, vec_ref, w_ref = refs[n:n + 4]
        upd_in = refs[n + 4:n + 4 + 4 * nu]
        dx_ref, st_ref = refs[n + 4 + 4 * nu:n + 6 + 4 * nu]
        upd_out = refs[n + 6 + 4 * nu:]
        for u in range(nu):
            w_u, m_u, v_u, g_u = upd_in[4 * u:4 * u + 4]
            go_u, d_u, mo_u, vo_u = upd_out[4 * u:4 * u + 4]
            gg = g_u[...]
            go_u[...] = gg
            d_u[...], mo_u[...], vo_u[...] = _adamw_math(w_u[...], gg, m_u[...], v_u[...])

        @pl.when(pl.program_id(0) == 0)
        def _():
            st_ref[...] = jnp.zeros_like(st_ref)

        dh = jnp.zeros((tm, D_MODEL), F32)
        for j, so, pi, po, width in segs:
            dh = dh + _dot_nt(refs[pi][:, po:po + width], w_ref[j, :, so:so + width])
        xx = x_ref[...]
        r1 = _rms(xx)
        xn = xx * r1
        g1, sc1 = _row(vec_ref, V_G1), _row(vec_ref, V_SC1)
        dxn = dh * g1 * (1.0 + sc1)
        dx_ref[...] = dh1_ref[...] + r1 * (dxn - xn * _rowmean(dxn * xn))
        st_ref[0:1, :] += _colsum(dh)
        st_ref[1:2, :] += _colsum(dh * xn * g1)
        st_ref[2:3, :] += _colsum(dh * xn * (1.0 + sc1))

    row = pl.BlockSpec((tm, D_MODEL), lambda i: (i, 0))
    upd_specs, upd_shapes, upd_args = [], [], []
    for arrays in updates:
        rows, cols = arrays[0].shape
        upd_specs += [pl.BlockSpec((rows // steps, cols), lambda i: (i, 0))] * 4
        upd_shapes += [jax.ShapeDtypeStruct((rows, cols), F32)] * 4
        upd_args += list(arrays)
    body, in_specs, args = _add_dep(
        body, [pl.BlockSpec((tm, p.shape[1]), lambda i: (i, 0)) for p in pieces] + [
            row, row, _resident((16, D_MODEL)), _resident(w_in4.shape)] + upd_specs,
        list(pieces) + [x, dh1, vecs, w_in4] + upd_args, dep)
    outs = pl.pallas_call(
        body, name="in_proj_bwd", grid=(steps,),
        in_specs=in_specs,
        out_specs=[row, pl.BlockSpec((8, D_MODEL), lambda i: (0, 0))] + upd_specs,
        out_shape=[jax.ShapeDtypeStruct((SEQ, D_MODEL), F32), jax.ShapeDtypeStruct((8, D_MODEL), F32)] + upd_shapes,
        compiler_params=_cp(("arbitrary",)),
    )(*args)
    return outs[0], outs[1], [tuple(outs[2 + 4 * u:6 + 4 * u]) for u in range(nu)]


def _grad_w_in(h, pieces, dep=None):
    ta = 512
    n = len(pieces)
    segs = _dproj_segments([p.shape[1] for p in pieces])

    def body(*refs):
        h_ref, o_ref = refs[n], refs[n + 1]
        hh = h_ref[...]
        for j, so, pi, po, width in segs:
            o_ref[j, :, so:so + width] = _dot_tn(hh, refs[pi][:, po:po + width]).astype(BF)

    body, in_specs, args = _add_dep(
        body, [_resident(p.shape) for p in pieces] + [pl.BlockSpec((SEQ, ta), lambda i: (0, i))],
        list(pieces) + [h], dep)
    return pl.pallas_call(
        body, name="grad_w_in", grid=(D_MODEL // ta,),
        in_specs=in_specs,
        out_specs=pl.BlockSpec((N_CHIPS, ta, D_IN // N_CHIPS), lambda i: (0, i, 0)),
        out_shape=jax.ShapeDtypeStruct((N_CHIPS, D_MODEL, D_IN // N_CHIPS), BF),
        compiler_params=_cp(("parallel",)),
    )(*args)


def _weight_grad(a, b, ta, tb, col_sharded, name, dep=None):
    ka, nb_ = a.shape[1], b.shape[1]

    def body(a_ref, b_ref, o_ref):
        o_ref[...] = _dot_tn(a_ref[...], b_ref[...]).astype(BF)

    body, in_specs, args = _add_dep(
        body, [pl.BlockSpec((SEQ, ta), lambda i, j: (0, i)), pl.BlockSpec((SEQ, tb), lambda i, j: (0, j))],
        [a, b], dep)

    if col_sharded:
        per = nb_ // N_CHIPS // tb
        out_shape = jax.ShapeDtypeStruct((N_CHIPS, ka, nb_ // N_CHIPS), BF)
        out_spec = pl.BlockSpec((None, ta, tb), lambda i, j: (j // per, i, j % per))
    else:
        per = ka // N_CHIPS // ta
        out_shape = jax.ShapeDtypeStruct((N_CHIPS, ka // N_CHIPS, nb_), BF)
        out_spec = pl.BlockSpec((None, ta, tb), lambda i, j: (i // per, i % per, j))
    return pl.pallas_call(
        body, name=name, grid=(ka // ta, nb_ // tb),
        in_specs=in_specs, out_specs=out_spec, out_shape=out_shape,
        compiler_params=_cp(("parallel", "parallel")),
    )(*args)


def _rope_constants():
    freq = np.float32(ROPE_BASE) ** (-np.arange(0, 64, 2, dtype=np.float32) / np.float32(64))
    inv = np.tile(freq.astype(np.float32), 4).reshape(1, 128)
    sign = np.tile(np.concatenate([-np.ones(32, np.float32), np.ones(32, np.float32)]), 2).reshape(1, 128)
    return jnp.asarray(inv), jnp.asarray(sign)


def _log_gamma():
    return jnp.asarray(np.log1p(-(2.0 ** (-5.0 - np.arange(8, dtype=np.float64)))).astype(np.float32))


def _halves(g):
    return g.reshape(N_CHIPS, 2, g.shape[1] // 2, g.shape[2])


def kernel(x, c, positions, ada_w, ada_b, pre_mix_g, post_mix_g, pre_ffn_g, post_ffn_g, w_in, ret_gn_g, w_ret_branch, w_sb_branch, w_out, w_ff1, w_ff2, loss_target, m_ada_w, m_ada_b, m_pre_mix_g, m_post_mix_g, m_pre_ffn_g, m_post_ffn_g, m_w_in, m_ret_gn_g, m_w_ret_branch, m_w_sb_branch, m_w_out, m_w_ff1, m_w_ff2, v_ada_w, v_ada_b, v_pre_mix_g, v_post_mix_g, v_pre_ffn_g, v_post_ffn_g, v_w_in, v_ret_gn_g, v_w_ret_branch, v_w_sb_branch, v_w_out, v_w_ff1, v_w_ff2):
    names = ["w_in", "w_ret", "w_sb", "w_out", "w_ff1", "w_ff2"]
    big = dict(zip(names, [w_in, w_ret_branch, w_sb_branch, w_out, w_ff1, w_ff2]))
    big_m = dict(zip(names, [m_w_in, m_w_ret_branch, m_w_sb_branch, m_w_out, m_w_ff1, m_w_ff2]))
    big_v = dict(zip(names, [v_w_in, v_w_ret_branch, v_w_sb_branch, v_w_out, v_w_ff1, v_w_ff2]))
    rest = names[1:]
    cidx = lax.axis_index("c").astype(jnp.int32).reshape(1)
    kidx = (2 * lax.axis_index("x") + lax.axis_index("y")).astype(jnp.int32).reshape(1)
    x0, target = x[0], loss_target[0]

    buf_in, sem_in, tok_in = _gather_start("gather_in_start", [_cast_bf16(w_in[0], kidx, c, "cast_w_in")])
    rest_bufs = [_cast_bf16(big[nm][0], kidx, tok_in, "cast_" + nm) for nm in rest]
    inv_freq, sign = _rope_constants()
    lg = _log_gamma()
    cos, sin_s = _rope_tables(positions.reshape(SEQ, 1), inv_freq, sign, dep=tok_in)

    def table(b6, g5):
        return jnp.concatenate([b6.reshape(6, D_MODEL)] + g5 + [jnp.zeros((5, D_MODEL), F32)], axis=0)

    wsm = table(ada_b, [pre_mix_g, post_mix_g, pre_ffn_g, post_ffn_g, ret_gn_g])
    msm = table(m_ada_b, [m_pre_mix_g, m_post_mix_g, m_pre_ffn_g, m_post_ffn_g, m_ret_gn_g])
    vsm = table(v_ada_b, [v_pre_mix_g, v_post_mix_g, v_pre_ffn_g, v_post_ffn_g, v_ret_gn_g])
    c_all, mod4 = _mod_exchange(c, ada_w[0], ada_b.reshape(N_CHIPS, -1), rest_bufs + [cos, wsm, msm, vsm])
    vecs = jnp.concatenate([mod4.reshape(6, D_MODEL), pre_mix_g, post_mix_g, pre_ffn_g, post_ffn_g,
                            jnp.zeros((6, D_MODEL), F32)], axis=0)
    buf_in, sem_in, tok_in = _gather_pass("gather_in_pass", buf_in, sem_in, vecs)
    buf_rest, sem_rest, tok_rest = _gather_start("gather_rest_start", rest_bufs, after=tok_in)
    (w_in4,) = _gather_finish("gather_in_finish", buf_in, sem_in, tok_rest)

    h, proj = _ln_proj(x0, vecs, w_in4)
    sb, tot = _sb_fwd(proj)
    buf_rest, sem_rest, tok_rest = _gather_pass("gather_rest_pass", buf_rest, sem_rest, sb)
    o_raw, retg, states = _ret_fwd(proj, cos, sin_s, ret_gn_g, lg, dep=tok_rest)
    w_ret4, w_sb4, w_out4, w_ff14, w_ff24 = _gather_finish("gather_rest_finish", buf_rest, sem_rest, retg)
    w_ret = w_ret4.reshape(D_MODEL, D_MODEL)
    w_out2 = w_out4.reshape(D_MODEL, D_MODEL)
    mixed, rb, sbp, y, h1, h2 = _mix_out(retg, sb, proj, x0, vecs, w_ret, w_sb4, w_out2)
    u, act, dout, df, st_a = _ffn_fwd_loss(h2, h1, target, vecs, w_ff14, w_ff24)

    du, dh1, dy, st_b = _ffn_bwd(df, u, h1, y, dout, vecs, w_ff14, w_ff24)
    grads = {"w_ff2": _weight_grad(act, df, 512, 1024, False, "grad_w_ff2")}
    grads["w_ff1"] = _weight_grad(h2, du, 1024, 1024, True, "grad_w_ff1")
    drb, dsbp, da, dsb, dret, dg_r, st_c = _mix_ret_bwd(dy, proj, rb, sbp, o_raw, ret_gn_g, w_out2, w_sb4, w_ret)
    grads["w_out"] = _weight_grad(mixed, dy, 256, 1024, False, "grad_w_out")
    grads["w_ret"] = _weight_grad(retg, drb, 256, 1024, False, "grad_w_ret")
    grads["w_sb"] = _weight_grad(sb, dsbp, 512, 256, True, "grad_w_sb")

    bufs, sems, tok = _pair_send_start("rs_rest_pair_send", [_halves(grads[nm]) for nm in rest])
    dqkv_r = _ret_bwd(proj, cos, sin_s, dret, states, lg, dep=tok)
    mine, theirs = _pair_send_wait("rs_rest_pair_recv", bufs, sems, dqkv_r)
    pair_sums = [_pair_add(g, r, cidx, "pair_add_" + nm) for g, r, nm in zip(mine, theirs, rest)]
    bufs, sems, tok = _chip_send_start("rs_rest_chip_send", pair_sums)
    dq_s, dk_s, dv_s = _sb_bwd(proj, dsb, tot, dep=tok)
    own, parts = _chip_send_wait("rs_rest_chip_recv", bufs, sems, dq_s)
    sums = [_chip_add(o, p, kidx, cidx, "chip_add_" + nm) for o, p, nm in zip(own, parts, rest)]
    bufs, sems, tok = _pair_swap_start("rs_rest_pair_swap", sums)
    dproj = [dqkv_r, dg_r, dq_s, dk_s, dv_s, da]
    g_in = _grad_w_in(h, dproj, dep=tok)
    full_rest = _pair_swap_wait("rs_rest_pair_swapped", bufs, sems, g_in)

    full_rest = dict(zip(rest, full_rest))
    bufs, sems, tok = _pair_send_start("rs_in_pair_send", [_halves(g_in)])
    out = {}
    for nm in ("w_out", "w_sb", "w_ret"):
        w = big[nm][0]
        out[nm] = _adamw(w, big_m[nm][0], big_v[nm][0], full_rest[nm].reshape(w.shape), "adamw_" + nm, dep=tok)
    mine, theirs = _pair_send_wait("rs_in_pair_recv", bufs, sems, out["w_ret"][1])
    bufs, sems, tok = _chip_send_start("rs_in_chip_send", [_pair_add(mine[0], theirs[0], cidx, "pair_add_w_in")])
    riding = ("w_ff2", "w_ff1")
    dx, st_d, updated = _in_proj_bwd(
        dproj, x0, dh1, vecs, w_in4,
        [(big[nm][0], big_m[nm][0], big_v[nm][0], full_rest[nm].reshape(big[nm].shape[1:])) for nm in riding],
        dep=tok)
    out.update(zip(riding, updated))

    a_, b_, c_, d_ = range(4)
    payload_rows = [(d_, 0), (d_, 1), (b_, 3), (b_, 0), (b_, 1), (a_, 0),
                    (d_, 2), (b_, 4), (b_, 2), (a_, 1), (c_, 0), (a_, 2)]
    g_ada, loss, small = _small_exchange([st_a, st_b, st_c, st_d], payload_rows, c_all, wsm, msm, vsm)
    ada_out = _adamw(ada_w[0], m_ada_w[0], v_ada_w[0], g_ada, "adamw_ada_w")

    own, parts = _chip_send_wait("rs_in_chip_recv", bufs, sems, ada_out[1])
    bufs, sems, tok = _pair_swap_start(
        "rs_in_pair_swap", [_chip_add(own[0], parts[0], kidx, cidx, "chip_add_w_in")])
    (full_in,) = _pair_swap_wait("rs_in_pair_swapped", bufs, sems, tok)
    out["w_in"] = _adamw(w_in[0], m_w_in[0], v_w_in[0], full_in.reshape(w_in.shape[1:]), "adamw_w_in")

    def ordered(which):
        sm = small[which]
        bg = [out[nm][which][None] for nm in names]
        return [ada_out[which][None], sm[0], sm[1], sm[2], sm[3], sm[4], bg[0], sm[5]] + bg[1:]

    return (loss.reshape(()), dx[None], *ordered(0), *ordered(1), *ordered(2), *ordered(3))
```

```python
import functools

import numpy as np
import jax
import jax.numpy as jnp
from jax import lax
from jax.experimental import pallas as pl
from jax.experimental.pallas import tpu as pltpu

SEQ = 2048
D_MODEL = 1024
D_IN = 6656
D_FF = 4096
N_CHIPS = 4
EPS = 1e-6
ROPE_BASE = 10000.0
RET_BLOCK = 256
RET_CHUNK_SHIFT = 6
SB_BLOCK = 256
QK_SCALE = 0.125
N_PAIRS = 4
SB_GROUP = 4

ADAM_LR = 0.001
ADAM_B1 = 0.9
ADAM_B2 = 0.999
ADAM_EPS = 1e-08
ADAM_WD = 0.01
ADAM_STEP = 10

BF = jnp.bfloat16
F32 = jnp.float32
MESH = pl.DeviceIdType.MESH
VMEM_LIMIT = 56 * 1024 * 1024
ANY = pl.BlockSpec(memory_space=pl.ANY)

C_QR, C_KR, C_VR, C_GR, C_QS, C_KS, C_VS, C_AR, C_AS = 0, 512, 1024, 2048, 3072, 3584, 4096, 4608, 5632

V_SH1, V_SC1, V_GT1, V_SH2, V_SC2, V_GT2, V_G1, V_G2, V_G3, V_G4 = range(10)
P_DSH1, P_DSC1, P_DGT1, P_DSH2, P_DSC2, P_DGT2, P_DG1, P_DG2, P_DG3, P_DG4, P_DGN, P_LOSS = range(12)
N_PAY = 12


def _cp(sem=None, **kw):
    if sem is not None:
        kw["dimension_semantics"] = sem
    return pltpu.CompilerParams(vmem_limit_bytes=VMEM_LIMIT, **kw)


def _dot(a, b):
    return jnp.dot(a, b, preferred_element_type=F32)


def _dot_nt(a, b):
    return lax.dot_general(a, b, (((1,), (1,)), ((), ())), preferred_element_type=F32)


def _dot_tn(a, b):
    return lax.dot_general(a, b, (((0,), (0,)), ((), ())), preferred_element_type=F32)


def _row(ref, i):
    return ref[i:i + 1, :]


def _rms(v):
    return lax.rsqrt(jnp.mean(v * v, axis=1, keepdims=True) + EPS)


def _colsum(v):
    return jnp.sum(v, axis=0, keepdims=True)


def _rowmean(v):
    return jnp.mean(v, axis=1, keepdims=True)


def _sigmoid(v):
    return 1.0 / (1.0 + jnp.exp(-v))


def _cast_bf16(w, kidx, dep, name):
    rows, cols = w.shape
    tr = min(rows, 512)

    def body(k_ref, w_ref, dep_ref, o_ref):
        o_ref[...] = w_ref[...].astype(BF)

    return pl.pallas_call(
        body, name=name,
        grid_spec=pltpu.PrefetchScalarGridSpec(
            num_scalar_prefetch=1, grid=(rows // tr,),
            in_specs=[pl.BlockSpec((tr, cols), lambda i, k_ref: (i, 0)), ANY],
            out_specs=pl.BlockSpec((None, tr, cols), lambda i, k_ref: (k_ref[0], i, 0))),
        out_shape=jax.ShapeDtypeStruct((N_CHIPS, rows, cols), BF),
        compiler_params=_cp(("parallel",)),
    )(kidx, w, dep)


def _adamw_math(w, g, m, v):
    m = ADAM_B1 * m + (1.0 - ADAM_B1) * g
    v = ADAM_B2 * v + (1.0 - ADAM_B2) * (g * g)
    m_hat = m / (1.0 - ADAM_B1 ** ADAM_STEP)
    v_hat = v / (1.0 - ADAM_B2 ** ADAM_STEP)
    delta = -ADAM_LR * (m_hat / (jnp.sqrt(v_hat) + ADAM_EPS) + ADAM_WD * w)
    return delta, m, v


def _adamw(w, m, v, g, name, dep=None):
    rows, cols = w.shape
    tr = min(rows, 256)

    def body(w_ref, m_ref, v_ref, g_ref, go_ref, d_ref, mo_ref, vo_ref):
        gg = g_ref[...]
        d, mm, vv = _adamw_math(w_ref[...], gg, m_ref[...], v_ref[...])
        go_ref[...] = gg
        d_ref[...] = d
        mo_ref[...] = mm
        vo_ref[...] = vv

    spec = pl.BlockSpec((tr, cols), lambda i: (i, 0))
    shp = jax.ShapeDtypeStruct((rows, cols), F32)
    body, in_specs, args = _add_dep(body, [spec] * 4, [w, m, v, g], dep)
    return pl.pallas_call(
        body, name=name, grid=(rows // tr,),
        in_specs=in_specs, out_specs=[spec] * 4, out_shape=[shp] * 4,
        compiler_params=_cp(("parallel",)),
    )(*args)


def _place():
    x, y, c = lax.axis_index("x"), lax.axis_index("y"), lax.axis_index("c")
    return x, y, c


HBM = pl.BlockSpec(memory_space=pltpu.HBM)
SEM = pl.BlockSpec(memory_space=pltpu.SEMAPHORE)
EFFECT = pltpu.SideEffectType.DATAFLOW_SIDE_EFFECTING


def _add_dep(body, in_specs, args, dep):
    if dep is None:
        return body, list(in_specs), list(args)
    n = len(args)

    def wrapped(*refs):
        body(*refs[:n], *refs[n + 1:])

    return wrapped, list(in_specs) + [ANY], list(args) + [dep]


def _split_call(name, bufs, run, old=None, after=None, new=0):
    nb = len(bufs)
    n_old = 2 if old is not None else 0
    n_in = nb + n_old + (1 if after is not None else 0)

    def body(*refs):
        old_sems = (refs[nb], refs[nb + 1]) if old is not None else None
        new_sems = (refs[n_in], refs[n_in + 1]) if new else None
        run(refs[:nb], old_sems, new_sems)
        if new:
            refs[-1][...] = jnp.zeros_like(refs[-1])

    in_specs = [HBM] * nb + [SEM] * n_old + ([ANY] if after is not None else [])
    out_shape = [pltpu.SemaphoreType.DMA((new,))] * 2 if new else []
    out_specs = [SEM, SEM] if new else []
    out_shape += [pltpu.HBM(b.shape, b.dtype) for b in bufs]
    out_specs += [HBM] * nb
    if new:
        out_shape.append(jax.ShapeDtypeStruct((8, 128), F32))
        out_specs.append(pl.BlockSpec(memory_space=pltpu.VMEM))
    first = 2 if new else 0
    args = [pltpu.with_memory_space_constraint(b, pltpu.HBM) for b in bufs]
    if old is not None:
        args += [old[0], old[1]]
    if after is not None:
        args.append(after)
    outs = pl.pallas_call(
        body, name=name, in_specs=tuple(in_specs), out_specs=tuple(out_specs), out_shape=tuple(out_shape),
        input_output_aliases={i: i + first for i in range(nb)},
        compiler_params=pltpu.CompilerParams(has_side_effects=EFFECT),
    )(*args)
    thru = list(outs[first:first + nb])
    if new:
        return thru, (outs[0], outs[1]), outs[-1]
    return thru, None, None


def _remote(part_src, part_dst, sems, i, to):
    return pltpu.make_async_remote_copy(src_ref=part_src, dst_ref=part_dst, send_sem=sems[0].at[i],
                                        recv_sem=sems[1].at[i], device_id=to, device_id_type=MESH)


def _other_chips(x, y):
    return [(1 - x, y), (x, 1 - y), (1 - x, 1 - y)]


def _gather_start(name, bufs, after=None):
    def run(refs, old, new):
        x, y, c = _place()
        k = 2 * x + y
        for w, ref in enumerate(refs):
            rh = bufs[w].shape[1] // 2
            part = ref.at[k, pl.ds(c * rh, rh)]
            for j, (cx, cy) in enumerate(_other_chips(x, y)):
                _remote(part, part, new, 3 * w + j, (cx, cy, c)).start()

    return _split_call(name, bufs, run, after=after, new=3 * len(bufs))


def _gather_pass(name, bufs, sems, after):
    def run(refs, old, new):
        x, y, c = _place()
        k = 2 * x + y
        sib = (x, y, 1 - c)
        for w, ref in enumerate(refs):
            rh = bufs[w].shape[1] // 2
            for j, (cx, cy) in enumerate(_other_chips(x, y)):
                land = ref.at[2 * cx + cy, pl.ds(c * rh, rh)]
                _remote(land, land, old, 3 * w + j, (cx, cy, c)).wait_recv()
                _remote(land, land, new, 3 * w + j, sib).start()
        for w, ref in enumerate(refs):
            rh = bufs[w].shape[1] // 2
            part = ref.at[k, pl.ds(c * rh, rh)]
            for j, (cx, cy) in enumerate(_other_chips(x, y)):
                _remote(part, part, old, 3 * w + j, (cx, cy, c)).wait_send()

    return _split_call(name, bufs, run, old=sems, after=after, new=3 * len(bufs))


def _gather_finish(name, bufs, sems, after):
    def run(refs, old, new):
        x, y, c = _place()
        sib = (x, y, 1 - c)
        for w, ref in enumerate(refs):
            rh = bufs[w].shape[1] // 2
            for j, (cx, cy) in enumerate(_other_chips(x, y)):
                sent = ref.at[2 * cx + cy, pl.ds(c * rh, rh)]
                _remote(sent, sent, old, 3 * w + j, sib).wait_send()
                land = ref.at[2 * cx + cy, pl.ds((1 - c) * rh, rh)]
                _remote(land, land, old, 3 * w + j, sib).wait_recv()

    return _split_call(name, bufs, run, old=sems, after=after)[0]


def _pair_send_start(name, grads):
    n = len(grads)
    lands = [lax.empty((N_CHIPS,) + g.shape[2:], g.dtype) for g in grads]

    def run(refs, old, new):
        x, y, c = _place()
        for w in range(n):
            _remote(refs[w].at[:, 1 - c], refs[n + w], new, w, (x, y, 1 - c)).start()

    return _split_call(name, list(grads) + lands, run, new=n)


def _pair_send_wait(name, bufs, sems, after):
    n = len(bufs) // 2

    def run(refs, old, new):
        x, y, c = _place()
        for w in range(n):
            cp = _remote(refs[w].at[:, 1 - c], refs[n + w], old, w, (x, y, 1 - c))
            cp.wait_send()
            cp.wait_recv()

    thru = _split_call(name, bufs, run, old=sems, after=after)[0]
    return thru[:n], thru[n:]


def _run_sets(name, scalars, sets, deps=()):
    starts = np.concatenate([[0], np.cumsum([s[0] for s in sets])]).tolist()

    def spec(block, index, lo, n):
        return pl.BlockSpec(block, lambda i, *sc: index(jnp.clip(i - lo, 0, n - 1), *sc))

    in_specs, out_specs, out_shape, args = [], [], [], []
    for (n, ins, outs, _), lo in zip(sets, starts):
        for array, block, index in ins:
            in_specs.append(spec(block, index, lo, n))
            args.append(array)
        for shape, dtype, block, index in outs:
            out_specs.append(spec(block, index, lo, n))
            out_shape.append(jax.ShapeDtypeStruct(shape, dtype))

    def body(*refs):
        refs = refs[len(scalars):]
        n_in = len(in_specs)
        i = pl.program_id(0)
        pos_in, pos_out = 0, n_in + len(deps)
        for (n, ins, outs, fn), lo in zip(sets, starts):
            in_refs = refs[pos_in:pos_in + len(ins)]
            out_refs = refs[pos_out:pos_out + len(outs)]
            pos_in += len(ins)
            pos_out += len(outs)
            pl.when((i >= lo) & (i < lo + n))(functools.partial(fn, in_refs, out_refs))

    outs = pl.pallas_call(
        body, name=name,
        grid_spec=pltpu.PrefetchScalarGridSpec(
            num_scalar_prefetch=len(scalars), grid=(starts[-1],), in_specs=in_specs + [ANY] * len(deps),
            out_specs=out_specs),
        out_shape=out_shape,
        compiler_params=_cp(("arbitrary",)),
    )(*scalars, *args, *deps)
    result, pos = [], 0
    for _, _, outs_s, _ in sets:
        result.append(list(outs[pos:pos + len(outs_s)]))
        pos += len(outs_s)
    return result


def _adamw_all(jobs, name, dep):
    def update(in_refs, out_refs):
        gg = in_refs[3][...]
        out_refs[0][...] = gg
        out_refs[1][...], out_refs[2][...], out_refs[3][...] = _adamw_math(
            in_refs[0][...], gg, in_refs[1][...], in_refs[2][...])

    whole = lambda l: (0, 0)
    sets = [(1, [(a, a.shape, whole) for a in job], [(job[0].shape, F32, job[0].shape, whole)] * 4, update)
            for job in jobs]
    return [tuple(o) for o in _run_sets(name, [], sets, deps=[dep])]


def _pair_add_all(gs, recvs, cidx, name):
    def add(in_refs, out_refs):
        out_refs[0][...] = (in_refs[0][...].astype(F32) + in_refs[1][...].astype(F32)).astype(BF)

    sets = []
    for g, r in zip(gs, recvs):
        _, _, rh, cols = g.shape
        tr = min(rh, 256)
        sets.append((rh // tr,
                     [(g, (N_CHIPS, None, tr, cols), lambda l, c_ref: (0, c_ref[0], l, 0)),
                      (r, (N_CHIPS, tr, cols), lambda l, c_ref: (0, l, 0))],
                     [((N_CHIPS, rh, cols), BF, (N_CHIPS, tr, cols), lambda l, c_ref: (0, l, 0))], add))
    return [o[0] for o in _run_sets(name, [cidx], sets)]


def _chip_add_all(owns, parts, kidx, cidx, name):
    def add(in_refs, out_refs):
        acc = in_refs[0][...].astype(F32)
        for s in range(3):
            acc = acc + in_refs[1][s].astype(F32)
        out_refs[0][...] = acc

    sets = []
    for own, p in zip(owns, parts):
        _, rh, cols = p.shape
        tr = min(rh, 256)
        sets.append((rh // tr,
                     [(own, (None, tr, cols), lambda l, k_ref, c_ref: (k_ref[0], l, 0)),
                      (p, (3, tr, cols), lambda l, k_ref, c_ref: (0, l, 0))],
                     [((2, rh, cols), F32, (None, tr, cols), lambda l, k_ref, c_ref: (c_ref[0], l, 0))], add))
    return [o[0] for o in _run_sets(name, [kidx, cidx], sets)]


def _pair_add(g, recv, cidx, name):
    _, _, rh, cols = g.shape
    tr = min(rh, 256)

    def body(c_ref, g_ref, r_ref, o_ref):
        o_ref[...] = (g_ref[...].astype(F32) + r_ref[...].astype(F32)).astype(BF)

    return pl.pallas_call(
        body, name=name,
        grid_spec=pltpu.PrefetchScalarGridSpec(
            num_scalar_prefetch=1, grid=(rh // tr,),
            in_specs=[pl.BlockSpec((N_CHIPS, None, tr, cols), lambda i, c_ref: (0, c_ref[0], i, 0)),
                      pl.BlockSpec((N_CHIPS, tr, cols), lambda i, c_ref: (0, i, 0))],
            out_specs=pl.BlockSpec((N_CHIPS, tr, cols), lambda i, c_ref: (0, i, 0))),
        out_shape=jax.ShapeDtypeStruct((N_CHIPS, rh, cols), BF),
        compiler_params=_cp(("parallel",)),
    )(cidx, g, recv)


def _chip_send_start(name, sums):
    n = len(sums)
    lands = [lax.empty((3,) + s.shape[1:], BF) for s in sums]

    def run(refs, old, new):
        x, y, c = _place()
        for w in range(n):
            for j, (cx, cy) in enumerate(_other_chips(x, y)):
                _remote(refs[w].at[2 * cx + cy], refs[n + w].at[j], new, 3 * w + j, (cx, cy, c)).start()

    return _split_call(name, list(sums) + lands, run, new=3 * n)


def _chip_send_wait(name, bufs, sems, after):
    n = len(bufs) // 2

    def run(refs, old, new):
        x, y, c = _place()
        for w in range(n):
            for j, (cx, cy) in enumerate(_other_chips(x, y)):
                cp = _remote(refs[w].at[2 * cx + cy], refs[n + w].at[j], old, 3 * w + j, (cx, cy, c))
                cp.wait_send()
                cp.wait_recv()

    thru = _split_call(name, bufs, run, old=sems, after=after)[0]
    return thru[:n], thru[n:]


def _chip_add(own, parts, kidx, cidx, name):
    _, rh, cols = parts.shape
    tr = min(rh, 512)

    def body(k_ref, c_ref, own_ref, p_ref, o_ref):
        acc = own_ref[...].astype(F32)
        for s in range(3):
            acc = acc + p_ref[s].astype(F32)
        o_ref[...] = acc

    return pl.pallas_call(
        body, name=name,
        grid_spec=pltpu.PrefetchScalarGridSpec(
            num_scalar_prefetch=2, grid=(rh // tr,),
            in_specs=[pl.BlockSpec((None, tr, cols), lambda i, k_ref, c_ref: (k_ref[0], i, 0)),
                      pl.BlockSpec((3, tr, cols), lambda i, k_ref, c_ref: (0, i, 0))],
            out_specs=pl.BlockSpec((None, tr, cols), lambda i, k_ref, c_ref: (c_ref[0], i, 0))),
        out_shape=jax.ShapeDtypeStruct((2, rh, cols), F32),
        compiler_params=_cp(("parallel",)),
    )(kidx, cidx, own, parts)


def _pair_swap_start(name, bufs):
    def run(refs, old, new):
        x, y, c = _place()
        for w, ref in enumerate(refs):
            _remote(ref.at[c], ref.at[c], new, w, (x, y, 1 - c)).start()

    return _split_call(name, bufs, run, new=len(bufs))


def _pair_swap_wait(name, bufs, sems, after):
    def run(refs, old, new):
        x, y, c = _place()
        for w, ref in enumerate(refs):
            _remote(ref.at[c], ref.at[c], old, w, (x, y, 1 - c)).wait_send()
            _remote(ref.at[1 - c], ref.at[1 - c], old, w, (x, y, 1 - c)).wait_recv()

    return _split_call(name, bufs, run, old=sems, after=after)[0]


def _peers(x, y, c):
    out = []
    for code in range(1, 8):
        fx, fy, fc = (code >> 2) & 1, (code >> 1) & 1, code & 1
        px = 1 - x if fx else x
        py = 1 - y if fy else y
        pc = 1 - c if fc else c
        out.append((code, (px, py, pc)))
    return out


def _mod_exchange(c_row, ada_w, ada_b4, deps):
    ncol = ada_w.shape[1]

    def body(c_ref, w_ref, b_ref, *rest):
        call_ref, mod_ref, part_ref, send_sems, recv_sems = rest[len(deps):]
        x, y, c = _place()
        k = 2 * x + y
        me = 4 * x + 2 * y + c
        call_ref[pl.ds(me, 1), :] = c_ref[...]
        sends = []
        for code, peer in _peers(x, y, c):
            cp = pltpu.make_async_remote_copy(
                src_ref=c_ref, dst_ref=call_ref.at[pl.ds(me, 1), :],
                send_sem=send_sems.at[code], recv_sem=recv_sems.at[code],
                device_id=peer, device_id_type=MESH)
            cp.start()
            sends.append(cp)
        for code, (px, py, pc) in _peers(x, y, c):
            land = call_ref.at[pl.ds(4 * px + 2 * py + pc, 1), :]
            pltpu.make_async_remote_copy(
                src_ref=land, dst_ref=land, send_sem=send_sems.at[code], recv_sem=recv_sems.at[code],
                device_id=(px, py, pc), device_id_type=MESH).wait_recv()
        call = call_ref[...]
        act = call * _sigmoid(call)
        part = jnp.dot(act, w_ref[...], preferred_element_type=F32,
                       precision=lax.Precision.HIGHEST) + b_ref[pl.ds(k, 1), :]
        part_ref[...] = part
        mod_ref[pl.ds(k, 1), :] = part_ref[pl.ds(me, 1), :]
        chips = [(8 + j, peer) for j, (code, peer) in enumerate(_peers(x, y, c)) if code in (2, 4, 6)]
        for slot, (px, py, pc) in chips:
            cp = pltpu.make_async_remote_copy(
                src_ref=part_ref.at[pl.ds(4 * px + 2 * py + pc, 1), :], dst_ref=mod_ref.at[pl.ds(k, 1), :],
                send_sem=send_sems.at[slot], recv_sem=recv_sems.at[slot],
                device_id=(px, py, pc), device_id_type=MESH)
            cp.start()
            sends.append(cp)
        for slot, (px, py, pc) in chips:
            land = mod_ref.at[pl.ds(2 * px + py, 1), :]
            pltpu.make_async_remote_copy(
                src_ref=land, dst_ref=land, send_sem=send_sems.at[slot], recv_sem=recv_sems.at[slot],
                device_id=(px, py, pc), device_id_type=MESH).wait_recv()
        for cp in sends:
            cp.wait_send()

    vm = pl.BlockSpec(memory_space=pltpu.VMEM)
    return pl.pallas_call(
        body, name="mod_exchange",
        in_specs=[vm, vm, vm] + [ANY] * len(deps), out_specs=[vm, vm],
        out_shape=[jax.ShapeDtypeStruct((8, D_MODEL), F32), jax.ShapeDtypeStruct((N_CHIPS, ncol), F32)],
        scratch_shapes=[pltpu.VMEM((8, ncol), F32), pltpu.SemaphoreType.DMA((16,)),
                        pltpu.SemaphoreType.DMA((16,))],
        compiler_params=_cp(),
    )(c_row, ada_w, ada_b4, *deps)


def _small_exchange(stats, rows, c_all, wsm, msm, vsm):
    ncol = 6 * D_MODEL // N_CHIPS
    ns = len(stats)

    def body(*refs):
        call_ref, w_ref, m_ref, v_ref, gw_ref, loss_ref = refs[ns:ns + 6]
        outs = refs[ns + 6:ns + 30]
        p_ref, g_ref, all_ref, dm_ref, send_sems, recv_sems = refs[ns + 30:]
        x, y, c = _place()
        k = 2 * x + y
        me = 4 * x + 2 * y + c
        for r, (tab, row) in enumerate(rows):
            p_ref[r] = refs[tab][row:row + 1, :]
        all_ref[:, pl.ds(me, 1), :] = p_ref[...]
        sends = []
        for code, peer in _peers(x, y, c):
            cp = pltpu.make_async_remote_copy(
                src_ref=p_ref, dst_ref=all_ref.at[:, pl.ds(me, 1), :],
                send_sem=send_sems.at[code], recv_sem=recv_sems.at[code],
                device_id=peer, device_id_type=MESH)
            cp.start()
            sends.append(cp)
        for code, (px, py, pc) in _peers(x, y, c):
            land = all_ref.at[:, pl.ds(4 * px + 2 * py + pc, 1), :]
            pltpu.make_async_remote_copy(
                src_ref=land, dst_ref=land, send_sem=send_sems.at[code], recv_sem=recv_sems.at[code],
                device_id=(px, py, pc), device_id_type=MESH).wait_recv()
        for cp in sends:
            cp.wait_send()
        tot = [_colsum(all_ref[r]) for r in range(N_PAY)]
        loss_ref[...] = jnp.sum(tot[P_LOSS], axis=1, keepdims=True)
        g_ref[...] = jnp.zeros_like(g_ref)
        for r in range(P_LOSS):
            g_ref[r:r + 1, :] = tot[r]
        g = g_ref[...]
        for kind, tab in enumerate((g,) + _adamw_math(w_ref[...], g, m_ref[...], v_ref[...])):
            for r in range(6):
                outs[6 * kind][:, r * D_MODEL:(r + 1) * D_MODEL] = tab[r:r + 1, :]
            for i in range(5):
                outs[6 * kind + 1 + i][...] = tab[6 + i:7 + i, :]
        half = D_MODEL // 2
        for kk in range(N_CHIPS):
            @pl.when(k == kk)
            def _():
                r0 = 3 * (kk // 2)
                if kk % 2 == 0:
                    dm_ref[:, :D_MODEL] = all_ref[r0]
                    dm_ref[:, D_MODEL:] = all_ref[r0 + 1][:, :half]
                else:
                    dm_ref[:, :half] = all_ref[r0 + 1][:, half:]
                    dm_ref[:, half:] = all_ref[r0 + 2]
        call = call_ref[...]
        act = call * _sigmoid(call)
        gw_ref[...] = lax.dot_general(act, dm_ref[...], (((0,), (0,)), ((), ())),
                                      preferred_element_type=F32, precision=lax.Precision.HIGHEST)

    vm = pl.BlockSpec(memory_space=pltpu.VMEM)
    vectors = [jax.ShapeDtypeStruct((1, 6 * D_MODEL), F32)] + [jax.ShapeDtypeStruct((1, D_MODEL), F32)] * 5
    outs = pl.pallas_call(
        body, name="small_exchange",
        in_specs=[vm] * (ns + 4), out_specs=[vm] * 26,
        out_shape=[jax.ShapeDtypeStruct((D_MODEL, ncol), F32), jax.ShapeDtypeStruct((1, 1), F32)] + vectors * 4,
        scratch_shapes=[pltpu.VMEM((N_PAY, 1, D_MODEL), F32), pltpu.VMEM((16, D_MODEL), F32),
                        pltpu.VMEM((N_PAY, 8, D_MODEL), F32), pltpu.VMEM((8, ncol), F32),
                        pltpu.SemaphoreType.DMA((8,)), pltpu.SemaphoreType.DMA((8,))],
        compiler_params=_cp(),
    )(*stats, c_all, wsm, msm, vsm)
    return outs[0], outs[1], [outs[2 + 6 * kind:8 + 6 * kind] for kind in range(4)]


def _rope_tables(pos_col, inv_freq, sign, dep=None):
    def body(p_ref, f_ref, s_ref, cos_ref, sin_ref):
        ang = p_ref[...].astype(F32) * f_ref[...]
        cos_ref[...] = jnp.cos(ang)
        sin_ref[...] = jnp.sin(ang) * s_ref[...]

    tr = 512
    shp = jax.ShapeDtypeStruct((SEQ, 128), F32)
    body, in_specs, args = _add_dep(
        body, [pl.BlockSpec((tr, 1), lambda i: (i, 0)), pl.BlockSpec((1, 128), lambda i: (0, 0)),
               pl.BlockSpec((1, 128), lambda i: (0, 0))], [pos_col, inv_freq, sign], dep)
    return pl.pallas_call(
        body, name="rope_tables", grid=(SEQ // tr,),
        in_specs=in_specs,
        out_specs=[pl.BlockSpec((tr, 128), lambda i: (i, 0))] * 2, out_shape=[shp, shp],
        compiler_params=_cp(("parallel",)),
    )(*args)


def _resident(shape):
    nd = len(shape)
    return pl.BlockSpec(shape, lambda *_: (0,) * nd, pipeline_mode=pl.Buffered(1))


def _ln_proj(x, vecs, w_in4):
    tm = min(512, SEQ)
    wc = w_in4.shape[2]

    def body(x_ref, vec_ref, w_ref, h_ref, proj_ref):
        xx = x_ref[...]
        g = _row(vec_ref, V_G1) * (1.0 + _row(vec_ref, V_SC1))
        h = (xx * _rms(xx) * g + _row(vec_ref, V_SH1)).astype(BF)
        h_ref[...] = h
        for j in range(N_CHIPS):
            proj_ref[:, j * wc:(j + 1) * wc] = _dot(h, w_ref[j]).astype(BF)

    return pl.pallas_call(
        body, name="ln_proj", grid=(SEQ // tm,),
        in_specs=[pl.BlockSpec((tm, D_MODEL), lambda i: (i, 0)), _resident((16, D_MODEL)),
                  _resident(w_in4.shape)],
        out_specs=[pl.BlockSpec((tm, D_MODEL), lambda i: (i, 0)), pl.BlockSpec((tm, D_IN), lambda i: (i, 0))],
        out_shape=[jax.ShapeDtypeStruct((SEQ, D_MODEL), BF), jax.ShapeDtypeStruct((SEQ, D_IN), BF)],
        compiler_params=_cp(("parallel",)),
    )(x, vecs, w_in4)


def _lane_first(shape):
    lane = lax.broadcasted_iota(jnp.int32, shape, 1)
    return (lane & 32) == 0


def _rot(v, cos, sin_s):
    partner = jnp.where(_lane_first(v.shape), pltpu.roll(v, 96, 1), pltpu.roll(v, 32, 1))
    return v * cos + partner * sin_s


def _rot_t(dv, cos, sin_s):
    t = dv * sin_s
    partner = jnp.where(_lane_first(dv.shape), pltpu.roll(t, 96, 1), pltpu.roll(t, 32, 1))
    return dv * cos + partner


def _ret_mask(lg):
    t = RET_BLOCK
    ii = lax.broadcasted_iota(jnp.int32, (t, t), 0)
    jj = lax.broadcasted_iota(jnp.int32, (t, t), 1)
    dist = jnp.abs(ii - jj).astype(F32)
    future = (jj >> RET_CHUNK_SHIFT) > (ii >> RET_CHUNK_SHIFT)
    return jnp.where(future, 0.0, jnp.exp(lg * dist))


def _ret_masks(lg, mask_ref, head):
    t = RET_BLOCK
    mask = mask_ref[head]
    ti = lax.broadcasted_iota(jnp.int32, (t, 1), 0).astype(F32)
    from_start = jnp.exp(lg * (ti + 1.0))
    to_end = jnp.exp(lg * (t - 1.0 - ti))
    whole = jnp.exp(jnp.full((1, 128), lg * t, F32))
    return mask, from_start, to_end, whole


def _head_lanes(shape, hh):
    lane = lax.broadcasted_iota(jnp.int32, shape, 1)
    return (lane >> 6) == hh


def _ret_specs():
    t = RET_BLOCK
    return dict(
        q=lambda f: pl.BlockSpec((t, 512), lambda n: (f(n), C_QR // 512)),
        k=lambda f: pl.BlockSpec((t, 512), lambda n: (f(n), C_KR // 512)),
        v=lambda f: pl.BlockSpec((t, D_MODEL), lambda n: (f(n), C_VR // D_MODEL)),
        g=lambda f: pl.BlockSpec((t, D_MODEL), lambda n: (f(n), C_GR // D_MODEL)),
        tab=lambda f: pl.BlockSpec((t, 128), lambda n: (f(n), 0)),
        wide=lambda f: pl.BlockSpec((t, D_MODEL), lambda n: (f(n), 0)),
        state=lambda f: pl.BlockSpec((N_PAIRS, None, 2, 128, 128), lambda n: (0, f(n), 0, 0, 0)),
    )


def _ret_fwd(proj, cos, sin_s, gn_g, log_gamma, dep=None):
    t = RET_BLOCK
    nb = SEQ // t

    def body(lg_ref, q_ref, k_ref, v_ref, g_ref, cos_ref, sin_ref, gn_ref, o_ref, retg_ref, st_ref, state, masks):
        @pl.when(pl.program_id(0) == 0)
        def _():
            state[...] = jnp.zeros_like(state)
            for head in range(2 * N_PAIRS):
                masks[head] = _ret_mask(lg_ref[head])

        cos, sn = cos_ref[...], sin_ref[...]
        for p in range(N_PAIRS):
            q = _rot(q_ref[:, 128 * p:128 * (p + 1)].astype(F32), cos, sn)
            k = _rot(k_ref[:, 128 * p:128 * (p + 1)].astype(F32), cos, sn) * QK_SCALE
            for hh in range(2):
                cols = slice(256 * p + 128 * hh, 256 * p + 128 * (hh + 1))
                lg = lg_ref[2 * p + hh]
                mask, from_start, to_end, whole = _ret_masks(lg, masks, 2 * p + hh)
                lanes = _head_lanes(q.shape, hh)
                qm = jnp.where(lanes, q, 0.0)
                km = jnp.where(lanes, k, 0.0)
                vh = v_ref[:, cols]
                sc = _dot_nt(qm.astype(BF), km.astype(BF)) * mask
                st = state[p, hh]
                st_ref[p, hh] = st
                o = _dot(sc.astype(BF), vh) + _dot((qm * from_start).astype(BF), st.astype(BF))
                state[p, hh] = whole * st + _dot_tn((km * to_end).astype(BF), vh)
                d = o - _rowmean(o)
                nh = d * lax.rsqrt(_rowmean(d * d) + EPS)
                gr = g_ref[:, cols].astype(F32)
                o_ref[:, cols] = o
                retg_ref[:, cols] = (gr * _sigmoid(gr) * nh * gn_ref[:, cols]).astype(BF)

    sp = _ret_specs()
    ident = lambda n: n
    body, in_specs, args = _add_dep(
        body, [pl.BlockSpec(memory_space=pltpu.SMEM), sp["q"](ident), sp["k"](ident), sp["v"](ident),
               sp["g"](ident), sp["tab"](ident), sp["tab"](ident), _resident((1, D_MODEL))],
        [log_gamma, proj, proj, proj, proj, cos, sin_s, gn_g], dep)
    return pl.pallas_call(
        body, name="ret_fwd", grid=(nb,),
        in_specs=in_specs,
        out_specs=[sp["wide"](ident), sp["wide"](ident), sp["state"](ident)],
        out_shape=[jax.ShapeDtypeStruct((SEQ, D_MODEL), F32), jax.ShapeDtypeStruct((SEQ, D_MODEL), BF),
                   jax.ShapeDtypeStruct((N_PAIRS, nb, 2, 128, 128), F32)],
        scratch_shapes=[pltpu.VMEM((N_PAIRS, 2, 128, 128), F32),
                        pltpu.VMEM((2 * N_PAIRS, RET_BLOCK, RET_BLOCK), F32)],
        compiler_params=_cp(("arbitrary",)),
    )(*args)


def _ret_bwd(proj, cos, sin_s, dret, states, log_gamma, dep=None):
    t = RET_BLOCK
    nb = SEQ // t

    def body(lg_ref, q_ref, k_ref, v_ref, cos_ref, sin_ref, do_ref, st_ref, dqkv_ref, dstate, masks):
        @pl.when(pl.program_id(0) == 0)
        def _():
            dstate[...] = jnp.zeros_like(dstate)
            for head in range(2 * N_PAIRS):
                masks[head] = _ret_mask(lg_ref[head])

        cos, sn = cos_ref[...], sin_ref[...]
        for p in range(N_PAIRS):
            q = _rot(q_ref[:, 128 * p:128 * (p + 1)].astype(F32), cos, sn)
            k = _rot(k_ref[:, 128 * p:128 * (p + 1)].astype(F32), cos, sn) * QK_SCALE
            dq_rot = jnp.zeros(q.shape, F32)
            dk_rot = jnp.zeros(q.shape, F32)
            for hh in range(2):
                cols = slice(256 * p + 128 * hh, 256 * p + 128 * (hh + 1))
                lg = lg_ref[2 * p + hh]
                mask, from_start, to_end, whole = _ret_masks(lg, masks, 2 * p + hh)
                lanes = _head_lanes(q.shape, hh)
                qm = jnp.where(lanes, q, 0.0)
                km = jnp.where(lanes, k, 0.0)
                qb, kb = qm.astype(BF), km.astype(BF)
                vh = v_ref[:, cols]
                do = do_ref[:, cols]
                sc = (_dot_nt(qb, kb) * mask).astype(BF)
                st = st_ref[p, hh].astype(BF)
                dst = dstate[p, hh]
                dstb = dst.astype(BF)
                k_end = (km * to_end).astype(BF)
                q_start = (qm * from_start).astype(BF)
                dqkv_ref[:, C_VR + 256 * p + 128 * hh:C_VR + 256 * p + 128 * (hh + 1)] = (
                    _dot_tn(sc, do) + _dot(k_end, dstb)).astype(BF)
                dsc = (_dot_nt(do, vh) * mask).astype(BF)
                dq_h = _dot(dsc, kb) + _dot_nt(do, st) * from_start
                dq_rot = dq_rot + jnp.where(lanes, dq_h, 0.0)
                dk_rot = dk_rot + _dot_tn(dsc, qb) + _dot_nt(vh, dstb) * to_end
                dstate[p, hh] = whole * dst + _dot_tn(q_start, do)
            dqkv_ref[:, C_QR + 128 * p:C_QR + 128 * (p + 1)] = _rot_t(dq_rot, cos, sn).astype(BF)
            dqkv_ref[:, C_KR + 128 * p:C_KR + 128 * (p + 1)] = _rot_t(dk_rot * QK_SCALE, cos, sn).astype(BF)

    sp = _ret_specs()
    rev = lambda n: nb - 1 - n
    body, in_specs, args = _add_dep(
        body, [pl.BlockSpec(memory_space=pltpu.SMEM), sp["q"](rev), sp["k"](rev), sp["v"](rev),
               sp["tab"](rev), sp["tab"](rev), sp["wide"](rev), sp["state"](rev)],
        [log_gamma, proj, proj, proj, cos, sin_s, dret, states], dep)
    return pl.pallas_call(
        body, name="ret_bwd", grid=(nb,),
        in_specs=in_specs,
        out_specs=pl.BlockSpec((t, C_GR), lambda n: (rev(n), 0)),
        out_shape=jax.ShapeDtypeStruct((SEQ, C_GR), BF),
        scratch_shapes=[pltpu.VMEM((N_PAIRS, 2, 128, 128), F32),
                        pltpu.VMEM((2 * N_PAIRS, RET_BLOCK, RET_BLOCK), F32)],
        compiler_params=_cp(("arbitrary",)),
    )(*args)


def _stack_heads(v):
    return jnp.concatenate([jnp.where(_head_lanes(v.shape, hh), v, jnp.zeros_like(v)) for hh in range(2)], axis=0)


def _unstack_heads(v):
    t = v.shape[0] // 2
    return jnp.where(_head_lanes((t, v.shape[1]), 0), v[:t], v[t:])


def _sb_masks(t, heads):
    rr = lax.broadcasted_iota(jnp.int32, (t, t), 0)
    cc = lax.broadcasted_iota(jnp.int32, (t, t), 1)
    r2 = lax.broadcasted_iota(jnp.int32, (heads * t, t), 0) & (t - 1)
    c2 = lax.broadcasted_iota(jnp.int32, (heads * t, t), 1)
    return rr, cc, c2 < r2


def _split_dot2(v, tri):
    return _dot(v.astype(BF), tri)


def _log_one_minus_beta(z):
    return -(jnp.maximum(z, 0.0) + jnp.log(1.0 + jnp.exp(-jnp.abs(z))))


def _sb_fwd(proj):
    t, g = SB_BLOCK, SB_GROUP
    nq = SEQ // t
    rows = 2 * g * t

    def body(q_ref, k_ref, v_ref, o_ref, tot_ref, kt_ref):
        i = pl.program_id(1)

        @pl.when(i == 0)
        def _():
            for p in range(g):
                for jj in range(nq):
                    kt_ref[p, jj] = k_ref[jj * t:(jj + 1) * t, 128 * p:128 * (p + 1)].T

        q2 = [_stack_heads((q_ref[:, 128 * p:128 * (p + 1)].astype(F32) * QK_SCALE).astype(BF)) for p in range(g)]
        rr, cc, valid = _sb_masks(t, 2 * g)
        later = (rr > cc).astype(BF)

        def tile(j, carry, diagonal):
            acc, run = carry
            z = jnp.concatenate([_dot(q2[p], kt_ref[p, j]) for p in range(g)], axis=0)
            lm = _log_one_minus_beta(z)
            if diagonal:
                lm = jnp.where(valid, lm, 0.0)
            after = _split_dot2(lm, later)
            a = jnp.exp(z + lm + after + run)
            if diagonal:
                a = jnp.where(valid, a, 0.0)
            ab = a.astype(BF)
            keys = pl.ds(pl.multiple_of(j * t, t), t)
            av = jnp.concatenate([_dot(ab[2 * t * p:2 * t * (p + 1)], v_ref[keys, 128 * p:128 * (p + 1)])
                                  for p in range(g)], axis=0)
            return acc + av, run + after[:, 0:1] + lm[:, 0:1]

        carry = tile(i, (jnp.zeros((rows, 128), F32), jnp.zeros((rows, 1), F32)), True)
        acc, run = lax.fori_loop(0, i, lambda s, cr: tile(i - 1 - s, cr, False), carry)
        run = jnp.broadcast_to(run, (rows, 128))
        for p in range(g):
            o_ref[:, 128 * p:128 * (p + 1)] = _unstack_heads(acc[2 * t * p:2 * t * (p + 1)]).astype(BF)
            tot_ref[:, 128 * p:128 * (p + 1)] = _unstack_heads(run[2 * t * p:2 * t * (p + 1)])

    w = 128 * g
    return pl.pallas_call(
        body, name="sb_fwd", grid=(N_PAIRS // g, nq),
        in_specs=[pl.BlockSpec((t, w), lambda p, i: (i, C_QS // w + p)),
                  pl.BlockSpec((SEQ, w), lambda p, i: (0, C_KS // w + p)),
                  pl.BlockSpec((SEQ, w), lambda p, i: (0, C_VS // w + p))],
        out_specs=[pl.BlockSpec((t, w), lambda p, i: (i, p))] * 2,
        out_shape=[jax.ShapeDtypeStruct((SEQ, 512), BF), jax.ShapeDtypeStruct((SEQ, 512), F32)],
        scratch_shapes=[pltpu.VMEM((g, nq, 128, t), BF)],
        compiler_params=_cp(("parallel", "arbitrary")),
    )(proj, proj, proj)


def _sb_bwd(proj, dsb, tot, dep=None):
    t, g = SB_BLOCK, SB_GROUP
    nq = SEQ // t
    rows = 2 * g * t

    def body(q_ref, k_ref, v_ref, do_ref, tot_ref, dq_ref, dk_ref, dv_ref, kt_ref, vt_ref, dkt_acc, dvt_acc):
        i = pl.program_id(1)

        @pl.when(i == 0)
        def _():
            dkt_acc[...] = jnp.zeros_like(dkt_acc)
            dvt_acc[...] = jnp.zeros_like(dvt_acc)
            for p in range(g):
                for jj in range(nq):
                    kt_ref[p, jj] = k_ref[jj * t:(jj + 1) * t, 128 * p:128 * (p + 1)].T
                    vt_ref[p, jj] = v_ref[jj * t:(jj + 1) * t, 128 * p:128 * (p + 1)].T

        q2 = [_stack_heads((q_ref[:, 128 * p:128 * (p + 1)].astype(F32) * QK_SCALE).astype(BF)) for p in range(g)]
        do2 = [_stack_heads(do_ref[:, 128 * p:128 * (p + 1)]) for p in range(g)]
        q2t = [v.T for v in q2]
        do2t = [v.T for v in do2]
        tots = tot_ref[...]
        total = jnp.concatenate([tots[:, 64 * h:64 * h + 1] for h in range(2 * g)], axis=0)
        rr, cc, valid = _sb_masks(t, 2 * g)
        upto = (rr <= cc).astype(BF)
        before = (rr < cc).astype(BF)

        def part(v, p):
            return v[2 * t * p:2 * t * (p + 1)]

        def tile(j, carry, diagonal):
            dq, run_l, run_g = carry
            z = jnp.concatenate([_dot(q2[p], kt_ref[p, j]) for p in range(g)], axis=0)
            lm = _log_one_minus_beta(z)
            if diagonal:
                lm = jnp.where(valid, lm, 0.0)
            incl = _split_dot2(lm, upto)
            a = jnp.exp(z + lm + (total - (incl + run_l)))
            if diagonal:
                a = jnp.where(valid, a, 0.0)
            gg = a * jnp.concatenate([_dot(do2[p], vt_ref[p, j]) for p in range(g)], axis=0)
            excl = _split_dot2(gg, before)
            dz = gg * jnp.exp(lm) - (excl + run_g) * jnp.exp(z + lm)
            if diagonal:
                dz = jnp.where(valid, dz, 0.0)
            dzb = dz.astype(BF)
            ab = a.astype(BF)
            keys = pl.ds(pl.multiple_of(j * t, t), t)
            for p in range(g):
                dkt_acc[p, j] += _dot(q2t[p], part(dzb, p))
                dvt_acc[p, j] += _dot(do2t[p], part(ab, p))
            dq_t = jnp.concatenate([_dot(part(dzb, p), k_ref[keys, 128 * p:128 * (p + 1)]) for p in range(g)], axis=0)
            return (dq + dq_t, run_l + incl[:, t - 1:t], run_g + excl[:, t - 1:t] + gg[:, t - 1:t])

        zero = jnp.zeros((rows, 1), F32)
        carry = lax.fori_loop(0, i, lambda j, cr: tile(j, cr, False), (jnp.zeros((rows, 128), F32), zero, zero))
        dq = tile(i, carry, True)[0]
        for p in range(g):
            dq_ref[:, 128 * p:128 * (p + 1)] = (_unstack_heads(part(dq, p)) * QK_SCALE).astype(BF)

        @pl.when(i == nq - 1)
        def _():
            for p in range(g):
                for jj in range(nq):
                    dk_ref[jj * t:(jj + 1) * t, 128 * p:128 * (p + 1)] = dkt_acc[p, jj].T.astype(BF)
                    dv_ref[jj * t:(jj + 1) * t, 128 * p:128 * (p + 1)] = dvt_acc[p, jj].T.astype(BF)

    w = 128 * g
    tile_spec = pl.BlockSpec((t, w), lambda p, i: (i, p))
    col_spec = pl.BlockSpec((SEQ, w), lambda p, i: (0, p))
    shp = jax.ShapeDtypeStruct((SEQ, 512), BF)
    body, in_specs, args = _add_dep(
        body, [pl.BlockSpec((t, w), lambda p, i: (i, C_QS // w + p)),
               pl.BlockSpec((SEQ, w), lambda p, i: (0, C_KS // w + p)),
               pl.BlockSpec((SEQ, w), lambda p, i: (0, C_VS // w + p)),
               tile_spec, tile_spec],
        [proj, proj, proj, dsb, tot], dep)
    return pl.pallas_call(
        body, name="sb_bwd", grid=(N_PAIRS // g, nq),
        in_specs=in_specs,
        out_specs=[tile_spec, col_spec, col_spec],
        out_shape=[shp, shp, shp],
        scratch_shapes=[pltpu.VMEM((g, nq, 128, t), BF), pltpu.VMEM((g, nq, 128, t), BF),
                        pltpu.VMEM((g, nq, 128, t), F32), pltpu.VMEM((g, nq, 128, t), F32)],
        compiler_params=_cp(("parallel", "arbitrary")),
    )(*args)


def _mix_out(retg, sb, proj, x, vecs, w_ret, w_sb4, w_out):
    tm, half = min(512, SEQ), 512

    def body(r_ref, s_ref, ar0, ar1, as0, as1, x_ref, vec_ref, wr_ref, ws_ref, wo_ref,
             mix_ref, rb_ref, sbp_ref, y_ref, h1_ref, h2_ref):
        rb = _dot(r_ref[...], wr_ref[...])
        sbv = s_ref[...]
        sbp = jnp.concatenate([_dot(sbv, ws_ref[k]) for k in range(N_CHIPS)], axis=1)
        gate_r = _sigmoid(jnp.concatenate([ar0[...], ar1[...]], axis=1).astype(F32))
        gate_s = _sigmoid(jnp.concatenate([as0[...], as1[...]], axis=1).astype(F32))
        mixed = (gate_r * rb + gate_s * sbp).astype(BF)
        mix_ref[...] = mixed
        rb_ref[...] = rb.astype(BF)
        sbp_ref[...] = sbp.astype(BF)
        y = _dot(mixed, wo_ref[...])
        h1 = x_ref[...] + _row(vec_ref, V_GT1) * (y * _rms(y)) * _row(vec_ref, V_G2)
        g = _row(vec_ref, V_G3) * (1.0 + _row(vec_ref, V_SC2))
        y_ref[...] = y
        h1_ref[...] = h1
        h2_ref[...] = (h1 * _rms(h1) * g + _row(vec_ref, V_SH2)).astype(BF)

    row = pl.BlockSpec((tm, D_MODEL), lambda i: (i, 0))
    gate = lambda c0: pl.BlockSpec((tm, half), lambda i: (i, c0 // half))
    bf = jax.ShapeDtypeStruct((SEQ, D_MODEL), BF)
    f32 = jax.ShapeDtypeStruct((SEQ, D_MODEL), F32)
    return pl.pallas_call(
        body, name="mix_out", grid=(SEQ // tm,),
        in_specs=[row, pl.BlockSpec((tm, 512), lambda i: (i, 0)), gate(C_AR), gate(C_AR + half), gate(C_AS),
                  gate(C_AS + half), row, _resident((16, D_MODEL)), _resident((D_MODEL, D_MODEL)),
                  _resident(w_sb4.shape), _resident((D_MODEL, D_MODEL))],
        out_specs=[row] * 6, out_shape=[bf, bf, bf, f32, f32, bf],
        compiler_params=_cp(("parallel",)),
    )(retg, sb, proj, proj, proj, proj, x, vecs, w_ret, w_sb4, w_out)


def _ffn_fwd_loss(h2, h1, target, vecs, w_ff14, w_ff24):
    tm = 256

    def body(h2_ref, h1_ref, t_ref, vec_ref, w1_ref, w2_ref, u_ref, a_ref, dout_ref, df_ref, st_ref):
        @pl.when(pl.program_id(0) == 0)
        def _():
            st_ref[...] = jnp.zeros_like(st_ref)

        hb = h2_ref[...]
        f = jnp.zeros((tm, D_MODEL), F32)
        for k in range(N_CHIPS):
            cols = slice(k * D_MODEL, (k + 1) * D_MODEL)
            u = _dot(hb, w1_ref[k])
            r = jnp.maximum(u, 0.0)
            act = (r * r).astype(BF)
            u_ref[:, cols] = u.astype(BF)
            a_ref[:, cols] = act
            f = f + _dot(act, w2_ref[k])
        r4 = _rms(f)
        fn = f * r4
        gt2, g4 = _row(vec_ref, V_GT2), _row(vec_ref, V_G4)
        diff = h1_ref[...] + gt2 * fn * g4 - t_ref[...]
        dout = diff * (1.0 / D_MODEL)
        dfn = dout * gt2 * g4
        dout_ref[...] = dout
        df_ref[...] = (r4 * (dfn - fn * _rowmean(dfn * fn))).astype(BF)
        st_ref[0:1, :] += _colsum(dout * fn * g4)
        st_ref[1:2, :] += _colsum(dout * gt2 * fn)
        st_ref[2:3, :] += _colsum(diff * diff) * (0.5 / D_MODEL)

    row = pl.BlockSpec((tm, D_MODEL), lambda i: (i, 0))
    wide = pl.BlockSpec((tm, D_FF), lambda i: (i, 0))
    return pl.pallas_call(
        body, name="ffn_fwd_loss", grid=(SEQ // tm,),
        in_specs=[row, row, row, _resident((16, D_MODEL)), _resident(w_ff14.shape), _resident(w_ff24.shape)],
        out_specs=[wide, wide, row, row, pl.BlockSpec((8, D_MODEL), lambda i: (0, 0))],
        out_shape=[jax.ShapeDtypeStruct((SEQ, D_FF), BF), jax.ShapeDtypeStruct((SEQ, D_FF), BF),
                   jax.ShapeDtypeStruct((SEQ, D_MODEL), F32), jax.ShapeDtypeStruct((SEQ, D_MODEL), BF),
                   jax.ShapeDtypeStruct((8, D_MODEL), F32)],
        compiler_params=_cp(("arbitrary",)),
    )(h2, h1, target, vecs, w_ff14, w_ff24)


def _ffn_bwd(df, u, h1, y, dout, vecs, w_ff14, w_ff24):
    tm = 256

    def body(df_ref, u_ref, h1_ref, y_ref, dout_ref, vec_ref, w1_ref, w2_ref, du_ref, dh1_ref, dy_ref, st_ref):
        @pl.when(pl.program_id(0) == 0)
        def _():
            st_ref[...] = jnp.zeros_like(st_ref)

        dfb = df_ref[...]
        dh2 = jnp.zeros((tm, D_MODEL), F32)
        for k in range(N_CHIPS):
            cols = slice(k * D_MODEL, (k + 1) * D_MODEL)
            da = _dot_nt(dfb, w2_ref[k])
            du = (da * (2.0 * jnp.maximum(u_ref[:, cols].astype(F32), 0.0))).astype(BF)
            du_ref[:, cols] = du
            dh2 = dh2 + _dot_nt(du, w1_ref[k])
        h1 = h1_ref[...]
        r3 = _rms(h1)
        hn3 = h1 * r3
        g3, sc2 = _row(vec_ref, V_G3), _row(vec_ref, V_SC2)
        dhn3 = dh2 * g3 * (1.0 + sc2)
        dh1 = dout_ref[...] + r3 * (dhn3 - hn3 * _rowmean(dhn3 * hn3))
        y = y_ref[...]
        r2 = _rms(y)
        yn = y * r2
        gt1, g2 = _row(vec_ref, V_GT1), _row(vec_ref, V_G2)
        dyn = dh1 * gt1 * g2
        dh1_ref[...] = dh1
        dy_ref[...] = (r2 * (dyn - yn * _rowmean(dyn * yn))).astype(BF)
        st_ref[0:1, :] += _colsum(dh2)
        st_ref[1:2, :] += _colsum(dh2 * hn3 * g3)
        st_ref[2:3, :] += _colsum(dh2 * hn3 * (1.0 + sc2))
        st_ref[3:4, :] += _colsum(dh1 * yn * g2)
        st_ref[4:5, :] += _colsum(dh1 * gt1 * yn)

    row = pl.BlockSpec((tm, D_MODEL), lambda i: (i, 0))
    wide = pl.BlockSpec((tm, D_FF), lambda i: (i, 0))
    return pl.pallas_call(
        body, name="ffn_bwd", grid=(SEQ // tm,),
        in_specs=[row, wide, row, row, row, _resident((16, D_MODEL)), _resident(w_ff14.shape),
                  _resident(w_ff24.shape)],
        out_specs=[wide, row, row, pl.BlockSpec((8, D_MODEL), lambda i: (0, 0))],
        out_shape=[jax.ShapeDtypeStruct((SEQ, D_FF), BF), jax.ShapeDtypeStruct((SEQ, D_MODEL), F32),
                   jax.ShapeDtypeStruct((SEQ, D_MODEL), BF), jax.ShapeDtypeStruct((8, D_MODEL), F32)],
        compiler_params=_cp(("arbitrary",)),
    )(df, u, h1, y, dout, vecs, w_ff14, w_ff24)


def _mix_ret_bwd(dy, proj, rb, sbp, o_raw, gn_g, w_out, w_sb4, w_ret):
    tm, half = min(512, SEQ), 512

    def body(dy_ref, ar0, ar1, as0, as1, rb_ref, sbp_ref, g_ref, o_ref, gn_ref, wo_ref, ws_ref, wr_ref,
             drb_ref, dsbp_ref, da_ref, dsb_ref, dret_ref, dgr_ref, st_ref):
        @pl.when(pl.program_id(0) == 0)
        def _():
            st_ref[...] = jnp.zeros_like(st_ref)

        dm_all = _dot_nt(dy_ref[...], wo_ref[...])
        dsb = jnp.zeros((tm, 512), F32)
        drbs = []
        for hf, (ar_ref, as_ref) in enumerate(((ar0, as0), (ar1, as1))):
            cols = slice(half * hf, half * (hf + 1))
            dm = dm_all[:, cols]
            sr = _sigmoid(ar_ref[...].astype(F32))
            ss = _sigmoid(as_ref[...].astype(F32))
            dsbp = (dm * ss).astype(BF)
            drbs.append((dm * sr).astype(BF))
            dsbp_ref[:, cols] = dsbp
            da_ref[:, cols] = (dm * rb_ref[:, cols].astype(F32) * sr * (1.0 - sr)).astype(BF)
            da_ref[:, D_MODEL + half * hf:D_MODEL + half * (hf + 1)] = (
                dm * sbp_ref[:, cols].astype(F32) * ss * (1.0 - ss)).astype(BF)
            dsb = dsb + _dot_nt(dsbp[:, :256], ws_ref[2 * hf]) + _dot_nt(dsbp[:, 256:], ws_ref[2 * hf + 1])
        dsb_ref[...] = dsb.astype(BF)
        drb = jnp.concatenate(drbs, axis=1)
        drb_ref[...] = drb
        dretg = _dot_nt(drb, wr_ref[...])
        for gi in range(D_MODEL // 128):
            cols = slice(128 * gi, 128 * (gi + 1))
            o = o_ref[:, cols]
            d = o - _rowmean(o)
            rstd = lax.rsqrt(_rowmean(d * d) + EPS)
            nh = d * rstd
            gain = gn_ref[:, cols]
            gr = g_ref[:, cols].astype(F32)
            sg = _sigmoid(gr)
            dg = dretg[:, cols]
            dgn = dg * gr * sg
            dnh = dgn * gain
            dgr_ref[:, cols] = (dg * nh * gain * sg * (1.0 + gr * (1.0 - sg))).astype(BF)
            dret_ref[:, cols] = (rstd * (dnh - _rowmean(dnh) - nh * _rowmean(dnh * nh))).astype(BF)
            st_ref[0:1, cols] += _colsum(dgn * nh)

    row = pl.BlockSpec((tm, D_MODEL), lambda i: (i, 0))
    gate = lambda c0: pl.BlockSpec((tm, half), lambda i: (i, c0 // half))
    shp = jax.ShapeDtypeStruct((SEQ, D_MODEL), BF)
    return pl.pallas_call(
        body, name="mix_ret_bwd", grid=(SEQ // tm,),
        in_specs=[row, gate(C_AR), gate(C_AR + half), gate(C_AS), gate(C_AS + half), row, row,
                  pl.BlockSpec((tm, D_MODEL), lambda i: (i, C_GR // D_MODEL)), row, _resident((1, D_MODEL)),
                  _resident((D_MODEL, D_MODEL)), _resident(w_sb4.shape), _resident((D_MODEL, D_MODEL))],
        out_specs=[row, row, pl.BlockSpec((tm, 2 * D_MODEL), lambda i: (i, 0)), pl.BlockSpec((tm, 512), lambda i: (i, 0)),
                   row, row, pl.BlockSpec((8, D_MODEL), lambda i: (0, 0))],
        out_shape=[shp, shp, jax.ShapeDtypeStruct((SEQ, 2 * D_MODEL), BF), jax.ShapeDtypeStruct((SEQ, 512), BF),
                   shp, shp, jax.ShapeDtypeStruct((8, D_MODEL), F32)],
        compiler_params=_cp(("arbitrary",)),
    )(dy, proj, proj, proj, proj, rb, sbp, proj, o_raw, gn_g, w_out, w_sb4, w_ret)


def _dproj_segments(widths):
    wc = D_IN // N_CHIPS
    segs, start = [], 0
    for pi, width in enumerate(widths):
        lo = start
        while lo < start + width:
            j = lo // wc
            hi = min(start + width, (j + 1) * wc)
            segs.append((j, lo - j * wc, pi, lo - start, hi - lo))
            lo = hi
        start += width
    assert start == D_IN
    return segs


def _in_proj_bwd(pieces, x, dh1, vecs, w_in4, updates, dep=None):
    tm = 256
    steps = SEQ // tm
    n, nu = len(pieces), len(updates)
    segs = _dproj_segments([p.shape[1] for p in pieces])

    def body(*refs):
        x_ref, dh1_ref, vec_ref, w_ref = refs[n:n + 4]
        upd_in = refs[n + 4:n + 4 + 4 * nu]
        dx_ref, st_ref = refs[n + 4 + 4 * nu:n + 6 + 4 * nu]
        upd_out = refs[n + 6 + 4 * nu:]
        for u in range(nu):
            w_u, m_u, v_u, g_u = upd_in[4 * u:4 * u + 4]
            go_u, d_u, mo_u, vo_u = upd_out[4 * u:4 * u + 4]
            gg = g_u[...]
            go_u[...] = gg
            d_u[...], mo_u[...], vo_u[...] = _adamw_math(w_u[...], gg, m_u[...], v_u[...])

        @pl.when(pl.program_id(0) == 0)
        def _():
            st_ref[...] = jnp.zeros_like(st_ref)

        dh = jnp.zeros((tm, D_MODEL), F32)
        for j, so, pi, po, width in segs:
            dh = dh + _dot_nt(refs[pi][:, po:po + width], w_ref[j, :, so:so + width])
        xx = x_ref[...]
        r1 = _rms(xx)
        xn = xx * r1
        g1, sc1 = _row(vec_ref, V_G1), _row(vec_ref, V_SC1)
        dxn = dh * g1 * (1.0 + sc1)
        dx_ref[...] = dh1_ref[...] + r1 * (dxn - xn * _rowmean(dxn * xn))
        st_ref[0:1, :] += _colsum(dh)
        st_ref[1:2, :] += _colsum(dh * xn * g1)
        st_ref[2:3, :] += _colsum(dh * xn * (1.0 + sc1))

    row = pl.BlockSpec((tm, D_MODEL), lambda i: (i, 0))
    upd_specs, upd_shapes, upd_args = [], [], []
    for arrays in updates:
        rows, cols = arrays[0].shape
        upd_specs += [pl.BlockSpec((rows // steps, cols), lambda i: (i, 0))] * 4
        upd_shapes += [jax.ShapeDtypeStruct((rows, cols), F32)] * 4
        upd_args += list(arrays)
    body, in_specs, args = _add_dep(
        body, [pl.BlockSpec((tm, p.shape[1]), lambda i: (i, 0)) for p in pieces] + [
            row, row, _resident((16, D_MODEL)), _resident(w_in4.shape)] + upd_specs,
        list(pieces) + [x, dh1, vecs, w_in4] + upd_args, dep)
    outs = pl.pallas_call(
        body, name="in_proj_bwd", grid=(steps,),
        in_specs=in_specs,
        out_specs=[row, pl.BlockSpec((8, D_MODEL), lambda i: (0, 0))] + upd_specs,
        out_shape=[jax.ShapeDtypeStruct((SEQ, D_MODEL), F32), jax.ShapeDtypeStruct((8, D_MODEL), F32)] + upd_shapes,
        compiler_params=_cp(("arbitrary",)),
    )(*args)
    return outs[0], outs[1], [tuple(outs[2 + 4 * u:6 + 4 * u]) for u in range(nu)]


def _grad_w_in(h, pieces, dep=None):
    ta = 512
    n = len(pieces)
    segs = _dproj_segments([p.shape[1] for p in pieces])

    def body(*refs):
        h_ref, o_ref = refs[n], refs[n + 1]
        hh = h_ref[...]
        for j, so, pi, po, width in segs:
            o_ref[j, :, so:so + width] = _dot_tn(hh, refs[pi][:, po:po + width]).astype(BF)

    body, in_specs, args = _add_dep(
        body, [_resident(p.shape) for p in pieces] + [pl.BlockSpec((SEQ, ta), lambda i: (0, i))],
        list(pieces) + [h], dep)
    return pl.pallas_call(
        body, name="grad_w_in", grid=(D_MODEL // ta,),
        in_specs=in_specs,
        out_specs=pl.BlockSpec((N_CHIPS, ta, D_IN // N_CHIPS), lambda i: (0, i, 0)),
        out_shape=jax.ShapeDtypeStruct((N_CHIPS, D_MODEL, D_IN // N_CHIPS), BF),
        compiler_params=_cp(("parallel",)),
    )(*args)


def _weight_grad(a, b, ta, tb, col_sharded, name, dep=None):
    ka, nb_ = a.shape[1], b.shape[1]

    def body(a_ref, b_ref, o_ref):
        o_ref[...] = _dot_tn(a_ref[...], b_ref[...]).astype(BF)

    body, in_specs, args = _add_dep(
        body, [pl.BlockSpec((SEQ, ta), lambda i, j: (0, i)), pl.BlockSpec((SEQ, tb), lambda i, j: (0, j))],
        [a, b], dep)

    if col_sharded:
        per = nb_ // N_CHIPS // tb
        out_shape = jax.ShapeDtypeStruct((N_CHIPS, ka, nb_ // N_CHIPS), BF)
        out_spec = pl.BlockSpec((None, ta, tb), lambda i, j: (j // per, i, j % per))
    else:
        per = ka // N_CHIPS // ta
        out_shape = jax.ShapeDtypeStruct((N_CHIPS, ka // N_CHIPS, nb_), BF)
        out_spec = pl.BlockSpec((None, ta, tb), lambda i, j: (i // per, i % per, j))
    return pl.pallas_call(
        body, name=name, grid=(ka // ta, nb_ // tb),
        in_specs=in_specs, out_specs=out_spec, out_shape=out_shape,
        compiler_params=_cp(("parallel", "parallel")),
    )(*args)


def _rope_constants():
    freq = np.float32(ROPE_BASE) ** (-np.arange(0, 64, 2, dtype=np.float32) / np.float32(64))
    inv = np.tile(freq.astype(np.float32), 4).reshape(1, 128)
    sign = np.tile(np.concatenate([-np.ones(32, np.float32), np.ones(32, np.float32)]), 2).reshape(1, 128)
    return jnp.asarray(inv), jnp.asarray(sign)


def _log_gamma():
    return jnp.asarray(np.log1p(-(2.0 ** (-5.0 - np.arange(8, dtype=np.float64)))).astype(np.float32))


def _halves(g):
    return g.reshape(N_CHIPS, 2, g.shape[1] // 2, g.shape[2])


def kernel(x, c, positions, ada_w, ada_b, pre_mix_g, post_mix_g, pre_ffn_g, post_ffn_g, w_in, ret_gn_g, w_ret_branch, w_sb_branch, w_out, w_ff1, w_ff2, loss_target, m_ada_w, m_ada_b, m_pre_mix_g, m_post_mix_g, m_pre_ffn_g, m_post_ffn_g, m_w_in, m_ret_gn_g, m_w_ret_branch, m_w_sb_branch, m_w_out, m_w_ff1, m_w_ff2, v_ada_w, v_ada_b, v_pre_mix_g, v_post_mix_g, v_pre_ffn_g, v_post_ffn_g, v_w_in, v_ret_gn_g, v_w_ret_branch, v_w_sb_branch, v_w_out, v_w_ff1, v_w_ff2):
    names = ["w_in", "w_ret", "w_sb", "w_out", "w_ff1", "w_ff2"]
    big = dict(zip(names, [w_in, w_ret_branch, w_sb_branch, w_out, w_ff1, w_ff2]))
    big_m = dict(zip(names, [m_w_in, m_w_ret_branch, m_w_sb_branch, m_w_out, m_w_ff1, m_w_ff2]))
    big_v = dict(zip(names, [v_w_in, v_w_ret_branch, v_w_sb_branch, v_w_out, v_w_ff1, v_w_ff2]))
    rest = names[1:]
    cidx = lax.axis_index("c").astype(jnp.int32).reshape(1)
    kidx = (2 * lax.axis_index("x") + lax.axis_index("y")).astype(jnp.int32).reshape(1)
    x0, target = x[0], loss_target[0]

    buf_in, sem_in, tok_in = _gather_start("gather_in_start", [_cast_bf16(w_in[0], kidx, c, "cast_w_in")])
    rest_bufs = [_cast_bf16(big[nm][0], kidx, tok_in, "cast_" + nm) for nm in rest]
    inv_freq, sign = _rope_constants()
    lg = _log_gamma()
    cos, sin_s = _rope_tables(positions.reshape(SEQ, 1), inv_freq, sign, dep=tok_in)

    def table(b6, g5):
        return jnp.concatenate([b6.reshape(6, D_MODEL)] + g5 + [jnp.zeros((5, D_MODEL), F32)], axis=0)

    wsm = table(ada_b, [pre_mix_g, post_mix_g, pre_ffn_g, post_ffn_g, ret_gn_g])
    msm = table(m_ada_b, [m_pre_mix_g, m_post_mix_g, m_pre_ffn_g, m_post_ffn_g, m_ret_gn_g])
    vsm = table(v_ada_b, [v_pre_mix_g, v_post_mix_g, v_pre_ffn_g, v_post_ffn_g, v_ret_gn_g])
    c_all, mod4 = _mod_exchange(c, ada_w[0], ada_b.reshape(N_CHIPS, -1), rest_bufs + [cos, wsm, msm, vsm])
    vecs = jnp.concatenate([mod4.reshape(6, D_MODEL), pre_mix_g, post_mix_g, pre_ffn_g, post_ffn_g,
                            jnp.zeros((6, D_MODEL), F32)], axis=0)
    buf_in, sem_in, tok_in = _gather_pass("gather_in_pass", buf_in, sem_in, vecs)
    buf_rest, sem_rest, tok_rest = _gather_start("gather_rest_start", rest_bufs, after=tok_in)
    (w_in4,) = _gather_finish("gather_in_finish", buf_in, sem_in, tok_rest)

    h, proj = _ln_proj(x0, vecs, w_in4)
    sb, tot = _sb_fwd(proj)
    buf_rest, sem_rest, tok_rest = _gather_pass("gather_rest_pass", buf_rest, sem_rest, sb)
    o_raw, retg, states = _ret_fwd(proj, cos, sin_s, ret_gn_g, lg, dep=tok_rest)
    w_ret4, w_sb4, w_out4, w_ff14, w_ff24 = _gather_finish("gather_rest_finish", buf_rest, sem_rest, retg)
    w_ret = w_ret4.reshape(D_MODEL, D_MODEL)
    w_out2 = w_out4.reshape(D_MODEL, D_MODEL)
    mixed, rb, sbp, y, h1, h2 = _mix_out(retg, sb, proj, x0, vecs, w_ret, w_sb4, w_out2)
    u, act, dout, df, st_a = _ffn_fwd_loss(h2, h1, target, vecs, w_ff14, w_ff24)

    du, dh1, dy, st_b = _ffn_bwd(df, u, h1, y, dout, vecs, w_ff14, w_ff24)
    grads = {"w_ff2": _weight_grad(act, df, 512, 1024, False, "grad_w_ff2")}
    grads["w_ff1"] = _weight_grad(h2, du, 1024, 1024, True, "grad_w_ff1")
    drb, dsbp, da, dsb, dret, dg_r, st_c = _mix_ret_bwd(dy, proj, rb, sbp, o_raw, ret_gn_g, w_out2, w_sb4, w_ret)
    grads["w_out"] = _weight_grad(mixed, dy, 256, 1024, False, "grad_w_out")
    grads["w_ret"] = _weight_grad(retg, drb, 256, 1024, False, "grad_w_ret")
    grads["w_sb"] = _weight_grad(sb, dsbp, 512, 256, True, "grad_w_sb")

    bufs, sems, tok = _pair_send_start("rs_rest_pair_send", [_halves(grads[nm]) for nm in rest])
    dqkv_r = _ret_bwd(proj, cos, sin_s, dret, states, lg, dep=tok)
    mine, theirs = _pair_send_wait("rs_rest_pair_recv", bufs, sems, dqkv_r)
    pair_sums = _pair_add_all(mine, theirs, cidx, "pair_add_rest")
    bufs, sems, tok = _chip_send_start("rs_rest_chip_send", pair_sums)
    dq_s, dk_s, dv_s = _sb_bwd(proj, dsb, tot, dep=tok)
    own, parts = _chip_send_wait("rs_rest_chip_recv", bufs, sems, dq_s)
    sums = _chip_add_all(own, parts, kidx, cidx, "chip_add_rest")
    bufs, sems, tok = _pair_swap_start("rs_rest_pair_swap", sums)
    dproj = [dqkv_r, dg_r, dq_s, dk_s, dv_s, da]
    g_in = _grad_w_in(h, dproj, dep=tok)
    full_rest = _pair_swap_wait("rs_rest_pair_swapped", bufs, sems, g_in)

    full_rest = dict(zip(rest, full_rest))
    bufs, sems, tok = _pair_send_start("rs_in_pair_send", [_halves(g_in)])
    small_w = ("w_out", "w_sb", "w_ret")
    out = dict(zip(small_w, _adamw_all(
        [(big[nm][0], big_m[nm][0], big_v[nm][0], full_rest[nm].reshape(big[nm].shape[1:])) for nm in small_w],
        "adamw_small", tok)))
    mine, theirs = _pair_send_wait("rs_in_pair_recv", bufs, sems, out["w_ret"][1])
    bufs, sems, tok = _chip_send_start("rs_in_chip_send", [_pair_add(mine[0], theirs[0], cidx, "pair_add_w_in")])
    riding = ("w_ff2", "w_ff1")
    dx, st_d, updated = _in_proj_bwd(
        dproj, x0, dh1, vecs, w_in4,
        [(big[nm][0], big_m[nm][0], big_v[nm][0], full_rest[nm].reshape(big[nm].shape[1:])) for nm in riding],
        dep=tok)
    out.update(zip(riding, updated))

    a_, b_, c_, d_ = range(4)
    payload_rows = [(d_, 0), (d_, 1), (b_, 3), (b_, 0), (b_, 1), (a_, 0),
                    (d_, 2), (b_, 4), (b_, 2), (a_, 1), (c_, 0), (a_, 2)]
    g_ada, loss, small = _small_exchange([st_a, st_b, st_c, st_d], payload_rows, c_all, wsm, msm, vsm)
    ada_out = _adamw(ada_w[0], m_ada_w[0], v_ada_w[0], g_ada, "adamw_ada_w")

    own, parts = _chip_send_wait("rs_in_chip_recv", bufs, sems, ada_out[1])
    bufs, sems, tok = _pair_swap_start(
        "rs_in_pair_swap", [_chip_add(own[0], parts[0], kidx, cidx, "chip_add_w_in")])
    (full_in,) = _pair_swap_wait("rs_in_pair_swapped", bufs, sems, tok)
    out["w_in"] = _adamw(w_in[0], m_w_in[0], v_w_in[0], full_in.reshape(w_in.shape[1:]), "adamw_w_in")

    def ordered(which):
        sm = small[which]
        bg = [out[nm][which][None] for nm in names]
        return [ada_out[which][None], sm[0], sm[1], sm[2], sm[3], sm[4], bg[0], sm[5]] + bg[1:]

    return (loss.reshape(()), dx[None], *ordered(0), *ordered(1), *ordered(2), *ordered(3))
```

```python
import functools

import numpy as np
import jax
import jax.numpy as jnp
from jax import lax
from jax.experimental import pallas as pl
from jax.experimental.pallas import tpu as pltpu

SEQ = 2048
D_MODEL = 1024
D_IN = 6656
D_FF = 4096
N_CHIPS = 4
EPS = 1e-6
ROPE_BASE = 10000.0
RET_BLOCK = 256
RET_CHUNK_SHIFT = 6
SB_BLOCK = 256
QK_SCALE = 0.125
N_PAIRS = 4
SB_GROUP = 4

ADAM_LR = 0.001
ADAM_B1 = 0.9
ADAM_B2 = 0.999
ADAM_EPS = 1e-08
ADAM_WD = 0.01
ADAM_STEP = 10

BF = jnp.bfloat16
F32 = jnp.float32
MESH = pl.DeviceIdType.MESH
VMEM_LIMIT = 56 * 1024 * 1024
ANY = pl.BlockSpec(memory_space=pl.ANY)

C_QR, C_KR, C_VR, C_GR, C_QS, C_KS, C_VS, C_AR, C_AS = 0, 512, 1024, 2048, 3072, 3584, 4096, 4608, 5632

V_SH1, V_SC1, V_GT1, V_SH2, V_SC2, V_GT2, V_G1, V_G2, V_G3, V_G4 = range(10)
P_DSH1, P_DSC1, P_DGT1, P_DSH2, P_DSC2, P_DGT2, P_DG1, P_DG2, P_DG3, P_DG4, P_DGN, P_LOSS = range(12)
N_PAY = 12


def _cp(sem=None, **kw):
    if sem is not None:
        kw["dimension_semantics"] = sem
    return pltpu.CompilerParams(vmem_limit_bytes=VMEM_LIMIT, **kw)


def _dot(a, b):
    return jnp.dot(a, b, preferred_element_type=F32)


def _dot_nt(a, b):
    return lax.dot_general(a, b, (((1,), (1,)), ((), ())), preferred_element_type=F32)


def _dot_tn(a, b):
    return lax.dot_general(a, b, (((0,), (0,)), ((), ())), preferred_element_type=F32)


def _row(ref, i):
    return ref[i:i + 1, :]


def _rms(v):
    return lax.rsqrt(jnp.mean(v * v, axis=1, keepdims=True) + EPS)


def _colsum(v):
    return jnp.sum(v, axis=0, keepdims=True)


def _rowmean(v):
    return jnp.mean(v, axis=1, keepdims=True)


def _sigmoid(v):
    return 1.0 / (1.0 + jnp.exp(-v))


def _cast_bf16(w, kidx, dep, name):
    rows, cols = w.shape
    tr = min(rows, 512)

    def body(k_ref, w_ref, dep_ref, o_ref):
        o_ref[...] = w_ref[...].astype(BF)

    return pl.pallas_call(
        body, name=name,
        grid_spec=pltpu.PrefetchScalarGridSpec(
            num_scalar_prefetch=1, grid=(rows // tr,),
            in_specs=[pl.BlockSpec((tr, cols), lambda i, k_ref: (i, 0)), ANY],
            out_specs=pl.BlockSpec((None, tr, cols), lambda i, k_ref: (k_ref[0], i, 0))),
        out_shape=jax.ShapeDtypeStruct((N_CHIPS, rows, cols), BF),
        compiler_params=_cp(("parallel",)),
    )(kidx, w, dep)


def _adamw_math(w, g, m, v):
    m = ADAM_B1 * m + (1.0 - ADAM_B1) * g
    v = ADAM_B2 * v + (1.0 - ADAM_B2) * (g * g)
    m_hat = m / (1.0 - ADAM_B1 ** ADAM_STEP)
    v_hat = v / (1.0 - ADAM_B2 ** ADAM_STEP)
    delta = -ADAM_LR * (m_hat / (jnp.sqrt(v_hat) + ADAM_EPS) + ADAM_WD * w)
    return delta, m, v


def _adamw(w, m, v, g, name, dep=None):
    rows, cols = w.shape
    tr = min(rows, 256)

    def body(w_ref, m_ref, v_ref, g_ref, go_ref, d_ref, mo_ref, vo_ref):
        gg = g_ref[...]
        d, mm, vv = _adamw_math(w_ref[...], gg, m_ref[...], v_ref[...])
        go_ref[...] = gg
        d_ref[...] = d
        mo_ref[...] = mm
        vo_ref[...] = vv

    spec = pl.BlockSpec((tr, cols), lambda i: (i, 0))
    shp = jax.ShapeDtypeStruct((rows, cols), F32)
    body, in_specs, args = _add_dep(body, [spec] * 4, [w, m, v, g], dep)
    return pl.pallas_call(
        body, name=name, grid=(rows // tr,),
        in_specs=in_specs, out_specs=[spec] * 4, out_shape=[shp] * 4,
        compiler_params=_cp(("parallel",)),
    )(*args)


def _place():
    x, y, c = lax.axis_index("x"), lax.axis_index("y"), lax.axis_index("c")
    return x, y, c


HBM = pl.BlockSpec(memory_space=pltpu.HBM)
SEM = pl.BlockSpec(memory_space=pltpu.SEMAPHORE)
EFFECT = pltpu.SideEffectType.DATAFLOW_SIDE_EFFECTING


def _add_dep(body, in_specs, args, dep):
    if dep is None:
        return body, list(in_specs), list(args)
    n = len(args)

    def wrapped(*refs):
        body(*refs[:n], *refs[n + 1:])

    return wrapped, list(in_specs) + [ANY], list(args) + [dep]


def _split_call(name, bufs, run, old=None, after=None, new=0):
    nb = len(bufs)
    n_old = 2 if old is not None else 0
    n_in = nb + n_old + (1 if after is not None else 0)

    def body(*refs):
        old_sems = (refs[nb], refs[nb + 1]) if old is not None else None
        new_sems = (refs[n_in], refs[n_in + 1]) if new else None
        run(refs[:nb], old_sems, new_sems)
        if new:
            refs[-1][...] = jnp.zeros_like(refs[-1])

    in_specs = [HBM] * nb + [SEM] * n_old + ([ANY] if after is not None else [])
    out_shape = [pltpu.SemaphoreType.DMA((new,))] * 2 if new else []
    out_specs = [SEM, SEM] if new else []
    out_shape += [pltpu.HBM(b.shape, b.dtype) for b in bufs]
    out_specs += [HBM] * nb
    if new:
        out_shape.append(jax.ShapeDtypeStruct((8, 128), F32))
        out_specs.append(pl.BlockSpec(memory_space=pltpu.VMEM))
    first = 2 if new else 0
    args = [pltpu.with_memory_space_constraint(b, pltpu.HBM) for b in bufs]
    if old is not None:
        args += [old[0], old[1]]
    if after is not None:
        args.append(after)
    outs = pl.pallas_call(
        body, name=name, in_specs=tuple(in_specs), out_specs=tuple(out_specs), out_shape=tuple(out_shape),
        input_output_aliases={i: i + first for i in range(nb)},
        compiler_params=pltpu.CompilerParams(has_side_effects=EFFECT),
    )(*args)
    thru = list(outs[first:first + nb])
    if new:
        return thru, (outs[0], outs[1]), outs[-1]
    return thru, None, None


def _remote(part_src, part_dst, sems, i, to):
    return pltpu.make_async_remote_copy(src_ref=part_src, dst_ref=part_dst, send_sem=sems[0].at[i],
                                        recv_sem=sems[1].at[i], device_id=to, device_id_type=MESH)


def _other_chips(x, y):
    return [(1 - x, y), (x, 1 - y), (1 - x, 1 - y)]


def _gather_start(name, bufs, after=None):
    def run(refs, old, new):
        x, y, c = _place()
        k = 2 * x + y
        for w, ref in enumerate(refs):
            rh = bufs[w].shape[1] // 2
            part = ref.at[k, pl.ds(c * rh, rh)]
            for j, (cx, cy) in enumerate(_other_chips(x, y)):
                _remote(part, part, new, 3 * w + j, (cx, cy, c)).start()

    return _split_call(name, bufs, run, after=after, new=3 * len(bufs))


def _gather_pass(name, bufs, sems, after):
    def run(refs, old, new):
        x, y, c = _place()
        k = 2 * x + y
        sib = (x, y, 1 - c)
        for w, ref in enumerate(refs):
            rh = bufs[w].shape[1] // 2
            for j, (cx, cy) in enumerate(_other_chips(x, y)):
                land = ref.at[2 * cx + cy, pl.ds(c * rh, rh)]
                _remote(land, land, old, 3 * w + j, (cx, cy, c)).wait_recv()
                _remote(land, land, new, 3 * w + j, sib).start()
        for w, ref in enumerate(refs):
            rh = bufs[w].shape[1] // 2
            part = ref.at[k, pl.ds(c * rh, rh)]
            for j, (cx, cy) in enumerate(_other_chips(x, y)):
                _remote(part, part, old, 3 * w + j, (cx, cy, c)).wait_send()

    return _split_call(name, bufs, run, old=sems, after=after, new=3 * len(bufs))


def _gather_finish(name, bufs, sems, after):
    def run(refs, old, new):
        x, y, c = _place()
        sib = (x, y, 1 - c)
        for w, ref in enumerate(refs):
            rh = bufs[w].shape[1] // 2
            for j, (cx, cy) in enumerate(_other_chips(x, y)):
                sent = ref.at[2 * cx + cy, pl.ds(c * rh, rh)]
                _remote(sent, sent, old, 3 * w + j, sib).wait_send()
                land = ref.at[2 * cx + cy, pl.ds((1 - c) * rh, rh)]
                _remote(land, land, old, 3 * w + j, sib).wait_recv()

    return _split_call(name, bufs, run, old=sems, after=after)[0]


def _pair_send_start(name, grads):
    n = len(grads)
    lands = [lax.empty((N_CHIPS,) + g.shape[2:], g.dtype) for g in grads]

    def run(refs, old, new):
        x, y, c = _place()
        for w in range(n):
            _remote(refs[w].at[:, 1 - c], refs[n + w], new, w, (x, y, 1 - c)).start()

    return _split_call(name, list(grads) + lands, run, new=n)


def _pair_send_wait(name, bufs, sems, after):
    n = len(bufs) // 2

    def run(refs, old, new):
        x, y, c = _place()
        for w in range(n):
            cp = _remote(refs[w].at[:, 1 - c], refs[n + w], old, w, (x, y, 1 - c))
            cp.wait_send()
            cp.wait_recv()

    thru = _split_call(name, bufs, run, old=sems, after=after)[0]
    return thru[:n], thru[n:]


def _run_sets(name, scalars, sets, deps=()):
    starts = np.concatenate([[0], np.cumsum([s[0] for s in sets])]).tolist()

    def spec(block, index, lo, n):
        return pl.BlockSpec(block, lambda i, *sc: index(jnp.clip(i - lo, 0, n - 1), *sc))

    in_specs, out_specs, out_shape, args = [], [], [], []
    for (n, ins, outs, _), lo in zip(sets, starts):
        for array, block, index in ins:
            in_specs.append(spec(block, index, lo, n))
            args.append(array)
        for shape, dtype, block, index in outs:
            out_specs.append(spec(block, index, lo, n))
            out_shape.append(jax.ShapeDtypeStruct(shape, dtype))

    def body(*refs):
        refs = refs[len(scalars):]
        n_in = len(in_specs)
        i = pl.program_id(0)
        pos_in, pos_out = 0, n_in + len(deps)
        for (n, ins, outs, fn), lo in zip(sets, starts):
            in_refs = refs[pos_in:pos_in + len(ins)]
            out_refs = refs[pos_out:pos_out + len(outs)]
            pos_in += len(ins)
            pos_out += len(outs)
            pl.when((i >= lo) & (i < lo + n))(functools.partial(fn, in_refs, out_refs))

    outs = pl.pallas_call(
        body, name=name,
        grid_spec=pltpu.PrefetchScalarGridSpec(
            num_scalar_prefetch=len(scalars), grid=(starts[-1],), in_specs=in_specs + [ANY] * len(deps),
            out_specs=out_specs),
        out_shape=out_shape,
        compiler_params=_cp(("arbitrary",)),
    )(*scalars, *args, *deps)
    result, pos = [], 0
    for _, _, outs_s, _ in sets:
        result.append(list(outs[pos:pos + len(outs_s)]))
        pos += len(outs_s)
    return result


def _adamw_all(jobs, name, dep):
    def update(in_refs, out_refs):
        gg = in_refs[3][...]
        out_refs[0][...] = gg
        out_refs[1][...], out_refs[2][...], out_refs[3][...] = _adamw_math(
            in_refs[0][...], gg, in_refs[1][...], in_refs[2][...])

    whole = lambda l: (0, 0)
    sets = [(1, [(a, a.shape, whole) for a in job], [(job[0].shape, F32, job[0].shape, whole)] * 4, update)
            for job in jobs]
    return [tuple(o) for o in _run_sets(name, [], sets, deps=[dep])]


def _pair_add_all(gs, recvs, cidx, name):
    def add(in_refs, out_refs):
        out_refs[0][...] = (in_refs[0][...].astype(F32) + in_refs[1][...].astype(F32)).astype(BF)

    sets = []
    for g, r in zip(gs, recvs):
        _, _, rh, cols = g.shape
        tr = min(rh, 256)
        sets.append((rh // tr,
                     [(g, (N_CHIPS, None, tr, cols), lambda l, c_ref: (0, c_ref[0], l, 0)),
                      (r, (N_CHIPS, tr, cols), lambda l, c_ref: (0, l, 0))],
                     [((N_CHIPS, rh, cols), BF, (N_CHIPS, tr, cols), lambda l, c_ref: (0, l, 0))], add))
    return [o[0] for o in _run_sets(name, [cidx], sets)]


def _chip_add_all(owns, parts, kidx, cidx, name):
    def add(in_refs, out_refs):
        acc = in_refs[0][...].astype(F32)
        for s in range(3):
            acc = acc + in_refs[1][s].astype(F32)
        out_refs[0][...] = acc

    sets = []
    for own, p in zip(owns, parts):
        _, rh, cols = p.shape
        tr = min(rh, 256)
        sets.append((rh // tr,
                     [(own, (None, tr, cols), lambda l, k_ref, c_ref: (k_ref[0], l, 0)),
                      (p, (3, tr, cols), lambda l, k_ref, c_ref: (0, l, 0))],
                     [((2, rh, cols), F32, (None, tr, cols), lambda l, k_ref, c_ref: (c_ref[0], l, 0))], add))
    return [o[0] for o in _run_sets(name, [kidx, cidx], sets)]


def _pair_add(g, recv, cidx, name):
    _, _, rh, cols = g.shape
    tr = min(rh, 256)

    def body(c_ref, g_ref, r_ref, o_ref):
        o_ref[...] = (g_ref[...].astype(F32) + r_ref[...].astype(F32)).astype(BF)

    return pl.pallas_call(
        body, name=name,
        grid_spec=pltpu.PrefetchScalarGridSpec(
            num_scalar_prefetch=1, grid=(rh // tr,),
            in_specs=[pl.BlockSpec((N_CHIPS, None, tr, cols), lambda i, c_ref: (0, c_ref[0], i, 0)),
                      pl.BlockSpec((N_CHIPS, tr, cols), lambda i, c_ref: (0, i, 0))],
            out_specs=pl.BlockSpec((N_CHIPS, tr, cols), lambda i, c_ref: (0, i, 0))),
        out_shape=jax.ShapeDtypeStruct((N_CHIPS, rh, cols), BF),
        compiler_params=_cp(("parallel",)),
    )(cidx, g, recv)


def _chip_send_start(name, sums):
    n = len(sums)
    lands = [lax.empty((3,) + s.shape[1:], BF) for s in sums]

    def run(refs, old, new):
        x, y, c = _place()
        for w in range(n):
            for j, (cx, cy) in enumerate(_other_chips(x, y)):
                _remote(refs[w].at[2 * cx + cy], refs[n + w].at[j], new, 3 * w + j, (cx, cy, c)).start()

    return _split_call(name, list(sums) + lands, run, new=3 * n)


def _chip_send_wait(name, bufs, sems, after):
    n = len(bufs) // 2

    def run(refs, old, new):
        x, y, c = _place()
        for w in range(n):
            for j, (cx, cy) in enumerate(_other_chips(x, y)):
                cp = _remote(refs[w].at[2 * cx + cy], refs[n + w].at[j], old, 3 * w + j, (cx, cy, c))
                cp.wait_send()
                cp.wait_recv()

    thru = _split_call(name, bufs, run, old=sems, after=after)[0]
    return thru[:n], thru[n:]


def _chip_add(own, parts, kidx, cidx, name):
    _, rh, cols = parts.shape
    tr = min(rh, 512)

    def body(k_ref, c_ref, own_ref, p_ref, o_ref):
        acc = own_ref[...].astype(F32)
        for s in range(3):
            acc = acc + p_ref[s].astype(F32)
        o_ref[...] = acc

    return pl.pallas_call(
        body, name=name,
        grid_spec=pltpu.PrefetchScalarGridSpec(
            num_scalar_prefetch=2, grid=(rh // tr,),
            in_specs=[pl.BlockSpec((None, tr, cols), lambda i, k_ref, c_ref: (k_ref[0], i, 0)),
                      pl.BlockSpec((3, tr, cols), lambda i, k_ref, c_ref: (0, i, 0))],
            out_specs=pl.BlockSpec((None, tr, cols), lambda i, k_ref, c_ref: (c_ref[0], i, 0))),
        out_shape=jax.ShapeDtypeStruct((2, rh, cols), F32),
        compiler_params=_cp(("parallel",)),
    )(kidx, cidx, own, parts)


def _pair_swap_start(name, bufs):
    def run(refs, old, new):
        x, y, c = _place()
        for w, ref in enumerate(refs):
            _remote(ref.at[c], ref.at[c], new, w, (x, y, 1 - c)).start()

    return _split_call(name, bufs, run, new=len(bufs))


def _pair_swap_wait(name, bufs, sems, after):
    def run(refs, old, new):
        x, y, c = _place()
        for w, ref in enumerate(refs):
            _remote(ref.at[c], ref.at[c], old, w, (x, y, 1 - c)).wait_send()
            _remote(ref.at[1 - c], ref.at[1 - c], old, w, (x, y, 1 - c)).wait_recv()

    return _split_call(name, bufs, run, old=sems, after=after)[0]


def _peers(x, y, c):
    out = []
    for code in range(1, 8):
        fx, fy, fc = (code >> 2) & 1, (code >> 1) & 1, code & 1
        px = 1 - x if fx else x
        py = 1 - y if fy else y
        pc = 1 - c if fc else c
        out.append((code, (px, py, pc)))
    return out


def _mod_exchange(c_row, ada_w, ada_b4, deps):
    ncol = ada_w.shape[1]

    def body(c_ref, w_ref, b_ref, *rest):
        call_ref, mod_ref, part_ref, send_sems, recv_sems = rest[len(deps):]
        x, y, c = _place()
        k = 2 * x + y
        me = 4 * x + 2 * y + c
        call_ref[pl.ds(me, 1), :] = c_ref[...]
        sends = []
        for code, peer in _peers(x, y, c):
            cp = pltpu.make_async_remote_copy(
                src_ref=c_ref, dst_ref=call_ref.at[pl.ds(me, 1), :],
                send_sem=send_sems.at[code], recv_sem=recv_sems.at[code],
                device_id=peer, device_id_type=MESH)
            cp.start()
            sends.append(cp)
        for code, (px, py, pc) in _peers(x, y, c):
            land = call_ref.at[pl.ds(4 * px + 2 * py + pc, 1), :]
            pltpu.make_async_remote_copy(
                src_ref=land, dst_ref=land, send_sem=send_sems.at[code], recv_sem=recv_sems.at[code],
                device_id=(px, py, pc), device_id_type=MESH).wait_recv()
        call = call_ref[...]
        act = call * _sigmoid(call)
        part = jnp.dot(act, w_ref[...], preferred_element_type=F32,
                       precision=lax.Precision.HIGHEST) + b_ref[pl.ds(k, 1), :]
        part_ref[...] = part
        mod_ref[pl.ds(k, 1), :] = part_ref[pl.ds(me, 1), :]
        chips = [(8 + j, peer) for j, (code, peer) in enumerate(_peers(x, y, c)) if code in (2, 4, 6)]
        for slot, (px, py, pc) in chips:
            cp = pltpu.make_async_remote_copy(
                src_ref=part_ref.at[pl.ds(4 * px + 2 * py + pc, 1), :], dst_ref=mod_ref.at[pl.ds(k, 1), :],
                send_sem=send_sems.at[slot], recv_sem=recv_sems.at[slot],
                device_id=(px, py, pc), device_id_type=MESH)
            cp.start()
            sends.append(cp)
        for slot, (px, py, pc) in chips:
            land = mod_ref.at[pl.ds(2 * px + py, 1), :]
            pltpu.make_async_remote_copy(
                src_ref=land, dst_ref=land, send_sem=send_sems.at[slot], recv_sem=recv_sems.at[slot],
                device_id=(px, py, pc), device_id_type=MESH).wait_recv()
        for cp in sends:
            cp.wait_send()

    vm = pl.BlockSpec(memory_space=pltpu.VMEM)
    return pl.pallas_call(
        body, name="mod_exchange",
        in_specs=[vm, vm, vm] + [ANY] * len(deps), out_specs=[vm, vm],
        out_shape=[jax.ShapeDtypeStruct((8, D_MODEL), F32), jax.ShapeDtypeStruct((N_CHIPS, ncol), F32)],
        scratch_shapes=[pltpu.VMEM((8, ncol), F32), pltpu.SemaphoreType.DMA((16,)),
                        pltpu.SemaphoreType.DMA((16,))],
        compiler_params=_cp(),
    )(c_row, ada_w, ada_b4, *deps)


def _small_exchange(stats, rows, c_all, wsm, msm, vsm):
    ncol = 6 * D_MODEL // N_CHIPS
    ns = len(stats)

    def body(*refs):
        call_ref, w_ref, m_ref, v_ref, gw_ref, loss_ref = refs[ns:ns + 6]
        outs = refs[ns + 6:ns + 30]
        p_ref, g_ref, all_ref, dm_ref, send_sems, recv_sems = refs[ns + 30:]
        x, y, c = _place()
        k = 2 * x + y
        me = 4 * x + 2 * y + c
        for r, (tab, row) in enumerate(rows):
            p_ref[r] = refs[tab][row:row + 1, :]
        all_ref[:, pl.ds(me, 1), :] = p_ref[...]
        sends = []
        for code, peer in _peers(x, y, c):
            cp = pltpu.make_async_remote_copy(
                src_ref=p_ref, dst_ref=all_ref.at[:, pl.ds(me, 1), :],
                send_sem=send_sems.at[code], recv_sem=recv_sems.at[code],
                device_id=peer, device_id_type=MESH)
            cp.start()
            sends.append(cp)
        for code, (px, py, pc) in _peers(x, y, c):
            land = all_ref.at[:, pl.ds(4 * px + 2 * py + pc, 1), :]
            pltpu.make_async_remote_copy(
                src_ref=land, dst_ref=land, send_sem=send_sems.at[code], recv_sem=recv_sems.at[code],
                device_id=(px, py, pc), device_id_type=MESH).wait_recv()
        for cp in sends:
            cp.wait_send()
        tot = [_colsum(all_ref[r]) for r in range(N_PAY)]
        loss_ref[...] = jnp.sum(tot[P_LOSS], axis=1, keepdims=True)
        g_ref[...] = jnp.zeros_like(g_ref)
        for r in range(P_LOSS):
            g_ref[r:r + 1, :] = tot[r]
        g = g_ref[...]
        for kind, tab in enumerate((g,) + _adamw_math(w_ref[...], g, m_ref[...], v_ref[...])):
            for r in range(6):
                outs[6 * kind][:, r * D_MODEL:(r + 1) * D_MODEL] = tab[r:r + 1, :]
            for i in range(5):
                outs[6 * kind + 1 + i][...] = tab[6 + i:7 + i, :]
        half = D_MODEL // 2
        for kk in range(N_CHIPS):
            @pl.when(k == kk)
            def _():
                r0 = 3 * (kk // 2)
                if kk % 2 == 0:
                    dm_ref[:, :D_MODEL] = all_ref[r0]
                    dm_ref[:, D_MODEL:] = all_ref[r0 + 1][:, :half]
                else:
                    dm_ref[:, :half] = all_ref[r0 + 1][:, half:]
                    dm_ref[:, half:] = all_ref[r0 + 2]
        call = call_ref[...]
        act = call * _sigmoid(call)
        gw_ref[...] = lax.dot_general(act, dm_ref[...], (((0,), (0,)), ((), ())),
                                      preferred_element_type=F32, precision=lax.Precision.HIGHEST)

    vm = pl.BlockSpec(memory_space=pltpu.VMEM)
    vectors = [jax.ShapeDtypeStruct((1, 6 * D_MODEL), F32)] + [jax.ShapeDtypeStruct((1, D_MODEL), F32)] * 5
    outs = pl.pallas_call(
        body, name="small_exchange",
        in_specs=[vm] * (ns + 4), out_specs=[vm] * 26,
        out_shape=[jax.ShapeDtypeStruct((D_MODEL, ncol), F32), jax.ShapeDtypeStruct((1, 1), F32)] + vectors * 4,
        scratch_shapes=[pltpu.VMEM((N_PAY, 1, D_MODEL), F32), pltpu.VMEM((16, D_MODEL), F32),
                        pltpu.VMEM((N_PAY, 8, D_MODEL), F32), pltpu.VMEM((8, ncol), F32),
                        pltpu.SemaphoreType.DMA((8,)), pltpu.SemaphoreType.DMA((8,))],
        compiler_params=_cp(),
    )(*stats, c_all, wsm, msm, vsm)
    return outs[0], outs[1], [outs[2 + 6 * kind:8 + 6 * kind] for kind in range(4)]


def _rope_tables(pos_col, inv_freq, sign, dep=None):
    def body(p_ref, f_ref, s_ref, cos_ref, sin_ref):
        ang = p_ref[...].astype(F32) * f_ref[...]
        cos_ref[...] = jnp.cos(ang)
        sin_ref[...] = jnp.sin(ang) * s_ref[...]

    tr = 512
    shp = jax.ShapeDtypeStruct((SEQ, 128), F32)
    body, in_specs, args = _add_dep(
        body, [pl.BlockSpec((tr, 1), lambda i: (i, 0)), pl.BlockSpec((1, 128), lambda i: (0, 0)),
               pl.BlockSpec((1, 128), lambda i: (0, 0))], [pos_col, inv_freq, sign], dep)
    return pl.pallas_call(
        body, name="rope_tables", grid=(SEQ // tr,),
        in_specs=in_specs,
        out_specs=[pl.BlockSpec((tr, 128), lambda i: (i, 0))] * 2, out_shape=[shp, shp],
        compiler_params=_cp(("parallel",)),
    )(*args)


def _resident(shape):
    nd = len(shape)
    return pl.BlockSpec(shape, lambda *_: (0,) * nd, pipeline_mode=pl.Buffered(1))


def _ln_proj(x, vecs, w_in4):
    tm = min(512, SEQ)
    wc = w_in4.shape[2]

    def body(x_ref, vec_ref, w_ref, h_ref, proj_ref):
        xx = x_ref[...]
        g = _row(vec_ref, V_G1) * (1.0 + _row(vec_ref, V_SC1))
        h = (xx * _rms(xx) * g + _row(vec_ref, V_SH1)).astype(BF)
        h_ref[...] = h
        for j in range(N_CHIPS):
            proj_ref[:, j * wc:(j + 1) * wc] = _dot(h, w_ref[j]).astype(BF)

    return pl.pallas_call(
        body, name="ln_proj", grid=(SEQ // tm,),
        in_specs=[pl.BlockSpec((tm, D_MODEL), lambda i: (i, 0)), _resident((16, D_MODEL)),
                  _resident(w_in4.shape)],
        out_specs=[pl.BlockSpec((tm, D_MODEL), lambda i: (i, 0)), pl.BlockSpec((tm, D_IN), lambda i: (i, 0))],
        out_shape=[jax.ShapeDtypeStruct((SEQ, D_MODEL), BF), jax.ShapeDtypeStruct((SEQ, D_IN), BF)],
        compiler_params=_cp(("parallel",)),
    )(x, vecs, w_in4)


def _lane_first(shape):
    lane = lax.broadcasted_iota(jnp.int32, shape, 1)
    return (lane & 32) == 0


def _rot(v, cos, sin_s):
    partner = jnp.where(_lane_first(v.shape), pltpu.roll(v, 96, 1), pltpu.roll(v, 32, 1))
    return v * cos + partner * sin_s


def _rot_t(dv, cos, sin_s):
    t = dv * sin_s
    partner = jnp.where(_lane_first(dv.shape), pltpu.roll(t, 96, 1), pltpu.roll(t, 32, 1))
    return dv * cos + partner


def _ret_mask(lg):
    t = RET_BLOCK
    ii = lax.broadcasted_iota(jnp.int32, (t, t), 0)
    jj = lax.broadcasted_iota(jnp.int32, (t, t), 1)
    dist = jnp.abs(ii - jj).astype(F32)
    future = (jj >> RET_CHUNK_SHIFT) > (ii >> RET_CHUNK_SHIFT)
    return jnp.where(future, 0.0, jnp.exp(lg * dist))


def _ret_masks(lg, mask_ref, head):
    t = RET_BLOCK
    mask = mask_ref[head]
    ti = lax.broadcasted_iota(jnp.int32, (t, 1), 0).astype(F32)
    from_start = jnp.exp(lg * (ti + 1.0))
    to_end = jnp.exp(lg * (t - 1.0 - ti))
    whole = jnp.exp(jnp.full((1, 128), lg * t, F32))
    return mask, from_start, to_end, whole


def _head_lanes(shape, hh):
    lane = lax.broadcasted_iota(jnp.int32, shape, 1)
    return (lane >> 6) == hh


def _ret_specs():
    t = RET_BLOCK
    return dict(
        q=lambda f: pl.BlockSpec((t, 512), lambda n: (f(n), C_QR // 512)),
        k=lambda f: pl.BlockSpec((t, 512), lambda n: (f(n), C_KR // 512)),
        v=lambda f: pl.BlockSpec((t, D_MODEL), lambda n: (f(n), C_VR // D_MODEL)),
        g=lambda f: pl.BlockSpec((t, D_MODEL), lambda n: (f(n), C_GR // D_MODEL)),
        tab=lambda f: pl.BlockSpec((t, 128), lambda n: (f(n), 0)),
        wide=lambda f: pl.BlockSpec((t, D_MODEL), lambda n: (f(n), 0)),
        state=lambda f: pl.BlockSpec((N_PAIRS, None, 2, 128, 128), lambda n: (0, f(n), 0, 0, 0)),
    )


def _ret_fwd(proj, cos, sin_s, gn_g, log_gamma, dep=None):
    t = RET_BLOCK
    nb = SEQ // t

    def body(lg_ref, q_ref, k_ref, v_ref, g_ref, cos_ref, sin_ref, gn_ref, o_ref, retg_ref, st_ref, state, masks):
        @pl.when(pl.program_id(0) == 0)
        def _():
            state[...] = jnp.zeros_like(state)
            for head in range(2 * N_PAIRS):
                masks[head] = _ret_mask(lg_ref[head])

        cos, sn = cos_ref[...], sin_ref[...]
        for p in range(N_PAIRS):
            q = _rot(q_ref[:, 128 * p:128 * (p + 1)].astype(F32), cos, sn)
            k = _rot(k_ref[:, 128 * p:128 * (p + 1)].astype(F32), cos, sn) * QK_SCALE
            for hh in range(2):
                cols = slice(256 * p + 128 * hh, 256 * p + 128 * (hh + 1))
                lg = lg_ref[2 * p + hh]
                mask, from_start, to_end, whole = _ret_masks(lg, masks, 2 * p + hh)
                lanes = _head_lanes(q.shape, hh)
                qm = jnp.where(lanes, q, 0.0)
                km = jnp.where(lanes, k, 0.0)
                vh = v_ref[:, cols]
                sc = _dot_nt(qm.astype(BF), km.astype(BF)) * mask
                st = state[p, hh]
                st_ref[p, hh] = st
                o = _dot(sc.astype(BF), vh) + _dot((qm * from_start).astype(BF), st.astype(BF))
                state[p, hh] = whole * st + _dot_tn((km * to_end).astype(BF), vh)
                d = o - _rowmean(o)
                nh = d * lax.rsqrt(_rowmean(d * d) + EPS)
                gr = g_ref[:, cols].astype(F32)
                o_ref[:, cols] = o
                retg_ref[:, cols] = (gr * _sigmoid(gr) * nh * gn_ref[:, cols]).astype(BF)

    sp = _ret_specs()
    ident = lambda n: n
    body, in_specs, args = _add_dep(
        body, [pl.BlockSpec(memory_space=pltpu.SMEM), sp["q"](ident), sp["k"](ident), sp["v"](ident),
               sp["g"](ident), sp["tab"](ident), sp["tab"](ident), _resident((1, D_MODEL))],
        [log_gamma, proj, proj, proj, proj, cos, sin_s, gn_g], dep)
    return pl.pallas_call(
        body, name="ret_fwd", grid=(nb,),
        in_specs=in_specs,
        out_specs=[sp["wide"](ident), sp["wide"](ident), sp["state"](ident)],
        out_shape=[jax.ShapeDtypeStruct((SEQ, D_MODEL), F32), jax.ShapeDtypeStruct((SEQ, D_MODEL), BF),
                   jax.ShapeDtypeStruct((N_PAIRS, nb, 2, 128, 128), F32)],
        scratch_shapes=[pltpu.VMEM((N_PAIRS, 2, 128, 128), F32),
                        pltpu.VMEM((2 * N_PAIRS, RET_BLOCK, RET_BLOCK), F32)],
        compiler_params=_cp(("arbitrary",)),
    )(*args)


def _ret_bwd(proj, cos, sin_s, dret, states, log_gamma, dep=None):
    t = RET_BLOCK
    nb = SEQ // t

    def body(lg_ref, q_ref, k_ref, v_ref, cos_ref, sin_ref, do_ref, st_ref, dqkv_ref, dstate, masks):
        @pl.when(pl.program_id(0) == 0)
        def _():
            dstate[...] = jnp.zeros_like(dstate)
            for head in range(2 * N_PAIRS):
                masks[head] = _ret_mask(lg_ref[head])

        cos, sn = cos_ref[...], sin_ref[...]
        for p in range(N_PAIRS):
            q = _rot(q_ref[:, 128 * p:128 * (p + 1)].astype(F32), cos, sn)
            k = _rot(k_ref[:, 128 * p:128 * (p + 1)].astype(F32), cos, sn) * QK_SCALE
            dq_rot = jnp.zeros(q.shape, F32)
            dk_rot = jnp.zeros(q.shape, F32)
            for hh in range(2):
                cols = slice(256 * p + 128 * hh, 256 * p + 128 * (hh + 1))
                lg = lg_ref[2 * p + hh]
                mask, from_start, to_end, whole = _ret_masks(lg, masks, 2 * p + hh)
                lanes = _head_lanes(q.shape, hh)
                qm = jnp.where(lanes, q, 0.0)
                km = jnp.where(lanes, k, 0.0)
                qb, kb = qm.astype(BF), km.astype(BF)
                vh = v_ref[:, cols]
                do = do_ref[:, cols]
                sc = (_dot_nt(qb, kb) * mask).astype(BF)
                st = st_ref[p, hh].astype(BF)
                dst = dstate[p, hh]
                dstb = dst.astype(BF)
                k_end = (km * to_end).astype(BF)
                q_start = (qm * from_start).astype(BF)
                dqkv_ref[:, C_VR + 256 * p + 128 * hh:C_VR + 256 * p + 128 * (hh + 1)] = (
                    _dot_tn(sc, do) + _dot(k_end, dstb)).astype(BF)
                dsc = (_dot_nt(do, vh) * mask).astype(BF)
                dq_h = _dot(dsc, kb) + _dot_nt(do, st) * from_start
                dq_rot = dq_rot + jnp.where(lanes, dq_h, 0.0)
                dk_rot = dk_rot + _dot_tn(dsc, qb) + _dot_nt(vh, dstb) * to_end
                dstate[p, hh] = whole * dst + _dot_tn(q_start, do)
            dqkv_ref[:, C_QR + 128 * p:C_QR + 128 * (p + 1)] = _rot_t(dq_rot, cos, sn).astype(BF)
            dqkv_ref[:, C_KR + 128 * p:C_KR + 128 * (p + 1)] = _rot_t(dk_rot * QK_SCALE, cos, sn).astype(BF)

    sp = _ret_specs()
    rev = lambda n: nb - 1 - n
    body, in_specs, args = _add_dep(
        body, [pl.BlockSpec(memory_space=pltpu.SMEM), sp["q"](rev), sp["k"](rev), sp["v"](rev),
               sp["tab"](rev), sp["tab"](rev), sp["wide"](rev), sp["state"](rev)],
        [log_gamma, proj, proj, proj, cos, sin_s, dret, states], dep)
    return pl.pallas_call(
        body, name="ret_bwd", grid=(nb,),
        in_specs=in_specs,
        out_specs=pl.BlockSpec((t, C_GR), lambda n: (rev(n), 0)),
        out_shape=jax.ShapeDtypeStruct((SEQ, C_GR), BF),
        scratch_shapes=[pltpu.VMEM((N_PAIRS, 2, 128, 128), F32),
                        pltpu.VMEM((2 * N_PAIRS, RET_BLOCK, RET_BLOCK), F32)],
        compiler_params=_cp(("arbitrary",)),
    )(*args)


def _stack_heads(v):
    return jnp.concatenate([jnp.where(_head_lanes(v.shape, hh), v, jnp.zeros_like(v)) for hh in range(2)], axis=0)


def _unstack_heads(v):
    t = v.shape[0] // 2
    return jnp.where(_head_lanes((t, v.shape[1]), 0), v[:t], v[t:])


def _sb_masks(t, heads):
    rr = lax.broadcasted_iota(jnp.int32, (t, t), 0)
    cc = lax.broadcasted_iota(jnp.int32, (t, t), 1)
    r2 = lax.broadcasted_iota(jnp.int32, (heads * t, t), 0) & (t - 1)
    c2 = lax.broadcasted_iota(jnp.int32, (heads * t, t), 1)
    return rr, cc, c2 < r2


def _split_dot2(v, tri):
    return _dot(v.astype(BF), tri)


def _log_one_minus_beta(z):
    return -(jnp.maximum(z, 0.0) + jnp.log(1.0 + jnp.exp(-jnp.abs(z))))


def _sb_fwd(proj):
    t, g = SB_BLOCK, SB_GROUP
    nq = SEQ // t
    rows = 2 * g * t

    def body(q_ref, k_ref, v_ref, o_ref, tot_ref, kt_ref):
        i = pl.program_id(1)

        @pl.when(i == 0)
        def _():
            for p in range(g):
                for jj in range(nq):
                    kt_ref[p, jj] = k_ref[jj * t:(jj + 1) * t, 128 * p:128 * (p + 1)].T

        q2 = [_stack_heads((q_ref[:, 128 * p:128 * (p + 1)].astype(F32) * QK_SCALE).astype(BF)) for p in range(g)]
        rr, cc, valid = _sb_masks(t, 2 * g)
        later = (rr > cc).astype(BF)

        def tile(j, carry, diagonal):
            acc, run = carry
            z = jnp.concatenate([_dot(q2[p], kt_ref[p, j]) for p in range(g)], axis=0)
            lm = _log_one_minus_beta(z)
            if diagonal:
                lm = jnp.where(valid, lm, 0.0)
            after = _split_dot2(lm, later)
            a = jnp.exp(z + lm + after + run)
            if diagonal:
                a = jnp.where(valid, a, 0.0)
            ab = a.astype(BF)
            keys = pl.ds(pl.multiple_of(j * t, t), t)
            av = jnp.concatenate([_dot(ab[2 * t * p:2 * t * (p + 1)], v_ref[keys, 128 * p:128 * (p + 1)])
                                  for p in range(g)], axis=0)
            return acc + av, run + after[:, 0:1] + lm[:, 0:1]

        carry = tile(i, (jnp.zeros((rows, 128), F32), jnp.zeros((rows, 1), F32)), True)
        acc, run = lax.fori_loop(0, i, lambda s, cr: tile(i - 1 - s, cr, False), carry)
        run = jnp.broadcast_to(run, (rows, 128))
        for p in range(g):
            o_ref[:, 128 * p:128 * (p + 1)] = _unstack_heads(acc[2 * t * p:2 * t * (p + 1)]).astype(BF)
            tot_ref[:, 128 * p:128 * (p + 1)] = _unstack_heads(run[2 * t * p:2 * t * (p + 1)])

    w = 128 * g
    return pl.pallas_call(
        body, name="sb_fwd", grid=(N_PAIRS // g, nq),
        in_specs=[pl.BlockSpec((t, w), lambda p, i: (i, C_QS // w + p)),
                  pl.BlockSpec((SEQ, w), lambda p, i: (0, C_KS // w + p)),
                  pl.BlockSpec((SEQ, w), lambda p, i: (0, C_VS // w + p))],
        out_specs=[pl.BlockSpec((t, w), lambda p, i: (i, p))] * 2,
        out_shape=[jax.ShapeDtypeStruct((SEQ, 512), BF), jax.ShapeDtypeStruct((SEQ, 512), F32)],
        scratch_shapes=[pltpu.VMEM((g, nq, 128, t), BF)],
        compiler_params=_cp(("parallel", "arbitrary")),
    )(proj, proj, proj)


def _sb_bwd(proj, dsb, tot, dep=None):
    t, g = SB_BLOCK, SB_GROUP
    nq = SEQ // t
    rows = 2 * g * t

    def body(q_ref, k_ref, v_ref, do_ref, tot_ref, dq_ref, dk_ref, dv_ref, kt_ref, vt_ref, dkt_acc, dvt_acc):
        i = pl.program_id(1)

        @pl.when(i == 0)
        def _():
            dkt_acc[...] = jnp.zeros_like(dkt_acc)
            dvt_acc[...] = jnp.zeros_like(dvt_acc)
            for p in range(g):
                for jj in range(nq):
                    kt_ref[p, jj] = k_ref[jj * t:(jj + 1) * t, 128 * p:128 * (p + 1)].T
                    vt_ref[p, jj] = v_ref[jj * t:(jj + 1) * t, 128 * p:128 * (p + 1)].T

        q2 = [_stack_heads((q_ref[:, 128 * p:128 * (p + 1)].astype(F32) * QK_SCALE).astype(BF)) for p in range(g)]
        do2 = [_stack_heads(do_ref[:, 128 * p:128 * (p + 1)]) for p in range(g)]
        q2t = [v.T for v in q2]
        do2t = [v.T for v in do2]
        tots = tot_ref[...]
        total = jnp.concatenate([tots[:, 64 * h:64 * h + 1] for h in range(2 * g)], axis=0)
        rr, cc, valid = _sb_masks(t, 2 * g)
        upto = (rr <= cc).astype(BF)
        before = (rr < cc).astype(BF)

        def part(v, p):
            return v[2 * t * p:2 * t * (p + 1)]

        def tile(j, carry, diagonal):
            dq, run_l, run_g = carry
            z = jnp.concatenate([_dot(q2[p], kt_ref[p, j]) for p in range(g)], axis=0)
            lm = _log_one_minus_beta(z)
            if diagonal:
                lm = jnp.where(valid, lm, 0.0)
            incl = _split_dot2(lm, upto)
            a = jnp.exp(z + lm + (total - (incl + run_l)))
            if diagonal:
                a = jnp.where(valid, a, 0.0)
            gg = a * jnp.concatenate([_dot(do2[p], vt_ref[p, j]) for p in range(g)], axis=0)
            excl = _split_dot2(gg, before)
            dz = gg * jnp.exp(lm) - (excl + run_g) * jnp.exp(z + lm)
            if diagonal:
                dz = jnp.where(valid, dz, 0.0)
            dzb = dz.astype(BF)
            ab = a.astype(BF)
            keys = pl.ds(pl.multiple_of(j * t, t), t)
            for p in range(g):
                dkt_acc[p, j] += _dot(q2t[p], part(dzb, p))
                dvt_acc[p, j] += _dot(do2t[p], part(ab, p))
            dq_t = jnp.concatenate([_dot(part(dzb, p), k_ref[keys, 128 * p:128 * (p + 1)]) for p in range(g)], axis=0)
            return (dq + dq_t, run_l + incl[:, t - 1:t], run_g + excl[:, t - 1:t] + gg[:, t - 1:t])

        zero = jnp.zeros((rows, 1), F32)
        carry = lax.fori_loop(0, i, lambda j, cr: tile(j, cr, False), (jnp.zeros((rows, 128), F32), zero, zero))
        dq = tile(i, carry, True)[0]
        for p in range(g):
            dq_ref[:, 128 * p:128 * (p + 1)] = (_unstack_heads(part(dq, p)) * QK_SCALE).astype(BF)

        @pl.when(i == nq - 1)
        def _():
            for p in range(g):
                for jj in range(nq):
                    dk_ref[jj * t:(jj + 1) * t, 128 * p:128 * (p + 1)] = dkt_acc[p, jj].T.astype(BF)
                    dv_ref[jj * t:(jj + 1) * t, 128 * p:128 * (p + 1)] = dvt_acc[p, jj].T.astype(BF)

    w = 128 * g
    tile_spec = pl.BlockSpec((t, w), lambda p, i: (i, p))
    col_spec = pl.BlockSpec((SEQ, w), lambda p, i: (0, p))
    shp = jax.ShapeDtypeStruct((SEQ, 512), BF)
    body, in_specs, args = _add_dep(
        body, [pl.BlockSpec((t, w), lambda p, i: (i, C_QS // w + p)),
               pl.BlockSpec((SEQ, w), lambda p, i: (0, C_KS // w + p)),
               pl.BlockSpec((SEQ, w), lambda p, i: (0, C_VS // w + p)),
               tile_spec, tile_spec],
        [proj, proj, proj, dsb, tot], dep)
    return pl.pallas_call(
        body, name="sb_bwd", grid=(N_PAIRS // g, nq),
        in_specs=in_specs,
        out_specs=[tile_spec, col_spec, col_spec],
        out_shape=[shp, shp, shp],
        scratch_shapes=[pltpu.VMEM((g, nq, 128, t), BF), pltpu.VMEM((g, nq, 128, t), BF),
                        pltpu.VMEM((g, nq, 128, t), F32), pltpu.VMEM((g, nq, 128, t), F32)],
        compiler_params=_cp(("parallel", "arbitrary")),
    )(*args)


def _mix_out(retg, sb, proj, x, vecs, w_ret, w_sb4, w_out):
    tm, half = min(512, SEQ), 512

    def body(r_ref, s_ref, ar0, ar1, as0, as1, x_ref, vec_ref, wr_ref, ws_ref, wo_ref,
             mix_ref, rb_ref, sbp_ref, y_ref, h1_ref, h2_ref):
        rb = _dot(r_ref[...], wr_ref[...])
        sbv = s_ref[...]
        sbp = jnp.concatenate([_dot(sbv, ws_ref[k]) for k in range(N_CHIPS)], axis=1)
        gate_r = _sigmoid(jnp.concatenate([ar0[...], ar1[...]], axis=1).astype(F32))
        gate_s = _sigmoid(jnp.concatenate([as0[...], as1[...]], axis=1).astype(F32))
        mixed = (gate_r * rb + gate_s * sbp).astype(BF)
        mix_ref[...] = mixed
        rb_ref[...] = rb.astype(BF)
        sbp_ref[...] = sbp.astype(BF)
        y = _dot(mixed, wo_ref[...])
        h1 = x_ref[...] + _row(vec_ref, V_GT1) * (y * _rms(y)) * _row(vec_ref, V_G2)
        g = _row(vec_ref, V_G3) * (1.0 + _row(vec_ref, V_SC2))
        y_ref[...] = y
        h1_ref[...] = h1
        h2_ref[...] = (h1 * _rms(h1) * g + _row(vec_ref, V_SH2)).astype(BF)

    row = pl.BlockSpec((tm, D_MODEL), lambda i: (i, 0))
    gate = lambda c0: pl.BlockSpec((tm, half), lambda i: (i, c0 // half))
    bf = jax.ShapeDtypeStruct((SEQ, D_MODEL), BF)
    f32 = jax.ShapeDtypeStruct((SEQ, D_MODEL), F32)
    return pl.pallas_call(
        body, name="mix_out", grid=(SEQ // tm,),
        in_specs=[row, pl.BlockSpec((tm, 512), lambda i: (i, 0)), gate(C_AR), gate(C_AR + half), gate(C_AS),
                  gate(C_AS + half), row, _resident((16, D_MODEL)), _resident((D_MODEL, D_MODEL)),
                  _resident(w_sb4.shape), _resident((D_MODEL, D_MODEL))],
        out_specs=[row] * 6, out_shape=[bf, bf, bf, f32, f32, bf],
        compiler_params=_cp(("parallel",)),
    )(retg, sb, proj, proj, proj, proj, x, vecs, w_ret, w_sb4, w_out)


def _ffn_fwd_loss(h2, h1, target, vecs, w_ff14, w_ff24):
    tm = 256

    def body(h2_ref, h1_ref, t_ref, vec_ref, w1_ref, w2_ref, u_ref, a_ref, dout_ref, df_ref, st_ref):
        @pl.when(pl.program_id(0) == 0)
        def _():
            st_ref[...] = jnp.zeros_like(st_ref)

        hb = h2_ref[...]
        f = jnp.zeros((tm, D_MODEL), F32)
        for k in range(N_CHIPS):
            cols = slice(k * D_MODEL, (k + 1) * D_MODEL)
            u = _dot(hb, w1_ref[k])
            r = jnp.maximum(u, 0.0)
            act = (r * r).astype(BF)
            u_ref[:, cols] = u.astype(BF)
            a_ref[:, cols] = act
            f = f + _dot(act, w2_ref[k])
        r4 = _rms(f)
        fn = f * r4
        gt2, g4 = _row(vec_ref, V_GT2), _row(vec_ref, V_G4)
        diff = h1_ref[...] + gt2 * fn * g4 - t_ref[...]
        dout = diff * (1.0 / D_MODEL)
        dfn = dout * gt2 * g4
        dout_ref[...] = dout
        df_ref[...] = (r4 * (dfn - fn * _rowmean(dfn * fn))).astype(BF)
        st_ref[0:1, :] += _colsum(dout * fn * g4)
        st_ref[1:2, :] += _colsum(dout * gt2 * fn)
        st_ref[2:3, :] += _colsum(diff * diff) * (0.5 / D_MODEL)

    row = pl.BlockSpec((tm, D_MODEL), lambda i: (i, 0))
    wide = pl.BlockSpec((tm, D_FF), lambda i: (i, 0))
    return pl.pallas_call(
        body, name="ffn_fwd_loss", grid=(SEQ // tm,),
        in_specs=[row, row, row, _resident((16, D_MODEL)), _resident(w_ff14.shape), _resident(w_ff24.shape)],
        out_specs=[wide, wide, row, row, pl.BlockSpec((8, D_MODEL), lambda i: (0, 0))],
        out_shape=[jax.ShapeDtypeStruct((SEQ, D_FF), BF), jax.ShapeDtypeStruct((SEQ, D_FF), BF),
                   jax.ShapeDtypeStruct((SEQ, D_MODEL), F32), jax.ShapeDtypeStruct((SEQ, D_MODEL), BF),
                   jax.ShapeDtypeStruct((8, D_MODEL), F32)],
        compiler_params=_cp(("arbitrary",)),
    )(h2, h1, target, vecs, w_ff14, w_ff24)


def _ffn_bwd(df, u, h1, y, dout, vecs, w_ff14, w_ff24):
    tm = 256

    def body(df_ref, u_ref, h1_ref, y_ref, dout_ref, vec_ref, w1_ref, w2_ref, du_ref, dh1_ref, dy_ref, st_ref):
        @pl.when(pl.program_id(0) == 0)
        def _():
            st_ref[...] = jnp.zeros_like(st_ref)

        dfb = df_ref[...]
        dh2 = jnp.zeros((tm, D_MODEL), F32)
        for k in range(N_CHIPS):
            cols = slice(k * D_MODEL, (k + 1) * D_MODEL)
            da = _dot_nt(dfb, w2_ref[k])
            du = (da * (2.0 * jnp.maximum(u_ref[:, cols].astype(F32), 0.0))).astype(BF)
            du_ref[:, cols] = du
            dh2 = dh2 + _dot_nt(du, w1_ref[k])
        h1 = h1_ref[...]
        r3 = _rms(h1)
        hn3 = h1 * r3
        g3, sc2 = _row(vec_ref, V_G3), _row(vec_ref, V_SC2)
        dhn3 = dh2 * g3 * (1.0 + sc2)
        dh1 = dout_ref[...] + r3 * (dhn3 - hn3 * _rowmean(dhn3 * hn3))
        y = y_ref[...]
        r2 = _rms(y)
        yn = y * r2
        gt1, g2 = _row(vec_ref, V_GT1), _row(vec_ref, V_G2)
        dyn = dh1 * gt1 * g2
        dh1_ref[...] = dh1
        dy_ref[...] = (r2 * (dyn - yn * _rowmean(dyn * yn))).astype(BF)
        st_ref[0:1, :] += _colsum(dh2)
        st_ref[1:2, :] += _colsum(dh2 * hn3 * g3)
        st_ref[2:3, :] += _colsum(dh2 * hn3 * (1.0 + sc2))
        st_ref[3:4, :] += _colsum(dh1 * yn * g2)
        st_ref[4:5, :] += _colsum(dh1 * gt1 * yn)

    row = pl.BlockSpec((tm, D_MODEL), lambda i: (i, 0))
    wide = pl.BlockSpec((tm, D_FF), lambda i: (i, 0))
    return pl.pallas_call(
        body, name="ffn_bwd", grid=(SEQ // tm,),
        in_specs=[row, wide, row, row, row, _resident((16, D_MODEL)), _resident(w_ff14.shape),
                  _resident(w_ff24.shape)],
        out_specs=[wide, row, row, pl.BlockSpec((8, D_MODEL), lambda i: (0, 0))],
        out_shape=[jax.ShapeDtypeStruct((SEQ, D_FF), BF), jax.ShapeDtypeStruct((SEQ, D_MODEL), F32),
                   jax.ShapeDtypeStruct((SEQ, D_MODEL), BF), jax.ShapeDtypeStruct((8, D_MODEL), F32)],
        compiler_params=_cp(("arbitrary",)),
    )(df, u, h1, y, dout, vecs, w_ff14, w_ff24)


def _mix_ret_bwd(dy, proj, rb, sbp, o_raw, gn_g, w_out, w_sb4, w_ret):
    tm, half = min(512, SEQ), 512

    def body(dy_ref, ar0, ar1, as0, as1, rb_ref, sbp_ref, g_ref, o_ref, gn_ref, wo_ref, ws_ref, wr_ref,
             drb_ref, dsbp_ref, da_ref, dsb_ref, dret_ref, dgr_ref, st_ref):
        @pl.when(pl.program_id(0) == 0)
        def _():
            st_ref[...] = jnp.zeros_like(st_ref)

        dm_all = _dot_nt(dy_ref[...], wo_ref[...])
        dsb = jnp.zeros((tm, 512), F32)
        drbs = []
        for hf, (ar_ref, as_ref) in enumerate(((ar0, as0), (ar1, as1))):
            cols = slice(half * hf, half * (hf + 1))
            dm = dm_all[:, cols]
            sr = _sigmoid(ar_ref[...].astype(F32))
            ss = _sigmoid(as_ref[...].astype(F32))
            dsbp = (dm * ss).astype(BF)
            drbs.append((dm * sr).astype(BF))
            dsbp_ref[:, cols] = dsbp
            da_ref[:, cols] = (dm * rb_ref[:, cols].astype(F32) * sr * (1.0 - sr)).astype(BF)
            da_ref[:, D_MODEL + half * hf:D_MODEL + half * (hf + 1)] = (
                dm * sbp_ref[:, cols].astype(F32) * ss * (1.0 - ss)).astype(BF)
            dsb = dsb + _dot_nt(dsbp[:, :256], ws_ref[2 * hf]) + _dot_nt(dsbp[:, 256:], ws_ref[2 * hf + 1])
        dsb_ref[...] = dsb.astype(BF)
        drb = jnp.concatenate(drbs, axis=1)
        drb_ref[...] = drb
        dretg = _dot_nt(drb, wr_ref[...])
        for gi in range(D_MODEL // 128):
            cols = slice(128 * gi, 128 * (gi + 1))
            o = o_ref[:, cols]
            d = o - _rowmean(o)
            rstd = lax.rsqrt(_rowmean(d * d) + EPS)
            nh = d * rstd
            gain = gn_ref[:, cols]
            gr = g_ref[:, cols].astype(F32)
            sg = _sigmoid(gr)
            dg = dretg[:, cols]
            dgn = dg * gr * sg
            dnh = dgn * gain
            dgr_ref[:, cols] = (dg * nh * gain * sg * (1.0 + gr * (1.0 - sg))).astype(BF)
            dret_ref[:, cols] = (rstd * (dnh - _rowmean(dnh) - nh * _rowmean(dnh * nh))).astype(BF)
            st_ref[0:1, cols] += _colsum(dgn * nh)

    row = pl.BlockSpec((tm, D_MODEL), lambda i: (i, 0))
    gate = lambda c0: pl.BlockSpec((tm, half), lambda i: (i, c0 // half))
    shp = jax.ShapeDtypeStruct((SEQ, D_MODEL), BF)
    return pl.pallas_call(
        body, name="mix_ret_bwd", grid=(SEQ // tm,),
        in_specs=[row, gate(C_AR), gate(C_AR + half), gate(C_AS), gate(C_AS + half), row, row,
                  pl.BlockSpec((tm, D_MODEL), lambda i: (i, C_GR // D_MODEL)), row, _resident((1, D_MODEL)),
                  _resident((D_MODEL, D_MODEL)), _resident(w_sb4.shape), _resident((D_MODEL, D_MODEL))],
        out_specs=[row, row, pl.BlockSpec((tm, 2 * D_MODEL), lambda i: (i, 0)), pl.BlockSpec((tm, 512), lambda i: (i, 0)),
                   row, row, pl.BlockSpec((8, D_MODEL), lambda i: (0, 0))],
        out_shape=[shp, shp, jax.ShapeDtypeStruct((SEQ, 2 * D_MODEL), BF), jax.ShapeDtypeStruct((SEQ, 512), BF),
                   shp, shp, jax.ShapeDtypeStruct((8, D_MODEL), F32)],
        compiler_params=_cp(("arbitrary",)),
    )(dy, proj, proj, proj, proj, rb, sbp, proj, o_raw, gn_g, w_out, w_sb4, w_ret)


def _dproj_segments(widths):
    wc = D_IN // N_CHIPS
    segs, start = [], 0
    for pi, width in enumerate(widths):
        lo = start
        while lo < start + width:
            j = lo // wc
            hi = min(start + width, (j + 1) * wc)
            segs.append((j, lo - j * wc, pi, lo - start, hi - lo))
            lo = hi
        start += width
    assert start == D_IN
    return segs


def _in_proj_bwd(pieces, x, dh1, vecs, w_in4, updates, dep=None):
    tm = 256
    steps = SEQ // tm
    n, nu = len(pieces), len(updates)
    segs = _dproj_segments([p.shape[1] for p in pieces])

    def body(*refs):
        x_ref, dh1_ref, vec_ref, w_ref = refs[n:n + 4]
        upd_in = refs[n + 4:n + 4 + 4 * nu]
        dx_ref, st_ref = refs[n + 4 + 4 * nu:n + 6 + 4 * nu]
        upd_out = refs[n + 6 + 4 * nu:]
        for u in range(nu):
            w_u, m_u, v_u, g_u = upd_in[4 * u:4 * u + 4]
            go_u, d_u, mo_u, vo_u = upd_out[4 * u:4 * u + 4]
            gg = g_u[...]
            go_u[...] = gg
            d_u[...], mo_u[...], vo_u[...] = _adamw_math(w_u[...], gg, m_u[...], v_u[...])

        @pl.when(pl.program_id(0) == 0)
        def _():
            st_ref[...] = jnp.zeros_like(st_ref)

        dh = jnp.zeros((tm, D_MODEL), F32)
        for j, so, pi, po, width in segs:
            dh = dh + _dot_nt(refs[pi][:, po:po + width], w_ref[j, :, so:so + width])
        xx = x_ref[...]
        r1 = _rms(xx)
        xn = xx * r1
        g1, sc1 = _row(vec_ref, V_G1), _row(vec_ref, V_SC1)
        dxn = dh * g1 * (1.0 + sc1)
        dx_ref[...] = dh1_ref[...] + r1 * (dxn - xn * _rowmean(dxn * xn))
        st_ref[0:1, :] += _colsum(dh)
        st_ref[1:2, :] += _colsum(dh * xn * g1)
        st_ref[2:3, :] += _colsum(dh * xn * (1.0 + sc1))

    row = pl.BlockSpec((tm, D_MODEL), lambda i: (i, 0))
    upd_specs, upd_shapes, upd_args = [], [], []
    for arrays in updates:
        rows, cols = arrays[0].shape
        upd_specs += [pl.BlockSpec((rows // steps, cols), lambda i: (i, 0))] * 4
        upd_shapes += [jax.ShapeDtypeStruct((rows, cols), F32)] * 4
        upd_args += list(arrays)
    body, in_specs, args = _add_dep(
        body, [pl.BlockSpec((tm, p.shape[1]), lambda i: (i, 0)) for p in pieces] + [
            row, row, _resident((16, D_MODEL)), _resident(w_in4.shape)] + upd_specs,
        list(pieces) + [x, dh1, vecs, w_in4] + upd_args, dep)
    outs = pl.pallas_call(
        body, name="in_proj_bwd", grid=(steps,),
        in_specs=in_specs,
        out_specs=[row, pl.BlockSpec((8, D_MODEL), lambda i: (0, 0))] + upd_specs,
        out_shape=[jax.ShapeDtypeStruct((SEQ, D_MODEL), F32), jax.ShapeDtypeStruct((8, D_MODEL), F32)] + upd_shapes,
        compiler_params=_cp(("arbitrary",)),
    )(*args)
    return outs[0], outs[1], [tuple(outs[2 + 4 * u:6 + 4 * u]) for u in range(nu)]


def _grad_w_in(h, pieces, dep=None):
    ta = 512
    n = len(pieces)
    segs = _dproj_segments([p.shape[1] for p in pieces])

    def body(*refs):
        h_ref, o_ref = refs[n], refs[n + 1]
        hh = h_ref[...]
        for j, so, pi, po, width in segs:
            o_ref[j, :, so:so + width] = _dot_tn(hh, refs[pi][:, po:po + width]).astype(BF)

    body, in_specs, args = _add_dep(
        body, [_resident(p.shape) for p in pieces] + [pl.BlockSpec((SEQ, ta), lambda i: (0, i))],
        list(pieces) + [h], dep)
    return pl.pallas_call(
        body, name="grad_w_in", grid=(D_MODEL // ta,),
        in_specs=in_specs,
        out_specs=pl.BlockSpec((N_CHIPS, ta, D_IN // N_CHIPS), lambda i: (0, i, 0)),
        out_shape=jax.ShapeDtypeStruct((N_CHIPS, D_MODEL, D_IN // N_CHIPS), BF),
        compiler_params=_cp(("parallel",)),
    )(*args)


def _weight_grads(jobs, name):
    def grad(in_refs, out_refs):
        out_refs[0][...] = _dot_tn(in_refs[0][...], in_refs[1][...]).astype(BF)

    sets = []
    for a, b, col_sharded in jobs:
        ka, nb_ = a.shape[1], b.shape[1]
        if col_sharded:
            ta, tb = ka, nb_ // N_CHIPS
            ins = [(a, (SEQ, ta), lambda l: (0, 0)), (b, (SEQ, tb), lambda l: (0, l))]
        else:
            ta, tb = ka // N_CHIPS, nb_
            ins = [(a, (SEQ, ta), lambda l: (0, l)), (b, (SEQ, tb), lambda l: (0, 0))]
        sets.append((N_CHIPS, ins, [((N_CHIPS, ta, tb), BF, (None, ta, tb), lambda l: (l, 0, 0))], grad))
    return [o[0] for o in _run_sets(name, [], sets)]


def _rope_constants():
    freq = np.float32(ROPE_BASE) ** (-np.arange(0, 64, 2, dtype=np.float32) / np.float32(64))
    inv = np.tile(freq.astype(np.float32), 4).reshape(1, 128)
    sign = np.tile(np.concatenate([-np.ones(32, np.float32), np.ones(32, np.float32)]), 2).reshape(1, 128)
    return jnp.asarray(inv), jnp.asarray(sign)


def _log_gamma():
    return jnp.asarray(np.log1p(-(2.0 ** (-5.0 - np.arange(8, dtype=np.float64)))).astype(np.float32))


def _halves(g):
    return g.reshape(N_CHIPS, 2, g.shape[1] // 2, g.shape[2])


def kernel(x, c, positions, ada_w, ada_b, pre_mix_g, post_mix_g, pre_ffn_g, post_ffn_g, w_in, ret_gn_g, w_ret_branch, w_sb_branch, w_out, w_ff1, w_ff2, loss_target, m_ada_w, m_ada_b, m_pre_mix_g, m_post_mix_g, m_pre_ffn_g, m_post_ffn_g, m_w_in, m_ret_gn_g, m_w_ret_branch, m_w_sb_branch, m_w_out, m_w_ff1, m_w_ff2, v_ada_w, v_ada_b, v_pre_mix_g, v_post_mix_g, v_pre_ffn_g, v_post_ffn_g, v_w_in, v_ret_gn_g, v_w_ret_branch, v_w_sb_branch, v_w_out, v_w_ff1, v_w_ff2):
    names = ["w_in", "w_ret", "w_sb", "w_out", "w_ff1", "w_ff2"]
    big = dict(zip(names, [w_in, w_ret_branch, w_sb_branch, w_out, w_ff1, w_ff2]))
    big_m = dict(zip(names, [m_w_in, m_w_ret_branch, m_w_sb_branch, m_w_out, m_w_ff1, m_w_ff2]))
    big_v = dict(zip(names, [v_w_in, v_w_ret_branch, v_w_sb_branch, v_w_out, v_w_ff1, v_w_ff2]))
    rest = names[1:]
    cidx = lax.axis_index("c").astype(jnp.int32).reshape(1)
    kidx = (2 * lax.axis_index("x") + lax.axis_index("y")).astype(jnp.int32).reshape(1)
    x0, target = x[0], loss_target[0]

    buf_in, sem_in, tok_in = _gather_start("gather_in_start", [_cast_bf16(w_in[0], kidx, c, "cast_w_in")])
    rest_bufs = [_cast_bf16(big[nm][0], kidx, tok_in, "cast_" + nm) for nm in rest]
    inv_freq, sign = _rope_constants()
    lg = _log_gamma()
    cos, sin_s = _rope_tables(positions.reshape(SEQ, 1), inv_freq, sign, dep=tok_in)

    def table(b6, g5):
        return jnp.concatenate([b6.reshape(6, D_MODEL)] + g5 + [jnp.zeros((5, D_MODEL), F32)], axis=0)

    wsm = table(ada_b, [pre_mix_g, post_mix_g, pre_ffn_g, post_ffn_g, ret_gn_g])
    msm = table(m_ada_b, [m_pre_mix_g, m_post_mix_g, m_pre_ffn_g, m_post_ffn_g, m_ret_gn_g])
    vsm = table(v_ada_b, [v_pre_mix_g, v_post_mix_g, v_pre_ffn_g, v_post_ffn_g, v_ret_gn_g])
    c_all, mod4 = _mod_exchange(c, ada_w[0], ada_b.reshape(N_CHIPS, -1), rest_bufs + [cos, wsm, msm, vsm])
    vecs = jnp.concatenate([mod4.reshape(6, D_MODEL), pre_mix_g, post_mix_g, pre_ffn_g, post_ffn_g,
                            jnp.zeros((6, D_MODEL), F32)], axis=0)
    buf_in, sem_in, tok_in = _gather_pass("gather_in_pass", buf_in, sem_in, vecs)
    buf_rest, sem_rest, tok_rest = _gather_start("gather_rest_start", rest_bufs, after=tok_in)
    (w_in4,) = _gather_finish("gather_in_finish", buf_in, sem_in, tok_rest)

    h, proj = _ln_proj(x0, vecs, w_in4)
    sb, tot = _sb_fwd(proj)
    buf_rest, sem_rest, tok_rest = _gather_pass("gather_rest_pass", buf_rest, sem_rest, sb)
    o_raw, retg, states = _ret_fwd(proj, cos, sin_s, ret_gn_g, lg, dep=tok_rest)
    w_ret4, w_sb4, w_out4, w_ff14, w_ff24 = _gather_finish("gather_rest_finish", buf_rest, sem_rest, retg)
    w_ret = w_ret4.reshape(D_MODEL, D_MODEL)
    w_out2 = w_out4.reshape(D_MODEL, D_MODEL)
    mixed, rb, sbp, y, h1, h2 = _mix_out(retg, sb, proj, x0, vecs, w_ret, w_sb4, w_out2)
    u, act, dout, df, st_a = _ffn_fwd_loss(h2, h1, target, vecs, w_ff14, w_ff24)

    du, dh1, dy, st_b = _ffn_bwd(df, u, h1, y, dout, vecs, w_ff14, w_ff24)
    grads = dict(zip(("w_ff2", "w_ff1"), _weight_grads([(act, df, False), (h2, du, True)], "grad_w_ff")))
    drb, dsbp, da, dsb, dret, dg_r, st_c = _mix_ret_bwd(dy, proj, rb, sbp, o_raw, ret_gn_g, w_out2, w_sb4, w_ret)
    grads.update(zip(("w_out", "w_ret", "w_sb"), _weight_grads(
        [(mixed, dy, False), (retg, drb, False), (sb, dsbp, True)], "grad_w_branches")))

    bufs, sems, tok = _pair_send_start("rs_rest_pair_send", [_halves(grads[nm]) for nm in rest])
    dqkv_r = _ret_bwd(proj, cos, sin_s, dret, states, lg, dep=tok)
    mine, theirs = _pair_send_wait("rs_rest_pair_recv", bufs, sems, dqkv_r)
    pair_sums = _pair_add_all(mine, theirs, cidx, "pair_add_rest")
    bufs, sems, tok = _chip_send_start("rs_rest_chip_send", pair_sums)
    dq_s, dk_s, dv_s = _sb_bwd(proj, dsb, tot, dep=tok)
    own, parts = _chip_send_wait("rs_rest_chip_recv", bufs, sems, dq_s)
    sums = _chip_add_all(own, parts, kidx, cidx, "chip_add_rest")
    bufs, sems, tok = _pair_swap_start("rs_rest_pair_swap", sums)
    dproj = [dqkv_r, dg_r, dq_s, dk_s, dv_s, da]
    g_in = _grad_w_in(h, dproj, dep=tok)
    full_rest = _pair_swap_wait("rs_rest_pair_swapped", bufs, sems, g_in)

    full_rest = dict(zip(rest, full_rest))
    bufs, sems, tok = _pair_send_start("rs_in_pair_send", [_halves(g_in)])
    small_w = ("w_out", "w_sb", "w_ret")
    out = dict(zip(small_w, _adamw_all(
        [(big[nm][0], big_m[nm][0], big_v[nm][0], full_rest[nm].reshape(big[nm].shape[1:])) for nm in small_w],
        "adamw_small", tok)))
    mine, theirs = _pair_send_wait("rs_in_pair_recv", bufs, sems, out["w_ret"][1])
    bufs, sems, tok = _chip_send_start("rs_in_chip_send", [_pair_add(mine[0], theirs[0], cidx, "pair_add_w_in")])
    riding = ("w_ff2", "w_ff1")
    dx, st_d, updated = _in_proj_bwd(
        dproj, x0, dh1, vecs, w_in4,
        [(big[nm][0], big_m[nm][0], big_v[nm][0], full_rest[nm].reshape(big[nm].shape[1:])) for nm in riding],
        dep=tok)
    out.update(zip(riding, updated))

    a_, b_, c_, d_ = range(4)
    payload_rows = [(d_, 0), (d_, 1), (b_, 3), (b_, 0), (b_, 1), (a_, 0),
                    (d_, 2), (b_, 4), (b_, 2), (a_, 1), (c_, 0), (a_, 2)]
    g_ada, loss, small = _small_exchange([st_a, st_b, st_c, st_d], payload_rows, c_all, wsm, msm, vsm)
    own, parts = _chip_send_wait("rs_in_chip_recv", bufs, sems, g_ada)
    bufs, sems, tok = _pair_swap_start(
        "rs_in_pair_swap", [_chip_add(own[0], parts[0], kidx, cidx, "chip_add_w_in")])
    ada_out = _adamw(ada_w[0], m_ada_w[0], v_ada_w[0], g_ada, "adamw_ada_w", dep=tok)
    (full_in,) = _pair_swap_wait("rs_in_pair_swapped", bufs, sems, ada_out[1])
    out["w_in"] = _adamw(w_in[0], m_w_in[0], v_w_in[0], full_in.reshape(w_in.shape[1:]), "adamw_w_in")

    def ordered(which):
        sm = small[which]
        bg = [out[nm][which][None] for nm in names]
        return [ada_out[which][None], sm[0], sm[1], sm[2], sm[3], sm[4], bg[0], sm[5]] + bg[1:]

    return (loss.reshape(()), dx[None], *ordered(0), *ordered(1), *ordered(2), *ordered(3))
```

```python
import functools

import numpy as np
import jax
import jax.numpy as jnp
from jax import lax
from jax.experimental import pallas as pl
from jax.experimental.pallas import tpu as pltpu

SEQ = 2048
D_MODEL = 1024
D_IN = 6656
D_FF = 4096
N_CHIPS = 4
EPS = 1e-6
ROPE_BASE = 10000.0
RET_BLOCK = 256
RET_CHUNK_SHIFT = 6
SB_BLOCK = 256
QK_SCALE = 0.125
LOG2E = 1.4426950408889634
N_PAIRS = 4
SB_GROUP = 4

ADAM_LR = 0.001
ADAM_B1 = 0.9
ADAM_B2 = 0.999
ADAM_EPS = 1e-08
ADAM_WD = 0.01
ADAM_STEP = 10

BF = jnp.bfloat16
F32 = jnp.float32
MESH = pl.DeviceIdType.MESH
VMEM_LIMIT = 56 * 1024 * 1024
ANY = pl.BlockSpec(memory_space=pl.ANY)

C_QR, C_KR, C_VR, C_GR, C_QS, C_KS, C_VS, C_AR, C_AS = 0, 512, 1024, 2048, 3072, 3584, 4096, 4608, 5632

V_SH1, V_SC1, V_GT1, V_SH2, V_SC2, V_GT2, V_G1, V_G2, V_G3, V_G4 = range(10)
P_DSH1, P_DSC1, P_DGT1, P_DSH2, P_DSC2, P_DGT2, P_DG1, P_DG2, P_DG3, P_DG4, P_DGN, P_LOSS = range(12)
N_PAY = 12


def _cp(sem=None, **kw):
    if sem is not None:
        kw["dimension_semantics"] = sem
    return pltpu.CompilerParams(vmem_limit_bytes=VMEM_LIMIT, **kw)


def _dot(a, b):
    return jnp.dot(a, b, preferred_element_type=F32)


def _dot_nt(a, b):
    return lax.dot_general(a, b, (((1,), (1,)), ((), ())), preferred_element_type=F32)


def _dot_tn(a, b):
    return lax.dot_general(a, b, (((0,), (0,)), ((), ())), preferred_element_type=F32)


def _row(ref, i):
    return ref[i:i + 1, :]


def _rms(v):
    return lax.rsqrt(jnp.mean(v * v, axis=1, keepdims=True) + EPS)


def _colsum(v):
    return jnp.sum(v, axis=0, keepdims=True)


def _rowmean(v):
    return jnp.mean(v, axis=1, keepdims=True)


def _sigmoid(v):
    return 1.0 / (1.0 + jnp.exp(-v))


def _cast_bf16(w, kidx, dep, name):
    rows, cols = w.shape
    tr = min(rows, 512)

    def body(k_ref, w_ref, dep_ref, o_ref):
        o_ref[...] = w_ref[...].astype(BF)

    return pl.pallas_call(
        body, name=name,
        grid_spec=pltpu.PrefetchScalarGridSpec(
            num_scalar_prefetch=1, grid=(rows // tr,),
            in_specs=[pl.BlockSpec((tr, cols), lambda i, k_ref: (i, 0)), ANY],
            out_specs=pl.BlockSpec((None, tr, cols), lambda i, k_ref: (k_ref[0], i, 0))),
        out_shape=jax.ShapeDtypeStruct((N_CHIPS, rows, cols), BF),
        compiler_params=_cp(("parallel",)),
    )(kidx, w, dep)


def _adamw_math(w, g, m, v):
    m = ADAM_B1 * m + (1.0 - ADAM_B1) * g
    v = ADAM_B2 * v + (1.0 - ADAM_B2) * (g * g)
    m_hat = m / (1.0 - ADAM_B1 ** ADAM_STEP)
    v_hat = v / (1.0 - ADAM_B2 ** ADAM_STEP)
    delta = -ADAM_LR * (m_hat / (jnp.sqrt(v_hat) + ADAM_EPS) + ADAM_WD * w)
    return delta, m, v


def _adamw(w, m, v, g, name, dep=None):
    rows, cols = w.shape
    tr = min(rows, 256)

    def body(w_ref, m_ref, v_ref, g_ref, go_ref, d_ref, mo_ref, vo_ref):
        gg = g_ref[...]
        d, mm, vv = _adamw_math(w_ref[...], gg, m_ref[...], v_ref[...])
        go_ref[...] = gg
        d_ref[...] = d
        mo_ref[...] = mm
        vo_ref[...] = vv

    spec = pl.BlockSpec((tr, cols), lambda i: (i, 0))
    shp = jax.ShapeDtypeStruct((rows, cols), F32)
    body, in_specs, args = _add_dep(body, [spec] * 4, [w, m, v, g], dep)
    return pl.pallas_call(
        body, name=name, grid=(rows // tr,),
        in_specs=in_specs, out_specs=[spec] * 4, out_shape=[shp] * 4,
        compiler_params=_cp(("parallel",)),
    )(*args)


def _place():
    x, y, c = lax.axis_index("x"), lax.axis_index("y"), lax.axis_index("c")
    return x, y, c


HBM = pl.BlockSpec(memory_space=pltpu.HBM)
SEM = pl.BlockSpec(memory_space=pltpu.SEMAPHORE)
EFFECT = pltpu.SideEffectType.DATAFLOW_SIDE_EFFECTING


def _add_dep(body, in_specs, args, dep):
    if dep is None:
        return body, list(in_specs), list(args)
    n = len(args)

    def wrapped(*refs):
        body(*refs[:n], *refs[n + 1:])

    return wrapped, list(in_specs) + [ANY], list(args) + [dep]


def _split_call(name, bufs, run, old=None, after=None, new=0):
    nb = len(bufs)
    n_old = 2 if old is not None else 0
    n_in = nb + n_old + (1 if after is not None else 0)

    def body(*refs):
        old_sems = (refs[nb], refs[nb + 1]) if old is not None else None
        new_sems = (refs[n_in], refs[n_in + 1]) if new else None
        run(refs[:nb], old_sems, new_sems)
        if new:
            refs[-1][...] = jnp.zeros_like(refs[-1])

    in_specs = [HBM] * nb + [SEM] * n_old + ([ANY] if after is not None else [])
    out_shape = [pltpu.SemaphoreType.DMA((new,))] * 2 if new else []
    out_specs = [SEM, SEM] if new else []
    out_shape += [pltpu.HBM(b.shape, b.dtype) for b in bufs]
    out_specs += [HBM] * nb
    if new:
        out_shape.append(jax.ShapeDtypeStruct((8, 128), F32))
        out_specs.append(pl.BlockSpec(memory_space=pltpu.VMEM))
    first = 2 if new else 0
    args = [pltpu.with_memory_space_constraint(b, pltpu.HBM) for b in bufs]
    if old is not None:
        args += [old[0], old[1]]
    if after is not None:
        args.append(after)
    outs = pl.pallas_call(
        body, name=name, in_specs=tuple(in_specs), out_specs=tuple(out_specs), out_shape=tuple(out_shape),
        input_output_aliases={i: i + first for i in range(nb)},
        compiler_params=pltpu.CompilerParams(has_side_effects=EFFECT),
    )(*args)
    thru = list(outs[first:first + nb])
    if new:
        return thru, (outs[0], outs[1]), outs[-1]
    return thru, None, None


def _remote(part_src, part_dst, sems, i, to):
    return pltpu.make_async_remote_copy(src_ref=part_src, dst_ref=part_dst, send_sem=sems[0].at[i],
                                        recv_sem=sems[1].at[i], device_id=to, device_id_type=MESH)


def _other_chips(x, y):
    return [(1 - x, y), (x, 1 - y), (1 - x, 1 - y)]


def _gather_start(name, bufs, after=None):
    def run(refs, old, new):
        x, y, c = _place()
        k = 2 * x + y
        for w, ref in enumerate(refs):
            rh = bufs[w].shape[1] // 2
            part = ref.at[k, pl.ds(c * rh, rh)]
            for j, (cx, cy) in enumerate(_other_chips(x, y)):
                _remote(part, part, new, 3 * w + j, (cx, cy, c)).start()

    return _split_call(name, bufs, run, after=after, new=3 * len(bufs))


def _gather_pass(name, bufs, sems, after):
    def run(refs, old, new):
        x, y, c = _place()
        k = 2 * x + y
        sib = (x, y, 1 - c)
        for w, ref in enumerate(refs):
            rh = bufs[w].shape[1] // 2
            for j, (cx, cy) in enumerate(_other_chips(x, y)):
                land = ref.at[2 * cx + cy, pl.ds(c * rh, rh)]
                _remote(land, land, old, 3 * w + j, (cx, cy, c)).wait_recv()
                _remote(land, land, new, 3 * w + j, sib).start()
        for w, ref in enumerate(refs):
            rh = bufs[w].shape[1] // 2
            part = ref.at[k, pl.ds(c * rh, rh)]
            for j, (cx, cy) in enumerate(_other_chips(x, y)):
                _remote(part, part, old, 3 * w + j, (cx, cy, c)).wait_send()

    return _split_call(name, bufs, run, old=sems, after=after, new=3 * len(bufs))


def _gather_finish(name, bufs, sems, after):
    def run(refs, old, new):
        x, y, c = _place()
        sib = (x, y, 1 - c)
        for w, ref in enumerate(refs):
            rh = bufs[w].shape[1] // 2
            for j, (cx, cy) in enumerate(_other_chips(x, y)):
                sent = ref.at[2 * cx + cy, pl.ds(c * rh, rh)]
                _remote(sent, sent, old, 3 * w + j, sib).wait_send()
                land = ref.at[2 * cx + cy, pl.ds((1 - c) * rh, rh)]
                _remote(land, land, old, 3 * w + j, sib).wait_recv()

    return _split_call(name, bufs, run, old=sems, after=after)[0]


def _pair_send_start(name, grads):
    n = len(grads)
    lands = [lax.empty((N_CHIPS,) + g.shape[2:], g.dtype) for g in grads]

    def run(refs, old, new):
        x, y, c = _place()
        for w in range(n):
            _remote(refs[w].at[:, 1 - c], refs[n + w], new, w, (x, y, 1 - c)).start()

    return _split_call(name, list(grads) + lands, run, new=n)


def _pair_send_wait(name, bufs, sems, after):
    n = len(bufs) // 2

    def run(refs, old, new):
        x, y, c = _place()
        for w in range(n):
            cp = _remote(refs[w].at[:, 1 - c], refs[n + w], old, w, (x, y, 1 - c))
            cp.wait_send()
            cp.wait_recv()

    thru = _split_call(name, bufs, run, old=sems, after=after)[0]
    return thru[:n], thru[n:]


def _run_sets(name, scalars, sets, deps=()):
    starts = np.concatenate([[0], np.cumsum([s[0] for s in sets])]).tolist()

    def spec(block, index, lo, n):
        return pl.BlockSpec(block, lambda i, *sc: index(jnp.clip(i - lo, 0, n - 1), *sc))

    in_specs, out_specs, out_shape, args = [], [], [], []
    for (n, ins, outs, _), lo in zip(sets, starts):
        for array, block, index in ins:
            in_specs.append(spec(block, index, lo, n))
            args.append(array)
        for shape, dtype, block, index in outs:
            out_specs.append(spec(block, index, lo, n))
            out_shape.append(jax.ShapeDtypeStruct(shape, dtype))

    def body(*refs):
        refs = refs[len(scalars):]
        n_in = len(in_specs)
        i = pl.program_id(0)
        pos_in, pos_out = 0, n_in + len(deps)
        for (n, ins, outs, fn), lo in zip(sets, starts):
            in_refs = refs[pos_in:pos_in + len(ins)]
            out_refs = refs[pos_out:pos_out + len(outs)]
            pos_in += len(ins)
            pos_out += len(outs)
            pl.when((i >= lo) & (i < lo + n))(functools.partial(fn, in_refs, out_refs))

    outs = pl.pallas_call(
        body, name=name,
        grid_spec=pltpu.PrefetchScalarGridSpec(
            num_scalar_prefetch=len(scalars), grid=(starts[-1],), in_specs=in_specs + [ANY] * len(deps),
            out_specs=out_specs),
        out_shape=out_shape,
        compiler_params=_cp(("arbitrary",)),
    )(*scalars, *args, *deps)
    result, pos = [], 0
    for _, _, outs_s, _ in sets:
        result.append(list(outs[pos:pos + len(outs_s)]))
        pos += len(outs_s)
    return result


def _adamw_all(jobs, name, dep):
    def update(in_refs, out_refs):
        gg = in_refs[3][...]
        out_refs[0][...] = gg
        out_refs[1][...], out_refs[2][...], out_refs[3][...] = _adamw_math(
            in_refs[0][...], gg, in_refs[1][...], in_refs[2][...])

    whole = lambda l: (0, 0)
    sets = [(1, [(a, a.shape, whole) for a in job], [(job[0].shape, F32, job[0].shape, whole)] * 4, update)
            for job in jobs]
    return [tuple(o) for o in _run_sets(name, [], sets, deps=[dep])]


def _pair_add_all(gs, recvs, cidx, name):
    def add(in_refs, out_refs):
        out_refs[0][...] = (in_refs[0][...].astype(F32) + in_refs[1][...].astype(F32)).astype(BF)

    sets = []
    for g, r in zip(gs, recvs):
        _, _, rh, cols = g.shape
        tr = min(rh, 256)
        sets.append((rh // tr,
                     [(g, (N_CHIPS, None, tr, cols), lambda l, c_ref: (0, c_ref[0], l, 0)),
                      (r, (N_CHIPS, tr, cols), lambda l, c_ref: (0, l, 0))],
                     [((N_CHIPS, rh, cols), BF, (N_CHIPS, tr, cols), lambda l, c_ref: (0, l, 0))], add))
    return [o[0] for o in _run_sets(name, [cidx], sets)]


def _chip_add_all(owns, parts, kidx, cidx, name):
    def add(in_refs, out_refs):
        acc = in_refs[0][...].astype(F32)
        for s in range(3):
            acc = acc + in_refs[1][s].astype(F32)
        out_refs[0][...] = acc

    sets = []
    for own, p in zip(owns, parts):
        _, rh, cols = p.shape
        tr = min(rh, 256)
        sets.append((rh // tr,
                     [(own, (None, tr, cols), lambda l, k_ref, c_ref: (k_ref[0], l, 0)),
                      (p, (3, tr, cols), lambda l, k_ref, c_ref: (0, l, 0))],
                     [((2, rh, cols), F32, (None, tr, cols), lambda l, k_ref, c_ref: (c_ref[0], l, 0))], add))
    return [o[0] for o in _run_sets(name, [kidx, cidx], sets)]


def _pair_add(g, recv, cidx, name):
    _, _, rh, cols = g.shape
    tr = min(rh, 256)

    def body(c_ref, g_ref, r_ref, o_ref):
        o_ref[...] = (g_ref[...].astype(F32) + r_ref[...].astype(F32)).astype(BF)

    return pl.pallas_call(
        body, name=name,
        grid_spec=pltpu.PrefetchScalarGridSpec(
            num_scalar_prefetch=1, grid=(rh // tr,),
            in_specs=[pl.BlockSpec((N_CHIPS, None, tr, cols), lambda i, c_ref: (0, c_ref[0], i, 0)),
                      pl.BlockSpec((N_CHIPS, tr, cols), lambda i, c_ref: (0, i, 0))],
            out_specs=pl.BlockSpec((N_CHIPS, tr, cols), lambda i, c_ref: (0, i, 0))),
        out_shape=jax.ShapeDtypeStruct((N_CHIPS, rh, cols), BF),
        compiler_params=_cp(("parallel",)),
    )(cidx, g, recv)


def _chip_send_start(name, sums):
    n = len(sums)
    lands = [lax.empty((3,) + s.shape[1:], BF) for s in sums]

    def run(refs, old, new):
        x, y, c = _place()
        for w in range(n):
            for j, (cx, cy) in enumerate(_other_chips(x, y)):
                _remote(refs[w].at[2 * cx + cy], refs[n + w].at[j], new, 3 * w + j, (cx, cy, c)).start()

    return _split_call(name, list(sums) + lands, run, new=3 * n)


def _chip_send_wait(name, bufs, sems, after):
    n = len(bufs) // 2

    def run(refs, old, new):
        x, y, c = _place()
        for w in range(n):
            for j, (cx, cy) in enumerate(_other_chips(x, y)):
                cp = _remote(refs[w].at[2 * cx + cy], refs[n + w].at[j], old, 3 * w + j, (cx, cy, c))
                cp.wait_send()
                cp.wait_recv()

    thru = _split_call(name, bufs, run, old=sems, after=after)[0]
    return thru[:n], thru[n:]


def _chip_add(own, parts, kidx, cidx, name):
    _, rh, cols = parts.shape
    tr = min(rh, 512)

    def body(k_ref, c_ref, own_ref, p_ref, o_ref):
        acc = own_ref[...].astype(F32)
        for s in range(3):
            acc = acc + p_ref[s].astype(F32)
        o_ref[...] = acc

    return pl.pallas_call(
        body, name=name,
        grid_spec=pltpu.PrefetchScalarGridSpec(
            num_scalar_prefetch=2, grid=(rh // tr,),
            in_specs=[pl.BlockSpec((None, tr, cols), lambda i, k_ref, c_ref: (k_ref[0], i, 0)),
                      pl.BlockSpec((3, tr, cols), lambda i, k_ref, c_ref: (0, i, 0))],
            out_specs=pl.BlockSpec((None, tr, cols), lambda i, k_ref, c_ref: (c_ref[0], i, 0))),
        out_shape=jax.ShapeDtypeStruct((2, rh, cols), F32),
        compiler_params=_cp(("parallel",)),
    )(kidx, cidx, own, parts)


def _pair_swap_start(name, bufs):
    def run(refs, old, new):
        x, y, c = _place()
        for w, ref in enumerate(refs):
            _remote(ref.at[c], ref.at[c], new, w, (x, y, 1 - c)).start()

    return _split_call(name, bufs, run, new=len(bufs))


def _pair_swap_wait(name, bufs, sems, after):
    def run(refs, old, new):
        x, y, c = _place()
        for w, ref in enumerate(refs):
            _remote(ref.at[c], ref.at[c], old, w, (x, y, 1 - c)).wait_send()
            _remote(ref.at[1 - c], ref.at[1 - c], old, w, (x, y, 1 - c)).wait_recv()

    return _split_call(name, bufs, run, old=sems, after=after)[0]


def _peers(x, y, c):
    out = []
    for code in range(1, 8):
        fx, fy, fc = (code >> 2) & 1, (code >> 1) & 1, code & 1
        px = 1 - x if fx else x
        py = 1 - y if fy else y
        pc = 1 - c if fc else c
        out.append((code, (px, py, pc)))
    return out


def _mod_exchange(c_row, ada_w, ada_b4, deps):
    ncol = ada_w.shape[1]

    def body(c_ref, w_ref, b_ref, *rest):
        call_ref, mod_ref, part_ref, send_sems, recv_sems = rest[len(deps):]
        x, y, c = _place()
        k = 2 * x + y
        me = 4 * x + 2 * y + c
        call_ref[pl.ds(me, 1), :] = c_ref[...]
        sends = []
        for code, peer in _peers(x, y, c):
            cp = pltpu.make_async_remote_copy(
                src_ref=c_ref, dst_ref=call_ref.at[pl.ds(me, 1), :],
                send_sem=send_sems.at[code], recv_sem=recv_sems.at[code],
                device_id=peer, device_id_type=MESH)
            cp.start()
            sends.append(cp)
        for code, (px, py, pc) in _peers(x, y, c):
            land = call_ref.at[pl.ds(4 * px + 2 * py + pc, 1), :]
            pltpu.make_async_remote_copy(
                src_ref=land, dst_ref=land, send_sem=send_sems.at[code], recv_sem=recv_sems.at[code],
                device_id=(px, py, pc), device_id_type=MESH).wait_recv()
        call = call_ref[...]
        act = call * _sigmoid(call)
        part = jnp.dot(act, w_ref[...], preferred_element_type=F32,
                       precision=lax.Precision.HIGHEST) + b_ref[pl.ds(k, 1), :]
        part_ref[...] = part
        mod_ref[pl.ds(k, 1), :] = part_ref[pl.ds(me, 1), :]
        chips = [(8 + j, peer) for j, (code, peer) in enumerate(_peers(x, y, c)) if code in (2, 4, 6)]
        for slot, (px, py, pc) in chips:
            cp = pltpu.make_async_remote_copy(
                src_ref=part_ref.at[pl.ds(4 * px + 2 * py + pc, 1), :], dst_ref=mod_ref.at[pl.ds(k, 1), :],
                send_sem=send_sems.at[slot], recv_sem=recv_sems.at[slot],
                device_id=(px, py, pc), device_id_type=MESH)
            cp.start()
            sends.append(cp)
        for slot, (px, py, pc) in chips:
            land = mod_ref.at[pl.ds(2 * px + py, 1), :]
            pltpu.make_async_remote_copy(
                src_ref=land, dst_ref=land, send_sem=send_sems.at[slot], recv_sem=recv_sems.at[slot],
                device_id=(px, py, pc), device_id_type=MESH).wait_recv()
        for cp in sends:
            cp.wait_send()

    vm = pl.BlockSpec(memory_space=pltpu.VMEM)
    return pl.pallas_call(
        body, name="mod_exchange",
        in_specs=[vm, vm, vm] + [ANY] * len(deps), out_specs=[vm, vm],
        out_shape=[jax.ShapeDtypeStruct((8, D_MODEL), F32), jax.ShapeDtypeStruct((N_CHIPS, ncol), F32)],
        scratch_shapes=[pltpu.VMEM((8, ncol), F32), pltpu.SemaphoreType.DMA((16,)),
                        pltpu.SemaphoreType.DMA((16,))],
        compiler_params=_cp(),
    )(c_row, ada_w, ada_b4, *deps)


def _small_exchange(stats, rows, c_all, wsm, msm, vsm):
    ncol = 6 * D_MODEL // N_CHIPS
    ns = len(stats)

    def body(*refs):
        call_ref, w_ref, m_ref, v_ref, gw_ref, loss_ref = refs[ns:ns + 6]
        outs = refs[ns + 6:ns + 30]
        p_ref, g_ref, all_ref, dm_ref, send_sems, recv_sems = refs[ns + 30:]
        x, y, c = _place()
        k = 2 * x + y
        me = 4 * x + 2 * y + c
        for r, (tab, row) in enumerate(rows):
            p_ref[r] = refs[tab][row:row + 1, :]
        all_ref[:, pl.ds(me, 1), :] = p_ref[...]
        sends = []
        for code, peer in _peers(x, y, c):
            cp = pltpu.make_async_remote_copy(
                src_ref=p_ref, dst_ref=all_ref.at[:, pl.ds(me, 1), :],
                send_sem=send_sems.at[code], recv_sem=recv_sems.at[code],
                device_id=peer, device_id_type=MESH)
            cp.start()
            sends.append(cp)
        for code, (px, py, pc) in _peers(x, y, c):
            land = all_ref.at[:, pl.ds(4 * px + 2 * py + pc, 1), :]
            pltpu.make_async_remote_copy(
                src_ref=land, dst_ref=land, send_sem=send_sems.at[code], recv_sem=recv_sems.at[code],
                device_id=(px, py, pc), device_id_type=MESH).wait_recv()
        for cp in sends:
            cp.wait_send()
        tot = [_colsum(all_ref[r]) for r in range(N_PAY)]
        loss_ref[...] = jnp.sum(tot[P_LOSS], axis=1, keepdims=True)
        g_ref[...] = jnp.zeros_like(g_ref)
        for r in range(P_LOSS):
            g_ref[r:r + 1, :] = tot[r]
        g = g_ref[...]
        for kind, tab in enumerate((g,) + _adamw_math(w_ref[...], g, m_ref[...], v_ref[...])):
            for r in range(6):
                outs[6 * kind][:, r * D_MODEL:(r + 1) * D_MODEL] = tab[r:r + 1, :]
            for i in range(5):
                outs[6 * kind + 1 + i][...] = tab[6 + i:7 + i, :]
        half = D_MODEL // 2
        for kk in range(N_CHIPS):
            @pl.when(k == kk)
            def _():
                r0 = 3 * (kk // 2)
                if kk % 2 == 0:
                    dm_ref[:, :D_MODEL] = all_ref[r0]
                    dm_ref[:, D_MODEL:] = all_ref[r0 + 1][:, :half]
                else:
                    dm_ref[:, :half] = all_ref[r0 + 1][:, half:]
                    dm_ref[:, half:] = all_ref[r0 + 2]
        call = call_ref[...]
        act = call * _sigmoid(call)
        gw_ref[...] = lax.dot_general(act, dm_ref[...], (((0,), (0,)), ((), ())),
                                      preferred_element_type=F32, precision=lax.Precision.HIGHEST)

    vm = pl.BlockSpec(memory_space=pltpu.VMEM)
    vectors = [jax.ShapeDtypeStruct((1, 6 * D_MODEL), F32)] + [jax.ShapeDtypeStruct((1, D_MODEL), F32)] * 5
    outs = pl.pallas_call(
        body, name="small_exchange",
        in_specs=[vm] * (ns + 4), out_specs=[vm] * 26,
        out_shape=[jax.ShapeDtypeStruct((D_MODEL, ncol), F32), jax.ShapeDtypeStruct((1, 1), F32)] + vectors * 4,
        scratch_shapes=[pltpu.VMEM((N_PAY, 1, D_MODEL), F32), pltpu.VMEM((16, D_MODEL), F32),
                        pltpu.VMEM((N_PAY, 8, D_MODEL), F32), pltpu.VMEM((8, ncol), F32),
                        pltpu.SemaphoreType.DMA((8,)), pltpu.SemaphoreType.DMA((8,))],
        compiler_params=_cp(),
    )(*stats, c_all, wsm, msm, vsm)
    return outs[0], outs[1], [outs[2 + 6 * kind:8 + 6 * kind] for kind in range(4)]


def _rope_tables(pos_col, inv_freq, sign, dep=None):
    def body(p_ref, f_ref, s_ref, cos_ref, sin_ref):
        ang = p_ref[...].astype(F32) * f_ref[...]
        cos_ref[...] = jnp.cos(ang)
        sin_ref[...] = jnp.sin(ang) * s_ref[...]

    tr = 512
    shp = jax.ShapeDtypeStruct((SEQ, 128), F32)
    body, in_specs, args = _add_dep(
        body, [pl.BlockSpec((tr, 1), lambda i: (i, 0)), pl.BlockSpec((1, 128), lambda i: (0, 0)),
               pl.BlockSpec((1, 128), lambda i: (0, 0))], [pos_col, inv_freq, sign], dep)
    return pl.pallas_call(
        body, name="rope_tables", grid=(SEQ // tr,),
        in_specs=in_specs,
        out_specs=[pl.BlockSpec((tr, 128), lambda i: (i, 0))] * 2, out_shape=[shp, shp],
        compiler_params=_cp(("parallel",)),
    )(*args)


def _resident(shape):
    nd = len(shape)
    return pl.BlockSpec(shape, lambda *_: (0,) * nd, pipeline_mode=pl.Buffered(1))


def _ln_proj(x, vecs, w_in4):
    tm = min(512, SEQ)
    wc = w_in4.shape[2]

    def body(x_ref, vec_ref, w_ref, h_ref, proj_ref):
        xx = x_ref[...]
        g = _row(vec_ref, V_G1) * (1.0 + _row(vec_ref, V_SC1))
        h = (xx * _rms(xx) * g + _row(vec_ref, V_SH1)).astype(BF)
        h_ref[...] = h
        for j in range(N_CHIPS):
            proj_ref[:, j * wc:(j + 1) * wc] = _dot(h, w_ref[j]).astype(BF)

    return pl.pallas_call(
        body, name="ln_proj", grid=(SEQ // tm,),
        in_specs=[pl.BlockSpec((tm, D_MODEL), lambda i: (i, 0)), _resident((16, D_MODEL)),
                  _resident(w_in4.shape)],
        out_specs=[pl.BlockSpec((tm, D_MODEL), lambda i: (i, 0)), pl.BlockSpec((tm, D_IN), lambda i: (i, 0))],
        out_shape=[jax.ShapeDtypeStruct((SEQ, D_MODEL), BF), jax.ShapeDtypeStruct((SEQ, D_IN), BF)],
        compiler_params=_cp(("parallel",)),
    )(x, vecs, w_in4)


def _lane_first(shape):
    lane = lax.broadcasted_iota(jnp.int32, shape, 1)
    return (lane & 32) == 0


def _rot(v, cos, sin_s):
    partner = jnp.where(_lane_first(v.shape), pltpu.roll(v, 96, 1), pltpu.roll(v, 32, 1))
    return v * cos + partner * sin_s


def _rot_t(dv, cos, sin_s):
    t = dv * sin_s
    partner = jnp.where(_lane_first(dv.shape), pltpu.roll(t, 96, 1), pltpu.roll(t, 32, 1))
    return dv * cos + partner


def _ret_mask(lg):
    t = RET_BLOCK
    ii = lax.broadcasted_iota(jnp.int32, (t, t), 0)
    jj = lax.broadcasted_iota(jnp.int32, (t, t), 1)
    dist = jnp.abs(ii - jj).astype(F32)
    future = (jj >> RET_CHUNK_SHIFT) > (ii >> RET_CHUNK_SHIFT)
    return jnp.where(future, 0.0, jnp.exp(lg * dist))


def _ret_masks(lg, mask_ref, head):
    t = RET_BLOCK
    mask = mask_ref[head]
    ti = lax.broadcasted_iota(jnp.int32, (t, 1), 0).astype(F32)
    from_start = jnp.exp(lg * (ti + 1.0))
    to_end = jnp.exp(lg * (t - 1.0 - ti))
    whole = jnp.exp(jnp.full((1, 128), lg * t, F32))
    return mask, from_start, to_end, whole


def _head_lanes(shape, hh):
    lane = lax.broadcasted_iota(jnp.int32, shape, 1)
    return (lane >> 6) == hh


def _ret_specs():
    t = RET_BLOCK
    return dict(
        q=lambda f: pl.BlockSpec((t, 512), lambda n: (f(n), C_QR // 512)),
        k=lambda f: pl.BlockSpec((t, 512), lambda n: (f(n), C_KR // 512)),
        v=lambda f: pl.BlockSpec((t, D_MODEL), lambda n: (f(n), C_VR // D_MODEL)),
        g=lambda f: pl.BlockSpec((t, D_MODEL), lambda n: (f(n), C_GR // D_MODEL)),
        tab=lambda f: pl.BlockSpec((t, 128), lambda n: (f(n), 0)),
        wide=lambda f: pl.BlockSpec((t, D_MODEL), lambda n: (f(n), 0)),
        state=lambda f: pl.BlockSpec((N_PAIRS, None, 2, 128, 128), lambda n: (0, f(n), 0, 0, 0)),
    )


def _ret_fwd(proj, cos, sin_s, gn_g, log_gamma, dep=None):
    t = RET_BLOCK
    nb = SEQ // t

    def body(lg_ref, q_ref, k_ref, v_ref, g_ref, cos_ref, sin_ref, gn_ref, o_ref, retg_ref, st_ref, state, masks):
        @pl.when(pl.program_id(0) == 0)
        def _():
            state[...] = jnp.zeros_like(state)
            for head in range(2 * N_PAIRS):
                masks[head] = _ret_mask(lg_ref[head])

        cos, sn = cos_ref[...], sin_ref[...]
        for p in range(N_PAIRS):
            q = _rot(q_ref[:, 128 * p:128 * (p + 1)].astype(F32), cos, sn)
            k = _rot(k_ref[:, 128 * p:128 * (p + 1)].astype(F32), cos, sn) * QK_SCALE
            for hh in range(2):
                cols = slice(256 * p + 128 * hh, 256 * p + 128 * (hh + 1))
                lg = lg_ref[2 * p + hh]
                mask, from_start, to_end, whole = _ret_masks(lg, masks, 2 * p + hh)
                lanes = _head_lanes(q.shape, hh)
                qm = jnp.where(lanes, q, 0.0)
                km = jnp.where(lanes, k, 0.0)
                vh = v_ref[:, cols]
                sc = _dot_nt(qm.astype(BF), km.astype(BF)) * mask
                st = state[p, hh]
                st_ref[p, hh] = st
                o = _dot(sc.astype(BF), vh) + _dot((qm * from_start).astype(BF), st.astype(BF))
                state[p, hh] = whole * st + _dot_tn((km * to_end).astype(BF), vh)
                d = o - _rowmean(o)
                nh = d * lax.rsqrt(_rowmean(d * d) + EPS)
                gr = g_ref[:, cols].astype(F32)
                o_ref[:, cols] = o
                retg_ref[:, cols] = (gr * _sigmoid(gr) * nh * gn_ref[:, cols]).astype(BF)

    sp = _ret_specs()
    ident = lambda n: n
    body, in_specs, args = _add_dep(
        body, [pl.BlockSpec(memory_space=pltpu.SMEM), sp["q"](ident), sp["k"](ident), sp["v"](ident),
               sp["g"](ident), sp["tab"](ident), sp["tab"](ident), _resident((1, D_MODEL))],
        [log_gamma, proj, proj, proj, proj, cos, sin_s, gn_g], dep)
    return pl.pallas_call(
        body, name="ret_fwd", grid=(nb,),
        in_specs=in_specs,
        out_specs=[sp["wide"](ident), sp["wide"](ident), sp["state"](ident)],
        out_shape=[jax.ShapeDtypeStruct((SEQ, D_MODEL), F32), jax.ShapeDtypeStruct((SEQ, D_MODEL), BF),
                   jax.ShapeDtypeStruct((N_PAIRS, nb, 2, 128, 128), F32)],
        scratch_shapes=[pltpu.VMEM((N_PAIRS, 2, 128, 128), F32),
                        pltpu.VMEM((2 * N_PAIRS, RET_BLOCK, RET_BLOCK), F32)],
        compiler_params=_cp(("arbitrary",)),
    )(*args)


def _ret_bwd(proj, cos, sin_s, dret, states, log_gamma, dep=None):
    t = RET_BLOCK
    nb = SEQ // t

    def body(lg_ref, q_ref, k_ref, v_ref, cos_ref, sin_ref, do_ref, st_ref, dqkv_ref, dstate, masks):
        @pl.when(pl.program_id(0) == 0)
        def _():
            dstate[...] = jnp.zeros_like(dstate)
            for head in range(2 * N_PAIRS):
                masks[head] = _ret_mask(lg_ref[head])

        cos, sn = cos_ref[...], sin_ref[...]
        for p in range(N_PAIRS):
            q = _rot(q_ref[:, 128 * p:128 * (p + 1)].astype(F32), cos, sn)
            k = _rot(k_ref[:, 128 * p:128 * (p + 1)].astype(F32), cos, sn) * QK_SCALE
            dq_rot = jnp.zeros(q.shape, F32)
            dk_rot = jnp.zeros(q.shape, F32)
            for hh in range(2):
                cols = slice(256 * p + 128 * hh, 256 * p + 128 * (hh + 1))
                lg = lg_ref[2 * p + hh]
                mask, from_start, to_end, whole = _ret_masks(lg, masks, 2 * p + hh)
                lanes = _head_lanes(q.shape, hh)
                qm = jnp.where(lanes, q, 0.0)
                km = jnp.where(lanes, k, 0.0)
                qb, kb = qm.astype(BF), km.astype(BF)
                vh = v_ref[:, cols]
                do = do_ref[:, cols]
                sc = (_dot_nt(qb, kb) * mask).astype(BF)
                st = st_ref[p, hh].astype(BF)
                dst = dstate[p, hh]
                dstb = dst.astype(BF)
                k_end = (km * to_end).astype(BF)
                q_start = (qm * from_start).astype(BF)
                dqkv_ref[:, C_VR + 256 * p + 128 * hh:C_VR + 256 * p + 128 * (hh + 1)] = (
                    _dot_tn(sc, do) + _dot(k_end, dstb)).astype(BF)
                dsc = (_dot_nt(do, vh) * mask).astype(BF)
                dq_h = _dot(dsc, kb) + _dot_nt(do, st) * from_start
                dq_rot = dq_rot + jnp.where(lanes, dq_h, 0.0)
                dk_rot = dk_rot + _dot_tn(dsc, qb) + _dot_nt(vh, dstb) * to_end
                dstate[p, hh] = whole * dst + _dot_tn(q_start, do)
            dqkv_ref[:, C_QR + 128 * p:C_QR + 128 * (p + 1)] = _rot_t(dq_rot, cos, sn).astype(BF)
            dqkv_ref[:, C_KR + 128 * p:C_KR + 128 * (p + 1)] = _rot_t(dk_rot * QK_SCALE, cos, sn).astype(BF)

    sp = _ret_specs()
    rev = lambda n: nb - 1 - n
    body, in_specs, args = _add_dep(
        body, [pl.BlockSpec(memory_space=pltpu.SMEM), sp["q"](rev), sp["k"](rev), sp["v"](rev),
               sp["tab"](rev), sp["tab"](rev), sp["wide"](rev), sp["state"](rev)],
        [log_gamma, proj, proj, proj, cos, sin_s, dret, states], dep)
    return pl.pallas_call(
        body, name="ret_bwd", grid=(nb,),
        in_specs=in_specs,
        out_specs=pl.BlockSpec((t, C_GR), lambda n: (rev(n), 0)),
        out_shape=jax.ShapeDtypeStruct((SEQ, C_GR), BF),
        scratch_shapes=[pltpu.VMEM((N_PAIRS, 2, 128, 128), F32),
                        pltpu.VMEM((2 * N_PAIRS, RET_BLOCK, RET_BLOCK), F32)],
        compiler_params=_cp(("arbitrary",)),
    )(*args)


def _stack_heads(v):
    return jnp.concatenate([jnp.where(_head_lanes(v.shape, hh), v, jnp.zeros_like(v)) for hh in range(2)], axis=0)


def _unstack_heads(v):
    t = v.shape[0] // 2
    return jnp.where(_head_lanes((t, v.shape[1]), 0), v[:t], v[t:])


def _sb_masks(t, heads):
    rr = lax.broadcasted_iota(jnp.int32, (t, t), 0)
    cc = lax.broadcasted_iota(jnp.int32, (t, t), 1)
    r2 = lax.broadcasted_iota(jnp.int32, (heads * t, t), 0) & (t - 1)
    c2 = lax.broadcasted_iota(jnp.int32, (heads * t, t), 1)
    return rr, cc, c2 < r2


def _split_dot2(v, tri):
    return _dot(v.astype(BF), tri)


def _log2_sigmoids(z2):
    minus_abs = lax.bitcast_convert_type(
        lax.bitcast_convert_type(z2, jnp.uint32) | jnp.uint32(0x80000000), F32)
    ls = jnp.minimum(z2, 0.0) - jnp.log2(1.0 + jnp.exp2(minus_abs))
    return ls, ls - z2


def _sb_fwd(proj):
    t, g = SB_BLOCK, SB_GROUP
    nq = SEQ // t
    rows = 2 * g * t

    def body(q_ref, k_ref, v_ref, o_ref, tot_ref, kt_ref):
        i = pl.program_id(1)

        @pl.when(i == 0)
        def _():
            for p in range(g):
                for jj in range(nq):
                    kt_ref[p, jj] = k_ref[jj * t:(jj + 1) * t, 128 * p:128 * (p + 1)].T

        q2 = [_stack_heads((q_ref[:, 128 * p:128 * (p + 1)].astype(F32) * QK_SCALE).astype(BF)) for p in range(g)]
        rr, cc, valid = _sb_masks(t, 2 * g)
        later = (rr > cc).astype(BF)

        def tile(j, carry, diagonal):
            acc, run = carry
            z = jnp.concatenate([_dot(q2[p], kt_ref[p, j]) for p in range(g)], axis=0) * LOG2E
            ls, lm = _log2_sigmoids(z)
            if diagonal:
                lm = jnp.where(valid, lm, 0.0)
            after = _split_dot2(lm, later)
            a = jnp.exp2(ls + after + run)
            if diagonal:
                a = jnp.where(valid, a, 0.0)
            ab = a.astype(BF)
            keys = pl.ds(pl.multiple_of(j * t, t), t)
            av = jnp.concatenate([_dot(ab[2 * t * p:2 * t * (p + 1)], v_ref[keys, 128 * p:128 * (p + 1)])
                                  for p in range(g)], axis=0)
            return acc + av, run + after[:, 0:1] + lm[:, 0:1]

        carry = tile(i, (jnp.zeros((rows, 128), F32), jnp.zeros((rows, 1), F32)), True)
        acc, run = lax.fori_loop(0, i, lambda s, cr: tile(i - 1 - s, cr, False), carry)
        run = jnp.broadcast_to(run, (rows, 128))
        for p in range(g):
            o_ref[:, 128 * p:128 * (p + 1)] = _unstack_heads(acc[2 * t * p:2 * t * (p + 1)]).astype(BF)
            tot_ref[:, 128 * p:128 * (p + 1)] = _unstack_heads(run[2 * t * p:2 * t * (p + 1)])

    w = 128 * g
    return pl.pallas_call(
        body, name="sb_fwd", grid=(N_PAIRS // g, nq),
        in_specs=[pl.BlockSpec((t, w), lambda p, i: (i, C_QS // w + p)),
                  pl.BlockSpec((SEQ, w), lambda p, i: (0, C_KS // w + p)),
                  pl.BlockSpec((SEQ, w), lambda p, i: (0, C_VS // w + p))],
        out_specs=[pl.BlockSpec((t, w), lambda p, i: (i, p))] * 2,
        out_shape=[jax.ShapeDtypeStruct((SEQ, 512), BF), jax.ShapeDtypeStruct((SEQ, 512), F32)],
        scratch_shapes=[pltpu.VMEM((g, nq, 128, t), BF)],
        compiler_params=_cp(("parallel", "arbitrary")),
    )(proj, proj, proj)


def _sb_bwd(proj, dsb, tot, dep=None):
    t, g = SB_BLOCK, SB_GROUP
    nq = SEQ // t
    rows = 2 * g * t

    def body(q_ref, k_ref, v_ref, do_ref, tot_ref, dq_ref, dk_ref, dv_ref, kt_ref, vt_ref, dkt_acc, dvt_acc):
        i = pl.program_id(1)

        @pl.when(i == 0)
        def _():
            dkt_acc[...] = jnp.zeros_like(dkt_acc)
            dvt_acc[...] = jnp.zeros_like(dvt_acc)
            for p in range(g):
                for jj in range(nq):
                    kt_ref[p, jj] = k_ref[jj * t:(jj + 1) * t, 128 * p:128 * (p + 1)].T
                    vt_ref[p, jj] = v_ref[jj * t:(jj + 1) * t, 128 * p:128 * (p + 1)].T

        q2 = [_stack_heads((q_ref[:, 128 * p:128 * (p + 1)].astype(F32) * QK_SCALE).astype(BF)) for p in range(g)]
        do2 = [_stack_heads(do_ref[:, 128 * p:128 * (p + 1)]) for p in range(g)]
        q2t = [v.T for v in q2]
        do2t = [v.T for v in do2]
        tots = tot_ref[...]
        total = jnp.concatenate([tots[:, 64 * h:64 * h + 1] for h in range(2 * g)], axis=0)
        rr, cc, valid = _sb_masks(t, 2 * g)
        upto = (rr <= cc).astype(BF)
        before = (rr < cc).astype(BF)

        def part(v, p):
            return v[2 * t * p:2 * t * (p + 1)]

        def tile(j, carry, diagonal):
            dq, run_l, run_g = carry
            z = jnp.concatenate([_dot(q2[p], kt_ref[p, j]) for p in range(g)], axis=0) * LOG2E
            ls, lm = _log2_sigmoids(z)
            if diagonal:
                lm = jnp.where(valid, lm, 0.0)
            incl = _split_dot2(lm, upto)
            a = jnp.exp2(ls + (total - (incl + run_l)))
            if diagonal:
                a = jnp.where(valid, a, 0.0)
            gg = a * jnp.concatenate([_dot(do2[p], vt_ref[p, j]) for p in range(g)], axis=0)
            excl = _split_dot2(gg, before)
            dz = gg * jnp.exp2(lm) - (excl + run_g) * jnp.exp2(ls)
            if diagonal:
                dz = jnp.where(valid, dz, 0.0)
            dzb = dz.astype(BF)
            ab = a.astype(BF)
            keys = pl.ds(pl.multiple_of(j * t, t), t)
            for p in range(g):
                dkt_acc[p, j] += _dot(q2t[p], part(dzb, p))
                dvt_acc[p, j] += _dot(do2t[p], part(ab, p))
            dq_t = jnp.concatenate([_dot(part(dzb, p), k_ref[keys, 128 * p:128 * (p + 1)]) for p in range(g)], axis=0)
            return (dq + dq_t, run_l + incl[:, t - 1:t], run_g + excl[:, t - 1:t] + gg[:, t - 1:t])

        zero = jnp.zeros((rows, 1), F32)
        carry = lax.fori_loop(0, i, lambda j, cr: tile(j, cr, False), (jnp.zeros((rows, 128), F32), zero, zero))
        dq = tile(i, carry, True)[0]
        for p in range(g):
            dq_ref[:, 128 * p:128 * (p + 1)] = (_unstack_heads(part(dq, p)) * QK_SCALE).astype(BF)

        @pl.when(i == nq - 1)
        def _():
            for p in range(g):
                for jj in range(nq):
                    dk_ref[jj * t:(jj + 1) * t, 128 * p:128 * (p + 1)] = dkt_acc[p, jj].T.astype(BF)
                    dv_ref[jj * t:(jj + 1) * t, 128 * p:128 * (p + 1)] = dvt_acc[p, jj].T.astype(BF)

    w = 128 * g
    tile_spec = pl.BlockSpec((t, w), lambda p, i: (i, p))
    col_spec = pl.BlockSpec((SEQ, w), lambda p, i: (0, p))
    shp = jax.ShapeDtypeStruct((SEQ, 512), BF)
    body, in_specs, args = _add_dep(
        body, [pl.BlockSpec((t, w), lambda p, i: (i, C_QS // w + p)),
               pl.BlockSpec((SEQ, w), lambda p, i: (0, C_KS // w + p)),
               pl.BlockSpec((SEQ, w), lambda p, i: (0, C_VS // w + p)),
               tile_spec, tile_spec],
        [proj, proj, proj, dsb, tot], dep)
    return pl.pallas_call(
        body, name="sb_bwd", grid=(N_PAIRS // g, nq),
        in_specs=in_specs,
        out_specs=[tile_spec, col_spec, col_spec],
        out_shape=[shp, shp, shp],
        scratch_shapes=[pltpu.VMEM((g, nq, 128, t), BF), pltpu.VMEM((g, nq, 128, t), BF),
                        pltpu.VMEM((g, nq, 128, t), F32), pltpu.VMEM((g, nq, 128, t), F32)],
        compiler_params=_cp(("parallel", "arbitrary")),
    )(*args)


def _mix_out(retg, sb, proj, x, vecs, w_ret, w_sb4, w_out):
    tm, half = min(512, SEQ), 512

    def body(r_ref, s_ref, ar0, ar1, as0, as1, x_ref, vec_ref, wr_ref, ws_ref, wo_ref,
             mix_ref, rb_ref, sbp_ref, y_ref, h1_ref, h2_ref):
        rb = _dot(r_ref[...], wr_ref[...])
        sbv = s_ref[...]
        sbp = jnp.concatenate([_dot(sbv, ws_ref[k]) for k in range(N_CHIPS)], axis=1)
        gate_r = _sigmoid(jnp.concatenate([ar0[...], ar1[...]], axis=1).astype(F32))
        gate_s = _sigmoid(jnp.concatenate([as0[...], as1[...]], axis=1).astype(F32))
        mixed = (gate_r * rb + gate_s * sbp).astype(BF)
        mix_ref[...] = mixed
        rb_ref[...] = rb.astype(BF)
        sbp_ref[...] = sbp.astype(BF)
        y = _dot(mixed, wo_ref[...])
        h1 = x_ref[...] + _row(vec_ref, V_GT1) * (y * _rms(y)) * _row(vec_ref, V_G2)
        g = _row(vec_ref, V_G3) * (1.0 + _row(vec_ref, V_SC2))
        y_ref[...] = y
        h1_ref[...] = h1
        h2_ref[...] = (h1 * _rms(h1) * g + _row(vec_ref, V_SH2)).astype(BF)

    row = pl.BlockSpec((tm, D_MODEL), lambda i: (i, 0))
    gate = lambda c0: pl.BlockSpec((tm, half), lambda i: (i, c0 // half))
    bf = jax.ShapeDtypeStruct((SEQ, D_MODEL), BF)
    f32 = jax.ShapeDtypeStruct((SEQ, D_MODEL), F32)
    return pl.pallas_call(
        body, name="mix_out", grid=(SEQ // tm,),
        in_specs=[row, pl.BlockSpec((tm, 512), lambda i: (i, 0)), gate(C_AR), gate(C_AR + half), gate(C_AS),
                  gate(C_AS + half), row, _resident((16, D_MODEL)), _resident((D_MODEL, D_MODEL)),
                  _resident(w_sb4.shape), _resident((D_MODEL, D_MODEL))],
        out_specs=[row] * 6, out_shape=[bf, bf, bf, f32, f32, bf],
        compiler_params=_cp(("parallel",)),
    )(retg, sb, proj, proj, proj, proj, x, vecs, w_ret, w_sb4, w_out)


def _ffn_fwd_loss(h2, h1, target, vecs, w_ff14, w_ff24):
    tm = 256

    def body(h2_ref, h1_ref, t_ref, vec_ref, w1_ref, w2_ref, u_ref, a_ref, dout_ref, df_ref, st_ref):
        @pl.when(pl.program_id(0) == 0)
        def _():
            st_ref[...] = jnp.zeros_like(st_ref)

        hb = h2_ref[...]
        f = jnp.zeros((tm, D_MODEL), F32)
        for k in range(N_CHIPS):
            cols = slice(k * D_MODEL, (k + 1) * D_MODEL)
            u = _dot(hb, w1_ref[k])
            r = jnp.maximum(u, 0.0)
            act = (r * r).astype(BF)
            u_ref[:, cols] = u.astype(BF)
            a_ref[:, cols] = act
            f = f + _dot(act, w2_ref[k])
        r4 = _rms(f)
        fn = f * r4
        gt2, g4 = _row(vec_ref, V_GT2), _row(vec_ref, V_G4)
        diff = h1_ref[...] + gt2 * fn * g4 - t_ref[...]
        dout = diff * (1.0 / D_MODEL)
        dfn = dout * gt2 * g4
        dout_ref[...] = dout
        df_ref[...] = (r4 * (dfn - fn * _rowmean(dfn * fn))).astype(BF)
        st_ref[0:1, :] += _colsum(dout * fn * g4)
        st_ref[1:2, :] += _colsum(dout * gt2 * fn)
        st_ref[2:3, :] += _colsum(diff * diff) * (0.5 / D_MODEL)

    row = pl.BlockSpec((tm, D_MODEL), lambda i: (i, 0))
    wide = pl.BlockSpec((tm, D_FF), lambda i: (i, 0))
    return pl.pallas_call(
        body, name="ffn_fwd_loss", grid=(SEQ // tm,),
        in_specs=[row, row, row, _resident((16, D_MODEL)), _resident(w_ff14.shape), _resident(w_ff24.shape)],
        out_specs=[wide, wide, row, row, pl.BlockSpec((8, D_MODEL), lambda i: (0, 0))],
        out_shape=[jax.ShapeDtypeStruct((SEQ, D_FF), BF), jax.ShapeDtypeStruct((SEQ, D_FF), BF),
                   jax.ShapeDtypeStruct((SEQ, D_MODEL), F32), jax.ShapeDtypeStruct((SEQ, D_MODEL), BF),
                   jax.ShapeDtypeStruct((8, D_MODEL), F32)],
        compiler_params=_cp(("arbitrary",)),
    )(h2, h1, target, vecs, w_ff14, w_ff24)


def _ffn_bwd(df, u, h1, y, dout, vecs, w_ff14, w_ff24):
    tm = 256

    def body(df_ref, u_ref, h1_ref, y_ref, dout_ref, vec_ref, w1_ref, w2_ref, du_ref, dh1_ref, dy_ref, st_ref):
        @pl.when(pl.program_id(0) == 0)
        def _():
            st_ref[...] = jnp.zeros_like(st_ref)

        dfb = df_ref[...]
        dh2 = jnp.zeros((tm, D_MODEL), F32)
        for k in range(N_CHIPS):
            cols = slice(k * D_MODEL, (k + 1) * D_MODEL)
            da = _dot_nt(dfb, w2_ref[k])
            du = (da * (2.0 * jnp.maximum(u_ref[:, cols].astype(F32), 0.0))).astype(BF)
            du_ref[:, cols] = du
            dh2 = dh2 + _dot_nt(du, w1_ref[k])
        h1 = h1_ref[...]
        r3 = _rms(h1)
        hn3 = h1 * r3
        g3, sc2 = _row(vec_ref, V_G3), _row(vec_ref, V_SC2)
        dhn3 = dh2 * g3 * (1.0 + sc2)
        dh1 = dout_ref[...] + r3 * (dhn3 - hn3 * _rowmean(dhn3 * hn3))
        y = y_ref[...]
        r2 = _rms(y)
        yn = y * r2
        gt1, g2 = _row(vec_ref, V_GT1), _row(vec_ref, V_G2)
        dyn = dh1 * gt1 * g2
        dh1_ref[...] = dh1
        dy_ref[...] = (r2 * (dyn - yn * _rowmean(dyn * yn))).astype(BF)
        st_ref[0:1, :] += _colsum(dh2)
        st_ref[1:2, :] += _colsum(dh2 * hn3 * g3)
        st_ref[2:3, :] += _colsum(dh2 * hn3 * (1.0 + sc2))
        st_ref[3:4, :] += _colsum(dh1 * yn * g2)
        st_ref[4:5, :] += _colsum(dh1 * gt1 * yn)

    row = pl.BlockSpec((tm, D_MODEL), lambda i: (i, 0))
    wide = pl.BlockSpec((tm, D_FF), lambda i: (i, 0))
    return pl.pallas_call(
        body, name="ffn_bwd", grid=(SEQ // tm,),
        in_specs=[row, wide, row, row, row, _resident((16, D_MODEL)), _resident(w_ff14.shape),
                  _resident(w_ff24.shape)],
        out_specs=[wide, row, row, pl.BlockSpec((8, D_MODEL), lambda i: (0, 0))],
        out_shape=[jax.ShapeDtypeStruct((SEQ, D_FF), BF), jax.ShapeDtypeStruct((SEQ, D_MODEL), F32),
                   jax.ShapeDtypeStruct((SEQ, D_MODEL), BF), jax.ShapeDtypeStruct((8, D_MODEL), F32)],
        compiler_params=_cp(("arbitrary",)),
    )(df, u, h1, y, dout, vecs, w_ff14, w_ff24)


def _mix_ret_bwd(dy, proj, rb, sbp, o_raw, gn_g, w_out, w_sb4, w_ret):
    tm, half = min(512, SEQ), 512

    def body(dy_ref, ar0, ar1, as0, as1, rb_ref, sbp_ref, g_ref, o_ref, gn_ref, wo_ref, ws_ref, wr_ref,
             drb_ref, dsbp_ref, da_ref, dsb_ref, dret_ref, dgr_ref, st_ref):
        @pl.when(pl.program_id(0) == 0)
        def _():
            st_ref[...] = jnp.zeros_like(st_ref)

        dm_all = _dot_nt(dy_ref[...], wo_ref[...])
        dsb = jnp.zeros((tm, 512), F32)
        drbs = []
        for hf, (ar_ref, as_ref) in enumerate(((ar0, as0), (ar1, as1))):
            cols = slice(half * hf, half * (hf + 1))
            dm = dm_all[:, cols]
            sr = _sigmoid(ar_ref[...].astype(F32))
            ss = _sigmoid(as_ref[...].astype(F32))
            dsbp = (dm * ss).astype(BF)
            drbs.append((dm * sr).astype(BF))
            dsbp_ref[:, cols] = dsbp
            da_ref[:, cols] = (dm * rb_ref[:, cols].astype(F32) * sr * (1.0 - sr)).astype(BF)
            da_ref[:, D_MODEL + half * hf:D_MODEL + half * (hf + 1)] = (
                dm * sbp_ref[:, cols].astype(F32) * ss * (1.0 - ss)).astype(BF)
            dsb = dsb + _dot_nt(dsbp[:, :256], ws_ref[2 * hf]) + _dot_nt(dsbp[:, 256:], ws_ref[2 * hf + 1])
        dsb_ref[...] = dsb.astype(BF)
        drb = jnp.concatenate(drbs, axis=1)
        drb_ref[...] = drb
        dretg = _dot_nt(drb, wr_ref[...])
        for gi in range(D_MODEL // 128):
            cols = slice(128 * gi, 128 * (gi + 1))
            o = o_ref[:, cols]
            d = o - _rowmean(o)
            rstd = lax.rsqrt(_rowmean(d * d) + EPS)
            nh = d * rstd
            gain = gn_ref[:, cols]
            gr = g_ref[:, cols].astype(F32)
            sg = _sigmoid(gr)
            dg = dretg[:, cols]
            dgn = dg * gr * sg
            dnh = dgn * gain
            dgr_ref[:, cols] = (dg * nh * gain * sg * (1.0 + gr * (1.0 - sg))).astype(BF)
            dret_ref[:, cols] = (rstd * (dnh - _rowmean(dnh) - nh * _rowmean(dnh * nh))).astype(BF)
            st_ref[0:1, cols] += _colsum(dgn * nh)

    row = pl.BlockSpec((tm, D_MODEL), lambda i: (i, 0))
    gate = lambda c0: pl.BlockSpec((tm, half), lambda i: (i, c0 // half))
    shp = jax.ShapeDtypeStruct((SEQ, D_MODEL), BF)
    return pl.pallas_call(
        body, name="mix_ret_bwd", grid=(SEQ // tm,),
        in_specs=[row, gate(C_AR), gate(C_AR + half), gate(C_AS), gate(C_AS + half), row, row,
                  pl.BlockSpec((tm, D_MODEL), lambda i: (i, C_GR // D_MODEL)), row, _resident((1, D_MODEL)),
                  _resident((D_MODEL, D_MODEL)), _resident(w_sb4.shape), _resident((D_MODEL, D_MODEL))],
        out_specs=[row, row, pl.BlockSpec((tm, 2 * D_MODEL), lambda i: (i, 0)), pl.BlockSpec((tm, 512), lambda i: (i, 0)),
                   row, row, pl.BlockSpec((8, D_MODEL), lambda i: (0, 0))],
        out_shape=[shp, shp, jax.ShapeDtypeStruct((SEQ, 2 * D_MODEL), BF), jax.ShapeDtypeStruct((SEQ, 512), BF),
                   shp, shp, jax.ShapeDtypeStruct((8, D_MODEL), F32)],
        compiler_params=_cp(("arbitrary",)),
    )(dy, proj, proj, proj, proj, rb, sbp, proj, o_raw, gn_g, w_out, w_sb4, w_ret)


def _dproj_segments(widths):
    wc = D_IN // N_CHIPS
    segs, start = [], 0
    for pi, width in enumerate(widths):
        lo = start
        while lo < start + width:
            j = lo // wc
            hi = min(start + width, (j + 1) * wc)
            segs.append((j, lo - j * wc, pi, lo - start, hi - lo))
            lo = hi
        start += width
    assert start == D_IN
    return segs


def _in_proj_bwd(pieces, x, dh1, vecs, w_in4, updates, dep=None):
    tm = 256
    steps = SEQ // tm
    n, nu = len(pieces), len(updates)
    segs = _dproj_segments([p.shape[1] for p in pieces])

    def body(*refs):
        x_ref, dh1_ref, vec_ref, w_ref = refs[n:n + 4]
        upd_in = refs[n + 4:n + 4 + 4 * nu]
        dx_ref, st_ref = refs[n + 4 + 4 * nu:n + 6 + 4 * nu]
        upd_out = refs[n + 6 + 4 * nu:]
        for u in range(nu):
            w_u, m_u, v_u, g_u = upd_in[4 * u:4 * u + 4]
            go_u, d_u, mo_u, vo_u = upd_out[4 * u:4 * u + 4]
            gg = g_u[...]
            go_u[...] = gg
            d_u[...], mo_u[...], vo_u[...] = _adamw_math(w_u[...], gg, m_u[...], v_u[...])

        @pl.when(pl.program_id(0) == 0)
        def _():
            st_ref[...] = jnp.zeros_like(st_ref)

        dh = jnp.zeros((tm, D_MODEL), F32)
        for j, so, pi, po, width in segs:
            dh = dh + _dot_nt(refs[pi][:, po:po + width], w_ref[j, :, so:so + width])
        xx = x_ref[...]
        r1 = _rms(xx)
        xn = xx * r1
        g1, sc1 = _row(vec_ref, V_G1), _row(vec_ref, V_SC1)
        dxn = dh * g1 * (1.0 + sc1)
        dx_ref[...] = dh1_ref[...] + r1 * (dxn - xn * _rowmean(dxn * xn))
        st_ref[0:1, :] += _colsum(dh)
        st_ref[1:2, :] += _colsum(dh * xn * g1)
        st_ref[2:3, :] += _colsum(dh * xn * (1.0 + sc1))

    row = pl.BlockSpec((tm, D_MODEL), lambda i: (i, 0))
    upd_specs, upd_shapes, upd_args = [], [], []
    for arrays in updates:
        rows, cols = arrays[0].shape
        upd_specs += [pl.BlockSpec((rows // steps, cols), lambda i: (i, 0))] * 4
        upd_shapes += [jax.ShapeDtypeStruct((rows, cols), F32)] * 4
        upd_args += list(arrays)
    body, in_specs, args = _add_dep(
        body, [pl.BlockSpec((tm, p.shape[1]), lambda i: (i, 0)) for p in pieces] + [
            row, row, _resident((16, D_MODEL)), _resident(w_in4.shape)] + upd_specs,
        list(pieces) + [x, dh1, vecs, w_in4] + upd_args, dep)
    outs = pl.pallas_call(
        body, name="in_proj_bwd", grid=(steps,),
        in_specs=in_specs,
        out_specs=[row, pl.BlockSpec((8, D_MODEL), lambda i: (0, 0))] + upd_specs,
        out_shape=[jax.ShapeDtypeStruct((SEQ, D_MODEL), F32), jax.ShapeDtypeStruct((8, D_MODEL), F32)] + upd_shapes,
        compiler_params=_cp(("arbitrary",)),
    )(*args)
    return outs[0], outs[1], [tuple(outs[2 + 4 * u:6 + 4 * u]) for u in range(nu)]


def _grad_w_in(h, pieces, dep=None):
    ta = 512
    n = len(pieces)
    segs = _dproj_segments([p.shape[1] for p in pieces])

    def body(*refs):
        h_ref, o_ref = refs[n], refs[n + 1]
        hh = h_ref[...]
        for j, so, pi, po, width in segs:
            o_ref[j, :, so:so + width] = _dot_tn(hh, refs[pi][:, po:po + width]).astype(BF)

    body, in_specs, args = _add_dep(
        body, [_resident(p.shape) for p in pieces] + [pl.BlockSpec((SEQ, ta), lambda i: (0, i))],
        list(pieces) + [h], dep)
    return pl.pallas_call(
        body, name="grad_w_in", grid=(D_MODEL // ta,),
        in_specs=in_specs,
        out_specs=pl.BlockSpec((N_CHIPS, ta, D_IN // N_CHIPS), lambda i: (0, i, 0)),
        out_shape=jax.ShapeDtypeStruct((N_CHIPS, D_MODEL, D_IN // N_CHIPS), BF),
        compiler_params=_cp(("parallel",)),
    )(*args)


def _weight_grads(jobs, name):
    def grad(in_refs, out_refs):
        out_refs[0][...] = _dot_tn(in_refs[0][...], in_refs[1][...]).astype(BF)

    sets = []
    for a, b, col_sharded in jobs:
        ka, nb_ = a.shape[1], b.shape[1]
        if col_sharded:
            ta, tb = ka, nb_ // N_CHIPS
            ins = [(a, (SEQ, ta), lambda l: (0, 0)), (b, (SEQ, tb), lambda l: (0, l))]
        else:
            ta, tb = ka // N_CHIPS, nb_
            ins = [(a, (SEQ, ta), lambda l: (0, l)), (b, (SEQ, tb), lambda l: (0, 0))]
        sets.append((N_CHIPS, ins, [((N_CHIPS, ta, tb), BF, (None, ta, tb), lambda l: (l, 0, 0))], grad))
    return [o[0] for o in _run_sets(name, [], sets)]


def _rope_constants():
    freq = np.float32(ROPE_BASE) ** (-np.arange(0, 64, 2, dtype=np.float32) / np.float32(64))
    inv = np.tile(freq.astype(np.float32), 4).reshape(1, 128)
    sign = np.tile(np.concatenate([-np.ones(32, np.float32), np.ones(32, np.float32)]), 2).reshape(1, 128)
    return jnp.asarray(inv), jnp.asarray(sign)


def _log_gamma():
    return jnp.asarray(np.log1p(-(2.0 ** (-5.0 - np.arange(8, dtype=np.float64)))).astype(np.float32))


def _halves(g):
    return g.reshape(N_CHIPS, 2, g.shape[1] // 2, g.shape[2])


def kernel(x, c, positions, ada_w, ada_b, pre_mix_g, post_mix_g, pre_ffn_g, post_ffn_g, w_in, ret_gn_g, w_ret_branch, w_sb_branch, w_out, w_ff1, w_ff2, loss_target, m_ada_w, m_ada_b, m_pre_mix_g, m_post_mix_g, m_pre_ffn_g, m_post_ffn_g, m_w_in, m_ret_gn_g, m_w_ret_branch, m_w_sb_branch, m_w_out, m_w_ff1, m_w_ff2, v_ada_w, v_ada_b, v_pre_mix_g, v_post_mix_g, v_pre_ffn_g, v_post_ffn_g, v_w_in, v_ret_gn_g, v_w_ret_branch, v_w_sb_branch, v_w_out, v_w_ff1, v_w_ff2):
    names = ["w_in", "w_ret", "w_sb", "w_out", "w_ff1", "w_ff2"]
    big = dict(zip(names, [w_in, w_ret_branch, w_sb_branch, w_out, w_ff1, w_ff2]))
    big_m = dict(zip(names, [m_w_in, m_w_ret_branch, m_w_sb_branch, m_w_out, m_w_ff1, m_w_ff2]))
    big_v = dict(zip(names, [v_w_in, v_w_ret_branch, v_w_sb_branch, v_w_out, v_w_ff1, v_w_ff2]))
    rest = names[1:]
    cidx = lax.axis_index("c").astype(jnp.int32).reshape(1)
    kidx = (2 * lax.axis_index("x") + lax.axis_index("y")).astype(jnp.int32).reshape(1)
    x0, target = x[0], loss_target[0]

    buf_in, sem_in, tok_in = _gather_start("gather_in_start", [_cast_bf16(w_in[0], kidx, c, "cast_w_in")])
    rest_bufs = [_cast_bf16(big[nm][0], kidx, tok_in, "cast_" + nm) for nm in rest]
    inv_freq, sign = _rope_constants()
    lg = _log_gamma()
    cos, sin_s = _rope_tables(positions.reshape(SEQ, 1), inv_freq, sign, dep=tok_in)

    def table(b6, g5):
        return jnp.concatenate([b6.reshape(6, D_MODEL)] + g5 + [jnp.zeros((5, D_MODEL), F32)], axis=0)

    wsm = table(ada_b, [pre_mix_g, post_mix_g, pre_ffn_g, post_ffn_g, ret_gn_g])
    msm = table(m_ada_b, [m_pre_mix_g, m_post_mix_g, m_pre_ffn_g, m_post_ffn_g, m_ret_gn_g])
    vsm = table(v_ada_b, [v_pre_mix_g, v_post_mix_g, v_pre_ffn_g, v_post_ffn_g, v_ret_gn_g])
    c_all, mod4 = _mod_exchange(c, ada_w[0], ada_b.reshape(N_CHIPS, -1), rest_bufs + [cos, wsm, msm, vsm])
    vecs = jnp.concatenate([mod4.reshape(6, D_MODEL), pre_mix_g, post_mix_g, pre_ffn_g, post_ffn_g,
                            jnp.zeros((6, D_MODEL), F32)], axis=0)
    buf_in, sem_in, tok_in = _gather_pass("gather_in_pass", buf_in, sem_in, vecs)
    buf_rest, sem_rest, tok_rest = _gather_start("gather_rest_start", rest_bufs, after=tok_in)
    (w_in4,) = _gather_finish("gather_in_finish", buf_in, sem_in, tok_rest)

    h, proj = _ln_proj(x0, vecs, w_in4)
    sb, tot = _sb_fwd(proj)
    buf_rest, sem_rest, tok_rest = _gather_pass("gather_rest_pass", buf_rest, sem_rest, sb)
    o_raw, retg, states = _ret_fwd(proj, cos, sin_s, ret_gn_g, lg, dep=tok_rest)
    w_ret4, w_sb4, w_out4, w_ff14, w_ff24 = _gather_finish("gather_rest_finish", buf_rest, sem_rest, retg)
    w_ret = w_ret4.reshape(D_MODEL, D_MODEL)
    w_out2 = w_out4.reshape(D_MODEL, D_MODEL)
    mixed, rb, sbp, y, h1, h2 = _mix_out(retg, sb, proj, x0, vecs, w_ret, w_sb4, w_out2)
    u, act, dout, df, st_a = _ffn_fwd_loss(h2, h1, target, vecs, w_ff14, w_ff24)

    du, dh1, dy, st_b = _ffn_bwd(df, u, h1, y, dout, vecs, w_ff14, w_ff24)
    grads = dict(zip(("w_ff2", "w_ff1"), _weight_grads([(act, df, False), (h2, du, True)], "grad_w_ff")))
    drb, dsbp, da, dsb, dret, dg_r, st_c = _mix_ret_bwd(dy, proj, rb, sbp, o_raw, ret_gn_g, w_out2, w_sb4, w_ret)
    grads.update(zip(("w_out", "w_ret", "w_sb"), _weight_grads(
        [(mixed, dy, False), (retg, drb, False), (sb, dsbp, True)], "grad_w_branches")))

    bufs, sems, tok = _pair_send_start("rs_rest_pair_send", [_halves(grads[nm]) for nm in rest])
    dqkv_r = _ret_bwd(proj, cos, sin_s, dret, states, lg, dep=tok)
    mine, theirs = _pair_send_wait("rs_rest_pair_recv", bufs, sems, dqkv_r)
    pair_sums = _pair_add_all(mine, theirs, cidx, "pair_add_rest")
    bufs, sems, tok = _chip_send_start("rs_rest_chip_send", pair_sums)
    dq_s, dk_s, dv_s = _sb_bwd(proj, dsb, tot, dep=tok)
    own, parts = _chip_send_wait("rs_rest_chip_recv", bufs, sems, dq_s)
    sums = _chip_add_all(own, parts, kidx, cidx, "chip_add_rest")
    bufs, sems, tok = _pair_swap_start("rs_rest_pair_swap", sums)
    dproj = [dqkv_r, dg_r, dq_s, dk_s, dv_s, da]
    g_in = _grad_w_in(h, dproj, dep=tok)
    full_rest = _pair_swap_wait("rs_rest_pair_swapped", bufs, sems, g_in)

    full_rest = dict(zip(rest, full_rest))
    bufs, sems, tok = _pair_send_start("rs_in_pair_send", [_halves(g_in)])
    small_w = ("w_out", "w_sb", "w_ret")
    out = dict(zip(small_w, _adamw_all(
        [(big[nm][0], big_m[nm][0], big_v[nm][0], full_rest[nm].reshape(big[nm].shape[1:])) for nm in small_w],
        "adamw_small", tok)))
    mine, theirs = _pair_send_wait("rs_in_pair_recv", bufs, sems, out["w_ret"][1])
    bufs, sems, tok = _chip_send_start("rs_in_chip_send", [_pair_add(mine[0], theirs[0], cidx, "pair_add_w_in")])
    riding = ("w_ff2", "w_ff1")
    dx, st_d, updated = _in_proj_bwd(
        dproj, x0, dh1, vecs, w_in4,
        [(big[nm][0], big_m[nm][0], big_v[nm][0], full_rest[nm].reshape(big[nm].shape[1:])) for nm in riding],
        dep=tok)
    out.update(zip(riding, updated))

    a_, b_, c_, d_ = range(4)
    payload_rows = [(d_, 0), (d_, 1), (b_, 3), (b_, 0), (b_, 1), (a_, 0),
                    (d_, 2), (b_, 4), (b_, 2), (a_, 1), (c_, 0), (a_, 2)]
    g_ada, loss, small = _small_exchange([st_a, st_b, st_c, st_d], payload_rows, c_all, wsm, msm, vsm)
    own, parts = _chip_send_wait("rs_in_chip_recv", bufs, sems, g_ada)
    bufs, sems, tok = _pair_swap_start(
        "rs_in_pair_swap", [_chip_add(own[0], parts[0], kidx, cidx, "chip_add_w_in")])
    ada_out = _adamw(ada_w[0], m_ada_w[0], v_ada_w[0], g_ada, "adamw_ada_w", dep=tok)
    (full_in,) = _pair_swap_wait("rs_in_pair_swapped", bufs, sems, ada_out[1])
    out["w_in"] = _adamw(w_in[0], m_w_in[0], v_w_in[0], full_in.reshape(w_in.shape[1:]), "adamw_w_in")

    def ordered(which):
        sm = small[which]
        bg = [out[nm][which][None] for nm in names]
        return [ada_out[which][None], sm[0], sm[1], sm[2], sm[3], sm[4], bg[0], sm[5]] + bg[1:]

    return (loss.reshape(()), dx[None], *ordered(0), *ordered(1), *ordered(2), *ordered(3))
```

```python
import functools

import numpy as np
import jax
import jax.numpy as jnp
from jax import lax
from jax.experimental import pallas as pl
from jax.experimental.pallas import tpu as pltpu

SEQ = 2048
D_MODEL = 1024
D_IN = 6656
D_FF = 4096
N_CHIPS = 4
EPS = 1e-6
ROPE_BASE = 10000.0
RET_BLOCK = 256
RET_CHUNK_SHIFT = 6
SB_BLOCK = 256
QK_SCALE = 0.125
LOG2E = 1.4426950408889634
N_PAIRS = 4
SB_GROUP = 4

ADAM_LR = 0.001
ADAM_B1 = 0.9
ADAM_B2 = 0.999
ADAM_EPS = 1e-08
ADAM_WD = 0.01
ADAM_STEP = 10

BF = jnp.bfloat16
F32 = jnp.float32
MESH = pl.DeviceIdType.MESH
VMEM_LIMIT = 56 * 1024 * 1024
ANY = pl.BlockSpec(memory_space=pl.ANY)

C_QR, C_KR, C_VR, C_GR, C_QS, C_KS, C_VS, C_AR, C_AS = 0, 512, 1024, 2048, 3072, 3584, 4096, 4608, 5632

V_SH1, V_SC1, V_GT1, V_SH2, V_SC2, V_GT2, V_G1, V_G2, V_G3, V_G4 = range(10)
P_DSH1, P_DSC1, P_DGT1, P_DSH2, P_DSC2, P_DGT2, P_DG1, P_DG2, P_DG3, P_DG4, P_DGN, P_LOSS = range(12)
N_PAY = 12


def _cp(sem=None, **kw):
    if sem is not None:
        kw["dimension_semantics"] = sem
    return pltpu.CompilerParams(vmem_limit_bytes=VMEM_LIMIT, **kw)


def _dot(a, b):
    return jnp.dot(a, b, preferred_element_type=F32)


def _dot_nt(a, b):
    return lax.dot_general(a, b, (((1,), (1,)), ((), ())), preferred_element_type=F32)


def _dot_tn(a, b):
    return lax.dot_general(a, b, (((0,), (0,)), ((), ())), preferred_element_type=F32)


def _row(ref, i):
    return ref[i:i + 1, :]


def _rms(v):
    return lax.rsqrt(jnp.mean(v * v, axis=1, keepdims=True) + EPS)


def _colsum(v):
    return jnp.sum(v, axis=0, keepdims=True)


def _rowmean(v):
    return jnp.mean(v, axis=1, keepdims=True)


def _sigmoid(v):
    return 1.0 / (1.0 + jnp.exp(-v))


def _cast_bf16(w, kidx, dep, name):
    rows, cols = w.shape
    tr = min(rows, 512)

    def body(k_ref, w_ref, dep_ref, o_ref):
        o_ref[...] = w_ref[...].astype(BF)

    return pl.pallas_call(
        body, name=name,
        grid_spec=pltpu.PrefetchScalarGridSpec(
            num_scalar_prefetch=1, grid=(rows // tr,),
            in_specs=[pl.BlockSpec((tr, cols), lambda i, k_ref: (i, 0)), ANY],
            out_specs=pl.BlockSpec((None, tr, cols), lambda i, k_ref: (k_ref[0], i, 0))),
        out_shape=jax.ShapeDtypeStruct((N_CHIPS, rows, cols), BF),
        compiler_params=_cp(("parallel",)),
    )(kidx, w, dep)


def _adamw_math(w, g, m, v):
    m = ADAM_B1 * m + (1.0 - ADAM_B1) * g
    v = ADAM_B2 * v + (1.0 - ADAM_B2) * (g * g)
    m_hat = m / (1.0 - ADAM_B1 ** ADAM_STEP)
    v_hat = v / (1.0 - ADAM_B2 ** ADAM_STEP)
    delta = -ADAM_LR * (m_hat / (jnp.sqrt(v_hat) + ADAM_EPS) + ADAM_WD * w)
    return delta, m, v


def _adamw(w, m, v, g, name, dep=None):
    rows, cols = w.shape
    tr = min(rows, 256)

    def body(w_ref, m_ref, v_ref, g_ref, go_ref, d_ref, mo_ref, vo_ref):
        gg = g_ref[...]
        d, mm, vv = _adamw_math(w_ref[...], gg, m_ref[...], v_ref[...])
        go_ref[...] = gg
        d_ref[...] = d
        mo_ref[...] = mm
        vo_ref[...] = vv

    spec = pl.BlockSpec((tr, cols), lambda i: (i, 0))
    shp = jax.ShapeDtypeStruct((rows, cols), F32)
    body, in_specs, args = _add_dep(body, [spec] * 4, [w, m, v, g], dep)
    return pl.pallas_call(
        body, name=name, grid=(rows // tr,),
        in_specs=in_specs, out_specs=[spec] * 4, out_shape=[shp] * 4,
        compiler_params=_cp(("parallel",)),
    )(*args)


def _place():
    x, y, c = lax.axis_index("x"), lax.axis_index("y"), lax.axis_index("c")
    return x, y, c


HBM = pl.BlockSpec(memory_space=pltpu.HBM)
SEM = pl.BlockSpec(memory_space=pltpu.SEMAPHORE)
EFFECT = pltpu.SideEffectType.DATAFLOW_SIDE_EFFECTING


def _add_dep(body, in_specs, args, dep):
    if dep is None:
        return body, list(in_specs), list(args)
    n = len(args)

    def wrapped(*refs):
        body(*refs[:n], *refs[n + 1:])

    return wrapped, list(in_specs) + [ANY], list(args) + [dep]


def _split_call(name, bufs, run, old=None, after=None, new=0):
    nb = len(bufs)
    n_old = 2 if old is not None else 0
    n_in = nb + n_old + (1 if after is not None else 0)

    def body(*refs):
        old_sems = (refs[nb], refs[nb + 1]) if old is not None else None
        new_sems = (refs[n_in], refs[n_in + 1]) if new else None
        run(refs[:nb], old_sems, new_sems)
        if new:
            refs[-1][...] = jnp.zeros_like(refs[-1])

    in_specs = [HBM] * nb + [SEM] * n_old + ([ANY] if after is not None else [])
    out_shape = [pltpu.SemaphoreType.DMA((new,))] * 2 if new else []
    out_specs = [SEM, SEM] if new else []
    out_shape += [pltpu.HBM(b.shape, b.dtype) for b in bufs]
    out_specs += [HBM] * nb
    if new:
        out_shape.append(jax.ShapeDtypeStruct((8, 128), F32))
        out_specs.append(pl.BlockSpec(memory_space=pltpu.VMEM))
    first = 2 if new else 0
    args = [pltpu.with_memory_space_constraint(b, pltpu.HBM) for b in bufs]
    if old is not None:
        args += [old[0], old[1]]
    if after is not None:
        args.append(after)
    outs = pl.pallas_call(
        body, name=name, in_specs=tuple(in_specs), out_specs=tuple(out_specs), out_shape=tuple(out_shape),
        input_output_aliases={i: i + first for i in range(nb)},
        compiler_params=pltpu.CompilerParams(has_side_effects=EFFECT),
    )(*args)
    thru = list(outs[first:first + nb])
    if new:
        return thru, (outs[0], outs[1]), outs[-1]
    return thru, None, None


def _remote(part_src, part_dst, sems, i, to):
    return pltpu.make_async_remote_copy(src_ref=part_src, dst_ref=part_dst, send_sem=sems[0].at[i],
                                        recv_sem=sems[1].at[i], device_id=to, device_id_type=MESH)


def _other_chips(x, y):
    return [(1 - x, y), (x, 1 - y), (1 - x, 1 - y)]


def _gather_start(name, bufs, after=None):
    def run(refs, old, new):
        x, y, c = _place()
        k = 2 * x + y
        for w, ref in enumerate(refs):
            rh = bufs[w].shape[1] // 2
            part = ref.at[k, pl.ds(c * rh, rh)]
            for j, (cx, cy) in enumerate(_other_chips(x, y)):
                _remote(part, part, new, 3 * w + j, (cx, cy, c)).start()

    return _split_call(name, bufs, run, after=after, new=3 * len(bufs))


def _gather_pass(name, bufs, sems, after):
    def run(refs, old, new):
        x, y, c = _place()
        k = 2 * x + y
        sib = (x, y, 1 - c)
        for w, ref in enumerate(refs):
            rh = bufs[w].shape[1] // 2
            for j, (cx, cy) in enumerate(_other_chips(x, y)):
                land = ref.at[2 * cx + cy, pl.ds(c * rh, rh)]
                _remote(land, land, old, 3 * w + j, (cx, cy, c)).wait_recv()
                _remote(land, land, new, 3 * w + j, sib).start()
        for w, ref in enumerate(refs):
            rh = bufs[w].shape[1] // 2
            part = ref.at[k, pl.ds(c * rh, rh)]
            for j, (cx, cy) in enumerate(_other_chips(x, y)):
                _remote(part, part, old, 3 * w + j, (cx, cy, c)).wait_send()

    return _split_call(name, bufs, run, old=sems, after=after, new=3 * len(bufs))


def _gather_finish(name, bufs, sems, after):
    def run(refs, old, new):
        x, y, c = _place()
        sib = (x, y, 1 - c)
        for w, ref in enumerate(refs):
            rh = bufs[w].shape[1] // 2
            for j, (cx, cy) in enumerate(_other_chips(x, y)):
                sent = ref.at[2 * cx + cy, pl.ds(c * rh, rh)]
                _remote(sent, sent, old, 3 * w + j, sib).wait_send()
                land = ref.at[2 * cx + cy, pl.ds((1 - c) * rh, rh)]
                _remote(land, land, old, 3 * w + j, sib).wait_recv()

    return _split_call(name, bufs, run, old=sems, after=after)[0]


def _pair_send_start(name, grads):
    n = len(grads)
    lands = [lax.empty((N_CHIPS,) + g.shape[2:], g.dtype) for g in grads]

    def run(refs, old, new):
        x, y, c = _place()
        for w in range(n):
            _remote(refs[w].at[:, 1 - c], refs[n + w], new, w, (x, y, 1 - c)).start()

    return _split_call(name, list(grads) + lands, run, new=n)


def _pair_send_wait(name, bufs, sems, after):
    n = len(bufs) // 2

    def run(refs, old, new):
        x, y, c = _place()
        for w in range(n):
            cp = _remote(refs[w].at[:, 1 - c], refs[n + w], old, w, (x, y, 1 - c))
            cp.wait_send()
            cp.wait_recv()

    thru = _split_call(name, bufs, run, old=sems, after=after)[0]
    return thru[:n], thru[n:]


def _run_sets(name, scalars, sets, deps=()):
    starts = np.concatenate([[0], np.cumsum([s[0] for s in sets])]).tolist()

    def spec(block, index, lo, n):
        return pl.BlockSpec(block, lambda i, *sc: index(jnp.clip(i - lo, 0, n - 1), *sc))

    in_specs, out_specs, out_shape, args = [], [], [], []
    for (n, ins, outs, _), lo in zip(sets, starts):
        for array, block, index in ins:
            in_specs.append(spec(block, index, lo, n))
            args.append(array)
        for shape, dtype, block, index in outs:
            out_specs.append(spec(block, index, lo, n))
            out_shape.append(jax.ShapeDtypeStruct(shape, dtype))

    def body(*refs):
        refs = refs[len(scalars):]
        n_in = len(in_specs)
        i = pl.program_id(0)
        pos_in, pos_out = 0, n_in + len(deps)
        for (n, ins, outs, fn), lo in zip(sets, starts):
            in_refs = refs[pos_in:pos_in + len(ins)]
            out_refs = refs[pos_out:pos_out + len(outs)]
            pos_in += len(ins)
            pos_out += len(outs)
            pl.when((i >= lo) & (i < lo + n))(functools.partial(fn, in_refs, out_refs))

    outs = pl.pallas_call(
        body, name=name,
        grid_spec=pltpu.PrefetchScalarGridSpec(
            num_scalar_prefetch=len(scalars), grid=(starts[-1],), in_specs=in_specs + [ANY] * len(deps),
            out_specs=out_specs),
        out_shape=out_shape,
        compiler_params=_cp(("arbitrary",)),
    )(*scalars, *args, *deps)
    result, pos = [], 0
    for _, _, outs_s, _ in sets:
        result.append(list(outs[pos:pos + len(outs_s)]))
        pos += len(outs_s)
    return result


def _adamw_all(jobs, name, dep):
    def update(in_refs, out_refs):
        gg = in_refs[3][...]
        out_refs[0][...] = gg
        out_refs[1][...], out_refs[2][...], out_refs[3][...] = _adamw_math(
            in_refs[0][...], gg, in_refs[1][...], in_refs[2][...])

    whole = lambda l: (0, 0)
    sets = [(1, [(a, a.shape, whole) for a in job], [(job[0].shape, F32, job[0].shape, whole)] * 4, update)
            for job in jobs]
    return [tuple(o) for o in _run_sets(name, [], sets, deps=[dep])]


def _pair_add_all(gs, recvs, cidx, name):
    def add(in_refs, out_refs):
        out_refs[0][...] = (in_refs[0][...].astype(F32) + in_refs[1][...].astype(F32)).astype(BF)

    sets = []
    for g, r in zip(gs, recvs):
        _, _, rh, cols = g.shape
        tr = min(rh, 256)
        sets.append((rh // tr,
                     [(g, (N_CHIPS, None, tr, cols), lambda l, c_ref: (0, c_ref[0], l, 0)),
                      (r, (N_CHIPS, tr, cols), lambda l, c_ref: (0, l, 0))],
                     [((N_CHIPS, rh, cols), BF, (N_CHIPS, tr, cols), lambda l, c_ref: (0, l, 0))], add))
    return [o[0] for o in _run_sets(name, [cidx], sets)]


def _chip_add_all(owns, parts, kidx, cidx, name):
    def add(in_refs, out_refs):
        acc = in_refs[0][...].astype(F32)
        for s in range(3):
            acc = acc + in_refs[1][s].astype(F32)
        out_refs[0][...] = acc

    sets = []
    for own, p in zip(owns, parts):
        _, rh, cols = p.shape
        tr = min(rh, 256)
        sets.append((rh // tr,
                     [(own, (None, tr, cols), lambda l, k_ref, c_ref: (k_ref[0], l, 0)),
                      (p, (3, tr, cols), lambda l, k_ref, c_ref: (0, l, 0))],
                     [((2, rh, cols), F32, (None, tr, cols), lambda l, k_ref, c_ref: (c_ref[0], l, 0))], add))
    return [o[0] for o in _run_sets(name, [kidx, cidx], sets)]


def _pair_add(g, recv, cidx, name):
    _, _, rh, cols = g.shape
    tr = min(rh, 256)

    def body(c_ref, g_ref, r_ref, o_ref):
        o_ref[...] = (g_ref[...].astype(F32) + r_ref[...].astype(F32)).astype(BF)

    return pl.pallas_call(
        body, name=name,
        grid_spec=pltpu.PrefetchScalarGridSpec(
            num_scalar_prefetch=1, grid=(rh // tr,),
            in_specs=[pl.BlockSpec((N_CHIPS, None, tr, cols), lambda i, c_ref: (0, c_ref[0], i, 0)),
                      pl.BlockSpec((N_CHIPS, tr, cols), lambda i, c_ref: (0, i, 0))],
            out_specs=pl.BlockSpec((N_CHIPS, tr, cols), lambda i, c_ref: (0, i, 0))),
        out_shape=jax.ShapeDtypeStruct((N_CHIPS, rh, cols), BF),
        compiler_params=_cp(("parallel",)),
    )(cidx, g, recv)


def _chip_send_start(name, sums):
    n = len(sums)
    lands = [lax.empty((3,) + s.shape[1:], BF) for s in sums]

    def run(refs, old, new):
        x, y, c = _place()
        for w in range(n):
            for j, (cx, cy) in enumerate(_other_chips(x, y)):
                _remote(refs[w].at[2 * cx + cy], refs[n + w].at[j], new, 3 * w + j, (cx, cy, c)).start()

    return _split_call(name, list(sums) + lands, run, new=3 * n)


def _chip_send_wait(name, bufs, sems, after):
    n = len(bufs) // 2

    def run(refs, old, new):
        x, y, c = _place()
        for w in range(n):
            for j, (cx, cy) in enumerate(_other_chips(x, y)):
                cp = _remote(refs[w].at[2 * cx + cy], refs[n + w].at[j], old, 3 * w + j, (cx, cy, c))
                cp.wait_send()
                cp.wait_recv()

    thru = _split_call(name, bufs, run, old=sems, after=after)[0]
    return thru[:n], thru[n:]


def _chip_add(own, parts, kidx, cidx, name):
    _, rh, cols = parts.shape
    tr = min(rh, 512)

    def body(k_ref, c_ref, own_ref, p_ref, o_ref):
        acc = own_ref[...].astype(F32)
        for s in range(3):
            acc = acc + p_ref[s].astype(F32)
        o_ref[...] = acc

    return pl.pallas_call(
        body, name=name,
        grid_spec=pltpu.PrefetchScalarGridSpec(
            num_scalar_prefetch=2, grid=(rh // tr,),
            in_specs=[pl.BlockSpec((None, tr, cols), lambda i, k_ref, c_ref: (k_ref[0], i, 0)),
                      pl.BlockSpec((3, tr, cols), lambda i, k_ref, c_ref: (0, i, 0))],
            out_specs=pl.BlockSpec((None, tr, cols), lambda i, k_ref, c_ref: (c_ref[0], i, 0))),
        out_shape=jax.ShapeDtypeStruct((2, rh, cols), F32),
        compiler_params=_cp(("parallel",)),
    )(kidx, cidx, own, parts)


def _pair_swap_start(name, bufs):
    def run(refs, old, new):
        x, y, c = _place()
        for w, ref in enumerate(refs):
            _remote(ref.at[c], ref.at[c], new, w, (x, y, 1 - c)).start()

    return _split_call(name, bufs, run, new=len(bufs))


def _pair_swap_wait(name, bufs, sems, after):
    def run(refs, old, new):
        x, y, c = _place()
        for w, ref in enumerate(refs):
            _remote(ref.at[c], ref.at[c], old, w, (x, y, 1 - c)).wait_send()
            _remote(ref.at[1 - c], ref.at[1 - c], old, w, (x, y, 1 - c)).wait_recv()

    return _split_call(name, bufs, run, old=sems, after=after)[0]


def _peers(x, y, c):
    out = []
    for code in range(1, 8):
        fx, fy, fc = (code >> 2) & 1, (code >> 1) & 1, code & 1
        px = 1 - x if fx else x
        py = 1 - y if fy else y
        pc = 1 - c if fc else c
        out.append((code, (px, py, pc)))
    return out


def _mod_exchange(c_row, ada_w, ada_b4, deps):
    ncol = ada_w.shape[1]

    def body(c_ref, w_ref, b_ref, *rest):
        call_ref, mod_ref, part_ref, send_sems, recv_sems = rest[len(deps):]
        x, y, c = _place()
        k = 2 * x + y
        me = 4 * x + 2 * y + c
        call_ref[pl.ds(me, 1), :] = c_ref[...]
        sends = []
        for code, peer in _peers(x, y, c):
            cp = pltpu.make_async_remote_copy(
                src_ref=c_ref, dst_ref=call_ref.at[pl.ds(me, 1), :],
                send_sem=send_sems.at[code], recv_sem=recv_sems.at[code],
                device_id=peer, device_id_type=MESH)
            cp.start()
            sends.append(cp)
        for code, (px, py, pc) in _peers(x, y, c):
            land = call_ref.at[pl.ds(4 * px + 2 * py + pc, 1), :]
            pltpu.make_async_remote_copy(
                src_ref=land, dst_ref=land, send_sem=send_sems.at[code], recv_sem=recv_sems.at[code],
                device_id=(px, py, pc), device_id_type=MESH).wait_recv()
        call = call_ref[...]
        act = call * _sigmoid(call)
        part = jnp.dot(act, w_ref[...], preferred_element_type=F32,
                       precision=lax.Precision.HIGHEST) + b_ref[pl.ds(k, 1), :]
        part_ref[...] = part
        mod_ref[pl.ds(k, 1), :] = part_ref[pl.ds(me, 1), :]
        chips = [(8 + j, peer) for j, (code, peer) in enumerate(_peers(x, y, c)) if code in (2, 4, 6)]
        for slot, (px, py, pc) in chips:
            cp = pltpu.make_async_remote_copy(
                src_ref=part_ref.at[pl.ds(4 * px + 2 * py + pc, 1), :], dst_ref=mod_ref.at[pl.ds(k, 1), :],
                send_sem=send_sems.at[slot], recv_sem=recv_sems.at[slot],
                device_id=(px, py, pc), device_id_type=MESH)
            cp.start()
            sends.append(cp)
        for slot, (px, py, pc) in chips:
            land = mod_ref.at[pl.ds(2 * px + py, 1), :]
            pltpu.make_async_remote_copy(
                src_ref=land, dst_ref=land, send_sem=send_sems.at[slot], recv_sem=recv_sems.at[slot],
                device_id=(px, py, pc), device_id_type=MESH).wait_recv()
        for cp in sends:
            cp.wait_send()

    vm = pl.BlockSpec(memory_space=pltpu.VMEM)
    return pl.pallas_call(
        body, name="mod_exchange",
        in_specs=[vm, vm, vm] + [ANY] * len(deps), out_specs=[vm, vm],
        out_shape=[jax.ShapeDtypeStruct((8, D_MODEL), F32), jax.ShapeDtypeStruct((N_CHIPS, ncol), F32)],
        scratch_shapes=[pltpu.VMEM((8, ncol), F32), pltpu.SemaphoreType.DMA((16,)),
                        pltpu.SemaphoreType.DMA((16,))],
        compiler_params=_cp(),
    )(c_row, ada_w, ada_b4, *deps)


def _small_exchange(stats, rows, c_all, wsm, msm, vsm):
    ncol = 6 * D_MODEL // N_CHIPS
    ns = len(stats)

    def body(*refs):
        call_ref, w_ref, m_ref, v_ref, gw_ref, loss_ref = refs[ns:ns + 6]
        outs = refs[ns + 6:ns + 30]
        p_ref, g_ref, all_ref, dm_ref, send_sems, recv_sems = refs[ns + 30:]
        x, y, c = _place()
        k = 2 * x + y
        me = 4 * x + 2 * y + c
        for r, (tab, row) in enumerate(rows):
            p_ref[r] = refs[tab][row:row + 1, :]
        all_ref[:, pl.ds(me, 1), :] = p_ref[...]
        sends = []
        for code, peer in _peers(x, y, c):
            cp = pltpu.make_async_remote_copy(
                src_ref=p_ref, dst_ref=all_ref.at[:, pl.ds(me, 1), :],
                send_sem=send_sems.at[code], recv_sem=recv_sems.at[code],
                device_id=peer, device_id_type=MESH)
            cp.start()
            sends.append(cp)
        for code, (px, py, pc) in _peers(x, y, c):
            land = all_ref.at[:, pl.ds(4 * px + 2 * py + pc, 1), :]
            pltpu.make_async_remote_copy(
                src_ref=land, dst_ref=land, send_sem=send_sems.at[code], recv_sem=recv_sems.at[code],
                device_id=(px, py, pc), device_id_type=MESH).wait_recv()
        for cp in sends:
            cp.wait_send()
        tot = [_colsum(all_ref[r]) for r in range(N_PAY)]
        loss_ref[...] = jnp.sum(tot[P_LOSS], axis=1, keepdims=True)
        g_ref[...] = jnp.zeros_like(g_ref)
        for r in range(P_LOSS):
            g_ref[r:r + 1, :] = tot[r]
        g = g_ref[...]
        for kind, tab in enumerate((g,) + _adamw_math(w_ref[...], g, m_ref[...], v_ref[...])):
            for r in range(6):
                outs[6 * kind][:, r * D_MODEL:(r + 1) * D_MODEL] = tab[r:r + 1, :]
            for i in range(5):
                outs[6 * kind + 1 + i][...] = tab[6 + i:7 + i, :]
        half = D_MODEL // 2
        for kk in range(N_CHIPS):
            @pl.when(k == kk)
            def _():
                r0 = 3 * (kk // 2)
                if kk % 2 == 0:
                    dm_ref[:, :D_MODEL] = all_ref[r0]
                    dm_ref[:, D_MODEL:] = all_ref[r0 + 1][:, :half]
                else:
                    dm_ref[:, :half] = all_ref[r0 + 1][:, half:]
                    dm_ref[:, half:] = all_ref[r0 + 2]
        call = call_ref[...]
        act = call * _sigmoid(call)
        gw_ref[...] = lax.dot_general(act, dm_ref[...], (((0,), (0,)), ((), ())),
                                      preferred_element_type=F32, precision=lax.Precision.HIGHEST)

    vm = pl.BlockSpec(memory_space=pltpu.VMEM)
    vectors = [jax.ShapeDtypeStruct((1, 6 * D_MODEL), F32)] + [jax.ShapeDtypeStruct((1, D_MODEL), F32)] * 5
    outs = pl.pallas_call(
        body, name="small_exchange",
        in_specs=[vm] * (ns + 4), out_specs=[vm] * 26,
        out_shape=[jax.ShapeDtypeStruct((D_MODEL, ncol), F32), jax.ShapeDtypeStruct((1, 1), F32)] + vectors * 4,
        scratch_shapes=[pltpu.VMEM((N_PAY, 1, D_MODEL), F32), pltpu.VMEM((16, D_MODEL), F32),
                        pltpu.VMEM((N_PAY, 8, D_MODEL), F32), pltpu.VMEM((8, ncol), F32),
                        pltpu.SemaphoreType.DMA((8,)), pltpu.SemaphoreType.DMA((8,))],
        compiler_params=_cp(),
    )(*stats, c_all, wsm, msm, vsm)
    return outs[0], outs[1], [outs[2 + 6 * kind:8 + 6 * kind] for kind in range(4)]


def _rope_tables(pos_col, inv_freq, sign, dep=None):
    def body(p_ref, f_ref, s_ref, cos_ref, sin_ref):
        ang = p_ref[...].astype(F32) * f_ref[...]
        cos_ref[...] = jnp.cos(ang)
        sin_ref[...] = jnp.sin(ang) * s_ref[...]

    tr = 512
    shp = jax.ShapeDtypeStruct((SEQ, 128), F32)
    body, in_specs, args = _add_dep(
        body, [pl.BlockSpec((tr, 1), lambda i: (i, 0)), pl.BlockSpec((1, 128), lambda i: (0, 0)),
               pl.BlockSpec((1, 128), lambda i: (0, 0))], [pos_col, inv_freq, sign], dep)
    return pl.pallas_call(
        body, name="rope_tables", grid=(SEQ // tr,),
        in_specs=in_specs,
        out_specs=[pl.BlockSpec((tr, 128), lambda i: (i, 0))] * 2, out_shape=[shp, shp],
        compiler_params=_cp(("parallel",)),
    )(*args)


def _resident(shape):
    nd = len(shape)
    return pl.BlockSpec(shape, lambda *_: (0,) * nd, pipeline_mode=pl.Buffered(1))


def _ln_proj(x, vecs, w_in4):
    tm = min(512, SEQ)
    wc = w_in4.shape[2]

    def body(x_ref, vec_ref, w_ref, h_ref, proj_ref):
        xx = x_ref[...]
        g = _row(vec_ref, V_G1) * (1.0 + _row(vec_ref, V_SC1))
        h = (xx * _rms(xx) * g + _row(vec_ref, V_SH1)).astype(BF)
        h_ref[...] = h
        for j in range(N_CHIPS):
            proj_ref[:, j * wc:(j + 1) * wc] = _dot(h, w_ref[j]).astype(BF)

    return pl.pallas_call(
        body, name="ln_proj", grid=(SEQ // tm,),
        in_specs=[pl.BlockSpec((tm, D_MODEL), lambda i: (i, 0)), _resident((16, D_MODEL)),
                  _resident(w_in4.shape)],
        out_specs=[pl.BlockSpec((tm, D_MODEL), lambda i: (i, 0)), pl.BlockSpec((tm, D_IN), lambda i: (i, 0))],
        out_shape=[jax.ShapeDtypeStruct((SEQ, D_MODEL), BF), jax.ShapeDtypeStruct((SEQ, D_IN), BF)],
        compiler_params=_cp(("parallel",)),
    )(x, vecs, w_in4)


def _lane_first(shape):
    lane = lax.broadcasted_iota(jnp.int32, shape, 1)
    return (lane & 32) == 0


def _rot(v, cos, sin_s):
    partner = jnp.where(_lane_first(v.shape), pltpu.roll(v, 96, 1), pltpu.roll(v, 32, 1))
    return v * cos + partner * sin_s


def _rot_t(dv, cos, sin_s):
    t = dv * sin_s
    partner = jnp.where(_lane_first(dv.shape), pltpu.roll(t, 96, 1), pltpu.roll(t, 32, 1))
    return dv * cos + partner


def _ret_mask(lg):
    t = RET_BLOCK
    ii = lax.broadcasted_iota(jnp.int32, (t, t), 0)
    jj = lax.broadcasted_iota(jnp.int32, (t, t), 1)
    dist = jnp.abs(ii - jj).astype(F32)
    future = (jj >> RET_CHUNK_SHIFT) > (ii >> RET_CHUNK_SHIFT)
    return jnp.where(future, 0.0, jnp.exp(lg * dist))


def _ret_masks(lg, mask_ref, head):
    t = RET_BLOCK
    mask = mask_ref[head]
    ti = lax.broadcasted_iota(jnp.int32, (t, 1), 0).astype(F32)
    from_start = jnp.exp(lg * (ti + 1.0))
    to_end = jnp.exp(lg * (t - 1.0 - ti))
    whole = jnp.exp(jnp.full((1, 128), lg * t, F32))
    return mask, from_start, to_end, whole


def _head_lanes(shape, hh):
    lane = lax.broadcasted_iota(jnp.int32, shape, 1)
    return (lane >> 6) == hh


def _ret_specs():
    t = RET_BLOCK
    return dict(
        q=lambda f: pl.BlockSpec((t, 512), lambda n: (f(n), C_QR // 512)),
        k=lambda f: pl.BlockSpec((t, 512), lambda n: (f(n), C_KR // 512)),
        v=lambda f: pl.BlockSpec((t, D_MODEL), lambda n: (f(n), C_VR // D_MODEL)),
        g=lambda f: pl.BlockSpec((t, D_MODEL), lambda n: (f(n), C_GR // D_MODEL)),
        tab=lambda f: pl.BlockSpec((t, 128), lambda n: (f(n), 0)),
        wide=lambda f: pl.BlockSpec((t, D_MODEL), lambda n: (f(n), 0)),
        state=lambda f: pl.BlockSpec((N_PAIRS, None, 2, 128, 128), lambda n: (0, f(n), 0, 0, 0)),
    )


def _ret_fwd(proj, cos, sin_s, gn_g, log_gamma, dep=None):
    t = RET_BLOCK
    nb = SEQ // t

    def body(lg_ref, q_ref, k_ref, v_ref, g_ref, cos_ref, sin_ref, gn_ref, o_ref, retg_ref, st_ref, state, masks):
        @pl.when(pl.program_id(0) == 0)
        def _():
            state[...] = jnp.zeros_like(state)
            for head in range(2 * N_PAIRS):
                masks[head] = _ret_mask(lg_ref[head])

        cos, sn = cos_ref[...], sin_ref[...]
        for p in range(N_PAIRS):
            q = _rot(q_ref[:, 128 * p:128 * (p + 1)].astype(F32), cos, sn)
            k = _rot(k_ref[:, 128 * p:128 * (p + 1)].astype(F32), cos, sn) * QK_SCALE
            for hh in range(2):
                cols = slice(256 * p + 128 * hh, 256 * p + 128 * (hh + 1))
                lg = lg_ref[2 * p + hh]
                mask, from_start, to_end, whole = _ret_masks(lg, masks, 2 * p + hh)
                lanes = _head_lanes(q.shape, hh)
                qm = jnp.where(lanes, q, 0.0)
                km = jnp.where(lanes, k, 0.0)
                vh = v_ref[:, cols]
                sc = _dot_nt(qm.astype(BF), km.astype(BF)) * mask
                st = state[p, hh]
                st_ref[p, hh] = st
                o = _dot(sc.astype(BF), vh) + _dot((qm * from_start).astype(BF), st.astype(BF))
                state[p, hh] = whole * st + _dot_tn((km * to_end).astype(BF), vh)
                d = o - _rowmean(o)
                nh = d * lax.rsqrt(_rowmean(d * d) + EPS)
                gr = g_ref[:, cols].astype(F32)
                o_ref[:, cols] = o
                retg_ref[:, cols] = (gr * _sigmoid(gr) * nh * gn_ref[:, cols]).astype(BF)

    sp = _ret_specs()
    ident = lambda n: n
    body, in_specs, args = _add_dep(
        body, [pl.BlockSpec(memory_space=pltpu.SMEM), sp["q"](ident), sp["k"](ident), sp["v"](ident),
               sp["g"](ident), sp["tab"](ident), sp["tab"](ident), _resident((1, D_MODEL))],
        [log_gamma, proj, proj, proj, proj, cos, sin_s, gn_g], dep)
    return pl.pallas_call(
        body, name="ret_fwd", grid=(nb,),
        in_specs=in_specs,
        out_specs=[sp["wide"](ident), sp["wide"](ident), sp["state"](ident)],
        out_shape=[jax.ShapeDtypeStruct((SEQ, D_MODEL), F32), jax.ShapeDtypeStruct((SEQ, D_MODEL), BF),
                   jax.ShapeDtypeStruct((N_PAIRS, nb, 2, 128, 128), F32)],
        scratch_shapes=[pltpu.VMEM((N_PAIRS, 2, 128, 128), F32),
                        pltpu.VMEM((2 * N_PAIRS, RET_BLOCK, RET_BLOCK), F32)],
        compiler_params=_cp(("arbitrary",)),
    )(*args)


def _ret_bwd(proj, cos, sin_s, dret, states, log_gamma, dep=None):
    t = RET_BLOCK
    nb = SEQ // t

    def body(lg_ref, q_ref, k_ref, v_ref, cos_ref, sin_ref, do_ref, st_ref, dqkv_ref, dstate, masks):
        @pl.when(pl.program_id(0) == 0)
        def _():
            dstate[...] = jnp.zeros_like(dstate)
            for head in range(2 * N_PAIRS):
                masks[head] = _ret_mask(lg_ref[head])

        cos, sn = cos_ref[...], sin_ref[...]
        for p in range(N_PAIRS):
            q = _rot(q_ref[:, 128 * p:128 * (p + 1)].astype(F32), cos, sn)
            k = _rot(k_ref[:, 128 * p:128 * (p + 1)].astype(F32), cos, sn) * QK_SCALE
            dq_rot = jnp.zeros(q.shape, F32)
            dk_rot = jnp.zeros(q.shape, F32)
            for hh in range(2):
                cols = slice(256 * p + 128 * hh, 256 * p + 128 * (hh + 1))
                lg = lg_ref[2 * p + hh]
                mask, from_start, to_end, whole = _ret_masks(lg, masks, 2 * p + hh)
                lanes = _head_lanes(q.shape, hh)
                qm = jnp.where(lanes, q, 0.0)
                km = jnp.where(lanes, k, 0.0)
                qb, kb = qm.astype(BF), km.astype(BF)
                vh = v_ref[:, cols]
                do = do_ref[:, cols]
                sc = (_dot_nt(qb, kb) * mask).astype(BF)
                st = st_ref[p, hh].astype(BF)
                dst = dstate[p, hh]
                dstb = dst.astype(BF)
                k_end = (km * to_end).astype(BF)
                q_start = (qm * from_start).astype(BF)
                dqkv_ref[:, C_VR + 256 * p + 128 * hh:C_VR + 256 * p + 128 * (hh + 1)] = (
                    _dot_tn(sc, do) + _dot(k_end, dstb)).astype(BF)
                dsc = (_dot_nt(do, vh) * mask).astype(BF)
                dq_h = _dot(dsc, kb) + _dot_nt(do, st) * from_start
                dq_rot = dq_rot + jnp.where(lanes, dq_h, 0.0)
                dk_rot = dk_rot + _dot_tn(dsc, qb) + _dot_nt(vh, dstb) * to_end
                dstate[p, hh] = whole * dst + _dot_tn(q_start, do)
            dqkv_ref[:, C_QR + 128 * p:C_QR + 128 * (p + 1)] = _rot_t(dq_rot, cos, sn).astype(BF)
            dqkv_ref[:, C_KR + 128 * p:C_KR + 128 * (p + 1)] = _rot_t(dk_rot * QK_SCALE, cos, sn).astype(BF)

    sp = _ret_specs()
    rev = lambda n: nb - 1 - n
    body, in_specs, args = _add_dep(
        body, [pl.BlockSpec(memory_space=pltpu.SMEM), sp["q"](rev), sp["k"](rev), sp["v"](rev),
               sp["tab"](rev), sp["tab"](rev), sp["wide"](rev), sp["state"](rev)],
        [log_gamma, proj, proj, proj, cos, sin_s, dret, states], dep)
    return pl.pallas_call(
        body, name="ret_bwd", grid=(nb,),
        in_specs=in_specs,
        out_specs=pl.BlockSpec((t, C_GR), lambda n: (rev(n), 0)),
        out_shape=jax.ShapeDtypeStruct((SEQ, C_GR), BF),
        scratch_shapes=[pltpu.VMEM((N_PAIRS, 2, 128, 128), F32),
                        pltpu.VMEM((2 * N_PAIRS, RET_BLOCK, RET_BLOCK), F32)],
        compiler_params=_cp(("arbitrary",)),
    )(*args)


def _stack_heads(v):
    return jnp.concatenate([jnp.where(_head_lanes(v.shape, hh), v, jnp.zeros_like(v)) for hh in range(2)], axis=0)


def _unstack_heads(v):
    t = v.shape[0] // 2
    return jnp.where(_head_lanes((t, v.shape[1]), 0), v[:t], v[t:])


def _sb_masks(t, heads):
    rr = lax.broadcasted_iota(jnp.int32, (t, t), 0)
    cc = lax.broadcasted_iota(jnp.int32, (t, t), 1)
    r2 = lax.broadcasted_iota(jnp.int32, (heads * t, t), 0) & (t - 1)
    c2 = lax.broadcasted_iota(jnp.int32, (heads * t, t), 1)
    return rr, cc, c2 < r2


def _split_dot2(v, tri):
    return _dot(v.astype(BF), tri)


def _log2_sigmoids(z2):
    minus_abs = lax.bitcast_convert_type(
        lax.bitcast_convert_type(z2, jnp.uint32) | jnp.uint32(0x80000000), F32)
    ls = jnp.minimum(z2, 0.0) - jnp.log2(1.0 + jnp.exp2(minus_abs))
    return ls, ls - z2


def _sb_fwd(proj):
    t, g = SB_BLOCK, SB_GROUP
    nq = SEQ // t
    rows = 2 * g * t

    def body(q_ref, k_ref, v_ref, o_ref, tot_ref, kt_ref):
        i = pl.program_id(1)

        @pl.when(i == 0)
        def _():
            for p in range(g):
                for jj in range(nq):
                    kt_ref[p, jj] = k_ref[jj * t:(jj + 1) * t, 128 * p:128 * (p + 1)].T

        q2 = [_stack_heads((q_ref[:, 128 * p:128 * (p + 1)].astype(F32) * QK_SCALE).astype(BF)) for p in range(g)]
        rr, cc, valid = _sb_masks(t, 2 * g)
        later = (rr > cc).astype(BF)

        def tile(j, carry, diagonal):
            acc, run = carry
            z = jnp.concatenate([_dot(q2[p], kt_ref[p, j]) for p in range(g)], axis=0) * LOG2E
            ls, lm = _log2_sigmoids(z)
            if diagonal:
                lm = jnp.where(valid, lm, 0.0)
            after = _split_dot2(lm, later)
            a = jnp.exp2(ls + after + run)
            if diagonal:
                a = jnp.where(valid, a, 0.0)
            ab = a.astype(BF)
            keys = pl.ds(pl.multiple_of(j * t, t), t)
            av = jnp.concatenate([_dot(ab[2 * t * p:2 * t * (p + 1)], v_ref[keys, 128 * p:128 * (p + 1)])
                                  for p in range(g)], axis=0)
            return acc + av, run + after[:, 0:1] + lm[:, 0:1]

        carry = tile(i, (jnp.zeros((rows, 128), F32), jnp.zeros((rows, 1), F32)), True)
        acc, run = lax.fori_loop(0, i, lambda s, cr: tile(i - 1 - s, cr, False), carry)
        run = jnp.broadcast_to(run, (rows, 128))
        for p in range(g):
            o_ref[:, 128 * p:128 * (p + 1)] = _unstack_heads(acc[2 * t * p:2 * t * (p + 1)]).astype(BF)
            tot_ref[:, 128 * p:128 * (p + 1)] = _unstack_heads(run[2 * t * p:2 * t * (p + 1)])

    w = 128 * g
    return pl.pallas_call(
        body, name="sb_fwd", grid=(N_PAIRS // g, nq),
        in_specs=[pl.BlockSpec((t, w), lambda p, i: (i, C_QS // w + p)),
                  pl.BlockSpec((SEQ, w), lambda p, i: (0, C_KS // w + p)),
                  pl.BlockSpec((SEQ, w), lambda p, i: (0, C_VS // w + p))],
        out_specs=[pl.BlockSpec((t, w), lambda p, i: (i, p))] * 2,
        out_shape=[jax.ShapeDtypeStruct((SEQ, 512), BF), jax.ShapeDtypeStruct((SEQ, 512), F32)],
        scratch_shapes=[pltpu.VMEM((g, nq, 128, t), BF)],
        compiler_params=_cp(("parallel", "arbitrary")),
    )(proj, proj, proj)


def _sb_bwd(proj, dsb, tot, dep=None):
    t, g = SB_BLOCK, SB_GROUP
    nq = SEQ // t
    rows = 2 * g * t

    def body(q_ref, k_ref, v_ref, do_ref, tot_ref, dq_ref, dk_ref, dv_ref, kt_ref, vt_ref, dkt_acc, dvt_acc):
        i = pl.program_id(1)

        @pl.when(i == 0)
        def _():
            dkt_acc[...] = jnp.zeros_like(dkt_acc)
            dvt_acc[...] = jnp.zeros_like(dvt_acc)
            for p in range(g):
                for jj in range(nq):
                    kt_ref[p, jj] = k_ref[jj * t:(jj + 1) * t, 128 * p:128 * (p + 1)].T
                    vt_ref[p, jj] = v_ref[jj * t:(jj + 1) * t, 128 * p:128 * (p + 1)].T

        q2 = [_stack_heads((q_ref[:, 128 * p:128 * (p + 1)].astype(F32) * QK_SCALE).astype(BF)) for p in range(g)]
        do2 = [_stack_heads(do_ref[:, 128 * p:128 * (p + 1)]) for p in range(g)]
        q2t = [v.T for v in q2]
        do2t = [v.T for v in do2]
        tots = tot_ref[...]
        total = jnp.concatenate([tots[:, 64 * h:64 * h + 1] for h in range(2 * g)], axis=0)
        rr, cc, valid = _sb_masks(t, 2 * g)
        upto = (rr <= cc).astype(BF)
        before = (rr < cc).astype(BF)

        def part(v, p):
            return v[2 * t * p:2 * t * (p + 1)]

        def tile(j, carry, diagonal):
            dq, run_l, run_g = carry
            z = jnp.concatenate([_dot(q2[p], kt_ref[p, j]) for p in range(g)], axis=0) * LOG2E
            ls, lm = _log2_sigmoids(z)
            if diagonal:
                lm = jnp.where(valid, lm, 0.0)
            incl = _split_dot2(lm, upto)
            a = jnp.exp2(ls + (total - (incl + run_l)))
            if diagonal:
                a = jnp.where(valid, a, 0.0)
            gg = a * jnp.concatenate([_dot(do2[p], vt_ref[p, j]) for p in range(g)], axis=0)
            excl = _split_dot2(gg, before)
            dz = gg * jnp.exp2(lm) - (excl + run_g) * jnp.exp2(ls)
            if diagonal:
                dz = jnp.where(valid, dz, 0.0)
            dzb = dz.astype(BF)
            ab = a.astype(BF)
            keys = pl.ds(pl.multiple_of(j * t, t), t)
            for p in range(g):
                dkt_acc[p, j] += _dot(q2t[p], part(dzb, p))
                dvt_acc[p, j] += _dot(do2t[p], part(ab, p))
            dq_t = jnp.concatenate([_dot(part(dzb, p), k_ref[keys, 128 * p:128 * (p + 1)]) for p in range(g)], axis=0)
            return (dq + dq_t, run_l + incl[:, t - 1:t], run_g + excl[:, t - 1:t] + gg[:, t - 1:t])

        zero = jnp.zeros((rows, 1), F32)
        carry = lax.fori_loop(0, i, lambda j, cr: tile(j, cr, False), (jnp.zeros((rows, 128), F32), zero, zero))
        dq = tile(i, carry, True)[0]
        for p in range(g):
            dq_ref[:, 128 * p:128 * (p + 1)] = (_unstack_heads(part(dq, p)) * QK_SCALE).astype(BF)

        @pl.when(i == nq - 1)
        def _():
            for p in range(g):
                for jj in range(nq):
                    dk_ref[jj * t:(jj + 1) * t, 128 * p:128 * (p + 1)] = dkt_acc[p, jj].T.astype(BF)
                    dv_ref[jj * t:(jj + 1) * t, 128 * p:128 * (p + 1)] = dvt_acc[p, jj].T.astype(BF)

    w = 128 * g
    tile_spec = pl.BlockSpec((t, w), lambda p, i: (i, p))
    col_spec = pl.BlockSpec((SEQ, w), lambda p, i: (0, p))
    shp = jax.ShapeDtypeStruct((SEQ, 512), BF)
    body, in_specs, args = _add_dep(
        body, [pl.BlockSpec((t, w), lambda p, i: (i, C_QS // w + p)),
               pl.BlockSpec((SEQ, w), lambda p, i: (0, C_KS // w + p)),
               pl.BlockSpec((SEQ, w), lambda p, i: (0, C_VS // w + p)),
               tile_spec, tile_spec],
        [proj, proj, proj, dsb, tot], dep)
    return pl.pallas_call(
        body, name="sb_bwd", grid=(N_PAIRS // g, nq),
        in_specs=in_specs,
        out_specs=[tile_spec, col_spec, col_spec],
        out_shape=[shp, shp, shp],
        scratch_shapes=[pltpu.VMEM((g, nq, 128, t), BF), pltpu.VMEM((g, nq, 128, t), BF),
                        pltpu.VMEM((g, nq, 128, t), F32), pltpu.VMEM((g, nq, 128, t), F32)],
        compiler_params=_cp(("parallel", "arbitrary")),
    )(*args)


def _mix_out(retg, sb, proj, x, vecs, w_ret, w_sb4, w_out):
    tm, half = min(512, SEQ), 512

    def body(r_ref, s_ref, ar0, ar1, as0, as1, x_ref, vec_ref, wr_ref, ws_ref, wo_ref,
             mix_ref, rb_ref, sbp_ref, y_ref, h1_ref, h2_ref):
        rb = _dot(r_ref[...], wr_ref[...])
        sbv = s_ref[...]
        sbp = jnp.concatenate([_dot(sbv, ws_ref[k]) for k in range(N_CHIPS)], axis=1)
        gate_r = _sigmoid(jnp.concatenate([ar0[...], ar1[...]], axis=1).astype(F32))
        gate_s = _sigmoid(jnp.concatenate([as0[...], as1[...]], axis=1).astype(F32))
        mixed = (gate_r * rb + gate_s * sbp).astype(BF)
        mix_ref[...] = mixed
        rb_ref[...] = rb.astype(BF)
        sbp_ref[...] = sbp.astype(BF)
        y = _dot(mixed, wo_ref[...])
        h1 = x_ref[...] + _row(vec_ref, V_GT1) * (y * _rms(y)) * _row(vec_ref, V_G2)
        g = _row(vec_ref, V_G3) * (1.0 + _row(vec_ref, V_SC2))
        y_ref[...] = y
        h1_ref[...] = h1
        h2_ref[...] = (h1 * _rms(h1) * g + _row(vec_ref, V_SH2)).astype(BF)

    row = pl.BlockSpec((tm, D_MODEL), lambda i: (i, 0))
    gate = lambda c0: pl.BlockSpec((tm, half), lambda i: (i, c0 // half))
    bf = jax.ShapeDtypeStruct((SEQ, D_MODEL), BF)
    f32 = jax.ShapeDtypeStruct((SEQ, D_MODEL), F32)
    return pl.pallas_call(
        body, name="mix_out", grid=(SEQ // tm,),
        in_specs=[row, pl.BlockSpec((tm, 512), lambda i: (i, 0)), gate(C_AR), gate(C_AR + half), gate(C_AS),
                  gate(C_AS + half), row, _resident((16, D_MODEL)), _resident((D_MODEL, D_MODEL)),
                  _resident(w_sb4.shape), _resident((D_MODEL, D_MODEL))],
        out_specs=[row] * 6, out_shape=[bf, bf, bf, f32, f32, bf],
        compiler_params=_cp(("parallel",)),
    )(retg, sb, proj, proj, proj, proj, x, vecs, w_ret, w_sb4, w_out)


def _ffn_fwd_loss(h2, h1, target, vecs, w_ff14, w_ff24):
    tm = 256

    def body(h2_ref, h1_ref, t_ref, vec_ref, w1_ref, w2_ref, u_ref, a_ref, dout_ref, df_ref, st_ref):
        @pl.when(pl.program_id(0) == 0)
        def _():
            st_ref[...] = jnp.zeros_like(st_ref)

        hb = h2_ref[...]
        f = jnp.zeros((tm, D_MODEL), F32)
        for k in range(N_CHIPS):
            cols = slice(k * D_MODEL, (k + 1) * D_MODEL)
            u = _dot(hb, w1_ref[k])
            r = jnp.maximum(u, 0.0)
            act = (r * r).astype(BF)
            u_ref[:, cols] = u.astype(BF)
            a_ref[:, cols] = act
            f = f + _dot(act, w2_ref[k])
        r4 = _rms(f)
        fn = f * r4
        gt2, g4 = _row(vec_ref, V_GT2), _row(vec_ref, V_G4)
        diff = h1_ref[...] + gt2 * fn * g4 - t_ref[...]
        dout = diff * (1.0 / D_MODEL)
        dfn = dout * gt2 * g4
        dout_ref[...] = dout
        df_ref[...] = (r4 * (dfn - fn * _rowmean(dfn * fn))).astype(BF)
        st_ref[0:1, :] += _colsum(dout * fn * g4)
        st_ref[1:2, :] += _colsum(dout * gt2 * fn)
        st_ref[2:3, :] += _colsum(diff * diff) * (0.5 / D_MODEL)

    row = pl.BlockSpec((tm, D_MODEL), lambda i: (i, 0))
    wide = pl.BlockSpec((tm, D_FF), lambda i: (i, 0))
    return pl.pallas_call(
        body, name="ffn_fwd_loss", grid=(SEQ // tm,),
        in_specs=[row, row, row, _resident((16, D_MODEL)), _resident(w_ff14.shape), _resident(w_ff24.shape)],
        out_specs=[wide, wide, row, row, pl.BlockSpec((8, D_MODEL), lambda i: (0, 0))],
        out_shape=[jax.ShapeDtypeStruct((SEQ, D_FF), BF), jax.ShapeDtypeStruct((SEQ, D_FF), BF),
                   jax.ShapeDtypeStruct((SEQ, D_MODEL), F32), jax.ShapeDtypeStruct((SEQ, D_MODEL), BF),
                   jax.ShapeDtypeStruct((8, D_MODEL), F32)],
        compiler_params=_cp(("arbitrary",)),
    )(h2, h1, target, vecs, w_ff14, w_ff24)


def _ffn_bwd(df, u, h1, y, dout, vecs, w_ff14, w_ff24):
    tm = 256

    def body(df_ref, u_ref, h1_ref, y_ref, dout_ref, vec_ref, w1_ref, w2_ref, du_ref, dh1_ref, dy_ref, st_ref):
        @pl.when(pl.program_id(0) == 0)
        def _():
            st_ref[...] = jnp.zeros_like(st_ref)

        dfb = df_ref[...]
        dh2 = jnp.zeros((tm, D_MODEL), F32)
        for k in range(N_CHIPS):
            cols = slice(k * D_MODEL, (k + 1) * D_MODEL)
            da = _dot_nt(dfb, w2_ref[k])
            du = (da * (2.0 * jnp.maximum(u_ref[:, cols].astype(F32), 0.0))).astype(BF)
            du_ref[:, cols] = du
            dh2 = dh2 + _dot_nt(du, w1_ref[k])
        h1 = h1_ref[...]
        r3 = _rms(h1)
        hn3 = h1 * r3
        g3, sc2 = _row(vec_ref, V_G3), _row(vec_ref, V_SC2)
        dhn3 = dh2 * g3 * (1.0 + sc2)
        dh1 = dout_ref[...] + r3 * (dhn3 - hn3 * _rowmean(dhn3 * hn3))
        y = y_ref[...]
        r2 = _rms(y)
        yn = y * r2
        gt1, g2 = _row(vec_ref, V_GT1), _row(vec_ref, V_G2)
        dyn = dh1 * gt1 * g2
        dh1_ref[...] = dh1
        dy_ref[...] = (r2 * (dyn - yn * _rowmean(dyn * yn))).astype(BF)
        st_ref[0:1, :] += _colsum(dh2)
        st_ref[1:2, :] += _colsum(dh2 * hn3 * g3)
        st_ref[2:3, :] += _colsum(dh2 * hn3 * (1.0 + sc2))
        st_ref[3:4, :] += _colsum(dh1 * yn * g2)
        st_ref[4:5, :] += _colsum(dh1 * gt1 * yn)

    row = pl.BlockSpec((tm, D_MODEL), lambda i: (i, 0))
    wide = pl.BlockSpec((tm, D_FF), lambda i: (i, 0))
    return pl.pallas_call(
        body, name="ffn_bwd", grid=(SEQ // tm,),
        in_specs=[row, wide, row, row, row, _resident((16, D_MODEL)), _resident(w_ff14.shape),
                  _resident(w_ff24.shape)],
        out_specs=[wide, row, row, pl.BlockSpec((8, D_MODEL), lambda i: (0, 0))],
        out_shape=[jax.ShapeDtypeStruct((SEQ, D_FF), BF), jax.ShapeDtypeStruct((SEQ, D_MODEL), F32),
                   jax.ShapeDtypeStruct((SEQ, D_MODEL), BF), jax.ShapeDtypeStruct((8, D_MODEL), F32)],
        compiler_params=_cp(("arbitrary",)),
    )(df, u, h1, y, dout, vecs, w_ff14, w_ff24)


def _mix_ret_bwd(dy, proj, rb, sbp, o_raw, gn_g, w_out, w_sb4, w_ret):
    tm, half = min(512, SEQ), 512

    def body(dy_ref, ar0, ar1, as0, as1, rb_ref, sbp_ref, g_ref, o_ref, gn_ref, wo_ref, ws_ref, wr_ref,
             drb_ref, dsbp_ref, da_ref, dsb_ref, dret_ref, dgr_ref, st_ref):
        @pl.when(pl.program_id(0) == 0)
        def _():
            st_ref[...] = jnp.zeros_like(st_ref)

        dm_all = _dot_nt(dy_ref[...], wo_ref[...])
        dsb = jnp.zeros((tm, 512), F32)
        drbs = []
        for hf, (ar_ref, as_ref) in enumerate(((ar0, as0), (ar1, as1))):
            cols = slice(half * hf, half * (hf + 1))
            dm = dm_all[:, cols]
            sr = _sigmoid(ar_ref[...].astype(F32))
            ss = _sigmoid(as_ref[...].astype(F32))
            dsbp = (dm * ss).astype(BF)
            drbs.append((dm * sr).astype(BF))
            dsbp_ref[:, cols] = dsbp
            da_ref[:, cols] = (dm * rb_ref[:, cols].astype(F32) * sr * (1.0 - sr)).astype(BF)
            da_ref[:, D_MODEL + half * hf:D_MODEL + half * (hf + 1)] = (
                dm * sbp_ref[:, cols].astype(F32) * ss * (1.0 - ss)).astype(BF)
            dsb = dsb + _dot_nt(dsbp[:, :256], ws_ref[2 * hf]) + _dot_nt(dsbp[:, 256:], ws_ref[2 * hf + 1])
        dsb_ref[...] = dsb.astype(BF)
        drb = jnp.concatenate(drbs, axis=1)
        drb_ref[...] = drb
        dretg = _dot_nt(drb, wr_ref[...])
        for gi in range(D_MODEL // 128):
            cols = slice(128 * gi, 128 * (gi + 1))
            o = o_ref[:, cols]
            d = o - _rowmean(o)
            rstd = lax.rsqrt(_rowmean(d * d) + EPS)
            nh = d * rstd
            gain = gn_ref[:, cols]
            gr = g_ref[:, cols].astype(F32)
            sg = _sigmoid(gr)
            dg = dretg[:, cols]
            dgn = dg * gr * sg
            dnh = dgn * gain
            dgr_ref[:, cols] = (dg * nh * gain * sg * (1.0 + gr * (1.0 - sg))).astype(BF)
            dret_ref[:, cols] = (rstd * (dnh - _rowmean(dnh) - nh * _rowmean(dnh * nh))).astype(BF)
            st_ref[0:1, cols] += _colsum(dgn * nh)

    row = pl.BlockSpec((tm, D_MODEL), lambda i: (i, 0))
    gate = lambda c0: pl.BlockSpec((tm, half), lambda i: (i, c0 // half))
    shp = jax.ShapeDtypeStruct((SEQ, D_MODEL), BF)
    return pl.pallas_call(
        body, name="mix_ret_bwd", grid=(SEQ // tm,),
        in_specs=[row, gate(C_AR), gate(C_AR + half), gate(C_AS), gate(C_AS + half), row, row,
                  pl.BlockSpec((tm, D_MODEL), lambda i: (i, C_GR // D_MODEL)), row, _resident((1, D_MODEL)),
                  _resident((D_MODEL, D_MODEL)), _resident(w_sb4.shape), _resident((D_MODEL, D_MODEL))],
        out_specs=[row, row, pl.BlockSpec((tm, 2 * D_MODEL), lambda i: (i, 0)), pl.BlockSpec((tm, 512), lambda i: (i, 0)),
                   row, row, pl.BlockSpec((8, D_MODEL), lambda i: (0, 0))],
        out_shape=[shp, shp, jax.ShapeDtypeStruct((SEQ, 2 * D_MODEL), BF), jax.ShapeDtypeStruct((SEQ, 512), BF),
                   shp, shp, jax.ShapeDtypeStruct((8, D_MODEL), F32)],
        compiler_params=_cp(("arbitrary",)),
    )(dy, proj, proj, proj, proj, rb, sbp, proj, o_raw, gn_g, w_out, w_sb4, w_ret)


def _dproj_segments(widths):
    wc = D_IN // N_CHIPS
    segs, start = [], 0
    for pi, width in enumerate(widths):
        lo = start
        while lo < start + width:
            j = lo // wc
            hi = min(start + width, (j + 1) * wc)
            segs.append((j, lo - j * wc, pi, lo - start, hi - lo))
            lo = hi
        start += width
    assert start == D_IN
    return segs


def _in_proj_bwd(pieces, x, dh1, vecs, w_in4, updates, dep=None):
    tm = 256
    steps = SEQ // tm
    n, nu = len(pieces), len(updates)
    segs = _dproj_segments([p.shape[1] for p in pieces])

    def body(*refs):
        x_ref, dh1_ref, vec_ref, w_ref = refs[n:n + 4]
        upd_in = refs[n + 4:n + 4 + 4 * nu]
        dx_ref, st_ref = refs[n + 4 + 4 * nu:n + 6 + 4 * nu]
        upd_out = refs[n + 6 + 4 * nu:]
        for u in range(nu):
            w_u, m_u, v_u, g_u = upd_in[4 * u:4 * u + 4]
            go_u, d_u, mo_u, vo_u = upd_out[4 * u:4 * u + 4]
            gg = g_u[...]
            go_u[...] = gg
            d_u[...], mo_u[...], vo_u[...] = _adamw_math(w_u[...], gg, m_u[...], v_u[...])

        @pl.when(pl.program_id(0) == 0)
        def _():
            st_ref[...] = jnp.zeros_like(st_ref)

        dh = jnp.zeros((tm, D_MODEL), F32)
        for j, so, pi, po, width in segs:
            dh = dh + _dot_nt(refs[pi][:, po:po + width], w_ref[j, :, so:so + width])
        xx = x_ref[...]
        r1 = _rms(xx)
        xn = xx * r1
        g1, sc1 = _row(vec_ref, V_G1), _row(vec_ref, V_SC1)
        dxn = dh * g1 * (1.0 + sc1)
        dx_ref[...] = dh1_ref[...] + r1 * (dxn - xn * _rowmean(dxn * xn))
        st_ref[0:1, :] += _colsum(dh)
        st_ref[1:2, :] += _colsum(dh * xn * g1)
        st_ref[2:3, :] += _colsum(dh * xn * (1.0 + sc1))

    row = pl.BlockSpec((tm, D_MODEL), lambda i: (i, 0))
    upd_specs, upd_shapes, upd_args = [], [], []
    for arrays in updates:
        rows, cols = arrays[0].shape
        upd_specs += [pl.BlockSpec((rows // steps, cols), lambda i: (i, 0))] * 4
        upd_shapes += [jax.ShapeDtypeStruct((rows, cols), F32)] * 4
        upd_args += list(arrays)
    body, in_specs, args = _add_dep(
        body, [pl.BlockSpec((tm, p.shape[1]), lambda i: (i, 0)) for p in pieces] + [
            row, row, _resident((16, D_MODEL)), _resident(w_in4.shape)] + upd_specs,
        list(pieces) + [x, dh1, vecs, w_in4] + upd_args, dep)
    outs = pl.pallas_call(
        body, name="in_proj_bwd", grid=(steps,),
        in_specs=in_specs,
        out_specs=[row, pl.BlockSpec((8, D_MODEL), lambda i: (0, 0))] + upd_specs,
        out_shape=[jax.ShapeDtypeStruct((SEQ, D_MODEL), F32), jax.ShapeDtypeStruct((8, D_MODEL), F32)] + upd_shapes,
        compiler_params=_cp(("arbitrary",)),
    )(*args)
    return outs[0], outs[1], [tuple(outs[2 + 4 * u:6 + 4 * u]) for u in range(nu)]


def _grad_w_in(h, pieces, dep=None):
    ta = 512
    n = len(pieces)
    segs = _dproj_segments([p.shape[1] for p in pieces])

    def body(*refs):
        h_ref, o_ref = refs[n], refs[n + 1]
        hh = h_ref[...]
        for j, so, pi, po, width in segs:
            o_ref[j, :, so:so + width] = _dot_tn(hh, refs[pi][:, po:po + width]).astype(BF)

    body, in_specs, args = _add_dep(
        body, [_resident(p.shape) for p in pieces] + [pl.BlockSpec((SEQ, ta), lambda i: (0, i))],
        list(pieces) + [h], dep)
    return pl.pallas_call(
        body, name="grad_w_in", grid=(D_MODEL // ta,),
        in_specs=in_specs,
        out_specs=pl.BlockSpec((N_CHIPS, ta, D_IN // N_CHIPS), lambda i: (0, i, 0)),
        out_shape=jax.ShapeDtypeStruct((N_CHIPS, D_MODEL, D_IN // N_CHIPS), BF),
        compiler_params=_cp(("parallel",)),
    )(*args)


def _weight_grads(jobs, name):
    def grad(in_refs, out_refs):
        out_refs[0][...] = _dot_tn(in_refs[0][...], in_refs[1][...]).astype(BF)

    sets = []
    for a, b, col_sharded in jobs:
        ka, nb_ = a.shape[1], b.shape[1]
        if col_sharded:
            ta, tb = ka, nb_ // N_CHIPS
            ins = [(a, (SEQ, ta), lambda l: (0, 0)), (b, (SEQ, tb), lambda l: (0, l))]
        else:
            ta, tb = ka // N_CHIPS, nb_
            ins = [(a, (SEQ, ta), lambda l: (0, l)), (b, (SEQ, tb), lambda l: (0, 0))]
        sets.append((N_CHIPS, ins, [((N_CHIPS, ta, tb), BF, (None, ta, tb), lambda l: (l, 0, 0))], grad))
    return [o[0] for o in _run_sets(name, [], sets)]


def _rope_constants():
    freq = np.float32(ROPE_BASE) ** (-np.arange(0, 64, 2, dtype=np.float32) / np.float32(64))
    inv = np.tile(freq.astype(np.float32), 4).reshape(1, 128)
    sign = np.tile(np.concatenate([-np.ones(32, np.float32), np.ones(32, np.float32)]), 2).reshape(1, 128)
    return jnp.asarray(inv), jnp.asarray(sign)


def _log_gamma():
    return jnp.asarray(np.log1p(-(2.0 ** (-5.0 - np.arange(8, dtype=np.float64)))).astype(np.float32))


def _halves(g):
    return g.reshape(N_CHIPS, 2, g.shape[1] // 2, g.shape[2])


def kernel(x, c, positions, ada_w, ada_b, pre_mix_g, post_mix_g, pre_ffn_g, post_ffn_g, w_in, ret_gn_g, w_ret_branch, w_sb_branch, w_out, w_ff1, w_ff2, loss_target, m_ada_w, m_ada_b, m_pre_mix_g, m_post_mix_g, m_pre_ffn_g, m_post_ffn_g, m_w_in, m_ret_gn_g, m_w_ret_branch, m_w_sb_branch, m_w_out, m_w_ff1, m_w_ff2, v_ada_w, v_ada_b, v_pre_mix_g, v_post_mix_g, v_pre_ffn_g, v_post_ffn_g, v_w_in, v_ret_gn_g, v_w_ret_branch, v_w_sb_branch, v_w_out, v_w_ff1, v_w_ff2):
    names = ["w_in", "w_ret", "w_sb", "w_out", "w_ff1", "w_ff2"]
    big = dict(zip(names, [w_in, w_ret_branch, w_sb_branch, w_out, w_ff1, w_ff2]))
    big_m = dict(zip(names, [m_w_in, m_w_ret_branch, m_w_sb_branch, m_w_out, m_w_ff1, m_w_ff2]))
    big_v = dict(zip(names, [v_w_in, v_w_ret_branch, v_w_sb_branch, v_w_out, v_w_ff1, v_w_ff2]))
    rest = names[1:]
    cidx = lax.axis_index("c").astype(jnp.int32).reshape(1)
    kidx = (2 * lax.axis_index("x") + lax.axis_index("y")).astype(jnp.int32).reshape(1)
    x0, target = x[0], loss_target[0]

    buf_in, sem_in, tok_in = _gather_start("gather_in_start", [_cast_bf16(w_in[0], kidx, c, "cast_w_in")])
    rest_bufs = [_cast_bf16(big[nm][0], kidx, tok_in, "cast_" + nm) for nm in rest]
    inv_freq, sign = _rope_constants()
    lg = _log_gamma()
    cos, sin_s = _rope_tables(positions.reshape(SEQ, 1), inv_freq, sign, dep=tok_in)

    def table(b6, g5):
        return jnp.concatenate([b6.reshape(6, D_MODEL)] + g5 + [jnp.zeros((5, D_MODEL), F32)], axis=0)

    wsm = table(ada_b, [pre_mix_g, post_mix_g, pre_ffn_g, post_ffn_g, ret_gn_g])
    msm = table(m_ada_b, [m_pre_mix_g, m_post_mix_g, m_pre_ffn_g, m_post_ffn_g, m_ret_gn_g])
    vsm = table(v_ada_b, [v_pre_mix_g, v_post_mix_g, v_pre_ffn_g, v_post_ffn_g, v_ret_gn_g])
    c_all, mod4 = _mod_exchange(c, ada_w[0], ada_b.reshape(N_CHIPS, -1), rest_bufs + [cos, wsm, msm, vsm])
    vecs = jnp.concatenate([mod4.reshape(6, D_MODEL), pre_mix_g, post_mix_g, pre_ffn_g, post_ffn_g,
                            jnp.zeros((6, D_MODEL), F32)], axis=0)
    buf_in, sem_in, tok_in = _gather_pass("gather_in_pass", buf_in, sem_in, vecs)
    buf_rest, sem_rest, tok_rest = _gather_start("gather_rest_start", rest_bufs, after=tok_in)
    (w_in4,) = _gather_finish("gather_in_finish", buf_in, sem_in, tok_rest)

    h, proj = _ln_proj(x0, vecs, w_in4)
    sb, tot = _sb_fwd(proj)
    buf_rest, sem_rest, tok_rest = _gather_pass("gather_rest_pass", buf_rest, sem_rest, sb)
    o_raw, retg, states = _ret_fwd(proj, cos, sin_s, ret_gn_g, lg, dep=tok_rest)
    w_ret4, w_sb4, w_out4, w_ff14, w_ff24 = _gather_finish("gather_rest_finish", buf_rest, sem_rest, retg)
    w_ret = w_ret4.reshape(D_MODEL, D_MODEL)
    w_out2 = w_out4.reshape(D_MODEL, D_MODEL)
    mixed, rb, sbp, y, h1, h2 = _mix_out(retg, sb, proj, x0, vecs, w_ret, w_sb4, w_out2)
    u, act, dout, df, st_a = _ffn_fwd_loss(h2, h1, target, vecs, w_ff14, w_ff24)

    du, dh1, dy, st_b = _ffn_bwd(df, u, h1, y, dout, vecs, w_ff14, w_ff24)
    grads = dict(zip(("w_ff2", "w_ff1"), _weight_grads([(act, df, False), (h2, du, True)], "grad_w_ff")))
    drb, dsbp, da, dsb, dret, dg_r, st_c = _mix_ret_bwd(dy, proj, rb, sbp, o_raw, ret_gn_g, w_out2, w_sb4, w_ret)
    grads.update(zip(("w_out", "w_ret", "w_sb"), _weight_grads(
        [(mixed, dy, False), (retg, drb, False), (sb, dsbp, True)], "grad_w_branches")))

    bufs, sems, tok = _pair_send_start("rs_rest_pair_send", [_halves(grads[nm]) for nm in rest])
    dqkv_r = _ret_bwd(proj, cos, sin_s, dret, states, lg, dep=tok)
    mine, theirs = _pair_send_wait("rs_rest_pair_recv", bufs, sems, dqkv_r)
    pair_sums = _pair_add_all(mine, theirs, cidx, "pair_add_rest")
    bufs, sems, tok = _chip_send_start("rs_rest_chip_send", pair_sums)
    dq_s, dk_s, dv_s = _sb_bwd(proj, dsb, tot, dep=tok)
    dproj = [dqkv_r, dg_r, dq_s, dk_s, dv_s, da]
    g_in = _grad_w_in(h, dproj)

    bufs_in, sems_in, tok_in = _pair_send_start("rs_in_pair_send", [_halves(g_in)])
    own, parts = _chip_send_wait("rs_rest_chip_recv", bufs, sems, tok_in)
    sums = _chip_add_all(own, parts, kidx, cidx, "chip_add_rest")
    bufs, sems, tok = _pair_swap_start("rs_rest_pair_swap", sums)
    mine, theirs = _pair_send_wait("rs_in_pair_recv", bufs_in, sems_in, tok)
    bufs_in, sems_in, tok_in = _chip_send_start(
        "rs_in_chip_send", [_pair_add(mine[0], theirs[0], cidx, "pair_add_w_in")])
    full_rest = dict(zip(rest, _pair_swap_wait("rs_rest_pair_swapped", bufs, sems, tok_in)))
    small_w = ("w_out", "w_sb", "w_ret")
    out = dict(zip(small_w, _adamw_all(
        [(big[nm][0], big_m[nm][0], big_v[nm][0], full_rest[nm].reshape(big[nm].shape[1:])) for nm in small_w],
        "adamw_small", tok_in)))
    bufs, sems, tok = bufs_in, sems_in, out["w_ret"][1]
    riding = ("w_ff2", "w_ff1")
    dx, st_d, updated = _in_proj_bwd(
        dproj, x0, dh1, vecs, w_in4,
        [(big[nm][0], big_m[nm][0], big_v[nm][0], full_rest[nm].reshape(big[nm].shape[1:])) for nm in riding],
        dep=tok)
    out.update(zip(riding, updated))

    a_, b_, c_, d_ = range(4)
    payload_rows = [(d_, 0), (d_, 1), (b_, 3), (b_, 0), (b_, 1), (a_, 0),
                    (d_, 2), (b_, 4), (b_, 2), (a_, 1), (c_, 0), (a_, 2)]
    g_ada, loss, small = _small_exchange([st_a, st_b, st_c, st_d], payload_rows, c_all, wsm, msm, vsm)
    own, parts = _chip_send_wait("rs_in_chip_recv", bufs, sems, g_ada)
    bufs, sems, tok = _pair_swap_start(
        "rs_in_pair_swap", [_chip_add(own[0], parts[0], kidx, cidx, "chip_add_w_in")])
    ada_out = _adamw(ada_w[0], m_ada_w[0], v_ada_w[0], g_ada, "adamw_ada_w", dep=tok)
    (full_in,) = _pair_swap_wait("rs_in_pair_swapped", bufs, sems, ada_out[1])
    out["w_in"] = _adamw(w_in[0], m_w_in[0], v_w_in[0], full_in.reshape(w_in.shape[1:]), "adamw_w_in")

    def ordered(which):
        sm = small[which]
        bg = [out[nm][which][None] for nm in names]
        return [ada_out[which][None], sm[0], sm[1], sm[2], sm[3], sm[4], bg[0], sm[5]] + bg[1:]

    return (loss.reshape(()), dx[None], *ordered(0), *ordered(1), *ordered(2), *ordered(3))
```

```python
import functools

import numpy as np
import jax
import jax.numpy as jnp
from jax import lax
from jax.experimental import pallas as pl
from jax.experimental.pallas import tpu as pltpu

SEQ = 2048
D_MODEL = 1024
D_IN = 6656
D_FF = 4096
N_CHIPS = 4
EPS = 1e-6
ROPE_BASE = 10000.0
RET_BLOCK = 256
RET_CHUNK_SHIFT = 6
SB_BLOCK = 256
QK_SCALE = 0.125
LOG2E = 1.4426950408889634
N_PAIRS = 4
SB_GROUP = 4

ADAM_LR = 0.001
ADAM_B1 = 0.9
ADAM_B2 = 0.999
ADAM_EPS = 1e-08
ADAM_WD = 0.01
ADAM_STEP = 10

BF = jnp.bfloat16
F32 = jnp.float32
MESH = pl.DeviceIdType.MESH
VMEM_LIMIT = 56 * 1024 * 1024
ANY = pl.BlockSpec(memory_space=pl.ANY)

C_QR, C_KR, C_VR, C_GR, C_QS, C_KS, C_VS, C_AR, C_AS = 0, 512, 1024, 2048, 3072, 3584, 4096, 4608, 5632

V_SH1, V_SC1, V_GT1, V_SH2, V_SC2, V_GT2, V_G1, V_G2, V_G3, V_G4 = range(10)
P_DSH1, P_DSC1, P_DGT1, P_DSH2, P_DSC2, P_DGT2, P_DG1, P_DG2, P_DG3, P_DG4, P_DGN, P_LOSS = range(12)
N_PAY = 12


def _cp(sem=None, **kw):
    if sem is not None:
        kw["dimension_semantics"] = sem
    return pltpu.CompilerParams(vmem_limit_bytes=VMEM_LIMIT, **kw)


def _dot(a, b):
    return jnp.dot(a, b, preferred_element_type=F32)


def _dot_nt(a, b):
    return lax.dot_general(a, b, (((1,), (1,)), ((), ())), preferred_element_type=F32)


def _dot_tn(a, b):
    return lax.dot_general(a, b, (((0,), (0,)), ((), ())), preferred_element_type=F32)


def _row(ref, i):
    return ref[i:i + 1, :]


def _rms(v):
    return lax.rsqrt(jnp.mean(v * v, axis=1, keepdims=True) + EPS)


def _colsum(v):
    return jnp.sum(v, axis=0, keepdims=True)


def _rowmean(v):
    return jnp.mean(v, axis=1, keepdims=True)


def _sigmoid(v):
    return 1.0 / (1.0 + jnp.exp(-v))


def _cast_bf16(w, kidx, dep, name):
    rows, cols = w.shape
    tr = min(rows, 512)

    def body(k_ref, w_ref, dep_ref, o_ref):
        o_ref[...] = w_ref[...].astype(BF)

    return pl.pallas_call(
        body, name=name,
        grid_spec=pltpu.PrefetchScalarGridSpec(
            num_scalar_prefetch=1, grid=(rows // tr,),
            in_specs=[pl.BlockSpec((tr, cols), lambda i, k_ref: (i, 0)), ANY],
            out_specs=pl.BlockSpec((None, tr, cols), lambda i, k_ref: (k_ref[0], i, 0))),
        out_shape=jax.ShapeDtypeStruct((N_CHIPS, rows, cols), BF),
        compiler_params=_cp(("parallel",)),
    )(kidx, w, dep)


def _adamw_math(w, g, m, v):
    m = ADAM_B1 * m + (1.0 - ADAM_B1) * g
    v = ADAM_B2 * v + (1.0 - ADAM_B2) * (g * g)
    m_hat = m / (1.0 - ADAM_B1 ** ADAM_STEP)
    v_hat = v / (1.0 - ADAM_B2 ** ADAM_STEP)
    delta = -ADAM_LR * (m_hat / (jnp.sqrt(v_hat) + ADAM_EPS) + ADAM_WD * w)
    return delta, m, v


def _adamw(w, m, v, g, name, dep=None):
    rows, cols = w.shape
    tr = min(rows, 256)

    def body(w_ref, m_ref, v_ref, g_ref, go_ref, d_ref, mo_ref, vo_ref):
        gg = g_ref[...]
        d, mm, vv = _adamw_math(w_ref[...], gg, m_ref[...], v_ref[...])
        go_ref[...] = gg
        d_ref[...] = d
        mo_ref[...] = mm
        vo_ref[...] = vv

    spec = pl.BlockSpec((tr, cols), lambda i: (i, 0))
    shp = jax.ShapeDtypeStruct((rows, cols), F32)
    body, in_specs, args = _add_dep(body, [spec] * 4, [w, m, v, g], dep)
    return pl.pallas_call(
        body, name=name, grid=(rows // tr,),
        in_specs=in_specs, out_specs=[spec] * 4, out_shape=[shp] * 4,
        compiler_params=_cp(("parallel",)),
    )(*args)


def _place():
    x, y, c = lax.axis_index("x"), lax.axis_index("y"), lax.axis_index("c")
    return x, y, c


HBM = pl.BlockSpec(memory_space=pltpu.HBM)
SEM = pl.BlockSpec(memory_space=pltpu.SEMAPHORE)
EFFECT = pltpu.SideEffectType.DATAFLOW_SIDE_EFFECTING


def _add_dep(body, in_specs, args, dep):
    if dep is None:
        return body, list(in_specs), list(args)
    n = len(args)

    def wrapped(*refs):
        body(*refs[:n], *refs[n + 1:])

    return wrapped, list(in_specs) + [ANY], list(args) + [dep]


def _split_call(name, bufs, run, old=None, after=None, new=0):
    nb = len(bufs)
    n_old = 2 if old is not None else 0
    n_in = nb + n_old + (1 if after is not None else 0)

    def body(*refs):
        old_sems = (refs[nb], refs[nb + 1]) if old is not None else None
        new_sems = (refs[n_in], refs[n_in + 1]) if new else None
        run(refs[:nb], old_sems, new_sems)
        if new:
            refs[-1][...] = jnp.zeros_like(refs[-1])

    in_specs = [HBM] * nb + [SEM] * n_old + ([ANY] if after is not None else [])
    out_shape = [pltpu.SemaphoreType.DMA((new,))] * 2 if new else []
    out_specs = [SEM, SEM] if new else []
    out_shape += [pltpu.HBM(b.shape, b.dtype) for b in bufs]
    out_specs += [HBM] * nb
    if new:
        out_shape.append(jax.ShapeDtypeStruct((8, 128), F32))
        out_specs.append(pl.BlockSpec(memory_space=pltpu.VMEM))
    first = 2 if new else 0
    args = [pltpu.with_memory_space_constraint(b, pltpu.HBM) for b in bufs]
    if old is not None:
        args += [old[0], old[1]]
    if after is not None:
        args.append(after)
    outs = pl.pallas_call(
        body, name=name, in_specs=tuple(in_specs), out_specs=tuple(out_specs), out_shape=tuple(out_shape),
        input_output_aliases={i: i + first for i in range(nb)},
        compiler_params=pltpu.CompilerParams(has_side_effects=EFFECT),
    )(*args)
    thru = list(outs[first:first + nb])
    if new:
        return thru, (outs[0], outs[1]), outs[-1]
    return thru, None, None


def _remote(part_src, part_dst, sems, i, to):
    return pltpu.make_async_remote_copy(src_ref=part_src, dst_ref=part_dst, send_sem=sems[0].at[i],
                                        recv_sem=sems[1].at[i], device_id=to, device_id_type=MESH)


def _other_chips(x, y):
    return [(1 - x, y), (x, 1 - y), (1 - x, 1 - y)]


def _gather_start(name, bufs, after=None):
    def run(refs, old, new):
        x, y, c = _place()
        k = 2 * x + y
        for w, ref in enumerate(refs):
            rh = bufs[w].shape[1] // 2
            part = ref.at[k, pl.ds(c * rh, rh)]
            for j, (cx, cy) in enumerate(_other_chips(x, y)):
                _remote(part, part, new, 3 * w + j, (cx, cy, c)).start()

    return _split_call(name, bufs, run, after=after, new=3 * len(bufs))


def _gather_pass(name, bufs, sems, after):
    def run(refs, old, new):
        x, y, c = _place()
        k = 2 * x + y
        sib = (x, y, 1 - c)
        for w, ref in enumerate(refs):
            rh = bufs[w].shape[1] // 2
            for j, (cx, cy) in enumerate(_other_chips(x, y)):
                land = ref.at[2 * cx + cy, pl.ds(c * rh, rh)]
                _remote(land, land, old, 3 * w + j, (cx, cy, c)).wait_recv()
                _remote(land, land, new, 3 * w + j, sib).start()
        for w, ref in enumerate(refs):
            rh = bufs[w].shape[1] // 2
            part = ref.at[k, pl.ds(c * rh, rh)]
            for j, (cx, cy) in enumerate(_other_chips(x, y)):
                _remote(part, part, old, 3 * w + j, (cx, cy, c)).wait_send()

    return _split_call(name, bufs, run, old=sems, after=after, new=3 * len(bufs))


def _gather_finish(name, bufs, sems, after):
    def run(refs, old, new):
        x, y, c = _place()
        sib = (x, y, 1 - c)
        for w, ref in enumerate(refs):
            rh = bufs[w].shape[1] // 2
            for j, (cx, cy) in enumerate(_other_chips(x, y)):
                sent = ref.at[2 * cx + cy, pl.ds(c * rh, rh)]
                _remote(sent, sent, old, 3 * w + j, sib).wait_send()
                land = ref.at[2 * cx + cy, pl.ds((1 - c) * rh, rh)]
                _remote(land, land, old, 3 * w + j, sib).wait_recv()

    return _split_call(name, bufs, run, old=sems, after=after)[0]


def _pair_send_start(name, grads):
    n = len(grads)
    lands = [lax.empty((N_CHIPS,) + g.shape[2:], g.dtype) for g in grads]

    def run(refs, old, new):
        x, y, c = _place()
        for w in range(n):
            _remote(refs[w].at[:, 1 - c], refs[n + w], new, w, (x, y, 1 - c)).start()

    return _split_call(name, list(grads) + lands, run, new=n)


def _pair_send_wait(name, bufs, sems, after):
    n = len(bufs) // 2

    def run(refs, old, new):
        x, y, c = _place()
        for w in range(n):
            cp = _remote(refs[w].at[:, 1 - c], refs[n + w], old, w, (x, y, 1 - c))
            cp.wait_send()
            cp.wait_recv()

    thru = _split_call(name, bufs, run, old=sems, after=after)[0]
    return thru[:n], thru[n:]


def _run_sets(name, scalars, sets, deps=()):
    starts = np.concatenate([[0], np.cumsum([s[0] for s in sets])]).tolist()

    def spec(block, index, lo, n):
        return pl.BlockSpec(block, lambda i, *sc: index(jnp.clip(i - lo, 0, n - 1), *sc))

    in_specs, out_specs, out_shape, args = [], [], [], []
    for (n, ins, outs, _), lo in zip(sets, starts):
        for array, block, index in ins:
            in_specs.append(spec(block, index, lo, n))
            args.append(array)
        for shape, dtype, block, index in outs:
            out_specs.append(spec(block, index, lo, n))
            out_shape.append(jax.ShapeDtypeStruct(shape, dtype))

    def body(*refs):
        refs = refs[len(scalars):]
        n_in = len(in_specs)
        i = pl.program_id(0)
        pos_in, pos_out = 0, n_in + len(deps)
        for (n, ins, outs, fn), lo in zip(sets, starts):
            in_refs = refs[pos_in:pos_in + len(ins)]
            out_refs = refs[pos_out:pos_out + len(outs)]
            pos_in += len(ins)
            pos_out += len(outs)
            pl.when((i >= lo) & (i < lo + n))(functools.partial(fn, in_refs, out_refs))

    outs = pl.pallas_call(
        body, name=name,
        grid_spec=pltpu.PrefetchScalarGridSpec(
            num_scalar_prefetch=len(scalars), grid=(starts[-1],), in_specs=in_specs + [ANY] * len(deps),
            out_specs=out_specs),
        out_shape=out_shape,
        compiler_params=_cp(("arbitrary",)),
    )(*scalars, *args, *deps)
    result, pos = [], 0
    for _, _, outs_s, _ in sets:
        result.append(list(outs[pos:pos + len(outs_s)]))
        pos += len(outs_s)
    return result


def _adamw_all(jobs, name, dep):
    def update(in_refs, out_refs):
        gg = in_refs[3][...]
        out_refs[0][...] = gg
        out_refs[1][...], out_refs[2][...], out_refs[3][...] = _adamw_math(
            in_refs[0][...], gg, in_refs[1][...], in_refs[2][...])

    whole = lambda l: (0, 0)
    sets = [(1, [(a, a.shape, whole) for a in job], [(job[0].shape, F32, job[0].shape, whole)] * 4, update)
            for job in jobs]
    return [tuple(o) for o in _run_sets(name, [], sets, deps=[dep])]


def _pair_add_all(gs, recvs, cidx, name):
    def add(in_refs, out_refs):
        out_refs[0][...] = (in_refs[0][...].astype(F32) + in_refs[1][...].astype(F32)).astype(BF)

    sets = []
    for g, r in zip(gs, recvs):
        _, _, rh, cols = g.shape
        tr = min(rh, 256)
        sets.append((rh // tr,
                     [(g, (N_CHIPS, None, tr, cols), lambda l, c_ref: (0, c_ref[0], l, 0)),
                      (r, (N_CHIPS, tr, cols), lambda l, c_ref: (0, l, 0))],
                     [((N_CHIPS, rh, cols), BF, (N_CHIPS, tr, cols), lambda l, c_ref: (0, l, 0))], add))
    return [o[0] for o in _run_sets(name, [cidx], sets)]


def _chip_add_all(owns, parts, kidx, cidx, name):
    def add(in_refs, out_refs):
        acc = in_refs[0][...].astype(F32)
        for s in range(3):
            acc = acc + in_refs[1][s].astype(F32)
        out_refs[0][...] = acc

    sets = []
    for own, p in zip(owns, parts):
        _, rh, cols = p.shape
        tr = min(rh, 256)
        sets.append((rh // tr,
                     [(own, (None, tr, cols), lambda l, k_ref, c_ref: (k_ref[0], l, 0)),
                      (p, (3, tr, cols), lambda l, k_ref, c_ref: (0, l, 0))],
                     [((2, rh, cols), F32, (None, tr, cols), lambda l, k_ref, c_ref: (c_ref[0], l, 0))], add))
    return [o[0] for o in _run_sets(name, [kidx, cidx], sets)]


def _pair_add(g, recv, cidx, name):
    _, _, rh, cols = g.shape
    tr = min(rh, 256)

    def body(c_ref, g_ref, r_ref, o_ref):
        o_ref[...] = (g_ref[...].astype(F32) + r_ref[...].astype(F32)).astype(BF)

    return pl.pallas_call(
        body, name=name,
        grid_spec=pltpu.PrefetchScalarGridSpec(
            num_scalar_prefetch=1, grid=(rh // tr,),
            in_specs=[pl.BlockSpec((N_CHIPS, None, tr, cols), lambda i, c_ref: (0, c_ref[0], i, 0)),
                      pl.BlockSpec((N_CHIPS, tr, cols), lambda i, c_ref: (0, i, 0))],
            out_specs=pl.BlockSpec((N_CHIPS, tr, cols), lambda i, c_ref: (0, i, 0))),
        out_shape=jax.ShapeDtypeStruct((N_CHIPS, rh, cols), BF),
        compiler_params=_cp(("parallel",)),
    )(cidx, g, recv)


def _chip_send_start(name, sums):
    n = len(sums)
    lands = [lax.empty((3,) + s.shape[1:], BF) for s in sums]

    def run(refs, old, new):
        x, y, c = _place()
        for w in range(n):
            for j, (cx, cy) in enumerate(_other_chips(x, y)):
                _remote(refs[w].at[2 * cx + cy], refs[n + w].at[j], new, 3 * w + j, (cx, cy, c)).start()

    return _split_call(name, list(sums) + lands, run, new=3 * n)


def _chip_send_wait(name, bufs, sems, after):
    n = len(bufs) // 2

    def run(refs, old, new):
        x, y, c = _place()
        for w in range(n):
            for j, (cx, cy) in enumerate(_other_chips(x, y)):
                cp = _remote(refs[w].at[2 * cx + cy], refs[n + w].at[j], old, 3 * w + j, (cx, cy, c))
                cp.wait_send()
                cp.wait_recv()

    thru = _split_call(name, bufs, run, old=sems, after=after)[0]
    return thru[:n], thru[n:]


def _chip_add(own, parts, kidx, cidx, name):
    _, rh, cols = parts.shape
    tr = min(rh, 512)

    def body(k_ref, c_ref, own_ref, p_ref, o_ref):
        acc = own_ref[...].astype(F32)
        for s in range(3):
            acc = acc + p_ref[s].astype(F32)
        o_ref[...] = acc

    return pl.pallas_call(
        body, name=name,
        grid_spec=pltpu.PrefetchScalarGridSpec(
            num_scalar_prefetch=2, grid=(rh // tr,),
            in_specs=[pl.BlockSpec((None, tr, cols), lambda i, k_ref, c_ref: (k_ref[0], i, 0)),
                      pl.BlockSpec((3, tr, cols), lambda i, k_ref, c_ref: (0, i, 0))],
            out_specs=pl.BlockSpec((None, tr, cols), lambda i, k_ref, c_ref: (c_ref[0], i, 0))),
        out_shape=jax.ShapeDtypeStruct((2, rh, cols), F32),
        compiler_params=_cp(("parallel",)),
    )(kidx, cidx, own, parts)


def _pair_swap_start(name, bufs):
    def run(refs, old, new):
        x, y, c = _place()
        for w, ref in enumerate(refs):
            _remote(ref.at[c], ref.at[c], new, w, (x, y, 1 - c)).start()

    return _split_call(name, bufs, run, new=len(bufs))


def _pair_swap_wait(name, bufs, sems, after):
    def run(refs, old, new):
        x, y, c = _place()
        for w, ref in enumerate(refs):
            _remote(ref.at[c], ref.at[c], old, w, (x, y, 1 - c)).wait_send()
            _remote(ref.at[1 - c], ref.at[1 - c], old, w, (x, y, 1 - c)).wait_recv()

    return _split_call(name, bufs, run, old=sems, after=after)[0]


def _peers(x, y, c):
    out = []
    for code in range(1, 8):
        fx, fy, fc = (code >> 2) & 1, (code >> 1) & 1, code & 1
        px = 1 - x if fx else x
        py = 1 - y if fy else y
        pc = 1 - c if fc else c
        out.append((code, (px, py, pc)))
    return out


def _mod_exchange(c_row, ada_w, ada_b4, deps):
    ncol = ada_w.shape[1]

    def body(c_ref, w_ref, b_ref, *rest):
        call_ref, mod_ref, part_ref, send_sems, recv_sems = rest[len(deps):]
        x, y, c = _place()
        k = 2 * x + y
        me = 4 * x + 2 * y + c
        call_ref[pl.ds(me, 1), :] = c_ref[...]
        sends = []
        for code, peer in _peers(x, y, c):
            cp = pltpu.make_async_remote_copy(
                src_ref=c_ref, dst_ref=call_ref.at[pl.ds(me, 1), :],
                send_sem=send_sems.at[code], recv_sem=recv_sems.at[code],
                device_id=peer, device_id_type=MESH)
            cp.start()
            sends.append(cp)
        for code, (px, py, pc) in _peers(x, y, c):
            land = call_ref.at[pl.ds(4 * px + 2 * py + pc, 1), :]
            pltpu.make_async_remote_copy(
                src_ref=land, dst_ref=land, send_sem=send_sems.at[code], recv_sem=recv_sems.at[code],
                device_id=(px, py, pc), device_id_type=MESH).wait_recv()
        call = call_ref[...]
        act = call * _sigmoid(call)
        part = jnp.dot(act, w_ref[...], preferred_element_type=F32,
                       precision=lax.Precision.HIGHEST) + b_ref[pl.ds(k, 1), :]
        part_ref[...] = part
        mod_ref[pl.ds(k, 1), :] = part_ref[pl.ds(me, 1), :]
        chips = [(8 + j, peer) for j, (code, peer) in enumerate(_peers(x, y, c)) if code in (2, 4, 6)]
        for slot, (px, py, pc) in chips:
            cp = pltpu.make_async_remote_copy(
                src_ref=part_ref.at[pl.ds(4 * px + 2 * py + pc, 1), :], dst_ref=mod_ref.at[pl.ds(k, 1), :],
                send_sem=send_sems.at[slot], recv_sem=recv_sems.at[slot],
                device_id=(px, py, pc), device_id_type=MESH)
            cp.start()
            sends.append(cp)
        for slot, (px, py, pc) in chips:
            land = mod_ref.at[pl.ds(2 * px + py, 1), :]
            pltpu.make_async_remote_copy(
                src_ref=land, dst_ref=land, send_sem=send_sems.at[slot], recv_sem=recv_sems.at[slot],
                device_id=(px, py, pc), device_id_type=MESH).wait_recv()
        for cp in sends:
            cp.wait_send()

    vm = pl.BlockSpec(memory_space=pltpu.VMEM)
    return pl.pallas_call(
        body, name="mod_exchange",
        in_specs=[vm, vm, vm] + [ANY] * len(deps), out_specs=[vm, vm],
        out_shape=[jax.ShapeDtypeStruct((8, D_MODEL), F32), jax.ShapeDtypeStruct((N_CHIPS, ncol), F32)],
        scratch_shapes=[pltpu.VMEM((8, ncol), F32), pltpu.SemaphoreType.DMA((16,)),
                        pltpu.SemaphoreType.DMA((16,))],
        compiler_params=_cp(),
    )(c_row, ada_w, ada_b4, *deps)


def _small_exchange(stats, rows, c_all, wsm, msm, vsm):
    ncol = 6 * D_MODEL // N_CHIPS
    ns = len(stats)

    def body(*refs):
        call_ref, w_ref, m_ref, v_ref, gw_ref, loss_ref = refs[ns:ns + 6]
        outs = refs[ns + 6:ns + 30]
        p_ref, g_ref, all_ref, dm_ref, send_sems, recv_sems = refs[ns + 30:]
        x, y, c = _place()
        k = 2 * x + y
        me = 4 * x + 2 * y + c
        for r, (tab, row) in enumerate(rows):
            p_ref[r] = refs[tab][row:row + 1, :]
        all_ref[:, pl.ds(me, 1), :] = p_ref[...]
        sends = []
        for code, peer in _peers(x, y, c):
            cp = pltpu.make_async_remote_copy(
                src_ref=p_ref, dst_ref=all_ref.at[:, pl.ds(me, 1), :],
                send_sem=send_sems.at[code], recv_sem=recv_sems.at[code],
                device_id=peer, device_id_type=MESH)
            cp.start()
            sends.append(cp)
        for code, (px, py, pc) in _peers(x, y, c):
            land = all_ref.at[:, pl.ds(4 * px + 2 * py + pc, 1), :]
            pltpu.make_async_remote_copy(
                src_ref=land, dst_ref=land, send_sem=send_sems.at[code], recv_sem=recv_sems.at[code],
                device_id=(px, py, pc), device_id_type=MESH).wait_recv()
        for cp in sends:
            cp.wait_send()
        tot = [_colsum(all_ref[r]) for r in range(N_PAY)]
        loss_ref[...] = jnp.sum(tot[P_LOSS], axis=1, keepdims=True)
        g_ref[...] = jnp.zeros_like(g_ref)
        for r in range(P_LOSS):
            g_ref[r:r + 1, :] = tot[r]
        g = g_ref[...]
        for kind, tab in enumerate((g,) + _adamw_math(w_ref[...], g, m_ref[...], v_ref[...])):
            for r in range(6):
                outs[6 * kind][:, r * D_MODEL:(r + 1) * D_MODEL] = tab[r:r + 1, :]
            for i in range(5):
                outs[6 * kind + 1 + i][...] = tab[6 + i:7 + i, :]
        half = D_MODEL // 2
        for kk in range(N_CHIPS):
            @pl.when(k == kk)
            def _():
                r0 = 3 * (kk // 2)
                if kk % 2 == 0:
                    dm_ref[:, :D_MODEL] = all_ref[r0]
                    dm_ref[:, D_MODEL:] = all_ref[r0 + 1][:, :half]
                else:
                    dm_ref[:, :half] = all_ref[r0 + 1][:, half:]
                    dm_ref[:, half:] = all_ref[r0 + 2]
        call = call_ref[...]
        act = call * _sigmoid(call)
        gw_ref[...] = lax.dot_general(act, dm_ref[...], (((0,), (0,)), ((), ())),
                                      preferred_element_type=F32, precision=lax.Precision.HIGHEST)

    vm = pl.BlockSpec(memory_space=pltpu.VMEM)
    vectors = [jax.ShapeDtypeStruct((1, 6 * D_MODEL), F32)] + [jax.ShapeDtypeStruct((1, D_MODEL), F32)] * 5
    outs = pl.pallas_call(
        body, name="small_exchange",
        in_specs=[vm] * (ns + 4), out_specs=[vm] * 26,
        out_shape=[jax.ShapeDtypeStruct((D_MODEL, ncol), F32), jax.ShapeDtypeStruct((1, 1), F32)] + vectors * 4,
        scratch_shapes=[pltpu.VMEM((N_PAY, 1, D_MODEL), F32), pltpu.VMEM((16, D_MODEL), F32),
                        pltpu.VMEM((N_PAY, 8, D_MODEL), F32), pltpu.VMEM((8, ncol), F32),
                        pltpu.SemaphoreType.DMA((8,)), pltpu.SemaphoreType.DMA((8,))],
        compiler_params=_cp(),
    )(*stats, c_all, wsm, msm, vsm)
    loss, *small = _pass_on(outs[1:])
    return outs[0], loss, [small[6 * kind:6 + 6 * kind] for kind in range(4)]


def _pass_on(arrays):
    n = len(arrays)

    def body(*refs):
        for i in range(n):
            refs[n + i][...] = refs[i][...]

    return pl.pallas_call(
        body, name="small_outputs", out_shape=[jax.ShapeDtypeStruct(a.shape, a.dtype) for a in arrays],
        compiler_params=_cp(),
    )(*arrays)


def _rope_tables(pos_col, inv_freq, sign, dep=None):
    def body(p_ref, f_ref, s_ref, cos_ref, sin_ref):
        ang = p_ref[...].astype(F32) * f_ref[...]
        cos_ref[...] = jnp.cos(ang)
        sin_ref[...] = jnp.sin(ang) * s_ref[...]

    tr = 512
    shp = jax.ShapeDtypeStruct((SEQ, 128), F32)
    body, in_specs, args = _add_dep(
        body, [pl.BlockSpec((tr, 1), lambda i: (i, 0)), pl.BlockSpec((1, 128), lambda i: (0, 0)),
               pl.BlockSpec((1, 128), lambda i: (0, 0))], [pos_col, inv_freq, sign], dep)
    return pl.pallas_call(
        body, name="rope_tables", grid=(SEQ // tr,),
        in_specs=in_specs,
        out_specs=[pl.BlockSpec((tr, 128), lambda i: (i, 0))] * 2, out_shape=[shp, shp],
        compiler_params=_cp(("parallel",)),
    )(*args)


def _resident(shape):
    nd = len(shape)
    return pl.BlockSpec(shape, lambda *_: (0,) * nd, pipeline_mode=pl.Buffered(1))


def _ln_proj(x, vecs, w_in4):
    tm = min(512, SEQ)
    wc = w_in4.shape[2]

    def body(x_ref, vec_ref, w_ref, h_ref, proj_ref):
        xx = x_ref[...]
        g = _row(vec_ref, V_G1) * (1.0 + _row(vec_ref, V_SC1))
        h = (xx * _rms(xx) * g + _row(vec_ref, V_SH1)).astype(BF)
        h_ref[...] = h
        for j in range(N_CHIPS):
            proj_ref[:, j * wc:(j + 1) * wc] = _dot(h, w_ref[j]).astype(BF)

    return pl.pallas_call(
        body, name="ln_proj", grid=(SEQ // tm,),
        in_specs=[pl.BlockSpec((tm, D_MODEL), lambda i: (i, 0)), _resident((16, D_MODEL)),
                  _resident(w_in4.shape)],
        out_specs=[pl.BlockSpec((tm, D_MODEL), lambda i: (i, 0)), pl.BlockSpec((tm, D_IN), lambda i: (i, 0))],
        out_shape=[jax.ShapeDtypeStruct((SEQ, D_MODEL), BF), jax.ShapeDtypeStruct((SEQ, D_IN), BF)],
        compiler_params=_cp(("parallel",)),
    )(x, vecs, w_in4)


def _lane_first(shape):
    lane = lax.broadcasted_iota(jnp.int32, shape, 1)
    return (lane & 32) == 0


def _rot(v, cos, sin_s):
    partner = jnp.where(_lane_first(v.shape), pltpu.roll(v, 96, 1), pltpu.roll(v, 32, 1))
    return v * cos + partner * sin_s


def _rot_t(dv, cos, sin_s):
    t = dv * sin_s
    partner = jnp.where(_lane_first(dv.shape), pltpu.roll(t, 96, 1), pltpu.roll(t, 32, 1))
    return dv * cos + partner


def _ret_mask(lg):
    t = RET_BLOCK
    ii = lax.broadcasted_iota(jnp.int32, (t, t), 0)
    jj = lax.broadcasted_iota(jnp.int32, (t, t), 1)
    dist = jnp.abs(ii - jj).astype(F32)
    future = (jj >> RET_CHUNK_SHIFT) > (ii >> RET_CHUNK_SHIFT)
    return jnp.where(future, 0.0, jnp.exp(lg * dist))


def _ret_masks(lg, mask_ref, head):
    t = RET_BLOCK
    mask = mask_ref[head]
    ti = lax.broadcasted_iota(jnp.int32, (t, 1), 0).astype(F32)
    from_start = jnp.exp(lg * (ti + 1.0))
    to_end = jnp.exp(lg * (t - 1.0 - ti))
    whole = jnp.exp(jnp.full((1, 128), lg * t, F32))
    return mask, from_start, to_end, whole


def _head_lanes(shape, hh):
    lane = lax.broadcasted_iota(jnp.int32, shape, 1)
    return (lane >> 6) == hh


def _ret_specs():
    t = RET_BLOCK
    return dict(
        q=lambda f: pl.BlockSpec((t, 512), lambda n: (f(n), C_QR // 512)),
        k=lambda f: pl.BlockSpec((t, 512), lambda n: (f(n), C_KR // 512)),
        v=lambda f: pl.BlockSpec((t, D_MODEL), lambda n: (f(n), C_VR // D_MODEL)),
        g=lambda f: pl.BlockSpec((t, D_MODEL), lambda n: (f(n), C_GR // D_MODEL)),
        tab=lambda f: pl.BlockSpec((t, 128), lambda n: (f(n), 0)),
        wide=lambda f: pl.BlockSpec((t, D_MODEL), lambda n: (f(n), 0)),
        state=lambda f: pl.BlockSpec((N_PAIRS, None, 2, 128, 128), lambda n: (0, f(n), 0, 0, 0)),
    )


def _ret_fwd(proj, cos, sin_s, gn_g, log_gamma, dep=None):
    t = RET_BLOCK
    nb = SEQ // t

    def body(lg_ref, q_ref, k_ref, v_ref, g_ref, cos_ref, sin_ref, gn_ref, o_ref, retg_ref, st_ref, state, masks):
        @pl.when(pl.program_id(0) == 0)
        def _():
            state[...] = jnp.zeros_like(state)
            for head in range(2 * N_PAIRS):
                masks[head] = _ret_mask(lg_ref[head])

        cos, sn = cos_ref[...], sin_ref[...]
        for p in range(N_PAIRS):
            q = _rot(q_ref[:, 128 * p:128 * (p + 1)].astype(F32), cos, sn)
            k = _rot(k_ref[:, 128 * p:128 * (p + 1)].astype(F32), cos, sn) * QK_SCALE
            for hh in range(2):
                cols = slice(256 * p + 128 * hh, 256 * p + 128 * (hh + 1))
                lg = lg_ref[2 * p + hh]
                mask, from_start, to_end, whole = _ret_masks(lg, masks, 2 * p + hh)
                lanes = _head_lanes(q.shape, hh)
                qm = jnp.where(lanes, q, 0.0)
                km = jnp.where(lanes, k, 0.0)
                vh = v_ref[:, cols]
                sc = _dot_nt(qm.astype(BF), km.astype(BF)) * mask
                st = state[p, hh]
                st_ref[p, hh] = st
                o = _dot(sc.astype(BF), vh) + _dot((qm * from_start).astype(BF), st.astype(BF))
                state[p, hh] = whole * st + _dot_tn((km * to_end).astype(BF), vh)
                d = o - _rowmean(o)
                nh = d * lax.rsqrt(_rowmean(d * d) + EPS)
                gr = g_ref[:, cols].astype(F32)
                o_ref[:, cols] = o
                retg_ref[:, cols] = (gr * _sigmoid(gr) * nh * gn_ref[:, cols]).astype(BF)

    sp = _ret_specs()
    ident = lambda n: n
    body, in_specs, args = _add_dep(
        body, [pl.BlockSpec(memory_space=pltpu.SMEM), sp["q"](ident), sp["k"](ident), sp["v"](ident),
               sp["g"](ident), sp["tab"](ident), sp["tab"](ident), _resident((1, D_MODEL))],
        [log_gamma, proj, proj, proj, proj, cos, sin_s, gn_g], dep)
    return pl.pallas_call(
        body, name="ret_fwd", grid=(nb,),
        in_specs=in_specs,
        out_specs=[sp["wide"](ident), sp["wide"](ident), sp["state"](ident)],
        out_shape=[jax.ShapeDtypeStruct((SEQ, D_MODEL), F32), jax.ShapeDtypeStruct((SEQ, D_MODEL), BF),
                   jax.ShapeDtypeStruct((N_PAIRS, nb, 2, 128, 128), F32)],
        scratch_shapes=[pltpu.VMEM((N_PAIRS, 2, 128, 128), F32),
                        pltpu.VMEM((2 * N_PAIRS, RET_BLOCK, RET_BLOCK), F32)],
        compiler_params=_cp(("arbitrary",)),
    )(*args)


def _ret_bwd(proj, cos, sin_s, dret, states, log_gamma, dep=None):
    t = RET_BLOCK
    nb = SEQ // t

    def body(lg_ref, q_ref, k_ref, v_ref, cos_ref, sin_ref, do_ref, st_ref, dqkv_ref, dstate, masks):
        @pl.when(pl.program_id(0) == 0)
        def _():
            dstate[...] = jnp.zeros_like(dstate)
            for head in range(2 * N_PAIRS):
                masks[head] = _ret_mask(lg_ref[head])

        cos, sn = cos_ref[...], sin_ref[...]
        for p in range(N_PAIRS):
            q = _rot(q_ref[:, 128 * p:128 * (p + 1)].astype(F32), cos, sn)
            k = _rot(k_ref[:, 128 * p:128 * (p + 1)].astype(F32), cos, sn) * QK_SCALE
            dq_rot = jnp.zeros(q.shape, F32)
            dk_rot = jnp.zeros(q.shape, F32)
            for hh in range(2):
                cols = slice(256 * p + 128 * hh, 256 * p + 128 * (hh + 1))
                lg = lg_ref[2 * p + hh]
                mask, from_start, to_end, whole = _ret_masks(lg, masks, 2 * p + hh)
                lanes = _head_lanes(q.shape, hh)
                qm = jnp.where(lanes, q, 0.0)
                km = jnp.where(lanes, k, 0.0)
                qb, kb = qm.astype(BF), km.astype(BF)
                vh = v_ref[:, cols]
                do = do_ref[:, cols]
                sc = (_dot_nt(qb, kb) * mask).astype(BF)
                st = st_ref[p, hh].astype(BF)
                dst = dstate[p, hh]
                dstb = dst.astype(BF)
                k_end = (km * to_end).astype(BF)
                q_start = (qm * from_start).astype(BF)
                dqkv_ref[:, C_VR + 256 * p + 128 * hh:C_VR + 256 * p + 128 * (hh + 1)] = (
                    _dot_tn(sc, do) + _dot(k_end, dstb)).astype(BF)
                dsc = (_dot_nt(do, vh) * mask).astype(BF)
                dq_h = _dot(dsc, kb) + _dot_nt(do, st) * from_start
                dq_rot = dq_rot + jnp.where(lanes, dq_h, 0.0)
                dk_rot = dk_rot + _dot_tn(dsc, qb) + _dot_nt(vh, dstb) * to_end
                dstate[p, hh] = whole * dst + _dot_tn(q_start, do)
            dqkv_ref[:, C_QR + 128 * p:C_QR + 128 * (p + 1)] = _rot_t(dq_rot, cos, sn).astype(BF)
            dqkv_ref[:, C_KR + 128 * p:C_KR + 128 * (p + 1)] = _rot_t(dk_rot * QK_SCALE, cos, sn).astype(BF)

    sp = _ret_specs()
    rev = lambda n: nb - 1 - n
    body, in_specs, args = _add_dep(
        body, [pl.BlockSpec(memory_space=pltpu.SMEM), sp["q"](rev), sp["k"](rev), sp["v"](rev),
               sp["tab"](rev), sp["tab"](rev), sp["wide"](rev), sp["state"](rev)],
        [log_gamma, proj, proj, proj, cos, sin_s, dret, states], dep)
    return pl.pallas_call(
        body, name="ret_bwd", grid=(nb,),
        in_specs=in_specs,
        out_specs=pl.BlockSpec((t, C_GR), lambda n: (rev(n), 0)),
        out_shape=jax.ShapeDtypeStruct((SEQ, C_GR), BF),
        scratch_shapes=[pltpu.VMEM((N_PAIRS, 2, 128, 128), F32),
                        pltpu.VMEM((2 * N_PAIRS, RET_BLOCK, RET_BLOCK), F32)],
        compiler_params=_cp(("arbitrary",)),
    )(*args)


def _stack_heads(v):
    return jnp.concatenate([jnp.where(_head_lanes(v.shape, hh), v, jnp.zeros_like(v)) for hh in range(2)], axis=0)


def _unstack_heads(v):
    t = v.shape[0] // 2
    return jnp.where(_head_lanes((t, v.shape[1]), 0), v[:t], v[t:])


def _sb_masks(t, heads):
    rr = lax.broadcasted_iota(jnp.int32, (t, t), 0)
    cc = lax.broadcasted_iota(jnp.int32, (t, t), 1)
    r2 = lax.broadcasted_iota(jnp.int32, (heads * t, t), 0) & (t - 1)
    c2 = lax.broadcasted_iota(jnp.int32, (heads * t, t), 1)
    return rr, cc, c2 < r2


def _split_dot2(v, tri):
    return _dot(v.astype(BF), tri)


def _log2_sigmoids(z2):
    minus_abs = lax.bitcast_convert_type(
        lax.bitcast_convert_type(z2, jnp.uint32) | jnp.uint32(0x80000000), F32)
    ls = jnp.minimum(z2, 0.0) - jnp.log2(1.0 + jnp.exp2(minus_abs))
    return ls, ls - z2


def _sb_fwd(proj):
    t, g = SB_BLOCK, SB_GROUP
    nq = SEQ // t
    rows = 2 * g * t

    def body(q_ref, k_ref, v_ref, o_ref, tot_ref, kt_ref):
        i = pl.program_id(1)

        @pl.when(i == 0)
        def _():
            for p in range(g):
                for jj in range(nq):
                    kt_ref[p, jj] = k_ref[jj * t:(jj + 1) * t, 128 * p:128 * (p + 1)].T

        q2 = [_stack_heads((q_ref[:, 128 * p:128 * (p + 1)].astype(F32) * QK_SCALE).astype(BF)) for p in range(g)]
        rr, cc, valid = _sb_masks(t, 2 * g)
        later = (rr > cc).astype(BF)

        def tile(j, carry, diagonal):
            acc, run = carry
            z = jnp.concatenate([_dot(q2[p], kt_ref[p, j]) for p in range(g)], axis=0) * LOG2E
            ls, lm = _log2_sigmoids(z)
            if diagonal:
                lm = jnp.where(valid, lm, 0.0)
            after = _split_dot2(lm, later)
            a = jnp.exp2(ls + after + run)
            if diagonal:
                a = jnp.where(valid, a, 0.0)
            ab = a.astype(BF)
            keys = pl.ds(pl.multiple_of(j * t, t), t)
            av = jnp.concatenate([_dot(ab[2 * t * p:2 * t * (p + 1)], v_ref[keys, 128 * p:128 * (p + 1)])
                                  for p in range(g)], axis=0)
            return acc + av, run + after[:, 0:1] + lm[:, 0:1]

        carry = tile(i, (jnp.zeros((rows, 128), F32), jnp.zeros((rows, 1), F32)), True)
        acc, run = lax.fori_loop(0, i, lambda s, cr: tile(i - 1 - s, cr, False), carry)
        run = jnp.broadcast_to(run, (rows, 128))
        for p in range(g):
            o_ref[:, 128 * p:128 * (p + 1)] = _unstack_heads(acc[2 * t * p:2 * t * (p + 1)]).astype(BF)
            tot_ref[:, 128 * p:128 * (p + 1)] = _unstack_heads(run[2 * t * p:2 * t * (p + 1)])

    w = 128 * g
    return pl.pallas_call(
        body, name="sb_fwd", grid=(N_PAIRS // g, nq),
        in_specs=[pl.BlockSpec((t, w), lambda p, i: (i, C_QS // w + p)),
                  pl.BlockSpec((SEQ, w), lambda p, i: (0, C_KS // w + p)),
                  pl.BlockSpec((SEQ, w), lambda p, i: (0, C_VS // w + p))],
        out_specs=[pl.BlockSpec((t, w), lambda p, i: (i, p))] * 2,
        out_shape=[jax.ShapeDtypeStruct((SEQ, 512), BF), jax.ShapeDtypeStruct((SEQ, 512), F32)],
        scratch_shapes=[pltpu.VMEM((g, nq, 128, t), BF)],
        compiler_params=_cp(("parallel", "arbitrary")),
    )(proj, proj, proj)


def _sb_bwd(proj, dsb, tot, dep=None):
    t, g = SB_BLOCK, SB_GROUP
    nq = SEQ // t
    rows = 2 * g * t

    def body(q_ref, k_ref, v_ref, do_ref, tot_ref, dq_ref, dk_ref, dv_ref, kt_ref, vt_ref, dkt_acc, dvt_acc):
        i = pl.program_id(1)

        @pl.when(i == 0)
        def _():
            dkt_acc[...] = jnp.zeros_like(dkt_acc)
            dvt_acc[...] = jnp.zeros_like(dvt_acc)
            for p in range(g):
                for jj in range(nq):
                    kt_ref[p, jj] = k_ref[jj * t:(jj + 1) * t, 128 * p:128 * (p + 1)].T
                    vt_ref[p, jj] = v_ref[jj * t:(jj + 1) * t, 128 * p:128 * (p + 1)].T

        q2 = [_stack_heads((q_ref[:, 128 * p:128 * (p + 1)].astype(F32) * QK_SCALE).astype(BF)) for p in range(g)]
        do2 = [_stack_heads(do_ref[:, 128 * p:128 * (p + 1)]) for p in range(g)]
        q2t = [v.T for v in q2]
        do2t = [v.T for v in do2]
        tots = tot_ref[...]
        total = jnp.concatenate([tots[:, 64 * h:64 * h + 1] for h in range(2 * g)], axis=0)
        rr, cc, valid = _sb_masks(t, 2 * g)
        upto = (rr <= cc).astype(BF)
        before = (rr < cc).astype(BF)

        def part(v, p):
            return v[2 * t * p:2 * t * (p + 1)]

        def tile(j, carry, diagonal):
            dq, run_l, run_g = carry
            z = jnp.concatenate([_dot(q2[p], kt_ref[p, j]) for p in range(g)], axis=0) * LOG2E
            ls, lm = _log2_sigmoids(z)
            if diagonal:
                lm = jnp.where(valid, lm, 0.0)
            incl = _split_dot2(lm, upto)
            a = jnp.exp2(ls + (total - (incl + run_l)))
            if diagonal:
                a = jnp.where(valid, a, 0.0)
            gg = a * jnp.concatenate([_dot(do2[p], vt_ref[p, j]) for p in range(g)], axis=0)
            excl = _split_dot2(gg, before)
            dz = gg * jnp.exp2(lm) - (excl + run_g) * jnp.exp2(ls)
            if diagonal:
                dz = jnp.where(valid, dz, 0.0)
            dzb = dz.astype(BF)
            ab = a.astype(BF)
            keys = pl.ds(pl.multiple_of(j * t, t), t)
            for p in range(g):
                dkt_acc[p, j] += _dot(q2t[p], part(dzb, p))
                dvt_acc[p, j] += _dot(do2t[p], part(ab, p))
            dq_t = jnp.concatenate([_dot(part(dzb, p), k_ref[keys, 128 * p:128 * (p + 1)]) for p in range(g)], axis=0)
            return (dq + dq_t, run_l + incl[:, t - 1:t], run_g + excl[:, t - 1:t] + gg[:, t - 1:t])

        zero = jnp.zeros((rows, 1), F32)
        carry = lax.fori_loop(0, i, lambda j, cr: tile(j, cr, False), (jnp.zeros((rows, 128), F32), zero, zero))
        dq = tile(i, carry, True)[0]
        for p in range(g):
            dq_ref[:, 128 * p:128 * (p + 1)] = (_unstack_heads(part(dq, p)) * QK_SCALE).astype(BF)

        @pl.when(i == nq - 1)
        def _():
            for p in range(g):
                for jj in range(nq):
                    dk_ref[jj * t:(jj + 1) * t, 128 * p:128 * (p + 1)] = dkt_acc[p, jj].T.astype(BF)
                    dv_ref[jj * t:(jj + 1) * t, 128 * p:128 * (p + 1)] = dvt_acc[p, jj].T.astype(BF)

    w = 128 * g
    tile_spec = pl.BlockSpec((t, w), lambda p, i: (i, p))
    col_spec = pl.BlockSpec((SEQ, w), lambda p, i: (0, p))
    shp = jax.ShapeDtypeStruct((SEQ, 512), BF)
    body, in_specs, args = _add_dep(
        body, [pl.BlockSpec((t, w), lambda p, i: (i, C_QS // w + p)),
               pl.BlockSpec((SEQ, w), lambda p, i: (0, C_KS // w + p)),
               pl.BlockSpec((SEQ, w), lambda p, i: (0, C_VS // w + p)),
               tile_spec, tile_spec],
        [proj, proj, proj, dsb, tot], dep)
    return pl.pallas_call(
        body, name="sb_bwd", grid=(N_PAIRS // g, nq),
        in_specs=in_specs,
        out_specs=[tile_spec, col_spec, col_spec],
        out_shape=[shp, shp, shp],
        scratch_shapes=[pltpu.VMEM((g, nq, 128, t), BF), pltpu.VMEM((g, nq, 128, t), BF),
                        pltpu.VMEM((g, nq, 128, t), F32), pltpu.VMEM((g, nq, 128, t), F32)],
        compiler_params=_cp(("parallel", "arbitrary")),
    )(*args)


def _mix_out(retg, sb, proj, x, vecs, w_ret, w_sb4, w_out):
    tm, half = min(512, SEQ), 512

    def body(r_ref, s_ref, ar0, ar1, as0, as1, x_ref, vec_ref, wr_ref, ws_ref, wo_ref,
             mix_ref, rb_ref, sbp_ref, y_ref, h1_ref, h2_ref):
        rb = _dot(r_ref[...], wr_ref[...])
        sbv = s_ref[...]
        sbp = jnp.concatenate([_dot(sbv, ws_ref[k]) for k in range(N_CHIPS)], axis=1)
        gate_r = _sigmoid(jnp.concatenate([ar0[...], ar1[...]], axis=1).astype(F32))
        gate_s = _sigmoid(jnp.concatenate([as0[...], as1[...]], axis=1).astype(F32))
        mixed = (gate_r * rb + gate_s * sbp).astype(BF)
        mix_ref[...] = mixed
        rb_ref[...] = rb.astype(BF)
        sbp_ref[...] = sbp.astype(BF)
        y = _dot(mixed, wo_ref[...])
        h1 = x_ref[...] + _row(vec_ref, V_GT1) * (y * _rms(y)) * _row(vec_ref, V_G2)
        g = _row(vec_ref, V_G3) * (1.0 + _row(vec_ref, V_SC2))
        y_ref[...] = y
        h1_ref[...] = h1
        h2_ref[...] = (h1 * _rms(h1) * g + _row(vec_ref, V_SH2)).astype(BF)

    row = pl.BlockSpec((tm, D_MODEL), lambda i: (i, 0))
    gate = lambda c0: pl.BlockSpec((tm, half), lambda i: (i, c0 // half))
    bf = jax.ShapeDtypeStruct((SEQ, D_MODEL), BF)
    f32 = jax.ShapeDtypeStruct((SEQ, D_MODEL), F32)
    return pl.pallas_call(
        body, name="mix_out", grid=(SEQ // tm,),
        in_specs=[row, pl.BlockSpec((tm, 512), lambda i: (i, 0)), gate(C_AR), gate(C_AR + half), gate(C_AS),
                  gate(C_AS + half), row, _resident((16, D_MODEL)), _resident((D_MODEL, D_MODEL)),
                  _resident(w_sb4.shape), _resident((D_MODEL, D_MODEL))],
        out_specs=[row] * 6, out_shape=[bf, bf, bf, f32, f32, bf],
        compiler_params=_cp(("parallel",)),
    )(retg, sb, proj, proj, proj, proj, x, vecs, w_ret, w_sb4, w_out)


def _ffn_fwd_loss(h2, h1, target, vecs, w_ff14, w_ff24):
    tm = 256

    def body(h2_ref, h1_ref, t_ref, vec_ref, w1_ref, w2_ref, u_ref, a_ref, dout_ref, df_ref, st_ref):
        @pl.when(pl.program_id(0) == 0)
        def _():
            st_ref[...] = jnp.zeros_like(st_ref)

        hb = h2_ref[...]
        f = jnp.zeros((tm, D_MODEL), F32)
        for k in range(N_CHIPS):
            cols = slice(k * D_MODEL, (k + 1) * D_MODEL)
            u = _dot(hb, w1_ref[k])
            r = jnp.maximum(u, 0.0)
            act = (r * r).astype(BF)
            u_ref[:, cols] = u.astype(BF)
            a_ref[:, cols] = act
            f = f + _dot(act, w2_ref[k])
        r4 = _rms(f)
        fn = f * r4
        gt2, g4 = _row(vec_ref, V_GT2), _row(vec_ref, V_G4)
        diff = h1_ref[...] + gt2 * fn * g4 - t_ref[...]
        dout = diff * (1.0 / D_MODEL)
        dfn = dout * gt2 * g4
        dout_ref[...] = dout
        df_ref[...] = (r4 * (dfn - fn * _rowmean(dfn * fn))).astype(BF)
        st_ref[0:1, :] += _colsum(dout * fn * g4)
        st_ref[1:2, :] += _colsum(dout * gt2 * fn)
        st_ref[2:3, :] += _colsum(diff * diff) * (0.5 / D_MODEL)

    row = pl.BlockSpec((tm, D_MODEL), lambda i: (i, 0))
    wide = pl.BlockSpec((tm, D_FF), lambda i: (i, 0))
    return pl.pallas_call(
        body, name="ffn_fwd_loss", grid=(SEQ // tm,),
        in_specs=[row, row, row, _resident((16, D_MODEL)), _resident(w_ff14.shape), _resident(w_ff24.shape)],
        out_specs=[wide, wide, row, row, pl.BlockSpec((8, D_MODEL), lambda i: (0, 0))],
        out_shape=[jax.ShapeDtypeStruct((SEQ, D_FF), BF), jax.ShapeDtypeStruct((SEQ, D_FF), BF),
                   jax.ShapeDtypeStruct((SEQ, D_MODEL), F32), jax.ShapeDtypeStruct((SEQ, D_MODEL), BF),
                   jax.ShapeDtypeStruct((8, D_MODEL), F32)],
        compiler_params=_cp(("arbitrary",)),
    )(h2, h1, target, vecs, w_ff14, w_ff24)


def _ffn_bwd(df, u, h1, y, dout, vecs, w_ff14, w_ff24):
    tm = 256

    def body(df_ref, u_ref, h1_ref, y_ref, dout_ref, vec_ref, w1_ref, w2_ref, du_ref, dh1_ref, dy_ref, st_ref):
        @pl.when(pl.program_id(0) == 0)
        def _():
            st_ref[...] = jnp.zeros_like(st_ref)

        dfb = df_ref[...]
        dh2 = jnp.zeros((tm, D_MODEL), F32)
        for k in range(N_CHIPS):
            cols = slice(k * D_MODEL, (k + 1) * D_MODEL)
            da = _dot_nt(dfb, w2_ref[k])
            du = (da * (2.0 * jnp.maximum(u_ref[:, cols].astype(F32), 0.0))).astype(BF)
            du_ref[:, cols] = du
            dh2 = dh2 + _dot_nt(du, w1_ref[k])
        h1 = h1_ref[...]
        r3 = _rms(h1)
        hn3 = h1 * r3
        g3, sc2 = _row(vec_ref, V_G3), _row(vec_ref, V_SC2)
        dhn3 = dh2 * g3 * (1.0 + sc2)
        dh1 = dout_ref[...] + r3 * (dhn3 - hn3 * _rowmean(dhn3 * hn3))
        y = y_ref[...]
        r2 = _rms(y)
        yn = y * r2
        gt1, g2 = _row(vec_ref, V_GT1), _row(vec_ref, V_G2)
        dyn = dh1 * gt1 * g2
        dh1_ref[...] = dh1
        dy_ref[...] = (r2 * (dyn - yn * _rowmean(dyn * yn))).astype(BF)
        st_ref[0:1, :] += _colsum(dh2)
        st_ref[1:2, :] += _colsum(dh2 * hn3 * g3)
        st_ref[2:3, :] += _colsum(dh2 * hn3 * (1.0 + sc2))
        st_ref[3:4, :] += _colsum(dh1 * yn * g2)
        st_ref[4:5, :] += _colsum(dh1 * gt1 * yn)

    row = pl.BlockSpec((tm, D_MODEL), lambda i: (i, 0))
    wide = pl.BlockSpec((tm, D_FF), lambda i: (i, 0))
    return pl.pallas_call(
        body, name="ffn_bwd", grid=(SEQ // tm,),
        in_specs=[row, wide, row, row, row, _resident((16, D_MODEL)), _resident(w_ff14.shape),
                  _resident(w_ff24.shape)],
        out_specs=[wide, row, row, pl.BlockSpec((8, D_MODEL), lambda i: (0, 0))],
        out_shape=[jax.ShapeDtypeStruct((SEQ, D_FF), BF), jax.ShapeDtypeStruct((SEQ, D_MODEL), F32),
                   jax.ShapeDtypeStruct((SEQ, D_MODEL), BF), jax.ShapeDtypeStruct((8, D_MODEL), F32)],
        compiler_params=_cp(("arbitrary",)),
    )(df, u, h1, y, dout, vecs, w_ff14, w_ff24)


def _mix_ret_bwd(dy, proj, rb, sbp, o_raw, gn_g, w_out, w_sb4, w_ret):
    tm, half = min(512, SEQ), 512

    def body(dy_ref, ar0, ar1, as0, as1, rb_ref, sbp_ref, g_ref, o_ref, gn_ref, wo_ref, ws_ref, wr_ref,
             drb_ref, dsbp_ref, da_ref, dsb_ref, dret_ref, dgr_ref, st_ref):
        @pl.when(pl.program_id(0) == 0)
        def _():
            st_ref[...] = jnp.zeros_like(st_ref)

        dm_all = _dot_nt(dy_ref[...], wo_ref[...])
        dsb = jnp.zeros((tm, 512), F32)
        drbs = []
        for hf, (ar_ref, as_ref) in enumerate(((ar0, as0), (ar1, as1))):
            cols = slice(half * hf, half * (hf + 1))
            dm = dm_all[:, cols]
            sr = _sigmoid(ar_ref[...].astype(F32))
            ss = _sigmoid(as_ref[...].astype(F32))
            dsbp = (dm * ss).astype(BF)
            drbs.append((dm * sr).astype(BF))
            dsbp_ref[:, cols] = dsbp
            da_ref[:, cols] = (dm * rb_ref[:, cols].astype(F32) * sr * (1.0 - sr)).astype(BF)
            da_ref[:, D_MODEL + half * hf:D_MODEL + half * (hf + 1)] = (
                dm * sbp_ref[:, cols].astype(F32) * ss * (1.0 - ss)).astype(BF)
            dsb = dsb + _dot_nt(dsbp[:, :256], ws_ref[2 * hf]) + _dot_nt(dsbp[:, 256:], ws_ref[2 * hf + 1])
        dsb_ref[...] = dsb.astype(BF)
        drb = jnp.concatenate(drbs, axis=1)
        drb_ref[...] = drb
        dretg = _dot_nt(drb, wr_ref[...])
        for gi in range(D_MODEL // 128):
            cols = slice(128 * gi, 128 * (gi + 1))
            o = o_ref[:, cols]
            d = o - _rowmean(o)
            rstd = lax.rsqrt(_rowmean(d * d) + EPS)
            nh = d * rstd
            gain = gn_ref[:, cols]
            gr = g_ref[:, cols].astype(F32)
            sg = _sigmoid(gr)
            dg = dretg[:, cols]
            dgn = dg * gr * sg
            dnh = dgn * gain
            dgr_ref[:, cols] = (dg * nh * gain * sg * (1.0 + gr * (1.0 - sg))).astype(BF)
            dret_ref[:, cols] = (rstd * (dnh - _rowmean(dnh) - nh * _rowmean(dnh * nh))).astype(BF)
            st_ref[0:1, cols] += _colsum(dgn * nh)

    row = pl.BlockSpec((tm, D_MODEL), lambda i: (i, 0))
    gate = lambda c0: pl.BlockSpec((tm, half), lambda i: (i, c0 // half))
    shp = jax.ShapeDtypeStruct((SEQ, D_MODEL), BF)
    return pl.pallas_call(
        body, name="mix_ret_bwd", grid=(SEQ // tm,),
        in_specs=[row, gate(C_AR), gate(C_AR + half), gate(C_AS), gate(C_AS + half), row, row,
                  pl.BlockSpec((tm, D_MODEL), lambda i: (i, C_GR // D_MODEL)), row, _resident((1, D_MODEL)),
                  _resident((D_MODEL, D_MODEL)), _resident(w_sb4.shape), _resident((D_MODEL, D_MODEL))],
        out_specs=[row, row, pl.BlockSpec((tm, 2 * D_MODEL), lambda i: (i, 0)), pl.BlockSpec((tm, 512), lambda i: (i, 0)),
                   row, row, pl.BlockSpec((8, D_MODEL), lambda i: (0, 0))],
        out_shape=[shp, shp, jax.ShapeDtypeStruct((SEQ, 2 * D_MODEL), BF), jax.ShapeDtypeStruct((SEQ, 512), BF),
                   shp, shp, jax.ShapeDtypeStruct((8, D_MODEL), F32)],
        compiler_params=_cp(("arbitrary",)),
    )(dy, proj, proj, proj, proj, rb, sbp, proj, o_raw, gn_g, w_out, w_sb4, w_ret)


def _dproj_segments(widths):
    wc = D_IN // N_CHIPS
    segs, start = [], 0
    for pi, width in enumerate(widths):
        lo = start
        while lo < start + width:
            j = lo // wc
            hi = min(start + width, (j + 1) * wc)
            segs.append((j, lo - j * wc, pi, lo - start, hi - lo))
            lo = hi
        start += width
    assert start == D_IN
    return segs


def _in_proj_bwd(pieces, x, dh1, vecs, w_in4, updates, dep=None):
    tm = 256
    steps = SEQ // tm
    n, nu = len(pieces), len(updates)
    segs = _dproj_segments([p.shape[1] for p in pieces])

    def body(*refs):
        x_ref, dh1_ref, vec_ref, w_ref = refs[n:n + 4]
        upd_in = refs[n + 4:n + 4 + 4 * nu]
        dx_ref, st_ref = refs[n + 4 + 4 * nu:n + 6 + 4 * nu]
        upd_out = refs[n + 6 + 4 * nu:]
        for u in range(nu):
            w_u, m_u, v_u, g_u = upd_in[4 * u:4 * u + 4]
            go_u, d_u, mo_u, vo_u = upd_out[4 * u:4 * u + 4]
            gg = g_u[...]
            go_u[...] = gg
            d_u[...], mo_u[...], vo_u[...] = _adamw_math(w_u[...], gg, m_u[...], v_u[...])

        @pl.when(pl.program_id(0) == 0)
        def _():
            st_ref[...] = jnp.zeros_like(st_ref)

        dh = jnp.zeros((tm, D_MODEL), F32)
        for j, so, pi, po, width in segs:
            dh = dh + _dot_nt(refs[pi][:, po:po + width], w_ref[j, :, so:so + width])
        xx = x_ref[...]
        r1 = _rms(xx)
        xn = xx * r1
        g1, sc1 = _row(vec_ref, V_G1), _row(vec_ref, V_SC1)
        dxn = dh * g1 * (1.0 + sc1)
        dx_ref[...] = dh1_ref[...] + r1 * (dxn - xn * _rowmean(dxn * xn))
        st_ref[0:1, :] += _colsum(dh)
        st_ref[1:2, :] += _colsum(dh * xn * g1)
        st_ref[2:3, :] += _colsum(dh * xn * (1.0 + sc1))

    row = pl.BlockSpec((tm, D_MODEL), lambda i: (i, 0))
    upd_specs, upd_shapes, upd_args = [], [], []
    for arrays in updates:
        rows, cols = arrays[0].shape
        upd_specs += [pl.BlockSpec((rows // steps, cols), lambda i: (i, 0))] * 4
        upd_shapes += [jax.ShapeDtypeStruct((rows, cols), F32)] * 4
        upd_args += list(arrays)
    body, in_specs, args = _add_dep(
        body, [pl.BlockSpec((tm, p.shape[1]), lambda i: (i, 0)) for p in pieces] + [
            row, row, _resident((16, D_MODEL)), _resident(w_in4.shape)] + upd_specs,
        list(pieces) + [x, dh1, vecs, w_in4] + upd_args, dep)
    outs = pl.pallas_call(
        body, name="in_proj_bwd", grid=(steps,),
        in_specs=in_specs,
        out_specs=[row, pl.BlockSpec((8, D_MODEL), lambda i: (0, 0))] + upd_specs,
        out_shape=[jax.ShapeDtypeStruct((SEQ, D_MODEL), F32), jax.ShapeDtypeStruct((8, D_MODEL), F32)] + upd_shapes,
        compiler_params=_cp(("arbitrary",)),
    )(*args)
    return outs[0], outs[1], [tuple(outs[2 + 4 * u:6 + 4 * u]) for u in range(nu)]


def _grad_w_in(h, pieces, dep=None):
    ta = 512
    n = len(pieces)
    segs = _dproj_segments([p.shape[1] for p in pieces])

    def body(*refs):
        h_ref, o_ref = refs[n], refs[n + 1]
        hh = h_ref[...]
        for j, so, pi, po, width in segs:
            o_ref[j, :, so:so + width] = _dot_tn(hh, refs[pi][:, po:po + width]).astype(BF)

    body, in_specs, args = _add_dep(
        body, [_resident(p.shape) for p in pieces] + [pl.BlockSpec((SEQ, ta), lambda i: (0, i))],
        list(pieces) + [h], dep)
    return pl.pallas_call(
        body, name="grad_w_in", grid=(D_MODEL // ta,),
        in_specs=in_specs,
        out_specs=pl.BlockSpec((N_CHIPS, ta, D_IN // N_CHIPS), lambda i: (0, i, 0)),
        out_shape=jax.ShapeDtypeStruct((N_CHIPS, D_MODEL, D_IN // N_CHIPS), BF),
        compiler_params=_cp(("parallel",)),
    )(*args)


def _weight_grads(jobs, name):
    def grad(in_refs, out_refs):
        out_refs[0][...] = _dot_tn(in_refs[0][...], in_refs[1][...]).astype(BF)

    sets = []
    for a, b, col_sharded in jobs:
        ka, nb_ = a.shape[1], b.shape[1]
        if col_sharded:
            ta, tb = ka, nb_ // N_CHIPS
            ins = [(a, (SEQ, ta), lambda l: (0, 0)), (b, (SEQ, tb), lambda l: (0, l))]
        else:
            ta, tb = ka // N_CHIPS, nb_
            ins = [(a, (SEQ, ta), lambda l: (0, l)), (b, (SEQ, tb), lambda l: (0, 0))]
        sets.append((N_CHIPS, ins, [((N_CHIPS, ta, tb), BF, (None, ta, tb), lambda l: (l, 0, 0))], grad))
    return [o[0] for o in _run_sets(name, [], sets)]


def _rope_constants():
    freq = np.float32(ROPE_BASE) ** (-np.arange(0, 64, 2, dtype=np.float32) / np.float32(64))
    inv = np.tile(freq.astype(np.float32), 4).reshape(1, 128)
    sign = np.tile(np.concatenate([-np.ones(32, np.float32), np.ones(32, np.float32)]), 2).reshape(1, 128)
    return jnp.asarray(inv), jnp.asarray(sign)


def _log_gamma():
    return jnp.asarray(np.log1p(-(2.0 ** (-5.0 - np.arange(8, dtype=np.float64)))).astype(np.float32))


def _halves(g):
    return g.reshape(N_CHIPS, 2, g.shape[1] // 2, g.shape[2])


def kernel(x, c, positions, ada_w, ada_b, pre_mix_g, post_mix_g, pre_ffn_g, post_ffn_g, w_in, ret_gn_g, w_ret_branch, w_sb_branch, w_out, w_ff1, w_ff2, loss_target, m_ada_w, m_ada_b, m_pre_mix_g, m_post_mix_g, m_pre_ffn_g, m_post_ffn_g, m_w_in, m_ret_gn_g, m_w_ret_branch, m_w_sb_branch, m_w_out, m_w_ff1, m_w_ff2, v_ada_w, v_ada_b, v_pre_mix_g, v_post_mix_g, v_pre_ffn_g, v_post_ffn_g, v_w_in, v_ret_gn_g, v_w_ret_branch, v_w_sb_branch, v_w_out, v_w_ff1, v_w_ff2):
    names = ["w_in", "w_ret", "w_sb", "w_out", "w_ff1", "w_ff2"]
    big = dict(zip(names, [w_in, w_ret_branch, w_sb_branch, w_out, w_ff1, w_ff2]))
    big_m = dict(zip(names, [m_w_in, m_w_ret_branch, m_w_sb_branch, m_w_out, m_w_ff1, m_w_ff2]))
    big_v = dict(zip(names, [v_w_in, v_w_ret_branch, v_w_sb_branch, v_w_out, v_w_ff1, v_w_ff2]))
    rest = names[1:]
    cidx = lax.axis_index("c").astype(jnp.int32).reshape(1)
    kidx = (2 * lax.axis_index("x") + lax.axis_index("y")).astype(jnp.int32).reshape(1)
    x0, target = x[0], loss_target[0]

    buf_in, sem_in, tok_in = _gather_start("gather_in_start", [_cast_bf16(w_in[0], kidx, c, "cast_w_in")])
    rest_bufs = [_cast_bf16(big[nm][0], kidx, tok_in, "cast_" + nm) for nm in rest]
    inv_freq, sign = _rope_constants()
    lg = _log_gamma()
    cos, sin_s = _rope_tables(positions.reshape(SEQ, 1), inv_freq, sign, dep=tok_in)

    def table(b6, g5):
        return jnp.concatenate([b6.reshape(6, D_MODEL)] + g5 + [jnp.zeros((5, D_MODEL), F32)], axis=0)

    wsm = table(ada_b, [pre_mix_g, post_mix_g, pre_ffn_g, post_ffn_g, ret_gn_g])
    msm = table(m_ada_b, [m_pre_mix_g, m_post_mix_g, m_pre_ffn_g, m_post_ffn_g, m_ret_gn_g])
    vsm = table(v_ada_b, [v_pre_mix_g, v_post_mix_g, v_pre_ffn_g, v_post_ffn_g, v_ret_gn_g])
    c_all, mod4 = _mod_exchange(c, ada_w[0], ada_b.reshape(N_CHIPS, -1), rest_bufs + [cos, wsm, msm, vsm])
    vecs = jnp.concatenate([mod4.reshape(6, D_MODEL), pre_mix_g, post_mix_g, pre_ffn_g, post_ffn_g,
                            jnp.zeros((6, D_MODEL), F32)], axis=0)
    buf_in, sem_in, tok_in = _gather_pass("gather_in_pass", buf_in, sem_in, vecs)
    buf_rest, sem_rest, tok_rest = _gather_start("gather_rest_start", rest_bufs, after=tok_in)
    (w_in4,) = _gather_finish("gather_in_finish", buf_in, sem_in, tok_rest)

    h, proj = _ln_proj(x0, vecs, w_in4)
    sb, tot = _sb_fwd(proj)
    buf_rest, sem_rest, tok_rest = _gather_pass("gather_rest_pass", buf_rest, sem_rest, sb)
    o_raw, retg, states = _ret_fwd(proj, cos, sin_s, ret_gn_g, lg, dep=tok_rest)
    w_ret4, w_sb4, w_out4, w_ff14, w_ff24 = _gather_finish("gather_rest_finish", buf_rest, sem_rest, retg)
    w_ret = w_ret4.reshape(D_MODEL, D_MODEL)
    w_out2 = w_out4.reshape(D_MODEL, D_MODEL)
    mixed, rb, sbp, y, h1, h2 = _mix_out(retg, sb, proj, x0, vecs, w_ret, w_sb4, w_out2)
    u, act, dout, df, st_a = _ffn_fwd_loss(h2, h1, target, vecs, w_ff14, w_ff24)

    du, dh1, dy, st_b = _ffn_bwd(df, u, h1, y, dout, vecs, w_ff14, w_ff24)
    grads = dict(zip(("w_ff2", "w_ff1"), _weight_grads([(act, df, False), (h2, du, True)], "grad_w_ff")))
    drb, dsbp, da, dsb, dret, dg_r, st_c = _mix_ret_bwd(dy, proj, rb, sbp, o_raw, ret_gn_g, w_out2, w_sb4, w_ret)
    grads.update(zip(("w_out", "w_ret", "w_sb"), _weight_grads(
        [(mixed, dy, False), (retg, drb, False), (sb, dsbp, True)], "grad_w_branches")))

    bufs, sems, tok = _pair_send_start("rs_rest_pair_send", [_halves(grads[nm]) for nm in rest])
    dqkv_r = _ret_bwd(proj, cos, sin_s, dret, states, lg, dep=tok)
    mine, theirs = _pair_send_wait("rs_rest_pair_recv", bufs, sems, dqkv_r)
    pair_sums = _pair_add_all(mine, theirs, cidx, "pair_add_rest")
    bufs, sems, tok = _chip_send_start("rs_rest_chip_send", pair_sums)
    dq_s, dk_s, dv_s = _sb_bwd(proj, dsb, tot, dep=tok)
    dproj = [dqkv_r, dg_r, dq_s, dk_s, dv_s, da]
    g_in = _grad_w_in(h, dproj)

    bufs_in, sems_in, tok_in = _pair_send_start("rs_in_pair_send", [_halves(g_in)])
    own, parts = _chip_send_wait("rs_rest_chip_recv", bufs, sems, tok_in)
    sums = _chip_add_all(own, parts, kidx, cidx, "chip_add_rest")
    bufs, sems, tok = _pair_swap_start("rs_rest_pair_swap", sums)
    mine, theirs = _pair_send_wait("rs_in_pair_recv", bufs_in, sems_in, tok)
    bufs_in, sems_in, tok_in = _chip_send_start(
        "rs_in_chip_send", [_pair_add(mine[0], theirs[0], cidx, "pair_add_w_in")])
    full_rest = dict(zip(rest, _pair_swap_wait("rs_rest_pair_swapped", bufs, sems, tok_in)))
    small_w = ("w_out", "w_sb", "w_ret")
    out = dict(zip(small_w, _adamw_all(
        [(big[nm][0], big_m[nm][0], big_v[nm][0], full_rest[nm].reshape(big[nm].shape[1:])) for nm in small_w],
        "adamw_small", tok_in)))
    bufs, sems, tok = bufs_in, sems_in, out["w_ret"][1]
    riding = ("w_ff2", "w_ff1")
    dx, st_d, updated = _in_proj_bwd(
        dproj, x0, dh1, vecs, w_in4,
        [(big[nm][0], big_m[nm][0], big_v[nm][0], full_rest[nm].reshape(big[nm].shape[1:])) for nm in riding],
        dep=tok)
    out.update(zip(riding, updated))

    a_, b_, c_, d_ = range(4)
    payload_rows = [(d_, 0), (d_, 1), (b_, 3), (b_, 0), (b_, 1), (a_, 0),
                    (d_, 2), (b_, 4), (b_, 2), (a_, 1), (c_, 0), (a_, 2)]
    g_ada, loss, small = _small_exchange([st_a, st_b, st_c, st_d], payload_rows, c_all, wsm, msm, vsm)
    own, parts = _chip_send_wait("rs_in_chip_recv", bufs, sems, g_ada)
    bufs, sems, tok = _pair_swap_start(
        "rs_in_pair_swap", [_chip_add(own[0], parts[0], kidx, cidx, "chip_add_w_in")])
    ada_out = _adamw(ada_w[0], m_ada_w[0], v_ada_w[0], g_ada, "adamw_ada_w", dep=tok)
    (full_in,) = _pair_swap_wait("rs_in_pair_swapped", bufs, sems, ada_out[1])
    out["w_in"] = _adamw(w_in[0], m_w_in[0], v_w_in[0], full_in.reshape(w_in.shape[1:]), "adamw_w_in")

    def ordered(which):
        sm = small[which]
        bg = [out[nm][which][None] for nm in names]
        return [ada_out[which][None], sm[0], sm[1], sm[2], sm[3], sm[4], bg[0], sm[5]] + bg[1:]

    return (loss.reshape(()), dx[None], *ordered(0), *ordered(1), *ordered(2), *ordered(3))
```

```python
import functools

import numpy as np
import jax
import jax.numpy as jnp
from jax import lax
from jax.experimental import pallas as pl
from jax.experimental.pallas import tpu as pltpu

SEQ = 2048
D_MODEL = 1024
D_IN = 6656
D_FF = 4096
N_CHIPS = 4
EPS = 1e-6
ROPE_BASE = 10000.0
RET_BLOCK = 256
RET_CHUNK_SHIFT = 6
SB_BLOCK = 256
QK_SCALE = 0.125
LOG2E = 1.4426950408889634
N_PAIRS = 4
SB_GROUP = 4

ADAM_LR = 0.001
ADAM_B1 = 0.9
ADAM_B2 = 0.999
ADAM_EPS = 1e-08
ADAM_WD = 0.01
ADAM_STEP = 10

BF = jnp.bfloat16
F32 = jnp.float32
MESH = pl.DeviceIdType.MESH
VMEM_LIMIT = 56 * 1024 * 1024
ANY = pl.BlockSpec(memory_space=pl.ANY)

C_QR, C_KR, C_VR, C_GR, C_QS, C_KS, C_VS, C_AR, C_AS = 0, 512, 1024, 2048, 3072, 3584, 4096, 4608, 5632

V_SH1, V_SC1, V_GT1, V_SH2, V_SC2, V_GT2, V_G1, V_G2, V_G3, V_G4 = range(10)
P_DSH1, P_DSC1, P_DGT1, P_DSH2, P_DSC2, P_DGT2, P_DG1, P_DG2, P_DG3, P_DG4, P_DGN, P_LOSS = range(12)
N_PAY = 12


def _cp(sem=None, **kw):
    if sem is not None:
        kw["dimension_semantics"] = sem
    return pltpu.CompilerParams(vmem_limit_bytes=VMEM_LIMIT, **kw)


def _dot(a, b):
    return jnp.dot(a, b, preferred_element_type=F32)


def _dot_nt(a, b):
    return lax.dot_general(a, b, (((1,), (1,)), ((), ())), preferred_element_type=F32)


def _dot_tn(a, b):
    return lax.dot_general(a, b, (((0,), (0,)), ((), ())), preferred_element_type=F32)


def _row(ref, i):
    return ref[i:i + 1, :]


def _rms(v):
    return lax.rsqrt(jnp.mean(v * v, axis=1, keepdims=True) + EPS)


def _colsum(v):
    return jnp.sum(v, axis=0, keepdims=True)


def _rowmean(v):
    return jnp.mean(v, axis=1, keepdims=True)


def _sigmoid(v):
    return 1.0 / (1.0 + jnp.exp(-v))


def _cast_bf16(w, kidx, dep, name):
    rows, cols = w.shape
    tr = min(rows, 512)

    def body(k_ref, w_ref, dep_ref, o_ref):
        o_ref[...] = w_ref[...].astype(BF)

    return pl.pallas_call(
        body, name=name,
        grid_spec=pltpu.PrefetchScalarGridSpec(
            num_scalar_prefetch=1, grid=(rows // tr,),
            in_specs=[pl.BlockSpec((tr, cols), lambda i, k_ref: (i, 0)), ANY],
            out_specs=pl.BlockSpec((None, tr, cols), lambda i, k_ref: (k_ref[0], i, 0))),
        out_shape=jax.ShapeDtypeStruct((N_CHIPS, rows, cols), BF),
        compiler_params=_cp(("parallel",)),
    )(kidx, w, dep)


def _adamw_math(w, g, m, v):
    m = ADAM_B1 * m + (1.0 - ADAM_B1) * g
    v = ADAM_B2 * v + (1.0 - ADAM_B2) * (g * g)
    m_hat = m / (1.0 - ADAM_B1 ** ADAM_STEP)
    v_hat = v / (1.0 - ADAM_B2 ** ADAM_STEP)
    delta = -ADAM_LR * (m_hat / (jnp.sqrt(v_hat) + ADAM_EPS) + ADAM_WD * w)
    return delta, m, v


def _adamw(w, m, v, g, name, dep=None):
    rows, cols = w.shape
    tr = min(rows, 256)

    def body(w_ref, m_ref, v_ref, g_ref, go_ref, d_ref, mo_ref, vo_ref):
        gg = g_ref[...]
        d, mm, vv = _adamw_math(w_ref[...], gg, m_ref[...], v_ref[...])
        go_ref[...] = gg
        d_ref[...] = d
        mo_ref[...] = mm
        vo_ref[...] = vv

    spec = pl.BlockSpec((tr, cols), lambda i: (i, 0))
    shp = jax.ShapeDtypeStruct((rows, cols), F32)
    body, in_specs, args = _add_dep(body, [spec] * 4, [w, m, v, g], dep)
    return pl.pallas_call(
        body, name=name, grid=(rows // tr,),
        in_specs=in_specs, out_specs=[spec] * 4, out_shape=[shp] * 4,
        compiler_params=_cp(("parallel",)),
    )(*args)


def _place():
    x, y, c = lax.axis_index("x"), lax.axis_index("y"), lax.axis_index("c")
    return x, y, c


HBM = pl.BlockSpec(memory_space=pltpu.HBM)
SEM = pl.BlockSpec(memory_space=pltpu.SEMAPHORE)
EFFECT = pltpu.SideEffectType.DATAFLOW_SIDE_EFFECTING


def _add_dep(body, in_specs, args, dep):
    if dep is None:
        return body, list(in_specs), list(args)
    n = len(args)

    def wrapped(*refs):
        body(*refs[:n], *refs[n + 1:])

    return wrapped, list(in_specs) + [ANY], list(args) + [dep]


def _split_call(name, bufs, run, old=None, after=None, new=0):
    nb = len(bufs)
    n_old = 2 if old is not None else 0
    n_in = nb + n_old + (1 if after is not None else 0)

    def body(*refs):
        old_sems = (refs[nb], refs[nb + 1]) if old is not None else None
        new_sems = (refs[n_in], refs[n_in + 1]) if new else None
        run(refs[:nb], old_sems, new_sems)
        if new:
            refs[-1][...] = jnp.zeros_like(refs[-1])

    in_specs = [HBM] * nb + [SEM] * n_old + ([ANY] if after is not None else [])
    out_shape = [pltpu.SemaphoreType.DMA((new,))] * 2 if new else []
    out_specs = [SEM, SEM] if new else []
    out_shape += [pltpu.HBM(b.shape, b.dtype) for b in bufs]
    out_specs += [HBM] * nb
    if new:
        out_shape.append(jax.ShapeDtypeStruct((8, 128), F32))
        out_specs.append(pl.BlockSpec(memory_space=pltpu.VMEM))
    first = 2 if new else 0
    args = [pltpu.with_memory_space_constraint(b, pltpu.HBM) for b in bufs]
    if old is not None:
        args += [old[0], old[1]]
    if after is not None:
        args.append(after)
    outs = pl.pallas_call(
        body, name=name, in_specs=tuple(in_specs), out_specs=tuple(out_specs), out_shape=tuple(out_shape),
        input_output_aliases={i: i + first for i in range(nb)},
        compiler_params=pltpu.CompilerParams(has_side_effects=EFFECT),
    )(*args)
    thru = list(outs[first:first + nb])
    if new:
        return thru, (outs[0], outs[1]), outs[-1]
    return thru, None, None


def _remote(part_src, part_dst, sems, i, to):
    return pltpu.make_async_remote_copy(src_ref=part_src, dst_ref=part_dst, send_sem=sems[0].at[i],
                                        recv_sem=sems[1].at[i], device_id=to, device_id_type=MESH)


def _other_chips(x, y):
    return [(1 - x, y), (x, 1 - y), (1 - x, 1 - y)]


def _gather_start(name, bufs, after=None):
    def run(refs, old, new):
        x, y, c = _place()
        k = 2 * x + y
        for w, ref in enumerate(refs):
            rh = bufs[w].shape[1] // 2
            part = ref.at[k, pl.ds(c * rh, rh)]
            for j, (cx, cy) in enumerate(_other_chips(x, y)):
                _remote(part, part, new, 3 * w + j, (cx, cy, c)).start()

    return _split_call(name, bufs, run, after=after, new=3 * len(bufs))


def _gather_pass(name, bufs, sems, after):
    def run(refs, old, new):
        x, y, c = _place()
        k = 2 * x + y
        sib = (x, y, 1 - c)
        for w, ref in enumerate(refs):
            rh = bufs[w].shape[1] // 2
            for j, (cx, cy) in enumerate(_other_chips(x, y)):
                land = ref.at[2 * cx + cy, pl.ds(c * rh, rh)]
                _remote(land, land, old, 3 * w + j, (cx, cy, c)).wait_recv()
                _remote(land, land, new, 3 * w + j, sib).start()
        for w, ref in enumerate(refs):
            rh = bufs[w].shape[1] // 2
            part = ref.at[k, pl.ds(c * rh, rh)]
            for j, (cx, cy) in enumerate(_other_chips(x, y)):
                _remote(part, part, old, 3 * w + j, (cx, cy, c)).wait_send()

    return _split_call(name, bufs, run, old=sems, after=after, new=3 * len(bufs))


def _gather_finish(name, bufs, sems, after):
    def run(refs, old, new):
        x, y, c = _place()
        sib = (x, y, 1 - c)
        for w, ref in enumerate(refs):
            rh = bufs[w].shape[1] // 2
            for j, (cx, cy) in enumerate(_other_chips(x, y)):
                sent = ref.at[2 * cx + cy, pl.ds(c * rh, rh)]
                _remote(sent, sent, old, 3 * w + j, sib).wait_send()
                land = ref.at[2 * cx + cy, pl.ds((1 - c) * rh, rh)]
                _remote(land, land, old, 3 * w + j, sib).wait_recv()

    return _split_call(name, bufs, run, old=sems, after=after)[0]


def _pair_send_start(name, grads):
    n = len(grads)
    lands = [lax.empty((N_CHIPS,) + g.shape[2:], g.dtype) for g in grads]

    def run(refs, old, new):
        x, y, c = _place()
        for w in range(n):
            _remote(refs[w].at[:, 1 - c], refs[n + w], new, w, (x, y, 1 - c)).start()

    return _split_call(name, list(grads) + lands, run, new=n)


def _pair_send_wait(name, bufs, sems, after):
    n = len(bufs) // 2

    def run(refs, old, new):
        x, y, c = _place()
        for w in range(n):
            cp = _remote(refs[w].at[:, 1 - c], refs[n + w], old, w, (x, y, 1 - c))
            cp.wait_send()
            cp.wait_recv()

    thru = _split_call(name, bufs, run, old=sems, after=after)[0]
    return thru[:n], thru[n:]


def _run_sets(name, scalars, sets, deps=()):
    starts = np.concatenate([[0], np.cumsum([s[0] for s in sets])]).tolist()

    def spec(block, index, lo, n):
        return pl.BlockSpec(block, lambda i, *sc: index(jnp.clip(i - lo, 0, n - 1), *sc))

    in_specs, out_specs, out_shape, args = [], [], [], []
    for (n, ins, outs, _), lo in zip(sets, starts):
        for array, block, index in ins:
            in_specs.append(spec(block, index, lo, n))
            args.append(array)
        for shape, dtype, block, index in outs:
            out_specs.append(spec(block, index, lo, n))
            out_shape.append(jax.ShapeDtypeStruct(shape, dtype))

    def body(*refs):
        refs = refs[len(scalars):]
        n_in = len(in_specs)
        i = pl.program_id(0)
        pos_in, pos_out = 0, n_in + len(deps)
        for (n, ins, outs, fn), lo in zip(sets, starts):
            in_refs = refs[pos_in:pos_in + len(ins)]
            out_refs = refs[pos_out:pos_out + len(outs)]
            pos_in += len(ins)
            pos_out += len(outs)
            pl.when((i >= lo) & (i < lo + n))(functools.partial(fn, in_refs, out_refs))

    outs = pl.pallas_call(
        body, name=name,
        grid_spec=pltpu.PrefetchScalarGridSpec(
            num_scalar_prefetch=len(scalars), grid=(starts[-1],), in_specs=in_specs + [ANY] * len(deps),
            out_specs=out_specs),
        out_shape=out_shape,
        compiler_params=_cp(("arbitrary",)),
    )(*scalars, *args, *deps)
    result, pos = [], 0
    for _, _, outs_s, _ in sets:
        result.append(list(outs[pos:pos + len(outs_s)]))
        pos += len(outs_s)
    return result


def _adamw_all(jobs, name, dep):
    def update(in_refs, out_refs):
        gg = in_refs[3][...]
        out_refs[0][...] = gg
        out_refs[1][...], out_refs[2][...], out_refs[3][...] = _adamw_math(
            in_refs[0][...], gg, in_refs[1][...], in_refs[2][...])

    whole = lambda l: (0, 0)
    sets = [(1, [(a, a.shape, whole) for a in job], [(job[0].shape, F32, job[0].shape, whole)] * 4, update)
            for job in jobs]
    return [tuple(o) for o in _run_sets(name, [], sets, deps=[dep])]


def _pair_add_all(gs, recvs, cidx, name):
    def add(in_refs, out_refs):
        out_refs[0][...] = (in_refs[0][...].astype(F32) + in_refs[1][...].astype(F32)).astype(BF)

    sets = []
    for g, r in zip(gs, recvs):
        _, _, rh, cols = g.shape
        tr = min(rh, 256)
        sets.append((rh // tr,
                     [(g, (N_CHIPS, None, tr, cols), lambda l, c_ref: (0, c_ref[0], l, 0)),
                      (r, (N_CHIPS, tr, cols), lambda l, c_ref: (0, l, 0))],
                     [((N_CHIPS, rh, cols), BF, (N_CHIPS, tr, cols), lambda l, c_ref: (0, l, 0))], add))
    return [o[0] for o in _run_sets(name, [cidx], sets)]


def _chip_add_all(owns, parts, kidx, cidx, name):
    def add(in_refs, out_refs):
        acc = in_refs[0][...].astype(F32)
        for s in range(3):
            acc = acc + in_refs[1][s].astype(F32)
        out_refs[0][...] = acc

    sets = []
    for own, p in zip(owns, parts):
        _, rh, cols = p.shape
        tr = min(rh, 256)
        sets.append((rh // tr,
                     [(own, (None, tr, cols), lambda l, k_ref, c_ref: (k_ref[0], l, 0)),
                      (p, (3, tr, cols), lambda l, k_ref, c_ref: (0, l, 0))],
                     [((2, rh, cols), F32, (None, tr, cols), lambda l, k_ref, c_ref: (c_ref[0], l, 0))], add))
    return [o[0] for o in _run_sets(name, [kidx, cidx], sets)]


def _pair_add(g, recv, cidx, name):
    _, _, rh, cols = g.shape
    tr = min(rh, 256)

    def body(c_ref, g_ref, r_ref, o_ref):
        o_ref[...] = (g_ref[...].astype(F32) + r_ref[...].astype(F32)).astype(BF)

    return pl.pallas_call(
        body, name=name,
        grid_spec=pltpu.PrefetchScalarGridSpec(
            num_scalar_prefetch=1, grid=(rh // tr,),
            in_specs=[pl.BlockSpec((N_CHIPS, None, tr, cols), lambda i, c_ref: (0, c_ref[0], i, 0)),
                      pl.BlockSpec((N_CHIPS, tr, cols), lambda i, c_ref: (0, i, 0))],
            out_specs=pl.BlockSpec((N_CHIPS, tr, cols), lambda i, c_ref: (0, i, 0))),
        out_shape=jax.ShapeDtypeStruct((N_CHIPS, rh, cols), BF),
        compiler_params=_cp(("parallel",)),
    )(cidx, g, recv)


def _chip_send_start(name, sums):
    n = len(sums)
    lands = [lax.empty((3,) + s.shape[1:], BF) for s in sums]

    def run(refs, old, new):
        x, y, c = _place()
        for w in range(n):
            for j, (cx, cy) in enumerate(_other_chips(x, y)):
                _remote(refs[w].at[2 * cx + cy], refs[n + w].at[j], new, 3 * w + j, (cx, cy, c)).start()

    return _split_call(name, list(sums) + lands, run, new=3 * n)


def _chip_send_wait(name, bufs, sems, after):
    n = len(bufs) // 2

    def run(refs, old, new):
        x, y, c = _place()
        for w in range(n):
            for j, (cx, cy) in enumerate(_other_chips(x, y)):
                cp = _remote(refs[w].at[2 * cx + cy], refs[n + w].at[j], old, 3 * w + j, (cx, cy, c))
                cp.wait_send()
                cp.wait_recv()

    thru = _split_call(name, bufs, run, old=sems, after=after)[0]
    return thru[:n], thru[n:]


def _chip_add(own, parts, kidx, cidx, name):
    _, rh, cols = parts.shape
    tr = min(rh, 512)

    def body(k_ref, c_ref, own_ref, p_ref, o_ref):
        acc = own_ref[...].astype(F32)
        for s in range(3):
            acc = acc + p_ref[s].astype(F32)
        o_ref[...] = acc

    return pl.pallas_call(
        body, name=name,
        grid_spec=pltpu.PrefetchScalarGridSpec(
            num_scalar_prefetch=2, grid=(rh // tr,),
            in_specs=[pl.BlockSpec((None, tr, cols), lambda i, k_ref, c_ref: (k_ref[0], i, 0)),
                      pl.BlockSpec((3, tr, cols), lambda i, k_ref, c_ref: (0, i, 0))],
            out_specs=pl.BlockSpec((None, tr, cols), lambda i, k_ref, c_ref: (c_ref[0], i, 0))),
        out_shape=jax.ShapeDtypeStruct((2, rh, cols), F32),
        compiler_params=_cp(("parallel",)),
    )(kidx, cidx, own, parts)


def _pair_swap_start(name, bufs):
    def run(refs, old, new):
        x, y, c = _place()
        for w, ref in enumerate(refs):
            _remote(ref.at[c], ref.at[c], new, w, (x, y, 1 - c)).start()

    return _split_call(name, bufs, run, new=len(bufs))


def _pair_swap_wait(name, bufs, sems, after):
    def run(refs, old, new):
        x, y, c = _place()
        for w, ref in enumerate(refs):
            _remote(ref.at[c], ref.at[c], old, w, (x, y, 1 - c)).wait_send()
            _remote(ref.at[1 - c], ref.at[1 - c], old, w, (x, y, 1 - c)).wait_recv()

    return _split_call(name, bufs, run, old=sems, after=after)[0]


def _peers(x, y, c):
    out = []
    for code in range(1, 8):
        fx, fy, fc = (code >> 2) & 1, (code >> 1) & 1, code & 1
        px = 1 - x if fx else x
        py = 1 - y if fy else y
        pc = 1 - c if fc else c
        out.append((code, (px, py, pc)))
    return out


def _mod_exchange(c_row, ada_w, ada_b4, deps):
    ncol = ada_w.shape[1]

    def body(c_ref, w_ref, b_ref, *rest):
        call_ref, mod_ref, part_ref, send_sems, recv_sems = rest[len(deps):]
        x, y, c = _place()
        k = 2 * x + y
        me = 4 * x + 2 * y + c
        call_ref[pl.ds(me, 1), :] = c_ref[...]
        sends = []
        for code, peer in _peers(x, y, c):
            cp = pltpu.make_async_remote_copy(
                src_ref=c_ref, dst_ref=call_ref.at[pl.ds(me, 1), :],
                send_sem=send_sems.at[code], recv_sem=recv_sems.at[code],
                device_id=peer, device_id_type=MESH)
            cp.start()
            sends.append(cp)
        for code, (px, py, pc) in _peers(x, y, c):
            land = call_ref.at[pl.ds(4 * px + 2 * py + pc, 1), :]
            pltpu.make_async_remote_copy(
                src_ref=land, dst_ref=land, send_sem=send_sems.at[code], recv_sem=recv_sems.at[code],
                device_id=(px, py, pc), device_id_type=MESH).wait_recv()
        call = call_ref[...]
        act = call * _sigmoid(call)
        part = jnp.dot(act, w_ref[...], preferred_element_type=F32,
                       precision=lax.Precision.HIGHEST) + b_ref[pl.ds(k, 1), :]
        part_ref[...] = part
        mod_ref[pl.ds(k, 1), :] = part_ref[pl.ds(me, 1), :]
        chips = [(8 + j, peer) for j, (code, peer) in enumerate(_peers(x, y, c)) if code in (2, 4, 6)]
        for slot, (px, py, pc) in chips:
            cp = pltpu.make_async_remote_copy(
                src_ref=part_ref.at[pl.ds(4 * px + 2 * py + pc, 1), :], dst_ref=mod_ref.at[pl.ds(k, 1), :],
                send_sem=send_sems.at[slot], recv_sem=recv_sems.at[slot],
                device_id=(px, py, pc), device_id_type=MESH)
            cp.start()
            sends.append(cp)
        for slot, (px, py, pc) in chips:
            land = mod_ref.at[pl.ds(2 * px + py, 1), :]
            pltpu.make_async_remote_copy(
                src_ref=land, dst_ref=land, send_sem=send_sems.at[slot], recv_sem=recv_sems.at[slot],
                device_id=(px, py, pc), device_id_type=MESH).wait_recv()
        for cp in sends:
            cp.wait_send()

    vm = pl.BlockSpec(memory_space=pltpu.VMEM)
    return pl.pallas_call(
        body, name="mod_exchange",
        in_specs=[vm, vm, vm] + [ANY] * len(deps), out_specs=[vm, vm],
        out_shape=[jax.ShapeDtypeStruct((8, D_MODEL), F32), jax.ShapeDtypeStruct((N_CHIPS, ncol), F32)],
        scratch_shapes=[pltpu.VMEM((8, ncol), F32), pltpu.SemaphoreType.DMA((16,)),
                        pltpu.SemaphoreType.DMA((16,))],
        compiler_params=_cp(),
    )(c_row, ada_w, ada_b4, *deps)


def _small_exchange(stats, rows, c_all, wsm, msm, vsm):
    ncol = 6 * D_MODEL // N_CHIPS
    ns = len(stats)

    def body(*refs):
        call_ref, w_ref, m_ref, v_ref, gw_ref, loss_ref = refs[ns:ns + 6]
        outs = refs[ns + 6:ns + 30]
        p_ref, g_ref, all_ref, dm_ref, send_sems, recv_sems = refs[ns + 30:]
        x, y, c = _place()
        k = 2 * x + y
        me = 4 * x + 2 * y + c
        for r, (tab, row) in enumerate(rows):
            p_ref[r] = refs[tab][row:row + 1, :]
        all_ref[:, pl.ds(me, 1), :] = p_ref[...]
        sends = []
        for code, peer in _peers(x, y, c):
            cp = pltpu.make_async_remote_copy(
                src_ref=p_ref, dst_ref=all_ref.at[:, pl.ds(me, 1), :],
                send_sem=send_sems.at[code], recv_sem=recv_sems.at[code],
                device_id=peer, device_id_type=MESH)
            cp.start()
            sends.append(cp)
        for code, (px, py, pc) in _peers(x, y, c):
            land = all_ref.at[:, pl.ds(4 * px + 2 * py + pc, 1), :]
            pltpu.make_async_remote_copy(
                src_ref=land, dst_ref=land, send_sem=send_sems.at[code], recv_sem=recv_sems.at[code],
                device_id=(px, py, pc), device_id_type=MESH).wait_recv()
        for cp in sends:
            cp.wait_send()
        tot = [_colsum(all_ref[r]) for r in range(N_PAY)]
        loss_ref[...] = jnp.sum(tot[P_LOSS], axis=1, keepdims=True)
        g_ref[...] = jnp.zeros_like(g_ref)
        for r in range(P_LOSS):
            g_ref[r:r + 1, :] = tot[r]
        g = g_ref[...]
        for kind, tab in enumerate((g,) + _adamw_math(w_ref[...], g, m_ref[...], v_ref[...])):
            for r in range(6):
                outs[6 * kind][:, r * D_MODEL:(r + 1) * D_MODEL] = tab[r:r + 1, :]
            for i in range(5):
                outs[6 * kind + 1 + i][...] = tab[6 + i:7 + i, :]
        half = D_MODEL // 2
        for kk in range(N_CHIPS):
            @pl.when(k == kk)
            def _():
                r0 = 3 * (kk // 2)
                if kk % 2 == 0:
                    dm_ref[:, :D_MODEL] = all_ref[r0]
                    dm_ref[:, D_MODEL:] = all_ref[r0 + 1][:, :half]
                else:
                    dm_ref[:, :half] = all_ref[r0 + 1][:, half:]
                    dm_ref[:, half:] = all_ref[r0 + 2]
        call = call_ref[...]
        act = call * _sigmoid(call)
        gw_ref[...] = lax.dot_general(act, dm_ref[...], (((0,), (0,)), ((), ())),
                                      preferred_element_type=F32, precision=lax.Precision.HIGHEST)

    vm = pl.BlockSpec(memory_space=pltpu.VMEM)
    vectors = [jax.ShapeDtypeStruct((1, 6 * D_MODEL), F32)] + [jax.ShapeDtypeStruct((1, D_MODEL), F32)] * 5
    outs = pl.pallas_call(
        body, name="small_exchange",
        in_specs=[vm] * (ns + 4), out_specs=[vm] * 26,
        out_shape=[jax.ShapeDtypeStruct((D_MODEL, ncol), F32), jax.ShapeDtypeStruct((1, 1), F32)] + vectors * 4,
        scratch_shapes=[pltpu.VMEM((N_PAY, 1, D_MODEL), F32), pltpu.VMEM((16, D_MODEL), F32),
                        pltpu.VMEM((N_PAY, 8, D_MODEL), F32), pltpu.VMEM((8, ncol), F32),
                        pltpu.SemaphoreType.DMA((8,)), pltpu.SemaphoreType.DMA((8,))],
        compiler_params=_cp(),
    )(*stats, c_all, wsm, msm, vsm)
    loss, *small = _pass_on(outs[1:])
    return outs[0], loss, [small[6 * kind:6 + 6 * kind] for kind in range(4)]


def _pass_on(arrays):
    n = len(arrays)

    def body(*refs):
        for i in range(n):
            refs[n + i][...] = refs[i][...]

    return pl.pallas_call(
        body, name="small_outputs", out_shape=[jax.ShapeDtypeStruct(a.shape, a.dtype) for a in arrays],
        compiler_params=_cp(),
    )(*arrays)


def _rope_tables(pos_col, inv_freq, sign, dep=None):
    def body(p_ref, f_ref, s_ref, cos_ref, sin_ref):
        ang = p_ref[...].astype(F32) * f_ref[...]
        cos_ref[...] = jnp.cos(ang)
        sin_ref[...] = jnp.sin(ang) * s_ref[...]

    tr = 512
    shp = jax.ShapeDtypeStruct((SEQ, 128), F32)
    body, in_specs, args = _add_dep(
        body, [pl.BlockSpec((tr, 1), lambda i: (i, 0)), pl.BlockSpec((1, 128), lambda i: (0, 0)),
               pl.BlockSpec((1, 128), lambda i: (0, 0))], [pos_col, inv_freq, sign], dep)
    return pl.pallas_call(
        body, name="rope_tables", grid=(SEQ // tr,),
        in_specs=in_specs,
        out_specs=[pl.BlockSpec((tr, 128), lambda i: (i, 0))] * 2, out_shape=[shp, shp],
        compiler_params=_cp(("parallel",)),
    )(*args)


def _resident(shape):
    nd = len(shape)
    return pl.BlockSpec(shape, lambda *_: (0,) * nd, pipeline_mode=pl.Buffered(1))


def _ln_proj(x, vecs, w_in4):
    tm = min(512, SEQ)
    wc = w_in4.shape[2]

    def body(x_ref, vec_ref, w_ref, h_ref, proj_ref):
        xx = x_ref[...]
        g = _row(vec_ref, V_G1) * (1.0 + _row(vec_ref, V_SC1))
        h = (xx * _rms(xx) * g + _row(vec_ref, V_SH1)).astype(BF)
        h_ref[...] = h
        for j in range(N_CHIPS):
            proj_ref[:, j * wc:(j + 1) * wc] = _dot(h, w_ref[j]).astype(BF)

    return pl.pallas_call(
        body, name="ln_proj", grid=(SEQ // tm,),
        in_specs=[pl.BlockSpec((tm, D_MODEL), lambda i: (i, 0)), _resident((16, D_MODEL)),
                  _resident(w_in4.shape)],
        out_specs=[pl.BlockSpec((tm, D_MODEL), lambda i: (i, 0)), pl.BlockSpec((tm, D_IN), lambda i: (i, 0))],
        out_shape=[jax.ShapeDtypeStruct((SEQ, D_MODEL), BF), jax.ShapeDtypeStruct((SEQ, D_IN), BF)],
        compiler_params=_cp(("parallel",)),
    )(x, vecs, w_in4)


def _lane_first(shape):
    lane = lax.broadcasted_iota(jnp.int32, shape, 1)
    return (lane & 32) == 0


def _rot(v, cos, sin_s):
    partner = jnp.where(_lane_first(v.shape), pltpu.roll(v, 96, 1), pltpu.roll(v, 32, 1))
    return v * cos + partner * sin_s


def _rot_t(dv, cos, sin_s):
    t = dv * sin_s
    partner = jnp.where(_lane_first(dv.shape), pltpu.roll(t, 96, 1), pltpu.roll(t, 32, 1))
    return dv * cos + partner


def _ret_mask(lg):
    t = RET_BLOCK
    ii = lax.broadcasted_iota(jnp.int32, (t, t), 0)
    jj = lax.broadcasted_iota(jnp.int32, (t, t), 1)
    dist = jnp.abs(ii - jj).astype(F32)
    future = (jj >> RET_CHUNK_SHIFT) > (ii >> RET_CHUNK_SHIFT)
    return jnp.where(future, 0.0, jnp.exp(lg * dist))


def _ret_masks(lg, mask_ref, head):
    t = RET_BLOCK
    mask = mask_ref[head]
    ti = lax.broadcasted_iota(jnp.int32, (t, 1), 0).astype(F32)
    from_start = jnp.exp(lg * (ti + 1.0))
    to_end = jnp.exp(lg * (t - 1.0 - ti))
    whole = jnp.exp(jnp.full((1, 128), lg * t, F32))
    return mask, from_start, to_end, whole


def _head_lanes(shape, hh):
    lane = lax.broadcasted_iota(jnp.int32, shape, 1)
    return (lane >> 6) == hh


def _ret_specs():
    t = RET_BLOCK
    return dict(
        q=lambda f: pl.BlockSpec((t, 512), lambda n: (f(n), C_QR // 512)),
        k=lambda f: pl.BlockSpec((t, 512), lambda n: (f(n), C_KR // 512)),
        v=lambda f: pl.BlockSpec((t, D_MODEL), lambda n: (f(n), C_VR // D_MODEL)),
        g=lambda f: pl.BlockSpec((t, D_MODEL), lambda n: (f(n), C_GR // D_MODEL)),
        tab=lambda f: pl.BlockSpec((t, 128), lambda n: (f(n), 0)),
        wide=lambda f: pl.BlockSpec((t, D_MODEL), lambda n: (f(n), 0)),
        state=lambda f: pl.BlockSpec((N_PAIRS, None, 2, 128, 128), lambda n: (0, f(n), 0, 0, 0)),
    )


def _ret_fwd(proj, cos, sin_s, gn_g, log_gamma, dep=None):
    t = RET_BLOCK
    nb = SEQ // t

    def body(lg_ref, q_ref, k_ref, v_ref, g_ref, cos_ref, sin_ref, gn_ref, o_ref, retg_ref, st_ref, state, masks):
        @pl.when(pl.program_id(0) == 0)
        def _():
            state[...] = jnp.zeros_like(state)
            for head in range(2 * N_PAIRS):
                masks[head] = _ret_mask(lg_ref[head])

        cos, sn = cos_ref[...], sin_ref[...]
        for p in range(N_PAIRS):
            q = _rot(q_ref[:, 128 * p:128 * (p + 1)].astype(F32), cos, sn)
            k = _rot(k_ref[:, 128 * p:128 * (p + 1)].astype(F32), cos, sn) * QK_SCALE
            for hh in range(2):
                cols = slice(256 * p + 128 * hh, 256 * p + 128 * (hh + 1))
                lg = lg_ref[2 * p + hh]
                mask, from_start, to_end, whole = _ret_masks(lg, masks, 2 * p + hh)
                lanes = _head_lanes(q.shape, hh)
                qm = jnp.where(lanes, q, 0.0)
                km = jnp.where(lanes, k, 0.0)
                vh = v_ref[:, cols]
                sc = _dot_nt(qm.astype(BF), km.astype(BF)) * mask
                st = state[p, hh]
                st_ref[p, hh] = st
                o = _dot(sc.astype(BF), vh) + _dot((qm * from_start).astype(BF), st.astype(BF))
                state[p, hh] = whole * st + _dot_tn((km * to_end).astype(BF), vh)
                d = o - _rowmean(o)
                nh = d * lax.rsqrt(_rowmean(d * d) + EPS)
                gr = g_ref[:, cols].astype(F32)
                o_ref[:, cols] = o
                retg_ref[:, cols] = (gr * _sigmoid(gr) * nh * gn_ref[:, cols]).astype(BF)

    sp = _ret_specs()
    ident = lambda n: n
    body, in_specs, args = _add_dep(
        body, [pl.BlockSpec(memory_space=pltpu.SMEM), sp["q"](ident), sp["k"](ident), sp["v"](ident),
               sp["g"](ident), sp["tab"](ident), sp["tab"](ident), _resident((1, D_MODEL))],
        [log_gamma, proj, proj, proj, proj, cos, sin_s, gn_g], dep)
    return pl.pallas_call(
        body, name="ret_fwd", grid=(nb,),
        in_specs=in_specs,
        out_specs=[sp["wide"](ident), sp["wide"](ident), sp["state"](ident)],
        out_shape=[jax.ShapeDtypeStruct((SEQ, D_MODEL), F32), jax.ShapeDtypeStruct((SEQ, D_MODEL), BF),
                   jax.ShapeDtypeStruct((N_PAIRS, nb, 2, 128, 128), F32)],
        scratch_shapes=[pltpu.VMEM((N_PAIRS, 2, 128, 128), F32),
                        pltpu.VMEM((2 * N_PAIRS, RET_BLOCK, RET_BLOCK), F32)],
        compiler_params=_cp(("arbitrary",)),
    )(*args)


def _ret_bwd(proj, cos, sin_s, dret, states, log_gamma, dep=None):
    t = RET_BLOCK
    nb = SEQ // t

    def body(lg_ref, q_ref, k_ref, v_ref, cos_ref, sin_ref, do_ref, st_ref, dqkv_ref, dstate, masks):
        @pl.when(pl.program_id(0) == 0)
        def _():
            dstate[...] = jnp.zeros_like(dstate)
            for head in range(2 * N_PAIRS):
                masks[head] = _ret_mask(lg_ref[head])

        cos, sn = cos_ref[...], sin_ref[...]
        for p in range(N_PAIRS):
            q = _rot(q_ref[:, 128 * p:128 * (p + 1)].astype(F32), cos, sn)
            k = _rot(k_ref[:, 128 * p:128 * (p + 1)].astype(F32), cos, sn) * QK_SCALE
            dq_rot = jnp.zeros(q.shape, F32)
            dk_rot = jnp.zeros(q.shape, F32)
            for hh in range(2):
                cols = slice(256 * p + 128 * hh, 256 * p + 128 * (hh + 1))
                lg = lg_ref[2 * p + hh]
                mask, from_start, to_end, whole = _ret_masks(lg, masks, 2 * p + hh)
                lanes = _head_lanes(q.shape, hh)
                qm = jnp.where(lanes, q, 0.0)
                km = jnp.where(lanes, k, 0.0)
                qb, kb = qm.astype(BF), km.astype(BF)
                vh = v_ref[:, cols]
                do = do_ref[:, cols]
                sc = (_dot_nt(qb, kb) * mask).astype(BF)
                st = st_ref[p, hh].astype(BF)
                dst = dstate[p, hh]
                dstb = dst.astype(BF)
                k_end = (km * to_end).astype(BF)
                q_start = (qm * from_start).astype(BF)
                dqkv_ref[:, C_VR + 256 * p + 128 * hh:C_VR + 256 * p + 128 * (hh + 1)] = (
                    _dot_tn(sc, do) + _dot(k_end, dstb)).astype(BF)
                dsc = (_dot_nt(do, vh) * mask).astype(BF)
                dq_h = _dot(dsc, kb) + _dot_nt(do, st) * from_start
                dq_rot = dq_rot + jnp.where(lanes, dq_h, 0.0)
                dk_rot = dk_rot + _dot_tn(dsc, qb) + _dot_nt(vh, dstb) * to_end
                dstate[p, hh] = whole * dst + _dot_tn(q_start, do)
            dqkv_ref[:, C_QR + 128 * p:C_QR + 128 * (p + 1)] = _rot_t(dq_rot, cos, sn).astype(BF)
            dqkv_ref[:, C_KR + 128 * p:C_KR + 128 * (p + 1)] = _rot_t(dk_rot * QK_SCALE, cos, sn).astype(BF)

    sp = _ret_specs()
    rev = lambda n: nb - 1 - n
    body, in_specs, args = _add_dep(
        body, [pl.BlockSpec(memory_space=pltpu.SMEM), sp["q"](rev), sp["k"](rev), sp["v"](rev),
               sp["tab"](rev), sp["tab"](rev), sp["wide"](rev), sp["state"](rev)],
        [log_gamma, proj, proj, proj, cos, sin_s, dret, states], dep)
    return pl.pallas_call(
        body, name="ret_bwd", grid=(nb,),
        in_specs=in_specs,
        out_specs=pl.BlockSpec((t, C_GR), lambda n: (rev(n), 0)),
        out_shape=jax.ShapeDtypeStruct((SEQ, C_GR), BF),
        scratch_shapes=[pltpu.VMEM((N_PAIRS, 2, 128, 128), F32),
                        pltpu.VMEM((2 * N_PAIRS, RET_BLOCK, RET_BLOCK), F32)],
        compiler_params=_cp(("arbitrary",)),
    )(*args)


def _stack_heads(v):
    return jnp.concatenate([jnp.where(_head_lanes(v.shape, hh), v, jnp.zeros_like(v)) for hh in range(2)], axis=0)


def _unstack_heads(v):
    t = v.shape[0] // 2
    return jnp.where(_head_lanes((t, v.shape[1]), 0), v[:t], v[t:])


def _sb_masks(t, heads):
    rr = lax.broadcasted_iota(jnp.int32, (t, t), 0)
    cc = lax.broadcasted_iota(jnp.int32, (t, t), 1)
    r2 = lax.broadcasted_iota(jnp.int32, (heads * t, t), 0) & (t - 1)
    c2 = lax.broadcasted_iota(jnp.int32, (heads * t, t), 1)
    return rr, cc, c2 < r2


def _split_dot2(v, tri):
    return _dot(v.astype(BF), tri)


def _log2_sigmoids(z2):
    minus_abs = lax.bitcast_convert_type(
        lax.bitcast_convert_type(z2, jnp.uint32) | jnp.uint32(0x80000000), F32)
    ls = jnp.minimum(z2, 0.0) - jnp.log2(1.0 + jnp.exp2(minus_abs))
    return ls, ls - z2


def _sb_fwd(proj):
    t, g = SB_BLOCK, SB_GROUP
    nq = SEQ // t
    rows = 2 * g * t

    def body(q_ref, k_ref, v_ref, o_ref, tot_ref, kt_ref):
        i = pl.program_id(1)

        @pl.when(i == 0)
        def _():
            for p in range(g):
                for jj in range(nq):
                    kt_ref[p, jj] = k_ref[jj * t:(jj + 1) * t, 128 * p:128 * (p + 1)].T

        q2 = [_stack_heads((q_ref[:, 128 * p:128 * (p + 1)].astype(F32) * QK_SCALE).astype(BF)) for p in range(g)]
        rr, cc, valid = _sb_masks(t, 2 * g)
        later = (rr > cc).astype(BF)

        def tile(j, carry, diagonal):
            acc, run = carry
            z = jnp.concatenate([_dot(q2[p], kt_ref[p, j]) for p in range(g)], axis=0) * LOG2E
            ls, lm = _log2_sigmoids(z)
            if diagonal:
                lm = jnp.where(valid, lm, 0.0)
            after = _split_dot2(lm, later)
            a = jnp.exp2(ls + after + run)
            if diagonal:
                a = jnp.where(valid, a, 0.0)
            ab = a.astype(BF)
            keys = pl.ds(pl.multiple_of(j * t, t), t)
            av = jnp.concatenate([_dot(ab[2 * t * p:2 * t * (p + 1)], v_ref[keys, 128 * p:128 * (p + 1)])
                                  for p in range(g)], axis=0)
            return acc + av, run + after[:, 0:1] + lm[:, 0:1]

        carry = tile(i, (jnp.zeros((rows, 128), F32), jnp.zeros((rows, 1), F32)), True)
        acc, run = lax.fori_loop(0, i, lambda s, cr: tile(i - 1 - s, cr, False), carry)
        run = jnp.broadcast_to(run, (rows, 128))
        for p in range(g):
            o_ref[:, 128 * p:128 * (p + 1)] = _unstack_heads(acc[2 * t * p:2 * t * (p + 1)]).astype(BF)
            tot_ref[:, 128 * p:128 * (p + 1)] = _unstack_heads(run[2 * t * p:2 * t * (p + 1)])

    w = 128 * g
    return pl.pallas_call(
        body, name="sb_fwd", grid=(N_PAIRS // g, nq),
        in_specs=[pl.BlockSpec((t, w), lambda p, i: (i, C_QS // w + p)),
                  pl.BlockSpec((SEQ, w), lambda p, i: (0, C_KS // w + p)),
                  pl.BlockSpec((SEQ, w), lambda p, i: (0, C_VS // w + p))],
        out_specs=[pl.BlockSpec((t, w), lambda p, i: (i, p))] * 2,
        out_shape=[jax.ShapeDtypeStruct((SEQ, 512), BF), jax.ShapeDtypeStruct((SEQ, 512), F32)],
        scratch_shapes=[pltpu.VMEM((g, nq, 128, t), BF)],
        compiler_params=_cp(("parallel", "arbitrary")),
    )(proj, proj, proj)


def _sb_bwd(proj, dsb, tot, dep=None):
    t, g = SB_BLOCK, SB_GROUP
    nq = SEQ // t
    rows = 2 * g * t

    def body(q_ref, k_ref, v_ref, do_ref, tot_ref, dq_ref, dk_ref, dv_ref, kt_ref, vt_ref, dkt_acc, dvt_acc):
        i = pl.program_id(1)

        @pl.when(i == 0)
        def _():
            dkt_acc[...] = jnp.zeros_like(dkt_acc)
            dvt_acc[...] = jnp.zeros_like(dvt_acc)
            for p in range(g):
                for jj in range(nq):
                    kt_ref[p, jj] = k_ref[jj * t:(jj + 1) * t, 128 * p:128 * (p + 1)].T
                    vt_ref[p, jj] = v_ref[jj * t:(jj + 1) * t, 128 * p:128 * (p + 1)].T

        q2 = [_stack_heads((q_ref[:, 128 * p:128 * (p + 1)].astype(F32) * QK_SCALE).astype(BF)) for p in range(g)]
        do2 = [_stack_heads(do_ref[:, 128 * p:128 * (p + 1)]) for p in range(g)]
        q2t = [v.T for v in q2]
        do2t = [v.T for v in do2]
        tots = tot_ref[...]
        total = jnp.concatenate([tots[:, 64 * h:64 * h + 1] for h in range(2 * g)], axis=0)
        rr, cc, valid = _sb_masks(t, 2 * g)
        upto = (rr <= cc).astype(BF)
        before = (rr < cc).astype(BF)

        def part(v, p):
            return v[2 * t * p:2 * t * (p + 1)]

        def tile(j, carry, diagonal):
            dq, run_l, run_g = carry
            z = jnp.concatenate([_dot(q2[p], kt_ref[p, j]) for p in range(g)], axis=0) * LOG2E
            ls, lm = _log2_sigmoids(z)
            if diagonal:
                lm = jnp.where(valid, lm, 0.0)
            incl = _split_dot2(lm, upto)
            a = jnp.exp2(ls + (total - (incl + run_l)))
            if diagonal:
                a = jnp.where(valid, a, 0.0)
            gg = a * jnp.concatenate([_dot(do2[p], vt_ref[p, j]) for p in range(g)], axis=0)
            excl = _split_dot2(gg, before)
            dz = gg * jnp.exp2(lm) - (excl + run_g) * jnp.exp2(ls)
            if diagonal:
                dz = jnp.where(valid, dz, 0.0)
            dzb = dz.astype(BF)
            ab = a.astype(BF)
            keys = pl.ds(pl.multiple_of(j * t, t), t)
            for p in range(g):
                dkt_acc[p, j] += _dot(q2t[p], part(dzb, p))
                dvt_acc[p, j] += _dot(do2t[p], part(ab, p))
            dq_t = jnp.concatenate([_dot(part(dzb, p), k_ref[keys, 128 * p:128 * (p + 1)]) for p in range(g)], axis=0)
            return (dq + dq_t, run_l + incl[:, t - 1:t], run_g + excl[:, t - 1:t] + gg[:, t - 1:t])

        zero = jnp.zeros((rows, 1), F32)
        carry = lax.fori_loop(0, i, lambda j, cr: tile(j, cr, False), (jnp.zeros((rows, 128), F32), zero, zero))
        dq = tile(i, carry, True)[0]
        for p in range(g):
            dq_ref[:, 128 * p:128 * (p + 1)] = (_unstack_heads(part(dq, p)) * QK_SCALE).astype(BF)

        @pl.when(i == nq - 1)
        def _():
            for p in range(g):
                for jj in range(nq):
                    dk_ref[jj * t:(jj + 1) * t, 128 * p:128 * (p + 1)] = dkt_acc[p, jj].T.astype(BF)
                    dv_ref[jj * t:(jj + 1) * t, 128 * p:128 * (p + 1)] = dvt_acc[p, jj].T.astype(BF)

    w = 128 * g
    tile_spec = pl.BlockSpec((t, w), lambda p, i: (i, p))
    col_spec = pl.BlockSpec((SEQ, w), lambda p, i: (0, p))
    shp = jax.ShapeDtypeStruct((SEQ, 512), BF)
    body, in_specs, args = _add_dep(
        body, [pl.BlockSpec((t, w), lambda p, i: (i, C_QS // w + p)),
               pl.BlockSpec((SEQ, w), lambda p, i: (0, C_KS // w + p)),
               pl.BlockSpec((SEQ, w), lambda p, i: (0, C_VS // w + p)),
               tile_spec, tile_spec],
        [proj, proj, proj, dsb, tot], dep)
    return pl.pallas_call(
        body, name="sb_bwd", grid=(N_PAIRS // g, nq),
        in_specs=in_specs,
        out_specs=[tile_spec, col_spec, col_spec],
        out_shape=[shp, shp, shp],
        scratch_shapes=[pltpu.VMEM((g, nq, 128, t), BF), pltpu.VMEM((g, nq, 128, t), BF),
                        pltpu.VMEM((g, nq, 128, t), F32), pltpu.VMEM((g, nq, 128, t), F32)],
        compiler_params=_cp(("parallel", "arbitrary")),
    )(*args)


def _mix_out(retg, sb, proj, x, vecs, w_ret, w_sb4, w_out):
    tm, half = min(512, SEQ), 512

    def body(r_ref, s_ref, ar0, ar1, as0, as1, x_ref, vec_ref, wr_ref, ws_ref, wo_ref,
             mix_ref, rb_ref, sbp_ref, y_ref, h1_ref, h2_ref):
        rb = _dot(r_ref[...], wr_ref[...])
        sbv = s_ref[...]
        sbp = jnp.concatenate([_dot(sbv, ws_ref[k]) for k in range(N_CHIPS)], axis=1)
        gate_r = _sigmoid(jnp.concatenate([ar0[...], ar1[...]], axis=1).astype(F32))
        gate_s = _sigmoid(jnp.concatenate([as0[...], as1[...]], axis=1).astype(F32))
        mixed = (gate_r * rb + gate_s * sbp).astype(BF)
        mix_ref[...] = mixed
        rb_ref[...] = rb.astype(BF)
        sbp_ref[...] = sbp.astype(BF)
        y = _dot(mixed, wo_ref[...])
        h1 = x_ref[...] + _row(vec_ref, V_GT1) * (y * _rms(y)) * _row(vec_ref, V_G2)
        g = _row(vec_ref, V_G3) * (1.0 + _row(vec_ref, V_SC2))
        y_ref[...] = y
        h1_ref[...] = h1
        h2_ref[...] = (h1 * _rms(h1) * g + _row(vec_ref, V_SH2)).astype(BF)

    row = pl.BlockSpec((tm, D_MODEL), lambda i: (i, 0))
    gate = lambda c0: pl.BlockSpec((tm, half), lambda i: (i, c0 // half))
    bf = jax.ShapeDtypeStruct((SEQ, D_MODEL), BF)
    f32 = jax.ShapeDtypeStruct((SEQ, D_MODEL), F32)
    return pl.pallas_call(
        body, name="mix_out", grid=(SEQ // tm,),
        in_specs=[row, pl.BlockSpec((tm, 512), lambda i: (i, 0)), gate(C_AR), gate(C_AR + half), gate(C_AS),
                  gate(C_AS + half), row, _resident((16, D_MODEL)), _resident((D_MODEL, D_MODEL)),
                  _resident(w_sb4.shape), _resident((D_MODEL, D_MODEL))],
        out_specs=[row] * 6, out_shape=[bf, bf, bf, f32, f32, bf],
        compiler_params=_cp(("parallel",)),
    )(retg, sb, proj, proj, proj, proj, x, vecs, w_ret, w_sb4, w_out)


def _ffn(h2, h1, y, target, vecs, w_ff14, w_ff24):
    tm = 256

    def body(h2_ref, h1_ref, t_ref, y_ref, vec_ref, w1_ref, w2_ref,
             a_ref, df_ref, du_ref, dh1_ref, dy_ref, sta_ref, stb_ref, u_scr):
        @pl.when(pl.program_id(0) == 0)
        def _():
            sta_ref[...] = jnp.zeros_like(sta_ref)
            stb_ref[...] = jnp.zeros_like(stb_ref)

        hb = h2_ref[...]
        f = jnp.zeros((tm, D_MODEL), F32)
        for k in range(N_CHIPS):
            cols = slice(k * D_MODEL, (k + 1) * D_MODEL)
            u = _dot(hb, w1_ref[k])
            r = jnp.maximum(u, 0.0)
            act = (r * r).astype(BF)
            u_scr[:, cols] = u.astype(BF)
            a_ref[:, cols] = act
            f = f + _dot(act, w2_ref[k])
        r4 = _rms(f)
        fn = f * r4
        gt2, g4 = _row(vec_ref, V_GT2), _row(vec_ref, V_G4)
        h1 = h1_ref[...]
        diff = h1 + gt2 * fn * g4 - t_ref[...]
        dout = diff * (1.0 / D_MODEL)
        dfn = dout * gt2 * g4
        dfb = (r4 * (dfn - fn * _rowmean(dfn * fn))).astype(BF)
        df_ref[...] = dfb
        sta_ref[0:1, :] += _colsum(dout * fn * g4)
        sta_ref[1:2, :] += _colsum(dout * gt2 * fn)
        sta_ref[2:3, :] += _colsum(diff * diff) * (0.5 / D_MODEL)

        dh2 = jnp.zeros((tm, D_MODEL), F32)
        for k in range(N_CHIPS):
            cols = slice(k * D_MODEL, (k + 1) * D_MODEL)
            da = _dot_nt(dfb, w2_ref[k])
            du = (da * (2.0 * jnp.maximum(u_scr[:, cols].astype(F32), 0.0))).astype(BF)
            du_ref[:, cols] = du
            dh2 = dh2 + _dot_nt(du, w1_ref[k])
        r3 = _rms(h1)
        hn3 = h1 * r3
        g3, sc2 = _row(vec_ref, V_G3), _row(vec_ref, V_SC2)
        dhn3 = dh2 * g3 * (1.0 + sc2)
        dh1 = dout + r3 * (dhn3 - hn3 * _rowmean(dhn3 * hn3))
        yy = y_ref[...]
        r2 = _rms(yy)
        yn = yy * r2
        gt1, g2 = _row(vec_ref, V_GT1), _row(vec_ref, V_G2)
        dyn = dh1 * gt1 * g2
        dh1_ref[...] = dh1
        dy_ref[...] = (r2 * (dyn - yn * _rowmean(dyn * yn))).astype(BF)
        stb_ref[0:1, :] += _colsum(dh2)
        stb_ref[1:2, :] += _colsum(dh2 * hn3 * g3)
        stb_ref[2:3, :] += _colsum(dh2 * hn3 * (1.0 + sc2))
        stb_ref[3:4, :] += _colsum(dh1 * yn * g2)
        stb_ref[4:5, :] += _colsum(dh1 * gt1 * yn)

    row = pl.BlockSpec((tm, D_MODEL), lambda i: (i, 0))
    wide = pl.BlockSpec((tm, D_FF), lambda i: (i, 0))
    sums = pl.BlockSpec((8, D_MODEL), lambda i: (0, 0))
    return pl.pallas_call(
        body, name="ffn", grid=(SEQ // tm,),
        in_specs=[row, row, row, row, _resident((16, D_MODEL)), _resident(w_ff14.shape), _resident(w_ff24.shape)],
        out_specs=[wide, row, wide, row, row, sums, sums],
        out_shape=[jax.ShapeDtypeStruct((SEQ, D_FF), BF), jax.ShapeDtypeStruct((SEQ, D_MODEL), BF),
                   jax.ShapeDtypeStruct((SEQ, D_FF), BF), jax.ShapeDtypeStruct((SEQ, D_MODEL), F32),
                   jax.ShapeDtypeStruct((SEQ, D_MODEL), BF), jax.ShapeDtypeStruct((8, D_MODEL), F32),
                   jax.ShapeDtypeStruct((8, D_MODEL), F32)],
        scratch_shapes=[pltpu.VMEM((tm, D_FF), BF)],
        compiler_params=_cp(("arbitrary",)),
    )(h2, h1, target, y, vecs, w_ff14, w_ff24)


def _mix_ret_bwd(dy, proj, rb, sbp, o_raw, gn_g, w_out, w_sb4, w_ret):
    tm, half = min(512, SEQ), 512

    def body(dy_ref, ar0, ar1, as0, as1, rb_ref, sbp_ref, g_ref, o_ref, gn_ref, wo_ref, ws_ref, wr_ref,
             drb_ref, dsbp_ref, da_ref, dsb_ref, dret_ref, dgr_ref, st_ref):
        @pl.when(pl.program_id(0) == 0)
        def _():
            st_ref[...] = jnp.zeros_like(st_ref)

        dm_all = _dot_nt(dy_ref[...], wo_ref[...])
        dsb = jnp.zeros((tm, 512), F32)
        drbs = []
        for hf, (ar_ref, as_ref) in enumerate(((ar0, as0), (ar1, as1))):
            cols = slice(half * hf, half * (hf + 1))
            dm = dm_all[:, cols]
            sr = _sigmoid(ar_ref[...].astype(F32))
            ss = _sigmoid(as_ref[...].astype(F32))
            dsbp = (dm * ss).astype(BF)
            drbs.append((dm * sr).astype(BF))
            dsbp_ref[:, cols] = dsbp
            da_ref[:, cols] = (dm * rb_ref[:, cols].astype(F32) * sr * (1.0 - sr)).astype(BF)
            da_ref[:, D_MODEL + half * hf:D_MODEL + half * (hf + 1)] = (
                dm * sbp_ref[:, cols].astype(F32) * ss * (1.0 - ss)).astype(BF)
            dsb = dsb + _dot_nt(dsbp[:, :256], ws_ref[2 * hf]) + _dot_nt(dsbp[:, 256:], ws_ref[2 * hf + 1])
        dsb_ref[...] = dsb.astype(BF)
        drb = jnp.concatenate(drbs, axis=1)
        drb_ref[...] = drb
        dretg = _dot_nt(drb, wr_ref[...])
        for gi in range(D_MODEL // 128):
            cols = slice(128 * gi, 128 * (gi + 1))
            o = o_ref[:, cols]
            d = o - _rowmean(o)
            rstd = lax.rsqrt(_rowmean(d * d) + EPS)
            nh = d * rstd
            gain = gn_ref[:, cols]
            gr = g_ref[:, cols].astype(F32)
            sg = _sigmoid(gr)
            dg = dretg[:, cols]
            dgn = dg * gr * sg
            dnh = dgn * gain
            dgr_ref[:, cols] = (dg * nh * gain * sg * (1.0 + gr * (1.0 - sg))).astype(BF)
            dret_ref[:, cols] = (rstd * (dnh - _rowmean(dnh) - nh * _rowmean(dnh * nh))).astype(BF)
            st_ref[0:1, cols] += _colsum(dgn * nh)

    row = pl.BlockSpec((tm, D_MODEL), lambda i: (i, 0))
    gate = lambda c0: pl.BlockSpec((tm, half), lambda i: (i, c0 // half))
    shp = jax.ShapeDtypeStruct((SEQ, D_MODEL), BF)
    return pl.pallas_call(
        body, name="mix_ret_bwd", grid=(SEQ // tm,),
        in_specs=[row, gate(C_AR), gate(C_AR + half), gate(C_AS), gate(C_AS + half), row, row,
                  pl.BlockSpec((tm, D_MODEL), lambda i: (i, C_GR // D_MODEL)), row, _resident((1, D_MODEL)),
                  _resident((D_MODEL, D_MODEL)), _resident(w_sb4.shape), _resident((D_MODEL, D_MODEL))],
        out_specs=[row, row, pl.BlockSpec((tm, 2 * D_MODEL), lambda i: (i, 0)), pl.BlockSpec((tm, 512), lambda i: (i, 0)),
                   row, row, pl.BlockSpec((8, D_MODEL), lambda i: (0, 0))],
        out_shape=[shp, shp, jax.ShapeDtypeStruct((SEQ, 2 * D_MODEL), BF), jax.ShapeDtypeStruct((SEQ, 512), BF),
                   shp, shp, jax.ShapeDtypeStruct((8, D_MODEL), F32)],
        compiler_params=_cp(("arbitrary",)),
    )(dy, proj, proj, proj, proj, rb, sbp, proj, o_raw, gn_g, w_out, w_sb4, w_ret)


def _dproj_segments(widths):
    wc = D_IN // N_CHIPS
    segs, start = [], 0
    for pi, width in enumerate(widths):
        lo = start
        while lo < start + width:
            j = lo // wc
            hi = min(start + width, (j + 1) * wc)
            segs.append((j, lo - j * wc, pi, lo - start, hi - lo))
            lo = hi
        start += width
    assert start == D_IN
    return segs


def _in_proj_bwd(pieces, x, dh1, vecs, w_in4, updates, dep=None):
    tm = 256
    steps = SEQ // tm
    n, nu = len(pieces), len(updates)
    segs = _dproj_segments([p.shape[1] for p in pieces])

    def body(*refs):
        x_ref, dh1_ref, vec_ref, w_ref = refs[n:n + 4]
        upd_in = refs[n + 4:n + 4 + 4 * nu]
        dx_ref, st_ref = refs[n + 4 + 4 * nu:n + 6 + 4 * nu]
        upd_out = refs[n + 6 + 4 * nu:]
        for u in range(nu):
            w_u, m_u, v_u, g_u = upd_in[4 * u:4 * u + 4]
            go_u, d_u, mo_u, vo_u = upd_out[4 * u:4 * u + 4]
            gg = g_u[...]
            go_u[...] = gg
            d_u[...], mo_u[...], vo_u[...] = _adamw_math(w_u[...], gg, m_u[...], v_u[...])

        @pl.when(pl.program_id(0) == 0)
        def _():
            st_ref[...] = jnp.zeros_like(st_ref)

        dh = jnp.zeros((tm, D_MODEL), F32)
        for j, so, pi, po, width in segs:
            dh = dh + _dot_nt(refs[pi][:, po:po + width], w_ref[j, :, so:so + width])
        xx = x_ref[...]
        r1 = _rms(xx)
        xn = xx * r1
        g1, sc1 = _row(vec_ref, V_G1), _row(vec_ref, V_SC1)
        dxn = dh * g1 * (1.0 + sc1)
        dx_ref[...] = dh1_ref[...] + r1 * (dxn - xn * _rowmean(dxn * xn))
        st_ref[0:1, :] += _colsum(dh)
        st_ref[1:2, :] += _colsum(dh * xn * g1)
        st_ref[2:3, :] += _colsum(dh * xn * (1.0 + sc1))

    row = pl.BlockSpec((tm, D_MODEL), lambda i: (i, 0))
    upd_specs, upd_shapes, upd_args = [], [], []
    for arrays in updates:
        rows, cols = arrays[0].shape
        upd_specs += [pl.BlockSpec((rows // steps, cols), lambda i: (i, 0))] * 4
        upd_shapes += [jax.ShapeDtypeStruct((rows, cols), F32)] * 4
        upd_args += list(arrays)
    body, in_specs, args = _add_dep(
        body, [pl.BlockSpec((tm, p.shape[1]), lambda i: (i, 0)) for p in pieces] + [
            row, row, _resident((16, D_MODEL)), _resident(w_in4.shape)] + upd_specs,
        list(pieces) + [x, dh1, vecs, w_in4] + upd_args, dep)
    outs = pl.pallas_call(
        body, name="in_proj_bwd", grid=(steps,),
        in_specs=in_specs,
        out_specs=[row, pl.BlockSpec((8, D_MODEL), lambda i: (0, 0))] + upd_specs,
        out_shape=[jax.ShapeDtypeStruct((SEQ, D_MODEL), F32), jax.ShapeDtypeStruct((8, D_MODEL), F32)] + upd_shapes,
        compiler_params=_cp(("arbitrary",)),
    )(*args)
    return outs[0], outs[1], [tuple(outs[2 + 4 * u:6 + 4 * u]) for u in range(nu)]


def _grad_w_in(h, pieces, dep=None):
    ta = 512
    n = len(pieces)
    segs = _dproj_segments([p.shape[1] for p in pieces])

    def body(*refs):
        h_ref, o_ref = refs[n], refs[n + 1]
        hh = h_ref[...]
        for j, so, pi, po, width in segs:
            o_ref[j, :, so:so + width] = _dot_tn(hh, refs[pi][:, po:po + width]).astype(BF)

    body, in_specs, args = _add_dep(
        body, [_resident(p.shape) for p in pieces] + [pl.BlockSpec((SEQ, ta), lambda i: (0, i))],
        list(pieces) + [h], dep)
    return pl.pallas_call(
        body, name="grad_w_in", grid=(D_MODEL // ta,),
        in_specs=in_specs,
        out_specs=pl.BlockSpec((N_CHIPS, ta, D_IN // N_CHIPS), lambda i: (0, i, 0)),
        out_shape=jax.ShapeDtypeStruct((N_CHIPS, D_MODEL, D_IN // N_CHIPS), BF),
        compiler_params=_cp(("parallel",)),
    )(*args)


def _weight_grads(jobs, name):
    def grad(in_refs, out_refs):
        out_refs[0][...] = _dot_tn(in_refs[0][...], in_refs[1][...]).astype(BF)

    sets = []
    for a, b, col_sharded in jobs:
        ka, nb_ = a.shape[1], b.shape[1]
        if col_sharded:
            ta, tb = ka, nb_ // N_CHIPS
            ins = [(a, (SEQ, ta), lambda l: (0, 0)), (b, (SEQ, tb), lambda l: (0, l))]
        else:
            ta, tb = ka // N_CHIPS, nb_
            ins = [(a, (SEQ, ta), lambda l: (0, l)), (b, (SEQ, tb), lambda l: (0, 0))]
        sets.append((N_CHIPS, ins, [((N_CHIPS, ta, tb), BF, (None, ta, tb), lambda l: (l, 0, 0))], grad))
    return [o[0] for o in _run_sets(name, [], sets)]


def _rope_constants():
    freq = np.float32(ROPE_BASE) ** (-np.arange(0, 64, 2, dtype=np.float32) / np.float32(64))
    inv = np.tile(freq.astype(np.float32), 4).reshape(1, 128)
    sign = np.tile(np.concatenate([-np.ones(32, np.float32), np.ones(32, np.float32)]), 2).reshape(1, 128)
    return jnp.asarray(inv), jnp.asarray(sign)


def _log_gamma():
    return jnp.asarray(np.log1p(-(2.0 ** (-5.0 - np.arange(8, dtype=np.float64)))).astype(np.float32))


def _halves(g):
    return g.reshape(N_CHIPS, 2, g.shape[1] // 2, g.shape[2])


def kernel(x, c, positions, ada_w, ada_b, pre_mix_g, post_mix_g, pre_ffn_g, post_ffn_g, w_in, ret_gn_g, w_ret_branch, w_sb_branch, w_out, w_ff1, w_ff2, loss_target, m_ada_w, m_ada_b, m_pre_mix_g, m_post_mix_g, m_pre_ffn_g, m_post_ffn_g, m_w_in, m_ret_gn_g, m_w_ret_branch, m_w_sb_branch, m_w_out, m_w_ff1, m_w_ff2, v_ada_w, v_ada_b, v_pre_mix_g, v_post_mix_g, v_pre_ffn_g, v_post_ffn_g, v_w_in, v_ret_gn_g, v_w_ret_branch, v_w_sb_branch, v_w_out, v_w_ff1, v_w_ff2):
    names = ["w_in", "w_ret", "w_sb", "w_out", "w_ff1", "w_ff2"]
    big = dict(zip(names, [w_in, w_ret_branch, w_sb_branch, w_out, w_ff1, w_ff2]))
    big_m = dict(zip(names, [m_w_in, m_w_ret_branch, m_w_sb_branch, m_w_out, m_w_ff1, m_w_ff2]))
    big_v = dict(zip(names, [v_w_in, v_w_ret_branch, v_w_sb_branch, v_w_out, v_w_ff1, v_w_ff2]))
    rest = names[1:]
    cidx = lax.axis_index("c").astype(jnp.int32).reshape(1)
    kidx = (2 * lax.axis_index("x") + lax.axis_index("y")).astype(jnp.int32).reshape(1)
    x0, target = x[0], loss_target[0]

    buf_in, sem_in, tok_in = _gather_start("gather_in_start", [_cast_bf16(w_in[0], kidx, c, "cast_w_in")])
    rest_bufs = [_cast_bf16(big[nm][0], kidx, tok_in, "cast_" + nm) for nm in rest]
    inv_freq, sign = _rope_constants()
    lg = _log_gamma()
    cos, sin_s = _rope_tables(positions.reshape(SEQ, 1), inv_freq, sign, dep=tok_in)

    def table(b6, g5):
        return jnp.concatenate([b6.reshape(6, D_MODEL)] + g5 + [jnp.zeros((5, D_MODEL), F32)], axis=0)

    wsm = table(ada_b, [pre_mix_g, post_mix_g, pre_ffn_g, post_ffn_g, ret_gn_g])
    msm = table(m_ada_b, [m_pre_mix_g, m_post_mix_g, m_pre_ffn_g, m_post_ffn_g, m_ret_gn_g])
    vsm = table(v_ada_b, [v_pre_mix_g, v_post_mix_g, v_pre_ffn_g, v_post_ffn_g, v_ret_gn_g])
    c_all, mod4 = _mod_exchange(c, ada_w[0], ada_b.reshape(N_CHIPS, -1), rest_bufs + [cos, wsm, msm, vsm])
    vecs = jnp.concatenate([mod4.reshape(6, D_MODEL), pre_mix_g, post_mix_g, pre_ffn_g, post_ffn_g,
                            jnp.zeros((6, D_MODEL), F32)], axis=0)
    buf_in, sem_in, tok_in = _gather_pass("gather_in_pass", buf_in, sem_in, vecs)
    buf_rest, sem_rest, tok_rest = _gather_start("gather_rest_start", rest_bufs, after=tok_in)
    (w_in4,) = _gather_finish("gather_in_finish", buf_in, sem_in, tok_rest)

    h, proj = _ln_proj(x0, vecs, w_in4)
    sb, tot = _sb_fwd(proj)
    buf_rest, sem_rest, tok_rest = _gather_pass("gather_rest_pass", buf_rest, sem_rest, sb)
    o_raw, retg, states = _ret_fwd(proj, cos, sin_s, ret_gn_g, lg, dep=tok_rest)
    w_ret4, w_sb4, w_out4, w_ff14, w_ff24 = _gather_finish("gather_rest_finish", buf_rest, sem_rest, retg)
    w_ret = w_ret4.reshape(D_MODEL, D_MODEL)
    w_out2 = w_out4.reshape(D_MODEL, D_MODEL)
    mixed, rb, sbp, y, h1, h2 = _mix_out(retg, sb, proj, x0, vecs, w_ret, w_sb4, w_out2)

    act, df, du, dh1, dy, st_a, st_b = _ffn(h2, h1, y, target, vecs, w_ff14, w_ff24)
    grads = dict(zip(("w_ff2", "w_ff1"), _weight_grads([(act, df, False), (h2, du, True)], "grad_w_ff")))
    drb, dsbp, da, dsb, dret, dg_r, st_c = _mix_ret_bwd(dy, proj, rb, sbp, o_raw, ret_gn_g, w_out2, w_sb4, w_ret)
    grads.update(zip(("w_out", "w_ret", "w_sb"), _weight_grads(
        [(mixed, dy, False), (retg, drb, False), (sb, dsbp, True)], "grad_w_branches")))

    bufs, sems, tok = _pair_send_start("rs_rest_pair_send", [_halves(grads[nm]) for nm in rest])
    dqkv_r = _ret_bwd(proj, cos, sin_s, dret, states, lg, dep=tok)
    mine, theirs = _pair_send_wait("rs_rest_pair_recv", bufs, sems, dqkv_r)
    pair_sums = _pair_add_all(mine, theirs, cidx, "pair_add_rest")
    bufs, sems, tok = _chip_send_start("rs_rest_chip_send", pair_sums)
    dq_s, dk_s, dv_s = _sb_bwd(proj, dsb, tot, dep=tok)
    dproj = [dqkv_r, dg_r, dq_s, dk_s, dv_s, da]
    g_in = _grad_w_in(h, dproj)

    bufs_in, sems_in, tok_in = _pair_send_start("rs_in_pair_send", [_halves(g_in)])
    own, parts = _chip_send_wait("rs_rest_chip_recv", bufs, sems, tok_in)
    sums = _chip_add_all(own, parts, kidx, cidx, "chip_add_rest")
    bufs, sems, tok = _pair_swap_start("rs_rest_pair_swap", sums)
    mine, theirs = _pair_send_wait("rs_in_pair_recv", bufs_in, sems_in, tok)
    bufs_in, sems_in, tok_in = _chip_send_start(
        "rs_in_chip_send", [_pair_add(mine[0], theirs[0], cidx, "pair_add_w_in")])
    full_rest = dict(zip(rest, _pair_swap_wait("rs_rest_pair_swapped", bufs, sems, tok_in)))
    small_w = ("w_out", "w_sb", "w_ret")
    out = dict(zip(small_w, _adamw_all(
        [(big[nm][0], big_m[nm][0], big_v[nm][0], full_rest[nm].reshape(big[nm].shape[1:])) for nm in small_w],
        "adamw_small", tok_in)))
    bufs, sems, tok = bufs_in, sems_in, out["w_ret"][1]
    riding = ("w_ff2", "w_ff1")
    dx, st_d, updated = _in_proj_bwd(
        dproj, x0, dh1, vecs, w_in4,
        [(big[nm][0], big_m[nm][0], big_v[nm][0], full_rest[nm].reshape(big[nm].shape[1:])) for nm in riding],
        dep=tok)
    out.update(zip(riding, updated))

    a_, b_, c_, d_ = range(4)
    payload_rows = [(d_, 0), (d_, 1), (b_, 3), (b_, 0), (b_, 1), (a_, 0),
                    (d_, 2), (b_, 4), (b_, 2), (a_, 1), (c_, 0), (a_, 2)]
    g_ada, loss, small = _small_exchange([st_a, st_b, st_c, st_d], payload_rows, c_all, wsm, msm, vsm)
    own, parts = _chip_send_wait("rs_in_chip_recv", bufs, sems, g_ada)
    bufs, sems, tok = _pair_swap_start(
        "rs_in_pair_swap", [_chip_add(own[0], parts[0], kidx, cidx, "chip_add_w_in")])
    ada_out = _adamw(ada_w[0], m_ada_w[0], v_ada_w[0], g_ada, "adamw_ada_w", dep=tok)
    (full_in,) = _pair_swap_wait("rs_in_pair_swapped", bufs, sems, ada_out[1])
    out["w_in"] = _adamw(w_in[0], m_w_in[0], v_w_in[0], full_in.reshape(w_in.shape[1:]), "adamw_w_in")

    def ordered(which):
        sm = small[which]
        bg = [out[nm][which][None] for nm in names]
        return [ada_out[which][None], sm[0], sm[1], sm[2], sm[3], sm[4], bg[0], sm[5]] + bg[1:]

    return (loss.reshape(()), dx[None], *ordered(0), *ordered(1), *ordered(2), *ordered(3))
```

```python
import functools

import numpy as np
import jax
import jax.numpy as jnp
from jax import lax
from jax.experimental import pallas as pl
from jax.experimental.pallas import tpu as pltpu

SEQ = 2048
D_MODEL = 1024
D_IN = 6656
D_FF = 4096
N_CHIPS = 4
EPS = 1e-6
ROPE_BASE = 10000.0
RET_BLOCK = 256
RET_CHUNK_SHIFT = 6
SB_BLOCK = 256
QK_SCALE = 0.125
LOG2E = 1.4426950408889634
N_PAIRS = 4
SB_GROUP = 4

ADAM_LR = 0.001
ADAM_B1 = 0.9
ADAM_B2 = 0.999
ADAM_EPS = 1e-08
ADAM_WD = 0.01
ADAM_STEP = 10

BF = jnp.bfloat16
F32 = jnp.float32
MESH = pl.DeviceIdType.MESH
VMEM_LIMIT = 56 * 1024 * 1024
ANY = pl.BlockSpec(memory_space=pl.ANY)

C_QR, C_KR, C_VR, C_GR, C_QS, C_KS, C_VS, C_AR, C_AS = 0, 512, 1024, 2048, 3072, 3584, 4096, 4608, 5632

V_SH1, V_SC1, V_GT1, V_SH2, V_SC2, V_GT2, V_G1, V_G2, V_G3, V_G4 = range(10)
P_DSH1, P_DSC1, P_DGT1, P_DSH2, P_DSC2, P_DGT2, P_DG1, P_DG2, P_DG3, P_DG4, P_DGN, P_LOSS = range(12)
N_PAY = 12


def _cp(sem=None, **kw):
    if sem is not None:
        kw["dimension_semantics"] = sem
    return pltpu.CompilerParams(vmem_limit_bytes=VMEM_LIMIT, **kw)


def _dot(a, b):
    return jnp.dot(a, b, preferred_element_type=F32)


def _dot_nt(a, b):
    return lax.dot_general(a, b, (((1,), (1,)), ((), ())), preferred_element_type=F32)


def _dot_tn(a, b):
    return lax.dot_general(a, b, (((0,), (0,)), ((), ())), preferred_element_type=F32)


def _row(ref, i):
    return ref[i:i + 1, :]


def _rms(v):
    return lax.rsqrt(jnp.mean(v * v, axis=1, keepdims=True) + EPS)


def _colsum(v):
    return jnp.sum(v, axis=0, keepdims=True)


def _rowmean(v):
    return jnp.mean(v, axis=1, keepdims=True)


def _sigmoid(v):
    return 1.0 / (1.0 + jnp.exp(-v))


def _cast_bf16(w, kidx, dep, name):
    rows, cols = w.shape
    tr = min(rows, 512)

    def body(k_ref, w_ref, dep_ref, o_ref):
        o_ref[...] = w_ref[...].astype(BF)

    return pl.pallas_call(
        body, name=name,
        grid_spec=pltpu.PrefetchScalarGridSpec(
            num_scalar_prefetch=1, grid=(rows // tr,),
            in_specs=[pl.BlockSpec((tr, cols), lambda i, k_ref: (i, 0)), ANY],
            out_specs=pl.BlockSpec((None, tr, cols), lambda i, k_ref: (k_ref[0], i, 0))),
        out_shape=jax.ShapeDtypeStruct((N_CHIPS, rows, cols), BF),
        compiler_params=_cp(("parallel",)),
    )(kidx, w, dep)


def _adamw_math(w, g, m, v):
    m = ADAM_B1 * m + (1.0 - ADAM_B1) * g
    v = ADAM_B2 * v + (1.0 - ADAM_B2) * (g * g)
    m_hat = m / (1.0 - ADAM_B1 ** ADAM_STEP)
    v_hat = v / (1.0 - ADAM_B2 ** ADAM_STEP)
    delta = -ADAM_LR * (m_hat / (jnp.sqrt(v_hat) + ADAM_EPS) + ADAM_WD * w)
    return delta, m, v


def _adamw(w, m, v, g, name, dep=None):
    rows, cols = w.shape
    tr = min(rows, 256)

    def body(w_ref, m_ref, v_ref, g_ref, go_ref, d_ref, mo_ref, vo_ref):
        gg = g_ref[...]
        d, mm, vv = _adamw_math(w_ref[...], gg, m_ref[...], v_ref[...])
        go_ref[...] = gg
        d_ref[...] = d
        mo_ref[...] = mm
        vo_ref[...] = vv

    spec = pl.BlockSpec((tr, cols), lambda i: (i, 0))
    shp = jax.ShapeDtypeStruct((rows, cols), F32)
    body, in_specs, args = _add_dep(body, [spec] * 4, [w, m, v, g], dep)
    return pl.pallas_call(
        body, name=name, grid=(rows // tr,),
        in_specs=in_specs, out_specs=[spec] * 4, out_shape=[shp] * 4,
        compiler_params=_cp(("parallel",)),
    )(*args)


def _place():
    x, y, c = lax.axis_index("x"), lax.axis_index("y"), lax.axis_index("c")
    return x, y, c


HBM = pl.BlockSpec(memory_space=pltpu.HBM)
SEM = pl.BlockSpec(memory_space=pltpu.SEMAPHORE)
EFFECT = pltpu.SideEffectType.DATAFLOW_SIDE_EFFECTING


def _add_dep(body, in_specs, args, dep):
    if dep is None:
        return body, list(in_specs), list(args)
    n = len(args)

    def wrapped(*refs):
        body(*refs[:n], *refs[n + 1:])

    return wrapped, list(in_specs) + [ANY], list(args) + [dep]


def _split_call(name, bufs, run, old=None, after=None, new=0):
    nb = len(bufs)
    n_old = 2 if old is not None else 0
    n_in = nb + n_old + (1 if after is not None else 0)

    def body(*refs):
        old_sems = (refs[nb], refs[nb + 1]) if old is not None else None
        new_sems = (refs[n_in], refs[n_in + 1]) if new else None
        run(refs[:nb], old_sems, new_sems)
        if new:
            refs[-1][...] = jnp.zeros_like(refs[-1])

    in_specs = [HBM] * nb + [SEM] * n_old + ([ANY] if after is not None else [])
    out_shape = [pltpu.SemaphoreType.DMA((new,))] * 2 if new else []
    out_specs = [SEM, SEM] if new else []
    out_shape += [pltpu.HBM(b.shape, b.dtype) for b in bufs]
    out_specs += [HBM] * nb
    if new:
        out_shape.append(jax.ShapeDtypeStruct((8, 128), F32))
        out_specs.append(pl.BlockSpec(memory_space=pltpu.VMEM))
    first = 2 if new else 0
    args = [pltpu.with_memory_space_constraint(b, pltpu.HBM) for b in bufs]
    if old is not None:
        args += [old[0], old[1]]
    if after is not None:
        args.append(after)
    outs = pl.pallas_call(
        body, name=name, in_specs=tuple(in_specs), out_specs=tuple(out_specs), out_shape=tuple(out_shape),
        input_output_aliases={i: i + first for i in range(nb)},
        compiler_params=pltpu.CompilerParams(has_side_effects=EFFECT),
    )(*args)
    thru = list(outs[first:first + nb])
    if new:
        return thru, (outs[0], outs[1]), outs[-1]
    return thru, None, None


def _remote(part_src, part_dst, sems, i, to):
    return pltpu.make_async_remote_copy(src_ref=part_src, dst_ref=part_dst, send_sem=sems[0].at[i],
                                        recv_sem=sems[1].at[i], device_id=to, device_id_type=MESH)


def _other_chips(x, y):
    return [(1 - x, y), (x, 1 - y), (1 - x, 1 - y)]


def _gather_start(name, bufs, after=None):
    def run(refs, old, new):
        x, y, c = _place()
        k = 2 * x + y
        for w, ref in enumerate(refs):
            rh = bufs[w].shape[1] // 2
            part = ref.at[k, pl.ds(c * rh, rh)]
            for j, (cx, cy) in enumerate(_other_chips(x, y)):
                _remote(part, part, new, 3 * w + j, (cx, cy, c)).start()

    return _split_call(name, bufs, run, after=after, new=3 * len(bufs))


def _gather_pass(name, bufs, sems, after):
    def run(refs, old, new):
        x, y, c = _place()
        k = 2 * x + y
        sib = (x, y, 1 - c)
        for w, ref in enumerate(refs):
            rh = bufs[w].shape[1] // 2
            for j, (cx, cy) in enumerate(_other_chips(x, y)):
                land = ref.at[2 * cx + cy, pl.ds(c * rh, rh)]
                _remote(land, land, old, 3 * w + j, (cx, cy, c)).wait_recv()
                _remote(land, land, new, 3 * w + j, sib).start()
        for w, ref in enumerate(refs):
            rh = bufs[w].shape[1] // 2
            part = ref.at[k, pl.ds(c * rh, rh)]
            for j, (cx, cy) in enumerate(_other_chips(x, y)):
                _remote(part, part, old, 3 * w + j, (cx, cy, c)).wait_send()

    return _split_call(name, bufs, run, old=sems, after=after, new=3 * len(bufs))


def _gather_finish(name, bufs, sems, after):
    def run(refs, old, new):
        x, y, c = _place()
        sib = (x, y, 1 - c)
        for w, ref in enumerate(refs):
            rh = bufs[w].shape[1] // 2
            for j, (cx, cy) in enumerate(_other_chips(x, y)):
                sent = ref.at[2 * cx + cy, pl.ds(c * rh, rh)]
                _remote(sent, sent, old, 3 * w + j, sib).wait_send()
                land = ref.at[2 * cx + cy, pl.ds((1 - c) * rh, rh)]
                _remote(land, land, old, 3 * w + j, sib).wait_recv()

    return _split_call(name, bufs, run, old=sems, after=after)[0]


def _pair_send_start(name, grads):
    n = len(grads)
    lands = [lax.empty((N_CHIPS,) + g.shape[2:], g.dtype) for g in grads]

    def run(refs, old, new):
        x, y, c = _place()
        for w in range(n):
            _remote(refs[w].at[:, 1 - c], refs[n + w], new, w, (x, y, 1 - c)).start()

    return _split_call(name, list(grads) + lands, run, new=n)


def _pair_send_wait(name, bufs, sems, after):
    n = len(bufs) // 2

    def run(refs, old, new):
        x, y, c = _place()
        for w in range(n):
            cp = _remote(refs[w].at[:, 1 - c], refs[n + w], old, w, (x, y, 1 - c))
            cp.wait_send()
            cp.wait_recv()

    thru = _split_call(name, bufs, run, old=sems, after=after)[0]
    return thru[:n], thru[n:]


def _run_sets(name, scalars, sets, deps=()):
    starts = np.concatenate([[0], np.cumsum([s[0] for s in sets])]).tolist()

    def spec(block, index, lo, n):
        return pl.BlockSpec(block, lambda i, *sc: index(jnp.clip(i - lo, 0, n - 1), *sc))

    in_specs, out_specs, out_shape, args = [], [], [], []
    for (n, ins, outs, _), lo in zip(sets, starts):
        for array, block, index in ins:
            in_specs.append(spec(block, index, lo, n))
            args.append(array)
        for shape, dtype, block, index in outs:
            out_specs.append(spec(block, index, lo, n))
            out_shape.append(jax.ShapeDtypeStruct(shape, dtype))

    def body(*refs):
        refs = refs[len(scalars):]
        n_in = len(in_specs)
        i = pl.program_id(0)
        pos_in, pos_out = 0, n_in + len(deps)
        for (n, ins, outs, fn), lo in zip(sets, starts):
            in_refs = refs[pos_in:pos_in + len(ins)]
            out_refs = refs[pos_out:pos_out + len(outs)]
            pos_in += len(ins)
            pos_out += len(outs)
            pl.when((i >= lo) & (i < lo + n))(functools.partial(fn, in_refs, out_refs))

    outs = pl.pallas_call(
        body, name=name,
        grid_spec=pltpu.PrefetchScalarGridSpec(
            num_scalar_prefetch=len(scalars), grid=(starts[-1],), in_specs=in_specs + [ANY] * len(deps),
            out_specs=out_specs),
        out_shape=out_shape,
        compiler_params=_cp(("arbitrary",)),
    )(*scalars, *args, *deps)
    result, pos = [], 0
    for _, _, outs_s, _ in sets:
        result.append(list(outs[pos:pos + len(outs_s)]))
        pos += len(outs_s)
    return result


def _adamw_all(jobs, name, dep):
    def update(in_refs, out_refs):
        gg = in_refs[3][...]
        out_refs[0][...] = gg
        out_refs[1][...], out_refs[2][...], out_refs[3][...] = _adamw_math(
            in_refs[0][...], gg, in_refs[1][...], in_refs[2][...])

    whole = lambda l: (0, 0)
    sets = [(1, [(a, a.shape, whole) for a in job], [(job[0].shape, F32, job[0].shape, whole)] * 4, update)
            for job in jobs]
    return [tuple(o) for o in _run_sets(name, [], sets, deps=[dep])]


def _pair_add_all(gs, recvs, cidx, name):
    def add(in_refs, out_refs):
        out_refs[0][...] = (in_refs[0][...].astype(F32) + in_refs[1][...].astype(F32)).astype(BF)

    sets = []
    for g, r in zip(gs, recvs):
        _, _, rh, cols = g.shape
        tr = min(rh, 256)
        sets.append((rh // tr,
                     [(g, (N_CHIPS, None, tr, cols), lambda l, c_ref: (0, c_ref[0], l, 0)),
                      (r, (N_CHIPS, tr, cols), lambda l, c_ref: (0, l, 0))],
                     [((N_CHIPS, rh, cols), BF, (N_CHIPS, tr, cols), lambda l, c_ref: (0, l, 0))], add))
    return [o[0] for o in _run_sets(name, [cidx], sets)]


def _chip_add_all(owns, parts, kidx, cidx, name):
    def add(in_refs, out_refs):
        acc = in_refs[0][...].astype(F32)
        for s in range(3):
            acc = acc + in_refs[1][s].astype(F32)
        out_refs[0][...] = acc

    sets = []
    for own, p in zip(owns, parts):
        _, rh, cols = p.shape
        tr = min(rh, 256)
        sets.append((rh // tr,
                     [(own, (None, tr, cols), lambda l, k_ref, c_ref: (k_ref[0], l, 0)),
                      (p, (3, tr, cols), lambda l, k_ref, c_ref: (0, l, 0))],
                     [((2, rh, cols), F32, (None, tr, cols), lambda l, k_ref, c_ref: (c_ref[0], l, 0))], add))
    return [o[0] for o in _run_sets(name, [kidx, cidx], sets)]


def _pair_add(g, recv, cidx, name):
    _, _, rh, cols = g.shape
    tr = min(rh, 256)

    def body(c_ref, g_ref, r_ref, o_ref):
        o_ref[...] = (g_ref[...].astype(F32) + r_ref[...].astype(F32)).astype(BF)

    return pl.pallas_call(
        body, name=name,
        grid_spec=pltpu.PrefetchScalarGridSpec(
            num_scalar_prefetch=1, grid=(rh // tr,),
            in_specs=[pl.BlockSpec((N_CHIPS, None, tr, cols), lambda i, c_ref: (0, c_ref[0], i, 0)),
                      pl.BlockSpec((N_CHIPS, tr, cols), lambda i, c_ref: (0, i, 0))],
            out_specs=pl.BlockSpec((N_CHIPS, tr, cols), lambda i, c_ref: (0, i, 0))),
        out_shape=jax.ShapeDtypeStruct((N_CHIPS, rh, cols), BF),
        compiler_params=_cp(("parallel",)),
    )(cidx, g, recv)


def _chip_send_start(name, sums):
    n = len(sums)
    lands = [lax.empty((3,) + s.shape[1:], BF) for s in sums]

    def run(refs, old, new):
        x, y, c = _place()
        for w in range(n):
            for j, (cx, cy) in enumerate(_other_chips(x, y)):
                _remote(refs[w].at[2 * cx + cy], refs[n + w].at[j], new, 3 * w + j, (cx, cy, c)).start()

    return _split_call(name, list(sums) + lands, run, new=3 * n)


def _chip_send_wait(name, bufs, sems, after):
    n = len(bufs) // 2

    def run(refs, old, new):
        x, y, c = _place()
        for w in range(n):
            for j, (cx, cy) in enumerate(_other_chips(x, y)):
                cp = _remote(refs[w].at[2 * cx + cy], refs[n + w].at[j], old, 3 * w + j, (cx, cy, c))
                cp.wait_send()
                cp.wait_recv()

    thru = _split_call(name, bufs, run, old=sems, after=after)[0]
    return thru[:n], thru[n:]


def _chip_add(own, parts, kidx, cidx, name):
    _, rh, cols = parts.shape
    tr = min(rh, 512)

    def body(k_ref, c_ref, own_ref, p_ref, o_ref):
        acc = own_ref[...].astype(F32)
        for s in range(3):
            acc = acc + p_ref[s].astype(F32)
        o_ref[...] = acc

    return pl.pallas_call(
        body, name=name,
        grid_spec=pltpu.PrefetchScalarGridSpec(
            num_scalar_prefetch=2, grid=(rh // tr,),
            in_specs=[pl.BlockSpec((None, tr, cols), lambda i, k_ref, c_ref: (k_ref[0], i, 0)),
                      pl.BlockSpec((3, tr, cols), lambda i, k_ref, c_ref: (0, i, 0))],
            out_specs=pl.BlockSpec((None, tr, cols), lambda i, k_ref, c_ref: (c_ref[0], i, 0))),
        out_shape=jax.ShapeDtypeStruct((2, rh, cols), F32),
        compiler_params=_cp(("parallel",)),
    )(kidx, cidx, own, parts)


def _pair_swap_start(name, bufs):
    def run(refs, old, new):
        x, y, c = _place()
        for w, ref in enumerate(refs):
            _remote(ref.at[c], ref.at[c], new, w, (x, y, 1 - c)).start()

    return _split_call(name, bufs, run, new=len(bufs))


def _pair_swap_wait(name, bufs, sems, after):
    def run(refs, old, new):
        x, y, c = _place()
        for w, ref in enumerate(refs):
            _remote(ref.at[c], ref.at[c], old, w, (x, y, 1 - c)).wait_send()
            _remote(ref.at[1 - c], ref.at[1 - c], old, w, (x, y, 1 - c)).wait_recv()

    return _split_call(name, bufs, run, old=sems, after=after)[0]


def _peers(x, y, c):
    out = []
    for code in range(1, 8):
        fx, fy, fc = (code >> 2) & 1, (code >> 1) & 1, code & 1
        px = 1 - x if fx else x
        py = 1 - y if fy else y
        pc = 1 - c if fc else c
        out.append((code, (px, py, pc)))
    return out


def _mod_exchange(c_row, ada_w, ada_b4, deps):
    ncol = ada_w.shape[1]

    def body(c_ref, w_ref, b_ref, *rest):
        call_ref, mod_ref, part_ref, send_sems, recv_sems = rest[len(deps):]
        x, y, c = _place()
        k = 2 * x + y
        me = 4 * x + 2 * y + c
        call_ref[pl.ds(me, 1), :] = c_ref[...]
        sends = []
        for code, peer in _peers(x, y, c):
            cp = pltpu.make_async_remote_copy(
                src_ref=c_ref, dst_ref=call_ref.at[pl.ds(me, 1), :],
                send_sem=send_sems.at[code], recv_sem=recv_sems.at[code],
                device_id=peer, device_id_type=MESH)
            cp.start()
            sends.append(cp)
        for code, (px, py, pc) in _peers(x, y, c):
            land = call_ref.at[pl.ds(4 * px + 2 * py + pc, 1), :]
            pltpu.make_async_remote_copy(
                src_ref=land, dst_ref=land, send_sem=send_sems.at[code], recv_sem=recv_sems.at[code],
                device_id=(px, py, pc), device_id_type=MESH).wait_recv()
        call = call_ref[...]
        act = call * _sigmoid(call)
        part = jnp.dot(act, w_ref[...], preferred_element_type=F32,
                       precision=lax.Precision.HIGHEST) + b_ref[pl.ds(k, 1), :]
        part_ref[...] = part
        mod_ref[pl.ds(k, 1), :] = part_ref[pl.ds(me, 1), :]
        chips = [(8 + j, peer) for j, (code, peer) in enumerate(_peers(x, y, c)) if code in (2, 4, 6)]
        for slot, (px, py, pc) in chips:
            cp = pltpu.make_async_remote_copy(
                src_ref=part_ref.at[pl.ds(4 * px + 2 * py + pc, 1), :], dst_ref=mod_ref.at[pl.ds(k, 1), :],
                send_sem=send_sems.at[slot], recv_sem=recv_sems.at[slot],
                device_id=(px, py, pc), device_id_type=MESH)
            cp.start()
            sends.append(cp)
        for slot, (px, py, pc) in chips:
            land = mod_ref.at[pl.ds(2 * px + py, 1), :]
            pltpu.make_async_remote_copy(
                src_ref=land, dst_ref=land, send_sem=send_sems.at[slot], recv_sem=recv_sems.at[slot],
                device_id=(px, py, pc), device_id_type=MESH).wait_recv()
        for cp in sends:
            cp.wait_send()

    vm = pl.BlockSpec(memory_space=pltpu.VMEM)
    return pl.pallas_call(
        body, name="mod_exchange",
        in_specs=[vm, vm, vm] + [ANY] * len(deps), out_specs=[vm, vm],
        out_shape=[jax.ShapeDtypeStruct((8, D_MODEL), F32), jax.ShapeDtypeStruct((N_CHIPS, ncol), F32)],
        scratch_shapes=[pltpu.VMEM((8, ncol), F32), pltpu.SemaphoreType.DMA((16,)),
                        pltpu.SemaphoreType.DMA((16,))],
        compiler_params=_cp(),
    )(c_row, ada_w, ada_b4, *deps)


def _small_exchange(stats, rows, c_all, wsm, msm, vsm):
    ncol = 6 * D_MODEL // N_CHIPS
    ns = len(stats)

    def body(*refs):
        call_ref, w_ref, m_ref, v_ref, gw_ref, loss_ref = refs[ns:ns + 6]
        outs = refs[ns + 6:ns + 30]
        p_ref, g_ref, all_ref, dm_ref, send_sems, recv_sems = refs[ns + 30:]
        x, y, c = _place()
        k = 2 * x + y
        me = 4 * x + 2 * y + c
        for r, (tab, row) in enumerate(rows):
            p_ref[r] = refs[tab][row:row + 1, :]
        all_ref[:, pl.ds(me, 1), :] = p_ref[...]
        sends = []
        for code, peer in _peers(x, y, c):
            cp = pltpu.make_async_remote_copy(
                src_ref=p_ref, dst_ref=all_ref.at[:, pl.ds(me, 1), :],
                send_sem=send_sems.at[code], recv_sem=recv_sems.at[code],
                device_id=peer, device_id_type=MESH)
            cp.start()
            sends.append(cp)
        for code, (px, py, pc) in _peers(x, y, c):
            land = all_ref.at[:, pl.ds(4 * px + 2 * py + pc, 1), :]
            pltpu.make_async_remote_copy(
                src_ref=land, dst_ref=land, send_sem=send_sems.at[code], recv_sem=recv_sems.at[code],
                device_id=(px, py, pc), device_id_type=MESH).wait_recv()
        for cp in sends:
            cp.wait_send()
        tot = [_colsum(all_ref[r]) for r in range(N_PAY)]
        loss_ref[...] = jnp.sum(tot[P_LOSS], axis=1, keepdims=True)
        g_ref[...] = jnp.zeros_like(g_ref)
        for r in range(P_LOSS):
            g_ref[r:r + 1, :] = tot[r]
        g = g_ref[...]
        for kind, tab in enumerate((g,) + _adamw_math(w_ref[...], g, m_ref[...], v_ref[...])):
            for r in range(6):
                outs[6 * kind][:, r * D_MODEL:(r + 1) * D_MODEL] = tab[r:r + 1, :]
            for i in range(5):
                outs[6 * kind + 1 + i][...] = tab[6 + i:7 + i, :]
        half = D_MODEL // 2
        for kk in range(N_CHIPS):
            @pl.when(k == kk)
            def _():
                r0 = 3 * (kk // 2)
                if kk % 2 == 0:
                    dm_ref[:, :D_MODEL] = all_ref[r0]
                    dm_ref[:, D_MODEL:] = all_ref[r0 + 1][:, :half]
                else:
                    dm_ref[:, :half] = all_ref[r0 + 1][:, half:]
                    dm_ref[:, half:] = all_ref[r0 + 2]
        call = call_ref[...]
        act = call * _sigmoid(call)
        gw_ref[...] = lax.dot_general(act, dm_ref[...], (((0,), (0,)), ((), ())),
                                      preferred_element_type=F32, precision=lax.Precision.HIGHEST)

    vm = pl.BlockSpec(memory_space=pltpu.VMEM)
    vectors = [jax.ShapeDtypeStruct((1, 6 * D_MODEL), F32)] + [jax.ShapeDtypeStruct((1, D_MODEL), F32)] * 5
    outs = pl.pallas_call(
        body, name="small_exchange",
        in_specs=[vm] * (ns + 4), out_specs=[vm] * 26,
        out_shape=[jax.ShapeDtypeStruct((D_MODEL, ncol), F32), jax.ShapeDtypeStruct((1, 1), F32)] + vectors * 4,
        scratch_shapes=[pltpu.VMEM((N_PAY, 1, D_MODEL), F32), pltpu.VMEM((16, D_MODEL), F32),
                        pltpu.VMEM((N_PAY, 8, D_MODEL), F32), pltpu.VMEM((8, ncol), F32),
                        pltpu.SemaphoreType.DMA((8,)), pltpu.SemaphoreType.DMA((8,))],
        compiler_params=_cp(),
    )(*stats, c_all, wsm, msm, vsm)
    loss, *small = _pass_on(outs[1:])
    return outs[0], loss, [small[6 * kind:6 + 6 * kind] for kind in range(4)]


def _pass_on(arrays):
    n = len(arrays)

    def body(*refs):
        for i in range(n):
            refs[n + i][...] = refs[i][...]

    return pl.pallas_call(
        body, name="small_outputs", out_shape=[jax.ShapeDtypeStruct(a.shape, a.dtype) for a in arrays],
        compiler_params=_cp(),
    )(*arrays)


def _rope_tables(pos_col, inv_freq, sign, dep=None):
    def body(p_ref, f_ref, s_ref, cos_ref, sin_ref):
        ang = p_ref[...].astype(F32) * f_ref[...]
        cos_ref[...] = jnp.cos(ang)
        sin_ref[...] = jnp.sin(ang) * s_ref[...]

    tr = 512
    shp = jax.ShapeDtypeStruct((SEQ, 128), F32)
    body, in_specs, args = _add_dep(
        body, [pl.BlockSpec((tr, 1), lambda i: (i, 0)), pl.BlockSpec((1, 128), lambda i: (0, 0)),
               pl.BlockSpec((1, 128), lambda i: (0, 0))], [pos_col, inv_freq, sign], dep)
    return pl.pallas_call(
        body, name="rope_tables", grid=(SEQ // tr,),
        in_specs=in_specs,
        out_specs=[pl.BlockSpec((tr, 128), lambda i: (i, 0))] * 2, out_shape=[shp, shp],
        compiler_params=_cp(("parallel",)),
    )(*args)


def _resident(shape):
    nd = len(shape)
    return pl.BlockSpec(shape, lambda *_: (0,) * nd, pipeline_mode=pl.Buffered(1))


def _ln_proj(x, vecs, w_in4):
    tm = min(512, SEQ)
    wc = w_in4.shape[2]

    def body(x_ref, vec_ref, w_ref, h_ref, proj_ref):
        xx = x_ref[...]
        g = _row(vec_ref, V_G1) * (1.0 + _row(vec_ref, V_SC1))
        h = (xx * _rms(xx) * g + _row(vec_ref, V_SH1)).astype(BF)
        h_ref[...] = h
        for j in range(N_CHIPS):
            proj_ref[:, j * wc:(j + 1) * wc] = _dot(h, w_ref[j]).astype(BF)

    return pl.pallas_call(
        body, name="ln_proj", grid=(SEQ // tm,),
        in_specs=[pl.BlockSpec((tm, D_MODEL), lambda i: (i, 0)), _resident((16, D_MODEL)),
                  _resident(w_in4.shape)],
        out_specs=[pl.BlockSpec((tm, D_MODEL), lambda i: (i, 0)), pl.BlockSpec((tm, D_IN), lambda i: (i, 0))],
        out_shape=[jax.ShapeDtypeStruct((SEQ, D_MODEL), BF), jax.ShapeDtypeStruct((SEQ, D_IN), BF)],
        compiler_params=_cp(("parallel",)),
    )(x, vecs, w_in4)


def _lane_first(shape):
    lane = lax.broadcasted_iota(jnp.int32, shape, 1)
    return (lane & 32) == 0


def _rot(v, cos, sin_s):
    partner = jnp.where(_lane_first(v.shape), pltpu.roll(v, 96, 1), pltpu.roll(v, 32, 1))
    return v * cos + partner * sin_s


def _rot_t(dv, cos, sin_s):
    t = dv * sin_s
    partner = jnp.where(_lane_first(dv.shape), pltpu.roll(t, 96, 1), pltpu.roll(t, 32, 1))
    return dv * cos + partner


def _ret_mask(lg):
    t = RET_BLOCK
    ii = lax.broadcasted_iota(jnp.int32, (t, t), 0)
    jj = lax.broadcasted_iota(jnp.int32, (t, t), 1)
    dist = jnp.abs(ii - jj).astype(F32)
    future = (jj >> RET_CHUNK_SHIFT) > (ii >> RET_CHUNK_SHIFT)
    return jnp.where(future, 0.0, jnp.exp(lg * dist))


def _ret_masks(lg, mask_ref, head):
    t = RET_BLOCK
    mask = mask_ref[head]
    ti = lax.broadcasted_iota(jnp.int32, (t, 1), 0).astype(F32)
    from_start = jnp.exp(lg * (ti + 1.0))
    to_end = jnp.exp(lg * (t - 1.0 - ti))
    whole = jnp.exp(jnp.full((1, 128), lg * t, F32))
    return mask, from_start, to_end, whole


def _head_lanes(shape, hh):
    lane = lax.broadcasted_iota(jnp.int32, shape, 1)
    return (lane >> 6) == hh


def _ret_specs():
    t = RET_BLOCK
    return dict(
        q=lambda f: pl.BlockSpec((t, 512), lambda n: (f(n), C_QR // 512)),
        k=lambda f: pl.BlockSpec((t, 512), lambda n: (f(n), C_KR // 512)),
        v=lambda f: pl.BlockSpec((t, D_MODEL), lambda n: (f(n), C_VR // D_MODEL)),
        g=lambda f: pl.BlockSpec((t, D_MODEL), lambda n: (f(n), C_GR // D_MODEL)),
        tab=lambda f: pl.BlockSpec((t, 128), lambda n: (f(n), 0)),
        wide=lambda f: pl.BlockSpec((t, D_MODEL), lambda n: (f(n), 0)),
        state=lambda f: pl.BlockSpec((N_PAIRS, None, 2, 128, 128), lambda n: (0, f(n), 0, 0, 0)),
    )


def _ret_fwd(proj, cos, sin_s, gn_g, log_gamma, dep=None):
    t = RET_BLOCK
    nb = SEQ // t

    def body(lg_ref, q_ref, k_ref, v_ref, g_ref, cos_ref, sin_ref, gn_ref, o_ref, retg_ref, st_ref, state, masks):
        @pl.when(pl.program_id(0) == 0)
        def _():
            state[...] = jnp.zeros_like(state)
            for head in range(2 * N_PAIRS):
                masks[head] = _ret_mask(lg_ref[head])

        cos, sn = cos_ref[...], sin_ref[...]
        for p in range(N_PAIRS):
            q = _rot(q_ref[:, 128 * p:128 * (p + 1)].astype(F32), cos, sn)
            k = _rot(k_ref[:, 128 * p:128 * (p + 1)].astype(F32), cos, sn) * QK_SCALE
            for hh in range(2):
                cols = slice(256 * p + 128 * hh, 256 * p + 128 * (hh + 1))
                lg = lg_ref[2 * p + hh]
                mask, from_start, to_end, whole = _ret_masks(lg, masks, 2 * p + hh)
                lanes = _head_lanes(q.shape, hh)
                qm = jnp.where(lanes, q, 0.0)
                km = jnp.where(lanes, k, 0.0)
                vh = v_ref[:, cols]
                sc = _dot_nt(qm.astype(BF), km.astype(BF)) * mask
                st = state[p, hh]
                st_ref[p, hh] = st
                o = _dot(sc.astype(BF), vh) + _dot((qm * from_start).astype(BF), st.astype(BF))
                state[p, hh] = whole * st + _dot_tn((km * to_end).astype(BF), vh)
                d = o - _rowmean(o)
                nh = d * lax.rsqrt(_rowmean(d * d) + EPS)
                gr = g_ref[:, cols].astype(F32)
                o_ref[:, cols] = o
                retg_ref[:, cols] = (gr * _sigmoid(gr) * nh * gn_ref[:, cols]).astype(BF)

    sp = _ret_specs()
    ident = lambda n: n
    body, in_specs, args = _add_dep(
        body, [pl.BlockSpec(memory_space=pltpu.SMEM), sp["q"](ident), sp["k"](ident), sp["v"](ident),
               sp["g"](ident), sp["tab"](ident), sp["tab"](ident), _resident((1, D_MODEL))],
        [log_gamma, proj, proj, proj, proj, cos, sin_s, gn_g], dep)
    return pl.pallas_call(
        body, name="ret_fwd", grid=(nb,),
        in_specs=in_specs,
        out_specs=[sp["wide"](ident), sp["wide"](ident), sp["state"](ident)],
        out_shape=[jax.ShapeDtypeStruct((SEQ, D_MODEL), F32), jax.ShapeDtypeStruct((SEQ, D_MODEL), BF),
                   jax.ShapeDtypeStruct((N_PAIRS, nb, 2, 128, 128), F32)],
        scratch_shapes=[pltpu.VMEM((N_PAIRS, 2, 128, 128), F32),
                        pltpu.VMEM((2 * N_PAIRS, RET_BLOCK, RET_BLOCK), F32)],
        compiler_params=_cp(("arbitrary",)),
    )(*args)


def _ret_bwd(proj, cos, sin_s, dret, states, log_gamma, dep=None):
    t = RET_BLOCK
    nb = SEQ // t

    def body(lg_ref, q_ref, k_ref, v_ref, cos_ref, sin_ref, do_ref, st_ref, dqkv_ref, dstate, masks):
        @pl.when(pl.program_id(0) == 0)
        def _():
            dstate[...] = jnp.zeros_like(dstate)
            for head in range(2 * N_PAIRS):
                masks[head] = _ret_mask(lg_ref[head])

        cos, sn = cos_ref[...], sin_ref[...]
        for p in range(N_PAIRS):
            q = _rot(q_ref[:, 128 * p:128 * (p + 1)].astype(F32), cos, sn)
            k = _rot(k_ref[:, 128 * p:128 * (p + 1)].astype(F32), cos, sn) * QK_SCALE
            dq_rot = jnp.zeros(q.shape, F32)
            dk_rot = jnp.zeros(q.shape, F32)
            for hh in range(2):
                cols = slice(256 * p + 128 * hh, 256 * p + 128 * (hh + 1))
                lg = lg_ref[2 * p + hh]
                mask, from_start, to_end, whole = _ret_masks(lg, masks, 2 * p + hh)
                lanes = _head_lanes(q.shape, hh)
                qm = jnp.where(lanes, q, 0.0)
                km = jnp.where(lanes, k, 0.0)
                qb, kb = qm.astype(BF), km.astype(BF)
                vh = v_ref[:, cols]
                do = do_ref[:, cols]
                sc = (_dot_nt(qb, kb) * mask).astype(BF)
                st = st_ref[p, hh].astype(BF)
                dst = dstate[p, hh]
                dstb = dst.astype(BF)
                k_end = (km * to_end).astype(BF)
                q_start = (qm * from_start).astype(BF)
                dqkv_ref[:, C_VR + 256 * p + 128 * hh:C_VR + 256 * p + 128 * (hh + 1)] = (
                    _dot_tn(sc, do) + _dot(k_end, dstb)).astype(BF)
                dsc = (_dot_nt(do, vh) * mask).astype(BF)
                dq_h = _dot(dsc, kb) + _dot_nt(do, st) * from_start
                dq_rot = dq_rot + jnp.where(lanes, dq_h, 0.0)
                dk_rot = dk_rot + _dot_tn(dsc, qb) + _dot_nt(vh, dstb) * to_end
                dstate[p, hh] = whole * dst + _dot_tn(q_start, do)
            dqkv_ref[:, C_QR + 128 * p:C_QR + 128 * (p + 1)] = _rot_t(dq_rot, cos, sn).astype(BF)
            dqkv_ref[:, C_KR + 128 * p:C_KR + 128 * (p + 1)] = _rot_t(dk_rot * QK_SCALE, cos, sn).astype(BF)

    sp = _ret_specs()
    rev = lambda n: nb - 1 - n
    body, in_specs, args = _add_dep(
        body, [pl.BlockSpec(memory_space=pltpu.SMEM), sp["q"](rev), sp["k"](rev), sp["v"](rev),
               sp["tab"](rev), sp["tab"](rev), sp["wide"](rev), sp["state"](rev)],
        [log_gamma, proj, proj, proj, cos, sin_s, dret, states], dep)
    return pl.pallas_call(
        body, name="ret_bwd", grid=(nb,),
        in_specs=in_specs,
        out_specs=pl.BlockSpec((t, C_GR), lambda n: (rev(n), 0)),
        out_shape=jax.ShapeDtypeStruct((SEQ, C_GR), BF),
        scratch_shapes=[pltpu.VMEM((N_PAIRS, 2, 128, 128), F32),
                        pltpu.VMEM((2 * N_PAIRS, RET_BLOCK, RET_BLOCK), F32)],
        compiler_params=_cp(("arbitrary",)),
    )(*args)


def _stack_heads(v):
    return jnp.concatenate([jnp.where(_head_lanes(v.shape, hh), v, jnp.zeros_like(v)) for hh in range(2)], axis=0)


def _unstack_heads(v):
    t = v.shape[0] // 2
    return jnp.where(_head_lanes((t, v.shape[1]), 0), v[:t], v[t:])


def _sb_masks(t, heads):
    rr = lax.broadcasted_iota(jnp.int32, (t, t), 0)
    cc = lax.broadcasted_iota(jnp.int32, (t, t), 1)
    r2 = lax.broadcasted_iota(jnp.int32, (heads * t, t), 0) & (t - 1)
    c2 = lax.broadcasted_iota(jnp.int32, (heads * t, t), 1)
    return rr, cc, c2 < r2


def _split_dot2(v, tri):
    return _dot(v.astype(BF), tri)


def _log2_sigmoids(z2):
    minus_abs = lax.bitcast_convert_type(
        lax.bitcast_convert_type(z2, jnp.uint32) | jnp.uint32(0x80000000), F32)
    ls = jnp.minimum(z2, 0.0) - jnp.log2(1.0 + jnp.exp2(minus_abs))
    return ls, ls - z2


def _sb_fwd(proj):
    t, g = SB_BLOCK, SB_GROUP
    nq = SEQ // t
    rows = 2 * g * t

    def body(q_ref, k_ref, v_ref, o_ref, tot_ref, kt_ref):
        i = pl.program_id(1)

        @pl.when(i == 0)
        def _():
            for p in range(g):
                for jj in range(nq):
                    kt_ref[p, jj] = k_ref[jj * t:(jj + 1) * t, 128 * p:128 * (p + 1)].T

        q2 = [_stack_heads((q_ref[:, 128 * p:128 * (p + 1)].astype(F32) * QK_SCALE).astype(BF)) for p in range(g)]
        rr, cc, valid = _sb_masks(t, 2 * g)
        later = (rr > cc).astype(BF)

        def tile(j, carry, diagonal):
            acc, run = carry
            z = jnp.concatenate([_dot(q2[p], kt_ref[p, j]) for p in range(g)], axis=0) * LOG2E
            ls, lm = _log2_sigmoids(z)
            if diagonal:
                lm = jnp.where(valid, lm, 0.0)
            after = _split_dot2(lm, later)
            a = jnp.exp2(ls + after + run)
            if diagonal:
                a = jnp.where(valid, a, 0.0)
            ab = a.astype(BF)
            keys = pl.ds(pl.multiple_of(j * t, t), t)
            av = jnp.concatenate([_dot(ab[2 * t * p:2 * t * (p + 1)], v_ref[keys, 128 * p:128 * (p + 1)])
                                  for p in range(g)], axis=0)
            return acc + av, run + after[:, 0:1] + lm[:, 0:1]

        carry = tile(i, (jnp.zeros((rows, 128), F32), jnp.zeros((rows, 1), F32)), True)
        acc, run = lax.fori_loop(0, i, lambda s, cr: tile(i - 1 - s, cr, False), carry)
        run = jnp.broadcast_to(run, (rows, 128))
        for p in range(g):
            o_ref[:, 128 * p:128 * (p + 1)] = _unstack_heads(acc[2 * t * p:2 * t * (p + 1)]).astype(BF)
            tot_ref[:, 128 * p:128 * (p + 1)] = _unstack_heads(run[2 * t * p:2 * t * (p + 1)])

    w = 128 * g
    return pl.pallas_call(
        body, name="sb_fwd", grid=(N_PAIRS // g, nq),
        in_specs=[pl.BlockSpec((t, w), lambda p, i: (i, C_QS // w + p)),
                  pl.BlockSpec((SEQ, w), lambda p, i: (0, C_KS // w + p)),
                  pl.BlockSpec((SEQ, w), lambda p, i: (0, C_VS // w + p))],
        out_specs=[pl.BlockSpec((t, w), lambda p, i: (i, p))] * 2,
        out_shape=[jax.ShapeDtypeStruct((SEQ, 512), BF), jax.ShapeDtypeStruct((SEQ, 512), F32)],
        scratch_shapes=[pltpu.VMEM((g, nq, 128, t), BF)],
        compiler_params=_cp(("parallel", "arbitrary")),
    )(proj, proj, proj)


def _sb_bwd(proj, dsb, tot, dep=None):
    t, g = SB_BLOCK, SB_GROUP
    nq = SEQ // t
    rows = 2 * g * t

    def body(q_ref, k_ref, v_ref, do_ref, tot_ref, dq_ref, dk_ref, dv_ref, kt_ref, vt_ref, dkt_acc, dvt_acc):
        i = pl.program_id(1)

        @pl.when(i == 0)
        def _():
            dkt_acc[...] = jnp.zeros_like(dkt_acc)
            dvt_acc[...] = jnp.zeros_like(dvt_acc)
            for p in range(g):
                for jj in range(nq):
                    kt_ref[p, jj] = k_ref[jj * t:(jj + 1) * t, 128 * p:128 * (p + 1)].T
                    vt_ref[p, jj] = v_ref[jj * t:(jj + 1) * t, 128 * p:128 * (p + 1)].T

        q2 = [_stack_heads((q_ref[:, 128 * p:128 * (p + 1)].astype(F32) * QK_SCALE).astype(BF)) for p in range(g)]
        do2 = [_stack_heads(do_ref[:, 128 * p:128 * (p + 1)]) for p in range(g)]
        q2t = [v.T for v in q2]
        do2t = [v.T for v in do2]
        tots = tot_ref[...]
        total = jnp.concatenate([tots[:, 64 * h:64 * h + 1] for h in range(2 * g)], axis=0)
        rr, cc, valid = _sb_masks(t, 2 * g)
        upto = (rr <= cc).astype(BF)
        before = (rr < cc).astype(BF)

        def part(v, p):
            return v[2 * t * p:2 * t * (p + 1)]

        def tile(j, carry, diagonal):
            dq, run_l, run_g = carry
            z = jnp.concatenate([_dot(q2[p], kt_ref[p, j]) for p in range(g)], axis=0) * LOG2E
            ls, lm = _log2_sigmoids(z)
            if diagonal:
                lm = jnp.where(valid, lm, 0.0)
            incl = _split_dot2(lm, upto)
            a = jnp.exp2(ls + (total - (incl + run_l)))
            if diagonal:
                a = jnp.where(valid, a, 0.0)
            gg = a * jnp.concatenate([_dot(do2[p], vt_ref[p, j]) for p in range(g)], axis=0)
            excl = _split_dot2(gg, before)
            dz = gg * jnp.exp2(lm) - (excl + run_g) * jnp.exp2(ls)
            if diagonal:
                dz = jnp.where(valid, dz, 0.0)
            dzb = dz.astype(BF)
            ab = a.astype(BF)
            keys = pl.ds(pl.multiple_of(j * t, t), t)
            for p in range(g):
                dkt_acc[p, j] += _dot(q2t[p], part(dzb, p))
                dvt_acc[p, j] += _dot(do2t[p], part(ab, p))
            dq_t = jnp.concatenate([_dot(part(dzb, p), k_ref[keys, 128 * p:128 * (p + 1)]) for p in range(g)], axis=0)
            return (dq + dq_t, run_l + incl[:, t - 1:t], run_g + excl[:, t - 1:t] + gg[:, t - 1:t])

        zero = jnp.zeros((rows, 1), F32)
        carry = lax.fori_loop(0, i, lambda j, cr: tile(j, cr, False), (jnp.zeros((rows, 128), F32), zero, zero))
        dq = tile(i, carry, True)[0]
        for p in range(g):
            dq_ref[:, 128 * p:128 * (p + 1)] = (_unstack_heads(part(dq, p)) * QK_SCALE).astype(BF)

        @pl.when(i == nq - 1)
        def _():
            for p in range(g):
                for jj in range(nq):
                    dk_ref[jj * t:(jj + 1) * t, 128 * p:128 * (p + 1)] = dkt_acc[p, jj].T.astype(BF)
                    dv_ref[jj * t:(jj + 1) * t, 128 * p:128 * (p + 1)] = dvt_acc[p, jj].T.astype(BF)

    w = 128 * g
    tile_spec = pl.BlockSpec((t, w), lambda p, i: (i, p))
    col_spec = pl.BlockSpec((SEQ, w), lambda p, i: (0, p))
    shp = jax.ShapeDtypeStruct((SEQ, 512), BF)
    body, in_specs, args = _add_dep(
        body, [pl.BlockSpec((t, w), lambda p, i: (i, C_QS // w + p)),
               pl.BlockSpec((SEQ, w), lambda p, i: (0, C_KS // w + p)),
               pl.BlockSpec((SEQ, w), lambda p, i: (0, C_VS // w + p)),
               tile_spec, tile_spec],
        [proj, proj, proj, dsb, tot], dep)
    return pl.pallas_call(
        body, name="sb_bwd", grid=(N_PAIRS // g, nq),
        in_specs=in_specs,
        out_specs=[tile_spec, col_spec, col_spec],
        out_shape=[shp, shp, shp],
        scratch_shapes=[pltpu.VMEM((g, nq, 128, t), BF), pltpu.VMEM((g, nq, 128, t), BF),
                        pltpu.VMEM((g, nq, 128, t), F32), pltpu.VMEM((g, nq, 128, t), F32)],
        compiler_params=_cp(("parallel", "arbitrary")),
    )(*args)


def _mix_ffn(retg, sb, proj, x, target, vecs, w_ret, w_sb4, w_out, w_ff14, w_ff24):
    tm, half = 256, 512

    def body(r_ref, s_ref, ar0, ar1, as0, as1, x_ref, t_ref, vec_ref, wr_ref, ws_ref, wo_ref, w1_ref, w2_ref,
             mix_ref, rb_ref, sbp_ref, h2_ref, a_ref, df_ref, du_ref, dh1_ref, dy_ref, sta_ref, stb_ref, u_scr):
        @pl.when(pl.program_id(0) == 0)
        def _():
            sta_ref[...] = jnp.zeros_like(sta_ref)
            stb_ref[...] = jnp.zeros_like(stb_ref)

        rb = _dot(r_ref[...], wr_ref[...])
        sbv = s_ref[...]
        sbp = jnp.concatenate([_dot(sbv, ws_ref[k]) for k in range(N_CHIPS)], axis=1)
        gate_r = _sigmoid(jnp.concatenate([ar0[...], ar1[...]], axis=1).astype(F32))
        gate_s = _sigmoid(jnp.concatenate([as0[...], as1[...]], axis=1).astype(F32))
        mixed = (gate_r * rb + gate_s * sbp).astype(BF)
        mix_ref[...] = mixed
        rb_ref[...] = rb.astype(BF)
        sbp_ref[...] = sbp.astype(BF)
        yy = _dot(mixed, wo_ref[...])
        gt1, g2 = _row(vec_ref, V_GT1), _row(vec_ref, V_G2)
        g3, sc2 = _row(vec_ref, V_G3), _row(vec_ref, V_SC2)
        r2 = _rms(yy)
        yn = yy * r2
        h1 = x_ref[...] + gt1 * yn * g2
        r3 = _rms(h1)
        hn3 = h1 * r3
        hb = (hn3 * (g3 * (1.0 + sc2)) + _row(vec_ref, V_SH2)).astype(BF)
        h2_ref[...] = hb

        f = jnp.zeros((tm, D_MODEL), F32)
        for k in range(N_CHIPS):
            cols = slice(k * D_MODEL, (k + 1) * D_MODEL)
            u = _dot(hb, w1_ref[k])
            r = jnp.maximum(u, 0.0)
            act = (r * r).astype(BF)
            u_scr[:, cols] = u.astype(BF)
            a_ref[:, cols] = act
            f = f + _dot(act, w2_ref[k])
        r4 = _rms(f)
        fn = f * r4
        gt2, g4 = _row(vec_ref, V_GT2), _row(vec_ref, V_G4)
        diff = h1 + gt2 * fn * g4 - t_ref[...]
        dout = diff * (1.0 / D_MODEL)
        dfn = dout * gt2 * g4
        dfb = (r4 * (dfn - fn * _rowmean(dfn * fn))).astype(BF)
        df_ref[...] = dfb
        sta_ref[0:1, :] += _colsum(dout * fn * g4)
        sta_ref[1:2, :] += _colsum(dout * gt2 * fn)
        sta_ref[2:3, :] += _colsum(diff * diff) * (0.5 / D_MODEL)

        dh2 = jnp.zeros((tm, D_MODEL), F32)
        for k in range(N_CHIPS):
            cols = slice(k * D_MODEL, (k + 1) * D_MODEL)
            da = _dot_nt(dfb, w2_ref[k])
            du = (da * (2.0 * jnp.maximum(u_scr[:, cols].astype(F32), 0.0))).astype(BF)
            du_ref[:, cols] = du
            dh2 = dh2 + _dot_nt(du, w1_ref[k])
        dhn3 = dh2 * g3 * (1.0 + sc2)
        dh1 = dout + r3 * (dhn3 - hn3 * _rowmean(dhn3 * hn3))
        dyn = dh1 * gt1 * g2
        dh1_ref[...] = dh1
        dy_ref[...] = (r2 * (dyn - yn * _rowmean(dyn * yn))).astype(BF)
        stb_ref[0:1, :] += _colsum(dh2)
        stb_ref[1:2, :] += _colsum(dh2 * hn3 * g3)
        stb_ref[2:3, :] += _colsum(dh2 * hn3 * (1.0 + sc2))
        stb_ref[3:4, :] += _colsum(dh1 * yn * g2)
        stb_ref[4:5, :] += _colsum(dh1 * gt1 * yn)

    row = pl.BlockSpec((tm, D_MODEL), lambda i: (i, 0))
    wide = pl.BlockSpec((tm, D_FF), lambda i: (i, 0))
    sums = pl.BlockSpec((8, D_MODEL), lambda i: (0, 0))
    gate = lambda c0: pl.BlockSpec((tm, half), lambda i: (i, c0 // half))
    bf = jax.ShapeDtypeStruct((SEQ, D_MODEL), BF)
    ff = jax.ShapeDtypeStruct((SEQ, D_FF), BF)
    st = jax.ShapeDtypeStruct((8, D_MODEL), F32)
    return pl.pallas_call(
        body, name="mix_ffn", grid=(SEQ // tm,),
        in_specs=[row, pl.BlockSpec((tm, 512), lambda i: (i, 0)), gate(C_AR), gate(C_AR + half), gate(C_AS),
                  gate(C_AS + half), row, row, _resident((16, D_MODEL)), _resident((D_MODEL, D_MODEL)),
                  _resident(w_sb4.shape), _resident((D_MODEL, D_MODEL)), _resident(w_ff14.shape),
                  _resident(w_ff24.shape)],
        out_specs=[row, row, row, row, wide, row, wide, row, row, sums, sums],
        out_shape=[bf, bf, bf, bf, ff, bf, ff, jax.ShapeDtypeStruct((SEQ, D_MODEL), F32), bf, st, st],
        scratch_shapes=[pltpu.VMEM((tm, D_FF), BF)],
        compiler_params=_cp(("arbitrary",)),
    )(retg, sb, proj, proj, proj, proj, x, target, vecs, w_ret, w_sb4, w_out, w_ff14, w_ff24)


def _mix_ret_bwd(dy, proj, rb, sbp, o_raw, gn_g, w_out, w_sb4, w_ret):
    tm, half = min(512, SEQ), 512

    def body(dy_ref, ar0, ar1, as0, as1, rb_ref, sbp_ref, g_ref, o_ref, gn_ref, wo_ref, ws_ref, wr_ref,
             drb_ref, dsbp_ref, da_ref, dsb_ref, dret_ref, dgr_ref, st_ref):
        @pl.when(pl.program_id(0) == 0)
        def _():
            st_ref[...] = jnp.zeros_like(st_ref)

        dm_all = _dot_nt(dy_ref[...], wo_ref[...])
        dsb = jnp.zeros((tm, 512), F32)
        drbs = []
        for hf, (ar_ref, as_ref) in enumerate(((ar0, as0), (ar1, as1))):
            cols = slice(half * hf, half * (hf + 1))
            dm = dm_all[:, cols]
            sr = _sigmoid(ar_ref[...].astype(F32))
            ss = _sigmoid(as_ref[...].astype(F32))
            dsbp = (dm * ss).astype(BF)
            drbs.append((dm * sr).astype(BF))
            dsbp_ref[:, cols] = dsbp
            da_ref[:, cols] = (dm * rb_ref[:, cols].astype(F32) * sr * (1.0 - sr)).astype(BF)
            da_ref[:, D_MODEL + half * hf:D_MODEL + half * (hf + 1)] = (
                dm * sbp_ref[:, cols].astype(F32) * ss * (1.0 - ss)).astype(BF)
            dsb = dsb + _dot_nt(dsbp[:, :256], ws_ref[2 * hf]) + _dot_nt(dsbp[:, 256:], ws_ref[2 * hf + 1])
        dsb_ref[...] = dsb.astype(BF)
        drb = jnp.concatenate(drbs, axis=1)
        drb_ref[...] = drb
        dretg = _dot_nt(drb, wr_ref[...])
        for gi in range(D_MODEL // 128):
            cols = slice(128 * gi, 128 * (gi + 1))
            o = o_ref[:, cols]
            d = o - _rowmean(o)
            rstd = lax.rsqrt(_rowmean(d * d) + EPS)
            nh = d * rstd
            gain = gn_ref[:, cols]
            gr = g_ref[:, cols].astype(F32)
            sg = _sigmoid(gr)
            dg = dretg[:, cols]
            dgn = dg * gr * sg
            dnh = dgn * gain
            dgr_ref[:, cols] = (dg * nh * gain * sg * (1.0 + gr * (1.0 - sg))).astype(BF)
            dret_ref[:, cols] = (rstd * (dnh - _rowmean(dnh) - nh * _rowmean(dnh * nh))).astype(BF)
            st_ref[0:1, cols] += _colsum(dgn * nh)

    row = pl.BlockSpec((tm, D_MODEL), lambda i: (i, 0))
    gate = lambda c0: pl.BlockSpec((tm, half), lambda i: (i, c0 // half))
    shp = jax.ShapeDtypeStruct((SEQ, D_MODEL), BF)
    return pl.pallas_call(
        body, name="mix_ret_bwd", grid=(SEQ // tm,),
        in_specs=[row, gate(C_AR), gate(C_AR + half), gate(C_AS), gate(C_AS + half), row, row,
                  pl.BlockSpec((tm, D_MODEL), lambda i: (i, C_GR // D_MODEL)), row, _resident((1, D_MODEL)),
                  _resident((D_MODEL, D_MODEL)), _resident(w_sb4.shape), _resident((D_MODEL, D_MODEL))],
        out_specs=[row, row, pl.BlockSpec((tm, 2 * D_MODEL), lambda i: (i, 0)), pl.BlockSpec((tm, 512), lambda i: (i, 0)),
                   row, row, pl.BlockSpec((8, D_MODEL), lambda i: (0, 0))],
        out_shape=[shp, shp, jax.ShapeDtypeStruct((SEQ, 2 * D_MODEL), BF), jax.ShapeDtypeStruct((SEQ, 512), BF),
                   shp, shp, jax.ShapeDtypeStruct((8, D_MODEL), F32)],
        compiler_params=_cp(("arbitrary",)),
    )(dy, proj, proj, proj, proj, rb, sbp, proj, o_raw, gn_g, w_out, w_sb4, w_ret)


def _dproj_segments(widths):
    wc = D_IN // N_CHIPS
    segs, start = [], 0
    for pi, width in enumerate(widths):
        lo = start
        while lo < start + width:
            j = lo // wc
            hi = min(start + width, (j + 1) * wc)
            segs.append((j, lo - j * wc, pi, lo - start, hi - lo))
            lo = hi
        start += width
    assert start == D_IN
    return segs


def _in_proj_bwd(pieces, x, dh1, vecs, w_in4, updates, dep=None):
    tm = 256
    steps = SEQ // tm
    n, nu = len(pieces), len(updates)
    segs = _dproj_segments([p.shape[1] for p in pieces])

    def body(*refs):
        x_ref, dh1_ref, vec_ref, w_ref = refs[n:n + 4]
        upd_in = refs[n + 4:n + 4 + 4 * nu]
        dx_ref, st_ref = refs[n + 4 + 4 * nu:n + 6 + 4 * nu]
        upd_out = refs[n + 6 + 4 * nu:]
        for u in range(nu):
            w_u, m_u, v_u, g_u = upd_in[4 * u:4 * u + 4]
            go_u, d_u, mo_u, vo_u = upd_out[4 * u:4 * u + 4]
            gg = g_u[...]
            go_u[...] = gg
            d_u[...], mo_u[...], vo_u[...] = _adamw_math(w_u[...], gg, m_u[...], v_u[...])

        @pl.when(pl.program_id(0) == 0)
        def _():
            st_ref[...] = jnp.zeros_like(st_ref)

        dh = jnp.zeros((tm, D_MODEL), F32)
        for j, so, pi, po, width in segs:
            dh = dh + _dot_nt(refs[pi][:, po:po + width], w_ref[j, :, so:so + width])
        xx = x_ref[...]
        r1 = _rms(xx)
        xn = xx * r1
        g1, sc1 = _row(vec_ref, V_G1), _row(vec_ref, V_SC1)
        dxn = dh * g1 * (1.0 + sc1)
        dx_ref[...] = dh1_ref[...] + r1 * (dxn - xn * _rowmean(dxn * xn))
        st_ref[0:1, :] += _colsum(dh)
        st_ref[1:2, :] += _colsum(dh * xn * g1)
        st_ref[2:3, :] += _colsum(dh * xn * (1.0 + sc1))

    row = pl.BlockSpec((tm, D_MODEL), lambda i: (i, 0))
    upd_specs, upd_shapes, upd_args = [], [], []
    for arrays in updates:
        rows, cols = arrays[0].shape
        upd_specs += [pl.BlockSpec((rows // steps, cols), lambda i: (i, 0))] * 4
        upd_shapes += [jax.ShapeDtypeStruct((rows, cols), F32)] * 4
        upd_args += list(arrays)
    body, in_specs, args = _add_dep(
        body, [pl.BlockSpec((tm, p.shape[1]), lambda i: (i, 0)) for p in pieces] + [
            row, row, _resident((16, D_MODEL)), _resident(w_in4.shape)] + upd_specs,
        list(pieces) + [x, dh1, vecs, w_in4] + upd_args, dep)
    outs = pl.pallas_call(
        body, name="in_proj_bwd", grid=(steps,),
        in_specs=in_specs,
        out_specs=[row, pl.BlockSpec((8, D_MODEL), lambda i: (0, 0))] + upd_specs,
        out_shape=[jax.ShapeDtypeStruct((SEQ, D_MODEL), F32), jax.ShapeDtypeStruct((8, D_MODEL), F32)] + upd_shapes,
        compiler_params=_cp(("arbitrary",)),
    )(*args)
    return outs[0], outs[1], [tuple(outs[2 + 4 * u:6 + 4 * u]) for u in range(nu)]


def _grad_w_in(h, pieces, dep=None):
    ta = 512
    n = len(pieces)
    segs = _dproj_segments([p.shape[1] for p in pieces])

    def body(*refs):
        h_ref, o_ref = refs[n], refs[n + 1]
        hh = h_ref[...]
        for j, so, pi, po, width in segs:
            o_ref[j, :, so:so + width] = _dot_tn(hh, refs[pi][:, po:po + width]).astype(BF)

    body, in_specs, args = _add_dep(
        body, [_resident(p.shape) for p in pieces] + [pl.BlockSpec((SEQ, ta), lambda i: (0, i))],
        list(pieces) + [h], dep)
    return pl.pallas_call(
        body, name="grad_w_in", grid=(D_MODEL // ta,),
        in_specs=in_specs,
        out_specs=pl.BlockSpec((N_CHIPS, ta, D_IN // N_CHIPS), lambda i: (0, i, 0)),
        out_shape=jax.ShapeDtypeStruct((N_CHIPS, D_MODEL, D_IN // N_CHIPS), BF),
        compiler_params=_cp(("parallel",)),
    )(*args)


def _weight_grads(jobs, name):
    def grad(in_refs, out_refs):
        out_refs[0][...] = _dot_tn(in_refs[0][...], in_refs[1][...]).astype(BF)

    sets = []
    for a, b, col_sharded in jobs:
        ka, nb_ = a.shape[1], b.shape[1]
        if col_sharded:
            ta, tb = ka, nb_ // N_CHIPS
            ins = [(a, (SEQ, ta), lambda l: (0, 0)), (b, (SEQ, tb), lambda l: (0, l))]
        else:
            ta, tb = ka // N_CHIPS, nb_
            ins = [(a, (SEQ, ta), lambda l: (0, l)), (b, (SEQ, tb), lambda l: (0, 0))]
        sets.append((N_CHIPS, ins, [((N_CHIPS, ta, tb), BF, (None, ta, tb), lambda l: (l, 0, 0))], grad))
    return [o[0] for o in _run_sets(name, [], sets)]


def _rope_constants():
    freq = np.float32(ROPE_BASE) ** (-np.arange(0, 64, 2, dtype=np.float32) / np.float32(64))
    inv = np.tile(freq.astype(np.float32), 4).reshape(1, 128)
    sign = np.tile(np.concatenate([-np.ones(32, np.float32), np.ones(32, np.float32)]), 2).reshape(1, 128)
    return jnp.asarray(inv), jnp.asarray(sign)


def _log_gamma():
    return jnp.asarray(np.log1p(-(2.0 ** (-5.0 - np.arange(8, dtype=np.float64)))).astype(np.float32))


def _halves(g):
    return g.reshape(N_CHIPS, 2, g.shape[1] // 2, g.shape[2])


def kernel(x, c, positions, ada_w, ada_b, pre_mix_g, post_mix_g, pre_ffn_g, post_ffn_g, w_in, ret_gn_g, w_ret_branch, w_sb_branch, w_out, w_ff1, w_ff2, loss_target, m_ada_w, m_ada_b, m_pre_mix_g, m_post_mix_g, m_pre_ffn_g, m_post_ffn_g, m_w_in, m_ret_gn_g, m_w_ret_branch, m_w_sb_branch, m_w_out, m_w_ff1, m_w_ff2, v_ada_w, v_ada_b, v_pre_mix_g, v_post_mix_g, v_pre_ffn_g, v_post_ffn_g, v_w_in, v_ret_gn_g, v_w_ret_branch, v_w_sb_branch, v_w_out, v_w_ff1, v_w_ff2):
    names = ["w_in", "w_ret", "w_sb", "w_out", "w_ff1", "w_ff2"]
    big = dict(zip(names, [w_in, w_ret_branch, w_sb_branch, w_out, w_ff1, w_ff2]))
    big_m = dict(zip(names, [m_w_in, m_w_ret_branch, m_w_sb_branch, m_w_out, m_w_ff1, m_w_ff2]))
    big_v = dict(zip(names, [v_w_in, v_w_ret_branch, v_w_sb_branch, v_w_out, v_w_ff1, v_w_ff2]))
    rest = names[1:]
    cidx = lax.axis_index("c").astype(jnp.int32).reshape(1)
    kidx = (2 * lax.axis_index("x") + lax.axis_index("y")).astype(jnp.int32).reshape(1)
    x0, target = x[0], loss_target[0]

    buf_in, sem_in, tok_in = _gather_start("gather_in_start", [_cast_bf16(w_in[0], kidx, c, "cast_w_in")])
    rest_bufs = [_cast_bf16(big[nm][0], kidx, tok_in, "cast_" + nm) for nm in rest]
    inv_freq, sign = _rope_constants()
    lg = _log_gamma()
    cos, sin_s = _rope_tables(positions.reshape(SEQ, 1), inv_freq, sign, dep=tok_in)

    def table(b6, g5):
        return jnp.concatenate([b6.reshape(6, D_MODEL)] + g5 + [jnp.zeros((5, D_MODEL), F32)], axis=0)

    wsm = table(ada_b, [pre_mix_g, post_mix_g, pre_ffn_g, post_ffn_g, ret_gn_g])
    msm = table(m_ada_b, [m_pre_mix_g, m_post_mix_g, m_pre_ffn_g, m_post_ffn_g, m_ret_gn_g])
    vsm = table(v_ada_b, [v_pre_mix_g, v_post_mix_g, v_pre_ffn_g, v_post_ffn_g, v_ret_gn_g])
    c_all, mod4 = _mod_exchange(c, ada_w[0], ada_b.reshape(N_CHIPS, -1), rest_bufs + [cos, wsm, msm, vsm])
    vecs = jnp.concatenate([mod4.reshape(6, D_MODEL), pre_mix_g, post_mix_g, pre_ffn_g, post_ffn_g,
                            jnp.zeros((6, D_MODEL), F32)], axis=0)
    buf_in, sem_in, tok_in = _gather_pass("gather_in_pass", buf_in, sem_in, vecs)
    buf_rest, sem_rest, tok_rest = _gather_start("gather_rest_start", rest_bufs, after=tok_in)
    (w_in4,) = _gather_finish("gather_in_finish", buf_in, sem_in, tok_rest)

    h, proj = _ln_proj(x0, vecs, w_in4)
    sb, tot = _sb_fwd(proj)
    buf_rest, sem_rest, tok_rest = _gather_pass("gather_rest_pass", buf_rest, sem_rest, sb)
    o_raw, retg, states = _ret_fwd(proj, cos, sin_s, ret_gn_g, lg, dep=tok_rest)
    w_ret4, w_sb4, w_out4, w_ff14, w_ff24 = _gather_finish("gather_rest_finish", buf_rest, sem_rest, retg)
    w_ret = w_ret4.reshape(D_MODEL, D_MODEL)
    w_out2 = w_out4.reshape(D_MODEL, D_MODEL)
    mixed, rb, sbp, h2, act, df, du, dh1, dy, st_a, st_b = _mix_ffn(
        retg, sb, proj, x0, target, vecs, w_ret, w_sb4, w_out2, w_ff14, w_ff24)
    grads = dict(zip(("w_ff2", "w_ff1"), _weight_grads([(act, df, False), (h2, du, True)], "grad_w_ff")))
    drb, dsbp, da, dsb, dret, dg_r, st_c = _mix_ret_bwd(dy, proj, rb, sbp, o_raw, ret_gn_g, w_out2, w_sb4, w_ret)
    grads.update(zip(("w_out", "w_ret", "w_sb"), _weight_grads(
        [(mixed, dy, False), (retg, drb, False), (sb, dsbp, True)], "grad_w_branches")))

    bufs, sems, tok = _pair_send_start("rs_rest_pair_send", [_halves(grads[nm]) for nm in rest])
    dqkv_r = _ret_bwd(proj, cos, sin_s, dret, states, lg, dep=tok)
    mine, theirs = _pair_send_wait("rs_rest_pair_recv", bufs, sems, dqkv_r)
    pair_sums = _pair_add_all(mine, theirs, cidx, "pair_add_rest")
    bufs, sems, tok = _chip_send_start("rs_rest_chip_send", pair_sums)
    dq_s, dk_s, dv_s = _sb_bwd(proj, dsb, tot, dep=tok)
    dproj = [dqkv_r, dg_r, dq_s, dk_s, dv_s, da]
    g_in = _grad_w_in(h, dproj)

    bufs_in, sems_in, tok_in = _pair_send_start("rs_in_pair_send", [_halves(g_in)])
    own, parts = _chip_send_wait("rs_rest_chip_recv", bufs, sems, tok_in)
    sums = _chip_add_all(own, parts, kidx, cidx, "chip_add_rest")
    bufs, sems, tok = _pair_swap_start("rs_rest_pair_swap", sums)
    mine, theirs = _pair_send_wait("rs_in_pair_recv", bufs_in, sems_in, tok)
    bufs_in, sems_in, tok_in = _chip_send_start(
        "rs_in_chip_send", [_pair_add(mine[0], theirs[0], cidx, "pair_add_w_in")])
    full_rest = dict(zip(rest, _pair_swap_wait("rs_rest_pair_swapped", bufs, sems, tok_in)))
    small_w = ("w_out", "w_sb", "w_ret")
    out = dict(zip(small_w, _adamw_all(
        [(big[nm][0], big_m[nm][0], big_v[nm][0], full_rest[nm].reshape(big[nm].shape[1:])) for nm in small_w],
        "adamw_small", tok_in)))
    bufs, sems, tok = bufs_in, sems_in, out["w_ret"][1]
    riding = ("w_ff2", "w_ff1")
    dx, st_d, updated = _in_proj_bwd(
        dproj, x0, dh1, vecs, w_in4,
        [(big[nm][0], big_m[nm][0], big_v[nm][0], full_rest[nm].reshape(big[nm].shape[1:])) for nm in riding],
        dep=tok)
    out.update(zip(riding, updated))

    a_, b_, c_, d_ = range(4)
    payload_rows = [(d_, 0), (d_, 1), (b_, 3), (b_, 0), (b_, 1), (a_, 0),
                    (d_, 2), (b_, 4), (b_, 2), (a_, 1), (c_, 0), (a_, 2)]
    g_ada, loss, small = _small_exchange([st_a, st_b, st_c, st_d], payload_rows, c_all, wsm, msm, vsm)
    own, parts = _chip_send_wait("rs_in_chip_recv", bufs, sems, g_ada)
    bufs, sems, tok = _pair_swap_start(
        "rs_in_pair_swap", [_chip_add(own[0], parts[0], kidx, cidx, "chip_add_w_in")])
    ada_out = _adamw(ada_w[0], m_ada_w[0], v_ada_w[0], g_ada, "adamw_ada_w", dep=tok)
    (full_in,) = _pair_swap_wait("rs_in_pair_swapped", bufs, sems, ada_out[1])
    out["w_in"] = _adamw(w_in[0], m_w_in[0], v_w_in[0], full_in.reshape(w_in.shape[1:]), "adamw_w_in")

    def ordered(which):
        sm = small[which]
        bg = [out[nm][which][None] for nm in names]
        return [ada_out[which][None], sm[0], sm[1], sm[2], sm[3], sm[4], bg[0], sm[5]] + bg[1:]

    return (loss.reshape(()), dx[None], *ordered(0), *ordered(1), *ordered(2), *ordered(3))
```

```python
import functools

import numpy as np
import jax
import jax.numpy as jnp
from jax import lax
from jax.experimental import pallas as pl
from jax.experimental.pallas import tpu as pltpu

SEQ = 2048
D_MODEL = 1024
D_IN = 6656
D_FF = 4096
N_CHIPS = 4
EPS = 1e-6
ROPE_BASE = 10000.0
RET_BLOCK = 256
RET_CHUNK_SHIFT = 6
SB_BLOCK = 256
QK_SCALE = 0.125
LOG2E = 1.4426950408889634
N_PAIRS = 4
SB_GROUP = 4

ADAM_LR = 0.001
ADAM_B1 = 0.9
ADAM_B2 = 0.999
ADAM_EPS = 1e-08
ADAM_WD = 0.01
ADAM_STEP = 10

BF = jnp.bfloat16
F32 = jnp.float32
MESH = pl.DeviceIdType.MESH
VMEM_LIMIT = 56 * 1024 * 1024
ANY = pl.BlockSpec(memory_space=pl.ANY)

C_QR, C_KR, C_VR, C_GR, C_QS, C_KS, C_VS, C_AR, C_AS = 0, 512, 1024, 2048, 3072, 3584, 4096, 4608, 5632

V_SH1, V_SC1, V_GT1, V_SH2, V_SC2, V_GT2, V_G1, V_G2, V_G3, V_G4 = range(10)
P_DSH1, P_DSC1, P_DGT1, P_DSH2, P_DSC2, P_DGT2, P_DG1, P_DG2, P_DG3, P_DG4, P_DGN, P_LOSS = range(12)
N_PAY = 12


def _cp(sem=None, **kw):
    if sem is not None:
        kw["dimension_semantics"] = sem
    return pltpu.CompilerParams(vmem_limit_bytes=VMEM_LIMIT, **kw)


def _dot(a, b):
    return jnp.dot(a, b, preferred_element_type=F32)


def _dot_nt(a, b):
    return lax.dot_general(a, b, (((1,), (1,)), ((), ())), preferred_element_type=F32)


def _dot_tn(a, b):
    return lax.dot_general(a, b, (((0,), (0,)), ((), ())), preferred_element_type=F32)


def _row(ref, i):
    return ref[i:i + 1, :]


def _rms(v):
    return lax.rsqrt(jnp.mean(v * v, axis=1, keepdims=True) + EPS)


def _colsum(v):
    return jnp.sum(v, axis=0, keepdims=True)


def _rowmean(v):
    return jnp.mean(v, axis=1, keepdims=True)


def _sigmoid(v):
    return 1.0 / (1.0 + jnp.exp(-v))


def _cast_bf16(w, kidx, dep, name):
    rows, cols = w.shape
    tr = min(rows, 512)

    def body(k_ref, w_ref, dep_ref, o_ref):
        o_ref[...] = w_ref[...].astype(BF)

    return pl.pallas_call(
        body, name=name,
        grid_spec=pltpu.PrefetchScalarGridSpec(
            num_scalar_prefetch=1, grid=(rows // tr,),
            in_specs=[pl.BlockSpec((tr, cols), lambda i, k_ref: (i, 0)), ANY],
            out_specs=pl.BlockSpec((None, tr, cols), lambda i, k_ref: (k_ref[0], i, 0))),
        out_shape=jax.ShapeDtypeStruct((N_CHIPS, rows, cols), BF),
        compiler_params=_cp(("parallel",)),
    )(kidx, w, dep)


def _adamw_math(w, g, m, v):
    m = ADAM_B1 * m + (1.0 - ADAM_B1) * g
    v = ADAM_B2 * v + (1.0 - ADAM_B2) * (g * g)
    m_hat = m / (1.0 - ADAM_B1 ** ADAM_STEP)
    v_hat = v / (1.0 - ADAM_B2 ** ADAM_STEP)
    delta = -ADAM_LR * (m_hat / (jnp.sqrt(v_hat) + ADAM_EPS) + ADAM_WD * w)
    return delta, m, v


def _adamw(w, m, v, g, name, dep=None):
    rows, cols = w.shape
    tr = min(rows, 256)

    def body(w_ref, m_ref, v_ref, g_ref, go_ref, d_ref, mo_ref, vo_ref):
        gg = g_ref[...]
        d, mm, vv = _adamw_math(w_ref[...], gg, m_ref[...], v_ref[...])
        go_ref[...] = gg
        d_ref[...] = d
        mo_ref[...] = mm
        vo_ref[...] = vv

    spec = pl.BlockSpec((tr, cols), lambda i: (i, 0))
    shp = jax.ShapeDtypeStruct((rows, cols), F32)
    body, in_specs, args = _add_dep(body, [spec] * 4, [w, m, v, g], dep)
    return pl.pallas_call(
        body, name=name, grid=(rows // tr,),
        in_specs=in_specs, out_specs=[spec] * 4, out_shape=[shp] * 4,
        compiler_params=_cp(("parallel",)),
    )(*args)


def _place():
    x, y, c = lax.axis_index("x"), lax.axis_index("y"), lax.axis_index("c")
    return x, y, c


HBM = pl.BlockSpec(memory_space=pltpu.HBM)
SEM = pl.BlockSpec(memory_space=pltpu.SEMAPHORE)
EFFECT = pltpu.SideEffectType.DATAFLOW_SIDE_EFFECTING


def _add_dep(body, in_specs, args, dep):
    if dep is None:
        return body, list(in_specs), list(args)
    n = len(args)

    def wrapped(*refs):
        body(*refs[:n], *refs[n + 1:])

    return wrapped, list(in_specs) + [ANY], list(args) + [dep]


def _split_call(name, bufs, run, old=None, after=None, new=0):
    nb = len(bufs)
    n_old = 2 if old is not None else 0
    n_in = nb + n_old + (1 if after is not None else 0)

    def body(*refs):
        old_sems = (refs[nb], refs[nb + 1]) if old is not None else None
        new_sems = (refs[n_in], refs[n_in + 1]) if new else None
        run(refs[:nb], old_sems, new_sems)
        if new:
            refs[-1][...] = jnp.zeros_like(refs[-1])

    in_specs = [HBM] * nb + [SEM] * n_old + ([ANY] if after is not None else [])
    out_shape = [pltpu.SemaphoreType.DMA((new,))] * 2 if new else []
    out_specs = [SEM, SEM] if new else []
    out_shape += [pltpu.HBM(b.shape, b.dtype) for b in bufs]
    out_specs += [HBM] * nb
    if new:
        out_shape.append(jax.ShapeDtypeStruct((8, 128), F32))
        out_specs.append(pl.BlockSpec(memory_space=pltpu.VMEM))
    first = 2 if new else 0
    args = [pltpu.with_memory_space_constraint(b, pltpu.HBM) for b in bufs]
    if old is not None:
        args += [old[0], old[1]]
    if after is not None:
        args.append(after)
    outs = pl.pallas_call(
        body, name=name, in_specs=tuple(in_specs), out_specs=tuple(out_specs), out_shape=tuple(out_shape),
        input_output_aliases={i: i + first for i in range(nb)},
        compiler_params=pltpu.CompilerParams(has_side_effects=EFFECT),
    )(*args)
    thru = list(outs[first:first + nb])
    if new:
        return thru, (outs[0], outs[1]), outs[-1]
    return thru, None, None


def _remote(part_src, part_dst, sems, i, to):
    return pltpu.make_async_remote_copy(src_ref=part_src, dst_ref=part_dst, send_sem=sems[0].at[i],
                                        recv_sem=sems[1].at[i], device_id=to, device_id_type=MESH)


def _other_chips(x, y):
    return [(1 - x, y), (x, 1 - y), (1 - x, 1 - y)]


def _gather_start(name, bufs, after=None):
    def run(refs, old, new):
        x, y, c = _place()
        k = 2 * x + y
        for w, ref in enumerate(refs):
            rh = bufs[w].shape[1] // 2
            part = ref.at[k, pl.ds(c * rh, rh)]
            for j, (cx, cy) in enumerate(_other_chips(x, y)):
                _remote(part, part, new, 3 * w + j, (cx, cy, c)).start()

    return _split_call(name, bufs, run, after=after, new=3 * len(bufs))


def _gather_pass(name, bufs, sems, after):
    def run(refs, old, new):
        x, y, c = _place()
        k = 2 * x + y
        sib = (x, y, 1 - c)
        for w, ref in enumerate(refs):
            rh = bufs[w].shape[1] // 2
            for j, (cx, cy) in enumerate(_other_chips(x, y)):
                land = ref.at[2 * cx + cy, pl.ds(c * rh, rh)]
                _remote(land, land, old, 3 * w + j, (cx, cy, c)).wait_recv()
                _remote(land, land, new, 3 * w + j, sib).start()
        for w, ref in enumerate(refs):
            rh = bufs[w].shape[1] // 2
            part = ref.at[k, pl.ds(c * rh, rh)]
            for j, (cx, cy) in enumerate(_other_chips(x, y)):
                _remote(part, part, old, 3 * w + j, (cx, cy, c)).wait_send()

    return _split_call(name, bufs, run, old=sems, after=after, new=3 * len(bufs))


def _gather_finish(name, bufs, sems, after):
    def run(refs, old, new):
        x, y, c = _place()
        sib = (x, y, 1 - c)
        for w, ref in enumerate(refs):
            rh = bufs[w].shape[1] // 2
            for j, (cx, cy) in enumerate(_other_chips(x, y)):
                sent = ref.at[2 * cx + cy, pl.ds(c * rh, rh)]
                _remote(sent, sent, old, 3 * w + j, sib).wait_send()
                land = ref.at[2 * cx + cy, pl.ds((1 - c) * rh, rh)]
                _remote(land, land, old, 3 * w + j, sib).wait_recv()

    return _split_call(name, bufs, run, old=sems, after=after)[0]


def _pair_send_start(name, grads):
    n = len(grads)
    lands = [lax.empty((N_CHIPS,) + g.shape[2:], g.dtype) for g in grads]

    def run(refs, old, new):
        x, y, c = _place()
        for w in range(n):
            _remote(refs[w].at[:, 1 - c], refs[n + w], new, w, (x, y, 1 - c)).start()

    return _split_call(name, list(grads) + lands, run, new=n)


def _pair_send_wait(name, bufs, sems, after):
    n = len(bufs) // 2

    def run(refs, old, new):
        x, y, c = _place()
        for w in range(n):
            cp = _remote(refs[w].at[:, 1 - c], refs[n + w], old, w, (x, y, 1 - c))
            cp.wait_send()
            cp.wait_recv()

    thru = _split_call(name, bufs, run, old=sems, after=after)[0]
    return thru[:n], thru[n:]


def _run_sets(name, scalars, sets, deps=()):
    starts = np.concatenate([[0], np.cumsum([s[0] for s in sets])]).tolist()

    def spec(block, index, lo, n):
        return pl.BlockSpec(block, lambda i, *sc: index(jnp.clip(i - lo, 0, n - 1), *sc))

    in_specs, out_specs, out_shape, args = [], [], [], []
    for (n, ins, outs, _), lo in zip(sets, starts):
        for array, block, index in ins:
            in_specs.append(spec(block, index, lo, n))
            args.append(array)
        for shape, dtype, block, index in outs:
            out_specs.append(spec(block, index, lo, n))
            out_shape.append(jax.ShapeDtypeStruct(shape, dtype))

    def body(*refs):
        refs = refs[len(scalars):]
        n_in = len(in_specs)
        i = pl.program_id(0)
        pos_in, pos_out = 0, n_in + len(deps)
        for (n, ins, outs, fn), lo in zip(sets, starts):
            in_refs = refs[pos_in:pos_in + len(ins)]
            out_refs = refs[pos_out:pos_out + len(outs)]
            pos_in += len(ins)
            pos_out += len(outs)
            pl.when((i >= lo) & (i < lo + n))(functools.partial(fn, in_refs, out_refs))

    outs = pl.pallas_call(
        body, name=name,
        grid_spec=pltpu.PrefetchScalarGridSpec(
            num_scalar_prefetch=len(scalars), grid=(starts[-1],), in_specs=in_specs + [ANY] * len(deps),
            out_specs=out_specs),
        out_shape=out_shape,
        compiler_params=_cp(("arbitrary",)),
    )(*scalars, *args, *deps)
    result, pos = [], 0
    for _, _, outs_s, _ in sets:
        result.append(list(outs[pos:pos + len(outs_s)]))
        pos += len(outs_s)
    return result


def _adamw_all(jobs, name, dep):
    def update(in_refs, out_refs):
        gg = in_refs[3][...]
        out_refs[0][...] = gg
        out_refs[1][...], out_refs[2][...], out_refs[3][...] = _adamw_math(
            in_refs[0][...], gg, in_refs[1][...], in_refs[2][...])

    whole = lambda l: (0, 0)
    sets = [(1, [(a, a.shape, whole) for a in job], [(job[0].shape, F32, job[0].shape, whole)] * 4, update)
            for job in jobs]
    return [tuple(o) for o in _run_sets(name, [], sets, deps=[dep])]


def _pair_add_all(gs, recvs, cidx, name):
    def add(in_refs, out_refs):
        out_refs[0][...] = (in_refs[0][...].astype(F32) + in_refs[1][...].astype(F32)).astype(BF)

    sets = []
    for g, r in zip(gs, recvs):
        _, _, rh, cols = g.shape
        tr = min(rh, 256)
        sets.append((rh // tr,
                     [(g, (N_CHIPS, None, tr, cols), lambda l, c_ref: (0, c_ref[0], l, 0)),
                      (r, (N_CHIPS, tr, cols), lambda l, c_ref: (0, l, 0))],
                     [((N_CHIPS, rh, cols), BF, (N_CHIPS, tr, cols), lambda l, c_ref: (0, l, 0))], add))
    return [o[0] for o in _run_sets(name, [cidx], sets)]


def _chip_add_all(owns, parts, kidx, cidx, name):
    def add(in_refs, out_refs):
        acc = in_refs[0][...].astype(F32)
        for s in range(3):
            acc = acc + in_refs[1][s].astype(F32)
        out_refs[0][...] = acc

    sets = []
    for own, p in zip(owns, parts):
        _, rh, cols = p.shape
        tr = min(rh, 256)
        sets.append((rh // tr,
                     [(own, (None, tr, cols), lambda l, k_ref, c_ref: (k_ref[0], l, 0)),
                      (p, (3, tr, cols), lambda l, k_ref, c_ref: (0, l, 0))],
                     [((2, rh, cols), F32, (None, tr, cols), lambda l, k_ref, c_ref: (c_ref[0], l, 0))], add))
    return [o[0] for o in _run_sets(name, [kidx, cidx], sets)]


def _pair_add(g, recv, cidx, name):
    _, _, rh, cols = g.shape
    tr = min(rh, 256)

    def body(c_ref, g_ref, r_ref, o_ref):
        o_ref[...] = (g_ref[...].astype(F32) + r_ref[...].astype(F32)).astype(BF)

    return pl.pallas_call(
        body, name=name,
        grid_spec=pltpu.PrefetchScalarGridSpec(
            num_scalar_prefetch=1, grid=(rh // tr,),
            in_specs=[pl.BlockSpec((N_CHIPS, None, tr, cols), lambda i, c_ref: (0, c_ref[0], i, 0)),
                      pl.BlockSpec((N_CHIPS, tr, cols), lambda i, c_ref: (0, i, 0))],
            out_specs=pl.BlockSpec((N_CHIPS, tr, cols), lambda i, c_ref: (0, i, 0))),
        out_shape=jax.ShapeDtypeStruct((N_CHIPS, rh, cols), BF),
        compiler_params=_cp(("parallel",)),
    )(cidx, g, recv)


def _chip_send_start(name, sums):
    n = len(sums)
    lands = [lax.empty((3,) + s.shape[1:], BF) for s in sums]

    def run(refs, old, new):
        x, y, c = _place()
        for w in range(n):
            for j, (cx, cy) in enumerate(_other_chips(x, y)):
                _remote(refs[w].at[2 * cx + cy], refs[n + w].at[j], new, 3 * w + j, (cx, cy, c)).start()

    return _split_call(name, list(sums) + lands, run, new=3 * n)


def _chip_send_wait(name, bufs, sems, after):
    n = len(bufs) // 2

    def run(refs, old, new):
        x, y, c = _place()
        for w in range(n):
            for j, (cx, cy) in enumerate(_other_chips(x, y)):
                cp = _remote(refs[w].at[2 * cx + cy], refs[n + w].at[j], old, 3 * w + j, (cx, cy, c))
                cp.wait_send()
                cp.wait_recv()

    thru = _split_call(name, bufs, run, old=sems, after=after)[0]
    return thru[:n], thru[n:]


def _chip_add(own, parts, kidx, cidx, name):
    _, rh, cols = parts.shape
    tr = min(rh, 512)

    def body(k_ref, c_ref, own_ref, p_ref, o_ref):
        acc = own_ref[...].astype(F32)
        for s in range(3):
            acc = acc + p_ref[s].astype(F32)
        o_ref[...] = acc

    return pl.pallas_call(
        body, name=name,
        grid_spec=pltpu.PrefetchScalarGridSpec(
            num_scalar_prefetch=2, grid=(rh // tr,),
            in_specs=[pl.BlockSpec((None, tr, cols), lambda i, k_ref, c_ref: (k_ref[0], i, 0)),
                      pl.BlockSpec((3, tr, cols), lambda i, k_ref, c_ref: (0, i, 0))],
            out_specs=pl.BlockSpec((None, tr, cols), lambda i, k_ref, c_ref: (c_ref[0], i, 0))),
        out_shape=jax.ShapeDtypeStruct((2, rh, cols), F32),
        compiler_params=_cp(("parallel",)),
    )(kidx, cidx, own, parts)


def _pair_swap_start(name, bufs):
    def run(refs, old, new):
        x, y, c = _place()
        for w, ref in enumerate(refs):
            _remote(ref.at[c], ref.at[c], new, w, (x, y, 1 - c)).start()

    return _split_call(name, bufs, run, new=len(bufs))


def _pair_swap_wait(name, bufs, sems, after):
    def run(refs, old, new):
        x, y, c = _place()
        for w, ref in enumerate(refs):
            _remote(ref.at[c], ref.at[c], old, w, (x, y, 1 - c)).wait_send()
            _remote(ref.at[1 - c], ref.at[1 - c], old, w, (x, y, 1 - c)).wait_recv()

    return _split_call(name, bufs, run, old=sems, after=after)[0]


def _peers(x, y, c):
    out = []
    for code in range(1, 8):
        fx, fy, fc = (code >> 2) & 1, (code >> 1) & 1, code & 1
        px = 1 - x if fx else x
        py = 1 - y if fy else y
        pc = 1 - c if fc else c
        out.append((code, (px, py, pc)))
    return out


def _mod_exchange(c_row, ada_w, ada_b4, gains, deps):
    ncol = ada_w.shape[1]

    def body(c_ref, w_ref, b_ref, g1_ref, g2_ref, g3_ref, g4_ref, *rest):
        call_ref, mod_ref, vec_ref, part_ref, send_sems, recv_sems = rest[len(deps):]
        x, y, c = _place()
        k = 2 * x + y
        me = 4 * x + 2 * y + c
        call_ref[pl.ds(me, 1), :] = c_ref[...]
        sends = []
        for code, peer in _peers(x, y, c):
            cp = pltpu.make_async_remote_copy(
                src_ref=c_ref, dst_ref=call_ref.at[pl.ds(me, 1), :],
                send_sem=send_sems.at[code], recv_sem=recv_sems.at[code],
                device_id=peer, device_id_type=MESH)
            cp.start()
            sends.append(cp)
        for code, (px, py, pc) in _peers(x, y, c):
            land = call_ref.at[pl.ds(4 * px + 2 * py + pc, 1), :]
            pltpu.make_async_remote_copy(
                src_ref=land, dst_ref=land, send_sem=send_sems.at[code], recv_sem=recv_sems.at[code],
                device_id=(px, py, pc), device_id_type=MESH).wait_recv()
        call = call_ref[...]
        act = call * _sigmoid(call)
        part = jnp.dot(act, w_ref[...], preferred_element_type=F32,
                       precision=lax.Precision.HIGHEST) + b_ref[pl.ds(k, 1), :]
        part_ref[...] = part
        mod_ref[pl.ds(k, 1), :] = part_ref[pl.ds(me, 1), :]
        chips = [(8 + j, peer) for j, (code, peer) in enumerate(_peers(x, y, c)) if code in (2, 4, 6)]
        for slot, (px, py, pc) in chips:
            cp = pltpu.make_async_remote_copy(
                src_ref=part_ref.at[pl.ds(4 * px + 2 * py + pc, 1), :], dst_ref=mod_ref.at[pl.ds(k, 1), :],
                send_sem=send_sems.at[slot], recv_sem=recv_sems.at[slot],
                device_id=(px, py, pc), device_id_type=MESH)
            cp.start()
            sends.append(cp)
        for slot, (px, py, pc) in chips:
            land = mod_ref.at[pl.ds(2 * px + py, 1), :]
            pltpu.make_async_remote_copy(
                src_ref=land, dst_ref=land, send_sem=send_sems.at[slot], recv_sem=recv_sems.at[slot],
                device_id=(px, py, pc), device_id_type=MESH).wait_recv()
        half = D_MODEL // 2
        vec_ref[...] = jnp.zeros_like(vec_ref)
        for kk in range(N_CHIPS):
            r0 = 3 * (kk // 2)
            if kk % 2 == 0:
                vec_ref[r0:r0 + 1, :] = mod_ref[kk:kk + 1, :D_MODEL]
                vec_ref[r0 + 1:r0 + 2, :half] = mod_ref[kk:kk + 1, D_MODEL:]
            else:
                vec_ref[r0 + 1:r0 + 2, half:] = mod_ref[kk:kk + 1, :half]
                vec_ref[r0 + 2:r0 + 3, :] = mod_ref[kk:kk + 1, half:]
        for i, g_ref in enumerate((g1_ref, g2_ref, g3_ref, g4_ref)):
            vec_ref[6 + i:7 + i, :] = g_ref[...]
        for cp in sends:
            cp.wait_send()

    vm = pl.BlockSpec(memory_space=pltpu.VMEM)
    outs = pl.pallas_call(
        body, name="mod_exchange",
        in_specs=[vm] * 7 + [ANY] * len(deps), out_specs=[vm, vm, vm],
        out_shape=[jax.ShapeDtypeStruct((8, D_MODEL), F32), jax.ShapeDtypeStruct((N_CHIPS, ncol), F32),
                   jax.ShapeDtypeStruct((16, D_MODEL), F32)],
        scratch_shapes=[pltpu.VMEM((8, ncol), F32), pltpu.SemaphoreType.DMA((16,)),
                        pltpu.SemaphoreType.DMA((16,))],
        compiler_params=_cp(),
    )(c_row, ada_w, ada_b4, *gains, *deps)
    return outs[0], outs[2]


def _small_exchange(stats, rows, c_all, wsm, msm, vsm):
    ncol = 6 * D_MODEL // N_CHIPS
    ns = len(stats)

    def body(*refs):
        call_ref, w_ref, m_ref, v_ref, gw_ref, loss_ref = refs[ns:ns + 6]
        outs = refs[ns + 6:ns + 30]
        p_ref, g_ref, all_ref, dm_ref, send_sems, recv_sems = refs[ns + 30:]
        x, y, c = _place()
        k = 2 * x + y
        me = 4 * x + 2 * y + c
        for r, (tab, row) in enumerate(rows):
            p_ref[r] = refs[tab][row:row + 1, :]
        all_ref[:, pl.ds(me, 1), :] = p_ref[...]
        sends = []
        for code, peer in _peers(x, y, c):
            cp = pltpu.make_async_remote_copy(
                src_ref=p_ref, dst_ref=all_ref.at[:, pl.ds(me, 1), :],
                send_sem=send_sems.at[code], recv_sem=recv_sems.at[code],
                device_id=peer, device_id_type=MESH)
            cp.start()
            sends.append(cp)
        for code, (px, py, pc) in _peers(x, y, c):
            land = all_ref.at[:, pl.ds(4 * px + 2 * py + pc, 1), :]
            pltpu.make_async_remote_copy(
                src_ref=land, dst_ref=land, send_sem=send_sems.at[code], recv_sem=recv_sems.at[code],
                device_id=(px, py, pc), device_id_type=MESH).wait_recv()
        for cp in sends:
            cp.wait_send()
        tot = [_colsum(all_ref[r]) for r in range(N_PAY)]
        loss_ref[...] = jnp.sum(tot[P_LOSS], axis=1, keepdims=True)
        g_ref[...] = jnp.zeros_like(g_ref)
        for r in range(P_LOSS):
            g_ref[r:r + 1, :] = tot[r]
        g = g_ref[...]
        for kind, tab in enumerate((g,) + _adamw_math(w_ref[...], g, m_ref[...], v_ref[...])):
            for r in range(6):
                outs[6 * kind][:, r * D_MODEL:(r + 1) * D_MODEL] = tab[r:r + 1, :]
            for i in range(5):
                outs[6 * kind + 1 + i][...] = tab[6 + i:7 + i, :]
        half = D_MODEL // 2
        for kk in range(N_CHIPS):
            @pl.when(k == kk)
            def _():
                r0 = 3 * (kk // 2)
                if kk % 2 == 0:
                    dm_ref[:, :D_MODEL] = all_ref[r0]
                    dm_ref[:, D_MODEL:] = all_ref[r0 + 1][:, :half]
                else:
                    dm_ref[:, :half] = all_ref[r0 + 1][:, half:]
                    dm_ref[:, half:] = all_ref[r0 + 2]
        call = call_ref[...]
        act = call * _sigmoid(call)
        gw_ref[...] = lax.dot_general(act, dm_ref[...], (((0,), (0,)), ((), ())),
                                      preferred_element_type=F32, precision=lax.Precision.HIGHEST)

    vm = pl.BlockSpec(memory_space=pltpu.VMEM)
    vectors = [jax.ShapeDtypeStruct((1, 6 * D_MODEL), F32)] + [jax.ShapeDtypeStruct((1, D_MODEL), F32)] * 5
    outs = pl.pallas_call(
        body, name="small_exchange",
        in_specs=[vm] * (ns + 4), out_specs=[vm] * 26,
        out_shape=[jax.ShapeDtypeStruct((D_MODEL, ncol), F32), jax.ShapeDtypeStruct((1, 1), F32)] + vectors * 4,
        scratch_shapes=[pltpu.VMEM((N_PAY, 1, D_MODEL), F32), pltpu.VMEM((16, D_MODEL), F32),
                        pltpu.VMEM((N_PAY, 8, D_MODEL), F32), pltpu.VMEM((8, ncol), F32),
                        pltpu.SemaphoreType.DMA((8,)), pltpu.SemaphoreType.DMA((8,))],
        compiler_params=_cp(),
    )(*stats, c_all, wsm, msm, vsm)
    loss, *small = _pass_on(outs[1:])
    return outs[0], loss, [small[6 * kind:6 + 6 * kind] for kind in range(4)]


def _pass_on(arrays):
    n = len(arrays)

    def body(*refs):
        for i in range(n):
            refs[n + i][...] = refs[i][...]

    return pl.pallas_call(
        body, name="small_outputs", out_shape=[jax.ShapeDtypeStruct(a.shape, a.dtype) for a in arrays],
        compiler_params=_cp(),
    )(*arrays)


def _rope_tables(pos_col, inv_freq, sign, dep=None):
    def body(p_ref, f_ref, s_ref, cos_ref, sin_ref):
        ang = p_ref[...].astype(F32) * f_ref[...]
        cos_ref[...] = jnp.cos(ang)
        sin_ref[...] = jnp.sin(ang) * s_ref[...]

    tr = 512
    shp = jax.ShapeDtypeStruct((SEQ, 128), F32)
    body, in_specs, args = _add_dep(
        body, [pl.BlockSpec((tr, 1), lambda i: (i, 0)), pl.BlockSpec((1, 128), lambda i: (0, 0)),
               pl.BlockSpec((1, 128), lambda i: (0, 0))], [pos_col, inv_freq, sign], dep)
    return pl.pallas_call(
        body, name="rope_tables", grid=(SEQ // tr,),
        in_specs=in_specs,
        out_specs=[pl.BlockSpec((tr, 128), lambda i: (i, 0))] * 2, out_shape=[shp, shp],
        compiler_params=_cp(("parallel",)),
    )(*args)


def _resident(shape):
    nd = len(shape)
    return pl.BlockSpec(shape, lambda *_: (0,) * nd, pipeline_mode=pl.Buffered(1))


def _ln_proj(x, vecs, w_in4):
    tm = min(512, SEQ)
    wc = w_in4.shape[2]

    def body(x_ref, vec_ref, w_ref, h_ref, proj_ref):
        xx = x_ref[...]
        g = _row(vec_ref, V_G1) * (1.0 + _row(vec_ref, V_SC1))
        h = (xx * _rms(xx) * g + _row(vec_ref, V_SH1)).astype(BF)
        h_ref[...] = h
        for j in range(N_CHIPS):
            proj_ref[:, j * wc:(j + 1) * wc] = _dot(h, w_ref[j]).astype(BF)

    return pl.pallas_call(
        body, name="ln_proj", grid=(SEQ // tm,),
        in_specs=[pl.BlockSpec((tm, D_MODEL), lambda i: (i, 0)), _resident((16, D_MODEL)),
                  _resident(w_in4.shape)],
        out_specs=[pl.BlockSpec((tm, D_MODEL), lambda i: (i, 0)), pl.BlockSpec((tm, D_IN), lambda i: (i, 0))],
        out_shape=[jax.ShapeDtypeStruct((SEQ, D_MODEL), BF), jax.ShapeDtypeStruct((SEQ, D_IN), BF)],
        compiler_params=_cp(("parallel",)),
    )(x, vecs, w_in4)


def _lane_first(shape):
    lane = lax.broadcasted_iota(jnp.int32, shape, 1)
    return (lane & 32) == 0


def _rot(v, cos, sin_s):
    partner = jnp.where(_lane_first(v.shape), pltpu.roll(v, 96, 1), pltpu.roll(v, 32, 1))
    return v * cos + partner * sin_s


def _rot_t(dv, cos, sin_s):
    t = dv * sin_s
    partner = jnp.where(_lane_first(dv.shape), pltpu.roll(t, 96, 1), pltpu.roll(t, 32, 1))
    return dv * cos + partner


def _ret_mask(lg):
    t = RET_BLOCK
    ii = lax.broadcasted_iota(jnp.int32, (t, t), 0)
    jj = lax.broadcasted_iota(jnp.int32, (t, t), 1)
    dist = jnp.abs(ii - jj).astype(F32)
    future = (jj >> RET_CHUNK_SHIFT) > (ii >> RET_CHUNK_SHIFT)
    return jnp.where(future, 0.0, jnp.exp(lg * dist))


def _ret_masks(lg, mask_ref, head):
    t = RET_BLOCK
    mask = mask_ref[head]
    ti = lax.broadcasted_iota(jnp.int32, (t, 1), 0).astype(F32)
    from_start = jnp.exp(lg * (ti + 1.0))
    to_end = jnp.exp(lg * (t - 1.0 - ti))
    whole = jnp.exp(jnp.full((1, 128), lg * t, F32))
    return mask, from_start, to_end, whole


def _head_lanes(shape, hh):
    lane = lax.broadcasted_iota(jnp.int32, shape, 1)
    return (lane >> 6) == hh


def _ret_specs():
    t = RET_BLOCK
    return dict(
        q=lambda f: pl.BlockSpec((t, 512), lambda n: (f(n), C_QR // 512)),
        k=lambda f: pl.BlockSpec((t, 512), lambda n: (f(n), C_KR // 512)),
        v=lambda f: pl.BlockSpec((t, D_MODEL), lambda n: (f(n), C_VR // D_MODEL)),
        g=lambda f: pl.BlockSpec((t, D_MODEL), lambda n: (f(n), C_GR // D_MODEL)),
        tab=lambda f: pl.BlockSpec((t, 128), lambda n: (f(n), 0)),
        wide=lambda f: pl.BlockSpec((t, D_MODEL), lambda n: (f(n), 0)),
        state=lambda f: pl.BlockSpec((N_PAIRS, None, 2, 128, 128), lambda n: (0, f(n), 0, 0, 0)),
    )


def _ret_fwd(proj, cos, sin_s, gn_g, log_gamma, dep=None):
    t = RET_BLOCK
    nb = SEQ // t

    def body(lg_ref, q_ref, k_ref, v_ref, g_ref, cos_ref, sin_ref, gn_ref, o_ref, retg_ref, st_ref, state, masks):
        @pl.when(pl.program_id(0) == 0)
        def _():
            state[...] = jnp.zeros_like(state)
            for head in range(2 * N_PAIRS):
                masks[head] = _ret_mask(lg_ref[head])

        cos, sn = cos_ref[...], sin_ref[...]
        for p in range(N_PAIRS):
            q = _rot(q_ref[:, 128 * p:128 * (p + 1)].astype(F32), cos, sn)
            k = _rot(k_ref[:, 128 * p:128 * (p + 1)].astype(F32), cos, sn) * QK_SCALE
            for hh in range(2):
                cols = slice(256 * p + 128 * hh, 256 * p + 128 * (hh + 1))
                lg = lg_ref[2 * p + hh]
                mask, from_start, to_end, whole = _ret_masks(lg, masks, 2 * p + hh)
                lanes = _head_lanes(q.shape, hh)
                qm = jnp.where(lanes, q, 0.0)
                km = jnp.where(lanes, k, 0.0)
                vh = v_ref[:, cols]
                sc = _dot_nt(qm.astype(BF), km.astype(BF)) * mask
                st = state[p, hh]
                st_ref[p, hh] = st
                o = _dot(sc.astype(BF), vh) + _dot((qm * from_start).astype(BF), st.astype(BF))
                state[p, hh] = whole * st + _dot_tn((km * to_end).astype(BF), vh)
                d = o - _rowmean(o)
                nh = d * lax.rsqrt(_rowmean(d * d) + EPS)
                gr = g_ref[:, cols].astype(F32)
                o_ref[:, cols] = o
                retg_ref[:, cols] = (gr * _sigmoid(gr) * nh * gn_ref[:, cols]).astype(BF)

    sp = _ret_specs()
    ident = lambda n: n
    body, in_specs, args = _add_dep(
        body, [pl.BlockSpec(memory_space=pltpu.SMEM), sp["q"](ident), sp["k"](ident), sp["v"](ident),
               sp["g"](ident), sp["tab"](ident), sp["tab"](ident), _resident((1, D_MODEL))],
        [log_gamma, proj, proj, proj, proj, cos, sin_s, gn_g], dep)
    return pl.pallas_call(
        body, name="ret_fwd", grid=(nb,),
        in_specs=in_specs,
        out_specs=[sp["wide"](ident), sp["wide"](ident), sp["state"](ident)],
        out_shape=[jax.ShapeDtypeStruct((SEQ, D_MODEL), F32), jax.ShapeDtypeStruct((SEQ, D_MODEL), BF),
                   jax.ShapeDtypeStruct((N_PAIRS, nb, 2, 128, 128), F32)],
        scratch_shapes=[pltpu.VMEM((N_PAIRS, 2, 128, 128), F32),
                        pltpu.VMEM((2 * N_PAIRS, RET_BLOCK, RET_BLOCK), F32)],
        compiler_params=_cp(("arbitrary",)),
    )(*args)


def _ret_bwd(proj, cos, sin_s, dret, states, log_gamma, dep=None):
    t = RET_BLOCK
    nb = SEQ // t

    def body(lg_ref, q_ref, k_ref, v_ref, cos_ref, sin_ref, do_ref, st_ref, dqkv_ref, dstate, masks):
        @pl.when(pl.program_id(0) == 0)
        def _():
            dstate[...] = jnp.zeros_like(dstate)
            for head in range(2 * N_PAIRS):
                masks[head] = _ret_mask(lg_ref[head])

        cos, sn = cos_ref[...], sin_ref[...]
        for p in range(N_PAIRS):
            q = _rot(q_ref[:, 128 * p:128 * (p + 1)].astype(F32), cos, sn)
            k = _rot(k_ref[:, 128 * p:128 * (p + 1)].astype(F32), cos, sn) * QK_SCALE
            dq_rot = jnp.zeros(q.shape, F32)
            dk_rot = jnp.zeros(q.shape, F32)
            for hh in range(2):
                cols = slice(256 * p + 128 * hh, 256 * p + 128 * (hh + 1))
                lg = lg_ref[2 * p + hh]
                mask, from_start, to_end, whole = _ret_masks(lg, masks, 2 * p + hh)
                lanes = _head_lanes(q.shape, hh)
                qm = jnp.where(lanes, q, 0.0)
                km = jnp.where(lanes, k, 0.0)
                qb, kb = qm.astype(BF), km.astype(BF)
                vh = v_ref[:, cols]
                do = do_ref[:, cols]
                sc = (_dot_nt(qb, kb) * mask).astype(BF)
                st = st_ref[p, hh].astype(BF)
                dst = dstate[p, hh]
                dstb = dst.astype(BF)
                k_end = (km * to_end).astype(BF)
                q_start = (qm * from_start).astype(BF)
                dqkv_ref[:, C_VR + 256 * p + 128 * hh:C_VR + 256 * p + 128 * (hh + 1)] = (
                    _dot_tn(sc, do) + _dot(k_end, dstb)).astype(BF)
                dsc = (_dot_nt(do, vh) * mask).astype(BF)
                dq_h = _dot(dsc, kb) + _dot_nt(do, st) * from_start
                dq_rot = dq_rot + jnp.where(lanes, dq_h, 0.0)
                dk_rot = dk_rot + _dot_tn(dsc, qb) + _dot_nt(vh, dstb) * to_end
                dstate[p, hh] = whole * dst + _dot_tn(q_start, do)
            dqkv_ref[:, C_QR + 128 * p:C_QR + 128 * (p + 1)] = _rot_t(dq_rot, cos, sn).astype(BF)
            dqkv_ref[:, C_KR + 128 * p:C_KR + 128 * (p + 1)] = _rot_t(dk_rot * QK_SCALE, cos, sn).astype(BF)

    sp = _ret_specs()
    rev = lambda n: nb - 1 - n
    body, in_specs, args = _add_dep(
        body, [pl.BlockSpec(memory_space=pltpu.SMEM), sp["q"](rev), sp["k"](rev), sp["v"](rev),
               sp["tab"](rev), sp["tab"](rev), sp["wide"](rev), sp["state"](rev)],
        [log_gamma, proj, proj, proj, cos, sin_s, dret, states], dep)
    return pl.pallas_call(
        body, name="ret_bwd", grid=(nb,),
        in_specs=in_specs,
        out_specs=pl.BlockSpec((t, C_GR), lambda n: (rev(n), 0)),
        out_shape=jax.ShapeDtypeStruct((SEQ, C_GR), BF),
        scratch_shapes=[pltpu.VMEM((N_PAIRS, 2, 128, 128), F32),
                        pltpu.VMEM((2 * N_PAIRS, RET_BLOCK, RET_BLOCK), F32)],
        compiler_params=_cp(("arbitrary",)),
    )(*args)


def _stack_heads(v):
    return jnp.concatenate([jnp.where(_head_lanes(v.shape, hh), v, jnp.zeros_like(v)) for hh in range(2)], axis=0)


def _unstack_heads(v):
    t = v.shape[0] // 2
    return jnp.where(_head_lanes((t, v.shape[1]), 0), v[:t], v[t:])


def _sb_masks(t, heads):
    rr = lax.broadcasted_iota(jnp.int32, (t, t), 0)
    cc = lax.broadcasted_iota(jnp.int32, (t, t), 1)
    r2 = lax.broadcasted_iota(jnp.int32, (heads * t, t), 0) & (t - 1)
    c2 = lax.broadcasted_iota(jnp.int32, (heads * t, t), 1)
    return rr, cc, c2 < r2


def _split_dot2(v, tri):
    return _dot(v.astype(BF), tri)


def _log2_sigmoids(z2):
    minus_abs = lax.bitcast_convert_type(
        lax.bitcast_convert_type(z2, jnp.uint32) | jnp.uint32(0x80000000), F32)
    ls = jnp.minimum(z2, 0.0) - jnp.log2(1.0 + jnp.exp2(minus_abs))
    return ls, ls - z2


def _sb_fwd(proj):
    t, g = SB_BLOCK, SB_GROUP
    nq = SEQ // t
    rows = 2 * g * t

    def body(q_ref, k_ref, v_ref, o_ref, tot_ref, kt_ref):
        i = pl.program_id(1)

        @pl.when(i == 0)
        def _():
            for p in range(g):
                for jj in range(nq):
                    kt_ref[p, jj] = k_ref[jj * t:(jj + 1) * t, 128 * p:128 * (p + 1)].T

        q2 = [_stack_heads((q_ref[:, 128 * p:128 * (p + 1)].astype(F32) * QK_SCALE).astype(BF)) for p in range(g)]
        rr, cc, valid = _sb_masks(t, 2 * g)
        later = (rr > cc).astype(BF)

        def tile(j, carry, diagonal):
            acc, run = carry
            z = jnp.concatenate([_dot(q2[p], kt_ref[p, j]) for p in range(g)], axis=0) * LOG2E
            ls, lm = _log2_sigmoids(z)
            if diagonal:
                lm = jnp.where(valid, lm, 0.0)
            after = _split_dot2(lm, later)
            a = jnp.exp2(ls + after + run)
            if diagonal:
                a = jnp.where(valid, a, 0.0)
            ab = a.astype(BF)
            keys = pl.ds(pl.multiple_of(j * t, t), t)
            av = jnp.concatenate([_dot(ab[2 * t * p:2 * t * (p + 1)], v_ref[keys, 128 * p:128 * (p + 1)])
                                  for p in range(g)], axis=0)
            return acc + av, run + after[:, 0:1] + lm[:, 0:1]

        carry = tile(i, (jnp.zeros((rows, 128), F32), jnp.zeros((rows, 1), F32)), True)
        acc, run = lax.fori_loop(0, i, lambda s, cr: tile(i - 1 - s, cr, False), carry)
        run = jnp.broadcast_to(run, (rows, 128))
        for p in range(g):
            o_ref[:, 128 * p:128 * (p + 1)] = _unstack_heads(acc[2 * t * p:2 * t * (p + 1)]).astype(BF)
            tot_ref[:, 128 * p:128 * (p + 1)] = _unstack_heads(run[2 * t * p:2 * t * (p + 1)])

    w = 128 * g
    return pl.pallas_call(
        body, name="sb_fwd", grid=(N_PAIRS // g, nq),
        in_specs=[pl.BlockSpec((t, w), lambda p, i: (i, C_QS // w + p)),
                  pl.BlockSpec((SEQ, w), lambda p, i: (0, C_KS // w + p)),
                  pl.BlockSpec((SEQ, w), lambda p, i: (0, C_VS // w + p))],
        out_specs=[pl.BlockSpec((t, w), lambda p, i: (i, p))] * 2,
        out_shape=[jax.ShapeDtypeStruct((SEQ, 512), BF), jax.ShapeDtypeStruct((SEQ, 512), F32)],
        scratch_shapes=[pltpu.VMEM((g, nq, 128, t), BF)],
        compiler_params=_cp(("parallel", "arbitrary")),
    )(proj, proj, proj)


def _sb_bwd(proj, dsb, tot, dep=None):
    t, g = SB_BLOCK, SB_GROUP
    nq = SEQ // t
    rows = 2 * g * t

    def body(q_ref, k_ref, v_ref, do_ref, tot_ref, dq_ref, dk_ref, dv_ref, kt_ref, vt_ref, dkt_acc, dvt_acc):
        i = pl.program_id(1)

        @pl.when(i == 0)
        def _():
            dkt_acc[...] = jnp.zeros_like(dkt_acc)
            dvt_acc[...] = jnp.zeros_like(dvt_acc)
            for p in range(g):
                for jj in range(nq):
                    kt_ref[p, jj] = k_ref[jj * t:(jj + 1) * t, 128 * p:128 * (p + 1)].T
                    vt_ref[p, jj] = v_ref[jj * t:(jj + 1) * t, 128 * p:128 * (p + 1)].T

        q2 = [_stack_heads((q_ref[:, 128 * p:128 * (p + 1)].astype(F32) * QK_SCALE).astype(BF)) for p in range(g)]
        do2 = [_stack_heads(do_ref[:, 128 * p:128 * (p + 1)]) for p in range(g)]
        q2t = [v.T for v in q2]
        do2t = [v.T for v in do2]
        tots = tot_ref[...]
        total = jnp.concatenate([tots[:, 64 * h:64 * h + 1] for h in range(2 * g)], axis=0)
        rr, cc, valid = _sb_masks(t, 2 * g)
        upto = (rr <= cc).astype(BF)
        before = (rr < cc).astype(BF)

        def part(v, p):
            return v[2 * t * p:2 * t * (p + 1)]

        def tile(j, carry, diagonal):
            dq, run_l, run_g = carry
            z = jnp.concatenate([_dot(q2[p], kt_ref[p, j]) for p in range(g)], axis=0) * LOG2E
            ls, lm = _log2_sigmoids(z)
            if diagonal:
                lm = jnp.where(valid, lm, 0.0)
            incl = _split_dot2(lm, upto)
            a = jnp.exp2(ls + (total - (incl + run_l)))
            if diagonal:
                a = jnp.where(valid, a, 0.0)
            gg = a * jnp.concatenate([_dot(do2[p], vt_ref[p, j]) for p in range(g)], axis=0)
            excl = _split_dot2(gg, before)
            dz = gg * jnp.exp2(lm) - (excl + run_g) * jnp.exp2(ls)
            if diagonal:
                dz = jnp.where(valid, dz, 0.0)
            dzb = dz.astype(BF)
            ab = a.astype(BF)
            keys = pl.ds(pl.multiple_of(j * t, t), t)
            for p in range(g):
                dkt_acc[p, j] += _dot(q2t[p], part(dzb, p))
                dvt_acc[p, j] += _dot(do2t[p], part(ab, p))
            dq_t = jnp.concatenate([_dot(part(dzb, p), k_ref[keys, 128 * p:128 * (p + 1)]) for p in range(g)], axis=0)
            return (dq + dq_t, run_l + incl[:, t - 1:t], run_g + excl[:, t - 1:t] + gg[:, t - 1:t])

        zero = jnp.zeros((rows, 1), F32)
        carry = lax.fori_loop(0, i, lambda j, cr: tile(j, cr, False), (jnp.zeros((rows, 128), F32), zero, zero))
        dq = tile(i, carry, True)[0]
        for p in range(g):
            dq_ref[:, 128 * p:128 * (p + 1)] = (_unstack_heads(part(dq, p)) * QK_SCALE).astype(BF)

        @pl.when(i == nq - 1)
        def _():
            for p in range(g):
                for jj in range(nq):
                    dk_ref[jj * t:(jj + 1) * t, 128 * p:128 * (p + 1)] = dkt_acc[p, jj].T.astype(BF)
                    dv_ref[jj * t:(jj + 1) * t, 128 * p:128 * (p + 1)] = dvt_acc[p, jj].T.astype(BF)

    w = 128 * g
    tile_spec = pl.BlockSpec((t, w), lambda p, i: (i, p))
    col_spec = pl.BlockSpec((SEQ, w), lambda p, i: (0, p))
    shp = jax.ShapeDtypeStruct((SEQ, 512), BF)
    body, in_specs, args = _add_dep(
        body, [pl.BlockSpec((t, w), lambda p, i: (i, C_QS // w + p)),
               pl.BlockSpec((SEQ, w), lambda p, i: (0, C_KS // w + p)),
               pl.BlockSpec((SEQ, w), lambda p, i: (0, C_VS // w + p)),
               tile_spec, tile_spec],
        [proj, proj, proj, dsb, tot], dep)
    return pl.pallas_call(
        body, name="sb_bwd", grid=(N_PAIRS // g, nq),
        in_specs=in_specs,
        out_specs=[tile_spec, col_spec, col_spec],
        out_shape=[shp, shp, shp],
        scratch_shapes=[pltpu.VMEM((g, nq, 128, t), BF), pltpu.VMEM((g, nq, 128, t), BF),
                        pltpu.VMEM((g, nq, 128, t), F32), pltpu.VMEM((g, nq, 128, t), F32)],
        compiler_params=_cp(("parallel", "arbitrary")),
    )(*args)


def _mix_ffn(retg, sb, proj, x, target, vecs, w_ret, w_sb4, w_out, w_ff14, w_ff24):
    tm, half = 256, 512

    def body(r_ref, s_ref, ar0, ar1, as0, as1, x_ref, t_ref, vec_ref, wr_ref, ws_ref, wo_ref, w1_ref, w2_ref,
             mix_ref, rb_ref, sbp_ref, h2_ref, a_ref, df_ref, du_ref, dh1_ref, dy_ref, sta_ref, stb_ref, u_scr):
        @pl.when(pl.program_id(0) == 0)
        def _():
            sta_ref[...] = jnp.zeros_like(sta_ref)
            stb_ref[...] = jnp.zeros_like(stb_ref)

        rb = _dot(r_ref[...], wr_ref[...])
        sbv = s_ref[...]
        sbp = jnp.concatenate([_dot(sbv, ws_ref[k]) for k in range(N_CHIPS)], axis=1)
        gate_r = _sigmoid(jnp.concatenate([ar0[...], ar1[...]], axis=1).astype(F32))
        gate_s = _sigmoid(jnp.concatenate([as0[...], as1[...]], axis=1).astype(F32))
        mixed = (gate_r * rb + gate_s * sbp).astype(BF)
        mix_ref[...] = mixed
        rb_ref[...] = rb.astype(BF)
        sbp_ref[...] = sbp.astype(BF)
        yy = _dot(mixed, wo_ref[...])
        gt1, g2 = _row(vec_ref, V_GT1), _row(vec_ref, V_G2)
        g3, sc2 = _row(vec_ref, V_G3), _row(vec_ref, V_SC2)
        r2 = _rms(yy)
        yn = yy * r2
        h1 = x_ref[...] + gt1 * yn * g2
        r3 = _rms(h1)
        hn3 = h1 * r3
        hb = (hn3 * (g3 * (1.0 + sc2)) + _row(vec_ref, V_SH2)).astype(BF)
        h2_ref[...] = hb

        f = jnp.zeros((tm, D_MODEL), F32)
        for k in range(N_CHIPS):
            cols = slice(k * D_MODEL, (k + 1) * D_MODEL)
            u = _dot(hb, w1_ref[k])
            r = jnp.maximum(u, 0.0)
            act = (r * r).astype(BF)
            u_scr[:, cols] = u.astype(BF)
            a_ref[:, cols] = act
            f = f + _dot(act, w2_ref[k])
        r4 = _rms(f)
        fn = f * r4
        gt2, g4 = _row(vec_ref, V_GT2), _row(vec_ref, V_G4)
        diff = h1 + gt2 * fn * g4 - t_ref[...]
        dout = diff * (1.0 / D_MODEL)
        dfn = dout * gt2 * g4
        dfb = (r4 * (dfn - fn * _rowmean(dfn * fn))).astype(BF)
        df_ref[...] = dfb
        sta_ref[0:1, :] += _colsum(dout * fn * g4)
        sta_ref[1:2, :] += _colsum(dout * gt2 * fn)
        sta_ref[2:3, :] += _colsum(diff * diff) * (0.5 / D_MODEL)

        dh2 = jnp.zeros((tm, D_MODEL), F32)
        for k in range(N_CHIPS):
            cols = slice(k * D_MODEL, (k + 1) * D_MODEL)
            da = _dot_nt(dfb, w2_ref[k])
            du = (da * (2.0 * jnp.maximum(u_scr[:, cols].astype(F32), 0.0))).astype(BF)
            du_ref[:, cols] = du
            dh2 = dh2 + _dot_nt(du, w1_ref[k])
        dhn3 = dh2 * g3 * (1.0 + sc2)
        dh1 = dout + r3 * (dhn3 - hn3 * _rowmean(dhn3 * hn3))
        dyn = dh1 * gt1 * g2
        dh1_ref[...] = dh1
        dy_ref[...] = (r2 * (dyn - yn * _rowmean(dyn * yn))).astype(BF)
        stb_ref[0:1, :] += _colsum(dh2)
        stb_ref[1:2, :] += _colsum(dh2 * hn3 * g3)
        stb_ref[2:3, :] += _colsum(dh2 * hn3 * (1.0 + sc2))
        stb_ref[3:4, :] += _colsum(dh1 * yn * g2)
        stb_ref[4:5, :] += _colsum(dh1 * gt1 * yn)

    row = pl.BlockSpec((tm, D_MODEL), lambda i: (i, 0))
    wide = pl.BlockSpec((tm, D_FF), lambda i: (i, 0))
    sums = pl.BlockSpec((8, D_MODEL), lambda i: (0, 0))
    gate = lambda c0: pl.BlockSpec((tm, half), lambda i: (i, c0 // half))
    bf = jax.ShapeDtypeStruct((SEQ, D_MODEL), BF)
    ff = jax.ShapeDtypeStruct((SEQ, D_FF), BF)
    st = jax.ShapeDtypeStruct((8, D_MODEL), F32)
    return pl.pallas_call(
        body, name="mix_ffn", grid=(SEQ // tm,),
        in_specs=[row, pl.BlockSpec((tm, 512), lambda i: (i, 0)), gate(C_AR), gate(C_AR + half), gate(C_AS),
                  gate(C_AS + half), row, row, _resident((16, D_MODEL)), _resident((D_MODEL, D_MODEL)),
                  _resident(w_sb4.shape), _resident((D_MODEL, D_MODEL)), _resident(w_ff14.shape),
                  _resident(w_ff24.shape)],
        out_specs=[row, row, row, row, wide, row, wide, row, row, sums, sums],
        out_shape=[bf, bf, bf, bf, ff, bf, ff, jax.ShapeDtypeStruct((SEQ, D_MODEL), F32), bf, st, st],
        scratch_shapes=[pltpu.VMEM((tm, D_FF), BF)],
        compiler_params=_cp(("arbitrary",)),
    )(retg, sb, proj, proj, proj, proj, x, target, vecs, w_ret, w_sb4, w_out, w_ff14, w_ff24)


def _mix_ret_bwd(dy, proj, rb, sbp, o_raw, gn_g, w_out, w_sb4, w_ret):
    tm, half = min(512, SEQ), 512

    def body(dy_ref, ar0, ar1, as0, as1, rb_ref, sbp_ref, g_ref, o_ref, gn_ref, wo_ref, ws_ref, wr_ref,
             drb_ref, dsbp_ref, da_ref, dsb_ref, dret_ref, dgr_ref, st_ref):
        @pl.when(pl.program_id(0) == 0)
        def _():
            st_ref[...] = jnp.zeros_like(st_ref)

        dm_all = _dot_nt(dy_ref[...], wo_ref[...])
        dsb = jnp.zeros((tm, 512), F32)
        drbs = []
        for hf, (ar_ref, as_ref) in enumerate(((ar0, as0), (ar1, as1))):
            cols = slice(half * hf, half * (hf + 1))
            dm = dm_all[:, cols]
            sr = _sigmoid(ar_ref[...].astype(F32))
            ss = _sigmoid(as_ref[...].astype(F32))
            dsbp = (dm * ss).astype(BF)
            drbs.append((dm * sr).astype(BF))
            dsbp_ref[:, cols] = dsbp
            da_ref[:, cols] = (dm * rb_ref[:, cols].astype(F32) * sr * (1.0 - sr)).astype(BF)
            da_ref[:, D_MODEL + half * hf:D_MODEL + half * (hf + 1)] = (
                dm * sbp_ref[:, cols].astype(F32) * ss * (1.0 - ss)).astype(BF)
            dsb = dsb + _dot_nt(dsbp[:, :256], ws_ref[2 * hf]) + _dot_nt(dsbp[:, 256:], ws_ref[2 * hf + 1])
        dsb_ref[...] = dsb.astype(BF)
        drb = jnp.concatenate(drbs, axis=1)
        drb_ref[...] = drb
        dretg = _dot_nt(drb, wr_ref[...])
        for gi in range(D_MODEL // 128):
            cols = slice(128 * gi, 128 * (gi + 1))
            o = o_ref[:, cols]
            d = o - _rowmean(o)
            rstd = lax.rsqrt(_rowmean(d * d) + EPS)
            nh = d * rstd
            gain = gn_ref[:, cols]
            gr = g_ref[:, cols].astype(F32)
            sg = _sigmoid(gr)
            dg = dretg[:, cols]
            dgn = dg * gr * sg
            dnh = dgn * gain
            dgr_ref[:, cols] = (dg * nh * gain * sg * (1.0 + gr * (1.0 - sg))).astype(BF)
            dret_ref[:, cols] = (rstd * (dnh - _rowmean(dnh) - nh * _rowmean(dnh * nh))).astype(BF)
            st_ref[0:1, cols] += _colsum(dgn * nh)

    row = pl.BlockSpec((tm, D_MODEL), lambda i: (i, 0))
    gate = lambda c0: pl.BlockSpec((tm, half), lambda i: (i, c0 // half))
    shp = jax.ShapeDtypeStruct((SEQ, D_MODEL), BF)
    return pl.pallas_call(
        body, name="mix_ret_bwd", grid=(SEQ // tm,),
        in_specs=[row, gate(C_AR), gate(C_AR + half), gate(C_AS), gate(C_AS + half), row, row,
                  pl.BlockSpec((tm, D_MODEL), lambda i: (i, C_GR // D_MODEL)), row, _resident((1, D_MODEL)),
                  _resident((D_MODEL, D_MODEL)), _resident(w_sb4.shape), _resident((D_MODEL, D_MODEL))],
        out_specs=[row, row, pl.BlockSpec((tm, 2 * D_MODEL), lambda i: (i, 0)), pl.BlockSpec((tm, 512), lambda i: (i, 0)),
                   row, row, pl.BlockSpec((8, D_MODEL), lambda i: (0, 0))],
        out_shape=[shp, shp, jax.ShapeDtypeStruct((SEQ, 2 * D_MODEL), BF), jax.ShapeDtypeStruct((SEQ, 512), BF),
                   shp, shp, jax.ShapeDtypeStruct((8, D_MODEL), F32)],
        compiler_params=_cp(("arbitrary",)),
    )(dy, proj, proj, proj, proj, rb, sbp, proj, o_raw, gn_g, w_out, w_sb4, w_ret)


def _dproj_segments(widths):
    wc = D_IN // N_CHIPS
    segs, start = [], 0
    for pi, width in enumerate(widths):
        lo = start
        while lo < start + width:
            j = lo // wc
            hi = min(start + width, (j + 1) * wc)
            segs.append((j, lo - j * wc, pi, lo - start, hi - lo))
            lo = hi
        start += width
    assert start == D_IN
    return segs


def _in_proj_bwd(pieces, x, dh1, vecs, w_in4, updates, dep=None):
    tm = 256
    steps = SEQ // tm
    n, nu = len(pieces), len(updates)
    segs = _dproj_segments([p.shape[1] for p in pieces])

    def body(*refs):
        x_ref, dh1_ref, vec_ref, w_ref = refs[n:n + 4]
        upd_in = refs[n + 4:n + 4 + 4 * nu]
        dx_ref, st_ref = refs[n + 4 + 4 * nu:n + 6 + 4 * nu]
        upd_out = refs[n + 6 + 4 * nu:]
        for u in range(nu):
            w_u, m_u, v_u, g_u = upd_in[4 * u:4 * u + 4]
            go_u, d_u, mo_u, vo_u = upd_out[4 * u:4 * u + 4]
            gg = g_u[...]
            go_u[...] = gg
            d_u[...], mo_u[...], vo_u[...] = _adamw_math(w_u[...], gg, m_u[...], v_u[...])

        @pl.when(pl.program_id(0) == 0)
        def _():
            st_ref[...] = jnp.zeros_like(st_ref)

        dh = jnp.zeros((tm, D_MODEL), F32)
        for j, so, pi, po, width in segs:
            dh = dh + _dot_nt(refs[pi][:, po:po + width], w_ref[j, :, so:so + width])
        xx = x_ref[...]
        r1 = _rms(xx)
        xn = xx * r1
        g1, sc1 = _row(vec_ref, V_G1), _row(vec_ref, V_SC1)
        dxn = dh * g1 * (1.0 + sc1)
        dx_ref[...] = dh1_ref[...] + r1 * (dxn - xn * _rowmean(dxn * xn))
        st_ref[0:1, :] += _colsum(dh)
        st_ref[1:2, :] += _colsum(dh * xn * g1)
        st_ref[2:3, :] += _colsum(dh * xn * (1.0 + sc1))

    row = pl.BlockSpec((tm, D_MODEL), lambda i: (i, 0))
    upd_specs, upd_shapes, upd_args = [], [], []
    for arrays in updates:
        rows, cols = arrays[0].shape
        upd_specs += [pl.BlockSpec((rows // steps, cols), lambda i: (i, 0))] * 4
        upd_shapes += [jax.ShapeDtypeStruct((rows, cols), F32)] * 4
        upd_args += list(arrays)
    body, in_specs, args = _add_dep(
        body, [pl.BlockSpec((tm, p.shape[1]), lambda i: (i, 0)) for p in pieces] + [
            row, row, _resident((16, D_MODEL)), _resident(w_in4.shape)] + upd_specs,
        list(pieces) + [x, dh1, vecs, w_in4] + upd_args, dep)
    outs = pl.pallas_call(
        body, name="in_proj_bwd", grid=(steps,),
        in_specs=in_specs,
        out_specs=[row, pl.BlockSpec((8, D_MODEL), lambda i: (0, 0))] + upd_specs,
        out_shape=[jax.ShapeDtypeStruct((SEQ, D_MODEL), F32), jax.ShapeDtypeStruct((8, D_MODEL), F32)] + upd_shapes,
        compiler_params=_cp(("arbitrary",)),
    )(*args)
    return outs[0], outs[1], [tuple(outs[2 + 4 * u:6 + 4 * u]) for u in range(nu)]


def _grad_w_in(h, pieces, dep=None):
    ta = 512
    n = len(pieces)
    segs = _dproj_segments([p.shape[1] for p in pieces])

    def body(*refs):
        h_ref, o_ref = refs[n], refs[n + 1]
        hh = h_ref[...]
        for j, so, pi, po, width in segs:
            o_ref[j, :, so:so + width] = _dot_tn(hh, refs[pi][:, po:po + width]).astype(BF)

    body, in_specs, args = _add_dep(
        body, [_resident(p.shape) for p in pieces] + [pl.BlockSpec((SEQ, ta), lambda i: (0, i))],
        list(pieces) + [h], dep)
    return pl.pallas_call(
        body, name="grad_w_in", grid=(D_MODEL // ta,),
        in_specs=in_specs,
        out_specs=pl.BlockSpec((N_CHIPS, ta, D_IN // N_CHIPS), lambda i: (0, i, 0)),
        out_shape=jax.ShapeDtypeStruct((N_CHIPS, D_MODEL, D_IN // N_CHIPS), BF),
        compiler_params=_cp(("parallel",)),
    )(*args)


def _weight_grads(jobs, name):
    def grad(in_refs, out_refs):
        out_refs[0][...] = _dot_tn(in_refs[0][...], in_refs[1][...]).astype(BF)

    sets = []
    for a, b, col_sharded in jobs:
        ka, nb_ = a.shape[1], b.shape[1]
        if col_sharded:
            ta, tb = ka, nb_ // N_CHIPS
            ins = [(a, (SEQ, ta), lambda l: (0, 0)), (b, (SEQ, tb), lambda l: (0, l))]
        else:
            ta, tb = ka // N_CHIPS, nb_
            ins = [(a, (SEQ, ta), lambda l: (0, l)), (b, (SEQ, tb), lambda l: (0, 0))]
        sets.append((N_CHIPS, ins, [((N_CHIPS, ta, tb), BF, (None, ta, tb), lambda l: (l, 0, 0))], grad))
    return [o[0] for o in _run_sets(name, [], sets)]


def _rope_constants():
    freq = np.float32(ROPE_BASE) ** (-np.arange(0, 64, 2, dtype=np.float32) / np.float32(64))
    inv = np.tile(freq.astype(np.float32), 4).reshape(1, 128)
    sign = np.tile(np.concatenate([-np.ones(32, np.float32), np.ones(32, np.float32)]), 2).reshape(1, 128)
    return jnp.asarray(inv), jnp.asarray(sign)


def _log_gamma():
    return jnp.asarray(np.log1p(-(2.0 ** (-5.0 - np.arange(8, dtype=np.float64)))).astype(np.float32))


def _halves(g):
    return g.reshape(N_CHIPS, 2, g.shape[1] // 2, g.shape[2])


def kernel(x, c, positions, ada_w, ada_b, pre_mix_g, post_mix_g, pre_ffn_g, post_ffn_g, w_in, ret_gn_g, w_ret_branch, w_sb_branch, w_out, w_ff1, w_ff2, loss_target, m_ada_w, m_ada_b, m_pre_mix_g, m_post_mix_g, m_pre_ffn_g, m_post_ffn_g, m_w_in, m_ret_gn_g, m_w_ret_branch, m_w_sb_branch, m_w_out, m_w_ff1, m_w_ff2, v_ada_w, v_ada_b, v_pre_mix_g, v_post_mix_g, v_pre_ffn_g, v_post_ffn_g, v_w_in, v_ret_gn_g, v_w_ret_branch, v_w_sb_branch, v_w_out, v_w_ff1, v_w_ff2):
    names = ["w_in", "w_ret", "w_sb", "w_out", "w_ff1", "w_ff2"]
    big = dict(zip(names, [w_in, w_ret_branch, w_sb_branch, w_out, w_ff1, w_ff2]))
    big_m = dict(zip(names, [m_w_in, m_w_ret_branch, m_w_sb_branch, m_w_out, m_w_ff1, m_w_ff2]))
    big_v = dict(zip(names, [v_w_in, v_w_ret_branch, v_w_sb_branch, v_w_out, v_w_ff1, v_w_ff2]))
    rest = names[1:]
    cidx = lax.axis_index("c").astype(jnp.int32).reshape(1)
    kidx = (2 * lax.axis_index("x") + lax.axis_index("y")).astype(jnp.int32).reshape(1)
    x0, target = x[0], loss_target[0]

    buf_in, sem_in, tok_in = _gather_start("gather_in_start", [_cast_bf16(w_in[0], kidx, c, "cast_w_in")])
    rest_bufs = [_cast_bf16(big[nm][0], kidx, tok_in, "cast_" + nm) for nm in rest]
    inv_freq, sign = _rope_constants()
    lg = _log_gamma()
    cos, sin_s = _rope_tables(positions.reshape(SEQ, 1), inv_freq, sign, dep=tok_in)

    def table(b6, g5):
        return jnp.concatenate([b6.reshape(6, D_MODEL)] + g5 + [jnp.zeros((5, D_MODEL), F32)], axis=0)

    wsm = table(ada_b, [pre_mix_g, post_mix_g, pre_ffn_g, post_ffn_g, ret_gn_g])
    msm = table(m_ada_b, [m_pre_mix_g, m_post_mix_g, m_pre_ffn_g, m_post_ffn_g, m_ret_gn_g])
    vsm = table(v_ada_b, [v_pre_mix_g, v_post_mix_g, v_pre_ffn_g, v_post_ffn_g, v_ret_gn_g])
    c_all, vecs = _mod_exchange(c, ada_w[0], ada_b.reshape(N_CHIPS, -1),
                                [pre_mix_g, post_mix_g, pre_ffn_g, post_ffn_g], rest_bufs + [cos, wsm, msm, vsm])
    buf_in, sem_in, tok_in = _gather_pass("gather_in_pass", buf_in, sem_in, vecs)
    buf_rest, sem_rest, tok_rest = _gather_start("gather_rest_start", rest_bufs, after=tok_in)
    (w_in4,) = _gather_finish("gather_in_finish", buf_in, sem_in, tok_rest)

    h, proj = _ln_proj(x0, vecs, w_in4)
    sb, tot = _sb_fwd(proj)
    buf_rest, sem_rest, tok_rest = _gather_pass("gather_rest_pass", buf_rest, sem_rest, sb)
    o_raw, retg, states = _ret_fwd(proj, cos, sin_s, ret_gn_g, lg, dep=tok_rest)
    w_ret4, w_sb4, w_out4, w_ff14, w_ff24 = _gather_finish("gather_rest_finish", buf_rest, sem_rest, retg)
    w_ret = w_ret4.reshape(D_MODEL, D_MODEL)
    w_out2 = w_out4.reshape(D_MODEL, D_MODEL)
    mixed, rb, sbp, h2, act, df, du, dh1, dy, st_a, st_b = _mix_ffn(
        retg, sb, proj, x0, target, vecs, w_ret, w_sb4, w_out2, w_ff14, w_ff24)
    grads = dict(zip(("w_ff2", "w_ff1"), _weight_grads([(act, df, False), (h2, du, True)], "grad_w_ff")))
    drb, dsbp, da, dsb, dret, dg_r, st_c = _mix_ret_bwd(dy, proj, rb, sbp, o_raw, ret_gn_g, w_out2, w_sb4, w_ret)
    grads.update(zip(("w_out", "w_ret", "w_sb"), _weight_grads(
        [(mixed, dy, False), (retg, drb, False), (sb, dsbp, True)], "grad_w_branches")))

    bufs, sems, tok = _pair_send_start("rs_rest_pair_send", [_halves(grads[nm]) for nm in rest])
    dqkv_r = _ret_bwd(proj, cos, sin_s, dret, states, lg, dep=tok)
    mine, theirs = _pair_send_wait("rs_rest_pair_recv", bufs, sems, dqkv_r)
    pair_sums = _pair_add_all(mine, theirs, cidx, "pair_add_rest")
    bufs, sems, tok = _chip_send_start("rs_rest_chip_send", pair_sums)
    dq_s, dk_s, dv_s = _sb_bwd(proj, dsb, tot, dep=tok)
    dproj = [dqkv_r, dg_r, dq_s, dk_s, dv_s, da]
    g_in = _grad_w_in(h, dproj)

    bufs_in, sems_in, tok_in = _pair_send_start("rs_in_pair_send", [_halves(g_in)])
    own, parts = _chip_send_wait("rs_rest_chip_recv", bufs, sems, tok_in)
    sums = _chip_add_all(own, parts, kidx, cidx, "chip_add_rest")
    bufs, sems, tok = _pair_swap_start("rs_rest_pair_swap", sums)
    mine, theirs = _pair_send_wait("rs_in_pair_recv", bufs_in, sems_in, tok)
    bufs_in, sems_in, tok_in = _chip_send_start(
        "rs_in_chip_send", [_pair_add(mine[0], theirs[0], cidx, "pair_add_w_in")])
    full_rest = dict(zip(rest, _pair_swap_wait("rs_rest_pair_swapped", bufs, sems, tok_in)))
    small_w = ("w_out", "w_sb", "w_ret")
    out = dict(zip(small_w, _adamw_all(
        [(big[nm][0], big_m[nm][0], big_v[nm][0], full_rest[nm].reshape(big[nm].shape[1:])) for nm in small_w],
        "adamw_small", tok_in)))
    bufs, sems, tok = bufs_in, sems_in, out["w_ret"][1]
    riding = ("w_ff2", "w_ff1")
    dx, st_d, updated = _in_proj_bwd(
        dproj, x0, dh1, vecs, w_in4,
        [(big[nm][0], big_m[nm][0], big_v[nm][0], full_rest[nm].reshape(big[nm].shape[1:])) for nm in riding],
        dep=tok)
    out.update(zip(riding, updated))

    a_, b_, c_, d_ = range(4)
    payload_rows = [(d_, 0), (d_, 1), (b_, 3), (b_, 0), (b_, 1), (a_, 0),
                    (d_, 2), (b_, 4), (b_, 2), (a_, 1), (c_, 0), (a_, 2)]
    g_ada, loss, small = _small_exchange([st_a, st_b, st_c, st_d], payload_rows, c_all, wsm, msm, vsm)
    own, parts = _chip_send_wait("rs_in_chip_recv", bufs, sems, g_ada)
    bufs, sems, tok = _pair_swap_start(
        "rs_in_pair_swap", [_chip_add(own[0], parts[0], kidx, cidx, "chip_add_w_in")])
    ada_out = _adamw(ada_w[0], m_ada_w[0], v_ada_w[0], g_ada, "adamw_ada_w", dep=tok)
    (full_in,) = _pair_swap_wait("rs_in_pair_swapped", bufs, sems, ada_out[1])
    out["w_in"] = _adamw(w_in[0], m_w_in[0], v_w_in[0], full_in.reshape(w_in.shape[1:]), "adamw_w_in")

    def ordered(which):
        sm = small[which]
        bg = [out[nm][which][None] for nm in names]
        return [ada_out[which][None], sm[0], sm[1], sm[2], sm[3], sm[4], bg[0], sm[5]] + bg[1:]

    return (loss.reshape(()), dx[None], *ordered(0), *ordered(1), *ordered(2), *ordered(3))
```
